```python
import jax
import jax.numpy as jnp
from jax import lax
import numpy as np

D_MODEL = 1024
BATCH = 32
SEQ = 256
DEPTH = 2
DEC_BATCH = 8
DEC_SEQ = 4096
PAST_LEN = 256

GRID_W = 64
HEAD_DIM = 64
NA_HEADS = 6
NA_KH_MAX = 8
NA_KW = 16
NA_QB = 16
NA_KB = 32
RW_HEADS = 4
RW_DECAY_LORA = 64
RW_A_LORA = 64
RW_GATE_LORA = 128
SWA_HEADS = 6
SWA_KV_HEADS = 2
SWA_WINDOW = 128
SWA_BLOCK = 128
N_EXPERTS = 16
EC_CAPACITY = 2
D_FF = 1024
ROPE_THETA = 10000.0
NORM_EPS = 1e-6
GN_EPS = 64e-5
NEG_INF = -1e30

NA_W = NA_HEADS * HEAD_DIM
RW_W = RW_HEADS * HEAD_DIM
SWA_W = SWA_HEADS * HEAD_DIM
SWA_KV_W = SWA_KV_HEADS * HEAD_DIM
MIX_W = NA_W + RW_W + SWA_W
RW_IN_W = 3 * RW_W + 2 * RW_DECAY_LORA + 2 * RW_A_LORA + RW_GATE_LORA
IN_W = 3 * NA_W + RW_IN_W + SWA_W + 2 * SWA_KV_W
IN_SPLITS = [NA_W, 2 * NA_W, 3 * NA_W, 3 * NA_W + RW_IN_W, 3 * NA_W + RW_IN_W + SWA_W, 3 * NA_W + RW_IN_W + SWA_W + SWA_KV_W]
RW_SPLITS = [RW_W, 2 * RW_W, 3 * RW_W, 3 * RW_W + 2 * RW_DECAY_LORA, 3 * RW_W + 2 * RW_DECAY_LORA + 2 * RW_A_LORA]
F32 = jnp.float32

kernel_name = 'hybrid_na_rwkv7_swa_ec_diffusion_step'


def rms_norm(x, g):
    xf = x.astype(F32)
    y = xf * lax.rsqrt(jnp.mean(xf * xf, axis=-1, keepdims=True) + NORM_EPS)
    return (y * g).astype(x.dtype)


def adaln(cond, w, b):
    m = jax.nn.silu(cond) @ w + b
    return [z[:, None, :] for z in jnp.split(m, 6, axis=-1)]


def split_heads(z, n_heads):
    b, l, _ = z.shape
    return z.reshape(b, l, n_heads, HEAD_DIM).transpose(0, 2, 1, 3)


def merge_heads(z):
    b, h, l, d = z.shape
    return z.transpose(0, 2, 1, 3).reshape(b, l, h * d)


def axial_rope(x):
    n = x.shape[2]
    t = jnp.arange(n)
    n_freq = HEAD_DIM // 4
    inv = ROPE_THETA ** (-jnp.arange(n_freq, dtype=F32) / n_freq)
    ang = jnp.concatenate([(t // GRID_W).astype(F32)[:, None] * inv, (t % GRID_W).astype(F32)[:, None] * inv], axis=-1)
    cos, sin = jnp.cos(ang), jnp.sin(ang)
    xf = x.astype(F32)
    x1, x2 = xf[..., :HEAD_DIM // 2], xf[..., HEAD_DIM // 2:]
    return jnp.concatenate([x1 * cos - x2 * sin, x1 * sin + x2 * cos], axis=-1).astype(x.dtype)


def project(h, p):
    u = h @ p['w_in']
    qa, ka, va, u_rw, qc, kc, vc = jnp.split(u, IN_SPLITS, axis=-1)
    qa = rms_norm(split_heads(qa, NA_HEADS), p['na_q_norm'])
    ka = rms_norm(split_heads(ka, NA_HEADS), p['na_k_norm'])
    va = split_heads(va, NA_HEADS)
    qc = rms_norm(split_heads(qc, SWA_HEADS), p['swa_q_norm'])
    kc = rms_norm(split_heads(kc, SWA_KV_HEADS), p['swa_k_norm'])
    vc = split_heads(vc, SWA_KV_HEADS)
    return qa, ka, va, u_rw, qc, kc, vc


def dense_attention(q, k, v, sink):
    b, hq, l, d = q.shape
    hkv = k.shape[1]
    g = hq // hkv
    qg = q.reshape(b, hkv, g, l, d)
    s = jnp.einsum('bhgqd,bhkd->bhgqk', qg, k).astype(F32) * HEAD_DIM ** -0.5
    if sink is None:
        pr = jax.nn.softmax(s, axis=-1)
    else:
        sk = jnp.broadcast_to(sink.astype(F32).reshape(1, hkv, g, 1, 1), s.shape[:-1] + (1,))
        pr = jax.nn.softmax(jnp.concatenate([s, sk], axis=-1), axis=-1)[..., :-1]
    o = jnp.einsum('bhgqk,bhkd->bhgqd', pr.astype(v.dtype), v)
    return o.reshape(b, hq, l, d)


def neighbourhood_attention(q, k, v, k_ctx, v_ctx, rpb):
    b, h, n, d = q.shape
    rows = n // GRID_W
    kh = min(NA_KH_MAX, rows)
    nqb = GRID_W // NA_QB
    scale = HEAD_DIM ** -0.5
    r = jnp.arange(rows)
    row_idx = jnp.clip(r - kh // 2, 0, rows - kh)[:, None] + jnp.arange(kh)
    j = jnp.arange(nqb)
    col_idx = jnp.clip(j * NA_QB - NA_KW // 2, 0, GRID_W - NA_KB)[:, None] + jnp.arange(NA_KB)
    q_col = j[:, None] * NA_QB + jnp.arange(NA_QB)
    c_start = jnp.clip(q_col - NA_KW // 2, 0, GRID_W - NA_KW)
    col_ok = (col_idx[:, None, :] >= c_start[..., None]) & (col_idx[:, None, :] < c_start[..., None] + NA_KW)
    flat = row_idx[:, None, :, None] * GRID_W + col_idx[None, :, None, :]
    kg = k[:, :, flat]
    vg = v[:, :, flat]
    qb = q.reshape(b, h, rows, nqb, NA_QB, d)
    s_win = jnp.einsum('bhrjqd,bhrjkcd->bhrjqkc', qb, kg).astype(F32) * scale
    d_row = row_idx - r[:, None] + NA_KH_MAX - 1
    d_col = jnp.clip(col_idx[:, None, :] - q_col[..., None], 1 - NA_KW, NA_KW - 1) + NA_KW - 1
    bias = rpb[:, d_row[:, None, None, :, None], d_col[None, :, :, None, :]].astype(F32)
    s_win = jnp.where(col_ok[:, :, None, :], s_win + bias, NEG_INF)
    s_win = s_win.reshape(b, h, rows, nqb, NA_QB, kh * NA_KB)
    s_ctx = jnp.einsum('bhnd,bhld->bhnl', q, k_ctx).astype(F32) * scale
    s_ctx = s_ctx.reshape(b, h, rows, nqb, NA_QB, -1)
    pr = jax.nn.softmax(jnp.concatenate([s_win, s_ctx], axis=-1), axis=-1).astype(v.dtype)
    p_win = pr[..., :kh * NA_KB].reshape(b, h, rows, nqb, NA_QB, kh, NA_KB)
    p_ctx = pr[..., kh * NA_KB:]
    o = jnp.einsum('bhrjqkc,bhrjkcd->bhrjqd', p_win, vg) + jnp.einsum('bhrjql,bhld->bhrjqd', p_ctx, v_ctx)
    return o.reshape(b, h, n, d)


def window_attention(q, k, v, k_ctx, v_ctx, sink):
    b, hq, n, d = q.shape
    hkv = k.shape[1]
    g = hq // hkv
    nb = n // SWA_BLOCK
    scale = HEAD_DIM ** -0.5
    qb = q.reshape(b, hkv, g, nb, SWA_BLOCK, d)
    pad = ((0, 0), (0, 0), (SWA_BLOCK, SWA_BLOCK), (0, 0))
    idx = jnp.arange(nb)[:, None] * SWA_BLOCK + jnp.arange(3 * SWA_BLOCK)
    kg = jnp.pad(k, pad)[:, :, idx]
    vg = jnp.pad(v, pad)[:, :, idx]
    q_pos = jnp.arange(nb)[:, None] * SWA_BLOCK + jnp.arange(SWA_BLOCK)
    k_pos = idx - SWA_BLOCK
    ok = (jnp.abs(q_pos[:, :, None] - k_pos[:, None, :]) <= SWA_WINDOW) & (k_pos >= 0)[:, None, :] & (k_pos < n)[:, None, :]
    s_win = jnp.einsum('bhgnqd,bhnkd->bhgnqk', qb, kg).astype(F32) * scale
    s_win = jnp.where(ok, s_win, NEG_INF)
    s_ctx = jnp.einsum('bhgnqd,bhld->bhgnql', qb, k_ctx).astype(F32) * scale
    sk = jnp.broadcast_to(sink.astype(F32).reshape(1, hkv, g, 1, 1, 1), s_win.shape[:-1] + (1,))
    pr = jax.nn.softmax(jnp.concatenate([s_win, s_ctx, sk], axis=-1), axis=-1).astype(v.dtype)
    w = 3 * SWA_BLOCK
    o = jnp.einsum('bhgnqk,bhnkd->bhgnqd', pr[..., :w], vg) + jnp.einsum('bhgnql,bhld->bhgnqd', pr[..., w:-1], v_ctx)
    return o.reshape(b, hq, n, d)


def token_shift(u, mu):
    prev = jnp.pad(u, ((0, 0), (1, 0), (0, 0)))[:, :-1]
    nxt = jnp.pad(u, ((0, 0), (0, 1), (0, 0)))[:, 1:]
    return u + mu[0] * (prev - u) + mu[1] * (nxt - u)


def orient(z):
    z = jnp.transpose(z, (1, 2, 0, 3, 4))
    return jnp.stack([z[:, 0], z[::-1, 1]], axis=1)


def rwkv_step(s, inp):
    decay, kk, kka, k, v, r = inp
    sa = jnp.einsum('dbhvk,dbhk->dbhv', s, -kk)
    s = s * decay[..., None, :] + sa[..., None] * kka[..., None, :] + v[..., None] * k[..., None, :]
    return s, jnp.einsum('dbhvk,dbhk->dbhv', s, r)


def rwkv_mix(u, s0, p):
    b, l, _ = u.shape
    out_dtype = u.dtype
    hd = (RW_HEADS, HEAD_DIM)
    u = token_shift(u.astype(F32), p['rw_mu'].astype(F32))
    r, k, v, wl, al, gl = jnp.split(u, RW_SPLITS, axis=-1)
    wl = wl.reshape(b, l, 2, RW_DECAY_LORA)
    al = al.reshape(b, l, 2, RW_A_LORA)
    w = -jax.nn.softplus(-(p['rw_w0'] + jnp.einsum('bldr,drc->bldc', jnp.tanh(wl), p['rw_w2']))) - 0.5
    decay = jnp.exp(-jnp.exp(w))
    a = jax.nn.sigmoid(p['rw_a0'] + jnp.einsum('bldr,drc->bldc', al, p['rw_a2']))
    g = jax.nn.sigmoid(gl) @ p['rw_g2']
    kk = (k * p['rw_k_k']).reshape(b, l, *hd)
    kk = kk * lax.rsqrt(jnp.maximum(jnp.sum(kk * kk, axis=-1, keepdims=True), 1e-24))
    kd = (k[:, :, None] * (1.0 + (a - 1.0) * p['rw_k_a'])).reshape(b, l, 2, *hd)
    r = r.reshape(b, l, *hd)
    v = v.reshape(b, l, *hd)
    both = lambda z: jnp.broadcast_to(z[:, :, None], (b, l, 2) + hd)
    xs = (orient(decay.reshape(b, l, 2, *hd)), orient(both(kk)), orient(both(kk) * a.reshape(b, l, 2, *hd)),
          orient(kd), orient(both(v)), orient(both(r)))
    s_fin, o = lax.scan(rwkv_step, s0.astype(F32), xs)
    y = jnp.transpose(o[:, 0] + o[::-1, 1], (1, 0, 2, 3))
    mu = jnp.mean(y, axis=-1, keepdims=True)
    var = jnp.mean(jnp.square(y - mu), axis=-1, keepdims=True)
    y = (y - mu) * lax.rsqrt(var + GN_EPS) * p['rw_ln_g'].reshape(hd) + p['rw_ln_b'].reshape(hd)
    bonus = jnp.einsum('blhn,bldhn,hn->blh', r, kd, p['rw_r_k'])[..., None] * v
    out = (y + bonus).reshape(b, l, RW_W) * g
    return out.astype(out_dtype), s_fin


def expert_choice(h, w_router, w_gate, w_up, w_down):
    b, l, _ = h.shape
    cap = EC_CAPACITY * l // N_EXPERTS
    aff = jax.nn.softmax((h @ w_router).astype(F32), axis=-1)
    gate, idx = lax.top_k(jnp.swapaxes(aff, 1, 2), cap)
    bi = jnp.arange(b)[:, None, None]
    xe = h[bi, idx]
    he = jax.nn.silu(jnp.einsum('becd,edf->becf', xe, w_gate)) * jnp.einsum('becd,edf->becf', xe, w_up)
    ye = jnp.einsum('becf,efd->becd', he, w_down) * gate[..., None].astype(h.dtype)
    return jnp.zeros_like(h).at[bi, idx].add(ye)


def finish(x, oa, ob, oc, g1, sh2, sc2, g2, p):
    mix = jnp.concatenate([merge_heads(oa), ob, merge_heads(oc)], axis=-1) @ p['w_out']
    x = x + g1 * mix
    h = rms_norm(x, p['norm2_g']) * (1.0 + sc2) + sh2
    return x + g2 * expert_choice(h, p['w_router'], p['w_gate'], p['w_up'], p['w_down'])


def context_layer(x, c_ctx, p):
    sh1, sc1, g1, sh2, sc2, g2 = adaln(c_ctx[None], p['ada_w'], p['ada_b'])
    h = rms_norm(x, p['norm1_g']) * (1.0 + sc1) + sh1
    qa, ka, va, u_rw, qc, kc, vc = project(h, p)
    oa = dense_attention(qa, ka, va, None)
    s0 = jnp.zeros((2, x.shape[0], RW_HEADS, HEAD_DIM, HEAD_DIM), F32)
    ob, s_fin = rwkv_mix(u_rw, s0, p)
    oc = dense_attention(qc, kc, vc, p['swa_sink'])
    x = finish(x, oa, ob, oc, g1, sh2, sc2, g2, p)
    return x, ka, va, kc, vc, jnp.transpose(s_fin, (1, 0, 2, 3, 4))


def latent_layer(x, c, k_na, v_na, k_swa, v_swa, s_rw, p):
    sh1, sc1, g1, sh2, sc2, g2 = adaln(c, p['ada_w'], p['ada_b'])
    h = rms_norm(x, p['norm1_g']) * (1.0 + sc1) + sh1
    qa, ka, va, u_rw, qc, kc, vc = project(h, p)
    oa = neighbourhood_attention(qa, ka, va, k_na, v_na, p['na_rpb'])
    ob, _ = rwkv_mix(u_rw, jnp.transpose(s_rw, (1, 0, 2, 3, 4)), p)
    oc = window_attention(axial_rope(qc), axial_rope(kc), vc, k_swa, v_swa, p['swa_sink'])
    return finish(x, oa, ob, oc, g1, sh2, sc2, g2, p)


def setup_inputs(seed: int = 0) -> dict:
    key = jax.random.key(seed)
    ks = iter(jax.random.split(key, 64))

    def nrm(shape, scale):
        return scale * jax.random.normal(next(ks), shape, F32)

    def gain(shape):
        return 1.0 + nrm(shape, 0.02)

    def unif(shape, lo, hi):
        return jax.random.uniform(next(ks), shape, F32, lo, hi)

    L, D = DEPTH, D_MODEL
    return {
        'x_prompt': nrm((BATCH, SEQ, D), 1.0),
        'x_sample': nrm((DEC_BATCH, DEC_SEQ, D), 1.0),
        'cache_na_k': nrm((DEC_BATCH, L, NA_HEADS, PAST_LEN, HEAD_DIM), 1.0),
        'cache_na_v': nrm((DEC_BATCH, L, NA_HEADS, PAST_LEN, HEAD_DIM), 1.0),
        'cache_swa_k': nrm((DEC_BATCH, L, SWA_KV_HEADS, PAST_LEN, HEAD_DIM), 1.0),
        'cache_swa_v': nrm((DEC_BATCH, L, SWA_KV_HEADS, PAST_LEN, HEAD_DIM), 1.0),
        'state_rwkv': nrm((DEC_BATCH, L, 2, RW_HEADS, HEAD_DIM, HEAD_DIM), 0.3),
        'c': nrm((DEC_BATCH, D), 1.0),
        'c_ctx': nrm((D,), 1.0),
        'ada_w': nrm((L, D, 6 * D), 0.5 * D ** -0.5),
        'ada_b': nrm((L, 6 * D), 0.02),
        'norm1_g': gain((L, D)),
        'norm2_g': gain((L, D)),
        'w_in': nrm((L, D, IN_W), D ** -0.5),
        'na_q_norm': gain((L, HEAD_DIM)),
        'na_k_norm': gain((L, HEAD_DIM)),
        'na_rpb': nrm((L, NA_HEADS, 2 * NA_KH_MAX - 1, 2 * NA_KW - 1), 0.1),
        'rw_mu': unif((L, 2, RW_IN_W), 0.0, 0.4),
        'rw_w0': unif((L, 2, RW_W), -6.0, 1.0),
        'rw_w2': nrm((L, 2, RW_DECAY_LORA, RW_W), 0.5 * RW_DECAY_LORA ** -0.5),
        'rw_a0': nrm((L, 2, RW_W), 0.3),
        'rw_a2': nrm((L, 2, RW_A_LORA, RW_W), 0.5 * RW_A_LORA ** -0.5),
        'rw_g2': nrm((L, RW_GATE_LORA, RW_W), RW_GATE_LORA ** -0.5),
        'rw_k_k': 0.85 + nrm((L, RW_W), 0.05),
        'rw_k_a': gain((L, RW_W)),
        'rw_r_k': nrm((L, RW_HEADS, HEAD_DIM), 0.1),
        'rw_ln_g': gain((L, RW_W)),
        'rw_ln_b': nrm((L, RW_W), 0.02),
        'swa_q_norm': gain((L, HEAD_DIM)),
        'swa_k_norm': gain((L, HEAD_DIM)),
        'swa_sink': nrm((L, SWA_HEADS), 1.0),
        'w_out': nrm((L, MIX_W, D), MIX_W ** -0.5),
        'w_router': nrm((L, D, N_EXPERTS), D ** -0.5),
        'w_gate': nrm((L, N_EXPERTS, D, D_FF), D ** -0.5),
        'w_up': nrm((L, N_EXPERTS, D, D_FF), D ** -0.5),
        'w_down': nrm((L, N_EXPERTS, D_FF, D), D_FF ** -0.5),
    }


def reference(x_prompt, x_sample, cache_na_k, cache_na_v, cache_swa_k, cache_swa_v, state_rwkv, c, c_ctx,
              ada_w, ada_b, norm1_g, norm2_g, w_in, na_q_norm, na_k_norm, na_rpb,
              rw_mu, rw_w0, rw_w2, rw_a0, rw_a2, rw_g2, rw_k_k, rw_k_a, rw_r_k, rw_ln_g, rw_ln_b,
              swa_q_norm, swa_k_norm, swa_sink, w_out, w_router, w_gate, w_up, w_down):
    xp = x_prompt
    xs = x_sample
    new_ka, new_va, new_kc, new_vc, new_s = [], [], [], [], []
    for l in range(DEPTH):
        p = {
            'ada_w': ada_w[l], 'ada_b': ada_b[l], 'norm1_g': norm1_g[l], 'norm2_g': norm2_g[l],
            'w_in': w_in[l], 'na_q_norm': na_q_norm[l], 'na_k_norm': na_k_norm[l], 'na_rpb': na_rpb[l],
            'rw_mu': rw_mu[l], 'rw_w0': rw_w0[l], 'rw_w2': rw_w2[l], 'rw_a0': rw_a0[l], 'rw_a2': rw_a2[l],
            'rw_g2': rw_g2[l], 'rw_k_k': rw_k_k[l], 'rw_k_a': rw_k_a[l], 'rw_r_k': rw_r_k[l],
            'rw_ln_g': rw_ln_g[l], 'rw_ln_b': rw_ln_b[l], 'swa_q_norm': swa_q_norm[l],
            'swa_k_norm': swa_k_norm[l], 'swa_sink': swa_sink[l], 'w_out': w_out[l],
            'w_router': w_router[l], 'w_gate': w_gate[l], 'w_up': w_up[l], 'w_down': w_down[l],
        }
        xp, ka, va, kc, vc, s = context_layer(xp, c_ctx, p)
        new_ka.append(ka)
        new_va.append(va)
        new_kc.append(kc)
        new_vc.append(vc)
        new_s.append(s)
        xs = latent_layer(xs, c, cache_na_k[:, l], cache_na_v[:, l], cache_swa_k[:, l], cache_swa_v[:, l],
                          state_rwkv[:, l], p)
    return (xp, xs, jnp.stack(new_ka, axis=1), jnp.stack(new_va, axis=1), jnp.stack(new_kc, axis=1),
            jnp.stack(new_vc, axis=1), jnp.stack(new_s, axis=1))
```

```python
import functools

import numpy as np
import jax
import jax.numpy as jnp
from jax import lax
from jax.experimental import pallas as pl
from jax.experimental.pallas import tpu as pltpu

F32 = jnp.float32
BF16 = jnp.bfloat16

HEAD_DIM = 64
GRID_W = 64
NA_HEADS = 6
NA_KH = 8
NA_KW = 16
RW_HEADS = 4
SWA_HEADS = 6
SWA_KV_HEADS = 2
SWA_WINDOW = 128
N_EXPERTS = 16
EC_CAPACITY = 2
ROPE_THETA = 10000.0
NORM_EPS = 1e-6
GN_EPS = 64e-5
NEG_INF = -1e30
RW_CHUNK = 64
RW_W = RW_HEADS * HEAD_DIM
NA_W = NA_HEADS * HEAD_DIM
SWA_W = SWA_HEADS * HEAD_DIM
SWA_KV_W = SWA_KV_HEADS * HEAD_DIM
RW_IN_W = 1152
VMEM_LIMIT = 56 * 1024 * 1024


def _cp(*sem):
    return pltpu.CompilerParams(dimension_semantics=sem, vmem_limit_bytes=VMEM_LIMIT)


def _bf(x):
    return x.astype(BF16)


def _dot(a, b):
    return jnp.dot(a, b, preferred_element_type=F32)


def _dot_nt(a, b):
    return lax.dot_general(a, b, (((1,), (1,)), ((), ())), preferred_element_type=F32)


def _dot_tn(a, b):
    return lax.dot_general(a, b, (((0,), (0,)), ((), ())), preferred_element_type=F32)


def _split2(x):
    hi = x.astype(BF16)
    lo = (x - hi.astype(F32)).astype(BF16)
    return hi, lo


def _split3(x):
    hi = x.astype(BF16)
    r1 = x - hi.astype(F32)
    mid = r1.astype(BF16)
    lo = (r1 - mid.astype(F32)).astype(BF16)
    return hi, mid, lo


def _dot2(a, b_bf):
    hi, lo = _split2(a)
    return _dot(hi, b_bf) + _dot(lo, b_bf)


def _sigmoid(x):
    return 1.0 / (1.0 + jnp.exp(-x))


def _block_ones(width):
    i = np.arange(width) // HEAD_DIM
    return jnp.asarray((i[:, None] == i[None, :]).astype(np.float32), dtype=BF16)


def _full(shape):
    return pl.BlockSpec(shape, lambda *_: (0,) * len(shape))


def _adaln_kernel(c_ref, w_ref, b_ref, o_ref):
    c = c_ref[...]
    s = c * _sigmoid(c)
    shi, slo = _split2(s)
    whi, wlo = _split2(w_ref[0])
    o_ref[0] = _dot(shi, whi) + _dot(slo, whi) + _dot(shi, wlo) + b_ref[0]


def _adaln(cond, ada_w, ada_b):
    nl, d, n6 = ada_w.shape
    tn = 1536
    rows = cond.shape[0]
    return pl.pallas_call(
        _adaln_kernel,
        grid=(nl, n6 // tn),
        in_specs=[pl.BlockSpec((rows, d), lambda l, j: (0, 0)),
                  pl.BlockSpec((1, d, tn), lambda l, j: (l, 0, j)),
                  pl.BlockSpec((1, 1, tn), lambda l, j: (l, 0, j))],
        out_specs=pl.BlockSpec((1, rows, tn), lambda l, j: (l, 0, j)),
        out_shape=jax.ShapeDtypeStruct((nl, rows, n6), F32),
        compiler_params=_cp("parallel", "parallel"),
        name="adaln",
    )(cond, ada_w, ada_b.reshape(nl, 1, n6))


def _head_norm(z, gain, ones_bf):
    ms = _dot2(z * z, ones_bf) * (1.0 / HEAD_DIM)
    return z * lax.rsqrt(ms + NORM_EPS) * gain


def _rope(z, cos, sin_signed):
    w = z.shape[1]
    lane = lax.broadcasted_iota(jnp.int32, z.shape, 1)
    first = (lane % HEAD_DIM) < (HEAD_DIM // 2)
    swapped = jnp.where(first, pltpu.roll(z, w - HEAD_DIM // 2, 1), pltpu.roll(z, HEAD_DIM // 2, 1))
    return z * cos + swapped * sin_signed


def _proj_kernel(*refs, rope):
    if rope:
        (x_ref, mod_ref, n1_ref, w_ref, gqa_ref, gka_ref, gqc_ref, gkc_ref, ones_ref, cos_ref, sin_ref,
         qa_ref, ka_ref, va_ref, urw_ref, qc_ref, kc_ref, vc_ref) = refs
    else:
        (x_ref, mod_ref, n1_ref, w_ref, gqa_ref, gka_ref, gqc_ref, gkc_ref, ones_ref,
         qa_ref, ka_ref, va_ref, urw_ref, qc_ref, kc_ref, vc_ref) = refs
    x = x_ref[...]
    sh1 = mod_ref[0, 0:1, :]
    sc1 = mod_ref[0, 1:2, :]
    ms = jnp.mean(x * x, axis=-1, keepdims=True)
    h = x * lax.rsqrt(ms + NORM_EPS) * n1_ref[...] * (1.0 + sc1) + sh1
    u = _dot(_bf(h), w_ref[...])
    o0 = 0
    o1 = NA_W
    o2 = 2 * NA_W
    o3 = 3 * NA_W
    o4 = o3 + RW_IN_W
    o5 = o4 + SWA_W
    o6 = o5 + SWA_KV_W
    ones = ones_ref[...]
    ones_kv = ones_ref[0:SWA_KV_W, 0:SWA_KV_W]
    qa = _head_norm(u[:, o0:o1], gqa_ref[...], ones)
    ka = _head_norm(u[:, o1:o2], gka_ref[...], ones)
    qc = _head_norm(u[:, o4:o5], gqc_ref[...], ones)
    kc = _head_norm(u[:, o5:o6], gkc_ref[...], ones_kv)
    if rope:
        qc = _rope(qc, cos_ref[...], sin_ref[...])
        kc = _rope(kc, cos_ref[:, 0:SWA_KV_W], sin_ref[:, 0:SWA_KV_W])
    qa_ref[...] = qa.astype(qa_ref.dtype)
    ka_ref[...] = ka.astype(ka_ref.dtype)
    va_ref[...] = u[:, o2:o3].astype(va_ref.dtype)
    urw_ref[...] = u[:, o3:o4]
    qc_ref[...] = qc.astype(qc_ref.dtype)
    kc_ref[...] = kc.astype(kc_ref.dtype)
    vc_ref[...] = u[:, o6:].astype(vc_ref.dtype)


def _proj(x, mod, n1, w_in_bf, gains, ones384, rope_tabs, seq, per_request_mod, qkv_dtype):
    tokens, d = x.shape
    tm = 256
    tiles_per_req = seq // tm
    in_w = w_in_bf.shape[1]
    rope = rope_tabs is not None
    mod_map = (lambda i: (i // tiles_per_req, 0, 0)) if per_request_mod else (lambda i: (0, 0, 0))
    row = lambda w: pl.BlockSpec((tm, w), lambda i: (i, 0))
    in_specs = [row(d), pl.BlockSpec((1, 6, d), mod_map), _full((1, d)), _full((d, in_w)),
                _full((1, NA_W)), _full((1, NA_W)), _full((1, SWA_W)), _full((1, SWA_KV_W)),
                _full((NA_W, NA_W))]
    args = [x, mod, n1, w_in_bf, *gains, ones384]
    if rope:
        tab = pl.BlockSpec((tm, SWA_W), lambda i: (i % tiles_per_req, 0))
        in_specs += [tab, tab]
        args += list(rope_tabs)
    widths = [NA_W, NA_W, NA_W, RW_IN_W, SWA_W, SWA_KV_W, SWA_KV_W]
    dtypes = [qkv_dtype, qkv_dtype, qkv_dtype, F32, qkv_dtype, qkv_dtype, qkv_dtype]
    return pl.pallas_call(
        functools.partial(_proj_kernel, rope=rope),
        grid=(tokens // tm,),
        in_specs=in_specs,
        out_specs=[row(w) for w in widths],
        out_shape=[jax.ShapeDtypeStruct((tokens, w), dt) for w, dt in zip(widths, dtypes)],
        compiler_params=_cp("parallel"),
        name="proj",
    )(*args)


def _half_masks(width=2 * HEAD_DIM):
    lane = lax.broadcasted_iota(jnp.int32, (1, width), 1)
    return lane < HEAD_DIM, lane >= HEAD_DIM


def _swap_halves(z):
    return pltpu.roll(z, HEAD_DIM, 1)


def _ctx_attn_kernel(sink_ref, qa_ref, ka_ref, va_ref, qc_ref, kc_ref, vc_ref, oa_ref, oc_ref):
    scale = HEAD_DIM ** -0.5
    m0, m1 = _half_masks()
    masks = (m0, m1)
    for pair in range(NA_HEADS // 2):
        sl = slice(pair * 128, (pair + 1) * 128)
        qp = qa_ref[:, sl].astype(F32) * scale
        kp = _bf(ka_ref[:, sl])
        vp = _bf(va_ref[:, sl])
        outs = []
        for half in range(2):
            qm = _bf(jnp.where(masks[half], qp, 0.0))
            s = _dot_nt(qm, kp)
            m = jnp.max(s, axis=-1, keepdims=True)
            e = jnp.exp(s - m)
            l = jnp.sum(e, axis=-1, keepdims=True)
            outs.append(_dot(_bf(e), vp) / l)
        oa_ref[:, sl] = jnp.where(m0, outs[0], outs[1])
    kc = _bf(kc_ref[...])
    vc = _bf(vc_ref[...])
    group = SWA_HEADS // SWA_KV_HEADS
    for pair in range(SWA_HEADS // 2):
        sl = slice(pair * 128, (pair + 1) * 128)
        qp = qc_ref[:, sl].astype(F32) * scale
        outs = []
        for half in range(2):
            h = 2 * pair + half
            g = h // group
            qh = qp if g == half else _swap_halves(qp)
            qm = _bf(jnp.where(masks[g], qh, 0.0))
            s = _dot_nt(qm, kc)
            sk = sink_ref[h]
            m = jnp.maximum(jnp.max(s, axis=-1, keepdims=True), sk)
            e = jnp.exp(s - m)
            l = jnp.sum(e, axis=-1, keepdims=True) + jnp.exp(sk - m)
            o = _dot(_bf(e), vc) / l
            outs.append(o if g == half else _swap_halves(o))
        oc_ref[:, sl] = jnp.where(m0, outs[0], outs[1])


def _ctx_attn(sink, qa, ka, va, qc, kc, vc, seq):
    tokens = qa.shape[0]
    blk = lambda w: pl.BlockSpec((seq, w), lambda b: (b, 0))
    return pl.pallas_call(
        _ctx_attn_kernel,
        grid=(tokens // seq,),
        in_specs=[pl.BlockSpec(memory_space=pltpu.SMEM), blk(NA_W), blk(NA_W), blk(NA_W), blk(SWA_W),
                  blk(SWA_KV_W), blk(SWA_KV_W)],
        out_specs=[blk(NA_W), blk(SWA_W)],
        out_shape=[jax.ShapeDtypeStruct((tokens, NA_W), F32), jax.ShapeDtypeStruct((tokens, SWA_W), F32)],
        compiler_params=_cp("parallel"),
        name="ctx_attn",
    )(sink, qa, ka, va, qc, kc, vc)


def _na_bias_table(rpb):
    qc = np.arange(GRID_W)
    kc = np.arange(GRID_W)
    c_start = np.clip(qc - NA_KW // 2, 0, GRID_W - NA_KW)
    ok = (kc[None, :] >= c_start[:, None]) & (kc[None, :] < c_start[:, None] + NA_KW)
    d_col = np.clip(kc[None, :] - qc[:, None], 1 - NA_KW, NA_KW - 1) + NA_KW - 1
    case = np.arange(NA_KH)
    d_row = case[:, None] + np.arange(NA_KH)[None, :]
    b = rpb[:, d_row[:, None, :, None], d_col[None, :, None, :]]
    b = jnp.where(jnp.asarray(ok)[None, None, :, None, :], b, NEG_INF)
    return b.reshape(rpb.shape[0], NA_KH, GRID_W, NA_KH * GRID_W).astype(F32)


def _na_kernel(q_ref, k_ref, v_ref, kx_ref, vx_ref, bias_ref, o_ref, *, rows):
    scale = HEAD_DIM ** -0.5
    m0, m1 = _half_masks()
    masks = (m0, m1)
    kx = _bf(kx_ref[...])
    vx = _bf(vx_ref[...])

    def body(r, carry):
        rs = jnp.clip(r - NA_KH // 2, 0, rows - NA_KH)
        case = rs - r + NA_KH - 1
        q0 = pl.multiple_of(r * GRID_W, GRID_W)
        k0 = pl.multiple_of(rs * GRID_W, GRID_W)
        qp = q_ref[pl.ds(q0, GRID_W), :].astype(F32) * scale
        kw = _bf(k_ref[pl.ds(k0, NA_KH * GRID_W), :])
        vw = _bf(v_ref[pl.ds(k0, NA_KH * GRID_W), :])
        outs = []
        for half in range(2):
            qm = _bf(jnp.where(masks[half], qp, 0.0))
            sw = _dot_nt(qm, kw) + bias_ref[half, pl.ds(case, 1)][0]
            sx = _dot_nt(qm, kx)
            m = jnp.maximum(jnp.max(sw, axis=-1, keepdims=True), jnp.max(sx, axis=-1, keepdims=True))
            ew = jnp.exp(sw - m)
            ex = jnp.exp(sx - m)
            l = jnp.sum(ew, axis=-1, keepdims=True) + jnp.sum(ex, axis=-1, keepdims=True)
            outs.append((_dot(_bf(ew), vw) + _dot(_bf(ex), vx)) / l)
        o_ref[pl.ds(q0, GRID_W), :] = jnp.where(m0, outs[0], outs[1])
        return carry

    lax.fori_loop(0, rows, body, 0)


def _na_attn(q, k, v, kx, vx, bias, seq, past):
    tokens = q.shape[0]
    nb = tokens // seq
    rows = seq // GRID_W
    blk = pl.BlockSpec((seq, 128), lambda b, p: (b, p))
    cblk = pl.BlockSpec((past, 128), lambda b, p: (b, p))
    return pl.pallas_call(
        functools.partial(_na_kernel, rows=rows),
        grid=(nb, NA_HEADS // 2),
        in_specs=[blk, blk, blk, cblk, cblk,
                  pl.BlockSpec((2, NA_KH, GRID_W, NA_KH * GRID_W), lambda b, p: (p, 0, 0, 0))],
        out_specs=blk,
        out_shape=jax.ShapeDtypeStruct((tokens, NA_W), F32),
        compiler_params=_cp("parallel", "parallel"),
        name="na_attn",
    )(q, k, v, kx, vx, bias)


def _swa_kernel(sink_ref, q_ref, k_ref, v_ref, kx_ref, vx_ref, o_ref, *, seq):
    scale = HEAD_DIM ** -0.5
    blk = SWA_WINDOW
    m0, m1 = _half_masks()
    masks = (m0, m1)
    kx = _bf(kx_ref[...])
    vx = _bf(vx_ref[...])
    group = SWA_HEADS // SWA_KV_HEADS

    def body(nb, carry):
        ks = jnp.clip((nb - 1) * blk, 0, seq - 3 * blk)
        ks = pl.multiple_of(ks, blk)
        q0 = pl.multiple_of(nb * blk, blk)
        kw = _bf(k_ref[pl.ds(ks, 3 * blk), :])
        vw = _bf(v_ref[pl.ds(ks, 3 * blk), :])
        qpos = q0 + lax.broadcasted_iota(jnp.int32, (blk, 1), 0)
        kpos = ks + lax.broadcasted_iota(jnp.int32, (1, 3 * blk), 1)
        ok = jnp.abs(qpos - kpos) <= SWA_WINDOW
        for pair in range(SWA_HEADS // 2):
            qp = q_ref[pl.ds(q0, blk), pair * 128:(pair + 1) * 128].astype(F32) * scale
            outs = []
            for half in range(2):
                h = 2 * pair + half
                g = h // group
                qh = qp if g == half else _swap_halves(qp)
                qm = _bf(jnp.where(masks[g], qh, 0.0))
                sw = jnp.where(ok, _dot_nt(qm, kw), NEG_INF)
                sx = _dot_nt(qm, kx)
                sk = sink_ref[h]
                m = jnp.maximum(jnp.maximum(jnp.max(sw, axis=-1, keepdims=True),
                                            jnp.max(sx, axis=-1, keepdims=True)), sk)
                ew = jnp.exp(sw - m)
                ex = jnp.exp(sx - m)
                l = jnp.sum(ew, axis=-1, keepdims=True) + jnp.sum(ex, axis=-1, keepdims=True) + jnp.exp(sk - m)
                o = (_dot(_bf(ew), vw) + _dot(_bf(ex), vx)) / l
                outs.append(o if g == half else _swap_halves(o))
            o_ref[pl.ds(q0, blk), pair * 128:(pair + 1) * 128] = jnp.where(m0, outs[0], outs[1])
        return carry

    lax.fori_loop(0, seq // blk, body, 0)


def _swa_attn(sink, q, k, v, kx, vx, seq, past):
    tokens = q.shape[0]
    blk = lambda w: pl.BlockSpec((seq, w), lambda b: (b, 0))
    cblk = pl.BlockSpec((past, SWA_KV_W), lambda b: (b, 0))
    return pl.pallas_call(
        functools.partial(_swa_kernel, seq=seq),
        grid=(tokens // seq,),
        in_specs=[pl.BlockSpec(memory_space=pltpu.SMEM), blk(SWA_W), blk(SWA_KV_W), blk(SWA_KV_W), cblk, cblk],
        out_specs=blk(SWA_W),
        out_shape=jax.ShapeDtypeStruct((tokens, SWA_W), F32),
        compiler_params=_cp("parallel"),
        name="swa_attn",
    )(sink, q, k, v, kx, vx)


def _rw_pre_kernel(u_ref, up_ref, un_ref, mu_ref, w0_ref, w2_ref, a0_ref, a2_ref, g2_ref, kk_ref_, ka_ref_,
                   rk_ref, ones_ref, r_o, kk_o, v_o, ld_o, kka_o, kd_o, g_o, bonus_o, *, tiles_per_req):
    i = pl.program_id(0)
    u = u_ref[...]
    tm = u.shape[0]
    rowi = lax.broadcasted_iota(jnp.int32, (tm, 1), 0)
    first = (i % tiles_per_req) == 0
    last = (i % tiles_per_req) == tiles_per_req - 1
    prev_row = jnp.where(first, 0.0, up_ref[7:8, :])
    next_row = jnp.where(last, 0.0, un_ref[0:1, :])
    prev = jnp.where(rowi == 0, prev_row, pltpu.roll(u, 1, 0))
    nxt = jnp.where(rowi == tm - 1, next_row, pltpu.roll(u, tm - 1, 0))
    us = u + mu_ref[0:1, :] * (prev - u) + mu_ref[1:2, :] * (nxt - u)
    r = us[:, 0:RW_W]
    k = us[:, RW_W:2 * RW_W]
    v = us[:, 2 * RW_W:3 * RW_W]
    wl = us[:, 3 * RW_W:3 * RW_W + 128]
    al = us[:, 3 * RW_W + 128:3 * RW_W + 256]
    gl = us[:, 3 * RW_W + 256:3 * RW_W + 384]
    z = -(w0_ref[...] + _dot(_bf(jnp.tanh(wl)), w2_ref[...]))
    softplus = jnp.maximum(z, 0.0) + jnp.log(1.0 + jnp.exp(-jnp.abs(z)))
    w = -softplus - 0.5
    ld = -jnp.exp(w)
    a = _sigmoid(a0_ref[...] + _dot(_bf(al), a2_ref[...]))
    g = _dot(_bf(_sigmoid(gl)), g2_ref[...])
    ones = ones_ref[...]
    kkr = k * kk_ref_[...]
    kk = kkr * lax.rsqrt(jnp.maximum(_dot2(kkr * kkr, ones), 1e-24))
    k_a = ka_ref_[...]
    kd_f = k * (1.0 + (a[:, 0:RW_W] - 1.0) * k_a)
    kd_b = k * (1.0 + (a[:, RW_W:] - 1.0) * k_a)
    r_o[...] = r
    kk_o[...] = kk
    v_o[...] = v
    ld_o[...] = ld
    kka_o[:, 0:RW_W] = kk * a[:, 0:RW_W]
    kka_o[:, RW_W:] = kk * a[:, RW_W:]
    kd_o[:, 0:RW_W] = kd_f
    kd_o[:, RW_W:] = kd_b
    g_o[...] = g
    bonus_o[...] = _dot2(r * (kd_f + kd_b) * rk_ref[...], ones) * v


def _rw_pre(urw, p, seq):
    tokens = urw.shape[0]
    tm = 256
    tpr = seq // tm
    nt = tokens // tm
    r8 = tm // 8
    row = lambda w: pl.BlockSpec((tm, w), lambda i: (i, 0))
    in_specs = [row(RW_IN_W),
                pl.BlockSpec((8, RW_IN_W), lambda i: (jnp.maximum(i * r8 - 1, 0), 0)),
                pl.BlockSpec((8, RW_IN_W), lambda i: (jnp.minimum((i + 1) * r8, nt * r8 - 1), 0)),
                _full((2, RW_IN_W)), _full((1, 2 * RW_W)), _full((128, 2 * RW_W)), _full((1, 2 * RW_W)),
                _full((128, 2 * RW_W)), _full((128, RW_W)), _full((1, RW_W)), _full((1, RW_W)),
                _full((1, RW_W)), _full((RW_W, RW_W))]
    widths = [RW_W, RW_W, RW_W, 2 * RW_W, 2 * RW_W, 2 * RW_W, RW_W, RW_W]
    return pl.pallas_call(
        functools.partial(_rw_pre_kernel, tiles_per_req=tpr),
        grid=(nt,),
        in_specs=in_specs,
        out_specs=[row(w) for w in widths],
        out_shape=[jax.ShapeDtypeStruct((tokens, w), F32) for w in widths],
        compiler_params=_cp("parallel"),
        name="rw_pre",
    )(urw, urw, urw, p['mu'], p['w0'], p['w2'], p['a0'], p['a2'], p['g2'], p['k_k'], p['k_a'], p['r_k'],
      p['ones256'])


def _rw_masks():
    t = RW_CHUNK
    n = RW_HEADS * t
    idx = np.arange(n)
    h, tt = idx // t, idx % t
    same = h[:, None] == h[None, :]
    lower = tt[None, :] < tt[:, None]
    upper = tt[None, :] > tt[:, None]
    diag = tt[None, :] == tt[:, None]
    strict = np.stack([same & lower, same & upper]).astype(np.float32)
    incl = np.stack([same & (lower | diag), same & (upper | diag)]).astype(np.float32)
    ti = np.arange(t)
    tri = np.stack([ti[None, :] <= ti[:, None], ti[None, :] >= ti[:, None]]).astype(np.float32)
    lane_head = np.arange(RW_W) // HEAD_DIM
    hm = (lane_head[None, :] == np.arange(RW_HEADS)[:, None]).astype(np.float32).reshape(RW_HEADS, 1, RW_W)
    eye = np.eye(n, dtype=np.float32)
    return (jnp.asarray(strict), jnp.asarray(incl), jnp.asarray(tri, dtype=BF16), jnp.asarray(hm),
            jnp.asarray(eye))


def _rw_scan_kernel(r_ref, kk_ref, v_ref, ld_ref, kka_ref, kd_ref, s0_ref, strict_ref, incl_ref, tri_ref,
                    hm_ref, eye_ref, o_ref, sfin_ref, s_scr, *, nsub):
    d = pl.program_id(1)
    c = pl.program_id(2)
    t = RW_CHUNK

    @pl.when(c == 0)
    def _():
        s_scr[...] = s0_ref[0, 0]

    strict = strict_ref[0]
    incl = incl_ref[0]
    tri = tri_ref[0]
    eye = eye_ref[...]

    def stack(x):
        return jnp.concatenate([x * hm_ref[h] for h in range(RW_HEADS)], axis=0)

    def tile(x):
        return jnp.concatenate([x] * RW_HEADS, axis=0)

    for j in range(nsub):
        jj = j + d * (nsub - 1 - 2 * j)
        rows = pl.ds(pl.multiple_of(jj * t, t), t)
        ld = ld_ref[rows, :]
        lhi, lmid, llo = _split3(ld)
        cum = _dot(tri, lhi) + _dot(tri, lmid) + _dot(tri, llo)
        cend = jnp.sum(ld, axis=0, keepdims=True)
        kk = kk_ref[rows, :]
        kka = kka_ref[rows, :]
        kd = kd_ref[rows, :]
        e_inv = jnp.exp(-cum)
        e_end = jnp.exp(cend - cum)
        at = -kk * jnp.exp(cum - ld)
        bt = kka * e_inv
        kt = kd * e_inv
        rt = r_ref[rows, :] * jnp.exp(cum)
        at_s = _bf(stack(at))
        rt_s = _bf(stack(rt))
        v_s = _bf(stack(v_ref[rows, :]))
        lhs = jnp.concatenate([at_s, rt_s], axis=0)
        rhs = _bf(jnp.concatenate([tile(bt), tile(kt)], axis=0))
        aa = _dot_nt(lhs, rhs)
        n = RW_HEADS * t
        a_ab = aa[0:n, 0:n] * strict
        a_ak = aa[0:n, n:] * strict
        a_rb = aa[n:, 0:n] * incl
        a_rk = aa[n:, n:] * incl
        x = eye + a_ab
        pw = a_ab
        for _ in range(5):
            pw_bf = _bf(pw)
            pw = _dot(pw_bf, pw_bf)
            x = x + _dot(_bf(x), _bf(pw))
        s_bf = _bf(s_scr[...])
        w_s = _dot_nt(at_s, s_bf) + _dot(_bf(a_ak), v_s)
        u_s = _dot(_bf(x), _bf(w_s))
        u_bf = _bf(u_s)
        o_s = _dot_nt(rt_s, s_bf) + _dot(_bf(a_rb), u_bf) + _dot(_bf(a_rk), v_s)
        o_ref[0, rows, :] = o_s[0:t] + o_s[t:2 * t] + o_s[2 * t:3 * t] + o_s[3 * t:]
        upd_l = jnp.concatenate([u_bf, v_s], axis=0)
        upd_r = _bf(jnp.concatenate([stack(kka * e_end), stack(kd * e_end)], axis=0))
        s_scr[...] = s_scr[...] * jnp.exp(cend) + _dot_tn(upd_l, upd_r)

    @pl.when(c == pl.num_programs(2) - 1)
    def _():
        sfin_ref[0, 0] = s_scr[...]


def _rw_scan(r, kk, v, ld, kka, kd, s0_bd, consts, seq):
    tokens = r.shape[0]
    nreq = tokens // seq
    tb = 256
    nblk = seq // tb
    nsub = tb // RW_CHUNK
    n = RW_HEADS * RW_CHUNK
    cc = lambda d, c: c + d * (nblk - 1 - 2 * c)
    shared = pl.BlockSpec((tb, RW_W), lambda b, d, c: (b * nblk + cc(d, c), 0))
    dirw = pl.BlockSpec((tb, RW_W), lambda b, d, c: (b * nblk + cc(d, c), d))
    strict, incl, tri, hm, eye = consts
    return pl.pallas_call(
        functools.partial(_rw_scan_kernel, nsub=nsub),
        grid=(nreq, 2, nblk),
        in_specs=[shared, shared, shared, dirw, dirw, dirw,
                  pl.BlockSpec((1, 1, n, n), lambda b, d, c: (b, d, 0, 0)),
                  pl.BlockSpec((1, n, n), lambda b, d, c: (d, 0, 0)),
                  pl.BlockSpec((1, n, n), lambda b, d, c: (d, 0, 0)),
                  pl.BlockSpec((1, RW_CHUNK, RW_CHUNK), lambda b, d, c: (d, 0, 0)),
                  _full((RW_HEADS, 1, RW_W)), _full((n, n))],
        out_specs=[pl.BlockSpec((1, tb, RW_W), lambda b, d, c: (d, b * nblk + cc(d, c), 0)),
                   pl.BlockSpec((1, 1, n, n), lambda b, d, c: (b, d, 0, 0))],
        out_shape=[jax.ShapeDtypeStruct((2, tokens, RW_W), F32), jax.ShapeDtypeStruct((nreq, 2, n, n), F32)],
        scratch_shapes=[pltpu.VMEM((n, n), F32)],
        compiler_params=_cp("parallel", "parallel", "arbitrary"),
        name="rw_scan",
    )(r, kk, v, ld, kka, kd, s0_bd, strict, incl, tri, hm, eye)


def _state_to_blockdiag(s):
    b = s.shape[0]
    eye = jnp.eye(RW_HEADS, dtype=s.dtype)
    out = s[:, :, :, :, None, :] * eye[None, None, :, None, :, None]
    return out.reshape(b, 2, RW_W, RW_W)


def _blockdiag_to_state(sbd):
    b = sbd.shape[0]
    s = sbd.reshape(b, 2, RW_HEADS, HEAD_DIM, RW_HEADS, HEAD_DIM)
    idx = jnp.arange(RW_HEADS)
    return jnp.transpose(s[:, :, idx, :, idx, :], (1, 2, 0, 3, 4))


def _finish_kernel(x_ref, oa_ref, oc_ref, o2_ref, bonus_ref, g_ref, mod_ref, wout_ref, lng_ref, lnb_ref,
                   n2_ref, wr_hi_ref, wr_lo_ref, ones_ref, x1_ref, h2_ref, aff_ref):
    ones = ones_ref[...]
    y = o2_ref[0] + o2_ref[1]
    mu = _dot2(y, ones) * (1.0 / HEAD_DIM)
    yc = y - mu
    var = _dot2(yc * yc, ones) * (1.0 / HEAD_DIM)
    yn = yc * lax.rsqrt(var + GN_EPS) * lng_ref[...] + lnb_ref[...]
    ob = (yn + bonus_ref[...]) * g_ref[...]
    mixin = jnp.concatenate([_bf(oa_ref[...]), _bf(ob), _bf(oc_ref[...])], axis=1)
    mix = _dot(mixin, wout_ref[...])
    g1 = mod_ref[0, 2:3, :]
    sh2 = mod_ref[0, 3:4, :]
    sc2 = mod_ref[0, 4:5, :]
    x1 = x_ref[...] + g1 * mix
    ms = jnp.mean(x1 * x1, axis=-1, keepdims=True)
    h2 = x1 * lax.rsqrt(ms + NORM_EPS) * n2_ref[...] * (1.0 + sc2) + sh2
    x1_ref[...] = x1
    h2_ref[...] = _bf(h2)
    hhi, hlo = _split2(h2)
    logits = _dot(hhi, wr_hi_ref[...]) + _dot(hlo, wr_hi_ref[...]) + _dot(hhi, wr_lo_ref[...])
    m = jnp.max(logits, axis=-1, keepdims=True)
    e = jnp.exp(logits - m)
    aff_ref[...] = e / jnp.sum(e, axis=-1, keepdims=True)


def _finish(x, oa, oc, o2, bonus, g, mod, p, seq, per_request_mod):
    tokens, d = x.shape
    tm = 256
    tpr = seq // tm
    mod_map = (lambda i: (i // tpr, 0, 0)) if per_request_mod else (lambda i: (0, 0, 0))
    row = lambda w: pl.BlockSpec((tm, w), lambda i: (i, 0))
    return pl.pallas_call(
        _finish_kernel,
        grid=(tokens // tm,),
        in_specs=[row(d), row(NA_W), row(SWA_W), pl.BlockSpec((2, tm, RW_W), lambda i: (0, i, 0)), row(RW_W),
                  row(RW_W), pl.BlockSpec((1, 6, d), mod_map), _full((d, d)), _full((1, RW_W)),
                  _full((1, RW_W)), _full((1, d)), _full((d, N_EXPERTS)), _full((d, N_EXPERTS)),
                  _full((RW_W, RW_W))],
        out_specs=[row(d), row(d), row(N_EXPERTS)],
        out_shape=[jax.ShapeDtypeStruct((tokens, d), F32), jax.ShapeDtypeStruct((tokens, d), BF16),
                   jax.ShapeDtypeStruct((tokens, N_EXPERTS), F32)],
        compiler_params=_cp("parallel"),
        name="finish",
    )(x, oa, oc, o2, bonus, g, mod, p['w_out'], p['ln_g'], p['ln_b'], p['n2'], p['wr_hi'], p['wr_lo'],
      p['ones256'])


def _topk_kernel(aff_ref, tri_ref, eye_ref, place_ref, slot_ref, slotrow_ref, gfull_ref, *, cap, group, tb):
    b = pl.program_id(0)
    aff = aff_ref[...]
    seq = aff.shape[0]
    bits = lax.bitcast_convert_type(aff, jnp.int32)
    capf = jnp.float32(cap)

    def bis(_, carry):
        lo, hi = carry
        mid = lo + ((hi - lo + 1) >> 1)
        cnt = jnp.sum(jnp.where(bits >= mid, 1.0, 0.0), axis=0, keepdims=True)
        ge = cnt >= capf
        return jnp.where(ge, mid, lo), jnp.where(ge, hi, mid - 1)

    lo0 = jnp.zeros((1, N_EXPERTS), jnp.int32)
    hi0 = jnp.full((1, N_EXPERTS), 0x7F7FFFFF, jnp.int32)
    thr, _ = lax.fori_loop(0, 31, bis, (lo0, hi0))
    gt = jnp.where(bits > thr, 1.0, 0.0)
    eq = jnp.where(bits == thr, 1.0, 0.0)
    need = capf - jnp.sum(gt, axis=0, keepdims=True)
    offset = ((b % group) * cap).astype(F32)
    tri = tri_ref[...]
    eye = eye_ref[...]
    carry_g = jnp.zeros((1, N_EXPERTS), F32)
    carry_e = jnp.zeros((1, N_EXPERTS), F32)
    ghi, gmid, glo = _split3(aff)
    for blk in range(seq // tb):
        sl = slice(blk * tb, (blk + 1) * tb)
        pg = _dot(tri, _bf(gt[sl])) + carry_g
        pe = _dot(tri, _bf(eq[sl])) + carry_e
        carry_g = pg[tb - 1:tb, :]
        carry_e = pe[tb - 1:tb, :]
        sel = gt[sl] + eq[sl] * jnp.where(pe <= need, 1.0, 0.0)
        slot = jnp.where(sel > 0.5, pg + jnp.minimum(pe, need) - 1.0 + offset, -1.0)
        slot_ref[sl, :] = slot
        shi, slo = _split2(slot)
        slotrow_ref[0, :, 0, sl] = _dot_nt(eye, shi) + _dot_nt(eye, slo)
        gfull_ref[sl, :] = _bf(_dot(ghi[sl], place_ref[0]) + _dot(gmid[sl], place_ref[1])
                               + _dot(glo[sl], place_ref[2]))


def _topk(aff, seq, group):
    tokens = aff.shape[0]
    nreq = tokens // seq
    cap = EC_CAPACITY * seq // N_EXPERTS
    tb = min(seq, 512)
    ti = np.arange(tb)
    tri = jnp.asarray((ti[None, :] <= ti[:, None]).astype(np.float32), dtype=BF16)
    eye = jnp.asarray(np.eye(N_EXPERTS, dtype=np.float32), dtype=BF16)
    place = np.zeros((3, N_EXPERTS, 128), np.float32)
    for s in range(3):
        place[s, np.arange(N_EXPERTS), s * N_EXPERTS + np.arange(N_EXPERTS)] = 1.0
    place = jnp.asarray(place, dtype=BF16)
    return pl.pallas_call(
        functools.partial(_topk_kernel, cap=cap, group=group, tb=tb),
        grid=(nreq,),
        in_specs=[pl.BlockSpec((seq, N_EXPERTS), lambda b: (b, 0)), _full((tb, tb)),
                  _full((N_EXPERTS, N_EXPERTS)), _full((3, N_EXPERTS, 128))],
        out_specs=[pl.BlockSpec((seq, N_EXPERTS), lambda b: (b, 0)),
                   pl.BlockSpec((1, N_EXPERTS, 1, seq), lambda b: (b // group, 0, 0, b % group)),
                   pl.BlockSpec((seq, 128), lambda b: (b, 0))],
        out_shape=[jax.ShapeDtypeStruct((tokens, N_EXPERTS), F32),
                   jax.ShapeDtypeStruct((nreq // group, N_EXPERTS, 1, group * seq), F32),
                   jax.ShapeDtypeStruct((tokens, 128), BF16)],
        compiler_params=_cp("parallel"),
        name="topk",
    )(aff, tri, eye, place)


def _moe_ffn_kernel(h_ref, slotrow_ref, gfull_ref, mod_ref, wg_ref, wu_ref, wd_ref, ye_ref, *, ct, kc):
    e = pl.program_id(1)
    lg = h_ref.shape[0]
    d = h_ref.shape[1]
    jcol = lax.broadcasted_iota(jnp.int32, (ct, 1), 0).astype(F32)
    xe = jnp.zeros((ct, d), F32)
    gs = jnp.zeros((ct, 128), F32)
    for c in range(lg // kc):
        sl = slice(c * kc, (c + 1) * kc)
        onehot = _bf(jnp.where(slotrow_ref[0, 0, :, sl] == jcol, 1.0, 0.0))
        xe = xe + _dot(onehot, h_ref[sl, :])
        gs = gs + _dot(onehot, gfull_ref[sl, :])
    lane = lax.broadcasted_iota(jnp.int32, (1, 128), 1)
    pick = (lane == e) | (lane == e + N_EXPERTS) | (lane == e + 2 * N_EXPERTS)
    gate = jnp.sum(jnp.where(pick, gs, 0.0), axis=-1, keepdims=True)
    xb = _bf(xe)
    hg = _dot(xb, wg_ref[0])
    hu = _dot(xb, wu_ref[0])
    he = _bf(hg * _sigmoid(hg) * hu)
    y = _dot(he, wd_ref[0])
    ye_ref[0, 0] = _bf(y * gate * mod_ref[0, 5:6, :])


def _moe_ffn(h2, slotrow, gfull, mod, wg, wu, wd, lg, ct, per_group_mod):
    tokens, d = h2.shape
    ngrp = tokens // lg
    f = wg.shape[2]
    mod_map = (lambda gi, e: (gi, 0, 0)) if per_group_mod else (lambda gi, e: (0, 0, 0))
    return pl.pallas_call(
        functools.partial(_moe_ffn_kernel, ct=ct, kc=min(lg, 1024)),
        grid=(ngrp, N_EXPERTS),
        in_specs=[pl.BlockSpec((lg, d), lambda gi, e: (gi, 0)),
                  pl.BlockSpec((1, 1, 1, lg), lambda gi, e: (gi, e, 0, 0)),
                  pl.BlockSpec((lg, 128), lambda gi, e: (gi, 0)),
                  pl.BlockSpec((1, 6, d), mod_map),
                  pl.BlockSpec((1, d, f), lambda gi, e: (e, 0, 0)),
                  pl.BlockSpec((1, d, f), lambda gi, e: (e, 0, 0)),
                  pl.BlockSpec((1, f, d), lambda gi, e: (e, 0, 0))],
        out_specs=pl.BlockSpec((1, 1, ct, d), lambda gi, e: (gi, e, 0, 0)),
        out_shape=jax.ShapeDtypeStruct((ngrp, N_EXPERTS, ct, d), BF16),
        compiler_params=_cp("parallel", "arbitrary"),
        name="moe_ffn",
    )(h2, slotrow, gfull, mod, wg, wu, wd)


def _moe_combine_kernel(x1_ref, slot_ref, ye_ref, o_ref, *, ct):
    e = pl.program_id(2)

    @pl.when(e == 0)
    def _():
        o_ref[...] = x1_ref[...]

    slot = slot_ref[...]
    shi, slo = _split2(slot)
    krow = lax.broadcasted_iota(jnp.int32, (N_EXPERTS, 128), 0)
    sel = _bf(jnp.where(krow == e, 1.0, 0.0))
    sb = _dot(shi, sel) + _dot(slo, sel)
    lane = lax.broadcasted_iota(jnp.int32, (1, 128), 1).astype(F32)
    pieces = [_bf(jnp.where(sb == lane + float(j * 128), 1.0, 0.0)) for j in range(ct // 128)]
    onehot_t = jnp.concatenate(pieces, axis=1) if len(pieces) > 1 else pieces[0]
    o_ref[...] += _dot(onehot_t, ye_ref[0, 0])


def _moe_combine(x1, slot, ye, lg, ct):
    tokens, d = x1.shape
    ngrp = tokens // lg
    tb = min(lg, 2048)
    nb = lg // tb
    return pl.pallas_call(
        functools.partial(_moe_combine_kernel, ct=ct),
        grid=(ngrp, nb, N_EXPERTS),
        in_specs=[pl.BlockSpec((tb, d), lambda gi, t, e: (gi * nb + t, 0)),
                  pl.BlockSpec((tb, N_EXPERTS), lambda gi, t, e: (gi * nb + t, 0)),
                  pl.BlockSpec((1, 1, ct, d), lambda gi, t, e: (gi, e, 0, 0))],
        out_specs=pl.BlockSpec((tb, d), lambda gi, t, e: (gi * nb + t, 0)),
        out_shape=jax.ShapeDtypeStruct((tokens, d), F32),
        compiler_params=_cp("parallel", "parallel", "arbitrary"),
        name="moe_combine",
    )(x1, slot, ye)


def _rope_tables(seq):
    t = np.arange(seq)
    n_freq = HEAD_DIM // 4
    inv = ROPE_THETA ** (-np.arange(n_freq, dtype=np.float32) / n_freq)
    ang = np.concatenate([(t // GRID_W).astype(np.float32)[:, None] * inv,
                          (t % GRID_W).astype(np.float32)[:, None] * inv], axis=-1)
    ang = jnp.asarray(ang, dtype=F32)
    cos, sin = jnp.cos(ang), jnp.sin(ang)
    cos_t = jnp.tile(jnp.concatenate([cos, cos], axis=-1), (1, SWA_HEADS))
    sin_t = jnp.tile(jnp.concatenate([-sin, sin], axis=-1), (1, SWA_HEADS))
    return cos_t, sin_t


def _blockdiag2(w):
    z = jnp.zeros_like(w[0])
    return jnp.concatenate([jnp.concatenate([w[0], z], axis=1), jnp.concatenate([z, w[1]], axis=1)], axis=0)


def _layer_params(l, ada_w, ada_b, norm1_g, norm2_g, w_in, na_q_norm, na_k_norm, na_rpb, rw_mu, rw_w0, rw_w2,
                  rw_a0, rw_a2, rw_g2, rw_k_k, rw_k_a, rw_r_k, rw_ln_g, rw_ln_b, swa_q_norm, swa_k_norm,
                  swa_sink, w_out, w_router, w_gate, w_up, w_down):
    wr = w_router[l]
    wr_hi = wr.astype(BF16)
    return {
        'n1': norm1_g[l][None], 'n2': norm2_g[l][None], 'w_in': w_in[l].astype(BF16),
        'gains': (jnp.tile(na_q_norm[l], NA_HEADS)[None], jnp.tile(na_k_norm[l], NA_HEADS)[None],
                  jnp.tile(swa_q_norm[l], SWA_HEADS)[None], jnp.tile(swa_k_norm[l], SWA_KV_HEADS)[None]),
        'bias': _na_bias_table(na_rpb[l]),
        'mu': rw_mu[l], 'w0': rw_w0[l].reshape(1, 2 * RW_W), 'w2': _blockdiag2(rw_w2[l]).astype(BF16),
        'a0': rw_a0[l].reshape(1, 2 * RW_W), 'a2': _blockdiag2(rw_a2[l]).astype(BF16),
        'g2': rw_g2[l].astype(BF16), 'k_k': rw_k_k[l][None], 'k_a': rw_k_a[l][None],
        'r_k': rw_r_k[l].reshape(1, RW_W), 'ln_g': rw_ln_g[l][None], 'ln_b': rw_ln_b[l][None],
        'sink': swa_sink[l], 'w_out': w_out[l].astype(BF16),
        'wr_hi': wr_hi, 'wr_lo': (wr - wr_hi.astype(F32)).astype(BF16),
        'wg': w_gate[l].astype(BF16), 'wu': w_up[l].astype(BF16), 'wd': w_down[l].astype(BF16),
        'ones256': _block_ones(RW_W),
    }


def _mix_and_ffn(x, mod, p, oa, oc, urw, s0_bd, scan_consts, seq, per_request_mod, group):
    r, kk, v, ld, kka, kd, g, bonus = _rw_pre(urw, p, seq)
    o2, sfin = _rw_scan(r, kk, v, ld, kka, kd, s0_bd, scan_consts, seq)
    x1, h2, aff = _finish(x, oa, oc, o2, bonus, g, mod, p, seq, per_request_mod)
    slot, slotrow, gfull = _topk(aff, seq, group)
    cap = EC_CAPACITY * seq // N_EXPERTS
    lg, ct = group * seq, group * cap
    ye = _moe_ffn(h2, slotrow, gfull, mod, p['wg'], p['wu'], p['wd'], lg, ct, per_request_mod)
    return _moe_combine(x1, slot, ye, lg, ct), sfin


def _context_layer(x, mod, p, ones384, scan_consts, seq):
    qa, ka, va, urw, qc, kc, vc = _proj(x, mod, p['n1'], p['w_in'], p['gains'], ones384, None, seq, False, F32)
    oa, oc = _ctx_attn(p['sink'], qa, ka, va, qc, kc, vc, seq)
    nreq = x.shape[0] // seq
    s0 = jnp.zeros((nreq, 2, RW_W, RW_W), F32)
    y, sfin = _mix_and_ffn(x, mod, p, oa, oc, urw, s0, scan_consts, seq, False, 8)
    return y, ka, va, kc, vc, sfin


def _latent_layer(x, mod, p, ones384, scan_consts, rope_tabs, seq, kx_na, vx_na, kx_swa, vx_swa, s0_bd, past):
    qa, ka, va, urw, qc, kc, vc = _proj(x, mod, p['n1'], p['w_in'], p['gains'], ones384, rope_tabs, seq, True,
                                        BF16)
    oa = _na_attn(qa, ka, va, kx_na, vx_na, p['bias'], seq, past)
    oc = _swa_attn(p['sink'], qc, kc, vc, kx_swa, vx_swa, seq, past)
    y, _ = _mix_and_ffn(x, mod, p, oa, oc, urw, s0_bd, scan_consts, seq, True, 1)
    return y


def _heads_first(z, nreq, seq, heads):
    return z.reshape(nreq, seq, heads, HEAD_DIM).transpose(0, 2, 1, 3)


def _tokens_first(z):
    b, h, n, dh = z.shape
    return z.transpose(0, 2, 1, 3).reshape(b * n, h * dh)


def kernel(x_prompt, x_sample, cache_na_k, cache_na_v, cache_swa_k, cache_swa_v, state_rwkv, c, c_ctx, ada_w, ada_b, norm1_g, norm2_g, w_in, na_q_norm, na_k_norm, na_rpb, rw_mu, rw_w0, rw_w2, rw_a0, rw_a2, rw_g2, rw_k_k, rw_k_a, rw_r_k, rw_ln_g, rw_ln_b, swa_q_norm, swa_k_norm, swa_sink, w_out, w_router, w_gate, w_up, w_down):
    nb, seq, d = x_prompt.shape
    db, dseq, _ = x_sample.shape
    depth = ada_w.shape[0]
    past = cache_na_k.shape[3]
    cond = jnp.concatenate([c, c_ctx[None], jnp.zeros((16 - db - 1, d), F32)], axis=0)
    mod_all = _adaln(cond, ada_w, ada_b).reshape(depth, 16, 6, d)
    ones384 = _block_ones(NA_W)
    scan_consts = _rw_masks()
    rope_tabs = _rope_tables(dseq)
    xp = x_prompt.reshape(nb * seq, d)
    xs = x_sample.reshape(db * dseq, d)
    new_ka, new_va, new_kc, new_vc, new_s = [], [], [], [], []
    for l in range(depth):
        p = _layer_params(l, ada_w, ada_b, norm1_g, norm2_g, w_in, na_q_norm, na_k_norm, na_rpb, rw_mu, rw_w0,
                          rw_w2, rw_a0, rw_a2, rw_g2, rw_k_k, rw_k_a, rw_r_k, rw_ln_g, rw_ln_b, swa_q_norm,
                          swa_k_norm, swa_sink, w_out, w_router, w_gate, w_up, w_down)
        mod_ctx = mod_all[l, db:db + 1]
        mod_lat = mod_all[l, 0:db]
        xp, ka, va, kc, vc, sfin = _context_layer(xp, mod_ctx, p, ones384, scan_consts, seq)
        new_ka.append(_heads_first(ka, nb, seq, NA_HEADS))
        new_va.append(_heads_first(va, nb, seq, NA_HEADS))
        new_kc.append(_heads_first(kc, nb, seq, SWA_KV_HEADS))
        new_vc.append(_heads_first(vc, nb, seq, SWA_KV_HEADS))
        new_s.append(_blockdiag_to_state(sfin))
        xs = _latent_layer(xs, mod_lat, p, ones384, scan_consts, rope_tabs, dseq,
                           _tokens_first(cache_na_k[:, l]), _tokens_first(cache_na_v[:, l]),
                           _tokens_first(cache_swa_k[:, l]), _tokens_first(cache_swa_v[:, l]),
                           _state_to_blockdiag(state_rwkv[:, l]), past)
    return (xp.reshape(nb, seq, d), xs.reshape(db, dseq, d), jnp.stack(new_ka, axis=1),
            jnp.stack(new_va, axis=1), jnp.stack(new_kc, axis=1), jnp.stack(new_vc, axis=1),
            jnp.stack(new_s, axis=1))
```

```python
import functools

import numpy as np
import jax
import jax.numpy as jnp
from jax import lax
from jax.experimental import pallas as pl
from jax.experimental.pallas import tpu as pltpu

F32 = jnp.float32
BF16 = jnp.bfloat16

HEAD_DIM = 64
GRID_W = 64
NA_HEADS = 6
NA_KH = 8
NA_KW = 16
RW_HEADS = 4
SWA_HEADS = 6
SWA_KV_HEADS = 2
SWA_WINDOW = 128
N_EXPERTS = 16
EC_CAPACITY = 2
ROPE_THETA = 10000.0
NORM_EPS = 1e-6
GN_EPS = 64e-5
NEG_INF = -1e30
RW_CHUNK = 64
RW_W = RW_HEADS * HEAD_DIM
NA_W = NA_HEADS * HEAD_DIM
SWA_W = SWA_HEADS * HEAD_DIM
SWA_KV_W = SWA_KV_HEADS * HEAD_DIM
RW_IN_W = 1152
VMEM_LIMIT = 56 * 1024 * 1024


def _cp(*sem):
    return pltpu.CompilerParams(dimension_semantics=sem, vmem_limit_bytes=VMEM_LIMIT)


def _bf(x):
    return x.astype(BF16)


def _dot(a, b):
    return jnp.dot(a, b, preferred_element_type=F32)


def _dot_nt(a, b):
    return lax.dot_general(a, b, (((1,), (1,)), ((), ())), preferred_element_type=F32)


def _dot_tn(a, b):
    return lax.dot_general(a, b, (((0,), (0,)), ((), ())), preferred_element_type=F32)


def _split2(x):
    hi = x.astype(BF16)
    lo = (x - hi.astype(F32)).astype(BF16)
    return hi, lo


def _split3(x):
    hi = x.astype(BF16)
    r1 = x - hi.astype(F32)
    mid = r1.astype(BF16)
    lo = (r1 - mid.astype(F32)).astype(BF16)
    return hi, mid, lo


def _dot2(a, b_bf):
    hi, lo = _split2(a)
    return _dot(hi, b_bf) + _dot(lo, b_bf)


def _sigmoid(x):
    return 1.0 / (1.0 + jnp.exp(-x))


def _block_ones(width):
    i = np.arange(width) // HEAD_DIM
    return jnp.asarray((i[:, None] == i[None, :]).astype(np.float32), dtype=BF16)


def _full(shape):
    return pl.BlockSpec(shape, lambda *_: (0,) * len(shape))


def _adaln_kernel(c_ref, w_ref, b_ref, o_ref):
    c = c_ref[...]
    s = c * _sigmoid(c)
    shi, slo = _split2(s)
    whi, wlo = _split2(w_ref[0])
    o_ref[0] = _dot(shi, whi) + _dot(slo, whi) + _dot(shi, wlo) + b_ref[0]


def _adaln(cond, ada_w, ada_b):
    nl, d, n6 = ada_w.shape
    tn = 1536
    rows = cond.shape[0]
    return pl.pallas_call(
        _adaln_kernel,
        grid=(nl, n6 // tn),
        in_specs=[pl.BlockSpec((rows, d), lambda l, j: (0, 0)),
                  pl.BlockSpec((1, d, tn), lambda l, j: (l, 0, j)),
                  pl.BlockSpec((1, 1, tn), lambda l, j: (l, 0, j))],
        out_specs=pl.BlockSpec((1, rows, tn), lambda l, j: (l, 0, j)),
        out_shape=jax.ShapeDtypeStruct((nl, rows, n6), F32),
        compiler_params=_cp("parallel", "parallel"),
        name="adaln",
    )(cond, ada_w, ada_b.reshape(nl, 1, n6))


def _head_norm(z, gain, ones_bf):
    ms = _dot2(z * z, ones_bf) * (1.0 / HEAD_DIM)
    return z * lax.rsqrt(ms + NORM_EPS) * gain


def _rope(z, cos, sin_signed):
    w = z.shape[1]
    lane = lax.broadcasted_iota(jnp.int32, z.shape, 1)
    first = (lane % HEAD_DIM) < (HEAD_DIM // 2)
    swapped = jnp.where(first, pltpu.roll(z, w - HEAD_DIM // 2, 1), pltpu.roll(z, HEAD_DIM // 2, 1))
    return z * cos + swapped * sin_signed


def _proj_kernel(*refs, rope):
    if rope:
        (x_ref, mod_ref, n1_ref, w_ref, gqa_ref, gka_ref, gqc_ref, gkc_ref, ones_ref, cos_ref, sin_ref,
         qa_ref, ka_ref, va_ref, urw_ref, qc_ref, kc_ref, vc_ref) = refs
    else:
        (x_ref, mod_ref, n1_ref, w_ref, gqa_ref, gka_ref, gqc_ref, gkc_ref, ones_ref,
         qa_ref, ka_ref, va_ref, urw_ref, qc_ref, kc_ref, vc_ref) = refs
    x = x_ref[...]
    sh1 = mod_ref[0, 0:1, :]
    sc1 = mod_ref[0, 1:2, :]
    ms = jnp.mean(x * x, axis=-1, keepdims=True)
    h = x * lax.rsqrt(ms + NORM_EPS) * n1_ref[...] * (1.0 + sc1) + sh1
    u = _dot(_bf(h), w_ref[...])
    o0 = 0
    o1 = NA_W
    o2 = 2 * NA_W
    o3 = 3 * NA_W
    o4 = o3 + RW_IN_W
    o5 = o4 + SWA_W
    o6 = o5 + SWA_KV_W
    ones = ones_ref[...]
    ones_kv = ones_ref[0:SWA_KV_W, 0:SWA_KV_W]
    qa = _head_norm(u[:, o0:o1], gqa_ref[...], ones)
    ka = _head_norm(u[:, o1:o2], gka_ref[...], ones)
    qc = _head_norm(u[:, o4:o5], gqc_ref[...], ones)
    kc = _head_norm(u[:, o5:o6], gkc_ref[...], ones_kv)
    if rope:
        qc = _rope(qc, cos_ref[...], sin_ref[...])
        kc = _rope(kc, cos_ref[:, 0:SWA_KV_W], sin_ref[:, 0:SWA_KV_W])
    qa_ref[...] = qa.astype(qa_ref.dtype)
    ka_ref[...] = ka.astype(ka_ref.dtype)
    va_ref[...] = u[:, o2:o3].astype(va_ref.dtype)
    urw_ref[...] = u[:, o3:o4]
    qc_ref[...] = qc.astype(qc_ref.dtype)
    kc_ref[...] = kc.astype(kc_ref.dtype)
    vc_ref[...] = u[:, o6:].astype(vc_ref.dtype)


def _proj(x, mod, n1, w_in_bf, gains, ones384, rope_tabs, seq, per_request_mod, qkv_dtype):
    tokens, d = x.shape
    tm = 256
    tiles_per_req = seq // tm
    in_w = w_in_bf.shape[1]
    rope = rope_tabs is not None
    mod_map = (lambda i: (i // tiles_per_req, 0, 0)) if per_request_mod else (lambda i: (0, 0, 0))
    row = lambda w: pl.BlockSpec((tm, w), lambda i: (i, 0))
    in_specs = [row(d), pl.BlockSpec((1, 6, d), mod_map), _full((1, d)), _full((d, in_w)),
                _full((1, NA_W)), _full((1, NA_W)), _full((1, SWA_W)), _full((1, SWA_KV_W)),
                _full((NA_W, NA_W))]
    args = [x, mod, n1, w_in_bf, *gains, ones384]
    if rope:
        tab = pl.BlockSpec((tm, SWA_W), lambda i: (i % tiles_per_req, 0))
        in_specs += [tab, tab]
        args += list(rope_tabs)
    widths = [NA_W, NA_W, NA_W, RW_IN_W, SWA_W, SWA_KV_W, SWA_KV_W]
    dtypes = [qkv_dtype, qkv_dtype, qkv_dtype, F32, qkv_dtype, qkv_dtype, qkv_dtype]
    return pl.pallas_call(
        functools.partial(_proj_kernel, rope=rope),
        grid=(tokens // tm,),
        in_specs=in_specs,
        out_specs=[row(w) for w in widths],
        out_shape=[jax.ShapeDtypeStruct((tokens, w), dt) for w, dt in zip(widths, dtypes)],
        compiler_params=_cp("parallel"),
        name="proj",
    )(*args)


def _half_masks(width=2 * HEAD_DIM):
    lane = lax.broadcasted_iota(jnp.int32, (1, width), 1)
    return lane < HEAD_DIM, lane >= HEAD_DIM


def _swap_halves(z):
    return pltpu.roll(z, HEAD_DIM, 1)


def _ctx_attn_kernel(sink_ref, qa_ref, ka_ref, va_ref, qc_ref, kc_ref, vc_ref, oa_ref, oc_ref):
    scale = HEAD_DIM ** -0.5
    m0, m1 = _half_masks()
    masks = (m0, m1)
    for pair in range(NA_HEADS // 2):
        sl = slice(pair * 128, (pair + 1) * 128)
        qp = qa_ref[:, sl].astype(F32) * scale
        kp = _bf(ka_ref[:, sl])
        vp = _bf(va_ref[:, sl])
        outs = []
        for half in range(2):
            qm = _bf(jnp.where(masks[half], qp, 0.0))
            s = _dot_nt(qm, kp)
            m = jnp.max(s, axis=-1, keepdims=True)
            e = jnp.exp(s - m)
            l = jnp.sum(e, axis=-1, keepdims=True)
            outs.append(_dot(_bf(e), vp) / l)
        oa_ref[:, sl] = jnp.where(m0, outs[0], outs[1])
    kc = _bf(kc_ref[...])
    vc = _bf(vc_ref[...])
    group = SWA_HEADS // SWA_KV_HEADS
    for pair in range(SWA_HEADS // 2):
        sl = slice(pair * 128, (pair + 1) * 128)
        qp = qc_ref[:, sl].astype(F32) * scale
        outs = []
        for half in range(2):
            h = 2 * pair + half
            g = h // group
            qh = qp if g == half else _swap_halves(qp)
            qm = _bf(jnp.where(masks[g], qh, 0.0))
            s = _dot_nt(qm, kc)
            sk = sink_ref[h]
            m = jnp.maximum(jnp.max(s, axis=-1, keepdims=True), sk)
            e = jnp.exp(s - m)
            l = jnp.sum(e, axis=-1, keepdims=True) + jnp.exp(sk - m)
            o = _dot(_bf(e), vc) / l
            outs.append(o if g == half else _swap_halves(o))
        oc_ref[:, sl] = jnp.where(m0, outs[0], outs[1])


def _ctx_attn(sink, qa, ka, va, qc, kc, vc, seq):
    tokens = qa.shape[0]
    blk = lambda w: pl.BlockSpec((seq, w), lambda b: (b, 0))
    return pl.pallas_call(
        _ctx_attn_kernel,
        grid=(tokens // seq,),
        in_specs=[pl.BlockSpec(memory_space=pltpu.SMEM), blk(NA_W), blk(NA_W), blk(NA_W), blk(SWA_W),
                  blk(SWA_KV_W), blk(SWA_KV_W)],
        out_specs=[blk(NA_W), blk(SWA_W)],
        out_shape=[jax.ShapeDtypeStruct((tokens, NA_W), F32), jax.ShapeDtypeStruct((tokens, SWA_W), F32)],
        compiler_params=_cp("parallel"),
        name="ctx_attn",
    )(sink, qa, ka, va, qc, kc, vc)


def _na_bias_kernel(rpb_ref, o_ref):
    h = pl.program_id(0)
    nrow = 2 * NA_KH - 1
    ncol = 2 * NA_KW - 1
    width = NA_KH * GRID_W
    shape = (GRID_W, width)
    lane = lax.broadcasted_iota(jnp.int32, shape, 1)
    qc = lax.broadcasted_iota(jnp.int32, shape, 0)
    kc = lane % GRID_W
    c_start = jnp.clip(qc - NA_KW // 2, 0, GRID_W - NA_KW)
    ok = (kc >= c_start) & (kc < c_start + NA_KW)
    d_col = jnp.clip(kc - qc, 1 - NA_KW, NA_KW - 1) + NA_KW - 1
    key_row = lax.broadcasted_iota(jnp.int32, (1, width), 1) // GRID_W

    def case_body(case, carry):
        acc = jnp.zeros(shape, F32)
        for dc in range(ncol):
            val = jnp.zeros((1, width), F32)
            for i in range(NA_KH):
                val = jnp.where(key_row == i, rpb_ref[(h * nrow + case + i) * ncol + dc], val)
            acc = jnp.where(d_col == dc, val, acc)
        o_ref[0, pl.ds(case, 1)] = jnp.where(ok, acc, NEG_INF)[None]
        return carry

    lax.fori_loop(0, NA_KH, case_body, 0)


def _na_bias_table(rpb):
    nh = rpb.shape[0]
    return pl.pallas_call(
        _na_bias_kernel,
        grid=(nh,),
        in_specs=[pl.BlockSpec(memory_space=pltpu.SMEM)],
        out_specs=pl.BlockSpec((1, NA_KH, GRID_W, NA_KH * GRID_W), lambda h: (h // 2, 0, h % 2, 0)),
        out_shape=jax.ShapeDtypeStruct((nh // 2, NA_KH, 2 * GRID_W, NA_KH * GRID_W), F32),
        compiler_params=_cp("parallel"),
        name="na_bias",
    )(rpb.reshape(-1))


def _na_kernel(q_ref, k_ref, v_ref, kx_ref, vx_ref, bias_ref, o_ref, *, rows):
    scale = HEAD_DIM ** -0.5
    m0, m1 = _half_masks()
    kx = _bf(kx_ref[...])
    vx = _bf(vx_ref[...])

    def body(r, carry):
        rs = jnp.clip(r - NA_KH // 2, 0, rows - NA_KH)
        case = rs - r + NA_KH - 1
        q0 = pl.multiple_of(r * GRID_W, GRID_W)
        k0 = pl.multiple_of(rs * GRID_W, GRID_W)
        qp = q_ref[pl.ds(q0, GRID_W), :].astype(F32) * scale
        kw = _bf(k_ref[pl.ds(k0, NA_KH * GRID_W), :])
        vw = _bf(v_ref[pl.ds(k0, NA_KH * GRID_W), :])
        q2 = _bf(jnp.concatenate([jnp.where(m0, qp, 0.0), jnp.where(m1, qp, 0.0)], axis=0))
        sw = _dot_nt(q2, kw) + bias_ref[0, pl.ds(case, 1)][0]
        sx = _dot_nt(q2, kx)
        m = jnp.maximum(jnp.max(sw, axis=-1, keepdims=True), jnp.max(sx, axis=-1, keepdims=True))
        ew = jnp.exp(sw - m)
        ex = jnp.exp(sx - m)
        l = jnp.sum(ew, axis=-1, keepdims=True) + jnp.sum(ex, axis=-1, keepdims=True)
        o = (_dot(_bf(ew), vw) + _dot(_bf(ex), vx)) / l
        o_ref[pl.ds(q0, GRID_W), :] = jnp.where(m0, o[0:GRID_W], o[GRID_W:])
        return carry

    lax.fori_loop(0, rows, body, 0)


def _na_attn(q, k, v, kx, vx, bias, seq, past):
    tokens = q.shape[0]
    nb = tokens // seq
    rows = seq // GRID_W
    blk = pl.BlockSpec((seq, 128), lambda b, p: (b, p))
    cblk = pl.BlockSpec((past, 128), lambda b, p: (b, p))
    return pl.pallas_call(
        functools.partial(_na_kernel, rows=rows),
        grid=(nb, NA_HEADS // 2),
        in_specs=[blk, blk, blk, cblk, cblk,
                  pl.BlockSpec((1, NA_KH, 2 * GRID_W, NA_KH * GRID_W), lambda b, p: (p, 0, 0, 0))],
        out_specs=blk,
        out_shape=jax.ShapeDtypeStruct((tokens, NA_W), F32),
        compiler_params=_cp("parallel", "parallel"),
        name="na_attn",
    )(q, k, v, kx, vx, bias)


def _swa_kernel(sink_ref, q_ref, k_ref, v_ref, kx_ref, vx_ref, o_ref, *, seq):
    scale = HEAD_DIM ** -0.5
    blk = SWA_WINDOW
    m0, m1 = _half_masks()
    masks = (m0, m1)
    kx = _bf(kx_ref[...])
    vx = _bf(vx_ref[...])
    group = SWA_HEADS // SWA_KV_HEADS

    def body(nb, carry):
        ks = jnp.clip((nb - 1) * blk, 0, seq - 3 * blk)
        ks = pl.multiple_of(ks, blk)
        q0 = pl.multiple_of(nb * blk, blk)
        kw = _bf(k_ref[pl.ds(ks, 3 * blk), :])
        vw = _bf(v_ref[pl.ds(ks, 3 * blk), :])
        qpos = q0 + lax.broadcasted_iota(jnp.int32, (group * blk, 1), 0) % blk
        kpos = ks + lax.broadcasted_iota(jnp.int32, (1, 3 * blk), 1)
        ok = jnp.abs(qpos - kpos) <= SWA_WINDOW
        pairs = [q_ref[pl.ds(q0, blk), p * 128:(p + 1) * 128].astype(F32) * scale for p in range(SWA_HEADS // 2)]
        head_out = []
        for g in range(SWA_KV_HEADS):
            qs, sinks = [], []
            for h in range(g * group, (g + 1) * group):
                qh = pairs[h // 2] if h % 2 == g else _swap_halves(pairs[h // 2])
                qs.append(jnp.where(masks[g], qh, 0.0))
                sinks.append(jnp.full((blk, 1), sink_ref[h], F32))
            qg = _bf(jnp.concatenate(qs, axis=0))
            sk = jnp.concatenate(sinks, axis=0)
            sw = jnp.where(ok, _dot_nt(qg, kw), NEG_INF)
            sx = _dot_nt(qg, kx)
            m = jnp.maximum(jnp.maximum(jnp.max(sw, axis=-1, keepdims=True),
                                        jnp.max(sx, axis=-1, keepdims=True)), sk)
            ew = jnp.exp(sw - m)
            ex = jnp.exp(sx - m)
            l = jnp.sum(ew, axis=-1, keepdims=True) + jnp.sum(ex, axis=-1, keepdims=True) + jnp.exp(sk - m)
            o = (_dot(_bf(ew), vw) + _dot(_bf(ex), vx)) / l
            for i in range(group):
                h = g * group + i
                oh = o[i * blk:(i + 1) * blk]
                head_out.append(oh if h % 2 == g else _swap_halves(oh))
        for p in range(SWA_HEADS // 2):
            o_ref[pl.ds(q0, blk), p * 128:(p + 1) * 128] = jnp.where(m0, head_out[2 * p], head_out[2 * p + 1])
        return carry

    lax.fori_loop(0, seq // blk, body, 0)


def _swa_attn(sink, q, k, v, kx, vx, seq, past):
    tokens = q.shape[0]
    blk = lambda w: pl.BlockSpec((seq, w), lambda b: (b, 0))
    cblk = pl.BlockSpec((past, SWA_KV_W), lambda b: (b, 0))
    return pl.pallas_call(
        functools.partial(_swa_kernel, seq=seq),
        grid=(tokens // seq,),
        in_specs=[pl.BlockSpec(memory_space=pltpu.SMEM), blk(SWA_W), blk(SWA_KV_W), blk(SWA_KV_W), cblk, cblk],
        out_specs=blk(SWA_W),
        out_shape=jax.ShapeDtypeStruct((tokens, SWA_W), F32),
        compiler_params=_cp("parallel"),
        name="swa_attn",
    )(sink, q, k, v, kx, vx)


def _rw_pre_kernel(u_ref, up_ref, un_ref, mu_ref, w0_ref, w2_ref, a0_ref, a2_ref, g2_ref, kk_ref_, ka_ref_,
                   rk_ref, ones_ref, r_o, kk_o, v_o, ld_o, kka_o, kd_o, g_o, bonus_o, *, tiles_per_req):
    i = pl.program_id(0)
    u = u_ref[...]
    tm = u.shape[0]
    rowi = lax.broadcasted_iota(jnp.int32, (tm, 1), 0)
    first = (i % tiles_per_req) == 0
    last = (i % tiles_per_req) == tiles_per_req - 1
    prev_row = jnp.where(first, 0.0, up_ref[7:8, :])
    next_row = jnp.where(last, 0.0, un_ref[0:1, :])
    prev = jnp.where(rowi == 0, prev_row, pltpu.roll(u, 1, 0))
    nxt = jnp.where(rowi == tm - 1, next_row, pltpu.roll(u, tm - 1, 0))
    us = u + mu_ref[0:1, :] * (prev - u) + mu_ref[1:2, :] * (nxt - u)
    r = us[:, 0:RW_W]
    k = us[:, RW_W:2 * RW_W]
    v = us[:, 2 * RW_W:3 * RW_W]
    wl = us[:, 3 * RW_W:3 * RW_W + 128]
    al = us[:, 3 * RW_W + 128:3 * RW_W + 256]
    gl = us[:, 3 * RW_W + 256:3 * RW_W + 384]
    z = -(w0_ref[...] + _dot(_bf(jnp.tanh(wl)), w2_ref[...]))
    softplus = jnp.maximum(z, 0.0) + jnp.log(1.0 + jnp.exp(-jnp.abs(z)))
    w = -softplus - 0.5
    ld = -jnp.exp(w)
    a = _sigmoid(a0_ref[...] + _dot(_bf(al), a2_ref[...]))
    g = _dot(_bf(_sigmoid(gl)), g2_ref[...])
    ones = ones_ref[...]
    kkr = k * kk_ref_[...]
    kk = kkr * lax.rsqrt(jnp.maximum(_dot2(kkr * kkr, ones), 1e-24))
    k_a = ka_ref_[...]
    kd_f = k * (1.0 + (a[:, 0:RW_W] - 1.0) * k_a)
    kd_b = k * (1.0 + (a[:, RW_W:] - 1.0) * k_a)
    r_o[...] = r
    kk_o[...] = kk
    v_o[...] = v
    ld_o[...] = ld
    kka_o[:, 0:RW_W] = kk * a[:, 0:RW_W]
    kka_o[:, RW_W:] = kk * a[:, RW_W:]
    kd_o[:, 0:RW_W] = kd_f
    kd_o[:, RW_W:] = kd_b
    g_o[...] = g
    bonus_o[...] = _dot2(r * (kd_f + kd_b) * rk_ref[...], ones) * v


def _rw_pre(urw, p, seq):
    tokens = urw.shape[0]
    tm = 256
    tpr = seq // tm
    nt = tokens // tm
    r8 = tm // 8
    row = lambda w: pl.BlockSpec((tm, w), lambda i: (i, 0))
    in_specs = [row(RW_IN_W),
                pl.BlockSpec((8, RW_IN_W), lambda i: (jnp.maximum(i * r8 - 1, 0), 0)),
                pl.BlockSpec((8, RW_IN_W), lambda i: (jnp.minimum((i + 1) * r8, nt * r8 - 1), 0)),
                _full((2, RW_IN_W)), _full((1, 2 * RW_W)), _full((128, 2 * RW_W)), _full((1, 2 * RW_W)),
                _full((128, 2 * RW_W)), _full((128, RW_W)), _full((1, RW_W)), _full((1, RW_W)),
                _full((1, RW_W)), _full((RW_W, RW_W))]
    widths = [RW_W, RW_W, RW_W, 2 * RW_W, 2 * RW_W, 2 * RW_W, RW_W, RW_W]
    return pl.pallas_call(
        functools.partial(_rw_pre_kernel, tiles_per_req=tpr),
        grid=(nt,),
        in_specs=in_specs,
        out_specs=[row(w) for w in widths],
        out_shape=[jax.ShapeDtypeStruct((tokens, w), F32) for w in widths],
        compiler_params=_cp("parallel"),
        name="rw_pre",
    )(urw, urw, urw, p['mu'], p['w0'], p['w2'], p['a0'], p['a2'], p['g2'], p['k_k'], p['k_a'], p['r_k'],
      p['ones256'])


def _rw_masks():
    t = RW_CHUNK
    n = RW_HEADS * t
    tt = np.arange(t)[:, None]
    ss = (np.arange(n) % t)[None, :]
    before = np.stack([ss < tt, ss > tt])
    diag = (ss == tt)
    strict = before.astype(np.float32)
    incl = (before | diag[None]).astype(np.float32)
    eye = diag.astype(np.float32)
    ti = np.arange(t)
    tri = np.stack([ti[None, :] <= ti[:, None], ti[None, :] >= ti[:, None]]).astype(np.float32)
    hd = np.arange(n) // t
    same = (hd[:, None] == hd[None, :]).astype(np.float32)
    return (jnp.asarray(strict), jnp.asarray(incl), jnp.asarray(tri, dtype=BF16), jnp.asarray(same),
            jnp.asarray(eye))


def _rw_scan_kernel(r_ref, kk_ref, v_ref, ld_ref, kka_ref, kd_ref, s0_ref, strict_ref, incl_ref, tri_ref,
                    same_ref, eye_ref, o_ref, sfin_ref, s_scr, *, nsub):
    d = pl.program_id(1)
    c = pl.program_id(2)
    t = RW_CHUNK
    n = RW_HEADS * t

    @pl.when(c == 0)
    def _():
        s_scr[...] = s0_ref[0, 0]

    strict = strict_ref[0]
    incl = incl_ref[0]
    tri = tri_ref[0]
    eye = eye_ref[...]
    same = same_ref[...]

    def bd(x):
        return _bf(jnp.concatenate([x] * RW_HEADS, axis=0) * same)

    js = range(nsub)
    rows = [pl.ds(pl.multiple_of((j + d * (nsub - 1 - 2 * j)) * t, t), t) for j in js]
    ld = [ld_ref[rows[j], :] for j in js]
    cum = []
    for j in js:
        lhi, lmid, llo = _split3(ld[j])
        cum.append(_dot(tri, lhi) + _dot(tri, lmid) + _dot(tri, llo))
    cend = [jnp.sum(ld[j], axis=0, keepdims=True) for j in js]
    kka = [kka_ref[rows[j], :] for j in js]
    kd = [kd_ref[rows[j], :] for j in js]
    v = [v_ref[rows[j], :] for j in js]
    at = [-kk_ref[rows[j], :] * jnp.exp(cum[j] - ld[j]) for j in js]
    rt = [r_ref[rows[j], :] * jnp.exp(cum[j]) for j in js]
    e_inv = [jnp.exp(-cum[j]) for j in js]
    aa = [_dot_nt(_bf(jnp.concatenate([at[j], rt[j]], axis=0)),
                  jnp.concatenate([bd(kka[j] * e_inv[j]), bd(kd[j] * e_inv[j])], axis=0)) for j in js]
    a_ab = [aa[j][0:t, 0:n] * strict for j in js]
    x = [eye + a_ab[j] for j in js]
    pw = a_ab
    for _ in range(5):
        pw = [_dot(_bf(pw[j]), bd(pw[j])) for j in js]
        x = [x[j] + _dot(_bf(x[j]), bd(pw[j])) for j in js]
    v_bd = [bd(v[j]) for j in js]
    wv = [_dot(_bf(aa[j][0:t, n:] * strict), v_bd[j]) for j in js]
    mu = [_dot(_bf(x[j]), jnp.concatenate([bd(at[j]), bd(wv[j])], axis=1)) for j in js]
    m1 = [mu[j][:, 0:n] for j in js]
    u0 = [mu[j][:, n:] for j in js]
    e_end = [jnp.exp(cend[j] - cum[j]) for j in js]
    bend = [_bf(kka[j] * e_end[j]) for j in js]
    g = [_bf(_dot_tn(_bf(m1[j]), bend[j]) * same) for j in js]
    cst = [_dot_tn(_bf(jnp.concatenate([u0[j], v[j]], axis=0)),
                   jnp.concatenate([bend[j], _bf(kd[j] * e_end[j])], axis=0)) * same for j in js]
    qo = [_dot(_bf(aa[j][t:, 0:n] * incl), jnp.concatenate([bd(m1[j]), bd(u0[j])], axis=1)) for j in js]
    q = [_bf(rt[j] + qo[j][:, 0:n]) for j in js]
    o0 = [qo[j][:, n:] + _dot(_bf(aa[j][t:, n:] * incl), v_bd[j]) for j in js]

    s = s_scr[...]
    for j in js:
        s_bf = _bf(s)
        o_ref[0, rows[j], :] = _dot_nt(q[j], s_bf) + o0[j]
        s = s * jnp.exp(cend[j]) + _dot(s_bf, g[j]) + cst[j]
    s_scr[...] = s

    @pl.when(c == pl.num_programs(2) - 1)
    def _():
        sfin_ref[0, 0] = s


def _rw_scan(r, kk, v, ld, kka, kd, s0_bd, consts, seq):
    tokens = r.shape[0]
    nreq = tokens // seq
    tb = 256
    nblk = seq // tb
    nsub = tb // RW_CHUNK
    n = RW_HEADS * RW_CHUNK
    cc = lambda d, c: c + d * (nblk - 1 - 2 * c)
    shared = pl.BlockSpec((tb, RW_W), lambda b, d, c: (b * nblk + cc(d, c), 0))
    dirw = pl.BlockSpec((tb, RW_W), lambda b, d, c: (b * nblk + cc(d, c), d))
    strict, incl, tri, same, eye = consts
    return pl.pallas_call(
        functools.partial(_rw_scan_kernel, nsub=nsub),
        grid=(nreq, 2, nblk),
        in_specs=[shared, shared, shared, dirw, dirw, dirw,
                  pl.BlockSpec((1, 1, n, n), lambda b, d, c: (b, d, 0, 0)),
                  pl.BlockSpec((1, RW_CHUNK, n), lambda b, d, c: (d, 0, 0)),
                  pl.BlockSpec((1, RW_CHUNK, n), lambda b, d, c: (d, 0, 0)),
                  pl.BlockSpec((1, RW_CHUNK, RW_CHUNK), lambda b, d, c: (d, 0, 0)),
                  _full((n, n)), _full((RW_CHUNK, n))],
        out_specs=[pl.BlockSpec((1, tb, RW_W), lambda b, d, c: (d, b * nblk + cc(d, c), 0)),
                   pl.BlockSpec((1, 1, n, n), lambda b, d, c: (b, d, 0, 0))],
        out_shape=[jax.ShapeDtypeStruct((2, tokens, RW_W), F32), jax.ShapeDtypeStruct((nreq, 2, n, n), F32)],
        scratch_shapes=[pltpu.VMEM((n, n), F32)],
        compiler_params=_cp("parallel", "parallel", "arbitrary"),
        name="rw_scan",
    )(r, kk, v, ld, kka, kd, s0_bd, strict, incl, tri, same, eye)


def _state_to_blockdiag(s):
    b = s.shape[0]
    eye = jnp.eye(RW_HEADS, dtype=s.dtype)
    out = s[:, :, :, :, None, :] * eye[None, None, :, None, :, None]
    return out.reshape(b, 2, RW_W, RW_W)


def _blockdiag_to_state(sbd):
    b = sbd.shape[0]
    s = sbd.reshape(b, 2, RW_HEADS, HEAD_DIM, RW_HEADS, HEAD_DIM)
    idx = jnp.arange(RW_HEADS)
    return jnp.transpose(s[:, :, idx, :, idx, :], (1, 2, 0, 3, 4))


def _finish_kernel(x_ref, oa_ref, oc_ref, o2_ref, bonus_ref, g_ref, mod_ref, wout_ref, lng_ref, lnb_ref,
                   n2_ref, wr_hi_ref, wr_lo_ref, ones_ref, x1_ref, h2_ref, aff_ref):
    ones = ones_ref[...]
    y = o2_ref[0] + o2_ref[1]
    mu = _dot2(y, ones) * (1.0 / HEAD_DIM)
    yc = y - mu
    var = _dot2(yc * yc, ones) * (1.0 / HEAD_DIM)
    yn = yc * lax.rsqrt(var + GN_EPS) * lng_ref[...] + lnb_ref[...]
    ob = (yn + bonus_ref[...]) * g_ref[...]
    mixin = jnp.concatenate([_bf(oa_ref[...]), _bf(ob), _bf(oc_ref[...])], axis=1)
    mix = _dot(mixin, wout_ref[...])
    g1 = mod_ref[0, 2:3, :]
    sh2 = mod_ref[0, 3:4, :]
    sc2 = mod_ref[0, 4:5, :]
    x1 = x_ref[...] + g1 * mix
    ms = jnp.mean(x1 * x1, axis=-1, keepdims=True)
    h2 = x1 * lax.rsqrt(ms + NORM_EPS) * n2_ref[...] * (1.0 + sc2) + sh2
    x1_ref[...] = x1
    h2_ref[...] = _bf(h2)
    hhi, hlo = _split2(h2)
    logits = _dot(hhi, wr_hi_ref[...]) + _dot(hlo, wr_hi_ref[...]) + _dot(hhi, wr_lo_ref[...])
    m = jnp.max(logits, axis=-1, keepdims=True)
    e = jnp.exp(logits - m)
    aff_ref[...] = e / jnp.sum(e, axis=-1, keepdims=True)


def _finish(x, oa, oc, o2, bonus, g, mod, p, seq, per_request_mod):
    tokens, d = x.shape
    tm = 256
    tpr = seq // tm
    mod_map = (lambda i: (i // tpr, 0, 0)) if per_request_mod else (lambda i: (0, 0, 0))
    row = lambda w: pl.BlockSpec((tm, w), lambda i: (i, 0))
    return pl.pallas_call(
        _finish_kernel,
        grid=(tokens // tm,),
        in_specs=[row(d), row(NA_W), row(SWA_W), pl.BlockSpec((2, tm, RW_W), lambda i: (0, i, 0)), row(RW_W),
                  row(RW_W), pl.BlockSpec((1, 6, d), mod_map), _full((d, d)), _full((1, RW_W)),
                  _full((1, RW_W)), _full((1, d)), _full((d, N_EXPERTS)), _full((d, N_EXPERTS)),
                  _full((RW_W, RW_W))],
        out_specs=[row(d), row(d), row(N_EXPERTS)],
        out_shape=[jax.ShapeDtypeStruct((tokens, d), F32), jax.ShapeDtypeStruct((tokens, d), BF16),
                   jax.ShapeDtypeStruct((tokens, N_EXPERTS), F32)],
        compiler_params=_cp("parallel"),
        name="finish",
    )(x, oa, oc, o2, bonus, g, mod, p['w_out'], p['ln_g'], p['ln_b'], p['n2'], p['wr_hi'], p['wr_lo'],
      p['ones256'])


def _topk_kernel(aff_ref, tri_ref, eye_ref, place_ref, slot_ref, slotrow_ref, gfull_ref, *, cap, group, tb):
    b = pl.program_id(0)
    aff = aff_ref[...]
    seq = aff.shape[0]
    bits = lax.bitcast_convert_type(aff, jnp.int32)
    capf = jnp.float32(cap)

    def bis(_, carry):
        lo, hi = carry
        mid = lo + ((hi - lo + 1) >> 1)
        cnt = jnp.sum(jnp.where(bits >= mid, 1.0, 0.0), axis=0, keepdims=True)
        ge = cnt >= capf
        return jnp.where(ge, mid, lo), jnp.where(ge, hi, mid - 1)

    lo0 = jnp.zeros((1, N_EXPERTS), jnp.int32)
    hi0 = jnp.full((1, N_EXPERTS), 0x7F7FFFFF, jnp.int32)
    thr, _ = lax.fori_loop(0, 31, bis, (lo0, hi0))
    gt = jnp.where(bits > thr, 1.0, 0.0)
    eq = jnp.where(bits == thr, 1.0, 0.0)
    need = capf - jnp.sum(gt, axis=0, keepdims=True)
    offset = ((b % group) * cap).astype(F32)
    tri = tri_ref[...]
    eye = eye_ref[...]
    carry_g = jnp.zeros((1, N_EXPERTS), F32)
    carry_e = jnp.zeros((1, N_EXPERTS), F32)
    ghi, gmid, glo = _split3(aff)
    for blk in range(seq // tb):
        sl = slice(blk * tb, (blk + 1) * tb)
        pg = _dot(tri, _bf(gt[sl])) + carry_g
        pe = _dot(tri, _bf(eq[sl])) + carry_e
        carry_g = pg[tb - 1:tb, :]
        carry_e = pe[tb - 1:tb, :]
        sel = gt[sl] + eq[sl] * jnp.where(pe <= need, 1.0, 0.0)
        slot = jnp.where(sel > 0.5, pg + jnp.minimum(pe, need) - 1.0 + offset, -1.0)
        slot_ref[sl, :] = slot
        shi, slo = _split2(slot)
        slotrow_ref[0, :, 0, sl] = _dot_nt(eye, shi) + _dot_nt(eye, slo)
        gfull_ref[sl, :] = _bf(_dot(ghi[sl], place_ref[0]) + _dot(gmid[sl], place_ref[1])
                               + _dot(glo[sl], place_ref[2]))


def _topk(aff, seq, group):
    tokens = aff.shape[0]
    nreq = tokens // seq
    cap = EC_CAPACITY * seq // N_EXPERTS
    tb = min(seq, 512)
    ti = np.arange(tb)
    tri = jnp.asarray((ti[None, :] <= ti[:, None]).astype(np.float32), dtype=BF16)
    eye = jnp.asarray(np.eye(N_EXPERTS, dtype=np.float32), dtype=BF16)
    place = np.zeros((3, N_EXPERTS, 128), np.float32)
    for s in range(3):
        place[s, np.arange(N_EXPERTS), s * N_EXPERTS + np.arange(N_EXPERTS)] = 1.0
    place = jnp.asarray(place, dtype=BF16)
    return pl.pallas_call(
        functools.partial(_topk_kernel, cap=cap, group=group, tb=tb),
        grid=(nreq,),
        in_specs=[pl.BlockSpec((seq, N_EXPERTS), lambda b: (b, 0)), _full((tb, tb)),
                  _full((N_EXPERTS, N_EXPERTS)), _full((3, N_EXPERTS, 128))],
        out_specs=[pl.BlockSpec((seq, N_EXPERTS), lambda b: (b, 0)),
                   pl.BlockSpec((1, N_EXPERTS, 1, seq), lambda b: (b // group, 0, 0, b % group)),
                   pl.BlockSpec((seq, 128), lambda b: (b, 0))],
        out_shape=[jax.ShapeDtypeStruct((tokens, N_EXPERTS), F32),
                   jax.ShapeDtypeStruct((nreq // group, N_EXPERTS, 1, group * seq), F32),
                   jax.ShapeDtypeStruct((tokens, 128), BF16)],
        compiler_params=_cp("parallel"),
        name="topk",
    )(aff, tri, eye, place)


def _moe_ffn_kernel(h_ref, slotrow_ref, gfull_ref, mod_ref, wg_ref, wu_ref, wd_ref, ye_ref, *, ct, kc):
    e = pl.program_id(1)
    lg = h_ref.shape[0]
    d = h_ref.shape[1]
    jcol = lax.broadcasted_iota(jnp.int32, (ct, 1), 0).astype(F32)
    xe = jnp.zeros((ct, d), F32)
    gs = jnp.zeros((ct, 128), F32)
    for c in range(lg // kc):
        sl = slice(c * kc, (c + 1) * kc)
        onehot = _bf(jnp.where(slotrow_ref[0, 0, :, sl] == jcol, 1.0, 0.0))
        xe = xe + _dot(onehot, h_ref[sl, :])
        gs = gs + _dot(onehot, gfull_ref[sl, :])
    lane = lax.broadcasted_iota(jnp.int32, (1, 128), 1)
    pick = (lane == e) | (lane == e + N_EXPERTS) | (lane == e + 2 * N_EXPERTS)
    gate = jnp.sum(jnp.where(pick, gs, 0.0), axis=-1, keepdims=True)
    xb = _bf(xe)
    hg = _dot(xb, wg_ref[0])
    hu = _dot(xb, wu_ref[0])
    he = _bf(hg * _sigmoid(hg) * hu)
    y = _dot(he, wd_ref[0])
    ye_ref[0, 0] = _bf(y * gate * mod_ref[0, 5:6, :])


def _moe_ffn(h2, slotrow, gfull, mod, wg, wu, wd, lg, ct, per_group_mod):
    tokens, d = h2.shape
    ngrp = tokens // lg
    f = wg.shape[2]
    mod_map = (lambda gi, e: (gi, 0, 0)) if per_group_mod else (lambda gi, e: (0, 0, 0))
    return pl.pallas_call(
        functools.partial(_moe_ffn_kernel, ct=ct, kc=min(lg, 1024)),
        grid=(ngrp, N_EXPERTS),
        in_specs=[pl.BlockSpec((lg, d), lambda gi, e: (gi, 0)),
                  pl.BlockSpec((1, 1, 1, lg), lambda gi, e: (gi, e, 0, 0)),
                  pl.BlockSpec((lg, 128), lambda gi, e: (gi, 0)),
                  pl.BlockSpec((1, 6, d), mod_map),
                  pl.BlockSpec((1, d, f), lambda gi, e: (e, 0, 0)),
                  pl.BlockSpec((1, d, f), lambda gi, e: (e, 0, 0)),
                  pl.BlockSpec((1, f, d), lambda gi, e: (e, 0, 0))],
        out_specs=pl.BlockSpec((1, 1, ct, d), lambda gi, e: (gi, e, 0, 0)),
        out_shape=jax.ShapeDtypeStruct((ngrp, N_EXPERTS, ct, d), BF16),
        compiler_params=_cp("parallel", "arbitrary"),
        name="moe_ffn",
    )(h2, slotrow, gfull, mod, wg, wu, wd)


def _moe_combine_kernel(x1_ref, slot_ref, ye_ref, o_ref, *, ct):
    e = pl.program_id(2)

    @pl.when(e == 0)
    def _():
        o_ref[...] = x1_ref[...]

    slot = slot_ref[...]
    shi, slo = _split2(slot)
    krow = lax.broadcasted_iota(jnp.int32, (N_EXPERTS, 128), 0)
    sel = _bf(jnp.where(krow == e, 1.0, 0.0))
    sb = _dot(shi, sel) + _dot(slo, sel)
    lane = lax.broadcasted_iota(jnp.int32, (1, 128), 1).astype(F32)
    pieces = [_bf(jnp.where(sb == lane + float(j * 128), 1.0, 0.0)) for j in range(ct // 128)]
    onehot_t = jnp.concatenate(pieces, axis=1) if len(pieces) > 1 else pieces[0]
    o_ref[...] += _dot(onehot_t, ye_ref[0, 0])


def _moe_combine(x1, slot, ye, lg, ct):
    tokens, d = x1.shape
    ngrp = tokens // lg
    tb = min(lg, 2048)
    nb = lg // tb
    return pl.pallas_call(
        functools.partial(_moe_combine_kernel, ct=ct),
        grid=(ngrp, nb, N_EXPERTS),
        in_specs=[pl.BlockSpec((tb, d), lambda gi, t, e: (gi * nb + t, 0)),
                  pl.BlockSpec((tb, N_EXPERTS), lambda gi, t, e: (gi * nb + t, 0)),
                  pl.BlockSpec((1, 1, ct, d), lambda gi, t, e: (gi, e, 0, 0))],
        out_specs=pl.BlockSpec((tb, d), lambda gi, t, e: (gi * nb + t, 0)),
        out_shape=jax.ShapeDtypeStruct((tokens, d), F32),
        compiler_params=_cp("parallel", "parallel", "arbitrary"),
        name="moe_combine",
    )(x1, slot, ye)


def _rope_tables(seq):
    t = np.arange(seq)
    n_freq = HEAD_DIM // 4
    inv = ROPE_THETA ** (-np.arange(n_freq, dtype=np.float32) / n_freq)
    ang = np.concatenate([(t // GRID_W).astype(np.float32)[:, None] * inv,
                          (t % GRID_W).astype(np.float32)[:, None] * inv], axis=-1)
    ang = jnp.asarray(ang, dtype=F32)
    cos, sin = jnp.cos(ang), jnp.sin(ang)
    cos_t = jnp.tile(jnp.concatenate([cos, cos], axis=-1), (1, SWA_HEADS))
    sin_t = jnp.tile(jnp.concatenate([-sin, sin], axis=-1), (1, SWA_HEADS))
    return cos_t, sin_t


def _blockdiag2(w):
    z = jnp.zeros_like(w[0])
    return jnp.concatenate([jnp.concatenate([w[0], z], axis=1), jnp.concatenate([z, w[1]], axis=1)], axis=0)


def _layer_params(l, ada_w, ada_b, norm1_g, norm2_g, w_in, na_q_norm, na_k_norm, na_rpb, rw_mu, rw_w0, rw_w2,
                  rw_a0, rw_a2, rw_g2, rw_k_k, rw_k_a, rw_r_k, rw_ln_g, rw_ln_b, swa_q_norm, swa_k_norm,
                  swa_sink, w_out, w_router, w_gate, w_up, w_down):
    wr = w_router[l]
    wr_hi = wr.astype(BF16)
    return {
        'n1': norm1_g[l][None], 'n2': norm2_g[l][None], 'w_in': w_in[l].astype(BF16),
        'gains': (jnp.tile(na_q_norm[l], NA_HEADS)[None], jnp.tile(na_k_norm[l], NA_HEADS)[None],
                  jnp.tile(swa_q_norm[l], SWA_HEADS)[None], jnp.tile(swa_k_norm[l], SWA_KV_HEADS)[None]),
        'bias': _na_bias_table(na_rpb[l]),
        'mu': rw_mu[l], 'w0': rw_w0[l].reshape(1, 2 * RW_W), 'w2': _blockdiag2(rw_w2[l]).astype(BF16),
        'a0': rw_a0[l].reshape(1, 2 * RW_W), 'a2': _blockdiag2(rw_a2[l]).astype(BF16),
        'g2': rw_g2[l].astype(BF16), 'k_k': rw_k_k[l][None], 'k_a': rw_k_a[l][None],
        'r_k': rw_r_k[l].reshape(1, RW_W), 'ln_g': rw_ln_g[l][None], 'ln_b': rw_ln_b[l][None],
        'sink': swa_sink[l], 'w_out': w_out[l].astype(BF16),
        'wr_hi': wr_hi, 'wr_lo': (wr - wr_hi.astype(F32)).astype(BF16),
        'wg': w_gate[l].astype(BF16), 'wu': w_up[l].astype(BF16), 'wd': w_down[l].astype(BF16),
        'ones256': _block_ones(RW_W),
    }


def _mix_and_ffn(x, mod, p, oa, oc, urw, s0_bd, scan_consts, seq, per_request_mod, group):
    r, kk, v, ld, kka, kd, g, bonus = _rw_pre(urw, p, seq)
    o2, sfin = _rw_scan(r, kk, v, ld, kka, kd, s0_bd, scan_consts, seq)
    x1, h2, aff = _finish(x, oa, oc, o2, bonus, g, mod, p, seq, per_request_mod)
    slot, slotrow, gfull = _topk(aff, seq, group)
    cap = EC_CAPACITY * seq // N_EXPERTS
    lg, ct = group * seq, group * cap
    ye = _moe_ffn(h2, slotrow, gfull, mod, p['wg'], p['wu'], p['wd'], lg, ct, per_request_mod)
    return _moe_combine(x1, slot, ye, lg, ct), sfin


def _context_layer(x, mod, p, ones384, scan_consts, seq):
    qa, ka, va, urw, qc, kc, vc = _proj(x, mod, p['n1'], p['w_in'], p['gains'], ones384, None, seq, False, F32)
    oa, oc = _ctx_attn(p['sink'], qa, ka, va, qc, kc, vc, seq)
    nreq = x.shape[0] // seq
    s0 = jnp.zeros((nreq, 2, RW_W, RW_W), F32)
    y, sfin = _mix_and_ffn(x, mod, p, oa, oc, urw, s0, scan_consts, seq, False, 8)
    return y, ka, va, kc, vc, sfin


def _latent_layer(x, mod, p, ones384, scan_consts, rope_tabs, seq, kx_na, vx_na, kx_swa, vx_swa, s0_bd, past):
    qa, ka, va, urw, qc, kc, vc = _proj(x, mod, p['n1'], p['w_in'], p['gains'], ones384, rope_tabs, seq, True,
                                        BF16)
    oa = _na_attn(qa, ka, va, kx_na, vx_na, p['bias'], seq, past)
    oc = _swa_attn(p['sink'], qc, kc, vc, kx_swa, vx_swa, seq, past)
    y, _ = _mix_and_ffn(x, mod, p, oa, oc, urw, s0_bd, scan_consts, seq, True, 1)
    return y


def _heads_first(z, nreq, seq, heads):
    return z.reshape(nreq, seq, heads, HEAD_DIM).transpose(0, 2, 1, 3)


def _tokens_first(z):
    b, h, n, dh = z.shape
    return z.transpose(0, 2, 1, 3).reshape(b * n, h * dh)


def kernel(x_prompt, x_sample, cache_na_k, cache_na_v, cache_swa_k, cache_swa_v, state_rwkv, c, c_ctx, ada_w, ada_b, norm1_g, norm2_g, w_in, na_q_norm, na_k_norm, na_rpb, rw_mu, rw_w0, rw_w2, rw_a0, rw_a2, rw_g2, rw_k_k, rw_k_a, rw_r_k, rw_ln_g, rw_ln_b, swa_q_norm, swa_k_norm, swa_sink, w_out, w_router, w_gate, w_up, w_down):
    nb, seq, d = x_prompt.shape
    db, dseq, _ = x_sample.shape
    depth = ada_w.shape[0]
    past = cache_na_k.shape[3]
    cond = jnp.concatenate([c, c_ctx[None], jnp.zeros((16 - db - 1, d), F32)], axis=0)
    mod_all = _adaln(cond, ada_w, ada_b).reshape(depth, 16, 6, d)
    ones384 = _block_ones(NA_W)
    scan_consts = _rw_masks()
    rope_tabs = _rope_tables(dseq)
    xp = x_prompt.reshape(nb * seq, d)
    xs = x_sample.reshape(db * dseq, d)
    new_ka, new_va, new_kc, new_vc, new_s = [], [], [], [], []
    for l in range(depth):
        p = _layer_params(l, ada_w, ada_b, norm1_g, norm2_g, w_in, na_q_norm, na_k_norm, na_rpb, rw_mu, rw_w0,
                          rw_w2, rw_a0, rw_a2, rw_g2, rw_k_k, rw_k_a, rw_r_k, rw_ln_g, rw_ln_b, swa_q_norm,
                          swa_k_norm, swa_sink, w_out, w_router, w_gate, w_up, w_down)
        mod_ctx = mod_all[l, db:db + 1]
        mod_lat = mod_all[l, 0:db]
        xp, ka, va, kc, vc, sfin = _context_layer(xp, mod_ctx, p, ones384, scan_consts, seq)
        new_ka.append(_heads_first(ka, nb, seq, NA_HEADS))
        new_va.append(_heads_first(va, nb, seq, NA_HEADS))
        new_kc.append(_heads_first(kc, nb, seq, SWA_KV_HEADS))
        new_vc.append(_heads_first(vc, nb, seq, SWA_KV_HEADS))
        new_s.append(_blockdiag_to_state(sfin))
        xs = _latent_layer(xs, mod_lat, p, ones384, scan_consts, rope_tabs, dseq,
                           _tokens_first(cache_na_k[:, l]), _tokens_first(cache_na_v[:, l]),
                           _tokens_first(cache_swa_k[:, l]), _tokens_first(cache_swa_v[:, l]),
                           _state_to_blockdiag(state_rwkv[:, l]), past)
    return (xp.reshape(nb, seq, d), xs.reshape(db, dseq, d), jnp.stack(new_ka, axis=1),
            jnp.stack(new_va, axis=1), jnp.stack(new_kc, axis=1), jnp.stack(new_vc, axis=1),
            jnp.stack(new_s, axis=1))
```

```python
import functools

import numpy as np
import jax
import jax.numpy as jnp
from jax import lax
from jax.experimental import pallas as pl
from jax.experimental.pallas import tpu as pltpu

F32 = jnp.float32
BF16 = jnp.bfloat16

HEAD_DIM = 64
GRID_W = 64
NA_HEADS = 6
NA_KH = 8
NA_KW = 16
RW_HEADS = 4
SWA_HEADS = 6
SWA_KV_HEADS = 2
SWA_WINDOW = 128
N_EXPERTS = 16
EC_CAPACITY = 2
ROPE_THETA = 10000.0
NORM_EPS = 1e-6
GN_EPS = 64e-5
NEG_INF = -1e30
NA_UNROLL = 4
SWA_UNROLL = 2
RW_CHUNK = 64
RW_W = RW_HEADS * HEAD_DIM
NA_W = NA_HEADS * HEAD_DIM
SWA_W = SWA_HEADS * HEAD_DIM
SWA_KV_W = SWA_KV_HEADS * HEAD_DIM
RW_IN_W = 1152
VMEM_LIMIT = 56 * 1024 * 1024


def _cp(*sem):
    return pltpu.CompilerParams(dimension_semantics=sem, vmem_limit_bytes=VMEM_LIMIT)


def _bf(x):
    return x.astype(BF16)


def _dot(a, b):
    return jnp.dot(a, b, preferred_element_type=F32)


def _dot_nt(a, b):
    return lax.dot_general(a, b, (((1,), (1,)), ((), ())), preferred_element_type=F32)


def _dot_tn(a, b):
    return lax.dot_general(a, b, (((0,), (0,)), ((), ())), preferred_element_type=F32)


def _split2(x):
    hi = x.astype(BF16)
    lo = (x - hi.astype(F32)).astype(BF16)
    return hi, lo


def _split3(x):
    hi = x.astype(BF16)
    r1 = x - hi.astype(F32)
    mid = r1.astype(BF16)
    lo = (r1 - mid.astype(F32)).astype(BF16)
    return hi, mid, lo


def _dot2(a, b_bf):
    hi, lo = _split2(a)
    return _dot(hi, b_bf) + _dot(lo, b_bf)


def _sigmoid(x):
    return 1.0 / (1.0 + jnp.exp(-x))


def _block_ones(width):
    i = np.arange(width) // HEAD_DIM
    return jnp.asarray((i[:, None] == i[None, :]).astype(np.float32), dtype=BF16)


def _row_tile(seq):
    return 512 if seq % 512 == 0 else 256


def _full(shape):
    return pl.BlockSpec(shape, lambda *_: (0,) * len(shape))


def _adaln_kernel(c_ref, w_ref, b_ref, o_ref):
    c = c_ref[...]
    s = c * _sigmoid(c)
    shi, slo = _split2(s)
    whi, wlo = _split2(w_ref[0])
    o_ref[0] = _dot(shi, whi) + _dot(slo, whi) + _dot(shi, wlo) + b_ref[0]


def _adaln(cond, ada_w, ada_b):
    nl, d, n6 = ada_w.shape
    tn = 1536
    rows = cond.shape[0]
    return pl.pallas_call(
        _adaln_kernel,
        grid=(nl, n6 // tn),
        in_specs=[pl.BlockSpec((rows, d), lambda l, j: (0, 0)),
                  pl.BlockSpec((1, d, tn), lambda l, j: (l, 0, j)),
                  pl.BlockSpec((1, 1, tn), lambda l, j: (l, 0, j))],
        out_specs=pl.BlockSpec((1, rows, tn), lambda l, j: (l, 0, j)),
        out_shape=jax.ShapeDtypeStruct((nl, rows, n6), F32),
        compiler_params=_cp("parallel", "parallel"),
        name="adaln",
    )(cond, ada_w, ada_b.reshape(nl, 1, n6))


def _head_norm(z, gain, ones_bf):
    ms = _dot2(z * z, ones_bf) * (1.0 / HEAD_DIM)
    return z * lax.rsqrt(ms + NORM_EPS) * gain


def _rope(z, cos, sin_signed):
    w = z.shape[1]
    lane = lax.broadcasted_iota(jnp.int32, z.shape, 1)
    first = (lane % HEAD_DIM) < (HEAD_DIM // 2)
    swapped = jnp.where(first, pltpu.roll(z, w - HEAD_DIM // 2, 1), pltpu.roll(z, HEAD_DIM // 2, 1))
    return z * cos + swapped * sin_signed


def _proj_kernel(*refs, rope):
    if rope:
        (x_ref, mod_ref, n1_ref, w_ref, gqa_ref, gka_ref, gqc_ref, gkc_ref, ones_ref, cos_ref, sin_ref,
         qa_ref, ka_ref, va_ref, urw_ref, qc_ref, kc_ref, vc_ref) = refs
    else:
        (x_ref, mod_ref, n1_ref, w_ref, gqa_ref, gka_ref, gqc_ref, gkc_ref, ones_ref,
         qa_ref, ka_ref, va_ref, urw_ref, qc_ref, kc_ref, vc_ref) = refs
    x = x_ref[...]
    sh1 = mod_ref[0, 0:1, :]
    sc1 = mod_ref[0, 1:2, :]
    ms = jnp.mean(x * x, axis=-1, keepdims=True)
    h = x * lax.rsqrt(ms + NORM_EPS) * n1_ref[...] * (1.0 + sc1) + sh1
    u = _dot(_bf(h), w_ref[...])
    o0 = 0
    o1 = NA_W
    o2 = 2 * NA_W
    o3 = 3 * NA_W
    o4 = o3 + RW_IN_W
    o5 = o4 + SWA_W
    o6 = o5 + SWA_KV_W
    ones = ones_ref[...]
    ones_kv = ones_ref[0:SWA_KV_W, 0:SWA_KV_W]
    qa = _head_norm(u[:, o0:o1], gqa_ref[...], ones)
    ka = _head_norm(u[:, o1:o2], gka_ref[...], ones)
    qc = _head_norm(u[:, o4:o5], gqc_ref[...], ones)
    kc = _head_norm(u[:, o5:o6], gkc_ref[...], ones_kv)
    if rope:
        qc = _rope(qc, cos_ref[...], sin_ref[...])
        kc = _rope(kc, cos_ref[:, 0:SWA_KV_W], sin_ref[:, 0:SWA_KV_W])
    qa_ref[...] = qa.astype(qa_ref.dtype)
    ka_ref[...] = ka.astype(ka_ref.dtype)
    va_ref[...] = u[:, o2:o3].astype(va_ref.dtype)
    urw_ref[...] = u[:, o3:o4]
    qc_ref[...] = qc.astype(qc_ref.dtype)
    kc_ref[...] = kc.astype(kc_ref.dtype)
    vc_ref[...] = u[:, o6:].astype(vc_ref.dtype)


def _proj(x, mod, n1, w_in_bf, gains, ones384, rope_tabs, seq, per_request_mod, qkv_dtype):
    tokens, d = x.shape
    tm = _row_tile(seq)
    tiles_per_req = seq // tm
    in_w = w_in_bf.shape[1]
    rope = rope_tabs is not None
    mod_map = (lambda i: (i // tiles_per_req, 0, 0)) if per_request_mod else (lambda i: (0, 0, 0))
    row = lambda w: pl.BlockSpec((tm, w), lambda i: (i, 0))
    in_specs = [row(d), pl.BlockSpec((1, 6, d), mod_map), _full((1, d)), _full((d, in_w)),
                _full((1, NA_W)), _full((1, NA_W)), _full((1, SWA_W)), _full((1, SWA_KV_W)),
                _full((NA_W, NA_W))]
    args = [x, mod, n1, w_in_bf, *gains, ones384]
    if rope:
        tab = pl.BlockSpec((tm, SWA_W), lambda i: (i % tiles_per_req, 0))
        in_specs += [tab, tab]
        args += list(rope_tabs)
    widths = [NA_W, NA_W, NA_W, RW_IN_W, SWA_W, SWA_KV_W, SWA_KV_W]
    dtypes = [qkv_dtype, qkv_dtype, qkv_dtype, F32, qkv_dtype, qkv_dtype, qkv_dtype]
    return pl.pallas_call(
        functools.partial(_proj_kernel, rope=rope),
        grid=(tokens // tm,),
        in_specs=in_specs,
        out_specs=[row(w) for w in widths],
        out_shape=[jax.ShapeDtypeStruct((tokens, w), dt) for w, dt in zip(widths, dtypes)],
        compiler_params=_cp("parallel"),
        name="proj",
    )(*args)


def _half_masks(width=2 * HEAD_DIM):
    lane = lax.broadcasted_iota(jnp.int32, (1, width), 1)
    return lane < HEAD_DIM, lane >= HEAD_DIM


def _swap_halves(z):
    return pltpu.roll(z, HEAD_DIM, 1)


def _ctx_attn_kernel(sink_ref, qa_ref, ka_ref, va_ref, qc_ref, kc_ref, vc_ref, oa_ref, oc_ref):
    scale = HEAD_DIM ** -0.5
    m0, m1 = _half_masks()
    masks = (m0, m1)
    for pair in range(NA_HEADS // 2):
        sl = slice(pair * 128, (pair + 1) * 128)
        qp = qa_ref[:, sl].astype(F32) * scale
        kp = _bf(ka_ref[:, sl])
        vp = _bf(va_ref[:, sl])
        outs = []
        for half in range(2):
            qm = _bf(jnp.where(masks[half], qp, 0.0))
            s = _dot_nt(qm, kp)
            m = jnp.max(s, axis=-1, keepdims=True)
            e = jnp.exp(s - m)
            l = jnp.sum(e, axis=-1, keepdims=True)
            outs.append(_dot(_bf(e), vp) / l)
        oa_ref[:, sl] = jnp.where(m0, outs[0], outs[1])
    kc = _bf(kc_ref[...])
    vc = _bf(vc_ref[...])
    group = SWA_HEADS // SWA_KV_HEADS
    for pair in range(SWA_HEADS // 2):
        sl = slice(pair * 128, (pair + 1) * 128)
        qp = qc_ref[:, sl].astype(F32) * scale
        outs = []
        for half in range(2):
            h = 2 * pair + half
            g = h // group
            qh = qp if g == half else _swap_halves(qp)
            qm = _bf(jnp.where(masks[g], qh, 0.0))
            s = _dot_nt(qm, kc)
            sk = sink_ref[h]
            m = jnp.maximum(jnp.max(s, axis=-1, keepdims=True), sk)
            e = jnp.exp(s - m)
            l = jnp.sum(e, axis=-1, keepdims=True) + jnp.exp(sk - m)
            o = _dot(_bf(e), vc) / l
            outs.append(o if g == half else _swap_halves(o))
        oc_ref[:, sl] = jnp.where(m0, outs[0], outs[1])


def _ctx_attn(sink, qa, ka, va, qc, kc, vc, seq):
    tokens = qa.shape[0]
    blk = lambda w: pl.BlockSpec((seq, w), lambda b: (b, 0))
    return pl.pallas_call(
        _ctx_attn_kernel,
        grid=(tokens // seq,),
        in_specs=[pl.BlockSpec(memory_space=pltpu.SMEM), blk(NA_W), blk(NA_W), blk(NA_W), blk(SWA_W),
                  blk(SWA_KV_W), blk(SWA_KV_W)],
        out_specs=[blk(NA_W), blk(SWA_W)],
        out_shape=[jax.ShapeDtypeStruct((tokens, NA_W), F32), jax.ShapeDtypeStruct((tokens, SWA_W), F32)],
        compiler_params=_cp("parallel"),
        name="ctx_attn",
    )(sink, qa, ka, va, qc, kc, vc)


def _na_bias_kernel(rpb_ref, o_ref):
    h = pl.program_id(0)
    nrow = 2 * NA_KH - 1
    ncol = 2 * NA_KW - 1
    width = NA_KH * GRID_W
    shape = (GRID_W, width)
    lane = lax.broadcasted_iota(jnp.int32, shape, 1)
    qc = lax.broadcasted_iota(jnp.int32, shape, 0)
    kc = lane % GRID_W
    c_start = jnp.clip(qc - NA_KW // 2, 0, GRID_W - NA_KW)
    ok = (kc >= c_start) & (kc < c_start + NA_KW)
    d_col = jnp.clip(kc - qc, 1 - NA_KW, NA_KW - 1) + NA_KW - 1
    key_row = lax.broadcasted_iota(jnp.int32, (1, width), 1) // GRID_W

    def case_body(case, carry):
        acc = jnp.zeros(shape, F32)
        for dc in range(ncol):
            val = jnp.zeros((1, width), F32)
            for i in range(NA_KH):
                val = jnp.where(key_row == i, rpb_ref[(h * nrow + case + i) * ncol + dc], val)
            acc = jnp.where(d_col == dc, val, acc)
        o_ref[0, pl.ds(case, 1)] = jnp.where(ok, acc, NEG_INF)[None]
        return carry

    lax.fori_loop(0, NA_KH, case_body, 0)


def _na_bias_table(rpb):
    nh = rpb.shape[0]
    return pl.pallas_call(
        _na_bias_kernel,
        grid=(nh,),
        in_specs=[pl.BlockSpec(memory_space=pltpu.SMEM)],
        out_specs=pl.BlockSpec((1, NA_KH, GRID_W, NA_KH * GRID_W), lambda h: (h // 2, 0, h % 2, 0)),
        out_shape=jax.ShapeDtypeStruct((nh // 2, NA_KH, 2 * GRID_W, NA_KH * GRID_W), F32),
        compiler_params=_cp("parallel"),
        name="na_bias",
    )(rpb.reshape(-1))


def _na_kernel(q_ref, k_ref, v_ref, kx_ref, vx_ref, bias_ref, o_ref, *, rows):
    scale = HEAD_DIM ** -0.5
    m0, m1 = _half_masks()
    kx = _bf(kx_ref[...])
    vx = _bf(vx_ref[...])

    def body(it, carry):
        us = range(NA_UNROLL)
        r = [it * NA_UNROLL + u for u in us]
        rs = [jnp.clip(r[u] - NA_KH // 2, 0, rows - NA_KH) for u in us]
        case = [rs[u] - r[u] + NA_KH - 1 for u in us]
        q0 = [pl.multiple_of(r[u] * GRID_W, GRID_W) for u in us]
        k0 = [pl.multiple_of(rs[u] * GRID_W, GRID_W) for u in us]
        qp = [q_ref[pl.ds(q0[u], GRID_W), :].astype(F32) * scale for u in us]
        kw = [_bf(k_ref[pl.ds(k0[u], NA_KH * GRID_W), :]) for u in us]
        vw = [_bf(v_ref[pl.ds(k0[u], NA_KH * GRID_W), :]) for u in us]
        q2 = [_bf(jnp.concatenate([jnp.where(m0, qp[u], 0.0), jnp.where(m1, qp[u], 0.0)], axis=0)) for u in us]
        sw = [_dot_nt(q2[u], kw[u]) + bias_ref[0, pl.ds(case[u], 1)][0] for u in us]
        sx = [_dot_nt(q2[u], kx) for u in us]
        m = [jnp.maximum(jnp.max(sw[u], axis=-1, keepdims=True), jnp.max(sx[u], axis=-1, keepdims=True))
             for u in us]
        ew = [jnp.exp(sw[u] - m[u]) for u in us]
        ex = [jnp.exp(sx[u] - m[u]) for u in us]
        l = [jnp.sum(ew[u], axis=-1, keepdims=True) + jnp.sum(ex[u], axis=-1, keepdims=True) for u in us]
        o = [(_dot(_bf(ew[u]), vw[u]) + _dot(_bf(ex[u]), vx)) / l[u] for u in us]
        for u in us:
            o_ref[pl.ds(q0[u], GRID_W), :] = jnp.where(m0, o[u][0:GRID_W], o[u][GRID_W:])
        return carry

    lax.fori_loop(0, rows // NA_UNROLL, body, 0)


def _na_attn(q, k, v, kx, vx, bias, seq, past):
    tokens = q.shape[0]
    nb = tokens // seq
    rows = seq // GRID_W
    blk = pl.BlockSpec((seq, 128), lambda b, p: (b, p))
    cblk = pl.BlockSpec((past, 128), lambda b, p: (b, p))
    return pl.pallas_call(
        functools.partial(_na_kernel, rows=rows),
        grid=(nb, NA_HEADS // 2),
        in_specs=[blk, blk, blk, cblk, cblk,
                  pl.BlockSpec((1, NA_KH, 2 * GRID_W, NA_KH * GRID_W), lambda b, p: (p, 0, 0, 0))],
        out_specs=blk,
        out_shape=jax.ShapeDtypeStruct((tokens, NA_W), F32),
        compiler_params=_cp("parallel", "parallel"),
        name="na_attn",
    )(q, k, v, kx, vx, bias)


def _swa_kernel(sink_ref, q_ref, k_ref, v_ref, kx_ref, vx_ref, o_ref, *, seq):
    scale = HEAD_DIM ** -0.5
    blk = SWA_WINDOW
    m0, m1 = _half_masks()
    masks = (m0, m1)
    kx = _bf(kx_ref[...])
    vx = _bf(vx_ref[...])
    group = SWA_HEADS // SWA_KV_HEADS

    sk = []
    for g in range(SWA_KV_HEADS):
        sk.append(jnp.concatenate([jnp.full((blk, 1), sink_ref[h], F32) for h in range(g * group, (g + 1) * group)],
                                  axis=0))

    def body(it, carry):
        us = range(SWA_UNROLL)
        cs = [(u, g) for u in us for g in range(SWA_KV_HEADS)]
        nb = [it * SWA_UNROLL + u for u in us]
        ks = [pl.multiple_of(jnp.clip((nb[u] - 1) * blk, 0, seq - 3 * blk), blk) for u in us]
        q0 = [pl.multiple_of(nb[u] * blk, blk) for u in us]
        kw = [_bf(k_ref[pl.ds(ks[u], 3 * blk), :]) for u in us]
        vw = [_bf(v_ref[pl.ds(ks[u], 3 * blk), :]) for u in us]
        ok = []
        for u in us:
            qpos = q0[u] + lax.broadcasted_iota(jnp.int32, (group * blk, 1), 0) % blk
            kpos = ks[u] + lax.broadcasted_iota(jnp.int32, (1, 3 * blk), 1)
            ok.append(jnp.abs(qpos - kpos) <= SWA_WINDOW)
        qg = {}
        for u in us:
            pairs = [q_ref[pl.ds(q0[u], blk), p * 128:(p + 1) * 128].astype(F32) * scale
                     for p in range(SWA_HEADS // 2)]
            for g in range(SWA_KV_HEADS):
                qs = []
                for h in range(g * group, (g + 1) * group):
                    qh = pairs[h // 2] if h % 2 == g else _swap_halves(pairs[h // 2])
                    qs.append(jnp.where(masks[g], qh, 0.0))
                qg[u, g] = _bf(jnp.concatenate(qs, axis=0))
        sw = {c: jnp.where(ok[c[0]], _dot_nt(qg[c], kw[c[0]]), NEG_INF) for c in cs}
        sx = {c: _dot_nt(qg[c], kx) for c in cs}
        m = {c: jnp.maximum(jnp.maximum(jnp.max(sw[c], axis=-1, keepdims=True),
                                        jnp.max(sx[c], axis=-1, keepdims=True)), sk[c[1]]) for c in cs}
        ew = {c: jnp.exp(sw[c] - m[c]) for c in cs}
        ex = {c: jnp.exp(sx[c] - m[c]) for c in cs}
        l = {c: jnp.sum(ew[c], axis=-1, keepdims=True) + jnp.sum(ex[c], axis=-1, keepdims=True)
             + jnp.exp(sk[c[1]] - m[c]) for c in cs}
        o = {c: (_dot(_bf(ew[c]), vw[c[0]]) + _dot(_bf(ex[c]), vx)) / l[c] for c in cs}
        for u in us:
            head_out = []
            for g in range(SWA_KV_HEADS):
                for i in range(group):
                    h = g * group + i
                    oh = o[u, g][i * blk:(i + 1) * blk]
                    head_out.append(oh if h % 2 == g else _swap_halves(oh))
            for p in range(SWA_HEADS // 2):
                o_ref[pl.ds(q0[u], blk), p * 128:(p + 1) * 128] = jnp.where(m0, head_out[2 * p], head_out[2 * p + 1])
        return carry

    lax.fori_loop(0, seq // (blk * SWA_UNROLL), body, 0)


def _swa_attn(sink, q, k, v, kx, vx, seq, past):
    tokens = q.shape[0]
    blk = lambda w: pl.BlockSpec((seq, w), lambda b: (b, 0))
    cblk = pl.BlockSpec((past, SWA_KV_W), lambda b: (b, 0))
    return pl.pallas_call(
        functools.partial(_swa_kernel, seq=seq),
        grid=(tokens // seq,),
        in_specs=[pl.BlockSpec(memory_space=pltpu.SMEM), blk(SWA_W), blk(SWA_KV_W), blk(SWA_KV_W), cblk, cblk],
        out_specs=blk(SWA_W),
        out_shape=jax.ShapeDtypeStruct((tokens, SWA_W), F32),
        compiler_params=_cp("parallel"),
        name="swa_attn",
    )(sink, q, k, v, kx, vx)


def _rw_pre_kernel(u_ref, up_ref, un_ref, mu_ref, w0_ref, w2_ref, a0_ref, a2_ref, g2_ref, kk_ref_, ka_ref_,
                   rk_ref, ones_ref, r_o, kk_o, v_o, ld_o, kka_o, kd_o, g_o, bonus_o, *, tiles_per_req):
    i = pl.program_id(0)
    u = u_ref[...]
    tm = u.shape[0]
    rowi = lax.broadcasted_iota(jnp.int32, (tm, 1), 0)
    first = (i % tiles_per_req) == 0
    last = (i % tiles_per_req) == tiles_per_req - 1
    prev_row = jnp.where(first, 0.0, up_ref[7:8, :])
    next_row = jnp.where(last, 0.0, un_ref[0:1, :])
    prev = jnp.where(rowi == 0, prev_row, pltpu.roll(u, 1, 0))
    nxt = jnp.where(rowi == tm - 1, next_row, pltpu.roll(u, tm - 1, 0))
    us = u + mu_ref[0:1, :] * (prev - u) + mu_ref[1:2, :] * (nxt - u)
    r = us[:, 0:RW_W]
    k = us[:, RW_W:2 * RW_W]
    v = us[:, 2 * RW_W:3 * RW_W]
    wl = us[:, 3 * RW_W:3 * RW_W + 128]
    al = us[:, 3 * RW_W + 128:3 * RW_W + 256]
    gl = us[:, 3 * RW_W + 256:3 * RW_W + 384]
    z = -(w0_ref[...] + _dot(_bf(jnp.tanh(wl)), w2_ref[...]))
    softplus = jnp.maximum(z, 0.0) + jnp.log(1.0 + jnp.exp(-jnp.abs(z)))
    w = -softplus - 0.5
    ld = -jnp.exp(w)
    a = _sigmoid(a0_ref[...] + _dot(_bf(al), a2_ref[...]))
    g = _dot(_bf(_sigmoid(gl)), g2_ref[...])
    ones = ones_ref[...]
    kkr = k * kk_ref_[...]
    kk = kkr * lax.rsqrt(jnp.maximum(_dot2(kkr * kkr, ones), 1e-24))
    k_a = ka_ref_[...]
    kd_f = k * (1.0 + (a[:, 0:RW_W] - 1.0) * k_a)
    kd_b = k * (1.0 + (a[:, RW_W:] - 1.0) * k_a)
    r_o[...] = r
    kk_o[...] = kk
    v_o[...] = v
    ld_o[...] = ld
    kka_o[:, 0:RW_W] = kk * a[:, 0:RW_W]
    kka_o[:, RW_W:] = kk * a[:, RW_W:]
    kd_o[:, 0:RW_W] = kd_f
    kd_o[:, RW_W:] = kd_b
    g_o[...] = g
    bonus_o[...] = _dot2(r * (kd_f + kd_b) * rk_ref[...], ones) * v


def _rw_pre(urw, p, seq):
    tokens = urw.shape[0]
    tm = 256
    tpr = seq // tm
    nt = tokens // tm
    r8 = tm // 8
    row = lambda w: pl.BlockSpec((tm, w), lambda i: (i, 0))
    in_specs = [row(RW_IN_W),
                pl.BlockSpec((8, RW_IN_W), lambda i: (jnp.maximum(i * r8 - 1, 0), 0)),
                pl.BlockSpec((8, RW_IN_W), lambda i: (jnp.minimum((i + 1) * r8, nt * r8 - 1), 0)),
                _full((2, RW_IN_W)), _full((1, 2 * RW_W)), _full((128, 2 * RW_W)), _full((1, 2 * RW_W)),
                _full((128, 2 * RW_W)), _full((128, RW_W)), _full((1, RW_W)), _full((1, RW_W)),
                _full((1, RW_W)), _full((RW_W, RW_W))]
    widths = [RW_W, RW_W, RW_W, 2 * RW_W, 2 * RW_W, 2 * RW_W, RW_W, RW_W]
    return pl.pallas_call(
        functools.partial(_rw_pre_kernel, tiles_per_req=tpr),
        grid=(nt,),
        in_specs=in_specs,
        out_specs=[row(w) for w in widths],
        out_shape=[jax.ShapeDtypeStruct((tokens, w), F32) for w in widths],
        compiler_params=_cp("parallel"),
        name="rw_pre",
    )(urw, urw, urw, p['mu'], p['w0'], p['w2'], p['a0'], p['a2'], p['g2'], p['k_k'], p['k_a'], p['r_k'],
      p['ones256'])


def _rw_masks():
    t = RW_CHUNK
    n = RW_HEADS * t
    tt = np.arange(t)[:, None]
    ss = (np.arange(n) % t)[None, :]
    before = np.stack([ss < tt, ss > tt])
    diag = (ss == tt)
    strict = before.astype(np.float32)
    incl = (before | diag[None]).astype(np.float32)
    eye = diag.astype(np.float32)
    ti = np.arange(t)
    tri = np.stack([ti[None, :] <= ti[:, None], ti[None, :] >= ti[:, None]]).astype(np.float32)
    hd = np.arange(n) // t
    same = (hd[:, None] == hd[None, :]).astype(np.float32)
    return (jnp.asarray(strict), jnp.asarray(incl), jnp.asarray(tri, dtype=BF16), jnp.asarray(same),
            jnp.asarray(eye))


def _rw_scan_kernel(r_ref, kk_ref, v_ref, ld_ref, kka_ref, kd_ref, s0_ref, strict_ref, incl_ref, tri_ref,
                    same_ref, eye_ref, o_ref, sfin_ref, s_scr, *, nsub):
    d = pl.program_id(1)
    c = pl.program_id(2)
    t = RW_CHUNK
    n = RW_HEADS * t

    @pl.when(c == 0)
    def _():
        s_scr[...] = s0_ref[0, 0]

    strict = strict_ref[0]
    incl = incl_ref[0]
    tri = tri_ref[0]
    eye = eye_ref[...]
    same = same_ref[...]
    same_bf = _bf(same)

    def bd(x):
        return jnp.concatenate([_bf(x)] * RW_HEADS, axis=0) * same_bf

    js = range(nsub)
    rows = [pl.ds(pl.multiple_of((j + d * (nsub - 1 - 2 * j)) * t, t), t) for j in js]
    ld = [ld_ref[rows[j], :] for j in js]
    cum = []
    for j in js:
        lhi, lmid, llo = _split3(ld[j])
        cum.append(_dot(tri, lhi) + _dot(tri, lmid) + _dot(tri, llo))
    cend = [jnp.sum(ld[j], axis=0, keepdims=True) for j in js]
    kka = [kka_ref[rows[j], :] for j in js]
    kd = [kd_ref[rows[j], :] for j in js]
    v = [v_ref[rows[j], :] for j in js]
    at = [-kk_ref[rows[j], :] * jnp.exp(cum[j] - ld[j]) for j in js]
    rt = [r_ref[rows[j], :] * jnp.exp(cum[j]) for j in js]
    e_inv = [jnp.exp(-cum[j]) for j in js]
    aa = [_dot_nt(_bf(jnp.concatenate([at[j], rt[j]], axis=0)),
                  jnp.concatenate([bd(kka[j] * e_inv[j]), bd(kd[j] * e_inv[j])], axis=0)) for j in js]
    a_ab = [aa[j][0:t, 0:n] * strict for j in js]
    x = [eye + a_ab[j] for j in js]
    pw = a_ab
    for _ in range(5):
        pw = [_dot(_bf(pw[j]), bd(pw[j])) for j in js]
        x = [x[j] + _dot(_bf(x[j]), bd(pw[j])) for j in js]
    v_bd = [bd(v[j]) for j in js]
    wv = [_dot(_bf(aa[j][0:t, n:] * strict), v_bd[j]) for j in js]
    mu = [_dot(_bf(x[j]), jnp.concatenate([bd(at[j]), bd(wv[j])], axis=1)) for j in js]
    m1 = [mu[j][:, 0:n] for j in js]
    u0 = [mu[j][:, n:] for j in js]
    e_end = [jnp.exp(cend[j] - cum[j]) for j in js]
    bend = [_bf(kka[j] * e_end[j]) for j in js]
    g = [_bf(_dot_tn(_bf(m1[j]), bend[j]) * same) for j in js]
    cst = [_dot_tn(_bf(jnp.concatenate([u0[j], v[j]], axis=0)),
                   jnp.concatenate([bend[j], _bf(kd[j] * e_end[j])], axis=0)) * same for j in js]
    qo = [_dot(_bf(aa[j][t:, 0:n] * incl), jnp.concatenate([bd(m1[j]), bd(u0[j])], axis=1)) for j in js]
    q = [_bf(rt[j] + qo[j][:, 0:n]) for j in js]
    o0 = [qo[j][:, n:] + _dot(_bf(aa[j][t:, n:] * incl), v_bd[j]) for j in js]

    s = s_scr[...]
    for j in js:
        s_bf = _bf(s)
        o_ref[0, rows[j], :] = _dot_nt(q[j], s_bf) + o0[j]
        s = s * jnp.exp(cend[j]) + _dot(s_bf, g[j]) + cst[j]
    s_scr[...] = s

    @pl.when(c == pl.num_programs(2) - 1)
    def _():
        sfin_ref[0, 0] = s


def _rw_scan(r, kk, v, ld, kka, kd, s0_bd, consts, seq):
    tokens = r.shape[0]
    nreq = tokens // seq
    tb = 256
    nblk = seq // tb
    nsub = tb // RW_CHUNK
    n = RW_HEADS * RW_CHUNK
    cc = lambda d, c: c + d * (nblk - 1 - 2 * c)
    shared = pl.BlockSpec((tb, RW_W), lambda b, d, c: (b * nblk + cc(d, c), 0))
    dirw = pl.BlockSpec((tb, RW_W), lambda b, d, c: (b * nblk + cc(d, c), d))
    strict, incl, tri, same, eye = consts
    return pl.pallas_call(
        functools.partial(_rw_scan_kernel, nsub=nsub),
        grid=(nreq, 2, nblk),
        in_specs=[shared, shared, shared, dirw, dirw, dirw,
                  pl.BlockSpec((1, 1, n, n), lambda b, d, c: (b, d, 0, 0)),
                  pl.BlockSpec((1, RW_CHUNK, n), lambda b, d, c: (d, 0, 0)),
                  pl.BlockSpec((1, RW_CHUNK, n), lambda b, d, c: (d, 0, 0)),
                  pl.BlockSpec((1, RW_CHUNK, RW_CHUNK), lambda b, d, c: (d, 0, 0)),
                  _full((n, n)), _full((RW_CHUNK, n))],
        out_specs=[pl.BlockSpec((1, tb, RW_W), lambda b, d, c: (d, b * nblk + cc(d, c), 0)),
                   pl.BlockSpec((1, 1, n, n), lambda b, d, c: (b, d, 0, 0))],
        out_shape=[jax.ShapeDtypeStruct((2, tokens, RW_W), F32), jax.ShapeDtypeStruct((nreq, 2, n, n), F32)],
        scratch_shapes=[pltpu.VMEM((n, n), F32)],
        compiler_params=_cp("parallel", "parallel", "arbitrary"),
        name="rw_scan",
    )(r, kk, v, ld, kka, kd, s0_bd, strict, incl, tri, same, eye)


def _state_to_blockdiag(s):
    b = s.shape[0]
    eye = jnp.eye(RW_HEADS, dtype=s.dtype)
    out = s[:, :, :, :, None, :] * eye[None, None, :, None, :, None]
    return out.reshape(b, 2, RW_W, RW_W)


def _blockdiag_to_state(sbd):
    b = sbd.shape[0]
    s = sbd.reshape(b, 2, RW_HEADS, HEAD_DIM, RW_HEADS, HEAD_DIM)
    idx = jnp.arange(RW_HEADS)
    return jnp.transpose(s[:, :, idx, :, idx, :], (1, 2, 0, 3, 4))


def _finish_kernel(x_ref, oa_ref, oc_ref, o2_ref, bonus_ref, g_ref, mod_ref, wout_ref, lng_ref, lnb_ref,
                   n2_ref, wr_hi_ref, wr_lo_ref, ones_ref, x1_ref, h2_ref, aff_ref):
    ones = ones_ref[...]
    y = o2_ref[0] + o2_ref[1]
    mu = _dot2(y, ones) * (1.0 / HEAD_DIM)
    yc = y - mu
    var = _dot2(yc * yc, ones) * (1.0 / HEAD_DIM)
    yn = yc * lax.rsqrt(var + GN_EPS) * lng_ref[...] + lnb_ref[...]
    ob = (yn + bonus_ref[...]) * g_ref[...]
    mixin = jnp.concatenate([_bf(oa_ref[...]), _bf(ob), _bf(oc_ref[...])], axis=1)
    mix = _dot(mixin, wout_ref[...])
    g1 = mod_ref[0, 2:3, :]
    sh2 = mod_ref[0, 3:4, :]
    sc2 = mod_ref[0, 4:5, :]
    x1 = x_ref[...] + g1 * mix
    ms = jnp.mean(x1 * x1, axis=-1, keepdims=True)
    h2 = x1 * lax.rsqrt(ms + NORM_EPS) * n2_ref[...] * (1.0 + sc2) + sh2
    x1_ref[...] = x1
    h2_ref[...] = _bf(h2)
    hhi, hlo = _split2(h2)
    logits = _dot(hhi, wr_hi_ref[...]) + _dot(hlo, wr_hi_ref[...]) + _dot(hhi, wr_lo_ref[...])
    m = jnp.max(logits, axis=-1, keepdims=True)
    e = jnp.exp(logits - m)
    aff_ref[...] = e / jnp.sum(e, axis=-1, keepdims=True)


def _finish(x, oa, oc, o2, bonus, g, mod, p, seq, per_request_mod):
    tokens, d = x.shape
    tm = _row_tile(seq)
    tpr = seq // tm
    mod_map = (lambda i: (i // tpr, 0, 0)) if per_request_mod else (lambda i: (0, 0, 0))
    row = lambda w: pl.BlockSpec((tm, w), lambda i: (i, 0))
    return pl.pallas_call(
        _finish_kernel,
        grid=(tokens // tm,),
        in_specs=[row(d), row(NA_W), row(SWA_W), pl.BlockSpec((2, tm, RW_W), lambda i: (0, i, 0)), row(RW_W),
                  row(RW_W), pl.BlockSpec((1, 6, d), mod_map), _full((d, d)), _full((1, RW_W)),
                  _full((1, RW_W)), _full((1, d)), _full((d, N_EXPERTS)), _full((d, N_EXPERTS)),
                  _full((RW_W, RW_W))],
        out_specs=[row(d), row(d), row(N_EXPERTS)],
        out_shape=[jax.ShapeDtypeStruct((tokens, d), F32), jax.ShapeDtypeStruct((tokens, d), BF16),
                   jax.ShapeDtypeStruct((tokens, N_EXPERTS), F32)],
        compiler_params=_cp("parallel"),
        name="finish",
    )(x, oa, oc, o2, bonus, g, mod, p['w_out'], p['ln_g'], p['ln_b'], p['n2'], p['wr_hi'], p['wr_lo'],
      p['ones256'])


def _topk_kernel(aff_ref, tri_ref, eye_ref, place_ref, slot_ref, slotrow_ref, gfull_ref, ends_ref, *, cap, group,
                 tb):
    b = pl.program_id(0)
    aff = aff_ref[...]
    seq = aff.shape[0]
    bits = lax.bitcast_convert_type(aff, jnp.int32)
    capf = jnp.float32(cap)

    def bis(_, carry):
        lo, hi = carry
        mid = lo + ((hi - lo + 1) >> 1)
        cnt = jnp.sum(jnp.where(bits >= mid, 1.0, 0.0), axis=0, keepdims=True)
        ge = cnt >= capf
        return jnp.where(ge, mid, lo), jnp.where(ge, hi, mid - 1)

    lo0 = jnp.zeros((1, N_EXPERTS), jnp.int32)
    hi0 = jnp.full((1, N_EXPERTS), 0x7F7FFFFF, jnp.int32)
    thr, _ = lax.fori_loop(0, 31, bis, (lo0, hi0))
    gt = jnp.where(bits > thr, 1.0, 0.0)
    eq = jnp.where(bits == thr, 1.0, 0.0)
    need = capf - jnp.sum(gt, axis=0, keepdims=True)
    offset = ((b % group) * cap).astype(F32)
    tri = tri_ref[...]
    eye = eye_ref[...]
    carry_g = jnp.zeros((1, N_EXPERTS), F32)
    carry_e = jnp.zeros((1, N_EXPERTS), F32)
    ghi, gmid, glo = _split3(aff)
    for blk in range(seq // tb):
        sl = slice(blk * tb, (blk + 1) * tb)
        pg = _dot(tri, _bf(gt[sl])) + carry_g
        pe = _dot(tri, _bf(eq[sl])) + carry_e
        carry_g = pg[tb - 1:tb, :]
        carry_e = pe[tb - 1:tb, :]
        sel = gt[sl] + eq[sl] * jnp.where(pe <= need, 1.0, 0.0)
        slot = jnp.where(sel > 0.5, pg + jnp.minimum(pe, need) - 1.0 + offset, -1.0)
        slot_ref[sl, :] = slot
        ends_ref[0, blk:blk + 1, :] = carry_g + jnp.minimum(carry_e, need) + offset
        shi, slo = _split2(slot)
        slotrow_ref[0, :, 0, sl] = _dot_nt(eye, shi) + _dot_nt(eye, slo)
        gfull_ref[sl, :] = _bf(_dot(ghi[sl], place_ref[0]) + _dot(gmid[sl], place_ref[1])
                               + _dot(glo[sl], place_ref[2]))


def _topk(aff, seq, group):
    tokens = aff.shape[0]
    nreq = tokens // seq
    cap = EC_CAPACITY * seq // N_EXPERTS
    tb = min(seq, 512)
    ti = np.arange(tb)
    tri = jnp.asarray((ti[None, :] <= ti[:, None]).astype(np.float32), dtype=BF16)
    eye = jnp.asarray(np.eye(N_EXPERTS, dtype=np.float32), dtype=BF16)
    place = np.zeros((3, N_EXPERTS, 128), np.float32)
    for s in range(3):
        place[s, np.arange(N_EXPERTS), s * N_EXPERTS + np.arange(N_EXPERTS)] = 1.0
    place = jnp.asarray(place, dtype=BF16)
    nblk = seq // tb
    slot, slotrow, gfull, ends = pl.pallas_call(
        functools.partial(_topk_kernel, cap=cap, group=group, tb=tb),
        grid=(nreq,),
        in_specs=[pl.BlockSpec((seq, N_EXPERTS), lambda b: (b, 0)), _full((tb, tb)),
                  _full((N_EXPERTS, N_EXPERTS)), _full((3, N_EXPERTS, 128))],
        out_specs=[pl.BlockSpec((seq, N_EXPERTS), lambda b: (b, 0)),
                   pl.BlockSpec((1, N_EXPERTS, 1, seq), lambda b: (b // group, 0, 0, b % group)),
                   pl.BlockSpec((seq, 128), lambda b: (b, 0)),
                   pl.BlockSpec((1, nblk, N_EXPERTS), lambda b: (b, 0, 0))],
        out_shape=[jax.ShapeDtypeStruct((tokens, N_EXPERTS), F32),
                   jax.ShapeDtypeStruct((nreq // group, N_EXPERTS, 1, group * seq), F32),
                   jax.ShapeDtypeStruct((tokens, 128), BF16),
                   jax.ShapeDtypeStruct((nreq, nblk, N_EXPERTS), F32)],
        compiler_params=_cp("parallel"),
        name="topk",
    )(aff, tri, eye, place)
    ends = ends.reshape(nreq // group, group * nblk, N_EXPERTS).transpose(0, 2, 1)
    return slot, slotrow, gfull, ends.astype(jnp.int32).reshape(-1), tb


def _moe_ffn_kernel(ends_ref, h_ref, slotrow_ref, gfull_ref, mod_ref, wg_ref, wu_ref, wd_ref, ye_ref, xe_scr,
                    gs_scr, *, ct, kc, nch):
    gi = pl.program_id(0)
    e = pl.program_id(1)
    mt = 128
    xe_scr[...] = jnp.zeros_like(xe_scr)
    gs_scr[...] = jnp.zeros_like(gs_scr)
    base = (gi * N_EXPERTS + e) * nch
    for c in range(nch):
        lo = 0 if c == 0 else ends_ref[base + c - 1]
        hi = ends_ref[base + c]
        sl = slice(c * kc, (c + 1) * kc)
        for m in range(ct // mt):
            rows = slice(m * mt, (m + 1) * mt)

            @pl.when((hi > m * mt) & (lo < (m + 1) * mt))
            def _(sl=sl, rows=rows, m=m):
                jcol = (lax.broadcasted_iota(jnp.int32, (mt, 1), 0) + m * mt).astype(F32)
                onehot = _bf(jnp.where(slotrow_ref[0, 0, :, sl] == jcol, 1.0, 0.0))
                xe_scr[rows, :] += _dot(onehot, h_ref[sl, :])
                gs_scr[rows, :] += _dot(onehot, gfull_ref[sl, :])

    lane = lax.broadcasted_iota(jnp.int32, (1, 128), 1)
    pick = (lane == e) | (lane == e + N_EXPERTS) | (lane == e + 2 * N_EXPERTS)
    gate = jnp.sum(jnp.where(pick, gs_scr[...], 0.0), axis=-1, keepdims=True)
    xb = _bf(xe_scr[...])
    hg = _dot(xb, wg_ref[0])
    hu = _dot(xb, wu_ref[0])
    he = _bf(hg * _sigmoid(hg) * hu)
    y = _dot(he, wd_ref[0])
    ye_ref[0, 0] = _bf(y * gate * mod_ref[0, 5:6, :])


def _moe_ffn(ends, h2, slotrow, gfull, mod, wg, wu, wd, lg, ct, kc, per_group_mod):
    tokens, d = h2.shape
    ngrp = tokens // lg
    f = wg.shape[2]
    mod_map = (lambda gi, e, ends: (gi, 0, 0)) if per_group_mod else (lambda gi, e, ends: (0, 0, 0))
    grid_spec = pltpu.PrefetchScalarGridSpec(
        num_scalar_prefetch=1,
        grid=(ngrp, N_EXPERTS),
        in_specs=[pl.BlockSpec((lg, d), lambda gi, e, ends: (gi, 0)),
                  pl.BlockSpec((1, 1, 1, lg), lambda gi, e, ends: (gi, e, 0, 0)),
                  pl.BlockSpec((lg, 128), lambda gi, e, ends: (gi, 0)),
                  pl.BlockSpec((1, 6, d), mod_map),
                  pl.BlockSpec((1, d, f), lambda gi, e, ends: (e, 0, 0)),
                  pl.BlockSpec((1, d, f), lambda gi, e, ends: (e, 0, 0)),
                  pl.BlockSpec((1, f, d), lambda gi, e, ends: (e, 0, 0))],
        out_specs=pl.BlockSpec((1, 1, ct, d), lambda gi, e, ends: (gi, e, 0, 0)),
        scratch_shapes=[pltpu.VMEM((ct, d), F32), pltpu.VMEM((ct, 128), F32)])
    return pl.pallas_call(
        functools.partial(_moe_ffn_kernel, ct=ct, kc=kc, nch=lg // kc),
        grid_spec=grid_spec,
        out_shape=jax.ShapeDtypeStruct((ngrp, N_EXPERTS, ct, d), BF16),
        compiler_params=_cp("parallel", "arbitrary"),
        name="moe_ffn",
    )(ends, h2, slotrow, gfull, mod, wg, wu, wd)


def _moe_combine_kernel(ends_ref, x1_ref, slot_ref, ye_ref, o_ref, *, ct, kc, nch):
    gi = pl.program_id(0)
    tblk = pl.program_id(1)
    e = pl.program_id(2)
    mt = 128

    @pl.when(e == 0)
    def _():
        o_ref[...] = x1_ref[...]

    slot = slot_ref[...]
    nsub = slot.shape[0] // kc
    shi, slo = _split2(slot)
    krow = lax.broadcasted_iota(jnp.int32, (N_EXPERTS, 128), 0)
    sel = _bf(jnp.where(krow == e, 1.0, 0.0))
    sb = _dot(shi, sel) + _dot(slo, sel)
    lane = lax.broadcasted_iota(jnp.int32, (1, 128), 1).astype(F32)
    base = (gi * N_EXPERTS + e) * nch
    for j in range(nsub):
        cg = tblk * nsub + j
        lo = jnp.where(cg == 0, 0, ends_ref[base + jnp.maximum(cg - 1, 0)])
        hi = ends_ref[base + cg]
        rows = slice(j * kc, (j + 1) * kc)
        for m in range(ct // mt):

            @pl.when((hi > m * mt) & (lo < (m + 1) * mt))
            def _(rows=rows, m=m):
                onehot_t = _bf(jnp.where(sb[rows] == lane + float(m * mt), 1.0, 0.0))
                o_ref[rows, :] += _dot(onehot_t, ye_ref[0, 0, m * mt:(m + 1) * mt, :])


def _moe_combine(ends, x1, slot, ye, lg, ct, kc):
    tokens, d = x1.shape
    ngrp = tokens // lg
    tb = min(lg, 2048)
    nb = lg // tb
    grid_spec = pltpu.PrefetchScalarGridSpec(
        num_scalar_prefetch=1,
        grid=(ngrp, nb, N_EXPERTS),
        in_specs=[pl.BlockSpec((tb, d), lambda gi, t, e, ends: (gi * nb + t, 0)),
                  pl.BlockSpec((tb, N_EXPERTS), lambda gi, t, e, ends: (gi * nb + t, 0)),
                  pl.BlockSpec((1, 1, ct, d), lambda gi, t, e, ends: (gi, e, 0, 0))],
        out_specs=pl.BlockSpec((tb, d), lambda gi, t, e, ends: (gi * nb + t, 0)))
    return pl.pallas_call(
        functools.partial(_moe_combine_kernel, ct=ct, kc=kc, nch=lg // kc),
        grid_spec=grid_spec,
        out_shape=jax.ShapeDtypeStruct((tokens, d), F32),
        compiler_params=_cp("parallel", "parallel", "arbitrary"),
        name="moe_combine",
    )(ends, x1, slot, ye)


def _rope_tables(seq):
    t = np.arange(seq)
    n_freq = HEAD_DIM // 4
    inv = ROPE_THETA ** (-np.arange(n_freq, dtype=np.float32) / n_freq)
    ang = np.concatenate([(t // GRID_W).astype(np.float32)[:, None] * inv,
                          (t % GRID_W).astype(np.float32)[:, None] * inv], axis=-1)
    ang = jnp.asarray(ang, dtype=F32)
    cos, sin = jnp.cos(ang), jnp.sin(ang)
    cos_t = jnp.tile(jnp.concatenate([cos, cos], axis=-1), (1, SWA_HEADS))
    sin_t = jnp.tile(jnp.concatenate([-sin, sin], axis=-1), (1, SWA_HEADS))
    return cos_t, sin_t


def _blockdiag2(w):
    z = jnp.zeros_like(w[0])
    return jnp.concatenate([jnp.concatenate([w[0], z], axis=1), jnp.concatenate([z, w[1]], axis=1)], axis=0)


def _layer_params(l, ada_w, ada_b, norm1_g, norm2_g, w_in, na_q_norm, na_k_norm, na_rpb, rw_mu, rw_w0, rw_w2,
                  rw_a0, rw_a2, rw_g2, rw_k_k, rw_k_a, rw_r_k, rw_ln_g, rw_ln_b, swa_q_norm, swa_k_norm,
                  swa_sink, w_out, w_router, w_gate, w_up, w_down):
    wr = w_router[l]
    wr_hi = wr.astype(BF16)
    return {
        'n1': norm1_g[l][None], 'n2': norm2_g[l][None], 'w_in': w_in[l].astype(BF16),
        'gains': (jnp.tile(na_q_norm[l], NA_HEADS)[None], jnp.tile(na_k_norm[l], NA_HEADS)[None],
                  jnp.tile(swa_q_norm[l], SWA_HEADS)[None], jnp.tile(swa_k_norm[l], SWA_KV_HEADS)[None]),
        'bias': _na_bias_table(na_rpb[l]),
        'mu': rw_mu[l], 'w0': rw_w0[l].reshape(1, 2 * RW_W), 'w2': _blockdiag2(rw_w2[l]).astype(BF16),
        'a0': rw_a0[l].reshape(1, 2 * RW_W), 'a2': _blockdiag2(rw_a2[l]).astype(BF16),
        'g2': rw_g2[l].astype(BF16), 'k_k': rw_k_k[l][None], 'k_a': rw_k_a[l][None],
        'r_k': rw_r_k[l].reshape(1, RW_W), 'ln_g': rw_ln_g[l][None], 'ln_b': rw_ln_b[l][None],
        'sink': swa_sink[l], 'w_out': w_out[l].astype(BF16),
        'wr_hi': wr_hi, 'wr_lo': (wr - wr_hi.astype(F32)).astype(BF16),
        'wg': w_gate[l].astype(BF16), 'wu': w_up[l].astype(BF16), 'wd': w_down[l].astype(BF16),
        'ones256': _block_ones(RW_W),
    }


def _mix_and_ffn(x, mod, p, oa, oc, urw, s0_bd, scan_consts, seq, per_request_mod, group):
    r, kk, v, ld, kka, kd, g, bonus = _rw_pre(urw, p, seq)
    o2, sfin = _rw_scan(r, kk, v, ld, kka, kd, s0_bd, scan_consts, seq)
    x1, h2, aff = _finish(x, oa, oc, o2, bonus, g, mod, p, seq, per_request_mod)
    slot, slotrow, gfull, ends, kc = _topk(aff, seq, group)
    cap = EC_CAPACITY * seq // N_EXPERTS
    lg, ct = group * seq, group * cap
    ye = _moe_ffn(ends, h2, slotrow, gfull, mod, p['wg'], p['wu'], p['wd'], lg, ct, kc, per_request_mod)
    return _moe_combine(ends, x1, slot, ye, lg, ct, kc), sfin


def _context_layer(x, mod, p, ones384, scan_consts, seq):
    qa, ka, va, urw, qc, kc, vc = _proj(x, mod, p['n1'], p['w_in'], p['gains'], ones384, None, seq, False, F32)
    oa, oc = _ctx_attn(p['sink'], qa, ka, va, qc, kc, vc, seq)
    nreq = x.shape[0] // seq
    s0 = jnp.zeros((nreq, 2, RW_W, RW_W), F32)
    y, sfin = _mix_and_ffn(x, mod, p, oa, oc, urw, s0, scan_consts, seq, False, 8)
    return y, ka, va, kc, vc, sfin


def _latent_layer(x, mod, p, ones384, scan_consts, rope_tabs, seq, kx_na, vx_na, kx_swa, vx_swa, s0_bd, past):
    qa, ka, va, urw, qc, kc, vc = _proj(x, mod, p['n1'], p['w_in'], p['gains'], ones384, rope_tabs, seq, True,
                                        BF16)
    oa = _na_attn(qa, ka, va, kx_na, vx_na, p['bias'], seq, past)
    oc = _swa_attn(p['sink'], qc, kc, vc, kx_swa, vx_swa, seq, past)
    y, _ = _mix_and_ffn(x, mod, p, oa, oc, urw, s0_bd, scan_consts, seq, True, 1)
    return y


def _heads_first(z, nreq, seq, heads):
    return z.reshape(nreq, seq, heads, HEAD_DIM).transpose(0, 2, 1, 3)


def _tokens_first(z):
    b, h, n, dh = z.shape
    return z.transpose(0, 2, 1, 3).reshape(b * n, h * dh)


def kernel(x_prompt, x_sample, cache_na_k, cache_na_v, cache_swa_k, cache_swa_v, state_rwkv, c, c_ctx, ada_w, ada_b, norm1_g, norm2_g, w_in, na_q_norm, na_k_norm, na_rpb, rw_mu, rw_w0, rw_w2, rw_a0, rw_a2, rw_g2, rw_k_k, rw_k_a, rw_r_k, rw_ln_g, rw_ln_b, swa_q_norm, swa_k_norm, swa_sink, w_out, w_router, w_gate, w_up, w_down):
    nb, seq, d = x_prompt.shape
    db, dseq, _ = x_sample.shape
    depth = ada_w.shape[0]
    past = cache_na_k.shape[3]
    cond = jnp.concatenate([c, c_ctx[None], jnp.zeros((16 - db - 1, d), F32)], axis=0)
    mod_all = _adaln(cond, ada_w, ada_b).reshape(depth, 16, 6, d)
    ones384 = _block_ones(NA_W)
    scan_consts = _rw_masks()
    rope_tabs = _rope_tables(dseq)
    xp = x_prompt.reshape(nb * seq, d)
    xs = x_sample.reshape(db * dseq, d)
    new_ka, new_va, new_kc, new_vc, new_s = [], [], [], [], []
    for l in range(depth):
        p = _layer_params(l, ada_w, ada_b, norm1_g, norm2_g, w_in, na_q_norm, na_k_norm, na_rpb, rw_mu, rw_w0,
                          rw_w2, rw_a0, rw_a2, rw_g2, rw_k_k, rw_k_a, rw_r_k, rw_ln_g, rw_ln_b, swa_q_norm,
                          swa_k_norm, swa_sink, w_out, w_router, w_gate, w_up, w_down)
        mod_ctx = mod_all[l, db:db + 1]
        mod_lat = mod_all[l, 0:db]
        xp, ka, va, kc, vc, sfin = _context_layer(xp, mod_ctx, p, ones384, scan_consts, seq)
        new_ka.append(_heads_first(ka, nb, seq, NA_HEADS))
        new_va.append(_heads_first(va, nb, seq, NA_HEADS))
        new_kc.append(_heads_first(kc, nb, seq, SWA_KV_HEADS))
        new_vc.append(_heads_first(vc, nb, seq, SWA_KV_HEADS))
        new_s.append(_blockdiag_to_state(sfin))
        xs = _latent_layer(xs, mod_lat, p, ones384, scan_consts, rope_tabs, dseq,
                           _tokens_first(cache_na_k[:, l]), _tokens_first(cache_na_v[:, l]),
                           _tokens_first(cache_swa_k[:, l]), _tokens_first(cache_swa_v[:, l]),
                           _state_to_blockdiag(state_rwkv[:, l]), past)
    return (xp.reshape(nb, seq, d), xs.reshape(db, dseq, d), jnp.stack(new_ka, axis=1),
            jnp.stack(new_va, axis=1), jnp.stack(new_kc, axis=1), jnp.stack(new_vc, axis=1),
            jnp.stack(new_s, axis=1))
```

```python
import functools

import numpy as np
import jax
import jax.numpy as jnp
from jax import lax
from jax.experimental import pallas as pl
from jax.experimental.pallas import tpu as pltpu

F32 = jnp.float32
BF16 = jnp.bfloat16

HEAD_DIM = 64
GRID_W = 64
NA_HEADS = 6
NA_KH = 8
NA_KW = 16
RW_HEADS = 4
SWA_HEADS = 6
SWA_KV_HEADS = 2
SWA_WINDOW = 128
N_EXPERTS = 16
EC_CAPACITY = 2
ROPE_THETA = 10000.0
NORM_EPS = 1e-6
GN_EPS = 64e-5
NEG_INF = -1e30
NA_UNROLL = 4
SWA_UNROLL = 2
RW_CHUNK = 64
RW_W = RW_HEADS * HEAD_DIM
NA_W = NA_HEADS * HEAD_DIM
SWA_W = SWA_HEADS * HEAD_DIM
SWA_KV_W = SWA_KV_HEADS * HEAD_DIM
RW_IN_W = 1152
VMEM_LIMIT = 56 * 1024 * 1024


def _cp(*sem):
    return pltpu.CompilerParams(dimension_semantics=sem, vmem_limit_bytes=VMEM_LIMIT)


def _bf(x):
    return x.astype(BF16)


def _dot(a, b):
    return jnp.dot(a, b, preferred_element_type=F32)


def _dot_nt(a, b):
    return lax.dot_general(a, b, (((1,), (1,)), ((), ())), preferred_element_type=F32)


def _dot_tn(a, b):
    return lax.dot_general(a, b, (((0,), (0,)), ((), ())), preferred_element_type=F32)


def _split2(x):
    hi = x.astype(BF16)
    lo = (x - hi.astype(F32)).astype(BF16)
    return hi, lo


def _split3(x):
    hi = x.astype(BF16)
    r1 = x - hi.astype(F32)
    mid = r1.astype(BF16)
    lo = (r1 - mid.astype(F32)).astype(BF16)
    return hi, mid, lo


def _dot2(a, b_bf):
    hi, lo = _split2(a)
    return _dot(hi, b_bf) + _dot(lo, b_bf)


def _sigmoid(x):
    return 1.0 / (1.0 + jnp.exp(-x))


def _block_ones(width):
    i = np.arange(width) // HEAD_DIM
    return jnp.asarray((i[:, None] == i[None, :]).astype(np.float32), dtype=BF16)


def _row_tile(seq):
    return 512 if seq % 512 == 0 else 256


def _full(shape):
    return pl.BlockSpec(shape, lambda *_: (0,) * len(shape))


def _adaln_kernel(c_ref, w_ref, b_ref, o_ref):
    c = c_ref[...]
    s = c * _sigmoid(c)
    shi, slo = _split2(s)
    whi, wlo = _split2(w_ref[0])
    o_ref[0] = _dot(shi, whi) + _dot(slo, whi) + _dot(shi, wlo) + b_ref[0]


def _adaln(cond, ada_w, ada_b):
    nl, d, n6 = ada_w.shape
    tn = 1536
    rows = cond.shape[0]
    return pl.pallas_call(
        _adaln_kernel,
        grid=(nl, n6 // tn),
        in_specs=[pl.BlockSpec((rows, d), lambda l, j: (0, 0)),
                  pl.BlockSpec((1, d, tn), lambda l, j: (l, 0, j)),
                  pl.BlockSpec((1, 1, tn), lambda l, j: (l, 0, j))],
        out_specs=pl.BlockSpec((1, rows, tn), lambda l, j: (l, 0, j)),
        out_shape=jax.ShapeDtypeStruct((nl, rows, n6), F32),
        compiler_params=_cp("parallel", "parallel"),
        name="adaln",
    )(cond, ada_w, ada_b.reshape(nl, 1, n6))


def _head_norm(z, gain, ones_bf):
    ms = _dot2(z * z, ones_bf) * (1.0 / HEAD_DIM)
    return z * lax.rsqrt(ms + NORM_EPS) * gain


def _rope(z, cos, sin_signed):
    w = z.shape[1]
    lane = lax.broadcasted_iota(jnp.int32, z.shape, 1)
    first = (lane % HEAD_DIM) < (HEAD_DIM // 2)
    swapped = jnp.where(first, pltpu.roll(z, w - HEAD_DIM // 2, 1), pltpu.roll(z, HEAD_DIM // 2, 1))
    return z * cos + swapped * sin_signed


def _proj_kernel(*refs, rope):
    if rope:
        (x_ref, mod_ref, n1_ref, w_ref, gqa_ref, gka_ref, gqc_ref, gkc_ref, ones_ref, cos_ref, sin_ref,
         qa_ref, ka_ref, va_ref, urw_ref, qc_ref, kc_ref, vc_ref) = refs
    else:
        (x_ref, mod_ref, n1_ref, w_ref, gqa_ref, gka_ref, gqc_ref, gkc_ref, ones_ref,
         qa_ref, ka_ref, va_ref, urw_ref, qc_ref, kc_ref, vc_ref) = refs
    x = x_ref[...]
    sh1 = mod_ref[0, 0:1, :]
    sc1 = mod_ref[0, 1:2, :]
    ms = jnp.mean(x * x, axis=-1, keepdims=True)
    h = x * lax.rsqrt(ms + NORM_EPS) * n1_ref[...] * (1.0 + sc1) + sh1
    u = _dot(_bf(h), w_ref[...])
    o0 = 0
    o1 = NA_W
    o2 = 2 * NA_W
    o3 = 3 * NA_W
    o4 = o3 + RW_IN_W
    o5 = o4 + SWA_W
    o6 = o5 + SWA_KV_W
    ones = ones_ref[...]
    ones_kv = ones_ref[0:SWA_KV_W, 0:SWA_KV_W]
    qa = _head_norm(u[:, o0:o1], gqa_ref[...], ones)
    ka = _head_norm(u[:, o1:o2], gka_ref[...], ones)
    qc = _head_norm(u[:, o4:o5], gqc_ref[...], ones)
    kc = _head_norm(u[:, o5:o6], gkc_ref[...], ones_kv)
    if rope:
        qc = _rope(qc, cos_ref[...], sin_ref[...])
        kc = _rope(kc, cos_ref[:, 0:SWA_KV_W], sin_ref[:, 0:SWA_KV_W])
    qa_ref[...] = qa.astype(qa_ref.dtype)
    ka_ref[...] = ka.astype(ka_ref.dtype)
    va_ref[...] = u[:, o2:o3].astype(va_ref.dtype)
    urw_ref[...] = u[:, o3:o4]
    qc_ref[...] = qc.astype(qc_ref.dtype)
    kc_ref[...] = kc.astype(kc_ref.dtype)
    vc_ref[...] = u[:, o6:].astype(vc_ref.dtype)


def _proj(x, mod, n1, w_in_bf, gains, ones384, rope_tabs, seq, per_request_mod, qkv_dtype):
    tokens, d = x.shape
    tm = _row_tile(seq)
    tiles_per_req = seq // tm
    in_w = w_in_bf.shape[1]
    rope = rope_tabs is not None
    mod_map = (lambda i: (i // tiles_per_req, 0, 0)) if per_request_mod else (lambda i: (0, 0, 0))
    row = lambda w: pl.BlockSpec((tm, w), lambda i: (i, 0))
    in_specs = [row(d), pl.BlockSpec((1, 6, d), mod_map), _full((1, d)), _full((d, in_w)),
                _full((1, NA_W)), _full((1, NA_W)), _full((1, SWA_W)), _full((1, SWA_KV_W)),
                _full((NA_W, NA_W))]
    args = [x, mod, n1, w_in_bf, *gains, ones384]
    if rope:
        tab = pl.BlockSpec((tm, SWA_W), lambda i: (i % tiles_per_req, 0))
        in_specs += [tab, tab]
        args += list(rope_tabs)
    widths = [NA_W, NA_W, NA_W, RW_IN_W, SWA_W, SWA_KV_W, SWA_KV_W]
    dtypes = [qkv_dtype, qkv_dtype, qkv_dtype, F32, qkv_dtype, qkv_dtype, qkv_dtype]
    return pl.pallas_call(
        functools.partial(_proj_kernel, rope=rope),
        grid=(tokens // tm,),
        in_specs=in_specs,
        out_specs=[row(w) for w in widths],
        out_shape=[jax.ShapeDtypeStruct((tokens, w), dt) for w, dt in zip(widths, dtypes)],
        compiler_params=_cp("parallel"),
        name="proj",
    )(*args)


def _half_masks(width=2 * HEAD_DIM):
    lane = lax.broadcasted_iota(jnp.int32, (1, width), 1)
    return lane < HEAD_DIM, lane >= HEAD_DIM


def _swap_halves(z):
    return pltpu.roll(z, HEAD_DIM, 1)


def _ctx_attn_kernel(sink_ref, qa_ref, ka_ref, va_ref, qc_ref, kc_ref, vc_ref, oa_ref, oc_ref):
    scale = HEAD_DIM ** -0.5
    m0, m1 = _half_masks()
    masks = (m0, m1)
    for pair in range(NA_HEADS // 2):
        sl = slice(pair * 128, (pair + 1) * 128)
        qp = qa_ref[:, sl].astype(F32) * scale
        kp = _bf(ka_ref[:, sl])
        vp = _bf(va_ref[:, sl])
        outs = []
        for half in range(2):
            qm = _bf(jnp.where(masks[half], qp, 0.0))
            s = _dot_nt(qm, kp)
            m = jnp.max(s, axis=-1, keepdims=True)
            e = jnp.exp(s - m)
            l = jnp.sum(e, axis=-1, keepdims=True)
            outs.append(_dot(_bf(e), vp) / l)
        oa_ref[:, sl] = jnp.where(m0, outs[0], outs[1])
    kc = _bf(kc_ref[...])
    vc = _bf(vc_ref[...])
    group = SWA_HEADS // SWA_KV_HEADS
    for pair in range(SWA_HEADS // 2):
        sl = slice(pair * 128, (pair + 1) * 128)
        qp = qc_ref[:, sl].astype(F32) * scale
        outs = []
        for half in range(2):
            h = 2 * pair + half
            g = h // group
            qh = qp if g == half else _swap_halves(qp)
            qm = _bf(jnp.where(masks[g], qh, 0.0))
            s = _dot_nt(qm, kc)
            sk = sink_ref[h]
            m = jnp.maximum(jnp.max(s, axis=-1, keepdims=True), sk)
            e = jnp.exp(s - m)
            l = jnp.sum(e, axis=-1, keepdims=True) + jnp.exp(sk - m)
            o = _dot(_bf(e), vc) / l
            outs.append(o if g == half else _swap_halves(o))
        oc_ref[:, sl] = jnp.where(m0, outs[0], outs[1])


def _ctx_attn(sink, qa, ka, va, qc, kc, vc, seq):
    tokens = qa.shape[0]
    blk = lambda w: pl.BlockSpec((seq, w), lambda b: (b, 0))
    return pl.pallas_call(
        _ctx_attn_kernel,
        grid=(tokens // seq,),
        in_specs=[pl.BlockSpec(memory_space=pltpu.SMEM), blk(NA_W), blk(NA_W), blk(NA_W), blk(SWA_W),
                  blk(SWA_KV_W), blk(SWA_KV_W)],
        out_specs=[blk(NA_W), blk(SWA_W)],
        out_shape=[jax.ShapeDtypeStruct((tokens, NA_W), F32), jax.ShapeDtypeStruct((tokens, SWA_W), F32)],
        compiler_params=_cp("parallel"),
        name="ctx_attn",
    )(sink, qa, ka, va, qc, kc, vc)


def _na_bias_kernel(rpb_ref, o_ref):
    h = pl.program_id(0)
    nrow = 2 * NA_KH - 1
    ncol = 2 * NA_KW - 1
    width = NA_KH * GRID_W
    shape = (GRID_W, width)
    lane = lax.broadcasted_iota(jnp.int32, shape, 1)
    qc = lax.broadcasted_iota(jnp.int32, shape, 0)
    kc = lane % GRID_W
    c_start = jnp.clip(qc - NA_KW // 2, 0, GRID_W - NA_KW)
    ok = (kc >= c_start) & (kc < c_start + NA_KW)
    d_col = jnp.clip(kc - qc, 1 - NA_KW, NA_KW - 1) + NA_KW - 1
    key_row = lax.broadcasted_iota(jnp.int32, (1, width), 1) // GRID_W

    def case_body(case, carry):
        acc = jnp.zeros(shape, F32)
        for dc in range(ncol):
            val = jnp.zeros((1, width), F32)
            for i in range(NA_KH):
                val = jnp.where(key_row == i, rpb_ref[(h * nrow + case + i) * ncol + dc], val)
            acc = jnp.where(d_col == dc, val, acc)
        o_ref[0, pl.ds(case, 1)] = jnp.where(ok, acc, NEG_INF)[None]
        return carry

    lax.fori_loop(0, NA_KH, case_body, 0)


def _na_bias_table(rpb):
    nh = rpb.shape[0]
    return pl.pallas_call(
        _na_bias_kernel,
        grid=(nh,),
        in_specs=[pl.BlockSpec(memory_space=pltpu.SMEM)],
        out_specs=pl.BlockSpec((1, NA_KH, GRID_W, NA_KH * GRID_W), lambda h: (h // 2, 0, h % 2, 0)),
        out_shape=jax.ShapeDtypeStruct((nh // 2, NA_KH, 2 * GRID_W, NA_KH * GRID_W), F32),
        compiler_params=_cp("parallel"),
        name="na_bias",
    )(rpb.reshape(-1))


def _na_kernel(q_ref, k_ref, v_ref, kx_ref, vx_ref, bias_ref, o_ref, *, rows):
    scale = HEAD_DIM ** -0.5
    m0, m1 = _half_masks()
    kx = _bf(kx_ref[...])
    vx = _bf(vx_ref[...])

    def body(it, carry):
        us = range(NA_UNROLL)
        r = [it * NA_UNROLL + u for u in us]
        rs = [jnp.clip(r[u] - NA_KH // 2, 0, rows - NA_KH) for u in us]
        case = [rs[u] - r[u] + NA_KH - 1 for u in us]
        q0 = [pl.multiple_of(r[u] * GRID_W, GRID_W) for u in us]
        k0 = [pl.multiple_of(rs[u] * GRID_W, GRID_W) for u in us]
        qp = [q_ref[pl.ds(q0[u], GRID_W), :].astype(F32) * scale for u in us]
        kw = [_bf(k_ref[pl.ds(k0[u], NA_KH * GRID_W), :]) for u in us]
        vw = [_bf(v_ref[pl.ds(k0[u], NA_KH * GRID_W), :]) for u in us]
        q2 = [_bf(jnp.concatenate([jnp.where(m0, qp[u], 0.0), jnp.where(m1, qp[u], 0.0)], axis=0)) for u in us]
        sw = [_dot_nt(q2[u], kw[u]) + bias_ref[0, pl.ds(case[u], 1)][0] for u in us]
        sx = [_dot_nt(q2[u], kx) for u in us]
        m = [jnp.maximum(jnp.max(sw[u], axis=-1, keepdims=True), jnp.max(sx[u], axis=-1, keepdims=True))
             for u in us]
        ew = [jnp.exp(sw[u] - m[u]) for u in us]
        ex = [jnp.exp(sx[u] - m[u]) for u in us]
        l = [jnp.sum(ew[u], axis=-1, keepdims=True) + jnp.sum(ex[u], axis=-1, keepdims=True) for u in us]
        o = [(_dot(_bf(ew[u]), vw[u]) + _dot(_bf(ex[u]), vx)) / l[u] for u in us]
        for u in us:
            o_ref[pl.ds(q0[u], GRID_W), :] = jnp.where(m0, o[u][0:GRID_W], o[u][GRID_W:])
        return carry

    lax.fori_loop(0, rows // NA_UNROLL, body, 0)


def _na_attn(q, k, v, kx, vx, bias, seq, past):
    tokens = q.shape[0]
    nb = tokens // seq
    rows = seq // GRID_W
    blk = pl.BlockSpec((seq, 128), lambda b, p: (b, p))
    cblk = pl.BlockSpec((past, 128), lambda b, p: (b, p))
    return pl.pallas_call(
        functools.partial(_na_kernel, rows=rows),
        grid=(nb, NA_HEADS // 2),
        in_specs=[blk, blk, blk, cblk, cblk,
                  pl.BlockSpec((1, NA_KH, 2 * GRID_W, NA_KH * GRID_W), lambda b, p: (p, 0, 0, 0))],
        out_specs=blk,
        out_shape=jax.ShapeDtypeStruct((tokens, NA_W), F32),
        compiler_params=_cp("parallel", "parallel"),
        name="na_attn",
    )(q, k, v, kx, vx, bias)


def _swa_kernel(sink_ref, q_ref, k_ref, v_ref, kx_ref, vx_ref, o_ref, *, seq):
    scale = HEAD_DIM ** -0.5
    blk = SWA_WINDOW
    m0, m1 = _half_masks()
    masks = (m0, m1)
    kx = _bf(kx_ref[...])
    vx = _bf(vx_ref[...])
    group = SWA_HEADS // SWA_KV_HEADS

    sk = []
    for g in range(SWA_KV_HEADS):
        sk.append(jnp.concatenate([jnp.full((blk, 1), sink_ref[h], F32) for h in range(g * group, (g + 1) * group)],
                                  axis=0))

    def body(it, carry):
        us = range(SWA_UNROLL)
        cs = [(u, g) for u in us for g in range(SWA_KV_HEADS)]
        nb = [it * SWA_UNROLL + u for u in us]
        ks = [pl.multiple_of(jnp.clip((nb[u] - 1) * blk, 0, seq - 3 * blk), blk) for u in us]
        q0 = [pl.multiple_of(nb[u] * blk, blk) for u in us]
        kw = [_bf(k_ref[pl.ds(ks[u], 3 * blk), :]) for u in us]
        vw = [_bf(v_ref[pl.ds(ks[u], 3 * blk), :]) for u in us]
        ok = []
        for u in us:
            qpos = q0[u] + lax.broadcasted_iota(jnp.int32, (group * blk, 1), 0) % blk
            kpos = ks[u] + lax.broadcasted_iota(jnp.int32, (1, 3 * blk), 1)
            ok.append(jnp.abs(qpos - kpos) <= SWA_WINDOW)
        qg = {}
        for u in us:
            pairs = [q_ref[pl.ds(q0[u], blk), p * 128:(p + 1) * 128].astype(F32) * scale
                     for p in range(SWA_HEADS // 2)]
            for g in range(SWA_KV_HEADS):
                qs = []
                for h in range(g * group, (g + 1) * group):
                    qh = pairs[h // 2] if h % 2 == g else _swap_halves(pairs[h // 2])
                    qs.append(jnp.where(masks[g], qh, 0.0))
                qg[u, g] = _bf(jnp.concatenate(qs, axis=0))
        sw = {c: jnp.where(ok[c[0]], _dot_nt(qg[c], kw[c[0]]), NEG_INF) for c in cs}
        sx = {c: _dot_nt(qg[c], kx) for c in cs}
        m = {c: jnp.maximum(jnp.maximum(jnp.max(sw[c], axis=-1, keepdims=True),
                                        jnp.max(sx[c], axis=-1, keepdims=True)), sk[c[1]]) for c in cs}
        ew = {c: jnp.exp(sw[c] - m[c]) for c in cs}
        ex = {c: jnp.exp(sx[c] - m[c]) for c in cs}
        l = {c: jnp.sum(ew[c], axis=-1, keepdims=True) + jnp.sum(ex[c], axis=-1, keepdims=True)
             + jnp.exp(sk[c[1]] - m[c]) for c in cs}
        o = {c: (_dot(_bf(ew[c]), vw[c[0]]) + _dot(_bf(ex[c]), vx)) / l[c] for c in cs}
        for u in us:
            head_out = []
            for g in range(SWA_KV_HEADS):
                for i in range(group):
                    h = g * group + i
                    oh = o[u, g][i * blk:(i + 1) * blk]
                    head_out.append(oh if h % 2 == g else _swap_halves(oh))
            for p in range(SWA_HEADS // 2):
                o_ref[pl.ds(q0[u], blk), p * 128:(p + 1) * 128] = jnp.where(m0, head_out[2 * p], head_out[2 * p + 1])
        return carry

    lax.fori_loop(0, seq // (blk * SWA_UNROLL), body, 0)


def _swa_attn(sink, q, k, v, kx, vx, seq, past):
    tokens = q.shape[0]
    blk = lambda w: pl.BlockSpec((seq, w), lambda b: (b, 0))
    cblk = pl.BlockSpec((past, SWA_KV_W), lambda b: (b, 0))
    return pl.pallas_call(
        functools.partial(_swa_kernel, seq=seq),
        grid=(tokens // seq,),
        in_specs=[pl.BlockSpec(memory_space=pltpu.SMEM), blk(SWA_W), blk(SWA_KV_W), blk(SWA_KV_W), cblk, cblk],
        out_specs=blk(SWA_W),
        out_shape=jax.ShapeDtypeStruct((tokens, SWA_W), F32),
        compiler_params=_cp("parallel"),
        name="swa_attn",
    )(sink, q, k, v, kx, vx)


def _rw_pre_kernel(u_ref, up_ref, un_ref, mu_ref, w0_ref, w2_ref, a0_ref, a2_ref, g2_ref, kk_ref_, ka_ref_,
                   rk_ref, ones_ref, r_o, kk_o, v_o, ld_o, kka_o, kd_o, g_o, bonus_o, *, tiles_per_req):
    i = pl.program_id(0)
    u = u_ref[...]
    tm = u.shape[0]
    rowi = lax.broadcasted_iota(jnp.int32, (tm, 1), 0)
    first = (i % tiles_per_req) == 0
    last = (i % tiles_per_req) == tiles_per_req - 1
    prev_row = jnp.where(first, 0.0, up_ref[7:8, :])
    next_row = jnp.where(last, 0.0, un_ref[0:1, :])
    prev = jnp.where(rowi == 0, prev_row, pltpu.roll(u, 1, 0))
    nxt = jnp.where(rowi == tm - 1, next_row, pltpu.roll(u, tm - 1, 0))
    us = u + mu_ref[0:1, :] * (prev - u) + mu_ref[1:2, :] * (nxt - u)
    r = us[:, 0:RW_W]
    k = us[:, RW_W:2 * RW_W]
    v = us[:, 2 * RW_W:3 * RW_W]
    wl = us[:, 3 * RW_W:3 * RW_W + 128]
    al = us[:, 3 * RW_W + 128:3 * RW_W + 256]
    gl = us[:, 3 * RW_W + 256:3 * RW_W + 384]
    z = -(w0_ref[...] + _dot(_bf(jnp.tanh(wl)), w2_ref[...]))
    softplus = jnp.maximum(z, 0.0) + jnp.log(1.0 + jnp.exp(-jnp.abs(z)))
    w = -softplus - 0.5
    ld = -jnp.exp(w)
    a = _sigmoid(a0_ref[...] + _dot(_bf(al), a2_ref[...]))
    g = _dot(_bf(_sigmoid(gl)), g2_ref[...])
    ones = ones_ref[...]
    kkr = k * kk_ref_[...]
    kk = kkr * lax.rsqrt(jnp.maximum(_dot2(kkr * kkr, ones), 1e-24))
    k_a = ka_ref_[...]
    kd_f = k * (1.0 + (a[:, 0:RW_W] - 1.0) * k_a)
    kd_b = k * (1.0 + (a[:, RW_W:] - 1.0) * k_a)
    r_o[...] = r
    kk_o[...] = kk
    v_o[...] = v
    ld_o[...] = ld
    kka_o[:, 0:RW_W] = kk * a[:, 0:RW_W]
    kka_o[:, RW_W:] = kk * a[:, RW_W:]
    kd_o[:, 0:RW_W] = kd_f
    kd_o[:, RW_W:] = kd_b
    g_o[...] = g
    bonus_o[...] = _dot2(r * (kd_f + kd_b) * rk_ref[...], ones) * v


def _rw_pre(urw, p, seq):
    tokens = urw.shape[0]
    tm = 256
    tpr = seq // tm
    nt = tokens // tm
    r8 = tm // 8
    row = lambda w: pl.BlockSpec((tm, w), lambda i: (i, 0))
    in_specs = [row(RW_IN_W),
                pl.BlockSpec((8, RW_IN_W), lambda i: (jnp.maximum(i * r8 - 1, 0), 0)),
                pl.BlockSpec((8, RW_IN_W), lambda i: (jnp.minimum((i + 1) * r8, nt * r8 - 1), 0)),
                _full((2, RW_IN_W)), _full((1, 2 * RW_W)), _full((128, 2 * RW_W)), _full((1, 2 * RW_W)),
                _full((128, 2 * RW_W)), _full((128, RW_W)), _full((1, RW_W)), _full((1, RW_W)),
                _full((1, RW_W)), _full((RW_W, RW_W))]
    widths = [RW_W, RW_W, RW_W, 2 * RW_W, 2 * RW_W, 2 * RW_W, RW_W, RW_W]
    return pl.pallas_call(
        functools.partial(_rw_pre_kernel, tiles_per_req=tpr),
        grid=(nt,),
        in_specs=in_specs,
        out_specs=[row(w) for w in widths],
        out_shape=[jax.ShapeDtypeStruct((tokens, w), F32) for w in widths],
        compiler_params=_cp("parallel"),
        name="rw_pre",
    )(urw, urw, urw, p['mu'], p['w0'], p['w2'], p['a0'], p['a2'], p['g2'], p['k_k'], p['k_a'], p['r_k'],
      p['ones256'])


def _rw_masks():
    t = RW_CHUNK
    n = RW_HEADS * t
    tt = np.arange(t)[:, None]
    ss = (np.arange(n) % t)[None, :]
    before = np.stack([ss < tt, ss > tt])
    diag = (ss == tt)
    strict = before.astype(np.float32)
    incl = (before | diag[None]).astype(np.float32)
    eye = diag.astype(np.float32)
    ti = np.arange(t)
    tri = np.stack([ti[None, :] <= ti[:, None], ti[None, :] >= ti[:, None]]).astype(np.float32)
    hd = np.arange(n) // t
    same = (hd[:, None] == hd[None, :]).astype(np.float32)
    return (jnp.asarray(strict), jnp.asarray(incl), jnp.asarray(tri, dtype=BF16), jnp.asarray(same),
            jnp.asarray(eye))


def _rw_scan_kernel(r_ref, kk_ref, v_ref, ld_ref, kka_ref, kd_ref, s0_ref, strict_ref, incl_ref, tri_ref,
                    same_ref, eye_ref, o_ref, sfin_ref, s_scr, *, nsub):
    d = pl.program_id(1)
    c = pl.program_id(2)
    t = RW_CHUNK
    n = RW_HEADS * t

    @pl.when(c == 0)
    def _():
        s_scr[...] = s0_ref[0, 0]

    strict = strict_ref[0]
    incl = incl_ref[0]
    tri = tri_ref[0]
    eye = eye_ref[...]
    same = same_ref[...]
    same_bf = _bf(same)

    def bd(x):
        return jnp.concatenate([_bf(x)] * RW_HEADS, axis=0) * same_bf

    js = range(nsub)
    rows = [pl.ds(pl.multiple_of((j + d * (nsub - 1 - 2 * j)) * t, t), t) for j in js]
    ld = [ld_ref[rows[j], :] for j in js]
    cum = []
    for j in js:
        lhi, lmid, llo = _split3(ld[j])
        cum.append(_dot(tri, lhi) + _dot(tri, lmid) + _dot(tri, llo))
    cend = [jnp.sum(ld[j], axis=0, keepdims=True) for j in js]
    kka = [kka_ref[rows[j], :] for j in js]
    kd = [kd_ref[rows[j], :] for j in js]
    v = [v_ref[rows[j], :] for j in js]
    at = [-kk_ref[rows[j], :] * jnp.exp(cum[j] - ld[j]) for j in js]
    rt = [r_ref[rows[j], :] * jnp.exp(cum[j]) for j in js]
    e_inv = [jnp.exp(-cum[j]) for j in js]
    aa = [_dot_nt(_bf(jnp.concatenate([at[j], rt[j]], axis=0)),
                  jnp.concatenate([bd(kka[j] * e_inv[j]), bd(kd[j] * e_inv[j])], axis=0)) for j in js]
    a_ab = [aa[j][0:t, 0:n] * strict for j in js]
    x = [eye + a_ab[j] for j in js]
    pw = a_ab
    for _ in range(5):
        pw = [_dot(_bf(pw[j]), bd(pw[j])) for j in js]
        x = [x[j] + _dot(_bf(x[j]), bd(pw[j])) for j in js]
    v_bd = [bd(v[j]) for j in js]
    wv = [_dot(_bf(aa[j][0:t, n:] * strict), v_bd[j]) for j in js]
    mu = [_dot(_bf(x[j]), jnp.concatenate([bd(at[j]), bd(wv[j])], axis=1)) for j in js]
    m1 = [mu[j][:, 0:n] for j in js]
    u0 = [mu[j][:, n:] for j in js]
    e_end = [jnp.exp(cend[j] - cum[j]) for j in js]
    bend = [_bf(kka[j] * e_end[j]) for j in js]
    g = [_bf(_dot_tn(_bf(m1[j]), bend[j]) * same) for j in js]
    cst = [_dot_tn(_bf(jnp.concatenate([u0[j], v[j]], axis=0)),
                   jnp.concatenate([bend[j], _bf(kd[j] * e_end[j])], axis=0)) * same for j in js]
    qo = [_dot(_bf(aa[j][t:, 0:n] * incl), jnp.concatenate([bd(m1[j]), bd(u0[j])], axis=1)) for j in js]
    q = [_bf(rt[j] + qo[j][:, 0:n]) for j in js]
    o0 = [qo[j][:, n:] + _dot(_bf(aa[j][t:, n:] * incl), v_bd[j]) for j in js]

    s = s_scr[...]
    for j in js:
        s_bf = _bf(s)
        o_ref[0, rows[j], :] = _dot_nt(q[j], s_bf) + o0[j]
        s = s * jnp.exp(cend[j]) + _dot(s_bf, g[j]) + cst[j]
    s_scr[...] = s

    @pl.when(c == pl.num_programs(2) - 1)
    def _():
        sfin_ref[0, 0] = s


def _rw_scan(r, kk, v, ld, kka, kd, s0_bd, consts, seq):
    tokens = r.shape[0]
    nreq = tokens // seq
    tb = 256
    nblk = seq // tb
    nsub = tb // RW_CHUNK
    n = RW_HEADS * RW_CHUNK
    cc = lambda d, c: c + d * (nblk - 1 - 2 * c)
    shared = pl.BlockSpec((tb, RW_W), lambda b, d, c: (b * nblk + cc(d, c), 0))
    dirw = pl.BlockSpec((tb, RW_W), lambda b, d, c: (b * nblk + cc(d, c), d))
    strict, incl, tri, same, eye = consts
    return pl.pallas_call(
        functools.partial(_rw_scan_kernel, nsub=nsub),
        grid=(nreq, 2, nblk),
        in_specs=[shared, shared, shared, dirw, dirw, dirw,
                  pl.BlockSpec((1, 1, n, n), lambda b, d, c: (b, d, 0, 0)),
                  pl.BlockSpec((1, RW_CHUNK, n), lambda b, d, c: (d, 0, 0)),
                  pl.BlockSpec((1, RW_CHUNK, n), lambda b, d, c: (d, 0, 0)),
                  pl.BlockSpec((1, RW_CHUNK, RW_CHUNK), lambda b, d, c: (d, 0, 0)),
                  _full((n, n)), _full((RW_CHUNK, n))],
        out_specs=[pl.BlockSpec((1, tb, RW_W), lambda b, d, c: (d, b * nblk + cc(d, c), 0)),
                   pl.BlockSpec((1, 1, n, n), lambda b, d, c: (b, d, 0, 0))],
        out_shape=[jax.ShapeDtypeStruct((2, tokens, RW_W), F32), jax.ShapeDtypeStruct((nreq, 2, n, n), F32)],
        scratch_shapes=[pltpu.VMEM((n, n), F32)],
        compiler_params=_cp("parallel", "parallel", "arbitrary"),
        name="rw_scan",
    )(r, kk, v, ld, kka, kd, s0_bd, strict, incl, tri, same, eye)


def _state_to_blockdiag(s):
    b = s.shape[0]
    eye = jnp.eye(RW_HEADS, dtype=s.dtype)
    out = s[:, :, :, :, None, :] * eye[None, None, :, None, :, None]
    return out.reshape(b, 2, RW_W, RW_W)


def _blockdiag_to_state(sbd):
    b = sbd.shape[0]
    s = sbd.reshape(b, 2, RW_HEADS, HEAD_DIM, RW_HEADS, HEAD_DIM)
    idx = jnp.arange(RW_HEADS)
    return jnp.transpose(s[:, :, idx, :, idx, :], (1, 2, 0, 3, 4))


def _finish_kernel(x_ref, oa_ref, oc_ref, o2_ref, bonus_ref, g_ref, mod_ref, wout_ref, lng_ref, lnb_ref,
                   n2_ref, wr_hi_ref, wr_lo_ref, ones_ref, x1_ref, h2_ref, aff_ref):
    ones = ones_ref[...]
    y = o2_ref[0] + o2_ref[1]
    mu = _dot2(y, ones) * (1.0 / HEAD_DIM)
    yc = y - mu
    var = _dot2(yc * yc, ones) * (1.0 / HEAD_DIM)
    yn = yc * lax.rsqrt(var + GN_EPS) * lng_ref[...] + lnb_ref[...]
    ob = (yn + bonus_ref[...]) * g_ref[...]
    mixin = jnp.concatenate([_bf(oa_ref[...]), _bf(ob), _bf(oc_ref[...])], axis=1)
    mix = _dot(mixin, wout_ref[...])
    g1 = mod_ref[0, 2:3, :]
    sh2 = mod_ref[0, 3:4, :]
    sc2 = mod_ref[0, 4:5, :]
    x1 = x_ref[...] + g1 * mix
    ms = jnp.mean(x1 * x1, axis=-1, keepdims=True)
    h2 = x1 * lax.rsqrt(ms + NORM_EPS) * n2_ref[...] * (1.0 + sc2) + sh2
    x1_ref[...] = x1
    h2_ref[...] = _bf(h2)
    hhi, hlo = _split2(h2)
    logits = _dot(hhi, wr_hi_ref[...]) + _dot(hlo, wr_hi_ref[...]) + _dot(hhi, wr_lo_ref[...])
    m = jnp.max(logits, axis=-1, keepdims=True)
    e = jnp.exp(logits - m)
    aff_ref[...] = e / jnp.sum(e, axis=-1, keepdims=True)


def _finish(x, oa, oc, o2, bonus, g, mod, p, seq, per_request_mod):
    tokens, d = x.shape
    tm = _row_tile(seq)
    tpr = seq // tm
    mod_map = (lambda i: (i // tpr, 0, 0)) if per_request_mod else (lambda i: (0, 0, 0))
    row = lambda w: pl.BlockSpec((tm, w), lambda i: (i, 0))
    return pl.pallas_call(
        _finish_kernel,
        grid=(tokens // tm,),
        in_specs=[row(d), row(NA_W), row(SWA_W), pl.BlockSpec((2, tm, RW_W), lambda i: (0, i, 0)), row(RW_W),
                  row(RW_W), pl.BlockSpec((1, 6, d), mod_map), _full((d, d)), _full((1, RW_W)),
                  _full((1, RW_W)), _full((1, d)), _full((d, N_EXPERTS)), _full((d, N_EXPERTS)),
                  _full((RW_W, RW_W))],
        out_specs=[row(d), row(d), row(N_EXPERTS)],
        out_shape=[jax.ShapeDtypeStruct((tokens, d), F32), jax.ShapeDtypeStruct((tokens, d), BF16),
                   jax.ShapeDtypeStruct((tokens, N_EXPERTS), F32)],
        compiler_params=_cp("parallel"),
        name="finish",
    )(x, oa, oc, o2, bonus, g, mod, p['w_out'], p['ln_g'], p['ln_b'], p['n2'], p['wr_hi'], p['wr_lo'],
      p['ones256'])


def _topk_kernel(aff_ref, tri_ref, eye_ref, place_ref, slot_ref, slotrow_ref, gfull_ref, ends_ref, *, cap, group,
                 tb):
    b = pl.program_id(0)
    aff = aff_ref[...]
    seq = aff.shape[0]
    bits = lax.bitcast_convert_type(aff, jnp.int32)
    capf = jnp.float32(cap)

    def bis(_, carry):
        lo, hi = carry
        mid = lo + ((hi - lo + 1) >> 1)
        cnt = jnp.sum(jnp.where(bits >= mid, 1.0, 0.0), axis=0, keepdims=True)
        ge = cnt >= capf
        return jnp.where(ge, mid, lo), jnp.where(ge, hi, mid - 1)

    lo0 = jnp.zeros((1, N_EXPERTS), jnp.int32)
    hi0 = jnp.full((1, N_EXPERTS), 0x7F7FFFFF, jnp.int32)
    thr, _ = lax.fori_loop(0, 31, bis, (lo0, hi0))
    gt = jnp.where(bits > thr, 1.0, 0.0)
    eq = jnp.where(bits == thr, 1.0, 0.0)
    need = capf - jnp.sum(gt, axis=0, keepdims=True)
    offset = ((b % group) * cap).astype(F32)
    tri = tri_ref[...]
    eye = eye_ref[...]
    carry_g = jnp.zeros((1, N_EXPERTS), F32)
    carry_e = jnp.zeros((1, N_EXPERTS), F32)
    ghi, gmid, glo = _split3(aff)
    for blk in range(seq // tb):
        sl = slice(blk * tb, (blk + 1) * tb)
        pg = _dot(tri, _bf(gt[sl])) + carry_g
        pe = _dot(tri, _bf(eq[sl])) + carry_e
        carry_g = pg[tb - 1:tb, :]
        carry_e = pe[tb - 1:tb, :]
        sel = gt[sl] + eq[sl] * jnp.where(pe <= need, 1.0, 0.0)
        slot = jnp.where(sel > 0.5, pg + jnp.minimum(pe, need) - 1.0 + offset, -1.0)
        slot_ref[sl, :] = slot
        ends_ref[0, blk:blk + 1, :] = carry_g + jnp.minimum(carry_e, need) + offset
        shi, slo = _split2(slot)
        slotrow_ref[0, :, 0, sl] = _dot_nt(eye, shi) + _dot_nt(eye, slo)
        gfull_ref[sl, :] = _bf(_dot(ghi[sl], place_ref[0]) + _dot(gmid[sl], place_ref[1])
                               + _dot(glo[sl], place_ref[2]))


def _topk(aff, seq, group):
    tokens = aff.shape[0]
    nreq = tokens // seq
    cap = EC_CAPACITY * seq // N_EXPERTS
    tb = min(seq, 512)
    ti = np.arange(tb)
    tri = jnp.asarray((ti[None, :] <= ti[:, None]).astype(np.float32), dtype=BF16)
    eye = jnp.asarray(np.eye(N_EXPERTS, dtype=np.float32), dtype=BF16)
    place = np.zeros((3, N_EXPERTS, 128), np.float32)
    for s in range(3):
        place[s, np.arange(N_EXPERTS), s * N_EXPERTS + np.arange(N_EXPERTS)] = 1.0
    place = jnp.asarray(place, dtype=BF16)
    nblk = seq // tb
    slot, slotrow, gfull, ends = pl.pallas_call(
        functools.partial(_topk_kernel, cap=cap, group=group, tb=tb),
        grid=(nreq,),
        in_specs=[pl.BlockSpec((seq, N_EXPERTS), lambda b: (b, 0)), _full((tb, tb)),
                  _full((N_EXPERTS, N_EXPERTS)), _full((3, N_EXPERTS, 128))],
        out_specs=[pl.BlockSpec((seq, N_EXPERTS), lambda b: (b, 0)),
                   pl.BlockSpec((1, N_EXPERTS, 1, seq), lambda b: (b // group, 0, 0, b % group)),
                   pl.BlockSpec((seq, 128), lambda b: (b, 0)),
                   pl.BlockSpec((1, nblk, N_EXPERTS), lambda b: (b, 0, 0))],
        out_shape=[jax.ShapeDtypeStruct((tokens, N_EXPERTS), F32),
                   jax.ShapeDtypeStruct((nreq // group, N_EXPERTS, 1, group * seq), F32),
                   jax.ShapeDtypeStruct((tokens, 128), BF16),
                   jax.ShapeDtypeStruct((nreq, nblk, N_EXPERTS), F32)],
        compiler_params=_cp("parallel"),
        name="topk",
    )(aff, tri, eye, place)
    ends = ends.reshape(nreq // group, group * nblk, N_EXPERTS).transpose(0, 2, 1)
    return slot, slotrow, gfull, ends.astype(jnp.int32).reshape(-1), tb


def _moe_ffn_kernel(ends_ref, h_ref, slotrow_ref, gfull_ref, mod_ref, wg_ref, wu_ref, wd_ref, ye_ref, xe_scr,
                    gs_scr, *, ct, kc, nch):
    gi = pl.program_id(0)
    e = pl.program_id(1)
    mt = 128
    xe_scr[...] = jnp.zeros_like(xe_scr)
    gs_scr[...] = jnp.zeros_like(gs_scr)
    base = (gi * N_EXPERTS + e) * nch
    for c in range(nch):
        lo = 0 if c == 0 else ends_ref[base + c - 1]
        hi = ends_ref[base + c]
        sl = slice(c * kc, (c + 1) * kc)
        for m in range(ct // mt):
            rows = slice(m * mt, (m + 1) * mt)

            @pl.when((hi > m * mt) & (lo < (m + 1) * mt))
            def _(sl=sl, rows=rows, m=m):
                jcol = (lax.broadcasted_iota(jnp.int32, (mt, 1), 0) + m * mt).astype(F32)
                onehot = _bf(jnp.where(slotrow_ref[0, 0, :, sl] == jcol, 1.0, 0.0))
                xe_scr[rows, :] += _dot(onehot, h_ref[sl, :])
                gs_scr[rows, :] += _dot(onehot, gfull_ref[sl, :])

    lane = lax.broadcasted_iota(jnp.int32, (1, 128), 1)
    pick = (lane == e) | (lane == e + N_EXPERTS) | (lane == e + 2 * N_EXPERTS)
    gate = jnp.sum(jnp.where(pick, gs_scr[...], 0.0), axis=-1, keepdims=True)
    xb = _bf(xe_scr[...])
    hg = _dot(xb, wg_ref[0])
    hu = _dot(xb, wu_ref[0])
    he = _bf(hg * _sigmoid(hg) * hu)
    y = _dot(he, wd_ref[0])
    ye_ref[0, 0] = _bf(y * gate * mod_ref[0, 5:6, :])


def _moe_ffn(ends, h2, slotrow, gfull, mod, wg, wu, wd, lg, ct, kc, per_group_mod):
    tokens, d = h2.shape
    ngrp = tokens // lg
    f = wg.shape[2]
    mod_map = (lambda gi, e, ends: (gi, 0, 0)) if per_group_mod else (lambda gi, e, ends: (0, 0, 0))
    grid_spec = pltpu.PrefetchScalarGridSpec(
        num_scalar_prefetch=1,
        grid=(ngrp, N_EXPERTS),
        in_specs=[pl.BlockSpec((lg, d), lambda gi, e, ends: (gi, 0)),
                  pl.BlockSpec((1, 1, 1, lg), lambda gi, e, ends: (gi, e, 0, 0)),
                  pl.BlockSpec((lg, 128), lambda gi, e, ends: (gi, 0)),
                  pl.BlockSpec((1, 6, d), mod_map),
                  pl.BlockSpec((1, d, f), lambda gi, e, ends: (e, 0, 0)),
                  pl.BlockSpec((1, d, f), lambda gi, e, ends: (e, 0, 0)),
                  pl.BlockSpec((1, f, d), lambda gi, e, ends: (e, 0, 0))],
        out_specs=pl.BlockSpec((1, 1, ct, d), lambda gi, e, ends: (gi, e, 0, 0)),
        scratch_shapes=[pltpu.VMEM((ct, d), F32), pltpu.VMEM((ct, 128), F32)])
    return pl.pallas_call(
        functools.partial(_moe_ffn_kernel, ct=ct, kc=kc, nch=lg // kc),
        grid_spec=grid_spec,
        out_shape=jax.ShapeDtypeStruct((ngrp, N_EXPERTS, ct, d), BF16),
        compiler_params=_cp("parallel", "arbitrary"),
        name="moe_ffn",
    )(ends, h2, slotrow, gfull, mod, wg, wu, wd)


def _moe_combine_kernel(ends_ref, x1_ref, slot_ref, ye_ref, o_ref, win_scr, *, ct, nch):
    gi = pl.program_id(0)
    j = pl.program_id(1)
    mt = 128
    wide = N_EXPERTS * mt
    shi, slo = _split2(slot_ref[...])
    col_e = lax.broadcasted_iota(jnp.int32, (N_EXPERTS, wide), 1) // mt
    row_e = lax.broadcasted_iota(jnp.int32, (N_EXPERTS, wide), 0)
    expand = _bf(jnp.where(col_e == row_e, 1.0, 0.0))
    sb = _dot(shi, expand) + _dot(slo, expand)
    lane = lax.broadcasted_iota(jnp.int32, (1, mt), 1)
    starts, his, targets = [], [], []
    for e in range(N_EXPERTS):
        base = (gi * N_EXPERTS + e) * nch
        lo = jnp.where(j == 0, 0, ends_ref[base + jnp.maximum(j - 1, 0)])
        his.append(ends_ref[base + j])
        start = pl.multiple_of(jnp.minimum((lo // 16) * 16, ct - mt), 16)
        win_scr[e * mt:(e + 1) * mt, :] = ye_ref[0, e, pl.ds(start, mt), :]
        starts.append(start)
        targets.append((lane + start).astype(F32))
    onehot = _bf(jnp.where(sb == jnp.concatenate(targets, axis=1), 1.0, 0.0))
    o_ref[...] = x1_ref[...] + _dot(onehot, win_scr[...])
    for e in range(N_EXPERTS):
        for w in range(1, ct // mt):
            wlo = starts[e] + w * mt

            @pl.when(wlo < his[e])
            def _(e=e, wlo=wlo):
                ws = pl.multiple_of(jnp.minimum(wlo, ct - mt), 16)
                sbe = sb[:, e * mt:(e + 1) * mt]
                hit = jnp.where(sbe == (lane + ws).astype(F32), 1.0, 0.0) * jnp.where(sbe >= wlo.astype(F32), 1.0, 0.0)
                o_ref[...] += _dot(_bf(hit), ye_ref[0, e, pl.ds(ws, mt), :])


def _moe_combine(ends, x1, slot, ye, lg, ct, kc):
    tokens, d = x1.shape
    ngrp = tokens // lg
    nch = lg // kc
    grid_spec = pltpu.PrefetchScalarGridSpec(
        num_scalar_prefetch=1,
        grid=(ngrp, nch),
        in_specs=[pl.BlockSpec((kc, d), lambda gi, j, ends: (gi * nch + j, 0)),
                  pl.BlockSpec((kc, N_EXPERTS), lambda gi, j, ends: (gi * nch + j, 0)),
                  pl.BlockSpec((1, N_EXPERTS, ct, d), lambda gi, j, ends: (gi, 0, 0, 0),
                               pipeline_mode=pl.Buffered(1))],
        out_specs=pl.BlockSpec((kc, d), lambda gi, j, ends: (gi * nch + j, 0)),
        scratch_shapes=[pltpu.VMEM((N_EXPERTS * 128, d), BF16)])
    return pl.pallas_call(
        functools.partial(_moe_combine_kernel, ct=ct, nch=nch),
        grid_spec=grid_spec,
        out_shape=jax.ShapeDtypeStruct((tokens, d), F32),
        compiler_params=_cp("parallel", "arbitrary"),
        name="moe_combine",
    )(ends, x1, slot, ye)


def _rope_tables(seq):
    t = np.arange(seq)
    n_freq = HEAD_DIM // 4
    inv = ROPE_THETA ** (-np.arange(n_freq, dtype=np.float32) / n_freq)
    ang = np.concatenate([(t // GRID_W).astype(np.float32)[:, None] * inv,
                          (t % GRID_W).astype(np.float32)[:, None] * inv], axis=-1)
    ang = jnp.asarray(ang, dtype=F32)
    cos, sin = jnp.cos(ang), jnp.sin(ang)
    cos_t = jnp.tile(jnp.concatenate([cos, cos], axis=-1), (1, SWA_HEADS))
    sin_t = jnp.tile(jnp.concatenate([-sin, sin], axis=-1), (1, SWA_HEADS))
    return cos_t, sin_t


def _blockdiag2(w):
    z = jnp.zeros_like(w[0])
    return jnp.concatenate([jnp.concatenate([w[0], z], axis=1), jnp.concatenate([z, w[1]], axis=1)], axis=0)


def _layer_params(l, ada_w, ada_b, norm1_g, norm2_g, w_in, na_q_norm, na_k_norm, na_rpb, rw_mu, rw_w0, rw_w2,
                  rw_a0, rw_a2, rw_g2, rw_k_k, rw_k_a, rw_r_k, rw_ln_g, rw_ln_b, swa_q_norm, swa_k_norm,
                  swa_sink, w_out, w_router, w_gate, w_up, w_down):
    wr = w_router[l]
    wr_hi = wr.astype(BF16)
    return {
        'n1': norm1_g[l][None], 'n2': norm2_g[l][None], 'w_in': w_in[l].astype(BF16),
        'gains': (jnp.tile(na_q_norm[l], NA_HEADS)[None], jnp.tile(na_k_norm[l], NA_HEADS)[None],
                  jnp.tile(swa_q_norm[l], SWA_HEADS)[None], jnp.tile(swa_k_norm[l], SWA_KV_HEADS)[None]),
        'bias': _na_bias_table(na_rpb[l]),
        'mu': rw_mu[l], 'w0': rw_w0[l].reshape(1, 2 * RW_W), 'w2': _blockdiag2(rw_w2[l]).astype(BF16),
        'a0': rw_a0[l].reshape(1, 2 * RW_W), 'a2': _blockdiag2(rw_a2[l]).astype(BF16),
        'g2': rw_g2[l].astype(BF16), 'k_k': rw_k_k[l][None], 'k_a': rw_k_a[l][None],
        'r_k': rw_r_k[l].reshape(1, RW_W), 'ln_g': rw_ln_g[l][None], 'ln_b': rw_ln_b[l][None],
        'sink': swa_sink[l], 'w_out': w_out[l].astype(BF16),
        'wr_hi': wr_hi, 'wr_lo': (wr - wr_hi.astype(F32)).astype(BF16),
        'wg': w_gate[l].astype(BF16), 'wu': w_up[l].astype(BF16), 'wd': w_down[l].astype(BF16),
        'ones256': _block_ones(RW_W),
    }


def _mix_and_ffn(x, mod, p, oa, oc, urw, s0_bd, scan_consts, seq, per_request_mod, group):
    r, kk, v, ld, kka, kd, g, bonus = _rw_pre(urw, p, seq)
    o2, sfin = _rw_scan(r, kk, v, ld, kka, kd, s0_bd, scan_consts, seq)
    x1, h2, aff = _finish(x, oa, oc, o2, bonus, g, mod, p, seq, per_request_mod)
    slot, slotrow, gfull, ends, kc = _topk(aff, seq, group)
    cap = EC_CAPACITY * seq // N_EXPERTS
    lg, ct = group * seq, group * cap
    ye = _moe_ffn(ends, h2, slotrow, gfull, mod, p['wg'], p['wu'], p['wd'], lg, ct, kc, per_request_mod)
    return _moe_combine(ends, x1, slot, ye, lg, ct, kc), sfin


def _context_layer(x, mod, p, ones384, scan_consts, seq):
    qa, ka, va, urw, qc, kc, vc = _proj(x, mod, p['n1'], p['w_in'], p['gains'], ones384, None, seq, False, F32)
    oa, oc = _ctx_attn(p['sink'], qa, ka, va, qc, kc, vc, seq)
    nreq = x.shape[0] // seq
    s0 = jnp.zeros((nreq, 2, RW_W, RW_W), F32)
    y, sfin = _mix_and_ffn(x, mod, p, oa, oc, urw, s0, scan_consts, seq, False, 8)
    return y, ka, va, kc, vc, sfin


def _latent_layer(x, mod, p, ones384, scan_consts, rope_tabs, seq, kx_na, vx_na, kx_swa, vx_swa, s0_bd, past):
    qa, ka, va, urw, qc, kc, vc = _proj(x, mod, p['n1'], p['w_in'], p['gains'], ones384, rope_tabs, seq, True,
                                        BF16)
    oa = _na_attn(qa, ka, va, kx_na, vx_na, p['bias'], seq, past)
    oc = _swa_attn(p['sink'], qc, kc, vc, kx_swa, vx_swa, seq, past)
    y, _ = _mix_and_ffn(x, mod, p, oa, oc, urw, s0_bd, scan_consts, seq, True, 1)
    return y


def _heads_first(z, nreq, seq, heads):
    return z.reshape(nreq, seq, heads, HEAD_DIM).transpose(0, 2, 1, 3)


def _tokens_first(z):
    b, h, n, dh = z.shape
    return z.transpose(0, 2, 1, 3).reshape(b * n, h * dh)


def kernel(x_prompt, x_sample, cache_na_k, cache_na_v, cache_swa_k, cache_swa_v, state_rwkv, c, c_ctx, ada_w, ada_b, norm1_g, norm2_g, w_in, na_q_norm, na_k_norm, na_rpb, rw_mu, rw_w0, rw_w2, rw_a0, rw_a2, rw_g2, rw_k_k, rw_k_a, rw_r_k, rw_ln_g, rw_ln_b, swa_q_norm, swa_k_norm, swa_sink, w_out, w_router, w_gate, w_up, w_down):
    nb, seq, d = x_prompt.shape
    db, dseq, _ = x_sample.shape
    depth = ada_w.shape[0]
    past = cache_na_k.shape[3]
    cond = jnp.concatenate([c, c_ctx[None], jnp.zeros((16 - db - 1, d), F32)], axis=0)
    mod_all = _adaln(cond, ada_w, ada_b).reshape(depth, 16, 6, d)
    ones384 = _block_ones(NA_W)
    scan_consts = _rw_masks()
    rope_tabs = _rope_tables(dseq)
    xp = x_prompt.reshape(nb * seq, d)
    xs = x_sample.reshape(db * dseq, d)
    new_ka, new_va, new_kc, new_vc, new_s = [], [], [], [], []
    for l in range(depth):
        p = _layer_params(l, ada_w, ada_b, norm1_g, norm2_g, w_in, na_q_norm, na_k_norm, na_rpb, rw_mu, rw_w0,
                          rw_w2, rw_a0, rw_a2, rw_g2, rw_k_k, rw_k_a, rw_r_k, rw_ln_g, rw_ln_b, swa_q_norm,
                          swa_k_norm, swa_sink, w_out, w_router, w_gate, w_up, w_down)
        mod_ctx = mod_all[l, db:db + 1]
        mod_lat = mod_all[l, 0:db]
        xp, ka, va, kc, vc, sfin = _context_layer(xp, mod_ctx, p, ones384, scan_consts, seq)
        new_ka.append(_heads_first(ka, nb, seq, NA_HEADS))
        new_va.append(_heads_first(va, nb, seq, NA_HEADS))
        new_kc.append(_heads_first(kc, nb, seq, SWA_KV_HEADS))
        new_vc.append(_heads_first(vc, nb, seq, SWA_KV_HEADS))
        new_s.append(_blockdiag_to_state(sfin))
        xs = _latent_layer(xs, mod_lat, p, ones384, scan_consts, rope_tabs, dseq,
                           _tokens_first(cache_na_k[:, l]), _tokens_first(cache_na_v[:, l]),
                           _tokens_first(cache_swa_k[:, l]), _tokens_first(cache_swa_v[:, l]),
                           _state_to_blockdiag(state_rwkv[:, l]), past)
    return (xp.reshape(nb, seq, d), xs.reshape(db, dseq, d), jnp.stack(new_ka, axis=1),
            jnp.stack(new_va, axis=1), jnp.stack(new_kc, axis=1), jnp.stack(new_vc, axis=1),
            jnp.stack(new_s, axis=1))
```

```python
import functools

import numpy as np
import jax
import jax.numpy as jnp
from jax import lax
from jax.experimental import pallas as pl
from jax.experimental.pallas import tpu as pltpu

F32 = jnp.float32
BF16 = jnp.bfloat16

HEAD_DIM = 64
GRID_W = 64
NA_HEADS = 6
NA_KH = 8
NA_KW = 16
RW_HEADS = 4
SWA_HEADS = 6
SWA_KV_HEADS = 2
SWA_WINDOW = 128
N_EXPERTS = 16
EC_CAPACITY = 2
ROPE_THETA = 10000.0
NORM_EPS = 1e-6
GN_EPS = 64e-5
NEG_INF = -1e30
NA_UNROLL = 4
SWA_UNROLL = 2
RW_CHUNK = 64
RW_W = RW_HEADS * HEAD_DIM
NA_W = NA_HEADS * HEAD_DIM
SWA_W = SWA_HEADS * HEAD_DIM
SWA_KV_W = SWA_KV_HEADS * HEAD_DIM
RW_IN_W = 1152
VMEM_LIMIT = 56 * 1024 * 1024


def _cp(*sem):
    return pltpu.CompilerParams(dimension_semantics=sem, vmem_limit_bytes=VMEM_LIMIT)


def _bf(x):
    return x.astype(BF16)


def _dot(a, b):
    return jnp.dot(a, b, preferred_element_type=F32)


def _dot_nt(a, b):
    return lax.dot_general(a, b, (((1,), (1,)), ((), ())), preferred_element_type=F32)


def _dot_tn(a, b):
    return lax.dot_general(a, b, (((0,), (0,)), ((), ())), preferred_element_type=F32)


def _split2(x):
    hi = x.astype(BF16)
    lo = (x - hi.astype(F32)).astype(BF16)
    return hi, lo


def _split3(x):
    hi = x.astype(BF16)
    r1 = x - hi.astype(F32)
    mid = r1.astype(BF16)
    lo = (r1 - mid.astype(F32)).astype(BF16)
    return hi, mid, lo


def _dot2(a, b_bf):
    hi, lo = _split2(a)
    return _dot(hi, b_bf) + _dot(lo, b_bf)


def _sigmoid(x):
    return 1.0 / (1.0 + jnp.exp(-x))


def _block_ones(width):
    i = np.arange(width) // HEAD_DIM
    return jnp.asarray((i[:, None] == i[None, :]).astype(np.float32), dtype=BF16)


def _row_tile(seq):
    return 512 if seq % 512 == 0 else 256


def _full(shape):
    return pl.BlockSpec(shape, lambda *_: (0,) * len(shape))


def _adaln_kernel(c_ref, w_ref, b_ref, o_ref):
    c = c_ref[...]
    s = c * _sigmoid(c)
    shi, slo = _split2(s)
    whi, wlo = _split2(w_ref[0])
    o_ref[0] = _dot(shi, whi) + _dot(slo, whi) + _dot(shi, wlo) + b_ref[0]


def _adaln(cond, ada_w, ada_b):
    nl, d, n6 = ada_w.shape
    tn = 1536
    rows = cond.shape[0]
    return pl.pallas_call(
        _adaln_kernel,
        grid=(nl, n6 // tn),
        in_specs=[pl.BlockSpec((rows, d), lambda l, j: (0, 0)),
                  pl.BlockSpec((1, d, tn), lambda l, j: (l, 0, j)),
                  pl.BlockSpec((1, 1, tn), lambda l, j: (l, 0, j))],
        out_specs=pl.BlockSpec((1, rows, tn), lambda l, j: (l, 0, j)),
        out_shape=jax.ShapeDtypeStruct((nl, rows, n6), F32),
        compiler_params=_cp("parallel", "parallel"),
        name="adaln",
    )(cond, ada_w, ada_b.reshape(nl, 1, n6))


def _head_norm(z, gain, ones_bf):
    ms = _dot2(z * z, ones_bf) * (1.0 / HEAD_DIM)
    return z * lax.rsqrt(ms + NORM_EPS) * gain


def _rope(z, cos, sin_signed):
    w = z.shape[1]
    lane = lax.broadcasted_iota(jnp.int32, z.shape, 1)
    first = (lane % HEAD_DIM) < (HEAD_DIM // 2)
    swapped = jnp.where(first, pltpu.roll(z, w - HEAD_DIM // 2, 1), pltpu.roll(z, HEAD_DIM // 2, 1))
    return z * cos + swapped * sin_signed


def _proj_kernel(*refs, rope):
    if rope:
        (x_ref, mod_ref, n1_ref, w_ref, gqa_ref, gka_ref, gqc_ref, gkc_ref, ones_ref, cos_ref, sin_ref,
         qa_ref, ka_ref, va_ref, urw_ref, qc_ref, kc_ref, vc_ref) = refs
    else:
        (x_ref, mod_ref, n1_ref, w_ref, gqa_ref, gka_ref, gqc_ref, gkc_ref, ones_ref,
         qa_ref, ka_ref, va_ref, urw_ref, qc_ref, kc_ref, vc_ref) = refs
    x = x_ref[...]
    sh1 = mod_ref[0, 0:1, :]
    sc1 = mod_ref[0, 1:2, :]
    ms = jnp.mean(x * x, axis=-1, keepdims=True)
    h = x * lax.rsqrt(ms + NORM_EPS) * n1_ref[...] * (1.0 + sc1) + sh1
    u = _dot(_bf(h), w_ref[...])
    o0 = 0
    o1 = NA_W
    o2 = 2 * NA_W
    o3 = 3 * NA_W
    o4 = o3 + RW_IN_W
    o5 = o4 + SWA_W
    o6 = o5 + SWA_KV_W
    ones = ones_ref[...]
    ones_kv = ones_ref[0:SWA_KV_W, 0:SWA_KV_W]
    qa = _head_norm(u[:, o0:o1], gqa_ref[...], ones)
    ka = _head_norm(u[:, o1:o2], gka_ref[...], ones)
    qc = _head_norm(u[:, o4:o5], gqc_ref[...], ones)
    kc = _head_norm(u[:, o5:o6], gkc_ref[...], ones_kv)
    if rope:
        qc = _rope(qc, cos_ref[...], sin_ref[...])
        kc = _rope(kc, cos_ref[:, 0:SWA_KV_W], sin_ref[:, 0:SWA_KV_W])
    qa_ref[...] = qa.astype(qa_ref.dtype)
    ka_ref[...] = ka.astype(ka_ref.dtype)
    va_ref[...] = u[:, o2:o3].astype(va_ref.dtype)
    urw_ref[...] = u[:, o3:o4]
    qc_ref[...] = qc.astype(qc_ref.dtype)
    kc_ref[...] = kc.astype(kc_ref.dtype)
    vc_ref[...] = u[:, o6:].astype(vc_ref.dtype)


def _proj(x, mod, n1, w_in_bf, gains, ones384, rope_tabs, seq, per_request_mod, qkv_dtype):
    tokens, d = x.shape
    tm = _row_tile(seq)
    tiles_per_req = seq // tm
    in_w = w_in_bf.shape[1]
    rope = rope_tabs is not None
    mod_map = (lambda i: (i // tiles_per_req, 0, 0)) if per_request_mod else (lambda i: (0, 0, 0))
    row = lambda w: pl.BlockSpec((tm, w), lambda i: (i, 0))
    in_specs = [row(d), pl.BlockSpec((1, 6, d), mod_map), _full((1, d)), _full((d, in_w)),
                _full((1, NA_W)), _full((1, NA_W)), _full((1, SWA_W)), _full((1, SWA_KV_W)),
                _full((NA_W, NA_W))]
    args = [x, mod, n1, w_in_bf, *gains, ones384]
    if rope:
        tab = pl.BlockSpec((tm, SWA_W), lambda i: (i % tiles_per_req, 0))
        in_specs += [tab, tab]
        args += list(rope_tabs)
    widths = [NA_W, NA_W, NA_W, RW_IN_W, SWA_W, SWA_KV_W, SWA_KV_W]
    dtypes = [qkv_dtype, qkv_dtype, qkv_dtype, F32, qkv_dtype, qkv_dtype, qkv_dtype]
    return pl.pallas_call(
        functools.partial(_proj_kernel, rope=rope),
        grid=(tokens // tm,),
        in_specs=in_specs,
        out_specs=[row(w) for w in widths],
        out_shape=[jax.ShapeDtypeStruct((tokens, w), dt) for w, dt in zip(widths, dtypes)],
        compiler_params=_cp("parallel"),
        name="proj",
    )(*args)


def _half_masks(width=2 * HEAD_DIM):
    lane = lax.broadcasted_iota(jnp.int32, (1, width), 1)
    return lane < HEAD_DIM, lane >= HEAD_DIM


def _swap_halves(z):
    return pltpu.roll(z, HEAD_DIM, 1)


def _ctx_attn_kernel(sink_ref, qa_ref, ka_ref, va_ref, qc_ref, kc_ref, vc_ref, oa_ref, oc_ref):
    scale = HEAD_DIM ** -0.5
    m0, m1 = _half_masks()
    masks = (m0, m1)
    for pair in range(NA_HEADS // 2):
        sl = slice(pair * 128, (pair + 1) * 128)
        qp = qa_ref[:, sl].astype(F32) * scale
        kp = _bf(ka_ref[:, sl])
        vp = _bf(va_ref[:, sl])
        outs = []
        for half in range(2):
            qm = _bf(jnp.where(masks[half], qp, 0.0))
            s = _dot_nt(qm, kp)
            m = jnp.max(s, axis=-1, keepdims=True)
            e = jnp.exp(s - m)
            l = jnp.sum(e, axis=-1, keepdims=True)
            outs.append(_dot(_bf(e), vp) / l)
        oa_ref[:, sl] = jnp.where(m0, outs[0], outs[1])
    kc = _bf(kc_ref[...])
    vc = _bf(vc_ref[...])
    group = SWA_HEADS // SWA_KV_HEADS
    for pair in range(SWA_HEADS // 2):
        sl = slice(pair * 128, (pair + 1) * 128)
        qp = qc_ref[:, sl].astype(F32) * scale
        outs = []
        for half in range(2):
            h = 2 * pair + half
            g = h // group
            qh = qp if g == half else _swap_halves(qp)
            qm = _bf(jnp.where(masks[g], qh, 0.0))
            s = _dot_nt(qm, kc)
            sk = sink_ref[h]
            m = jnp.maximum(jnp.max(s, axis=-1, keepdims=True), sk)
            e = jnp.exp(s - m)
            l = jnp.sum(e, axis=-1, keepdims=True) + jnp.exp(sk - m)
            o = _dot(_bf(e), vc) / l
            outs.append(o if g == half else _swap_halves(o))
        oc_ref[:, sl] = jnp.where(m0, outs[0], outs[1])


def _ctx_attn(sink, qa, ka, va, qc, kc, vc, seq):
    tokens = qa.shape[0]
    blk = lambda w: pl.BlockSpec((seq, w), lambda b: (b, 0))
    return pl.pallas_call(
        _ctx_attn_kernel,
        grid=(tokens // seq,),
        in_specs=[pl.BlockSpec(memory_space=pltpu.SMEM), blk(NA_W), blk(NA_W), blk(NA_W), blk(SWA_W),
                  blk(SWA_KV_W), blk(SWA_KV_W)],
        out_specs=[blk(NA_W), blk(SWA_W)],
        out_shape=[jax.ShapeDtypeStruct((tokens, NA_W), F32), jax.ShapeDtypeStruct((tokens, SWA_W), F32)],
        compiler_params=_cp("parallel"),
        name="ctx_attn",
    )(sink, qa, ka, va, qc, kc, vc)


def _na_bias_kernel(rpb_ref, o_ref):
    h = pl.program_id(0)
    nrow = 2 * NA_KH - 1
    ncol = 2 * NA_KW - 1
    width = NA_KH * GRID_W
    shape = (GRID_W, width)
    lane = lax.broadcasted_iota(jnp.int32, shape, 1)
    qc = lax.broadcasted_iota(jnp.int32, shape, 0)
    kc = lane % GRID_W
    c_start = jnp.clip(qc - NA_KW // 2, 0, GRID_W - NA_KW)
    ok = (kc >= c_start) & (kc < c_start + NA_KW)
    d_col = jnp.clip(kc - qc, 1 - NA_KW, NA_KW - 1) + NA_KW - 1
    key_row = lax.broadcasted_iota(jnp.int32, (1, width), 1) // GRID_W

    def case_body(case, carry):
        acc = jnp.zeros(shape, F32)
        for dc in range(ncol):
            val = jnp.zeros((1, width), F32)
            for i in range(NA_KH):
                val = jnp.where(key_row == i, rpb_ref[(h * nrow + case + i) * ncol + dc], val)
            acc = jnp.where(d_col == dc, val, acc)
        o_ref[0, pl.ds(case, 1)] = jnp.where(ok, acc, NEG_INF)[None]
        return carry

    lax.fori_loop(0, NA_KH, case_body, 0)


def _na_bias_table(rpb):
    nh = rpb.shape[0]
    return pl.pallas_call(
        _na_bias_kernel,
        grid=(nh,),
        in_specs=[pl.BlockSpec(memory_space=pltpu.SMEM)],
        out_specs=pl.BlockSpec((1, NA_KH, GRID_W, NA_KH * GRID_W), lambda h: (h // 2, 0, h % 2, 0)),
        out_shape=jax.ShapeDtypeStruct((nh // 2, NA_KH, 2 * GRID_W, NA_KH * GRID_W), F32),
        compiler_params=_cp("parallel"),
        name="na_bias",
    )(rpb.reshape(-1))


def _na_kernel(q_ref, k_ref, v_ref, kx_ref, vx_ref, bias_ref, o_ref, *, rows):
    scale = HEAD_DIM ** -0.5
    m0, m1 = _half_masks()
    kx = _bf(kx_ref[...])
    vx = _bf(vx_ref[...])

    def body(it, carry):
        us = range(NA_UNROLL)
        r = [it * NA_UNROLL + u for u in us]
        rs = [jnp.clip(r[u] - NA_KH // 2, 0, rows - NA_KH) for u in us]
        case = [rs[u] - r[u] + NA_KH - 1 for u in us]
        q0 = [pl.multiple_of(r[u] * GRID_W, GRID_W) for u in us]
        k0 = [pl.multiple_of(rs[u] * GRID_W, GRID_W) for u in us]
        qp = [q_ref[pl.ds(q0[u], GRID_W), :].astype(F32) * scale for u in us]
        kw = [_bf(k_ref[pl.ds(k0[u], NA_KH * GRID_W), :]) for u in us]
        vw = [_bf(v_ref[pl.ds(k0[u], NA_KH * GRID_W), :]) for u in us]
        q2 = [_bf(jnp.concatenate([jnp.where(m0, qp[u], 0.0), jnp.where(m1, qp[u], 0.0)], axis=0)) for u in us]
        sw = [_dot_nt(q2[u], kw[u]) + bias_ref[0, pl.ds(case[u], 1)][0] for u in us]
        sx = [_dot_nt(q2[u], kx) for u in us]
        m = [jnp.maximum(jnp.max(sw[u], axis=-1, keepdims=True), jnp.max(sx[u], axis=-1, keepdims=True))
             for u in us]
        ew = [jnp.exp(sw[u] - m[u]) for u in us]
        ex = [jnp.exp(sx[u] - m[u]) for u in us]
        l = [jnp.sum(ew[u], axis=-1, keepdims=True) + jnp.sum(ex[u], axis=-1, keepdims=True) for u in us]
        o = [(_dot(_bf(ew[u]), vw[u]) + _dot(_bf(ex[u]), vx)) / l[u] for u in us]
        for u in us:
            o_ref[pl.ds(q0[u], GRID_W), :] = jnp.where(m0, o[u][0:GRID_W], o[u][GRID_W:])
        return carry

    lax.fori_loop(0, rows // NA_UNROLL, body, 0)


def _na_attn(q, k, v, kx, vx, bias, seq, past):
    tokens = q.shape[0]
    nb = tokens // seq
    rows = seq // GRID_W
    blk = pl.BlockSpec((seq, 128), lambda b, p: (b, p))
    cblk = pl.BlockSpec((past, 128), lambda b, p: (b, p))
    return pl.pallas_call(
        functools.partial(_na_kernel, rows=rows),
        grid=(nb, NA_HEADS // 2),
        in_specs=[blk, blk, blk, cblk, cblk,
                  pl.BlockSpec((1, NA_KH, 2 * GRID_W, NA_KH * GRID_W), lambda b, p: (p, 0, 0, 0))],
        out_specs=blk,
        out_shape=jax.ShapeDtypeStruct((tokens, NA_W), F32),
        compiler_params=_cp("parallel", "parallel"),
        name="na_attn",
    )(q, k, v, kx, vx, bias)


def _swa_kernel(sink_ref, q_ref, k_ref, v_ref, kx_ref, vx_ref, o_ref, *, seq):
    scale = HEAD_DIM ** -0.5
    blk = SWA_WINDOW
    m0, m1 = _half_masks()
    masks = (m0, m1)
    kx = _bf(kx_ref[...])
    vx = _bf(vx_ref[...])
    group = SWA_HEADS // SWA_KV_HEADS

    sk = []
    for g in range(SWA_KV_HEADS):
        sk.append(jnp.concatenate([jnp.full((blk, 1), sink_ref[h], F32) for h in range(g * group, (g + 1) * group)],
                                  axis=0))

    def body(it, carry):
        us = range(SWA_UNROLL)
        cs = [(u, g) for u in us for g in range(SWA_KV_HEADS)]
        nb = [it * SWA_UNROLL + u for u in us]
        ks = [pl.multiple_of(jnp.clip((nb[u] - 1) * blk, 0, seq - 3 * blk), blk) for u in us]
        q0 = [pl.multiple_of(nb[u] * blk, blk) for u in us]
        kw = [_bf(k_ref[pl.ds(ks[u], 3 * blk), :]) for u in us]
        vw = [_bf(v_ref[pl.ds(ks[u], 3 * blk), :]) for u in us]
        ok = []
        for u in us:
            qpos = q0[u] + lax.broadcasted_iota(jnp.int32, (group * blk, 1), 0) % blk
            kpos = ks[u] + lax.broadcasted_iota(jnp.int32, (1, 3 * blk), 1)
            ok.append(jnp.abs(qpos - kpos) <= SWA_WINDOW)
        qg = {}
        for u in us:
            pairs = [q_ref[pl.ds(q0[u], blk), p * 128:(p + 1) * 128].astype(F32) * scale
                     for p in range(SWA_HEADS // 2)]
            for g in range(SWA_KV_HEADS):
                qs = []
                for h in range(g * group, (g + 1) * group):
                    qh = pairs[h // 2] if h % 2 == g else _swap_halves(pairs[h // 2])
                    qs.append(jnp.where(masks[g], qh, 0.0))
                qg[u, g] = _bf(jnp.concatenate(qs, axis=0))
        sw = {c: jnp.where(ok[c[0]], _dot_nt(qg[c], kw[c[0]]), NEG_INF) for c in cs}
        sx = {c: _dot_nt(qg[c], kx) for c in cs}
        m = {c: jnp.maximum(jnp.maximum(jnp.max(sw[c], axis=-1, keepdims=True),
                                        jnp.max(sx[c], axis=-1, keepdims=True)), sk[c[1]]) for c in cs}
        ew = {c: jnp.exp(sw[c] - m[c]) for c in cs}
        ex = {c: jnp.exp(sx[c] - m[c]) for c in cs}
        l = {c: jnp.sum(ew[c], axis=-1, keepdims=True) + jnp.sum(ex[c], axis=-1, keepdims=True)
             + jnp.exp(sk[c[1]] - m[c]) for c in cs}
        o = {c: (_dot(_bf(ew[c]), vw[c[0]]) + _dot(_bf(ex[c]), vx)) / l[c] for c in cs}
        for u in us:
            head_out = []
            for g in range(SWA_KV_HEADS):
                for i in range(group):
                    h = g * group + i
                    oh = o[u, g][i * blk:(i + 1) * blk]
                    head_out.append(oh if h % 2 == g else _swap_halves(oh))
            for p in range(SWA_HEADS // 2):
                o_ref[pl.ds(q0[u], blk), p * 128:(p + 1) * 128] = jnp.where(m0, head_out[2 * p], head_out[2 * p + 1])
        return carry

    lax.fori_loop(0, seq // (blk * SWA_UNROLL), body, 0)


def _swa_attn(sink, q, k, v, kx, vx, seq, past):
    tokens = q.shape[0]
    blk = lambda w: pl.BlockSpec((seq, w), lambda b: (b, 0))
    cblk = pl.BlockSpec((past, SWA_KV_W), lambda b: (b, 0))
    return pl.pallas_call(
        functools.partial(_swa_kernel, seq=seq),
        grid=(tokens // seq,),
        in_specs=[pl.BlockSpec(memory_space=pltpu.SMEM), blk(SWA_W), blk(SWA_KV_W), blk(SWA_KV_W), cblk, cblk],
        out_specs=blk(SWA_W),
        out_shape=jax.ShapeDtypeStruct((tokens, SWA_W), F32),
        compiler_params=_cp("parallel"),
        name="swa_attn",
    )(sink, q, k, v, kx, vx)


def _rw_pre_kernel(u_ref, up_ref, un_ref, mu_ref, w0_ref, w2_ref, a0_ref, a2_ref, g2_ref, kk_ref_, ka_ref_,
                   rk_ref, ones_ref, r_o, kk_o, v_o, ld_o, kka_o, kd_o, g_o, bonus_o, *, tiles_per_req):
    i = pl.program_id(0)
    u = u_ref[...]
    tm = u.shape[0]
    rowi = lax.broadcasted_iota(jnp.int32, (tm, 1), 0)
    first = (i % tiles_per_req) == 0
    last = (i % tiles_per_req) == tiles_per_req - 1
    prev_row = jnp.where(first, 0.0, up_ref[7:8, :])
    next_row = jnp.where(last, 0.0, un_ref[0:1, :])
    prev = jnp.where(rowi == 0, prev_row, pltpu.roll(u, 1, 0))
    nxt = jnp.where(rowi == tm - 1, next_row, pltpu.roll(u, tm - 1, 0))
    us = u + mu_ref[0:1, :] * (prev - u) + mu_ref[1:2, :] * (nxt - u)
    r = us[:, 0:RW_W]
    k = us[:, RW_W:2 * RW_W]
    v = us[:, 2 * RW_W:3 * RW_W]
    wl = us[:, 3 * RW_W:3 * RW_W + 128]
    al = us[:, 3 * RW_W + 128:3 * RW_W + 256]
    gl = us[:, 3 * RW_W + 256:3 * RW_W + 384]
    z = -(w0_ref[...] + _dot(_bf(jnp.tanh(wl)), w2_ref[...]))
    softplus = jnp.maximum(z, 0.0) + jnp.log(1.0 + jnp.exp(-jnp.abs(z)))
    w = -softplus - 0.5
    ld = -jnp.exp(w)
    a = _sigmoid(a0_ref[...] + _dot(_bf(al), a2_ref[...]))
    g = _dot(_bf(_sigmoid(gl)), g2_ref[...])
    ones = ones_ref[...]
    kkr = k * kk_ref_[...]
    kk = kkr * lax.rsqrt(jnp.maximum(_dot2(kkr * kkr, ones), 1e-24))
    k_a = ka_ref_[...]
    kd_f = k * (1.0 + (a[:, 0:RW_W] - 1.0) * k_a)
    kd_b = k * (1.0 + (a[:, RW_W:] - 1.0) * k_a)
    r_o[...] = r
    kk_o[...] = kk
    v_o[...] = v
    ld_o[...] = ld
    kka_o[:, 0:RW_W] = kk * a[:, 0:RW_W]
    kka_o[:, RW_W:] = kk * a[:, RW_W:]
    kd_o[:, 0:RW_W] = kd_f
    kd_o[:, RW_W:] = kd_b
    g_o[...] = g
    bonus_o[...] = _dot2(r * (kd_f + kd_b) * rk_ref[...], ones) * v


def _rw_pre(urw, p, seq):
    tokens = urw.shape[0]
    tm = _row_tile(seq)
    tpr = seq // tm
    nt = tokens // tm
    r8 = tm // 8
    row = lambda w: pl.BlockSpec((tm, w), lambda i: (i, 0))
    in_specs = [row(RW_IN_W),
                pl.BlockSpec((8, RW_IN_W), lambda i: (jnp.maximum(i * r8 - 1, 0), 0)),
                pl.BlockSpec((8, RW_IN_W), lambda i: (jnp.minimum((i + 1) * r8, nt * r8 - 1), 0)),
                _full((2, RW_IN_W)), _full((1, 2 * RW_W)), _full((128, 2 * RW_W)), _full((1, 2 * RW_W)),
                _full((128, 2 * RW_W)), _full((128, RW_W)), _full((1, RW_W)), _full((1, RW_W)),
                _full((1, RW_W)), _full((RW_W, RW_W))]
    widths = [RW_W, RW_W, RW_W, 2 * RW_W, 2 * RW_W, 2 * RW_W, RW_W, RW_W]
    return pl.pallas_call(
        functools.partial(_rw_pre_kernel, tiles_per_req=tpr),
        grid=(nt,),
        in_specs=in_specs,
        out_specs=[row(w) for w in widths],
        out_shape=[jax.ShapeDtypeStruct((tokens, w), F32) for w in widths],
        compiler_params=_cp("parallel"),
        name="rw_pre",
    )(urw, urw, urw, p['mu'], p['w0'], p['w2'], p['a0'], p['a2'], p['g2'], p['k_k'], p['k_a'], p['r_k'],
      p['ones256'])


def _rw_masks():
    t = RW_CHUNK
    n = RW_HEADS * t
    tt = np.arange(t)[:, None]
    ss = (np.arange(n) % t)[None, :]
    before = np.stack([ss < tt, ss > tt])
    diag = (ss == tt)
    strict = before.astype(np.float32)
    incl = (before | diag[None]).astype(np.float32)
    eye = diag.astype(np.float32)
    ti = np.arange(t)
    tri = np.stack([ti[None, :] <= ti[:, None], ti[None, :] >= ti[:, None]]).astype(np.float32)
    hd = np.arange(n) // t
    same = (hd[:, None] == hd[None, :]).astype(np.float32)
    return (jnp.asarray(strict), jnp.asarray(incl), jnp.asarray(tri, dtype=BF16), jnp.asarray(same),
            jnp.asarray(eye))


def _rw_scan_kernel(r_ref, kk_ref, v_ref, ld_ref, kka_ref, kd_ref, s0_ref, strict_ref, incl_ref, tri_ref,
                    same_ref, eye_ref, o_ref, sfin_ref, s_scr, *, nsub):
    d = pl.program_id(1)
    c = pl.program_id(2)
    t = RW_CHUNK
    n = RW_HEADS * t

    @pl.when(c == 0)
    def _():
        s_scr[...] = s0_ref[0, 0]

    strict = strict_ref[0]
    incl = incl_ref[0]
    tri = tri_ref[0]
    eye = eye_ref[...]
    same = same_ref[...]
    same_bf = _bf(same)

    def bd(x):
        return jnp.concatenate([_bf(x)] * RW_HEADS, axis=0) * same_bf

    js = range(nsub)
    rows = [pl.ds(pl.multiple_of((j + d * (nsub - 1 - 2 * j)) * t, t), t) for j in js]
    ld = [ld_ref[rows[j], :] for j in js]
    cum = []
    for j in js:
        lhi, lmid, llo = _split3(ld[j])
        cum.append(_dot(tri, lhi) + _dot(tri, lmid) + _dot(tri, llo))
    cend = [jnp.sum(ld[j], axis=0, keepdims=True) for j in js]
    kka = [kka_ref[rows[j], :] for j in js]
    kd = [kd_ref[rows[j], :] for j in js]
    v = [v_ref[rows[j], :] for j in js]
    at = [-kk_ref[rows[j], :] * jnp.exp(cum[j] - ld[j]) for j in js]
    rt = [r_ref[rows[j], :] * jnp.exp(cum[j]) for j in js]
    e_inv = [jnp.exp(-cum[j]) for j in js]
    aa = [_dot_nt(_bf(jnp.concatenate([at[j], rt[j]], axis=0)),
                  jnp.concatenate([bd(kka[j] * e_inv[j]), bd(kd[j] * e_inv[j])], axis=0)) for j in js]
    a_ab = [aa[j][0:t, 0:n] * strict for j in js]
    x = [eye + a_ab[j] for j in js]
    pw = a_ab
    for _ in range(5):
        pw = [_dot(_bf(pw[j]), bd(pw[j])) for j in js]
        x = [x[j] + _dot(_bf(x[j]), bd(pw[j])) for j in js]
    v_bd = [bd(v[j]) for j in js]
    wv = [_dot(_bf(aa[j][0:t, n:] * strict), v_bd[j]) for j in js]
    mu = [_dot(_bf(x[j]), jnp.concatenate([bd(at[j]), bd(wv[j])], axis=1)) for j in js]
    m1 = [mu[j][:, 0:n] for j in js]
    u0 = [mu[j][:, n:] for j in js]
    e_end = [jnp.exp(cend[j] - cum[j]) for j in js]
    bend = [_bf(kka[j] * e_end[j]) for j in js]
    g = [_bf(_dot_tn(_bf(m1[j]), bend[j]) * same) for j in js]
    cst = [_dot_tn(_bf(jnp.concatenate([u0[j], v[j]], axis=0)),
                   jnp.concatenate([bend[j], _bf(kd[j] * e_end[j])], axis=0)) * same for j in js]
    qo = [_dot(_bf(aa[j][t:, 0:n] * incl), jnp.concatenate([bd(m1[j]), bd(u0[j])], axis=1)) for j in js]
    q = [_bf(rt[j] + qo[j][:, 0:n]) for j in js]
    o0 = [qo[j][:, n:] + _dot(_bf(aa[j][t:, n:] * incl), v_bd[j]) for j in js]

    s = s_scr[...]
    for j in js:
        s_bf = _bf(s)
        o_ref[0, rows[j], :] = _dot_nt(q[j], s_bf) + o0[j]
        s = s * jnp.exp(cend[j]) + _dot(s_bf, g[j]) + cst[j]
    s_scr[...] = s

    @pl.when(c == pl.num_programs(2) - 1)
    def _():
        sfin_ref[0, 0] = s


def _rw_scan(r, kk, v, ld, kka, kd, s0_bd, consts, seq):
    tokens = r.shape[0]
    nreq = tokens // seq
    tb = 256
    nblk = seq // tb
    nsub = tb // RW_CHUNK
    n = RW_HEADS * RW_CHUNK
    cc = lambda d, c: c + d * (nblk - 1 - 2 * c)
    shared = pl.BlockSpec((tb, RW_W), lambda b, d, c: (b * nblk + cc(d, c), 0))
    dirw = pl.BlockSpec((tb, RW_W), lambda b, d, c: (b * nblk + cc(d, c), d))
    strict, incl, tri, same, eye = consts
    return pl.pallas_call(
        functools.partial(_rw_scan_kernel, nsub=nsub),
        grid=(nreq, 2, nblk),
        in_specs=[shared, shared, shared, dirw, dirw, dirw,
                  pl.BlockSpec((1, 1, n, n), lambda b, d, c: (b, d, 0, 0)),
                  pl.BlockSpec((1, RW_CHUNK, n), lambda b, d, c: (d, 0, 0)),
                  pl.BlockSpec((1, RW_CHUNK, n), lambda b, d, c: (d, 0, 0)),
                  pl.BlockSpec((1, RW_CHUNK, RW_CHUNK), lambda b, d, c: (d, 0, 0)),
                  _full((n, n)), _full((RW_CHUNK, n))],
        out_specs=[pl.BlockSpec((1, tb, RW_W), lambda b, d, c: (d, b * nblk + cc(d, c), 0)),
                   pl.BlockSpec((1, 1, n, n), lambda b, d, c: (b, d, 0, 0))],
        out_shape=[jax.ShapeDtypeStruct((2, tokens, RW_W), F32), jax.ShapeDtypeStruct((nreq, 2, n, n), F32)],
        scratch_shapes=[pltpu.VMEM((n, n), F32)],
        compiler_params=_cp("parallel", "parallel", "arbitrary"),
        name="rw_scan",
    )(r, kk, v, ld, kka, kd, s0_bd, strict, incl, tri, same, eye)


def _state_to_blockdiag(s):
    b = s.shape[0]
    eye = jnp.eye(RW_HEADS, dtype=s.dtype)
    out = s[:, :, :, :, None, :] * eye[None, None, :, None, :, None]
    return out.reshape(b, 2, RW_W, RW_W)


def _blockdiag_to_state(sbd):
    b = sbd.shape[0]
    s = sbd.reshape(b, 2, RW_HEADS, HEAD_DIM, RW_HEADS, HEAD_DIM)
    idx = jnp.arange(RW_HEADS)
    return jnp.transpose(s[:, :, idx, :, idx, :], (1, 2, 0, 3, 4))


def _finish_kernel(x_ref, oa_ref, oc_ref, o2_ref, bonus_ref, g_ref, mod_ref, wout_ref, lng_ref, lnb_ref,
                   n2_ref, wr_hi_ref, wr_lo_ref, ones_ref, x1_ref, h2_ref, aff_ref):
    ones = ones_ref[...]
    y = o2_ref[0] + o2_ref[1]
    mu = _dot2(y, ones) * (1.0 / HEAD_DIM)
    yc = y - mu
    var = _dot2(yc * yc, ones) * (1.0 / HEAD_DIM)
    yn = yc * lax.rsqrt(var + GN_EPS) * lng_ref[...] + lnb_ref[...]
    ob = (yn + bonus_ref[...]) * g_ref[...]
    mixin = jnp.concatenate([_bf(oa_ref[...]), _bf(ob), _bf(oc_ref[...])], axis=1)
    mix = _dot(mixin, wout_ref[...])
    g1 = mod_ref[0, 2:3, :]
    sh2 = mod_ref[0, 3:4, :]
    sc2 = mod_ref[0, 4:5, :]
    x1 = x_ref[...] + g1 * mix
    ms = jnp.mean(x1 * x1, axis=-1, keepdims=True)
    h2 = x1 * lax.rsqrt(ms + NORM_EPS) * n2_ref[...] * (1.0 + sc2) + sh2
    x1_ref[...] = x1
    h2_ref[...] = _bf(h2)
    hhi, hlo = _split2(h2)
    logits = _dot(hhi, wr_hi_ref[...]) + _dot(hlo, wr_hi_ref[...]) + _dot(hhi, wr_lo_ref[...])
    m = jnp.max(logits, axis=-1, keepdims=True)
    e = jnp.exp(logits - m)
    aff_ref[...] = e / jnp.sum(e, axis=-1, keepdims=True)


def _finish(x, oa, oc, o2, bonus, g, mod, p, seq, per_request_mod):
    tokens, d = x.shape
    tm = _row_tile(seq)
    tpr = seq // tm
    mod_map = (lambda i: (i // tpr, 0, 0)) if per_request_mod else (lambda i: (0, 0, 0))
    row = lambda w: pl.BlockSpec((tm, w), lambda i: (i, 0))
    return pl.pallas_call(
        _finish_kernel,
        grid=(tokens // tm,),
        in_specs=[row(d), row(NA_W), row(SWA_W), pl.BlockSpec((2, tm, RW_W), lambda i: (0, i, 0)), row(RW_W),
                  row(RW_W), pl.BlockSpec((1, 6, d), mod_map), _full((d, d)), _full((1, RW_W)),
                  _full((1, RW_W)), _full((1, d)), _full((d, N_EXPERTS)), _full((d, N_EXPERTS)),
                  _full((RW_W, RW_W))],
        out_specs=[row(d), row(d), row(N_EXPERTS)],
        out_shape=[jax.ShapeDtypeStruct((tokens, d), F32), jax.ShapeDtypeStruct((tokens, d), BF16),
                   jax.ShapeDtypeStruct((tokens, N_EXPERTS), F32)],
        compiler_params=_cp("parallel"),
        name="finish",
    )(x, oa, oc, o2, bonus, g, mod, p['w_out'], p['ln_g'], p['ln_b'], p['n2'], p['wr_hi'], p['wr_lo'],
      p['ones256'])


def _topk_kernel(aff_ref, tri_ref, eye_ref, place_ref, slot_ref, slotrow_ref, gfull_ref, ends_ref, *, cap, group,
                 tb):
    b = pl.program_id(0)
    aff = aff_ref[...]
    seq = aff.shape[0]
    bits = lax.bitcast_convert_type(aff, jnp.int32)
    capf = jnp.float32(cap)

    def bis(_, carry):
        lo, hi = carry
        mid = lo + ((hi - lo + 1) >> 1)
        cnt = jnp.sum(jnp.where(bits >= mid, 1.0, 0.0), axis=0, keepdims=True)
        ge = cnt >= capf
        return jnp.where(ge, mid, lo), jnp.where(ge, hi, mid - 1)

    lo0 = jnp.zeros((1, N_EXPERTS), jnp.int32)
    hi0 = jnp.full((1, N_EXPERTS), 0x7F7FFFFF, jnp.int32)
    thr, _ = lax.fori_loop(0, 31, bis, (lo0, hi0))
    gt = jnp.where(bits > thr, 1.0, 0.0)
    eq = jnp.where(bits == thr, 1.0, 0.0)
    need = capf - jnp.sum(gt, axis=0, keepdims=True)
    offset = ((b % group) * cap).astype(F32)
    tri = tri_ref[...]
    eye = eye_ref[...]
    carry_g = jnp.zeros((1, N_EXPERTS), F32)
    carry_e = jnp.zeros((1, N_EXPERTS), F32)
    ghi, gmid, glo = _split3(aff)
    for blk in range(seq // tb):
        sl = slice(blk * tb, (blk + 1) * tb)
        pg = _dot(tri, _bf(gt[sl])) + carry_g
        pe = _dot(tri, _bf(eq[sl])) + carry_e
        carry_g = pg[tb - 1:tb, :]
        carry_e = pe[tb - 1:tb, :]
        sel = gt[sl] + eq[sl] * jnp.where(pe <= need, 1.0, 0.0)
        slot = jnp.where(sel > 0.5, pg + jnp.minimum(pe, need) - 1.0 + offset, -1.0)
        slot_ref[sl, :] = slot
        ends_ref[0, blk:blk + 1, :] = carry_g + jnp.minimum(carry_e, need) + offset
        shi, slo = _split2(slot)
        slotrow_ref[0, :, 0, sl] = _dot_nt(eye, shi) + _dot_nt(eye, slo)
        gfull_ref[sl, :] = _bf(_dot(ghi[sl], place_ref[0]) + _dot(gmid[sl], place_ref[1])
                               + _dot(glo[sl], place_ref[2]))


def _topk(aff, seq, group):
    tokens = aff.shape[0]
    nreq = tokens // seq
    cap = EC_CAPACITY * seq // N_EXPERTS
    tb = min(seq, 512)
    ti = np.arange(tb)
    tri = jnp.asarray((ti[None, :] <= ti[:, None]).astype(np.float32), dtype=BF16)
    eye = jnp.asarray(np.eye(N_EXPERTS, dtype=np.float32), dtype=BF16)
    place = np.zeros((3, N_EXPERTS, 128), np.float32)
    for s in range(3):
        place[s, np.arange(N_EXPERTS), s * N_EXPERTS + np.arange(N_EXPERTS)] = 1.0
    place = jnp.asarray(place, dtype=BF16)
    nblk = seq // tb
    slot, slotrow, gfull, ends = pl.pallas_call(
        functools.partial(_topk_kernel, cap=cap, group=group, tb=tb),
        grid=(nreq,),
        in_specs=[pl.BlockSpec((seq, N_EXPERTS), lambda b: (b, 0)), _full((tb, tb)),
                  _full((N_EXPERTS, N_EXPERTS)), _full((3, N_EXPERTS, 128))],
        out_specs=[pl.BlockSpec((seq, N_EXPERTS), lambda b: (b, 0)),
                   pl.BlockSpec((1, N_EXPERTS, 1, seq), lambda b: (b // group, 0, 0, b % group)),
                   pl.BlockSpec((seq, 128), lambda b: (b, 0)),
                   pl.BlockSpec((1, nblk, N_EXPERTS), lambda b: (b, 0, 0))],
        out_shape=[jax.ShapeDtypeStruct((tokens, N_EXPERTS), F32),
                   jax.ShapeDtypeStruct((nreq // group, N_EXPERTS, 1, group * seq), F32),
                   jax.ShapeDtypeStruct((tokens, 128), BF16),
                   jax.ShapeDtypeStruct((nreq, nblk, N_EXPERTS), F32)],
        compiler_params=_cp("parallel"),
        name="topk",
    )(aff, tri, eye, place)
    ends = ends.reshape(nreq // group, group * nblk, N_EXPERTS).transpose(0, 2, 1)
    return slot, slotrow, gfull, ends.astype(jnp.int32).reshape(-1), tb


MOE_EB = 8


def _moe_dispatch_kernel(ends_ref, h_ref, slotrow_ref, gfull_ref, xe_ref, gs_ref, *, ct, nch):
    gi = pl.program_id(0)
    eb = pl.program_id(1)
    c = pl.program_id(2)
    mt = 128

    @pl.when(c == 0)
    def _():
        xe_ref[...] = jnp.zeros_like(xe_ref)
        gs_ref[...] = jnp.zeros_like(gs_ref)

    jcol = lax.broadcasted_iota(jnp.int32, (mt, 1), 0)
    starts, his, pieces = [], [], []
    for i in range(MOE_EB):
        base = (gi * N_EXPERTS + eb * MOE_EB + i) * nch
        lo = jnp.where(c == 0, 0, ends_ref[base + jnp.maximum(c - 1, 0)])
        his.append(ends_ref[base + c])
        start = pl.multiple_of(jnp.minimum((lo // 16) * 16, ct - mt), 16)
        starts.append(start)
        pieces.append(_bf(jnp.where(slotrow_ref[0, i] == (jcol + start).astype(F32), 1.0, 0.0)))
    onehot = jnp.concatenate(pieces, axis=0)
    xw = _bf(_dot(onehot, h_ref[...]))
    gw = _bf(_dot(onehot, gfull_ref[...]))
    for i in range(MOE_EB):
        rows = pl.ds(starts[i], mt)
        xe_ref[0, i, rows, :] += xw[i * mt:(i + 1) * mt]
        gs_ref[0, i, rows, :] += gw[i * mt:(i + 1) * mt]
    for i in range(MOE_EB):
        for w in range(1, ct // mt):
            wlo = starts[i] + w * mt

            @pl.when(wlo < his[i])
            def _(i=i, wlo=wlo):
                ws = pl.multiple_of(jnp.minimum(wlo, ct - mt), 16)
                slot = slotrow_ref[0, i]
                hit = (jnp.where(slot == (jcol + ws).astype(F32), 1.0, 0.0)
                       * jnp.where(slot >= wlo.astype(F32), 1.0, 0.0))
                rows = pl.ds(ws, mt)
                xe_ref[0, i, rows, :] += _bf(_dot(_bf(hit), h_ref[...]))
                gs_ref[0, i, rows, :] += _bf(_dot(_bf(hit), gfull_ref[...]))


def _moe_dispatch(ends, h2, slotrow, gfull, lg, ct, kc):
    tokens, d = h2.shape
    ngrp = tokens // lg
    nch = lg // kc
    grid_spec = pltpu.PrefetchScalarGridSpec(
        num_scalar_prefetch=1,
        grid=(ngrp, N_EXPERTS // MOE_EB, nch),
        in_specs=[pl.BlockSpec((kc, d), lambda gi, eb, c, ends: (gi * nch + c, 0)),
                  pl.BlockSpec((1, MOE_EB, 1, kc), lambda gi, eb, c, ends: (gi, eb, 0, c)),
                  pl.BlockSpec((kc, 128), lambda gi, eb, c, ends: (gi * nch + c, 0))],
        out_specs=[pl.BlockSpec((1, MOE_EB, ct, d), lambda gi, eb, c, ends: (gi, eb, 0, 0)),
                   pl.BlockSpec((1, MOE_EB, ct, 128), lambda gi, eb, c, ends: (gi, eb, 0, 0))])
    return pl.pallas_call(
        functools.partial(_moe_dispatch_kernel, ct=ct, nch=nch),
        grid_spec=grid_spec,
        out_shape=[jax.ShapeDtypeStruct((ngrp, N_EXPERTS, ct, d), BF16),
                   jax.ShapeDtypeStruct((ngrp, N_EXPERTS, ct, 128), BF16)],
        compiler_params=_cp("parallel", "parallel", "arbitrary"),
        name="moe_dispatch",
    )(ends, h2, slotrow, gfull)


def _moe_ffn_kernel(xe_ref, gs_ref, mod_ref, wg_ref, wu_ref, wd_ref, ye_ref):
    e = pl.program_id(1)
    lane = lax.broadcasted_iota(jnp.int32, (1, 128), 1)
    pick = (lane == e) | (lane == e + N_EXPERTS) | (lane == e + 2 * N_EXPERTS)
    gate = jnp.sum(jnp.where(pick, gs_ref[0, 0].astype(F32), 0.0), axis=-1, keepdims=True)
    xb = xe_ref[0, 0]
    hg = _dot(xb, wg_ref[0])
    hu = _dot(xb, wu_ref[0])
    he = _bf(hg * _sigmoid(hg) * hu)
    y = _dot(he, wd_ref[0])
    ye_ref[0, 0] = _bf(y * gate * mod_ref[0, 5:6, :])


def _moe_ffn(xe, gs, mod, wg, wu, wd, per_group_mod):
    ngrp, _, ct, d = xe.shape
    f = wg.shape[2]
    mod_map = (lambda gi, e: (gi, 0, 0)) if per_group_mod else (lambda gi, e: (0, 0, 0))
    return pl.pallas_call(
        _moe_ffn_kernel,
        grid=(ngrp, N_EXPERTS),
        in_specs=[pl.BlockSpec((1, 1, ct, d), lambda gi, e: (gi, e, 0, 0)),
                  pl.BlockSpec((1, 1, ct, 128), lambda gi, e: (gi, e, 0, 0)),
                  pl.BlockSpec((1, 6, d), mod_map),
                  pl.BlockSpec((1, d, f), lambda gi, e: (e, 0, 0)),
                  pl.BlockSpec((1, d, f), lambda gi, e: (e, 0, 0)),
                  pl.BlockSpec((1, f, d), lambda gi, e: (e, 0, 0))],
        out_specs=pl.BlockSpec((1, 1, ct, d), lambda gi, e: (gi, e, 0, 0)),
        out_shape=jax.ShapeDtypeStruct((ngrp, N_EXPERTS, ct, d), BF16),
        compiler_params=_cp("parallel", "parallel"),
        name="moe_ffn",
    )(xe, gs, mod, wg, wu, wd)


def _moe_combine_kernel(ends_ref, x1_ref, slot_ref, ye_ref, o_ref, win_scr, *, ct, nch):
    gi = pl.program_id(0)
    j = pl.program_id(1)
    mt = 128
    wide = N_EXPERTS * mt
    shi, slo = _split2(slot_ref[...])
    col_e = lax.broadcasted_iota(jnp.int32, (N_EXPERTS, wide), 1) // mt
    row_e = lax.broadcasted_iota(jnp.int32, (N_EXPERTS, wide), 0)
    expand = _bf(jnp.where(col_e == row_e, 1.0, 0.0))
    sb = _dot(shi, expand) + _dot(slo, expand)
    lane = lax.broadcasted_iota(jnp.int32, (1, mt), 1)
    starts, his, targets = [], [], []
    for e in range(N_EXPERTS):
        base = (gi * N_EXPERTS + e) * nch
        lo = jnp.where(j == 0, 0, ends_ref[base + jnp.maximum(j - 1, 0)])
        his.append(ends_ref[base + j])
        start = pl.multiple_of(jnp.minimum((lo // 16) * 16, ct - mt), 16)
        win_scr[e * mt:(e + 1) * mt, :] = ye_ref[0, e, pl.ds(start, mt), :]
        starts.append(start)
        targets.append((lane + start).astype(F32))
    onehot = _bf(jnp.where(sb == jnp.concatenate(targets, axis=1), 1.0, 0.0))
    o_ref[...] = x1_ref[...] + _dot(onehot, win_scr[...])
    for e in range(N_EXPERTS):
        for w in range(1, ct // mt):
            wlo = starts[e] + w * mt

            @pl.when(wlo < his[e])
            def _(e=e, wlo=wlo):
                ws = pl.multiple_of(jnp.minimum(wlo, ct - mt), 16)
                sbe = sb[:, e * mt:(e + 1) * mt]
                hit = jnp.where(sbe == (lane + ws).astype(F32), 1.0, 0.0) * jnp.where(sbe >= wlo.astype(F32), 1.0, 0.0)
                o_ref[...] += _dot(_bf(hit), ye_ref[0, e, pl.ds(ws, mt), :])


def _moe_combine(ends, x1, slot, ye, lg, ct, kc):
    tokens, d = x1.shape
    ngrp = tokens // lg
    nch = lg // kc
    grid_spec = pltpu.PrefetchScalarGridSpec(
        num_scalar_prefetch=1,
        grid=(ngrp, nch),
        in_specs=[pl.BlockSpec((kc, d), lambda gi, j, ends: (gi * nch + j, 0)),
                  pl.BlockSpec((kc, N_EXPERTS), lambda gi, j, ends: (gi * nch + j, 0)),
                  pl.BlockSpec((1, N_EXPERTS, ct, d), lambda gi, j, ends: (gi, 0, 0, 0),
                               pipeline_mode=pl.Buffered(1))],
        out_specs=pl.BlockSpec((kc, d), lambda gi, j, ends: (gi * nch + j, 0)),
        scratch_shapes=[pltpu.VMEM((N_EXPERTS * 128, d), BF16)])
    return pl.pallas_call(
        functools.partial(_moe_combine_kernel, ct=ct, nch=nch),
        grid_spec=grid_spec,
        out_shape=jax.ShapeDtypeStruct((tokens, d), F32),
        compiler_params=_cp("parallel", "arbitrary"),
        name="moe_combine",
    )(ends, x1, slot, ye)


def _rope_tables(seq):
    t = np.arange(seq)
    n_freq = HEAD_DIM // 4
    inv = ROPE_THETA ** (-np.arange(n_freq, dtype=np.float32) / n_freq)
    ang = np.concatenate([(t // GRID_W).astype(np.float32)[:, None] * inv,
                          (t % GRID_W).astype(np.float32)[:, None] * inv], axis=-1)
    ang = jnp.asarray(ang, dtype=F32)
    cos, sin = jnp.cos(ang), jnp.sin(ang)
    cos_t = jnp.tile(jnp.concatenate([cos, cos], axis=-1), (1, SWA_HEADS))
    sin_t = jnp.tile(jnp.concatenate([-sin, sin], axis=-1), (1, SWA_HEADS))
    return cos_t, sin_t


def _blockdiag2(w):
    z = jnp.zeros_like(w[0])
    return jnp.concatenate([jnp.concatenate([w[0], z], axis=1), jnp.concatenate([z, w[1]], axis=1)], axis=0)


def _layer_params(l, ada_w, ada_b, norm1_g, norm2_g, w_in, na_q_norm, na_k_norm, na_rpb, rw_mu, rw_w0, rw_w2,
                  rw_a0, rw_a2, rw_g2, rw_k_k, rw_k_a, rw_r_k, rw_ln_g, rw_ln_b, swa_q_norm, swa_k_norm,
                  swa_sink, w_out, w_router, w_gate, w_up, w_down):
    wr = w_router[l]
    wr_hi = wr.astype(BF16)
    return {
        'n1': norm1_g[l][None], 'n2': norm2_g[l][None], 'w_in': w_in[l].astype(BF16),
        'gains': (jnp.tile(na_q_norm[l], NA_HEADS)[None], jnp.tile(na_k_norm[l], NA_HEADS)[None],
                  jnp.tile(swa_q_norm[l], SWA_HEADS)[None], jnp.tile(swa_k_norm[l], SWA_KV_HEADS)[None]),
        'bias': _na_bias_table(na_rpb[l]),
        'mu': rw_mu[l], 'w0': rw_w0[l].reshape(1, 2 * RW_W), 'w2': _blockdiag2(rw_w2[l]).astype(BF16),
        'a0': rw_a0[l].reshape(1, 2 * RW_W), 'a2': _blockdiag2(rw_a2[l]).astype(BF16),
        'g2': rw_g2[l].astype(BF16), 'k_k': rw_k_k[l][None], 'k_a': rw_k_a[l][None],
        'r_k': rw_r_k[l].reshape(1, RW_W), 'ln_g': rw_ln_g[l][None], 'ln_b': rw_ln_b[l][None],
        'sink': swa_sink[l], 'w_out': w_out[l].astype(BF16),
        'wr_hi': wr_hi, 'wr_lo': (wr - wr_hi.astype(F32)).astype(BF16),
        'wg': w_gate[l].astype(BF16), 'wu': w_up[l].astype(BF16), 'wd': w_down[l].astype(BF16),
        'ones256': _block_ones(RW_W),
    }


def _mix_and_ffn(x, mod, p, oa, oc, urw, s0_bd, scan_consts, seq, per_request_mod, group):
    r, kk, v, ld, kka, kd, g, bonus = _rw_pre(urw, p, seq)
    o2, sfin = _rw_scan(r, kk, v, ld, kka, kd, s0_bd, scan_consts, seq)
    x1, h2, aff = _finish(x, oa, oc, o2, bonus, g, mod, p, seq, per_request_mod)
    slot, slotrow, gfull, ends, kc = _topk(aff, seq, group)
    cap = EC_CAPACITY * seq // N_EXPERTS
    lg, ct = group * seq, group * cap
    xe, gs = _moe_dispatch(ends, h2, slotrow, gfull, lg, ct, kc)
    ye = _moe_ffn(xe, gs, mod, p['wg'], p['wu'], p['wd'], per_request_mod)
    return _moe_combine(ends, x1, slot, ye, lg, ct, kc), sfin


def _context_layer(x, mod, p, ones384, scan_consts, seq):
    qa, ka, va, urw, qc, kc, vc = _proj(x, mod, p['n1'], p['w_in'], p['gains'], ones384, None, seq, False, F32)
    oa, oc = _ctx_attn(p['sink'], qa, ka, va, qc, kc, vc, seq)
    nreq = x.shape[0] // seq
    s0 = jnp.zeros((nreq, 2, RW_W, RW_W), F32)
    y, sfin = _mix_and_ffn(x, mod, p, oa, oc, urw, s0, scan_consts, seq, False, 8)
    return y, ka, va, kc, vc, sfin


def _latent_layer(x, mod, p, ones384, scan_consts, rope_tabs, seq, kx_na, vx_na, kx_swa, vx_swa, s0_bd, past):
    qa, ka, va, urw, qc, kc, vc = _proj(x, mod, p['n1'], p['w_in'], p['gains'], ones384, rope_tabs, seq, True,
                                        BF16)
    oa = _na_attn(qa, ka, va, kx_na, vx_na, p['bias'], seq, past)
    oc = _swa_attn(p['sink'], qc, kc, vc, kx_swa, vx_swa, seq, past)
    y, _ = _mix_and_ffn(x, mod, p, oa, oc, urw, s0_bd, scan_consts, seq, True, 1)
    return y


def _heads_first(z, nreq, seq, heads):
    return z.reshape(nreq, seq, heads, HEAD_DIM).transpose(0, 2, 1, 3)


def _tokens_first(z):
    b, h, n, dh = z.shape
    return z.transpose(0, 2, 1, 3).reshape(b * n, h * dh)


def kernel(x_prompt, x_sample, cache_na_k, cache_na_v, cache_swa_k, cache_swa_v, state_rwkv, c, c_ctx, ada_w, ada_b, norm1_g, norm2_g, w_in, na_q_norm, na_k_norm, na_rpb, rw_mu, rw_w0, rw_w2, rw_a0, rw_a2, rw_g2, rw_k_k, rw_k_a, rw_r_k, rw_ln_g, rw_ln_b, swa_q_norm, swa_k_norm, swa_sink, w_out, w_router, w_gate, w_up, w_down):
    nb, seq, d = x_prompt.shape
    db, dseq, _ = x_sample.shape
    depth = ada_w.shape[0]
    past = cache_na_k.shape[3]
    cond = jnp.concatenate([c, c_ctx[None], jnp.zeros((16 - db - 1, d), F32)], axis=0)
    mod_all = _adaln(cond, ada_w, ada_b).reshape(depth, 16, 6, d)
    ones384 = _block_ones(NA_W)
    scan_consts = _rw_masks()
    rope_tabs = _rope_tables(dseq)
    xp = x_prompt.reshape(nb * seq, d)
    xs = x_sample.reshape(db * dseq, d)
    new_ka, new_va, new_kc, new_vc, new_s = [], [], [], [], []
    for l in range(depth):
        p = _layer_params(l, ada_w, ada_b, norm1_g, norm2_g, w_in, na_q_norm, na_k_norm, na_rpb, rw_mu, rw_w0,
                          rw_w2, rw_a0, rw_a2, rw_g2, rw_k_k, rw_k_a, rw_r_k, rw_ln_g, rw_ln_b, swa_q_norm,
                          swa_k_norm, swa_sink, w_out, w_router, w_gate, w_up, w_down)
        mod_ctx = mod_all[l, db:db + 1]
        mod_lat = mod_all[l, 0:db]
        xp, ka, va, kc, vc, sfin = _context_layer(xp, mod_ctx, p, ones384, scan_consts, seq)
        new_ka.append(_heads_first(ka, nb, seq, NA_HEADS))
        new_va.append(_heads_first(va, nb, seq, NA_HEADS))
        new_kc.append(_heads_first(kc, nb, seq, SWA_KV_HEADS))
        new_vc.append(_heads_first(vc, nb, seq, SWA_KV_HEADS))
        new_s.append(_blockdiag_to_state(sfin))
        xs = _latent_layer(xs, mod_lat, p, ones384, scan_consts, rope_tabs, dseq,
                           _tokens_first(cache_na_k[:, l]), _tokens_first(cache_na_v[:, l]),
                           _tokens_first(cache_swa_k[:, l]), _tokens_first(cache_swa_v[:, l]),
                           _state_to_blockdiag(state_rwkv[:, l]), past)
    return (xp.reshape(nb, seq, d), xs.reshape(db, dseq, d), jnp.stack(new_ka, axis=1),
            jnp.stack(new_va, axis=1), jnp.stack(new_kc, axis=1), jnp.stack(new_vc, axis=1),
            jnp.stack(new_s, axis=1))
```

```python
import functools

import numpy as np
import jax
import jax.numpy as jnp
from jax import lax
from jax.experimental import pallas as pl
from jax.experimental.pallas import tpu as pltpu

F32 = jnp.float32
BF16 = jnp.bfloat16

HEAD_DIM = 64
GRID_W = 64
NA_HEADS = 6
NA_KH = 8
NA_KW = 16
RW_HEADS = 4
SWA_HEADS = 6
SWA_KV_HEADS = 2
SWA_WINDOW = 128
N_EXPERTS = 16
EC_CAPACITY = 2
ROPE_THETA = 10000.0
NORM_EPS = 1e-6
GN_EPS = 64e-5
NEG_INF = -1e30
NA_UNROLL = 4
SWA_UNROLL = 2
RW_CHUNK = 64
RW_W = RW_HEADS * HEAD_DIM
NA_W = NA_HEADS * HEAD_DIM
SWA_W = SWA_HEADS * HEAD_DIM
SWA_KV_W = SWA_KV_HEADS * HEAD_DIM
RW_IN_W = 1152
VMEM_LIMIT = 56 * 1024 * 1024


def _cp(*sem):
    return pltpu.CompilerParams(dimension_semantics=sem, vmem_limit_bytes=VMEM_LIMIT)


def _bf(x):
    return x.astype(BF16)


def _dot(a, b):
    return jnp.dot(a, b, preferred_element_type=F32)


def _dot_nt(a, b):
    return lax.dot_general(a, b, (((1,), (1,)), ((), ())), preferred_element_type=F32)


def _dot_tn(a, b):
    return lax.dot_general(a, b, (((0,), (0,)), ((), ())), preferred_element_type=F32)


def _split2(x):
    hi = x.astype(BF16)
    lo = (x - hi.astype(F32)).astype(BF16)
    return hi, lo


def _split3(x):
    hi = x.astype(BF16)
    r1 = x - hi.astype(F32)
    mid = r1.astype(BF16)
    lo = (r1 - mid.astype(F32)).astype(BF16)
    return hi, mid, lo


def _dot2(a, b_bf):
    hi, lo = _split2(a)
    return _dot(hi, b_bf) + _dot(lo, b_bf)


def _sigmoid(x):
    return 1.0 / (1.0 + jnp.exp(-x))


def _block_ones(width):
    i = np.arange(width) // HEAD_DIM
    return jnp.asarray((i[:, None] == i[None, :]).astype(np.float32), dtype=BF16)


def _row_tile(seq):
    return 512 if seq % 512 == 0 else 256


def _full(shape):
    return pl.BlockSpec(shape, lambda *_: (0,) * len(shape))


def _adaln_kernel(c_ref, w_ref, b_ref, o_ref):
    c = c_ref[...]
    s = c * _sigmoid(c)
    shi, slo = _split2(s)
    whi, wlo = _split2(w_ref[0])
    o_ref[0] = _dot(shi, whi) + _dot(slo, whi) + _dot(shi, wlo) + b_ref[0]


def _adaln(cond, ada_w, ada_b):
    nl, d, n6 = ada_w.shape
    tn = 1536
    rows = cond.shape[0]
    return pl.pallas_call(
        _adaln_kernel,
        grid=(nl, n6 // tn),
        in_specs=[pl.BlockSpec((rows, d), lambda l, j: (0, 0)),
                  pl.BlockSpec((1, d, tn), lambda l, j: (l, 0, j)),
                  pl.BlockSpec((1, 1, tn), lambda l, j: (l, 0, j))],
        out_specs=pl.BlockSpec((1, rows, tn), lambda l, j: (l, 0, j)),
        out_shape=jax.ShapeDtypeStruct((nl, rows, n6), F32),
        compiler_params=_cp("parallel", "parallel"),
        name="adaln",
    )(cond, ada_w, ada_b.reshape(nl, 1, n6))


def _head_norm(z, gain, ones_bf):
    ms = _dot2(z * z, ones_bf) * (1.0 / HEAD_DIM)
    return z * lax.rsqrt(ms + NORM_EPS) * gain


def _rope(z, cos, sin_signed):
    w = z.shape[1]
    lane = lax.broadcasted_iota(jnp.int32, z.shape, 1)
    first = (lane % HEAD_DIM) < (HEAD_DIM // 2)
    swapped = jnp.where(first, pltpu.roll(z, w - HEAD_DIM // 2, 1), pltpu.roll(z, HEAD_DIM // 2, 1))
    return z * cos + swapped * sin_signed


def _proj_kernel(*refs, rope, split_rw):
    refs = list(refs)
    x_ref, mod_ref, n1_ref, w_ref, gqa_ref, gka_ref, gqc_ref, gkc_ref, ones_ref = refs[:9]
    qa_ref, ka_ref, va_ref, urw_ref, qc_ref, kc_ref, vc_ref = refs[-7:]
    extra = refs[9:-7]
    if rope:
        cos_ref, sin_ref = extra[:2]
    x = x_ref[...]
    sh1 = mod_ref[0, 0:1, :]
    sc1 = mod_ref[0, 1:2, :]
    ms = jnp.mean(x * x, axis=-1, keepdims=True)
    h = x * lax.rsqrt(ms + NORM_EPS) * n1_ref[...] * (1.0 + sc1) + sh1
    h_hi = _bf(h)
    u = _dot(h_hi, w_ref[...])
    o0 = 0
    o1 = NA_W
    o2 = 2 * NA_W
    o3 = 3 * NA_W
    o4 = o3 + RW_IN_W
    o5 = o4 + SWA_W
    o6 = o5 + SWA_KV_W
    ones = ones_ref[...]
    ones_kv = ones_ref[0:SWA_KV_W, 0:SWA_KV_W]
    qa = _head_norm(u[:, o0:o1], gqa_ref[...], ones)
    ka = _head_norm(u[:, o1:o2], gka_ref[...], ones)
    qc = _head_norm(u[:, o4:o5], gqc_ref[...], ones)
    kc = _head_norm(u[:, o5:o6], gkc_ref[...], ones_kv)
    if rope:
        qc = _rope(qc, cos_ref[...], sin_ref[...])
        kc = _rope(kc, cos_ref[:, 0:SWA_KV_W], sin_ref[:, 0:SWA_KV_W])
    qa_ref[...] = qa.astype(qa_ref.dtype)
    ka_ref[...] = ka.astype(ka_ref.dtype)
    va_ref[...] = u[:, o2:o3].astype(va_ref.dtype)
    urw = u[:, o3:o4]
    if split_rw:
        h_lo = _bf(h - h_hi.astype(F32))
        urw = urw + _dot(h_lo, w_ref[:, o3:o4]) + _dot(h_hi, extra[-1][...])
    urw_ref[...] = urw
    qc_ref[...] = qc.astype(qc_ref.dtype)
    kc_ref[...] = kc.astype(kc_ref.dtype)
    vc_ref[...] = u[:, o6:].astype(vc_ref.dtype)


def _proj(x, mod, n1, w_in_bf, gains, ones384, rope_tabs, seq, per_request_mod, qkv_dtype, w_rw_lo=None):
    tokens, d = x.shape
    tm = _row_tile(seq)
    tiles_per_req = seq // tm
    in_w = w_in_bf.shape[1]
    rope = rope_tabs is not None
    mod_map = (lambda i: (i // tiles_per_req, 0, 0)) if per_request_mod else (lambda i: (0, 0, 0))
    row = lambda w: pl.BlockSpec((tm, w), lambda i: (i, 0))
    in_specs = [row(d), pl.BlockSpec((1, 6, d), mod_map), _full((1, d)), _full((d, in_w)),
                _full((1, NA_W)), _full((1, NA_W)), _full((1, SWA_W)), _full((1, SWA_KV_W)),
                _full((NA_W, NA_W))]
    args = [x, mod, n1, w_in_bf, *gains, ones384]
    if rope:
        tab = pl.BlockSpec((tm, SWA_W), lambda i: (i % tiles_per_req, 0))
        in_specs += [tab, tab]
        args += list(rope_tabs)
    if w_rw_lo is not None:
        in_specs.append(_full((d, RW_IN_W)))
        args.append(w_rw_lo)
    widths = [NA_W, NA_W, NA_W, RW_IN_W, SWA_W, SWA_KV_W, SWA_KV_W]
    dtypes = [qkv_dtype, qkv_dtype, qkv_dtype, F32, qkv_dtype, qkv_dtype, qkv_dtype]
    return pl.pallas_call(
        functools.partial(_proj_kernel, rope=rope, split_rw=w_rw_lo is not None),
        grid=(tokens // tm,),
        in_specs=in_specs,
        out_specs=[row(w) for w in widths],
        out_shape=[jax.ShapeDtypeStruct((tokens, w), dt) for w, dt in zip(widths, dtypes)],
        compiler_params=_cp("parallel"),
        name="proj",
    )(*args)


def _half_masks(width=2 * HEAD_DIM):
    lane = lax.broadcasted_iota(jnp.int32, (1, width), 1)
    return lane < HEAD_DIM, lane >= HEAD_DIM


def _swap_halves(z):
    return pltpu.roll(z, HEAD_DIM, 1)


def _ctx_attn_kernel(sink_ref, qa_ref, ka_ref, va_ref, qc_ref, kc_ref, vc_ref, oa_ref, oc_ref):
    scale = HEAD_DIM ** -0.5
    m0, m1 = _half_masks()
    masks = (m0, m1)
    for pair in range(NA_HEADS // 2):
        sl = slice(pair * 128, (pair + 1) * 128)
        qp = qa_ref[:, sl].astype(F32) * scale
        kp = _bf(ka_ref[:, sl])
        vp = _bf(va_ref[:, sl])
        outs = []
        for half in range(2):
            qm = _bf(jnp.where(masks[half], qp, 0.0))
            s = _dot_nt(qm, kp)
            m = jnp.max(s, axis=-1, keepdims=True)
            e = jnp.exp(s - m)
            l = jnp.sum(e, axis=-1, keepdims=True)
            outs.append(_dot(_bf(e), vp) / l)
        oa_ref[:, sl] = jnp.where(m0, outs[0], outs[1])
    kc = _bf(kc_ref[...])
    vc = _bf(vc_ref[...])
    group = SWA_HEADS // SWA_KV_HEADS
    for pair in range(SWA_HEADS // 2):
        sl = slice(pair * 128, (pair + 1) * 128)
        qp = qc_ref[:, sl].astype(F32) * scale
        outs = []
        for half in range(2):
            h = 2 * pair + half
            g = h // group
            qh = qp if g == half else _swap_halves(qp)
            qm = _bf(jnp.where(masks[g], qh, 0.0))
            s = _dot_nt(qm, kc)
            sk = sink_ref[h]
            m = jnp.maximum(jnp.max(s, axis=-1, keepdims=True), sk)
            e = jnp.exp(s - m)
            l = jnp.sum(e, axis=-1, keepdims=True) + jnp.exp(sk - m)
            o = _dot(_bf(e), vc) / l
            outs.append(o if g == half else _swap_halves(o))
        oc_ref[:, sl] = jnp.where(m0, outs[0], outs[1])


def _ctx_attn(sink, qa, ka, va, qc, kc, vc, seq):
    tokens = qa.shape[0]
    blk = lambda w: pl.BlockSpec((seq, w), lambda b: (b, 0))
    return pl.pallas_call(
        _ctx_attn_kernel,
        grid=(tokens // seq,),
        in_specs=[pl.BlockSpec(memory_space=pltpu.SMEM), blk(NA_W), blk(NA_W), blk(NA_W), blk(SWA_W),
                  blk(SWA_KV_W), blk(SWA_KV_W)],
        out_specs=[blk(NA_W), blk(SWA_W)],
        out_shape=[jax.ShapeDtypeStruct((tokens, NA_W), F32), jax.ShapeDtypeStruct((tokens, SWA_W), F32)],
        compiler_params=_cp("parallel"),
        name="ctx_attn",
    )(sink, qa, ka, va, qc, kc, vc)


def _na_bias_kernel(rpb_ref, o_ref):
    h = pl.program_id(0)
    nrow = 2 * NA_KH - 1
    ncol = 2 * NA_KW - 1
    width = NA_KH * GRID_W
    shape = (GRID_W, width)
    lane = lax.broadcasted_iota(jnp.int32, shape, 1)
    qc = lax.broadcasted_iota(jnp.int32, shape, 0)
    kc = lane % GRID_W
    c_start = jnp.clip(qc - NA_KW // 2, 0, GRID_W - NA_KW)
    ok = (kc >= c_start) & (kc < c_start + NA_KW)
    d_col = jnp.clip(kc - qc, 1 - NA_KW, NA_KW - 1) + NA_KW - 1
    key_row = lax.broadcasted_iota(jnp.int32, (1, width), 1) // GRID_W

    def case_body(case, carry):
        acc = jnp.zeros(shape, F32)
        for dc in range(ncol):
            val = jnp.zeros((1, width), F32)
            for i in range(NA_KH):
                val = jnp.where(key_row == i, rpb_ref[(h * nrow + case + i) * ncol + dc], val)
            acc = jnp.where(d_col == dc, val, acc)
        o_ref[0, pl.ds(case, 1)] = jnp.where(ok, acc, NEG_INF)[None]
        return carry

    lax.fori_loop(0, NA_KH, case_body, 0)


def _na_bias_table(rpb):
    nh = rpb.shape[0]
    return pl.pallas_call(
        _na_bias_kernel,
        grid=(nh,),
        in_specs=[pl.BlockSpec(memory_space=pltpu.SMEM)],
        out_specs=pl.BlockSpec((1, NA_KH, GRID_W, NA_KH * GRID_W), lambda h: (h // 2, 0, h % 2, 0)),
        out_shape=jax.ShapeDtypeStruct((nh // 2, NA_KH, 2 * GRID_W, NA_KH * GRID_W), F32),
        compiler_params=_cp("parallel"),
        name="na_bias",
    )(rpb.reshape(-1))


def _na_kernel(q_ref, k_ref, v_ref, kx_ref, vx_ref, bias_ref, o_ref, *, rows):
    scale = HEAD_DIM ** -0.5
    m0, m1 = _half_masks()
    kx = _bf(kx_ref[...])
    vx = _bf(vx_ref[...])

    def body(it, carry):
        us = range(NA_UNROLL)
        r = [it * NA_UNROLL + u for u in us]
        rs = [jnp.clip(r[u] - NA_KH // 2, 0, rows - NA_KH) for u in us]
        case = [rs[u] - r[u] + NA_KH - 1 for u in us]
        q0 = [pl.multiple_of(r[u] * GRID_W, GRID_W) for u in us]
        k0 = [pl.multiple_of(rs[u] * GRID_W, GRID_W) for u in us]
        qp = [q_ref[pl.ds(q0[u], GRID_W), :].astype(F32) * scale for u in us]
        kw = [_bf(k_ref[pl.ds(k0[u], NA_KH * GRID_W), :]) for u in us]
        vw = [_bf(v_ref[pl.ds(k0[u], NA_KH * GRID_W), :]) for u in us]
        q2 = [_bf(jnp.concatenate([jnp.where(m0, qp[u], 0.0), jnp.where(m1, qp[u], 0.0)], axis=0)) for u in us]
        sw = [_dot_nt(q2[u], kw[u]) + bias_ref[0, pl.ds(case[u], 1)][0] for u in us]
        sx = [_dot_nt(q2[u], kx) for u in us]
        m = [jnp.maximum(jnp.max(sw[u], axis=-1, keepdims=True), jnp.max(sx[u], axis=-1, keepdims=True))
             for u in us]
        ew = [jnp.exp(sw[u] - m[u]) for u in us]
        ex = [jnp.exp(sx[u] - m[u]) for u in us]
        l = [jnp.sum(ew[u], axis=-1, keepdims=True) + jnp.sum(ex[u], axis=-1, keepdims=True) for u in us]
        o = [(_dot(_bf(ew[u]), vw[u]) + _dot(_bf(ex[u]), vx)) / l[u] for u in us]
        for u in us:
            o_ref[pl.ds(q0[u], GRID_W), :] = jnp.where(m0, o[u][0:GRID_W], o[u][GRID_W:])
        return carry

    lax.fori_loop(0, rows // NA_UNROLL, body, 0)


def _na_attn(q, k, v, kx, vx, bias, seq, past):
    tokens = q.shape[0]
    nb = tokens // seq
    rows = seq // GRID_W
    blk = pl.BlockSpec((seq, 128), lambda b, p: (b, p))
    cblk = pl.BlockSpec((past, 128), lambda b, p: (b, p))
    return pl.pallas_call(
        functools.partial(_na_kernel, rows=rows),
        grid=(nb, NA_HEADS // 2),
        in_specs=[blk, blk, blk, cblk, cblk,
                  pl.BlockSpec((1, NA_KH, 2 * GRID_W, NA_KH * GRID_W), lambda b, p: (p, 0, 0, 0))],
        out_specs=blk,
        out_shape=jax.ShapeDtypeStruct((tokens, NA_W), F32),
        compiler_params=_cp("parallel", "parallel"),
        name="na_attn",
    )(q, k, v, kx, vx, bias)


def _swa_kernel(sink_ref, q_ref, k_ref, v_ref, kx_ref, vx_ref, o_ref, *, seq):
    scale = HEAD_DIM ** -0.5
    blk = SWA_WINDOW
    m0, m1 = _half_masks()
    masks = (m0, m1)
    kx = _bf(kx_ref[...])
    vx = _bf(vx_ref[...])
    group = SWA_HEADS // SWA_KV_HEADS

    sk = []
    for g in range(SWA_KV_HEADS):
        sk.append(jnp.concatenate([jnp.full((blk, 1), sink_ref[h], F32) for h in range(g * group, (g + 1) * group)],
                                  axis=0))

    def body(it, carry):
        us = range(SWA_UNROLL)
        cs = [(u, g) for u in us for g in range(SWA_KV_HEADS)]
        nb = [it * SWA_UNROLL + u for u in us]
        ks = [pl.multiple_of(jnp.clip((nb[u] - 1) * blk, 0, seq - 3 * blk), blk) for u in us]
        q0 = [pl.multiple_of(nb[u] * blk, blk) for u in us]
        kw = [_bf(k_ref[pl.ds(ks[u], 3 * blk), :]) for u in us]
        vw = [_bf(v_ref[pl.ds(ks[u], 3 * blk), :]) for u in us]
        ok = []
        for u in us:
            qpos = q0[u] + lax.broadcasted_iota(jnp.int32, (group * blk, 1), 0) % blk
            kpos = ks[u] + lax.broadcasted_iota(jnp.int32, (1, 3 * blk), 1)
            ok.append(jnp.abs(qpos - kpos) <= SWA_WINDOW)
        qg = {}
        for u in us:
            pairs = [q_ref[pl.ds(q0[u], blk), p * 128:(p + 1) * 128].astype(F32) * scale
                     for p in range(SWA_HEADS // 2)]
            for g in range(SWA_KV_HEADS):
                qs = []
                for h in range(g * group, (g + 1) * group):
                    qh = pairs[h // 2] if h % 2 == g else _swap_halves(pairs[h // 2])
                    qs.append(jnp.where(masks[g], qh, 0.0))
                qg[u, g] = _bf(jnp.concatenate(qs, axis=0))
        sw = {c: jnp.where(ok[c[0]], _dot_nt(qg[c], kw[c[0]]), NEG_INF) for c in cs}
        sx = {c: _dot_nt(qg[c], kx) for c in cs}
        m = {c: jnp.maximum(jnp.maximum(jnp.max(sw[c], axis=-1, keepdims=True),
                                        jnp.max(sx[c], axis=-1, keepdims=True)), sk[c[1]]) for c in cs}
        ew = {c: jnp.exp(sw[c] - m[c]) for c in cs}
        ex = {c: jnp.exp(sx[c] - m[c]) for c in cs}
        l = {c: jnp.sum(ew[c], axis=-1, keepdims=True) + jnp.sum(ex[c], axis=-1, keepdims=True)
             + jnp.exp(sk[c[1]] - m[c]) for c in cs}
        o = {c: (_dot(_bf(ew[c]), vw[c[0]]) + _dot(_bf(ex[c]), vx)) / l[c] for c in cs}
        for u in us:
            head_out = []
            for g in range(SWA_KV_HEADS):
                for i in range(group):
                    h = g * group + i
                    oh = o[u, g][i * blk:(i + 1) * blk]
                    head_out.append(oh if h % 2 == g else _swap_halves(oh))
            for p in range(SWA_HEADS // 2):
                o_ref[pl.ds(q0[u], blk), p * 128:(p + 1) * 128] = jnp.where(m0, head_out[2 * p], head_out[2 * p + 1])
        return carry

    lax.fori_loop(0, seq // (blk * SWA_UNROLL), body, 0)


def _swa_attn(sink, q, k, v, kx, vx, seq, past):
    tokens = q.shape[0]
    blk = lambda w: pl.BlockSpec((seq, w), lambda b: (b, 0))
    cblk = pl.BlockSpec((past, SWA_KV_W), lambda b: (b, 0))
    return pl.pallas_call(
        functools.partial(_swa_kernel, seq=seq),
        grid=(tokens // seq,),
        in_specs=[pl.BlockSpec(memory_space=pltpu.SMEM), blk(SWA_W), blk(SWA_KV_W), blk(SWA_KV_W), cblk, cblk],
        out_specs=blk(SWA_W),
        out_shape=jax.ShapeDtypeStruct((tokens, SWA_W), F32),
        compiler_params=_cp("parallel"),
        name="swa_attn",
    )(sink, q, k, v, kx, vx)


def _rw_pre_kernel(u_ref, up_ref, un_ref, mu_ref, w0_ref, w2_ref, a0_ref, a2_ref, g2_ref, kk_ref_, ka_ref_,
                   rk_ref, ones_ref, r_o, kk_o, v_o, ld_o, kka_o, kd_o, g_o, bonus_o, *, tiles_per_req):
    i = pl.program_id(0)
    u = u_ref[...]
    tm = u.shape[0]
    rowi = lax.broadcasted_iota(jnp.int32, (tm, 1), 0)
    first = (i % tiles_per_req) == 0
    last = (i % tiles_per_req) == tiles_per_req - 1
    prev_row = jnp.where(first, 0.0, up_ref[7:8, :])
    next_row = jnp.where(last, 0.0, un_ref[0:1, :])
    prev = jnp.where(rowi == 0, prev_row, pltpu.roll(u, 1, 0))
    nxt = jnp.where(rowi == tm - 1, next_row, pltpu.roll(u, tm - 1, 0))
    us = u + mu_ref[0:1, :] * (prev - u) + mu_ref[1:2, :] * (nxt - u)
    r = us[:, 0:RW_W]
    k = us[:, RW_W:2 * RW_W]
    v = us[:, 2 * RW_W:3 * RW_W]
    wl = us[:, 3 * RW_W:3 * RW_W + 128]
    al = us[:, 3 * RW_W + 128:3 * RW_W + 256]
    gl = us[:, 3 * RW_W + 256:3 * RW_W + 384]
    z = -(w0_ref[...] + _dot(_bf(jnp.tanh(wl)), w2_ref[...]))
    softplus = jnp.maximum(z, 0.0) + jnp.log(1.0 + jnp.exp(-jnp.abs(z)))
    w = -softplus - 0.5
    ld = -jnp.exp(w)
    a = _sigmoid(a0_ref[...] + _dot(_bf(al), a2_ref[...]))
    g = _dot(_bf(_sigmoid(gl)), g2_ref[...])
    ones = ones_ref[...]
    kkr = k * kk_ref_[...]
    kk = kkr * lax.rsqrt(jnp.maximum(_dot2(kkr * kkr, ones), 1e-24))
    k_a = ka_ref_[...]
    kd_f = k * (1.0 + (a[:, 0:RW_W] - 1.0) * k_a)
    kd_b = k * (1.0 + (a[:, RW_W:] - 1.0) * k_a)
    r_o[...] = r
    kk_o[...] = kk
    v_o[...] = v
    ld_o[...] = ld
    kka_o[:, 0:RW_W] = kk * a[:, 0:RW_W]
    kka_o[:, RW_W:] = kk * a[:, RW_W:]
    kd_o[:, 0:RW_W] = kd_f
    kd_o[:, RW_W:] = kd_b
    g_o[...] = g
    bonus_o[...] = _dot2(r * (kd_f + kd_b) * rk_ref[...], ones) * v


def _rw_pre(urw, p, seq):
    tokens = urw.shape[0]
    tm = _row_tile(seq)
    tpr = seq // tm
    nt = tokens // tm
    r8 = tm // 8
    row = lambda w: pl.BlockSpec((tm, w), lambda i: (i, 0))
    in_specs = [row(RW_IN_W),
                pl.BlockSpec((8, RW_IN_W), lambda i: (jnp.maximum(i * r8 - 1, 0), 0)),
                pl.BlockSpec((8, RW_IN_W), lambda i: (jnp.minimum((i + 1) * r8, nt * r8 - 1), 0)),
                _full((2, RW_IN_W)), _full((1, 2 * RW_W)), _full((128, 2 * RW_W)), _full((1, 2 * RW_W)),
                _full((128, 2 * RW_W)), _full((128, RW_W)), _full((1, RW_W)), _full((1, RW_W)),
                _full((1, RW_W)), _full((RW_W, RW_W))]
    widths = [RW_W, RW_W, RW_W, 2 * RW_W, 2 * RW_W, 2 * RW_W, RW_W, RW_W]
    return pl.pallas_call(
        functools.partial(_rw_pre_kernel, tiles_per_req=tpr),
        grid=(nt,),
        in_specs=in_specs,
        out_specs=[row(w) for w in widths],
        out_shape=[jax.ShapeDtypeStruct((tokens, w), F32) for w in widths],
        compiler_params=_cp("parallel"),
        name="rw_pre",
    )(urw, urw, urw, p['mu'], p['w0'], p['w2'], p['a0'], p['a2'], p['g2'], p['k_k'], p['k_a'], p['r_k'],
      p['ones256'])


def _rw_masks():
    t = RW_CHUNK
    n = RW_HEADS * t
    tt = np.arange(t)[:, None]
    ss = (np.arange(n) % t)[None, :]
    before = np.stack([ss < tt, ss > tt])
    diag = (ss == tt)
    strict = before.astype(np.float32)
    incl = (before | diag[None]).astype(np.float32)
    eye = diag.astype(np.float32)
    ti = np.arange(t)
    tri = np.stack([ti[None, :] <= ti[:, None], ti[None, :] >= ti[:, None]]).astype(np.float32)
    hd = np.arange(n) // t
    same = (hd[:, None] == hd[None, :]).astype(np.float32)
    return (jnp.asarray(strict), jnp.asarray(incl), jnp.asarray(tri, dtype=BF16), jnp.asarray(same),
            jnp.asarray(eye))


def _rw_scan_kernel(r_ref, kk_ref, v_ref, ld_ref, kka_ref, kd_ref, s0_ref, strict_ref, incl_ref, tri_ref,
                    same_ref, eye_ref, o_ref, sfin_ref, s_scr, *, nsub):
    d = pl.program_id(1)
    c = pl.program_id(2)
    t = RW_CHUNK
    n = RW_HEADS * t

    @pl.when(c == 0)
    def _():
        s_scr[...] = s0_ref[0, 0]

    strict = strict_ref[0]
    incl = incl_ref[0]
    tri = tri_ref[0]
    eye = eye_ref[...]
    same = same_ref[...]
    same_bf = _bf(same)

    def bd(x):
        return jnp.concatenate([_bf(x)] * RW_HEADS, axis=0) * same_bf

    js = range(nsub)
    rows = [pl.ds(pl.multiple_of((j + d * (nsub - 1 - 2 * j)) * t, t), t) for j in js]
    ld = [ld_ref[rows[j], :] for j in js]
    cum = []
    for j in js:
        lhi, lmid, llo = _split3(ld[j])
        cum.append(_dot(tri, lhi) + _dot(tri, lmid) + _dot(tri, llo))
    cend = [jnp.sum(ld[j], axis=0, keepdims=True) for j in js]
    kka = [kka_ref[rows[j], :] for j in js]
    kd = [kd_ref[rows[j], :] for j in js]
    v = [v_ref[rows[j], :] for j in js]
    at = [-kk_ref[rows[j], :] * jnp.exp(cum[j] - ld[j]) for j in js]
    rt = [r_ref[rows[j], :] * jnp.exp(cum[j]) for j in js]
    e_inv = [jnp.exp(-cum[j]) for j in js]
    aa = [_dot_nt(_bf(jnp.concatenate([at[j], rt[j]], axis=0)),
                  jnp.concatenate([bd(kka[j] * e_inv[j]), bd(kd[j] * e_inv[j])], axis=0)) for j in js]
    a_ab = [aa[j][0:t, 0:n] * strict for j in js]
    x = [eye + a_ab[j] for j in js]
    pw = a_ab
    for _ in range(RW_CHUNK.bit_length() - 3):
        pw = [_dot(_bf(pw[j]), bd(pw[j])) for j in js]
        x = [x[j] + _dot(_bf(x[j]), bd(pw[j])) for j in js]
    xs = [_split2(x[j]) for j in js]
    sa = [_split2(a_ab[j]) for j in js]
    ax = [_dot(sa[j][0], bd(xs[j][0])) + _dot(sa[j][1], bd(xs[j][0])) + _dot(sa[j][0], bd(xs[j][1])) for j in js]
    x = [x[j] + _dot(xs[j][0], bd(eye - x[j] + ax[j])) for j in js]
    v_bd = [bd(v[j]) for j in js]
    wv = [_dot(_bf(aa[j][0:t, n:] * strict), v_bd[j]) for j in js]
    mu = [_dot(_bf(x[j]), jnp.concatenate([bd(at[j]), bd(wv[j])], axis=1)) for j in js]
    m1 = [mu[j][:, 0:n] for j in js]
    u0 = [mu[j][:, n:] for j in js]
    e_end = [jnp.exp(cend[j] - cum[j]) for j in js]
    bend = [_bf(kka[j] * e_end[j]) for j in js]
    g = [_bf(_dot_tn(_bf(m1[j]), bend[j]) * same) for j in js]
    cst = [_dot_tn(_bf(jnp.concatenate([u0[j], v[j]], axis=0)),
                   jnp.concatenate([bend[j], _bf(kd[j] * e_end[j])], axis=0)) * same for j in js]
    qo = [_dot(_bf(aa[j][t:, 0:n] * incl), jnp.concatenate([bd(m1[j]), bd(u0[j])], axis=1)) for j in js]
    q = [_bf(rt[j] + qo[j][:, 0:n]) for j in js]
    o0 = [qo[j][:, n:] + _dot(_bf(aa[j][t:, n:] * incl), v_bd[j]) for j in js]

    s = s_scr[...]
    for j in js:
        s_bf = _bf(s)
        o_ref[0, rows[j], :] = _dot_nt(q[j], s_bf) + o0[j]
        s = s * jnp.exp(cend[j]) + _dot(s_bf, g[j]) + cst[j]
    s_scr[...] = s

    @pl.when(c == pl.num_programs(2) - 1)
    def _():
        sfin_ref[0, 0] = s


def _rw_scan(r, kk, v, ld, kka, kd, s0_bd, consts, seq):
    tokens = r.shape[0]
    nreq = tokens // seq
    tb = min(seq, 512)
    nblk = seq // tb
    nsub = tb // RW_CHUNK
    n = RW_HEADS * RW_CHUNK
    cc = lambda d, c: c + d * (nblk - 1 - 2 * c)
    shared = pl.BlockSpec((tb, RW_W), lambda b, d, c: (b * nblk + cc(d, c), 0))
    dirw = pl.BlockSpec((tb, RW_W), lambda b, d, c: (b * nblk + cc(d, c), d))
    strict, incl, tri, same, eye = consts
    return pl.pallas_call(
        functools.partial(_rw_scan_kernel, nsub=nsub),
        grid=(nreq, 2, nblk),
        in_specs=[shared, shared, shared, dirw, dirw, dirw,
                  pl.BlockSpec((1, 1, n, n), lambda b, d, c: (b, d, 0, 0)),
                  pl.BlockSpec((1, RW_CHUNK, n), lambda b, d, c: (d, 0, 0)),
                  pl.BlockSpec((1, RW_CHUNK, n), lambda b, d, c: (d, 0, 0)),
                  pl.BlockSpec((1, RW_CHUNK, RW_CHUNK), lambda b, d, c: (d, 0, 0)),
                  _full((n, n)), _full((RW_CHUNK, n))],
        out_specs=[pl.BlockSpec((1, tb, RW_W), lambda b, d, c: (d, b * nblk + cc(d, c), 0)),
                   pl.BlockSpec((1, 1, n, n), lambda b, d, c: (b, d, 0, 0))],
        out_shape=[jax.ShapeDtypeStruct((2, tokens, RW_W), F32), jax.ShapeDtypeStruct((nreq, 2, n, n), F32)],
        scratch_shapes=[pltpu.VMEM((n, n), F32)],
        compiler_params=_cp("parallel", "parallel", "arbitrary"),
        name="rw_scan",
    )(r, kk, v, ld, kka, kd, s0_bd, strict, incl, tri, same, eye)


def _state_to_blockdiag(s):
    b = s.shape[0]
    eye = jnp.eye(RW_HEADS, dtype=s.dtype)
    out = s[:, :, :, :, None, :] * eye[None, None, :, None, :, None]
    return out.reshape(b, 2, RW_W, RW_W)


def _blockdiag_to_state(sbd):
    b = sbd.shape[0]
    s = sbd.reshape(b, 2, RW_HEADS, HEAD_DIM, RW_HEADS, HEAD_DIM)
    idx = jnp.arange(RW_HEADS)
    return jnp.transpose(s[:, :, idx, :, idx, :], (1, 2, 0, 3, 4))


def _finish_kernel(x_ref, oa_ref, oc_ref, o2_ref, bonus_ref, g_ref, mod_ref, wout_ref, lng_ref, lnb_ref,
                   n2_ref, wr_hi_ref, wr_lo_ref, ones_ref, x1_ref, h2_ref, aff_ref):
    ones = ones_ref[...]
    y = o2_ref[0] + o2_ref[1]
    mu = _dot2(y, ones) * (1.0 / HEAD_DIM)
    yc = y - mu
    var = _dot2(yc * yc, ones) * (1.0 / HEAD_DIM)
    yn = yc * lax.rsqrt(var + GN_EPS) * lng_ref[...] + lnb_ref[...]
    ob = (yn + bonus_ref[...]) * g_ref[...]
    mixin = jnp.concatenate([_bf(oa_ref[...]), _bf(ob), _bf(oc_ref[...])], axis=1)
    mix = _dot(mixin, wout_ref[...])
    g1 = mod_ref[0, 2:3, :]
    sh2 = mod_ref[0, 3:4, :]
    sc2 = mod_ref[0, 4:5, :]
    x1 = x_ref[...] + g1 * mix
    ms = jnp.mean(x1 * x1, axis=-1, keepdims=True)
    h2 = x1 * lax.rsqrt(ms + NORM_EPS) * n2_ref[...] * (1.0 + sc2) + sh2
    x1_ref[...] = x1
    h2_ref[...] = _bf(h2)
    hhi, hlo = _split2(h2)
    logits = _dot(hhi, wr_hi_ref[...]) + _dot(hlo, wr_hi_ref[...]) + _dot(hhi, wr_lo_ref[...])
    m = jnp.max(logits, axis=-1, keepdims=True)
    e = jnp.exp(logits - m)
    aff_ref[...] = e / jnp.sum(e, axis=-1, keepdims=True)


def _finish(x, oa, oc, o2, bonus, g, mod, p, seq, per_request_mod):
    tokens, d = x.shape
    tm = _row_tile(seq)
    tpr = seq // tm
    mod_map = (lambda i: (i // tpr, 0, 0)) if per_request_mod else (lambda i: (0, 0, 0))
    row = lambda w: pl.BlockSpec((tm, w), lambda i: (i, 0))
    return pl.pallas_call(
        _finish_kernel,
        grid=(tokens // tm,),
        in_specs=[row(d), row(NA_W), row(SWA_W), pl.BlockSpec((2, tm, RW_W), lambda i: (0, i, 0)), row(RW_W),
                  row(RW_W), pl.BlockSpec((1, 6, d), mod_map), _full((d, d)), _full((1, RW_W)),
                  _full((1, RW_W)), _full((1, d)), _full((d, N_EXPERTS)), _full((d, N_EXPERTS)),
                  _full((RW_W, RW_W))],
        out_specs=[row(d), row(d), row(N_EXPERTS)],
        out_shape=[jax.ShapeDtypeStruct((tokens, d), F32), jax.ShapeDtypeStruct((tokens, d), BF16),
                   jax.ShapeDtypeStruct((tokens, N_EXPERTS), F32)],
        compiler_params=_cp("parallel"),
        name="finish",
    )(x, oa, oc, o2, bonus, g, mod, p['w_out'], p['ln_g'], p['ln_b'], p['n2'], p['wr_hi'], p['wr_lo'],
      p['ones256'])


def _topk_kernel(aff_ref, tri_ref, eye_ref, place_ref, slot_ref, slotrow_ref, gfull_ref, ends_ref, *, cap, group,
                 tb):
    b = pl.program_id(0)
    aff = aff_ref[...]
    seq = aff.shape[0]
    bits = lax.bitcast_convert_type(aff, jnp.int32)
    capf = jnp.float32(cap)

    def bis(_, carry):
        lo, hi = carry
        mid = lo + ((hi - lo + 1) >> 1)
        cnt = jnp.sum(jnp.where(bits >= mid, 1.0, 0.0), axis=0, keepdims=True)
        ge = cnt >= capf
        return jnp.where(ge, mid, lo), jnp.where(ge, hi, mid - 1)

    lo0 = jnp.zeros((1, N_EXPERTS), jnp.int32)
    hi0 = jnp.full((1, N_EXPERTS), 0x7F7FFFFF, jnp.int32)
    thr, _ = lax.fori_loop(0, 31, bis, (lo0, hi0))
    gt = jnp.where(bits > thr, 1.0, 0.0)
    eq = jnp.where(bits == thr, 1.0, 0.0)
    need = capf - jnp.sum(gt, axis=0, keepdims=True)
    offset = ((b % group) * cap).astype(F32)
    tri = tri_ref[...]
    eye = eye_ref[...]
    carry_g = jnp.zeros((1, N_EXPERTS), F32)
    carry_e = jnp.zeros((1, N_EXPERTS), F32)
    ghi, gmid, glo = _split3(aff)
    for blk in range(seq // tb):
        sl = slice(blk * tb, (blk + 1) * tb)
        pg = _dot(tri, _bf(gt[sl])) + carry_g
        pe = _dot(tri, _bf(eq[sl])) + carry_e
        carry_g = pg[tb - 1:tb, :]
        carry_e = pe[tb - 1:tb, :]
        sel = gt[sl] + eq[sl] * jnp.where(pe <= need, 1.0, 0.0)
        slot = jnp.where(sel > 0.5, pg + jnp.minimum(pe, need) - 1.0 + offset, -1.0)
        slot_ref[sl, :] = slot
        ends_ref[0, blk:blk + 1, :] = carry_g + jnp.minimum(carry_e, need) + offset
        shi, slo = _split2(slot)
        slotrow_ref[0, :, 0, sl] = _dot_nt(eye, shi) + _dot_nt(eye, slo)
        gfull_ref[sl, :] = _bf(_dot(ghi[sl], place_ref[0]) + _dot(gmid[sl], place_ref[1])
                               + _dot(glo[sl], place_ref[2]))


def _topk(aff, seq, group):
    tokens = aff.shape[0]
    nreq = tokens // seq
    cap = EC_CAPACITY * seq // N_EXPERTS
    tb = min(seq, 512)
    ti = np.arange(tb)
    tri = jnp.asarray((ti[None, :] <= ti[:, None]).astype(np.float32), dtype=BF16)
    eye = jnp.asarray(np.eye(N_EXPERTS, dtype=np.float32), dtype=BF16)
    place = np.zeros((3, N_EXPERTS, 128), np.float32)
    for s in range(3):
        place[s, np.arange(N_EXPERTS), s * N_EXPERTS + np.arange(N_EXPERTS)] = 1.0
    place = jnp.asarray(place, dtype=BF16)
    nblk = seq // tb
    slot, slotrow, gfull, ends = pl.pallas_call(
        functools.partial(_topk_kernel, cap=cap, group=group, tb=tb),
        grid=(nreq,),
        in_specs=[pl.BlockSpec((seq, N_EXPERTS), lambda b: (b, 0)), _full((tb, tb)),
                  _full((N_EXPERTS, N_EXPERTS)), _full((3, N_EXPERTS, 128))],
        out_specs=[pl.BlockSpec((seq, N_EXPERTS), lambda b: (b, 0)),
                   pl.BlockSpec((1, N_EXPERTS, 1, seq), lambda b: (b // group, 0, 0, b % group)),
                   pl.BlockSpec((seq, 128), lambda b: (b, 0)),
                   pl.BlockSpec((1, nblk, N_EXPERTS), lambda b: (b, 0, 0))],
        out_shape=[jax.ShapeDtypeStruct((tokens, N_EXPERTS), F32),
                   jax.ShapeDtypeStruct((nreq // group, N_EXPERTS, 1, group * seq), F32),
                   jax.ShapeDtypeStruct((tokens, 128), BF16),
                   jax.ShapeDtypeStruct((nreq, nblk, N_EXPERTS), F32)],
        compiler_params=_cp("parallel"),
        name="topk",
    )(aff, tri, eye, place)
    ends = ends.reshape(nreq // group, group * nblk, N_EXPERTS).transpose(0, 2, 1)
    return slot, slotrow, gfull, ends.astype(jnp.int32).reshape(-1), tb


MOE_EB = 8


def _moe_dispatch_kernel(ends_ref, h_ref, slotrow_ref, gfull_ref, xe_ref, gs_ref, *, ct, nch):
    gi = pl.program_id(0)
    eb = pl.program_id(1)
    c = pl.program_id(2)
    mt = 128

    @pl.when(c == 0)
    def _():
        xe_ref[...] = jnp.zeros_like(xe_ref)
        gs_ref[...] = jnp.zeros_like(gs_ref)

    jcol = lax.broadcasted_iota(jnp.int32, (mt, 1), 0)
    starts, his, pieces = [], [], []
    for i in range(MOE_EB):
        base = (gi * N_EXPERTS + eb * MOE_EB + i) * nch
        lo = jnp.where(c == 0, 0, ends_ref[base + jnp.maximum(c - 1, 0)])
        his.append(ends_ref[base + c])
        start = pl.multiple_of(jnp.minimum((lo // 16) * 16, ct - mt), 16)
        starts.append(start)
        pieces.append(_bf(jnp.where(slotrow_ref[0, i] == (jcol + start).astype(F32), 1.0, 0.0)))
    onehot = jnp.concatenate(pieces, axis=0)
    xw = _bf(_dot(onehot, h_ref[...]))
    gw = _bf(_dot(onehot, gfull_ref[...]))
    for i in range(MOE_EB):
        rows = pl.ds(starts[i], mt)
        xe_ref[0, i, rows, :] += xw[i * mt:(i + 1) * mt]
        gs_ref[0, i, rows, :] += gw[i * mt:(i + 1) * mt]
    for i in range(MOE_EB):
        for w in range(1, ct // mt):
            wlo = starts[i] + w * mt

            @pl.when(wlo < his[i])
            def _(i=i, wlo=wlo):
                ws = pl.multiple_of(jnp.minimum(wlo, ct - mt), 16)
                slot = slotrow_ref[0, i]
                hit = (jnp.where(slot == (jcol + ws).astype(F32), 1.0, 0.0)
                       * jnp.where(slot >= wlo.astype(F32), 1.0, 0.0))
                rows = pl.ds(ws, mt)
                xe_ref[0, i, rows, :] += _bf(_dot(_bf(hit), h_ref[...]))
                gs_ref[0, i, rows, :] += _bf(_dot(_bf(hit), gfull_ref[...]))


def _moe_dispatch(ends, h2, slotrow, gfull, lg, ct, kc):
    tokens, d = h2.shape
    ngrp = tokens // lg
    nch = lg // kc
    grid_spec = pltpu.PrefetchScalarGridSpec(
        num_scalar_prefetch=1,
        grid=(ngrp, N_EXPERTS // MOE_EB, nch),
        in_specs=[pl.BlockSpec((kc, d), lambda gi, eb, c, ends: (gi * nch + c, 0)),
                  pl.BlockSpec((1, MOE_EB, 1, kc), lambda gi, eb, c, ends: (gi, eb, 0, c)),
                  pl.BlockSpec((kc, 128), lambda gi, eb, c, ends: (gi * nch + c, 0))],
        out_specs=[pl.BlockSpec((1, MOE_EB, ct, d), lambda gi, eb, c, ends: (gi, eb, 0, 0)),
                   pl.BlockSpec((1, MOE_EB, ct, 128), lambda gi, eb, c, ends: (gi, eb, 0, 0))])
    return pl.pallas_call(
        functools.partial(_moe_dispatch_kernel, ct=ct, nch=nch),
        grid_spec=grid_spec,
        out_shape=[jax.ShapeDtypeStruct((ngrp, N_EXPERTS, ct, d), BF16),
                   jax.ShapeDtypeStruct((ngrp, N_EXPERTS, ct, 128), BF16)],
        compiler_params=_cp("parallel", "parallel", "arbitrary"),
        name="moe_dispatch",
    )(ends, h2, slotrow, gfull)


def _moe_ffn_kernel(xe_ref, gs_ref, mod_ref, wg_ref, wu_ref, wd_ref, ye_ref):
    e = pl.program_id(1)
    lane = lax.broadcasted_iota(jnp.int32, (1, 128), 1)
    pick = (lane == e) | (lane == e + N_EXPERTS) | (lane == e + 2 * N_EXPERTS)
    gate = jnp.sum(jnp.where(pick, gs_ref[0, 0].astype(F32), 0.0), axis=-1, keepdims=True)
    xb = xe_ref[0, 0]
    hg = _dot(xb, wg_ref[0])
    hu = _dot(xb, wu_ref[0])
    he = _bf(hg * _sigmoid(hg) * hu)
    y = _dot(he, wd_ref[0])
    ye_ref[0, 0] = _bf(y * gate * mod_ref[0, 5:6, :])


def _moe_ffn(xe, gs, mod, wg, wu, wd, per_group_mod):
    ngrp, _, ct, d = xe.shape
    f = wg.shape[2]
    mod_map = (lambda gi, e: (gi, 0, 0)) if per_group_mod else (lambda gi, e: (0, 0, 0))
    return pl.pallas_call(
        _moe_ffn_kernel,
        grid=(ngrp, N_EXPERTS),
        in_specs=[pl.BlockSpec((1, 1, ct, d), lambda gi, e: (gi, e, 0, 0)),
                  pl.BlockSpec((1, 1, ct, 128), lambda gi, e: (gi, e, 0, 0)),
                  pl.BlockSpec((1, 6, d), mod_map),
                  pl.BlockSpec((1, d, f), lambda gi, e: (e, 0, 0)),
                  pl.BlockSpec((1, d, f), lambda gi, e: (e, 0, 0)),
                  pl.BlockSpec((1, f, d), lambda gi, e: (e, 0, 0))],
        out_specs=pl.BlockSpec((1, 1, ct, d), lambda gi, e: (gi, e, 0, 0)),
        out_shape=jax.ShapeDtypeStruct((ngrp, N_EXPERTS, ct, d), BF16),
        compiler_params=_cp("parallel", "parallel"),
        name="moe_ffn",
    )(xe, gs, mod, wg, wu, wd)


def _moe_combine_kernel(ends_ref, x1_ref, slot_ref, ye_ref, o_ref, win_scr, *, ct, nch):
    gi = pl.program_id(0)
    j = pl.program_id(1)
    mt = 128
    wide = N_EXPERTS * mt
    shi, slo = _split2(slot_ref[...])
    col_e = lax.broadcasted_iota(jnp.int32, (N_EXPERTS, wide), 1) // mt
    row_e = lax.broadcasted_iota(jnp.int32, (N_EXPERTS, wide), 0)
    expand = _bf(jnp.where(col_e == row_e, 1.0, 0.0))
    sb = _dot(shi, expand) + _dot(slo, expand)
    lane = lax.broadcasted_iota(jnp.int32, (1, mt), 1)
    starts, his, targets = [], [], []
    for e in range(N_EXPERTS):
        base = (gi * N_EXPERTS + e) * nch
        lo = jnp.where(j == 0, 0, ends_ref[base + jnp.maximum(j - 1, 0)])
        his.append(ends_ref[base + j])
        start = pl.multiple_of(jnp.minimum((lo // 16) * 16, ct - mt), 16)
        win_scr[e * mt:(e + 1) * mt, :] = ye_ref[0, e, pl.ds(start, mt), :]
        starts.append(start)
        targets.append((lane + start).astype(F32))
    onehot = _bf(jnp.where(sb == jnp.concatenate(targets, axis=1), 1.0, 0.0))
    o_ref[...] = x1_ref[...] + _dot(onehot, win_scr[...])
    for e in range(N_EXPERTS):
        for w in range(1, ct // mt):
            wlo = starts[e] + w * mt

            @pl.when(wlo < his[e])
            def _(e=e, wlo=wlo):
                ws = pl.multiple_of(jnp.minimum(wlo, ct - mt), 16)
                sbe = sb[:, e * mt:(e + 1) * mt]
                hit = jnp.where(sbe == (lane + ws).astype(F32), 1.0, 0.0) * jnp.where(sbe >= wlo.astype(F32), 1.0, 0.0)
                o_ref[...] += _dot(_bf(hit), ye_ref[0, e, pl.ds(ws, mt), :])


def _moe_combine(ends, x1, slot, ye, lg, ct, kc):
    tokens, d = x1.shape
    ngrp = tokens // lg
    nch = lg // kc
    grid_spec = pltpu.PrefetchScalarGridSpec(
        num_scalar_prefetch=1,
        grid=(ngrp, nch),
        in_specs=[pl.BlockSpec((kc, d), lambda gi, j, ends: (gi * nch + j, 0)),
                  pl.BlockSpec((kc, N_EXPERTS), lambda gi, j, ends: (gi * nch + j, 0)),
                  pl.BlockSpec((1, N_EXPERTS, ct, d), lambda gi, j, ends: (gi, 0, 0, 0),
                               pipeline_mode=pl.Buffered(1))],
        out_specs=pl.BlockSpec((kc, d), lambda gi, j, ends: (gi * nch + j, 0)),
        scratch_shapes=[pltpu.VMEM((N_EXPERTS * 128, d), BF16)])
    return pl.pallas_call(
        functools.partial(_moe_combine_kernel, ct=ct, nch=nch),
        grid_spec=grid_spec,
        out_shape=jax.ShapeDtypeStruct((tokens, d), F32),
        compiler_params=_cp("parallel", "arbitrary"),
        name="moe_combine",
    )(ends, x1, slot, ye)


def _rope_tables(seq):
    t = np.arange(seq)
    n_freq = HEAD_DIM // 4
    inv = ROPE_THETA ** (-np.arange(n_freq, dtype=np.float32) / n_freq)
    ang = np.concatenate([(t // GRID_W).astype(np.float32)[:, None] * inv,
                          (t % GRID_W).astype(np.float32)[:, None] * inv], axis=-1)
    ang = jnp.asarray(ang, dtype=F32)
    cos, sin = jnp.cos(ang), jnp.sin(ang)
    cos_t = jnp.tile(jnp.concatenate([cos, cos], axis=-1), (1, SWA_HEADS))
    sin_t = jnp.tile(jnp.concatenate([-sin, sin], axis=-1), (1, SWA_HEADS))
    return cos_t, sin_t


def _blockdiag2(w):
    z = jnp.zeros_like(w[0])
    return jnp.concatenate([jnp.concatenate([w[0], z], axis=1), jnp.concatenate([z, w[1]], axis=1)], axis=0)


def _layer_params(l, ada_w, ada_b, norm1_g, norm2_g, w_in, na_q_norm, na_k_norm, na_rpb, rw_mu, rw_w0, rw_w2,
                  rw_a0, rw_a2, rw_g2, rw_k_k, rw_k_a, rw_r_k, rw_ln_g, rw_ln_b, swa_q_norm, swa_k_norm,
                  swa_sink, w_out, w_router, w_gate, w_up, w_down):
    wr = w_router[l]
    wr_hi = wr.astype(BF16)
    w_rw = w_in[l][:, 3 * NA_W:3 * NA_W + RW_IN_W]
    return {
        'n1': norm1_g[l][None], 'n2': norm2_g[l][None], 'w_in': w_in[l].astype(BF16),
        'w_rw_lo': (w_rw - w_rw.astype(BF16).astype(F32)).astype(BF16),
        'gains': (jnp.tile(na_q_norm[l], NA_HEADS)[None], jnp.tile(na_k_norm[l], NA_HEADS)[None],
                  jnp.tile(swa_q_norm[l], SWA_HEADS)[None], jnp.tile(swa_k_norm[l], SWA_KV_HEADS)[None]),
        'bias': _na_bias_table(na_rpb[l]),
        'mu': rw_mu[l], 'w0': rw_w0[l].reshape(1, 2 * RW_W), 'w2': _blockdiag2(rw_w2[l]).astype(BF16),
        'a0': rw_a0[l].reshape(1, 2 * RW_W), 'a2': _blockdiag2(rw_a2[l]).astype(BF16),
        'g2': rw_g2[l].astype(BF16), 'k_k': rw_k_k[l][None], 'k_a': rw_k_a[l][None],
        'r_k': rw_r_k[l].reshape(1, RW_W), 'ln_g': rw_ln_g[l][None], 'ln_b': rw_ln_b[l][None],
        'sink': swa_sink[l], 'w_out': w_out[l].astype(BF16),
        'wr_hi': wr_hi, 'wr_lo': (wr - wr_hi.astype(F32)).astype(BF16),
        'wg': w_gate[l].astype(BF16), 'wu': w_up[l].astype(BF16), 'wd': w_down[l].astype(BF16),
        'ones256': _block_ones(RW_W),
    }


def _mix_and_ffn(x, mod, p, oa, oc, urw, s0_bd, scan_consts, seq, per_request_mod, group):
    r, kk, v, ld, kka, kd, g, bonus = _rw_pre(urw, p, seq)
    o2, sfin = _rw_scan(r, kk, v, ld, kka, kd, s0_bd, scan_consts, seq)
    x1, h2, aff = _finish(x, oa, oc, o2, bonus, g, mod, p, seq, per_request_mod)
    slot, slotrow, gfull, ends, kc = _topk(aff, seq, group)
    cap = EC_CAPACITY * seq // N_EXPERTS
    lg, ct = group * seq, group * cap
    xe, gs = _moe_dispatch(ends, h2, slotrow, gfull, lg, ct, kc)
    ye = _moe_ffn(xe, gs, mod, p['wg'], p['wu'], p['wd'], per_request_mod)
    return _moe_combine(ends, x1, slot, ye, lg, ct, kc), sfin


def _context_layer(x, mod, p, ones384, scan_consts, seq):
    qa, ka, va, urw, qc, kc, vc = _proj(x, mod, p['n1'], p['w_in'], p['gains'], ones384, None, seq, False, F32,
                                        w_rw_lo=p['w_rw_lo'])
    oa, oc = _ctx_attn(p['sink'], qa, ka, va, qc, kc, vc, seq)
    nreq = x.shape[0] // seq
    s0 = jnp.zeros((nreq, 2, RW_W, RW_W), F32)
    y, sfin = _mix_and_ffn(x, mod, p, oa, oc, urw, s0, scan_consts, seq, False, 8)
    return y, ka, va, kc, vc, sfin


def _latent_layer(x, mod, p, ones384, scan_consts, rope_tabs, seq, kx_na, vx_na, kx_swa, vx_swa, s0_bd, past):
    qa, ka, va, urw, qc, kc, vc = _proj(x, mod, p['n1'], p['w_in'], p['gains'], ones384, rope_tabs, seq, True,
                                        BF16)
    oa = _na_attn(qa, ka, va, kx_na, vx_na, p['bias'], seq, past)
    oc = _swa_attn(p['sink'], qc, kc, vc, kx_swa, vx_swa, seq, past)
    y, _ = _mix_and_ffn(x, mod, p, oa, oc, urw, s0_bd, scan_consts, seq, True, 1)
    return y


def _heads_first(z, nreq, seq, heads):
    return z.reshape(nreq, seq, heads, HEAD_DIM).transpose(0, 2, 1, 3)


def _tokens_first(z):
    b, h, n, dh = z.shape
    return z.transpose(0, 2, 1, 3).reshape(b * n, h * dh)


def kernel(x_prompt, x_sample, cache_na_k, cache_na_v, cache_swa_k, cache_swa_v, state_rwkv, c, c_ctx, ada_w, ada_b, norm1_g, norm2_g, w_in, na_q_norm, na_k_norm, na_rpb, rw_mu, rw_w0, rw_w2, rw_a0, rw_a2, rw_g2, rw_k_k, rw_k_a, rw_r_k, rw_ln_g, rw_ln_b, swa_q_norm, swa_k_norm, swa_sink, w_out, w_router, w_gate, w_up, w_down):
    nb, seq, d = x_prompt.shape
    db, dseq, _ = x_sample.shape
    depth = ada_w.shape[0]
    past = cache_na_k.shape[3]
    cond = jnp.concatenate([c, c_ctx[None], jnp.zeros((16 - db - 1, d), F32)], axis=0)
    mod_all = _adaln(cond, ada_w, ada_b).reshape(depth, 16, 6, d)
    ones384 = _block_ones(NA_W)
    scan_consts = _rw_masks()
    rope_tabs = _rope_tables(dseq)
    xp = x_prompt.reshape(nb * seq, d)
    xs = x_sample.reshape(db * dseq, d)
    new_ka, new_va, new_kc, new_vc, new_s = [], [], [], [], []
    for l in range(depth):
        p = _layer_params(l, ada_w, ada_b, norm1_g, norm2_g, w_in, na_q_norm, na_k_norm, na_rpb, rw_mu, rw_w0,
                          rw_w2, rw_a0, rw_a2, rw_g2, rw_k_k, rw_k_a, rw_r_k, rw_ln_g, rw_ln_b, swa_q_norm,
                          swa_k_norm, swa_sink, w_out, w_router, w_gate, w_up, w_down)
        mod_ctx = mod_all[l, db:db + 1]
        mod_lat = mod_all[l, 0:db]
        xp, ka, va, kc, vc, sfin = _context_layer(xp, mod_ctx, p, ones384, scan_consts, seq)
        new_ka.append(_heads_first(ka, nb, seq, NA_HEADS))
        new_va.append(_heads_first(va, nb, seq, NA_HEADS))
        new_kc.append(_heads_first(kc, nb, seq, SWA_KV_HEADS))
        new_vc.append(_heads_first(vc, nb, seq, SWA_KV_HEADS))
        new_s.append(_blockdiag_to_state(sfin))
        xs = _latent_layer(xs, mod_lat, p, ones384, scan_consts, rope_tabs, dseq,
                           _tokens_first(cache_na_k[:, l]), _tokens_first(cache_na_v[:, l]),
                           _tokens_first(cache_swa_k[:, l]), _tokens_first(cache_swa_v[:, l]),
                           _state_to_blockdiag(state_rwkv[:, l]), past)
    return (xp.reshape(nb, seq, d), xs.reshape(db, dseq, d), jnp.stack(new_ka, axis=1),
            jnp.stack(new_va, axis=1), jnp.stack(new_kc, axis=1), jnp.stack(new_vc, axis=1),
            jnp.stack(new_s, axis=1))
```

```python
import functools

import numpy as np
import jax
import jax.numpy as jnp
from jax import lax
from jax.experimental import pallas as pl
from jax.experimental.pallas import tpu as pltpu

F32 = jnp.float32
BF16 = jnp.bfloat16

HEAD_DIM = 64
GRID_W = 64
NA_HEADS = 6
NA_KH = 8
NA_KW = 16
RW_HEADS = 4
SWA_HEADS = 6
SWA_KV_HEADS = 2
SWA_WINDOW = 128
N_EXPERTS = 16
EC_CAPACITY = 2
ROPE_THETA = 10000.0
NORM_EPS = 1e-6
GN_EPS = 64e-5
NEG_INF = -1e30
SUB_ROWS = 256
NA_UNROLL = 4
SWA_UNROLL = 2
RW_CHUNK = 64
RW_W = RW_HEADS * HEAD_DIM
NA_W = NA_HEADS * HEAD_DIM
SWA_W = SWA_HEADS * HEAD_DIM
SWA_KV_W = SWA_KV_HEADS * HEAD_DIM
RW_IN_W = 1152
VMEM_LIMIT = 56 * 1024 * 1024


def _cp(*sem):
    return pltpu.CompilerParams(dimension_semantics=sem, vmem_limit_bytes=VMEM_LIMIT)


def _bf(x):
    return x.astype(BF16)


def _dot(a, b):
    return jnp.dot(a, b, preferred_element_type=F32)


def _dot_nt(a, b):
    return lax.dot_general(a, b, (((1,), (1,)), ((), ())), preferred_element_type=F32)


def _dot_tn(a, b):
    return lax.dot_general(a, b, (((0,), (0,)), ((), ())), preferred_element_type=F32)


def _split2(x):
    hi = x.astype(BF16)
    lo = (x - hi.astype(F32)).astype(BF16)
    return hi, lo


def _split3(x):
    hi = x.astype(BF16)
    r1 = x - hi.astype(F32)
    mid = r1.astype(BF16)
    lo = (r1 - mid.astype(F32)).astype(BF16)
    return hi, mid, lo


def _dot2(a, b_bf):
    hi, lo = _split2(a)
    return _dot(hi, b_bf) + _dot(lo, b_bf)


def _sigmoid(x):
    return 1.0 / (1.0 + jnp.exp(-x))


def _block_ones(width):
    i = np.arange(width) // HEAD_DIM
    return jnp.asarray((i[:, None] == i[None, :]).astype(np.float32), dtype=BF16)


def _row_tile(seq):
    return 512 if seq % 512 == 0 else 256


def _full(shape):
    return pl.BlockSpec(shape, lambda *_: (0,) * len(shape))


def _adaln_kernel(c_ref, w_ref, b_ref, o_ref):
    c = c_ref[...]
    s = c * _sigmoid(c)
    shi, slo = _split2(s)
    whi, wlo = _split2(w_ref[0])
    o_ref[0] = _dot(shi, whi) + _dot(slo, whi) + _dot(shi, wlo) + b_ref[0]


def _adaln(cond, ada_w, ada_b):
    nl, d, n6 = ada_w.shape
    tn = 1536
    rows = cond.shape[0]
    return pl.pallas_call(
        _adaln_kernel,
        grid=(nl, n6 // tn),
        in_specs=[pl.BlockSpec((rows, d), lambda l, j: (0, 0)),
                  pl.BlockSpec((1, d, tn), lambda l, j: (l, 0, j)),
                  pl.BlockSpec((1, 1, tn), lambda l, j: (l, 0, j))],
        out_specs=pl.BlockSpec((1, rows, tn), lambda l, j: (l, 0, j)),
        out_shape=jax.ShapeDtypeStruct((nl, rows, n6), F32),
        compiler_params=_cp("parallel", "parallel"),
        name="adaln",
    )(cond, ada_w, ada_b.reshape(nl, 1, n6))


def _head_sums(zz, ones128):
    zz = _bf(zz)
    parts = [_dot(zz[:, i:i + 128], ones128) for i in range(0, zz.shape[1], 128)]
    return jnp.concatenate(parts, axis=1) if len(parts) > 1 else parts[0]


def _head_norm(z, gain, ones128):
    ms = _head_sums(z * z, ones128) * (1.0 / HEAD_DIM)
    return z * lax.rsqrt(ms + NORM_EPS) * gain


def _rope(z, cos, sin_signed):
    w = z.shape[1]
    lane = lax.broadcasted_iota(jnp.int32, z.shape, 1)
    first = (lane % HEAD_DIM) < (HEAD_DIM // 2)
    swapped = jnp.where(first, pltpu.roll(z, w - HEAD_DIM // 2, 1), pltpu.roll(z, HEAD_DIM // 2, 1))
    return z * cos + swapped * sin_signed


def _proj_kernel(*refs, rope, split_rw):
    refs = list(refs)
    x_ref, mod_ref, n1_ref, w_ref, gqa_ref, gka_ref, gqc_ref, gkc_ref, ones_ref = refs[:9]
    qa_ref, ka_ref, va_ref, urw_ref, qc_ref, kc_ref, vc_ref = refs[-7:]
    extra = refs[9:-7]
    if rope:
        cos_ref, sin_ref = extra[:2]
    sh1 = mod_ref[0, 0:1, :]
    sc1 = mod_ref[0, 1:2, :]
    o0 = 0
    o1 = NA_W
    o2 = 2 * NA_W
    o3 = 3 * NA_W
    o4 = o3 + RW_IN_W
    o5 = o4 + SWA_W
    o6 = o5 + SWA_KV_W
    ones = ones_ref[0:128, 0:128]
    ones_kv = ones
    subs = [slice(i, i + SUB_ROWS) for i in range(0, x_ref.shape[0], SUB_ROWS)]
    hs, us = [], []
    for sl in subs:
        x = x_ref[sl, :]
        ms = jnp.mean(x * x, axis=-1, keepdims=True)
        hs.append(x * lax.rsqrt(ms + NORM_EPS) * n1_ref[...] * (1.0 + sc1) + sh1)
    h_hi = [_bf(h) for h in hs]
    us = [_dot(hh, w_ref[...]) for hh in h_hi]
    for sl, h, hh, u in zip(subs, hs, h_hi, us):
        qa = _head_norm(u[:, o0:o1], gqa_ref[...], ones)
        ka = _head_norm(u[:, o1:o2], gka_ref[...], ones)
        qc = _head_norm(u[:, o4:o5], gqc_ref[...], ones)
        kc = _head_norm(u[:, o5:o6], gkc_ref[...], ones_kv)
        if rope:
            qc = _rope(qc, cos_ref[sl, :], sin_ref[sl, :])
            kc = _rope(kc, cos_ref[sl, 0:SWA_KV_W], sin_ref[sl, 0:SWA_KV_W])
        qa_ref[sl, :] = qa.astype(qa_ref.dtype)
        ka_ref[sl, :] = ka.astype(ka_ref.dtype)
        va_ref[sl, :] = u[:, o2:o3].astype(va_ref.dtype)
        urw = u[:, o3:o4]
        if split_rw:
            h_lo = _bf(h - hh.astype(F32))
            urw = urw + _dot(h_lo, w_ref[:, o3:o4]) + _dot(hh, extra[-1][...])
        urw_ref[sl, :] = urw
        qc_ref[sl, :] = qc.astype(qc_ref.dtype)
        kc_ref[sl, :] = kc.astype(kc_ref.dtype)
        vc_ref[sl, :] = u[:, o6:].astype(vc_ref.dtype)


def _proj(x, mod, n1, w_in_bf, gains, ones384, rope_tabs, seq, per_request_mod, qkv_dtype, w_rw_lo=None):
    tokens, d = x.shape
    tm = _row_tile(seq)
    tiles_per_req = seq // tm
    in_w = w_in_bf.shape[1]
    rope = rope_tabs is not None
    mod_map = (lambda i: (i // tiles_per_req, 0, 0)) if per_request_mod else (lambda i: (0, 0, 0))
    row = lambda w: pl.BlockSpec((tm, w), lambda i: (i, 0))
    in_specs = [row(d), pl.BlockSpec((1, 6, d), mod_map), _full((1, d)), _full((d, in_w)),
                _full((1, NA_W)), _full((1, NA_W)), _full((1, SWA_W)), _full((1, SWA_KV_W)),
                _full((NA_W, NA_W))]
    args = [x, mod, n1, w_in_bf, *gains, ones384]
    if rope:
        tab = pl.BlockSpec((tm, SWA_W), lambda i: (i % tiles_per_req, 0))
        in_specs += [tab, tab]
        args += list(rope_tabs)
    if w_rw_lo is not None:
        in_specs.append(_full((d, RW_IN_W)))
        args.append(w_rw_lo)
    widths = [NA_W, NA_W, NA_W, RW_IN_W, SWA_W, SWA_KV_W, SWA_KV_W]
    dtypes = [qkv_dtype, qkv_dtype, qkv_dtype, F32, qkv_dtype, qkv_dtype, qkv_dtype]
    return pl.pallas_call(
        functools.partial(_proj_kernel, rope=rope, split_rw=w_rw_lo is not None),
        grid=(tokens // tm,),
        in_specs=in_specs,
        out_specs=[row(w) for w in widths],
        out_shape=[jax.ShapeDtypeStruct((tokens, w), dt) for w, dt in zip(widths, dtypes)],
        compiler_params=_cp("parallel"),
        name="proj",
    )(*args)


def _half_masks(width=2 * HEAD_DIM):
    lane = lax.broadcasted_iota(jnp.int32, (1, width), 1)
    return lane < HEAD_DIM, lane >= HEAD_DIM


def _swap_halves(z):
    return pltpu.roll(z, HEAD_DIM, 1)


def _ctx_attn_kernel(sink_ref, qa_ref, ka_ref, va_ref, qc_ref, kc_ref, vc_ref, oa_ref, oc_ref):
    scale = HEAD_DIM ** -0.5
    m0, m1 = _half_masks()
    masks = (m0, m1)
    for pair in range(NA_HEADS // 2):
        sl = slice(pair * 128, (pair + 1) * 128)
        qp = qa_ref[:, sl].astype(F32) * scale
        kp = _bf(ka_ref[:, sl])
        vp = _bf(va_ref[:, sl])
        outs = []
        for half in range(2):
            qm = _bf(jnp.where(masks[half], qp, 0.0))
            s = _dot_nt(qm, kp)
            m = jnp.max(s, axis=-1, keepdims=True)
            e = jnp.exp(s - m)
            l = jnp.sum(e, axis=-1, keepdims=True)
            outs.append(_dot(_bf(e), vp) / l)
        oa_ref[:, sl] = jnp.where(m0, outs[0], outs[1])
    kc = _bf(kc_ref[...])
    vc = _bf(vc_ref[...])
    group = SWA_HEADS // SWA_KV_HEADS
    for pair in range(SWA_HEADS // 2):
        sl = slice(pair * 128, (pair + 1) * 128)
        qp = qc_ref[:, sl].astype(F32) * scale
        outs = []
        for half in range(2):
            h = 2 * pair + half
            g = h // group
            qh = qp if g == half else _swap_halves(qp)
            qm = _bf(jnp.where(masks[g], qh, 0.0))
            s = _dot_nt(qm, kc)
            sk = sink_ref[h]
            m = jnp.maximum(jnp.max(s, axis=-1, keepdims=True), sk)
            e = jnp.exp(s - m)
            l = jnp.sum(e, axis=-1, keepdims=True) + jnp.exp(sk - m)
            o = _dot(_bf(e), vc) / l
            outs.append(o if g == half else _swap_halves(o))
        oc_ref[:, sl] = jnp.where(m0, outs[0], outs[1])


def _ctx_attn(sink, qa, ka, va, qc, kc, vc, seq):
    tokens = qa.shape[0]
    blk = lambda w: pl.BlockSpec((seq, w), lambda b: (b, 0))
    return pl.pallas_call(
        _ctx_attn_kernel,
        grid=(tokens // seq,),
        in_specs=[pl.BlockSpec(memory_space=pltpu.SMEM), blk(NA_W), blk(NA_W), blk(NA_W), blk(SWA_W),
                  blk(SWA_KV_W), blk(SWA_KV_W)],
        out_specs=[blk(NA_W), blk(SWA_W)],
        out_shape=[jax.ShapeDtypeStruct((tokens, NA_W), F32), jax.ShapeDtypeStruct((tokens, SWA_W), F32)],
        compiler_params=_cp("parallel"),
        name="ctx_attn",
    )(sink, qa, ka, va, qc, kc, vc)


def _na_bias_kernel(rpb_ref, o_ref):
    h = pl.program_id(0)
    nrow = 2 * NA_KH - 1
    ncol = 2 * NA_KW - 1
    width = NA_KH * GRID_W
    shape = (GRID_W, width)
    lane = lax.broadcasted_iota(jnp.int32, shape, 1)
    qc = lax.broadcasted_iota(jnp.int32, shape, 0)
    kc = lane % GRID_W
    c_start = jnp.clip(qc - NA_KW // 2, 0, GRID_W - NA_KW)
    ok = (kc >= c_start) & (kc < c_start + NA_KW)
    d_col = jnp.clip(kc - qc, 1 - NA_KW, NA_KW - 1) + NA_KW - 1
    key_row = lax.broadcasted_iota(jnp.int32, (1, width), 1) // GRID_W

    def case_body(case, carry):
        acc = jnp.zeros(shape, F32)
        for dc in range(ncol):
            val = jnp.zeros((1, width), F32)
            for i in range(NA_KH):
                val = jnp.where(key_row == i, rpb_ref[(h * nrow + case + i) * ncol + dc], val)
            acc = jnp.where(d_col == dc, val, acc)
        o_ref[0, pl.ds(case, 1)] = jnp.where(ok, acc, NEG_INF)[None]
        return carry

    lax.fori_loop(0, NA_KH, case_body, 0)


def _na_bias_table(rpb):
    nh = rpb.shape[0]
    return pl.pallas_call(
        _na_bias_kernel,
        grid=(nh,),
        in_specs=[pl.BlockSpec(memory_space=pltpu.SMEM)],
        out_specs=pl.BlockSpec((1, NA_KH, GRID_W, NA_KH * GRID_W), lambda h: (h // 2, 0, h % 2, 0)),
        out_shape=jax.ShapeDtypeStruct((nh // 2, NA_KH, 2 * GRID_W, NA_KH * GRID_W), F32),
        compiler_params=_cp("parallel"),
        name="na_bias",
    )(rpb.reshape(-1))


def _na_kernel(q_ref, k_ref, v_ref, kx_ref, vx_ref, bias_ref, o_ref, *, rows):
    scale = HEAD_DIM ** -0.5
    m0, m1 = _half_masks()
    kx = _bf(kx_ref[...])
    vx = _bf(vx_ref[...])

    def body(it, carry):
        us = range(NA_UNROLL)
        r = [it * NA_UNROLL + u for u in us]
        rs = [jnp.clip(r[u] - NA_KH // 2, 0, rows - NA_KH) for u in us]
        case = [rs[u] - r[u] + NA_KH - 1 for u in us]
        q0 = [pl.multiple_of(r[u] * GRID_W, GRID_W) for u in us]
        k0 = [pl.multiple_of(rs[u] * GRID_W, GRID_W) for u in us]
        qp = [q_ref[pl.ds(q0[u], GRID_W), :].astype(F32) * scale for u in us]
        kw = [_bf(k_ref[pl.ds(k0[u], NA_KH * GRID_W), :]) for u in us]
        vw = [_bf(v_ref[pl.ds(k0[u], NA_KH * GRID_W), :]) for u in us]
        q2 = [_bf(jnp.concatenate([jnp.where(m0, qp[u], 0.0), jnp.where(m1, qp[u], 0.0)], axis=0)) for u in us]
        sw = [_dot_nt(q2[u], kw[u]) + bias_ref[0, pl.ds(case[u], 1)][0] for u in us]
        sx = [_dot_nt(q2[u], kx) for u in us]
        m = [jnp.maximum(jnp.max(sw[u], axis=-1, keepdims=True), jnp.max(sx[u], axis=-1, keepdims=True))
             for u in us]
        ew = [jnp.exp(sw[u] - m[u]) for u in us]
        ex = [jnp.exp(sx[u] - m[u]) for u in us]
        l = [jnp.sum(ew[u], axis=-1, keepdims=True) + jnp.sum(ex[u], axis=-1, keepdims=True) for u in us]
        o = [(_dot(_bf(ew[u]), vw[u]) + _dot(_bf(ex[u]), vx)) / l[u] for u in us]
        for u in us:
            o_ref[pl.ds(q0[u], GRID_W), :] = jnp.where(m0, o[u][0:GRID_W], o[u][GRID_W:])
        return carry

    lax.fori_loop(0, rows // NA_UNROLL, body, 0)


def _na_attn(q, k, v, kx, vx, bias, seq, past):
    tokens = q.shape[0]
    nb = tokens // seq
    rows = seq // GRID_W
    blk = pl.BlockSpec((seq, 128), lambda b, p: (b, p))
    cblk = pl.BlockSpec((past, 128), lambda b, p: (b, p))
    return pl.pallas_call(
        functools.partial(_na_kernel, rows=rows),
        grid=(nb, NA_HEADS // 2),
        in_specs=[blk, blk, blk, cblk, cblk,
                  pl.BlockSpec((1, NA_KH, 2 * GRID_W, NA_KH * GRID_W), lambda b, p: (p, 0, 0, 0))],
        out_specs=blk,
        out_shape=jax.ShapeDtypeStruct((tokens, NA_W), F32),
        compiler_params=_cp("parallel", "parallel"),
        name="na_attn",
    )(q, k, v, kx, vx, bias)


def _swa_kernel(sink_ref, q_ref, k_ref, v_ref, kx_ref, vx_ref, o_ref, *, seq):
    scale = HEAD_DIM ** -0.5
    blk = SWA_WINDOW
    m0, m1 = _half_masks()
    masks = (m0, m1)
    kx = _bf(kx_ref[...])
    vx = _bf(vx_ref[...])
    group = SWA_HEADS // SWA_KV_HEADS

    sk = []
    for g in range(SWA_KV_HEADS):
        sk.append(jnp.concatenate([jnp.full((blk, 1), sink_ref[h], F32) for h in range(g * group, (g + 1) * group)],
                                  axis=0))

    def body(it, carry):
        us = range(SWA_UNROLL)
        cs = [(u, g) for u in us for g in range(SWA_KV_HEADS)]
        nb = [it * SWA_UNROLL + u for u in us]
        ks = [pl.multiple_of(jnp.clip((nb[u] - 1) * blk, 0, seq - 3 * blk), blk) for u in us]
        q0 = [pl.multiple_of(nb[u] * blk, blk) for u in us]
        kw = [_bf(k_ref[pl.ds(ks[u], 3 * blk), :]) for u in us]
        vw = [_bf(v_ref[pl.ds(ks[u], 3 * blk), :]) for u in us]
        ok = []
        for u in us:
            qpos = q0[u] + lax.broadcasted_iota(jnp.int32, (group * blk, 1), 0) % blk
            kpos = ks[u] + lax.broadcasted_iota(jnp.int32, (1, 3 * blk), 1)
            ok.append(jnp.abs(qpos - kpos) <= SWA_WINDOW)
        qg = {}
        for u in us:
            pairs = [q_ref[pl.ds(q0[u], blk), p * 128:(p + 1) * 128].astype(F32) * scale
                     for p in range(SWA_HEADS // 2)]
            for g in range(SWA_KV_HEADS):
                qs = []
                for h in range(g * group, (g + 1) * group):
                    qh = pairs[h // 2] if h % 2 == g else _swap_halves(pairs[h // 2])
                    qs.append(jnp.where(masks[g], qh, 0.0))
                qg[u, g] = _bf(jnp.concatenate(qs, axis=0))
        sw = {c: jnp.where(ok[c[0]], _dot_nt(qg[c], kw[c[0]]), NEG_INF) for c in cs}
        sx = {c: _dot_nt(qg[c], kx) for c in cs}
        m = {c: jnp.maximum(jnp.maximum(jnp.max(sw[c], axis=-1, keepdims=True),
                                        jnp.max(sx[c], axis=-1, keepdims=True)), sk[c[1]]) for c in cs}
        ew = {c: jnp.exp(sw[c] - m[c]) for c in cs}
        ex = {c: jnp.exp(sx[c] - m[c]) for c in cs}
        l = {c: jnp.sum(ew[c], axis=-1, keepdims=True) + jnp.sum(ex[c], axis=-1, keepdims=True)
             + jnp.exp(sk[c[1]] - m[c]) for c in cs}
        o = {c: (_dot(_bf(ew[c]), vw[c[0]]) + _dot(_bf(ex[c]), vx)) / l[c] for c in cs}
        for u in us:
            head_out = []
            for g in range(SWA_KV_HEADS):
                for i in range(group):
                    h = g * group + i
                    oh = o[u, g][i * blk:(i + 1) * blk]
                    head_out.append(oh if h % 2 == g else _swap_halves(oh))
            for p in range(SWA_HEADS // 2):
                o_ref[pl.ds(q0[u], blk), p * 128:(p + 1) * 128] = jnp.where(m0, head_out[2 * p], head_out[2 * p + 1])
        return carry

    lax.fori_loop(0, seq // (blk * SWA_UNROLL), body, 0)


def _swa_attn(sink, q, k, v, kx, vx, seq, past):
    tokens = q.shape[0]
    blk = lambda w: pl.BlockSpec((seq, w), lambda b: (b, 0))
    cblk = pl.BlockSpec((past, SWA_KV_W), lambda b: (b, 0))
    return pl.pallas_call(
        functools.partial(_swa_kernel, seq=seq),
        grid=(tokens // seq,),
        in_specs=[pl.BlockSpec(memory_space=pltpu.SMEM), blk(SWA_W), blk(SWA_KV_W), blk(SWA_KV_W), cblk, cblk],
        out_specs=blk(SWA_W),
        out_shape=jax.ShapeDtypeStruct((tokens, SWA_W), F32),
        compiler_params=_cp("parallel"),
        name="swa_attn",
    )(sink, q, k, v, kx, vx)


def _rw_pre_kernel(u_ref, up_ref, un_ref, mu_ref, w0_ref, w2_ref, a0_ref, a2_ref, g2_ref, kk_ref_, ka_ref_,
                   rk_ref, ones_ref, r_o, kk_o, v_o, ld_o, kka_o, kd_o, g_o, bonus_o, *, tiles_per_req):
    i = pl.program_id(0)
    u = u_ref[...]
    tm = u.shape[0]
    rowi = lax.broadcasted_iota(jnp.int32, (tm, 1), 0)
    first = (i % tiles_per_req) == 0
    last = (i % tiles_per_req) == tiles_per_req - 1
    prev_row = jnp.where(first, 0.0, up_ref[7:8, :])
    next_row = jnp.where(last, 0.0, un_ref[0:1, :])
    prev = jnp.where(rowi == 0, prev_row, pltpu.roll(u, 1, 0))
    nxt = jnp.where(rowi == tm - 1, next_row, pltpu.roll(u, tm - 1, 0))
    us = u + mu_ref[0:1, :] * (prev - u) + mu_ref[1:2, :] * (nxt - u)
    r = us[:, 0:RW_W]
    k = us[:, RW_W:2 * RW_W]
    v = us[:, 2 * RW_W:3 * RW_W]
    wl = us[:, 3 * RW_W:3 * RW_W + 128]
    al = us[:, 3 * RW_W + 128:3 * RW_W + 256]
    gl = us[:, 3 * RW_W + 256:3 * RW_W + 384]
    z = -(w0_ref[...] + _dot(_bf(jnp.tanh(wl)), w2_ref[...]))
    softplus = jnp.maximum(z, 0.0) + jnp.log(1.0 + jnp.exp(-jnp.abs(z)))
    w = -softplus - 0.5
    ld = -jnp.exp(w)
    a = _sigmoid(a0_ref[...] + _dot(_bf(al), a2_ref[...]))
    g = _dot(_bf(_sigmoid(gl)), g2_ref[...])
    ones = ones_ref[...]
    kkr = k * kk_ref_[...]
    kk = kkr * lax.rsqrt(jnp.maximum(_dot2(kkr * kkr, ones), 1e-24))
    k_a = ka_ref_[...]
    kd_f = k * (1.0 + (a[:, 0:RW_W] - 1.0) * k_a)
    kd_b = k * (1.0 + (a[:, RW_W:] - 1.0) * k_a)
    r_o[...] = r
    kk_o[...] = kk
    v_o[...] = v
    ld_o[...] = ld
    kka_o[:, 0:RW_W] = kk * a[:, 0:RW_W]
    kka_o[:, RW_W:] = kk * a[:, RW_W:]
    kd_o[:, 0:RW_W] = kd_f
    kd_o[:, RW_W:] = kd_b
    g_o[...] = g
    bonus_o[...] = _dot2(r * (kd_f + kd_b) * rk_ref[...], ones) * v


def _rw_pre(urw, p, seq):
    tokens = urw.shape[0]
    tm = _row_tile(seq)
    tpr = seq // tm
    nt = tokens // tm
    r8 = tm // 8
    row = lambda w: pl.BlockSpec((tm, w), lambda i: (i, 0))
    in_specs = [row(RW_IN_W),
                pl.BlockSpec((8, RW_IN_W), lambda i: (jnp.maximum(i * r8 - 1, 0), 0)),
                pl.BlockSpec((8, RW_IN_W), lambda i: (jnp.minimum((i + 1) * r8, nt * r8 - 1), 0)),
                _full((2, RW_IN_W)), _full((1, 2 * RW_W)), _full((128, 2 * RW_W)), _full((1, 2 * RW_W)),
                _full((128, 2 * RW_W)), _full((128, RW_W)), _full((1, RW_W)), _full((1, RW_W)),
                _full((1, RW_W)), _full((RW_W, RW_W))]
    widths = [RW_W, RW_W, RW_W, 2 * RW_W, 2 * RW_W, 2 * RW_W, RW_W, RW_W]
    return pl.pallas_call(
        functools.partial(_rw_pre_kernel, tiles_per_req=tpr),
        grid=(nt,),
        in_specs=in_specs,
        out_specs=[row(w) for w in widths],
        out_shape=[jax.ShapeDtypeStruct((tokens, w), F32) for w in widths],
        compiler_params=_cp("parallel"),
        name="rw_pre",
    )(urw, urw, urw, p['mu'], p['w0'], p['w2'], p['a0'], p['a2'], p['g2'], p['k_k'], p['k_a'], p['r_k'],
      p['ones256'])


def _rw_masks():
    t = RW_CHUNK
    n = RW_HEADS * t
    tt = np.arange(t)[:, None]
    ss = (np.arange(n) % t)[None, :]
    before = np.stack([ss < tt, ss > tt])
    diag = (ss == tt)
    strict = before.astype(np.float32)
    incl = (before | diag[None]).astype(np.float32)
    eye = diag.astype(np.float32)
    ti = np.arange(t)
    tri = np.stack([ti[None, :] <= ti[:, None], ti[None, :] >= ti[:, None]]).astype(np.float32)
    hd = np.arange(n) // t
    same = (hd[:, None] == hd[None, :]).astype(np.float32)
    return (jnp.asarray(strict), jnp.asarray(incl), jnp.asarray(tri, dtype=BF16), jnp.asarray(same),
            jnp.asarray(eye))


def _rw_scan_kernel(r_ref, kk_ref, v_ref, ld_ref, kka_ref, kd_ref, s0_ref, strict_ref, incl_ref, tri_ref,
                    same_ref, eye_ref, o_ref, sfin_ref, s_scr, *, nsub):
    d = pl.program_id(1)
    c = pl.program_id(2)
    t = RW_CHUNK
    n = RW_HEADS * t

    @pl.when(c == 0)
    def _():
        s_scr[...] = s0_ref[0, 0]

    strict = strict_ref[0]
    incl = incl_ref[0]
    tri = tri_ref[0]
    eye = eye_ref[...]
    same = same_ref[...]
    same_bf = _bf(same)

    def bd(x):
        return jnp.concatenate([_bf(x)] * RW_HEADS, axis=0) * same_bf

    js = range(nsub)
    rows = [pl.ds(pl.multiple_of((j + d * (nsub - 1 - 2 * j)) * t, t), t) for j in js]
    ld = [ld_ref[rows[j], :] for j in js]
    cum = []
    for j in js:
        lhi, lmid, llo = _split3(ld[j])
        cum.append(_dot(tri, lhi) + _dot(tri, lmid) + _dot(tri, llo))
    cend = [jnp.sum(ld[j], axis=0, keepdims=True) for j in js]
    kka = [kka_ref[rows[j], :] for j in js]
    kd = [kd_ref[rows[j], :] for j in js]
    v = [v_ref[rows[j], :] for j in js]
    at = [-kk_ref[rows[j], :] * jnp.exp(cum[j] - ld[j]) for j in js]
    rt = [r_ref[rows[j], :] * jnp.exp(cum[j]) for j in js]
    e_inv = [jnp.exp(-cum[j]) for j in js]
    aa = [_dot_nt(_bf(jnp.concatenate([at[j], rt[j]], axis=0)),
                  jnp.concatenate([bd(kka[j] * e_inv[j]), bd(kd[j] * e_inv[j])], axis=0)) for j in js]
    a_ab = [aa[j][0:t, 0:n] * strict for j in js]
    x = [eye + a_ab[j] for j in js]
    pw = a_ab
    for _ in range(RW_CHUNK.bit_length() - 3):
        pw = [_dot(_bf(pw[j]), bd(pw[j])) for j in js]
        x = [x[j] + _dot(_bf(x[j]), bd(pw[j])) for j in js]
    xs = [_split2(x[j]) for j in js]
    sa = [_split2(a_ab[j]) for j in js]
    ax = [_dot(sa[j][0], bd(xs[j][0])) + _dot(sa[j][1], bd(xs[j][0])) + _dot(sa[j][0], bd(xs[j][1])) for j in js]
    x = [x[j] + _dot(xs[j][0], bd(eye - x[j] + ax[j])) for j in js]
    v_bd = [bd(v[j]) for j in js]
    wv = [_dot(_bf(aa[j][0:t, n:] * strict), v_bd[j]) for j in js]
    mu = [_dot(_bf(x[j]), jnp.concatenate([bd(at[j]), bd(wv[j])], axis=1)) for j in js]
    m1 = [mu[j][:, 0:n] for j in js]
    u0 = [mu[j][:, n:] for j in js]
    e_end = [jnp.exp(cend[j] - cum[j]) for j in js]
    bend = [_bf(kka[j] * e_end[j]) for j in js]
    g = [_bf(_dot_tn(_bf(m1[j]), bend[j]) * same) for j in js]
    cst = [_dot_tn(_bf(jnp.concatenate([u0[j], v[j]], axis=0)),
                   jnp.concatenate([bend[j], _bf(kd[j] * e_end[j])], axis=0)) * same for j in js]
    qo = [_dot(_bf(aa[j][t:, 0:n] * incl), jnp.concatenate([bd(m1[j]), bd(u0[j])], axis=1)) for j in js]
    q = [_bf(rt[j] + qo[j][:, 0:n]) for j in js]
    o0 = [qo[j][:, n:] + _dot(_bf(aa[j][t:, n:] * incl), v_bd[j]) for j in js]

    s = s_scr[...]
    for j in js:
        s_bf = _bf(s)
        o_ref[0, rows[j], :] = _dot_nt(q[j], s_bf) + o0[j]
        s = s * jnp.exp(cend[j]) + _dot(s_bf, g[j]) + cst[j]
    s_scr[...] = s

    @pl.when(c == pl.num_programs(2) - 1)
    def _():
        sfin_ref[0, 0] = s


def _rw_scan(r, kk, v, ld, kka, kd, s0_bd, consts, seq):
    tokens = r.shape[0]
    nreq = tokens // seq
    tb = min(seq, 512)
    nblk = seq // tb
    nsub = tb // RW_CHUNK
    n = RW_HEADS * RW_CHUNK
    cc = lambda d, c: c + d * (nblk - 1 - 2 * c)
    shared = pl.BlockSpec((tb, RW_W), lambda b, d, c: (b * nblk + cc(d, c), 0))
    dirw = pl.BlockSpec((tb, RW_W), lambda b, d, c: (b * nblk + cc(d, c), d))
    strict, incl, tri, same, eye = consts
    return pl.pallas_call(
        functools.partial(_rw_scan_kernel, nsub=nsub),
        grid=(nreq, 2, nblk),
        in_specs=[shared, shared, shared, dirw, dirw, dirw,
                  pl.BlockSpec((1, 1, n, n), lambda b, d, c: (b, d, 0, 0)),
                  pl.BlockSpec((1, RW_CHUNK, n), lambda b, d, c: (d, 0, 0)),
                  pl.BlockSpec((1, RW_CHUNK, n), lambda b, d, c: (d, 0, 0)),
                  pl.BlockSpec((1, RW_CHUNK, RW_CHUNK), lambda b, d, c: (d, 0, 0)),
                  _full((n, n)), _full((RW_CHUNK, n))],
        out_specs=[pl.BlockSpec((1, tb, RW_W), lambda b, d, c: (d, b * nblk + cc(d, c), 0)),
                   pl.BlockSpec((1, 1, n, n), lambda b, d, c: (b, d, 0, 0))],
        out_shape=[jax.ShapeDtypeStruct((2, tokens, RW_W), F32), jax.ShapeDtypeStruct((nreq, 2, n, n), F32)],
        scratch_shapes=[pltpu.VMEM((n, n), F32)],
        compiler_params=_cp("parallel", "parallel", "arbitrary"),
        name="rw_scan",
    )(r, kk, v, ld, kka, kd, s0_bd, strict, incl, tri, same, eye)


def _state_to_blockdiag(s):
    b = s.shape[0]
    eye = jnp.eye(RW_HEADS, dtype=s.dtype)
    out = s[:, :, :, :, None, :] * eye[None, None, :, None, :, None]
    return out.reshape(b, 2, RW_W, RW_W)


def _blockdiag_to_state(sbd):
    b = sbd.shape[0]
    s = sbd.reshape(b, 2, RW_HEADS, HEAD_DIM, RW_HEADS, HEAD_DIM)
    idx = jnp.arange(RW_HEADS)
    return jnp.transpose(s[:, :, idx, :, idx, :], (1, 2, 0, 3, 4))


def _finish_kernel(x_ref, oa_ref, oc_ref, o2_ref, bonus_ref, g_ref, mod_ref, wout_ref, lng_ref, lnb_ref,
                   n2_ref, wr_hi_ref, wr_lo_ref, ones_ref, x1_ref, h2_ref, aff_ref):
    ones = ones_ref[0:128, 0:128]
    g1 = mod_ref[0, 2:3, :]
    sh2 = mod_ref[0, 3:4, :]
    sc2 = mod_ref[0, 4:5, :]
    subs = [slice(i, i + SUB_ROWS) for i in range(0, x_ref.shape[0], SUB_ROWS)]
    mixins = []
    for sl in subs:
        y = o2_ref[0, sl, :] + o2_ref[1, sl, :]
        mu = _head_sums(y, ones) * (1.0 / HEAD_DIM)
        yc = y - mu
        var = _head_sums(yc * yc, ones) * (1.0 / HEAD_DIM)
        yn = yc * lax.rsqrt(var + GN_EPS) * lng_ref[...] + lnb_ref[...]
        ob = (yn + bonus_ref[sl, :]) * g_ref[sl, :]
        mixins.append(jnp.concatenate([_bf(oa_ref[sl, :]), _bf(ob), _bf(oc_ref[sl, :])], axis=1))
    mixes = [_dot(mixin, wout_ref[...]) for mixin in mixins]
    for sl, mix in zip(subs, mixes):
        x1 = x_ref[sl, :] + g1 * mix
        ms = jnp.mean(x1 * x1, axis=-1, keepdims=True)
        h2 = x1 * lax.rsqrt(ms + NORM_EPS) * n2_ref[...] * (1.0 + sc2) + sh2
        x1_ref[sl, :] = x1
        h2_ref[sl, :] = _bf(h2)
        hhi, hlo = _split2(h2)
        logits = _dot(hhi, wr_hi_ref[...]) + _dot(hlo, wr_hi_ref[...]) + _dot(hhi, wr_lo_ref[...])
        m = jnp.max(logits, axis=-1, keepdims=True)
        e = jnp.exp(logits - m)
        aff_ref[sl, :] = e / jnp.sum(e, axis=-1, keepdims=True)


def _finish(x, oa, oc, o2, bonus, g, mod, p, seq, per_request_mod):
    tokens, d = x.shape
    tm = _row_tile(seq)
    tpr = seq // tm
    mod_map = (lambda i: (i // tpr, 0, 0)) if per_request_mod else (lambda i: (0, 0, 0))
    row = lambda w: pl.BlockSpec((tm, w), lambda i: (i, 0))
    return pl.pallas_call(
        _finish_kernel,
        grid=(tokens // tm,),
        in_specs=[row(d), row(NA_W), row(SWA_W), pl.BlockSpec((2, tm, RW_W), lambda i: (0, i, 0)), row(RW_W),
                  row(RW_W), pl.BlockSpec((1, 6, d), mod_map), _full((d, d)), _full((1, RW_W)),
                  _full((1, RW_W)), _full((1, d)), _full((d, N_EXPERTS)), _full((d, N_EXPERTS)),
                  _full((RW_W, RW_W))],
        out_specs=[row(d), row(d), row(N_EXPERTS)],
        out_shape=[jax.ShapeDtypeStruct((tokens, d), F32), jax.ShapeDtypeStruct((tokens, d), BF16),
                   jax.ShapeDtypeStruct((tokens, N_EXPERTS), F32)],
        compiler_params=_cp("parallel"),
        name="finish",
    )(x, oa, oc, o2, bonus, g, mod, p['w_out'], p['ln_g'], p['ln_b'], p['n2'], p['wr_hi'], p['wr_lo'],
      p['ones256'])


def _topk_kernel(aff_ref, tri_ref, eye_ref, place_ref, slot_ref, slotrow_ref, gfull_ref, ends_ref, *, cap, group,
                 tb):
    b = pl.program_id(0)
    aff = aff_ref[...]
    seq = aff.shape[0]
    bits = lax.bitcast_convert_type(aff, jnp.int32)
    capf = jnp.float32(cap)
    eye = eye_ref[...]
    ghi, gmid, glo = _split3(aff)
    aff_t = _dot_nt(eye, ghi) + _dot_nt(eye, gmid) + _dot_nt(eye, glo)
    bits_t = lax.bitcast_convert_type(aff_t, jnp.int32)

    def bis(_, carry):
        lo, hi = carry
        mid = lo + ((hi - lo + 1) >> 1)
        cnt = jnp.sum(jnp.where(bits_t >= mid, 1.0, 0.0), axis=1, keepdims=True)
        ge = cnt >= capf
        return jnp.where(ge, mid, lo), jnp.where(ge, hi, mid - 1)

    lo0 = jnp.zeros((N_EXPERTS, 1), jnp.int32)
    hi0 = jnp.full((N_EXPERTS, 1), 0x7F7FFFFF, jnp.int32)
    thr_col, _ = lax.fori_loop(0, 31, bis, (lo0, hi0))
    ri = lax.broadcasted_iota(jnp.int32, (N_EXPERTS, N_EXPERTS), 0)
    ci = lax.broadcasted_iota(jnp.int32, (N_EXPERTS, N_EXPERTS), 1)
    thr = jnp.sum(jnp.where(ri == ci, thr_col, 0), axis=0, keepdims=True)
    gt = jnp.where(bits > thr, 1.0, 0.0)
    eq = jnp.where(bits == thr, 1.0, 0.0)
    need = capf - jnp.sum(gt, axis=0, keepdims=True)
    offset = ((b % group) * cap).astype(F32)
    tri = tri_ref[...]
    carry_g = jnp.zeros((1, N_EXPERTS), F32)
    carry_e = jnp.zeros((1, N_EXPERTS), F32)
    for blk in range(seq // tb):
        sl = slice(blk * tb, (blk + 1) * tb)
        pg = _dot(tri, _bf(gt[sl])) + carry_g
        pe = _dot(tri, _bf(eq[sl])) + carry_e
        carry_g = pg[tb - 1:tb, :]
        carry_e = pe[tb - 1:tb, :]
        sel = gt[sl] + eq[sl] * jnp.where(pe <= need, 1.0, 0.0)
        slot = jnp.where(sel > 0.5, pg + jnp.minimum(pe, need) - 1.0 + offset, -1.0)
        slot_ref[sl, :] = slot
        ends_ref[0, blk:blk + 1, :] = carry_g + jnp.minimum(carry_e, need) + offset
        shi, slo = _split2(slot)
        slotrow_ref[0, :, 0, sl] = _dot_nt(eye, shi) + _dot_nt(eye, slo)
        gfull_ref[sl, :] = _bf(_dot(ghi[sl], place_ref[0]) + _dot(gmid[sl], place_ref[1])
                               + _dot(glo[sl], place_ref[2]))


def _topk(aff, seq, group):
    tokens = aff.shape[0]
    nreq = tokens // seq
    cap = EC_CAPACITY * seq // N_EXPERTS
    tb = min(seq, 512)
    ti = np.arange(tb)
    tri = jnp.asarray((ti[None, :] <= ti[:, None]).astype(np.float32), dtype=BF16)
    eye = jnp.asarray(np.eye(N_EXPERTS, dtype=np.float32), dtype=BF16)
    place = np.zeros((3, N_EXPERTS, 128), np.float32)
    for s in range(3):
        place[s, np.arange(N_EXPERTS), s * N_EXPERTS + np.arange(N_EXPERTS)] = 1.0
    place = jnp.asarray(place, dtype=BF16)
    nblk = seq // tb
    slot, slotrow, gfull, ends = pl.pallas_call(
        functools.partial(_topk_kernel, cap=cap, group=group, tb=tb),
        grid=(nreq,),
        in_specs=[pl.BlockSpec((seq, N_EXPERTS), lambda b: (b, 0)), _full((tb, tb)),
                  _full((N_EXPERTS, N_EXPERTS)), _full((3, N_EXPERTS, 128))],
        out_specs=[pl.BlockSpec((seq, N_EXPERTS), lambda b: (b, 0)),
                   pl.BlockSpec((1, N_EXPERTS, 1, seq), lambda b: (b // group, 0, 0, b % group)),
                   pl.BlockSpec((seq, 128), lambda b: (b, 0)),
                   pl.BlockSpec((1, nblk, N_EXPERTS), lambda b: (b, 0, 0))],
        out_shape=[jax.ShapeDtypeStruct((tokens, N_EXPERTS), F32),
                   jax.ShapeDtypeStruct((nreq // group, N_EXPERTS, 1, group * seq), F32),
                   jax.ShapeDtypeStruct((tokens, 128), BF16),
                   jax.ShapeDtypeStruct((nreq, nblk, N_EXPERTS), F32)],
        compiler_params=_cp("parallel"),
        name="topk",
    )(aff, tri, eye, place)
    ends = ends.reshape(nreq // group, group * nblk, N_EXPERTS).transpose(0, 2, 1)
    return slot, slotrow, gfull, ends.astype(jnp.int32).reshape(-1), tb


MOE_EB = 8


def _moe_dispatch_kernel(ends_ref, h_ref, slotrow_ref, gfull_ref, xe_ref, gs_ref, *, ct, nch):
    gi = pl.program_id(0)
    eb = pl.program_id(1)
    c = pl.program_id(2)
    mt = 128

    @pl.when(c == 0)
    def _():
        xe_ref[...] = jnp.zeros_like(xe_ref)
        gs_ref[...] = jnp.zeros_like(gs_ref)

    jcol = lax.broadcasted_iota(jnp.int32, (mt, 1), 0)
    starts, his, pieces = [], [], []
    for i in range(MOE_EB):
        base = (gi * N_EXPERTS + eb * MOE_EB + i) * nch
        lo = jnp.where(c == 0, 0, ends_ref[base + jnp.maximum(c - 1, 0)])
        his.append(ends_ref[base + c])
        start = pl.multiple_of(jnp.minimum((lo // 16) * 16, ct - mt), 16)
        starts.append(start)
        pieces.append(_bf(jnp.where(slotrow_ref[0, i] == (jcol + start).astype(F32), 1.0, 0.0)))
    onehot = jnp.concatenate(pieces, axis=0)
    xw = _bf(_dot(onehot, h_ref[...]))
    gw = _bf(_dot(onehot, gfull_ref[...]))
    for i in range(MOE_EB):
        rows = pl.ds(starts[i], mt)
        xe_ref[0, i, rows, :] += xw[i * mt:(i + 1) * mt]
        gs_ref[0, i, rows, :] += gw[i * mt:(i + 1) * mt]
    for i in range(MOE_EB):
        for w in range(1, ct // mt):
            wlo = starts[i] + w * mt

            @pl.when(wlo < his[i])
            def _(i=i, wlo=wlo):
                ws = pl.multiple_of(jnp.minimum(wlo, ct - mt), 16)
                slot = slotrow_ref[0, i]
                hit = (jnp.where(slot == (jcol + ws).astype(F32), 1.0, 0.0)
                       * jnp.where(slot >= wlo.astype(F32), 1.0, 0.0))
                rows = pl.ds(ws, mt)
                xe_ref[0, i, rows, :] += _bf(_dot(_bf(hit), h_ref[...]))
                gs_ref[0, i, rows, :] += _bf(_dot(_bf(hit), gfull_ref[...]))


def _moe_dispatch(ends, h2, slotrow, gfull, lg, ct, kc):
    tokens, d = h2.shape
    ngrp = tokens // lg
    nch = lg // kc
    grid_spec = pltpu.PrefetchScalarGridSpec(
        num_scalar_prefetch=1,
        grid=(ngrp, N_EXPERTS // MOE_EB, nch),
        in_specs=[pl.BlockSpec((kc, d), lambda gi, eb, c, ends: (gi * nch + c, 0)),
                  pl.BlockSpec((1, MOE_EB, 1, kc), lambda gi, eb, c, ends: (gi, eb, 0, c)),
                  pl.BlockSpec((kc, 128), lambda gi, eb, c, ends: (gi * nch + c, 0))],
        out_specs=[pl.BlockSpec((1, MOE_EB, ct, d), lambda gi, eb, c, ends: (gi, eb, 0, 0)),
                   pl.BlockSpec((1, MOE_EB, ct, 128), lambda gi, eb, c, ends: (gi, eb, 0, 0))])
    return pl.pallas_call(
        functools.partial(_moe_dispatch_kernel, ct=ct, nch=nch),
        grid_spec=grid_spec,
        out_shape=[jax.ShapeDtypeStruct((ngrp, N_EXPERTS, ct, d), BF16),
                   jax.ShapeDtypeStruct((ngrp, N_EXPERTS, ct, 128), BF16)],
        compiler_params=_cp("parallel", "parallel", "arbitrary"),
        name="moe_dispatch",
    )(ends, h2, slotrow, gfull)


def _moe_ffn_kernel(xe_ref, gs_ref, mod_ref, wg_ref, wu_ref, wd_ref, ye_ref):
    e = pl.program_id(1)
    lane = lax.broadcasted_iota(jnp.int32, (1, 128), 1)
    pick = (lane == e) | (lane == e + N_EXPERTS) | (lane == e + 2 * N_EXPERTS)
    gate = jnp.sum(jnp.where(pick, gs_ref[0, 0].astype(F32), 0.0), axis=-1, keepdims=True)
    xb = xe_ref[0, 0]
    hg = _dot(xb, wg_ref[0])
    hu = _dot(xb, wu_ref[0])
    he = _bf(hg * _sigmoid(hg) * hu)
    y = _dot(he, wd_ref[0])
    ye_ref[0, 0] = _bf(y * gate * mod_ref[0, 5:6, :])


def _moe_ffn(xe, gs, mod, wg, wu, wd, per_group_mod):
    ngrp, _, ct, d = xe.shape
    f = wg.shape[2]
    mod_map = (lambda gi, e: (gi, 0, 0)) if per_group_mod else (lambda gi, e: (0, 0, 0))
    return pl.pallas_call(
        _moe_ffn_kernel,
        grid=(ngrp, N_EXPERTS),
        in_specs=[pl.BlockSpec((1, 1, ct, d), lambda gi, e: (gi, e, 0, 0)),
                  pl.BlockSpec((1, 1, ct, 128), lambda gi, e: (gi, e, 0, 0)),
                  pl.BlockSpec((1, 6, d), mod_map),
                  pl.BlockSpec((1, d, f), lambda gi, e: (e, 0, 0)),
                  pl.BlockSpec((1, d, f), lambda gi, e: (e, 0, 0)),
                  pl.BlockSpec((1, f, d), lambda gi, e: (e, 0, 0))],
        out_specs=pl.BlockSpec((1, 1, ct, d), lambda gi, e: (gi, e, 0, 0)),
        out_shape=jax.ShapeDtypeStruct((ngrp, N_EXPERTS, ct, d), BF16),
        compiler_params=_cp("parallel", "parallel"),
        name="moe_ffn",
    )(xe, gs, mod, wg, wu, wd)


def _moe_combine_kernel(ends_ref, x1_ref, slot_ref, ye_ref, o_ref, win_scr, *, ct, nch):
    gi = pl.program_id(0)
    j = pl.program_id(1)
    mt = 128
    wide = N_EXPERTS * mt
    shi, slo = _split2(slot_ref[...])
    col_e = lax.broadcasted_iota(jnp.int32, (N_EXPERTS, wide), 1) // mt
    row_e = lax.broadcasted_iota(jnp.int32, (N_EXPERTS, wide), 0)
    expand = _bf(jnp.where(col_e == row_e, 1.0, 0.0))
    sb = _dot(shi, expand) + _dot(slo, expand)
    lane = lax.broadcasted_iota(jnp.int32, (1, mt), 1)
    starts, his, targets = [], [], []
    for e in range(N_EXPERTS):
        base = (gi * N_EXPERTS + e) * nch
        lo = jnp.where(j == 0, 0, ends_ref[base + jnp.maximum(j - 1, 0)])
        his.append(ends_ref[base + j])
        start = pl.multiple_of(jnp.minimum((lo // 16) * 16, ct - mt), 16)
        win_scr[e * mt:(e + 1) * mt, :] = ye_ref[0, e, pl.ds(start, mt), :]
        starts.append(start)
        targets.append((lane + start).astype(F32))
    onehot = _bf(jnp.where(sb == jnp.concatenate(targets, axis=1), 1.0, 0.0))
    o_ref[...] = x1_ref[...] + _dot(onehot, win_scr[...])
    for e in range(N_EXPERTS):
        for w in range(1, ct // mt):
            wlo = starts[e] + w * mt

            @pl.when(wlo < his[e])
            def _(e=e, wlo=wlo):
                ws = pl.multiple_of(jnp.minimum(wlo, ct - mt), 16)
                sbe = sb[:, e * mt:(e + 1) * mt]
                hit = jnp.where(sbe == (lane + ws).astype(F32), 1.0, 0.0) * jnp.where(sbe >= wlo.astype(F32), 1.0, 0.0)
                o_ref[...] += _dot(_bf(hit), ye_ref[0, e, pl.ds(ws, mt), :])


def _moe_combine(ends, x1, slot, ye, lg, ct, kc):
    tokens, d = x1.shape
    ngrp = tokens // lg
    nch = lg // kc
    grid_spec = pltpu.PrefetchScalarGridSpec(
        num_scalar_prefetch=1,
        grid=(ngrp, nch),
        in_specs=[pl.BlockSpec((kc, d), lambda gi, j, ends: (gi * nch + j, 0)),
                  pl.BlockSpec((kc, N_EXPERTS), lambda gi, j, ends: (gi * nch + j, 0)),
                  pl.BlockSpec((1, N_EXPERTS, ct, d), lambda gi, j, ends: (gi, 0, 0, 0),
                               pipeline_mode=pl.Buffered(1))],
        out_specs=pl.BlockSpec((kc, d), lambda gi, j, ends: (gi * nch + j, 0)),
        scratch_shapes=[pltpu.VMEM((N_EXPERTS * 128, d), BF16)])
    return pl.pallas_call(
        functools.partial(_moe_combine_kernel, ct=ct, nch=nch),
        grid_spec=grid_spec,
        out_shape=jax.ShapeDtypeStruct((tokens, d), F32),
        compiler_params=_cp("parallel", "arbitrary"),
        name="moe_combine",
    )(ends, x1, slot, ye)


def _rope_tables(seq):
    t = np.arange(seq)
    n_freq = HEAD_DIM // 4
    inv = ROPE_THETA ** (-np.arange(n_freq, dtype=np.float32) / n_freq)
    ang = np.concatenate([(t // GRID_W).astype(np.float32)[:, None] * inv,
                          (t % GRID_W).astype(np.float32)[:, None] * inv], axis=-1)
    ang = jnp.asarray(ang, dtype=F32)
    cos, sin = jnp.cos(ang), jnp.sin(ang)
    cos_t = jnp.tile(jnp.concatenate([cos, cos], axis=-1), (1, SWA_HEADS))
    sin_t = jnp.tile(jnp.concatenate([-sin, sin], axis=-1), (1, SWA_HEADS))
    return cos_t, sin_t


def _blockdiag2(w):
    z = jnp.zeros_like(w[0])
    return jnp.concatenate([jnp.concatenate([w[0], z], axis=1), jnp.concatenate([z, w[1]], axis=1)], axis=0)


def _layer_params(l, ada_w, ada_b, norm1_g, norm2_g, w_in, na_q_norm, na_k_norm, na_rpb, rw_mu, rw_w0, rw_w2,
                  rw_a0, rw_a2, rw_g2, rw_k_k, rw_k_a, rw_r_k, rw_ln_g, rw_ln_b, swa_q_norm, swa_k_norm,
                  swa_sink, w_out, w_router, w_gate, w_up, w_down):
    wr = w_router[l]
    wr_hi = wr.astype(BF16)
    w_rw = w_in[l][:, 3 * NA_W:3 * NA_W + RW_IN_W]
    return {
        'n1': norm1_g[l][None], 'n2': norm2_g[l][None], 'w_in': w_in[l].astype(BF16),
        'w_rw_lo': (w_rw - w_rw.astype(BF16).astype(F32)).astype(BF16),
        'gains': (jnp.tile(na_q_norm[l], NA_HEADS)[None], jnp.tile(na_k_norm[l], NA_HEADS)[None],
                  jnp.tile(swa_q_norm[l], SWA_HEADS)[None], jnp.tile(swa_k_norm[l], SWA_KV_HEADS)[None]),
        'bias': _na_bias_table(na_rpb[l]),
        'mu': rw_mu[l], 'w0': rw_w0[l].reshape(1, 2 * RW_W), 'w2': _blockdiag2(rw_w2[l]).astype(BF16),
        'a0': rw_a0[l].reshape(1, 2 * RW_W), 'a2': _blockdiag2(rw_a2[l]).astype(BF16),
        'g2': rw_g2[l].astype(BF16), 'k_k': rw_k_k[l][None], 'k_a': rw_k_a[l][None],
        'r_k': rw_r_k[l].reshape(1, RW_W), 'ln_g': rw_ln_g[l][None], 'ln_b': rw_ln_b[l][None],
        'sink': swa_sink[l], 'w_out': w_out[l].astype(BF16),
        'wr_hi': wr_hi, 'wr_lo': (wr - wr_hi.astype(F32)).astype(BF16),
        'wg': w_gate[l].astype(BF16), 'wu': w_up[l].astype(BF16), 'wd': w_down[l].astype(BF16),
        'ones256': _block_ones(RW_W),
    }


def _mix_and_ffn(x, mod, p, oa, oc, urw, s0_bd, scan_consts, seq, per_request_mod, group):
    r, kk, v, ld, kka, kd, g, bonus = _rw_pre(urw, p, seq)
    o2, sfin = _rw_scan(r, kk, v, ld, kka, kd, s0_bd, scan_consts, seq)
    x1, h2, aff = _finish(x, oa, oc, o2, bonus, g, mod, p, seq, per_request_mod)
    slot, slotrow, gfull, ends, kc = _topk(aff, seq, group)
    cap = EC_CAPACITY * seq // N_EXPERTS
    lg, ct = group * seq, group * cap
    xe, gs = _moe_dispatch(ends, h2, slotrow, gfull, lg, ct, kc)
    ye = _moe_ffn(xe, gs, mod, p['wg'], p['wu'], p['wd'], per_request_mod)
    return _moe_combine(ends, x1, slot, ye, lg, ct, kc), sfin


def _context_layer(x, mod, p, ones384, scan_consts, seq):
    qa, ka, va, urw, qc, kc, vc = _proj(x, mod, p['n1'], p['w_in'], p['gains'], ones384, None, seq, False, F32,
                                        w_rw_lo=p['w_rw_lo'])
    oa, oc = _ctx_attn(p['sink'], qa, ka, va, qc, kc, vc, seq)
    nreq = x.shape[0] // seq
    s0 = jnp.zeros((nreq, 2, RW_W, RW_W), F32)
    y, sfin = _mix_and_ffn(x, mod, p, oa, oc, urw, s0, scan_consts, seq, False, 8)
    return y, ka, va, kc, vc, sfin


def _latent_layer(x, mod, p, ones384, scan_consts, rope_tabs, seq, kx_na, vx_na, kx_swa, vx_swa, s0_bd, past):
    qa, ka, va, urw, qc, kc, vc = _proj(x, mod, p['n1'], p['w_in'], p['gains'], ones384, rope_tabs, seq, True,
                                        BF16)
    oa = _na_attn(qa, ka, va, kx_na, vx_na, p['bias'], seq, past)
    oc = _swa_attn(p['sink'], qc, kc, vc, kx_swa, vx_swa, seq, past)
    y, _ = _mix_and_ffn(x, mod, p, oa, oc, urw, s0_bd, scan_consts, seq, True, 1)
    return y


def _heads_first(z, nreq, seq, heads):
    return z.reshape(nreq, seq, heads, HEAD_DIM).transpose(0, 2, 1, 3)


def _tokens_first(z):
    b, h, n, dh = z.shape
    return z.transpose(0, 2, 1, 3).reshape(b * n, h * dh)


def kernel(x_prompt, x_sample, cache_na_k, cache_na_v, cache_swa_k, cache_swa_v, state_rwkv, c, c_ctx, ada_w, ada_b, norm1_g, norm2_g, w_in, na_q_norm, na_k_norm, na_rpb, rw_mu, rw_w0, rw_w2, rw_a0, rw_a2, rw_g2, rw_k_k, rw_k_a, rw_r_k, rw_ln_g, rw_ln_b, swa_q_norm, swa_k_norm, swa_sink, w_out, w_router, w_gate, w_up, w_down):
    nb, seq, d = x_prompt.shape
    db, dseq, _ = x_sample.shape
    depth = ada_w.shape[0]
    past = cache_na_k.shape[3]
    cond = jnp.concatenate([c, c_ctx[None], jnp.zeros((16 - db - 1, d), F32)], axis=0)
    mod_all = _adaln(cond, ada_w, ada_b).reshape(depth, 16, 6, d)
    ones384 = _block_ones(NA_W)
    scan_consts = _rw_masks()
    rope_tabs = _rope_tables(dseq)
    xp = x_prompt.reshape(nb * seq, d)
    xs = x_sample.reshape(db * dseq, d)
    new_ka, new_va, new_kc, new_vc, new_s = [], [], [], [], []
    for l in range(depth):
        p = _layer_params(l, ada_w, ada_b, norm1_g, norm2_g, w_in, na_q_norm, na_k_norm, na_rpb, rw_mu, rw_w0,
                          rw_w2, rw_a0, rw_a2, rw_g2, rw_k_k, rw_k_a, rw_r_k, rw_ln_g, rw_ln_b, swa_q_norm,
                          swa_k_norm, swa_sink, w_out, w_router, w_gate, w_up, w_down)
        mod_ctx = mod_all[l, db:db + 1]
        mod_lat = mod_all[l, 0:db]
        xp, ka, va, kc, vc, sfin = _context_layer(xp, mod_ctx, p, ones384, scan_consts, seq)
        new_ka.append(_heads_first(ka, nb, seq, NA_HEADS))
        new_va.append(_heads_first(va, nb, seq, NA_HEADS))
        new_kc.append(_heads_first(kc, nb, seq, SWA_KV_HEADS))
        new_vc.append(_heads_first(vc, nb, seq, SWA_KV_HEADS))
        new_s.append(_blockdiag_to_state(sfin))
        xs = _latent_layer(xs, mod_lat, p, ones384, scan_consts, rope_tabs, dseq,
                           _tokens_first(cache_na_k[:, l]), _tokens_first(cache_na_v[:, l]),
                           _tokens_first(cache_swa_k[:, l]), _tokens_first(cache_swa_v[:, l]),
                           _state_to_blockdiag(state_rwkv[:, l]), past)
    return (xp.reshape(nb, seq, d), xs.reshape(db, dseq, d), jnp.stack(new_ka, axis=1),
            jnp.stack(new_va, axis=1), jnp.stack(new_kc, axis=1), jnp.stack(new_vc, axis=1),
            jnp.stack(new_s, axis=1))
```

```python
import functools

import numpy as np
import jax
import jax.numpy as jnp
from jax import lax
from jax.experimental import pallas as pl
from jax.experimental.pallas import tpu as pltpu

F32 = jnp.float32
BF16 = jnp.bfloat16

HEAD_DIM = 64
GRID_W = 64
NA_HEADS = 6
NA_KH = 8
NA_KW = 16
RW_HEADS = 4
SWA_HEADS = 6
SWA_KV_HEADS = 2
SWA_WINDOW = 128
N_EXPERTS = 16
EC_CAPACITY = 2
ROPE_THETA = 10000.0
NORM_EPS = 1e-6
GN_EPS = 64e-5
NEG_INF = -1e30
SUB_ROWS = 256
NA_UNROLL = 8
SWA_UNROLL = 2
RW_CHUNK = 64
RW_W = RW_HEADS * HEAD_DIM
NA_W = NA_HEADS * HEAD_DIM
SWA_W = SWA_HEADS * HEAD_DIM
SWA_KV_W = SWA_KV_HEADS * HEAD_DIM
RW_IN_W = 1152
VMEM_LIMIT = 56 * 1024 * 1024


def _cp(*sem):
    return pltpu.CompilerParams(dimension_semantics=sem, vmem_limit_bytes=VMEM_LIMIT)


def _bf(x):
    return x.astype(BF16)


def _dot(a, b):
    return jnp.dot(a, b, preferred_element_type=F32)


def _dot_nt(a, b):
    return lax.dot_general(a, b, (((1,), (1,)), ((), ())), preferred_element_type=F32)


def _dot_tn(a, b):
    return lax.dot_general(a, b, (((0,), (0,)), ((), ())), preferred_element_type=F32)


def _split2(x):
    hi = x.astype(BF16)
    lo = (x - hi.astype(F32)).astype(BF16)
    return hi, lo


def _split3(x):
    hi = x.astype(BF16)
    r1 = x - hi.astype(F32)
    mid = r1.astype(BF16)
    lo = (r1 - mid.astype(F32)).astype(BF16)
    return hi, mid, lo


def _dot2(a, b_bf):
    hi, lo = _split2(a)
    return _dot(hi, b_bf) + _dot(lo, b_bf)


def _sigmoid(x):
    return 1.0 / (1.0 + jnp.exp(-x))


def _block_ones(width):
    i = np.arange(width) // HEAD_DIM
    return jnp.asarray((i[:, None] == i[None, :]).astype(np.float32), dtype=BF16)


def _row_tile(seq):
    return 512 if seq % 512 == 0 else 256


def _full(shape):
    return pl.BlockSpec(shape, lambda *_: (0,) * len(shape))


def _adaln_kernel(c_ref, w_ref, b_ref, o_ref):
    c = c_ref[...]
    s = c * _sigmoid(c)
    shi, slo = _split2(s)
    whi, wlo = _split2(w_ref[0])
    o_ref[0] = _dot(shi, whi) + _dot(slo, whi) + _dot(shi, wlo) + b_ref[0]


def _adaln(cond, ada_w, ada_b):
    nl, d, n6 = ada_w.shape
    tn = 1536
    rows = cond.shape[0]
    return pl.pallas_call(
        _adaln_kernel,
        grid=(nl, n6 // tn),
        in_specs=[pl.BlockSpec((rows, d), lambda l, j: (0, 0)),
                  pl.BlockSpec((1, d, tn), lambda l, j: (l, 0, j)),
                  pl.BlockSpec((1, 1, tn), lambda l, j: (l, 0, j))],
        out_specs=pl.BlockSpec((1, rows, tn), lambda l, j: (l, 0, j)),
        out_shape=jax.ShapeDtypeStruct((nl, rows, n6), F32),
        compiler_params=_cp("parallel", "parallel"),
        name="adaln",
    )(cond, ada_w, ada_b.reshape(nl, 1, n6))


def _head_sums(zz, ones128):
    zz = _bf(zz)
    parts = [_dot(zz[:, i:i + 128], ones128) for i in range(0, zz.shape[1], 128)]
    return jnp.concatenate(parts, axis=1) if len(parts) > 1 else parts[0]


def _head_norm(z, gain, ones128):
    ms = _head_sums(z * z, ones128) * (1.0 / HEAD_DIM)
    return z * lax.rsqrt(ms + NORM_EPS) * gain


def _rope(z, cos, sin_signed):
    w = z.shape[1]
    lane = lax.broadcasted_iota(jnp.int32, z.shape, 1)
    first = (lane % HEAD_DIM) < (HEAD_DIM // 2)
    swapped = jnp.where(first, pltpu.roll(z, w - HEAD_DIM // 2, 1), pltpu.roll(z, HEAD_DIM // 2, 1))
    return z * cos + swapped * sin_signed


def _proj_kernel(*refs, rope, split_rw):
    refs = list(refs)
    x_ref, mod_ref, n1_ref, w_ref, gqa_ref, gka_ref, gqc_ref, gkc_ref, ones_ref = refs[:9]
    qa_ref, ka_ref, va_ref, urw_ref, qc_ref, kc_ref, vc_ref = refs[-7:]
    extra = refs[9:-7]
    if rope:
        cos_ref, sin_ref = extra[:2]
    sh1 = mod_ref[0, 0:1, :]
    sc1 = mod_ref[0, 1:2, :]
    o0 = 0
    o1 = NA_W
    o2 = 2 * NA_W
    o3 = 3 * NA_W
    o4 = o3 + RW_IN_W
    o5 = o4 + SWA_W
    o6 = o5 + SWA_KV_W
    ones = ones_ref[0:128, 0:128]
    ones_kv = ones
    subs = [slice(i, i + SUB_ROWS) for i in range(0, x_ref.shape[0], SUB_ROWS)]
    hs, us = [], []
    for sl in subs:
        x = x_ref[sl, :]
        ms = jnp.mean(x * x, axis=-1, keepdims=True)
        hs.append(x * lax.rsqrt(ms + NORM_EPS) * n1_ref[...] * (1.0 + sc1) + sh1)
    h_hi = [_bf(h) for h in hs]
    us = [_dot(hh, w_ref[...]) for hh in h_hi]
    for sl, h, hh, u in zip(subs, hs, h_hi, us):
        qa = _head_norm(u[:, o0:o1], gqa_ref[...], ones)
        ka = _head_norm(u[:, o1:o2], gka_ref[...], ones)
        qc = _head_norm(u[:, o4:o5], gqc_ref[...], ones)
        kc = _head_norm(u[:, o5:o6], gkc_ref[...], ones_kv)
        if rope:
            qc = _rope(qc, cos_ref[sl, :], sin_ref[sl, :])
            kc = _rope(kc, cos_ref[sl, 0:SWA_KV_W], sin_ref[sl, 0:SWA_KV_W])
        qa_ref[sl, :] = qa.astype(qa_ref.dtype)
        ka_ref[sl, :] = ka.astype(ka_ref.dtype)
        va_ref[sl, :] = u[:, o2:o3].astype(va_ref.dtype)
        urw = u[:, o3:o4]
        if split_rw:
            h_lo = _bf(h - hh.astype(F32))
            urw = urw + _dot(h_lo, w_ref[:, o3:o4]) + _dot(hh, extra[-1][...])
        urw_ref[sl, :] = urw
        qc_ref[sl, :] = qc.astype(qc_ref.dtype)
        kc_ref[sl, :] = kc.astype(kc_ref.dtype)
        vc_ref[sl, :] = u[:, o6:].astype(vc_ref.dtype)


def _proj(x, mod, n1, w_in_bf, gains, ones384, rope_tabs, seq, per_request_mod, qkv_dtype, w_rw_lo=None):
    tokens, d = x.shape
    tm = _row_tile(seq)
    tiles_per_req = seq // tm
    in_w = w_in_bf.shape[1]
    rope = rope_tabs is not None
    mod_map = (lambda i: (i // tiles_per_req, 0, 0)) if per_request_mod else (lambda i: (0, 0, 0))
    row = lambda w: pl.BlockSpec((tm, w), lambda i: (i, 0))
    in_specs = [row(d), pl.BlockSpec((1, 6, d), mod_map), _full((1, d)), _full((d, in_w)),
                _full((1, NA_W)), _full((1, NA_W)), _full((1, SWA_W)), _full((1, SWA_KV_W)),
                _full((NA_W, NA_W))]
    args = [x, mod, n1, w_in_bf, *gains, ones384]
    if rope:
        tab = pl.BlockSpec((tm, SWA_W), lambda i: (i % tiles_per_req, 0))
        in_specs += [tab, tab]
        args += list(rope_tabs)
    if w_rw_lo is not None:
        in_specs.append(_full((d, RW_IN_W)))
        args.append(w_rw_lo)
    widths = [NA_W, NA_W, NA_W, RW_IN_W, SWA_W, SWA_KV_W, SWA_KV_W]
    dtypes = [qkv_dtype, qkv_dtype, qkv_dtype, F32, qkv_dtype, qkv_dtype, qkv_dtype]
    return pl.pallas_call(
        functools.partial(_proj_kernel, rope=rope, split_rw=w_rw_lo is not None),
        grid=(tokens // tm,),
        in_specs=in_specs,
        out_specs=[row(w) for w in widths],
        out_shape=[jax.ShapeDtypeStruct((tokens, w), dt) for w, dt in zip(widths, dtypes)],
        compiler_params=_cp("parallel"),
        name="proj",
    )(*args)


def _half_masks(width=2 * HEAD_DIM):
    lane = lax.broadcasted_iota(jnp.int32, (1, width), 1)
    return lane < HEAD_DIM, lane >= HEAD_DIM


def _swap_halves(z):
    return pltpu.roll(z, HEAD_DIM, 1)


def _ctx_attn_kernel(sink_ref, qa_ref, ka_ref, va_ref, qc_ref, kc_ref, vc_ref, oa_ref, oc_ref):
    scale = HEAD_DIM ** -0.5
    m0, m1 = _half_masks()
    masks = (m0, m1)
    for pair in range(NA_HEADS // 2):
        sl = slice(pair * 128, (pair + 1) * 128)
        qp = qa_ref[:, sl].astype(F32) * scale
        kp = _bf(ka_ref[:, sl])
        vp = _bf(va_ref[:, sl])
        outs = []
        for half in range(2):
            qm = _bf(jnp.where(masks[half], qp, 0.0))
            s = _dot_nt(qm, kp)
            m = jnp.max(s, axis=-1, keepdims=True)
            e = jnp.exp(s - m)
            l = jnp.sum(e, axis=-1, keepdims=True)
            outs.append(_dot(_bf(e), vp) / l)
        oa_ref[:, sl] = jnp.where(m0, outs[0], outs[1])
    kc = _bf(kc_ref[...])
    vc = _bf(vc_ref[...])
    group = SWA_HEADS // SWA_KV_HEADS
    for pair in range(SWA_HEADS // 2):
        sl = slice(pair * 128, (pair + 1) * 128)
        qp = qc_ref[:, sl].astype(F32) * scale
        outs = []
        for half in range(2):
            h = 2 * pair + half
            g = h // group
            qh = qp if g == half else _swap_halves(qp)
            qm = _bf(jnp.where(masks[g], qh, 0.0))
            s = _dot_nt(qm, kc)
            sk = sink_ref[h]
            m = jnp.maximum(jnp.max(s, axis=-1, keepdims=True), sk)
            e = jnp.exp(s - m)
            l = jnp.sum(e, axis=-1, keepdims=True) + jnp.exp(sk - m)
            o = _dot(_bf(e), vc) / l
            outs.append(o if g == half else _swap_halves(o))
        oc_ref[:, sl] = jnp.where(m0, outs[0], outs[1])


def _ctx_attn(sink, qa, ka, va, qc, kc, vc, seq):
    tokens = qa.shape[0]
    blk = lambda w: pl.BlockSpec((seq, w), lambda b: (b, 0))
    return pl.pallas_call(
        _ctx_attn_kernel,
        grid=(tokens // seq,),
        in_specs=[pl.BlockSpec(memory_space=pltpu.SMEM), blk(NA_W), blk(NA_W), blk(NA_W), blk(SWA_W),
                  blk(SWA_KV_W), blk(SWA_KV_W)],
        out_specs=[blk(NA_W), blk(SWA_W)],
        out_shape=[jax.ShapeDtypeStruct((tokens, NA_W), F32), jax.ShapeDtypeStruct((tokens, SWA_W), F32)],
        compiler_params=_cp("parallel"),
        name="ctx_attn",
    )(sink, qa, ka, va, qc, kc, vc)


def _na_bias_kernel(rpb_ref, o_ref):
    h = pl.program_id(0)
    nrow = 2 * NA_KH - 1
    ncol = 2 * NA_KW - 1
    width = NA_KH * GRID_W
    shape = (GRID_W, width)
    lane = lax.broadcasted_iota(jnp.int32, shape, 1)
    qc = lax.broadcasted_iota(jnp.int32, shape, 0)
    kc = lane % GRID_W
    c_start = jnp.clip(qc - NA_KW // 2, 0, GRID_W - NA_KW)
    ok = (kc >= c_start) & (kc < c_start + NA_KW)
    d_col = jnp.clip(kc - qc, 1 - NA_KW, NA_KW - 1) + NA_KW - 1
    key_row = lax.broadcasted_iota(jnp.int32, (1, width), 1) // GRID_W

    def case_body(case, carry):
        acc = jnp.zeros(shape, F32)
        for dc in range(ncol):
            val = jnp.zeros((1, width), F32)
            for i in range(NA_KH):
                val = jnp.where(key_row == i, rpb_ref[(h * nrow + case + i) * ncol + dc], val)
            acc = jnp.where(d_col == dc, val, acc)
        o_ref[0, pl.ds(case, 1)] = jnp.where(ok, acc, NEG_INF)[None]
        return carry

    lax.fori_loop(0, NA_KH, case_body, 0)


def _na_bias_table(rpb):
    nh = rpb.shape[0]
    return pl.pallas_call(
        _na_bias_kernel,
        grid=(nh,),
        in_specs=[pl.BlockSpec(memory_space=pltpu.SMEM)],
        out_specs=pl.BlockSpec((1, NA_KH, GRID_W, NA_KH * GRID_W), lambda h: (h // 2, 0, h % 2, 0)),
        out_shape=jax.ShapeDtypeStruct((nh // 2, NA_KH, 2 * GRID_W, NA_KH * GRID_W), F32),
        compiler_params=_cp("parallel"),
        name="na_bias",
    )(rpb.reshape(-1))


def _na_kernel(q_ref, k_ref, v_ref, kx_ref, vx_ref, bias_ref, o_ref, *, rows):
    scale = HEAD_DIM ** -0.5
    m0, m1 = _half_masks()
    kx = _bf(kx_ref[...])
    vx = _bf(vx_ref[...])

    def body(it, carry):
        us = range(NA_UNROLL)
        r = [it * NA_UNROLL + u for u in us]
        rs = [jnp.clip(r[u] - NA_KH // 2, 0, rows - NA_KH) for u in us]
        case = [rs[u] - r[u] + NA_KH - 1 for u in us]
        q0 = [pl.multiple_of(r[u] * GRID_W, GRID_W) for u in us]
        k0 = [pl.multiple_of(rs[u] * GRID_W, GRID_W) for u in us]
        qp = [q_ref[pl.ds(q0[u], GRID_W), :].astype(F32) * scale for u in us]
        kw = [_bf(k_ref[pl.ds(k0[u], NA_KH * GRID_W), :]) for u in us]
        vw = [_bf(v_ref[pl.ds(k0[u], NA_KH * GRID_W), :]) for u in us]
        q2 = [_bf(jnp.concatenate([jnp.where(m0, qp[u], 0.0), jnp.where(m1, qp[u], 0.0)], axis=0)) for u in us]
        s = [jnp.concatenate([_dot_nt(q2[u], kw[u]) + bias_ref[0, pl.ds(case[u], 1)][0], _dot_nt(q2[u], kx)],
                             axis=1) for u in us]
        m = [jnp.max(s[u], axis=-1, keepdims=True) for u in us]
        e = [jnp.exp(s[u] - m[u]) for u in us]
        l = [jnp.sum(e[u], axis=-1, keepdims=True) for u in us]
        o = [_dot(_bf(e[u]), jnp.concatenate([vw[u], vx], axis=0)) / l[u] for u in us]
        for u in us:
            o_ref[pl.ds(q0[u], GRID_W), :] = jnp.where(m0, o[u][0:GRID_W], o[u][GRID_W:])
        return carry

    lax.fori_loop(0, rows // NA_UNROLL, body, 0)


def _na_attn(q, k, v, kx, vx, bias, seq, past):
    tokens = q.shape[0]
    nb = tokens // seq
    rows = seq // GRID_W
    blk = pl.BlockSpec((seq, 128), lambda b, p: (b, p))
    cblk = pl.BlockSpec((past, 128), lambda b, p: (b, p))
    return pl.pallas_call(
        functools.partial(_na_kernel, rows=rows),
        grid=(nb, NA_HEADS // 2),
        in_specs=[blk, blk, blk, cblk, cblk,
                  pl.BlockSpec((1, NA_KH, 2 * GRID_W, NA_KH * GRID_W), lambda b, p: (p, 0, 0, 0))],
        out_specs=blk,
        out_shape=jax.ShapeDtypeStruct((tokens, NA_W), F32),
        compiler_params=_cp("parallel", "parallel"),
        name="na_attn",
    )(q, k, v, kx, vx, bias)


def _swa_kernel(sink_ref, q_ref, k_ref, v_ref, kx_ref, vx_ref, o_ref, *, seq):
    scale = HEAD_DIM ** -0.5
    blk = SWA_WINDOW
    m0, m1 = _half_masks()
    masks = (m0, m1)
    kx = _bf(kx_ref[...])
    vx = _bf(vx_ref[...])
    group = SWA_HEADS // SWA_KV_HEADS

    sk = []
    for g in range(SWA_KV_HEADS):
        sk.append(jnp.concatenate([jnp.full((blk, 1), sink_ref[h], F32) for h in range(g * group, (g + 1) * group)],
                                  axis=0))

    def body(it, carry):
        us = range(SWA_UNROLL)
        cs = [(u, g) for u in us for g in range(SWA_KV_HEADS)]
        nb = [it * SWA_UNROLL + u for u in us]
        ks = [pl.multiple_of(jnp.clip((nb[u] - 1) * blk, 0, seq - 3 * blk), blk) for u in us]
        q0 = [pl.multiple_of(nb[u] * blk, blk) for u in us]
        kw = [_bf(k_ref[pl.ds(ks[u], 3 * blk), :]) for u in us]
        vw = [_bf(v_ref[pl.ds(ks[u], 3 * blk), :]) for u in us]
        ok = []
        for u in us:
            qpos = q0[u] + lax.broadcasted_iota(jnp.int32, (group * blk, 1), 0) % blk
            kpos = ks[u] + lax.broadcasted_iota(jnp.int32, (1, 3 * blk), 1)
            ok.append(jnp.abs(qpos - kpos) <= SWA_WINDOW)
        qg = {}
        for u in us:
            pairs = [q_ref[pl.ds(q0[u], blk), p * 128:(p + 1) * 128].astype(F32) * scale
                     for p in range(SWA_HEADS // 2)]
            for g in range(SWA_KV_HEADS):
                qs = []
                for h in range(g * group, (g + 1) * group):
                    qh = pairs[h // 2] if h % 2 == g else _swap_halves(pairs[h // 2])
                    qs.append(jnp.where(masks[g], qh, 0.0))
                qg[u, g] = _bf(jnp.concatenate(qs, axis=0))
        sw = {c: jnp.where(ok[c[0]], _dot_nt(qg[c], kw[c[0]]), NEG_INF) for c in cs}
        sx = {c: _dot_nt(qg[c], kx) for c in cs}
        m = {c: jnp.maximum(jnp.maximum(jnp.max(sw[c], axis=-1, keepdims=True),
                                        jnp.max(sx[c], axis=-1, keepdims=True)), sk[c[1]]) for c in cs}
        ew = {c: jnp.exp(sw[c] - m[c]) for c in cs}
        ex = {c: jnp.exp(sx[c] - m[c]) for c in cs}
        l = {c: jnp.sum(ew[c], axis=-1, keepdims=True) + jnp.sum(ex[c], axis=-1, keepdims=True)
             + jnp.exp(sk[c[1]] - m[c]) for c in cs}
        o = {c: (_dot(_bf(ew[c]), vw[c[0]]) + _dot(_bf(ex[c]), vx)) / l[c] for c in cs}
        for u in us:
            head_out = []
            for g in range(SWA_KV_HEADS):
                for i in range(group):
                    h = g * group + i
                    oh = o[u, g][i * blk:(i + 1) * blk]
                    head_out.append(oh if h % 2 == g else _swap_halves(oh))
            for p in range(SWA_HEADS // 2):
                o_ref[pl.ds(q0[u], blk), p * 128:(p + 1) * 128] = jnp.where(m0, head_out[2 * p], head_out[2 * p + 1])
        return carry

    lax.fori_loop(0, seq // (blk * SWA_UNROLL), body, 0)


def _swa_attn(sink, q, k, v, kx, vx, seq, past):
    tokens = q.shape[0]
    blk = lambda w: pl.BlockSpec((seq, w), lambda b: (b, 0))
    cblk = pl.BlockSpec((past, SWA_KV_W), lambda b: (b, 0))
    return pl.pallas_call(
        functools.partial(_swa_kernel, seq=seq),
        grid=(tokens // seq,),
        in_specs=[pl.BlockSpec(memory_space=pltpu.SMEM), blk(SWA_W), blk(SWA_KV_W), blk(SWA_KV_W), cblk, cblk],
        out_specs=blk(SWA_W),
        out_shape=jax.ShapeDtypeStruct((tokens, SWA_W), F32),
        compiler_params=_cp("parallel"),
        name="swa_attn",
    )(sink, q, k, v, kx, vx)


def _rw_pre_kernel(u_ref, up_ref, un_ref, mu_ref, w0_ref, w2_ref, a0_ref, a2_ref, g2_ref, kk_ref_, ka_ref_,
                   rk_ref, ones_ref, r_o, kk_o, v_o, ld_o, kka_o, kd_o, g_o, bonus_o, *, tiles_per_req):
    i = pl.program_id(0)
    u = u_ref[...]
    tm = u.shape[0]
    rowi = lax.broadcasted_iota(jnp.int32, (tm, 1), 0)
    first = (i % tiles_per_req) == 0
    last = (i % tiles_per_req) == tiles_per_req - 1
    prev_row = jnp.where(first, 0.0, up_ref[7:8, :])
    next_row = jnp.where(last, 0.0, un_ref[0:1, :])
    prev = jnp.where(rowi == 0, prev_row, pltpu.roll(u, 1, 0))
    nxt = jnp.where(rowi == tm - 1, next_row, pltpu.roll(u, tm - 1, 0))
    us = u + mu_ref[0:1, :] * (prev - u) + mu_ref[1:2, :] * (nxt - u)
    r = us[:, 0:RW_W]
    k = us[:, RW_W:2 * RW_W]
    v = us[:, 2 * RW_W:3 * RW_W]
    wl = us[:, 3 * RW_W:3 * RW_W + 128]
    al = us[:, 3 * RW_W + 128:3 * RW_W + 256]
    gl = us[:, 3 * RW_W + 256:3 * RW_W + 384]
    z = -(w0_ref[...] + _dot(_bf(jnp.tanh(wl)), w2_ref[...]))
    softplus = jnp.maximum(z, 0.0) + jnp.log(1.0 + jnp.exp(-jnp.abs(z)))
    w = -softplus - 0.5
    ld = -jnp.exp(w)
    a = _sigmoid(a0_ref[...] + _dot(_bf(al), a2_ref[...]))
    g = _dot(_bf(_sigmoid(gl)), g2_ref[...])
    ones = ones_ref[...]
    kkr = k * kk_ref_[...]
    kk = kkr * lax.rsqrt(jnp.maximum(_dot2(kkr * kkr, ones), 1e-24))
    k_a = ka_ref_[...]
    kd_f = k * (1.0 + (a[:, 0:RW_W] - 1.0) * k_a)
    kd_b = k * (1.0 + (a[:, RW_W:] - 1.0) * k_a)
    r_o[...] = r
    kk_o[...] = kk
    v_o[...] = v
    ld_o[...] = ld
    kka_o[:, 0:RW_W] = kk * a[:, 0:RW_W]
    kka_o[:, RW_W:] = kk * a[:, RW_W:]
    kd_o[:, 0:RW_W] = kd_f
    kd_o[:, RW_W:] = kd_b
    g_o[...] = g
    bonus_o[...] = _dot2(r * (kd_f + kd_b) * rk_ref[...], ones) * v


def _rw_pre(urw, p, seq):
    tokens = urw.shape[0]
    tm = _row_tile(seq)
    tpr = seq // tm
    nt = tokens // tm
    r8 = tm // 8
    row = lambda w: pl.BlockSpec((tm, w), lambda i: (i, 0))
    in_specs = [row(RW_IN_W),
                pl.BlockSpec((8, RW_IN_W), lambda i: (jnp.maximum(i * r8 - 1, 0), 0)),
                pl.BlockSpec((8, RW_IN_W), lambda i: (jnp.minimum((i + 1) * r8, nt * r8 - 1), 0)),
                _full((2, RW_IN_W)), _full((1, 2 * RW_W)), _full((128, 2 * RW_W)), _full((1, 2 * RW_W)),
                _full((128, 2 * RW_W)), _full((128, RW_W)), _full((1, RW_W)), _full((1, RW_W)),
                _full((1, RW_W)), _full((RW_W, RW_W))]
    widths = [RW_W, RW_W, RW_W, 2 * RW_W, 2 * RW_W, 2 * RW_W, RW_W, RW_W]
    return pl.pallas_call(
        functools.partial(_rw_pre_kernel, tiles_per_req=tpr),
        grid=(nt,),
        in_specs=in_specs,
        out_specs=[row(w) for w in widths],
        out_shape=[jax.ShapeDtypeStruct((tokens, w), F32) for w in widths],
        compiler_params=_cp("parallel"),
        name="rw_pre",
    )(urw, urw, urw, p['mu'], p['w0'], p['w2'], p['a0'], p['a2'], p['g2'], p['k_k'], p['k_a'], p['r_k'],
      p['ones256'])


def _rw_masks():
    t = RW_CHUNK
    n = RW_HEADS * t
    tt = np.arange(t)[:, None]
    ss = (np.arange(n) % t)[None, :]
    before = np.stack([ss < tt, ss > tt])
    diag = (ss == tt)
    strict = before.astype(np.float32)
    incl = (before | diag[None]).astype(np.float32)
    eye = diag.astype(np.float32)
    ti = np.arange(t)
    tri = np.stack([ti[None, :] <= ti[:, None], ti[None, :] >= ti[:, None]]).astype(np.float32)
    hd = np.arange(n) // t
    same = (hd[:, None] == hd[None, :]).astype(np.float32)
    return (jnp.asarray(strict), jnp.asarray(incl), jnp.asarray(tri, dtype=BF16), jnp.asarray(same),
            jnp.asarray(eye))


def _rw_scan_kernel(r_ref, kk_ref, v_ref, ld_ref, kka_ref, kd_ref, s0_ref, strict_ref, incl_ref, tri_ref,
                    same_ref, eye_ref, o_ref, sfin_ref, s_scr, *, nsub):
    d = pl.program_id(1)
    c = pl.program_id(2)
    t = RW_CHUNK
    n = RW_HEADS * t

    @pl.when(c == 0)
    def _():
        s_scr[...] = s0_ref[0, 0]

    strict = strict_ref[0]
    incl = incl_ref[0]
    tri = tri_ref[0]
    eye = eye_ref[...]
    same = same_ref[...]
    same_bf = _bf(same)

    def bd(x):
        return jnp.concatenate([_bf(x)] * RW_HEADS, axis=0) * same_bf

    js = range(nsub)
    rows = [pl.ds(pl.multiple_of((j + d * (nsub - 1 - 2 * j)) * t, t), t) for j in js]
    ld = [ld_ref[rows[j], :] for j in js]
    cum = []
    for j in js:
        lhi, lmid, llo = _split3(ld[j])
        cum.append(_dot(tri, lhi) + _dot(tri, lmid) + _dot(tri, llo))
    cend = [jnp.sum(ld[j], axis=0, keepdims=True) for j in js]
    kka = [kka_ref[rows[j], :] for j in js]
    kd = [kd_ref[rows[j], :] for j in js]
    v = [v_ref[rows[j], :] for j in js]
    at = [-kk_ref[rows[j], :] * jnp.exp(cum[j] - ld[j]) for j in js]
    rt = [r_ref[rows[j], :] * jnp.exp(cum[j]) for j in js]
    e_inv = [jnp.exp(-cum[j]) for j in js]
    aa = [_dot_nt(_bf(jnp.concatenate([at[j], rt[j]], axis=0)),
                  jnp.concatenate([bd(kka[j] * e_inv[j]), bd(kd[j] * e_inv[j])], axis=0)) for j in js]
    a_ab = [aa[j][0:t, 0:n] * strict for j in js]
    x = [eye + a_ab[j] for j in js]
    pw = a_ab
    for _ in range(RW_CHUNK.bit_length() - 3):
        pw = [_dot(_bf(pw[j]), bd(pw[j])) for j in js]
        x = [x[j] + _dot(_bf(x[j]), bd(pw[j])) for j in js]
    xs = [_split2(x[j]) for j in js]
    sa = [_split2(a_ab[j]) for j in js]
    ax = [_dot(sa[j][0], bd(xs[j][0])) + _dot(sa[j][1], bd(xs[j][0])) + _dot(sa[j][0], bd(xs[j][1])) for j in js]
    x = [x[j] + _dot(xs[j][0], bd(eye - x[j] + ax[j])) for j in js]
    v_bd = [bd(v[j]) for j in js]
    wv = [_dot(_bf(aa[j][0:t, n:] * strict), v_bd[j]) for j in js]
    mu = [_dot(_bf(x[j]), jnp.concatenate([bd(at[j]), bd(wv[j])], axis=1)) for j in js]
    m1 = [mu[j][:, 0:n] for j in js]
    u0 = [mu[j][:, n:] for j in js]
    e_end = [jnp.exp(cend[j] - cum[j]) for j in js]
    bend = [_bf(kka[j] * e_end[j]) for j in js]
    g = [_bf(_dot_tn(_bf(m1[j]), bend[j]) * same) for j in js]
    cst = [_dot_tn(_bf(jnp.concatenate([u0[j], v[j]], axis=0)),
                   jnp.concatenate([bend[j], _bf(kd[j] * e_end[j])], axis=0)) * same for j in js]
    qo = [_dot(_bf(aa[j][t:, 0:n] * incl), jnp.concatenate([bd(m1[j]), bd(u0[j])], axis=1)) for j in js]
    q = [_bf(rt[j] + qo[j][:, 0:n]) for j in js]
    o0 = [qo[j][:, n:] + _dot(_bf(aa[j][t:, n:] * incl), v_bd[j]) for j in js]

    s = s_scr[...]
    for j in js:
        s_bf = _bf(s)
        o_ref[0, rows[j], :] = _dot_nt(q[j], s_bf) + o0[j]
        s = s * jnp.exp(cend[j]) + _dot(s_bf, g[j]) + cst[j]
    s_scr[...] = s

    @pl.when(c == pl.num_programs(2) - 1)
    def _():
        sfin_ref[0, 0] = s


def _rw_scan(r, kk, v, ld, kka, kd, s0_bd, consts, seq):
    tokens = r.shape[0]
    nreq = tokens // seq
    tb = min(seq, 512)
    nblk = seq // tb
    nsub = tb // RW_CHUNK
    n = RW_HEADS * RW_CHUNK
    cc = lambda d, c: c + d * (nblk - 1 - 2 * c)
    shared = pl.BlockSpec((tb, RW_W), lambda b, d, c: (b * nblk + cc(d, c), 0))
    dirw = pl.BlockSpec((tb, RW_W), lambda b, d, c: (b * nblk + cc(d, c), d))
    strict, incl, tri, same, eye = consts
    return pl.pallas_call(
        functools.partial(_rw_scan_kernel, nsub=nsub),
        grid=(nreq, 2, nblk),
        in_specs=[shared, shared, shared, dirw, dirw, dirw,
                  pl.BlockSpec((1, 1, n, n), lambda b, d, c: (b, d, 0, 0)),
                  pl.BlockSpec((1, RW_CHUNK, n), lambda b, d, c: (d, 0, 0)),
                  pl.BlockSpec((1, RW_CHUNK, n), lambda b, d, c: (d, 0, 0)),
                  pl.BlockSpec((1, RW_CHUNK, RW_CHUNK), lambda b, d, c: (d, 0, 0)),
                  _full((n, n)), _full((RW_CHUNK, n))],
        out_specs=[pl.BlockSpec((1, tb, RW_W), lambda b, d, c: (d, b * nblk + cc(d, c), 0)),
                   pl.BlockSpec((1, 1, n, n), lambda b, d, c: (b, d, 0, 0))],
        out_shape=[jax.ShapeDtypeStruct((2, tokens, RW_W), F32), jax.ShapeDtypeStruct((nreq, 2, n, n), F32)],
        scratch_shapes=[pltpu.VMEM((n, n), F32)],
        compiler_params=_cp("parallel", "parallel", "arbitrary"),
        name="rw_scan",
    )(r, kk, v, ld, kka, kd, s0_bd, strict, incl, tri, same, eye)


def _state_to_blockdiag(s):
    b = s.shape[0]
    eye = jnp.eye(RW_HEADS, dtype=s.dtype)
    out = s[:, :, :, :, None, :] * eye[None, None, :, None, :, None]
    return out.reshape(b, 2, RW_W, RW_W)


def _blockdiag_to_state(sbd):
    b = sbd.shape[0]
    s = sbd.reshape(b, 2, RW_HEADS, HEAD_DIM, RW_HEADS, HEAD_DIM)
    idx = jnp.arange(RW_HEADS)
    return jnp.transpose(s[:, :, idx, :, idx, :], (1, 2, 0, 3, 4))


def _finish_kernel(x_ref, oa_ref, oc_ref, o2_ref, bonus_ref, g_ref, mod_ref, wout_ref, lng_ref, lnb_ref,
                   n2_ref, wr_hi_ref, wr_lo_ref, ones_ref, x1_ref, h2_ref, aff_ref):
    ones = ones_ref[0:128, 0:128]
    g1 = mod_ref[0, 2:3, :]
    sh2 = mod_ref[0, 3:4, :]
    sc2 = mod_ref[0, 4:5, :]
    subs = [slice(i, i + SUB_ROWS) for i in range(0, x_ref.shape[0], SUB_ROWS)]
    mixins = []
    for sl in subs:
        y = o2_ref[0, sl, :] + o2_ref[1, sl, :]
        mu = _head_sums(y, ones) * (1.0 / HEAD_DIM)
        yc = y - mu
        var = _head_sums(yc * yc, ones) * (1.0 / HEAD_DIM)
        yn = yc * lax.rsqrt(var + GN_EPS) * lng_ref[...] + lnb_ref[...]
        ob = (yn + bonus_ref[sl, :]) * g_ref[sl, :]
        mixins.append(jnp.concatenate([_bf(oa_ref[sl, :]), _bf(ob), _bf(oc_ref[sl, :])], axis=1))
    mixes = [_dot(mixin, wout_ref[...]) for mixin in mixins]
    for sl, mix in zip(subs, mixes):
        x1 = x_ref[sl, :] + g1 * mix
        ms = jnp.mean(x1 * x1, axis=-1, keepdims=True)
        h2 = x1 * lax.rsqrt(ms + NORM_EPS) * n2_ref[...] * (1.0 + sc2) + sh2
        x1_ref[sl, :] = x1
        h2_ref[sl, :] = _bf(h2)
        hhi, hlo = _split2(h2)
        logits = _dot(hhi, wr_hi_ref[...]) + _dot(hlo, wr_hi_ref[...]) + _dot(hhi, wr_lo_ref[...])
        m = jnp.max(logits, axis=-1, keepdims=True)
        e = jnp.exp(logits - m)
        aff_ref[sl, :] = e / jnp.sum(e, axis=-1, keepdims=True)


def _finish(x, oa, oc, o2, bonus, g, mod, p, seq, per_request_mod):
    tokens, d = x.shape
    tm = _row_tile(seq)
    tpr = seq // tm
    mod_map = (lambda i: (i // tpr, 0, 0)) if per_request_mod else (lambda i: (0, 0, 0))
    row = lambda w: pl.BlockSpec((tm, w), lambda i: (i, 0))
    return pl.pallas_call(
        _finish_kernel,
        grid=(tokens // tm,),
        in_specs=[row(d), row(NA_W), row(SWA_W), pl.BlockSpec((2, tm, RW_W), lambda i: (0, i, 0)), row(RW_W),
                  row(RW_W), pl.BlockSpec((1, 6, d), mod_map), _full((d, d)), _full((1, RW_W)),
                  _full((1, RW_W)), _full((1, d)), _full((d, N_EXPERTS)), _full((d, N_EXPERTS)),
                  _full((RW_W, RW_W))],
        out_specs=[row(d), row(d), row(N_EXPERTS)],
        out_shape=[jax.ShapeDtypeStruct((tokens, d), F32), jax.ShapeDtypeStruct((tokens, d), BF16),
                   jax.ShapeDtypeStruct((tokens, N_EXPERTS), F32)],
        compiler_params=_cp("parallel"),
        name="finish",
    )(x, oa, oc, o2, bonus, g, mod, p['w_out'], p['ln_g'], p['ln_b'], p['n2'], p['wr_hi'], p['wr_lo'],
      p['ones256'])


def _topk_kernel(aff_ref, tri_ref, eye_ref, place_ref, slot_ref, slotrow_ref, gfull_ref, ends_ref, *, cap, group,
                 seq, tb):
    aff = aff_ref[...]
    bits = lax.bitcast_convert_type(aff, jnp.int32)
    capf = jnp.float32(cap)
    eye = eye_ref[...]
    ghi, gmid, glo = _split3(aff)
    aff_t = _dot_nt(eye, ghi) + _dot_nt(eye, gmid) + _dot_nt(eye, glo)
    bits_t = lax.bitcast_convert_type(aff_t, jnp.int32)
    rs = range(group)

    def bis(_, carry):
        los, his = carry
        nlo, nhi = [], []
        for r in rs:
            mid = los[r] + ((his[r] - los[r] + 1) >> 1)
            cnt = jnp.sum(jnp.where(bits_t[:, r * seq:(r + 1) * seq] >= mid, 1.0, 0.0), axis=1, keepdims=True)
            ge = cnt >= capf
            nlo.append(jnp.where(ge, mid, los[r]))
            nhi.append(jnp.where(ge, his[r], mid - 1))
        return tuple(nlo), tuple(nhi)

    lo0 = tuple(jnp.zeros((N_EXPERTS, 1), jnp.int32) for _ in rs)
    hi0 = tuple(jnp.full((N_EXPERTS, 1), 0x7F7FFFFF, jnp.int32) for _ in rs)
    thr_cols, _ = lax.fori_loop(0, 31, bis, (lo0, hi0))
    ri = lax.broadcasted_iota(jnp.int32, (N_EXPERTS, N_EXPERTS), 0)
    ci = lax.broadcasted_iota(jnp.int32, (N_EXPERTS, N_EXPERTS), 1)
    tri = tri_ref[...]
    for r in rs:
        thr = jnp.sum(jnp.where(ri == ci, thr_cols[r], 0), axis=0, keepdims=True)
        rows = slice(r * seq, (r + 1) * seq)
        gt = jnp.where(bits[rows] > thr, 1.0, 0.0)
        eq = jnp.where(bits[rows] == thr, 1.0, 0.0)
        need = capf - jnp.sum(gt, axis=0, keepdims=True)
        offset = float(r * cap)
        carry_g = jnp.zeros((1, N_EXPERTS), F32)
        carry_e = jnp.zeros((1, N_EXPERTS), F32)
        for blk in range(seq // tb):
            sl = slice(blk * tb, (blk + 1) * tb)
            out = slice(r * seq + blk * tb, r * seq + (blk + 1) * tb)
            pg = _dot(tri, _bf(gt[sl])) + carry_g
            pe = _dot(tri, _bf(eq[sl])) + carry_e
            carry_g = pg[tb - 1:tb, :]
            carry_e = pe[tb - 1:tb, :]
            sel = gt[sl] + eq[sl] * jnp.where(pe <= need, 1.0, 0.0)
            slot = jnp.where(sel > 0.5, pg + jnp.minimum(pe, need) - 1.0 + offset, -1.0)
            slot_ref[out, :] = slot
            ends_ref[r, blk:blk + 1, :] = carry_g + jnp.minimum(carry_e, need) + offset
            shi, slo = _split2(slot)
            slotrow_ref[0, :, 0, out] = _dot_nt(eye, shi) + _dot_nt(eye, slo)
            gfull_ref[out, :] = _bf(_dot(ghi[out], place_ref[0]) + _dot(gmid[out], place_ref[1])
                                    + _dot(glo[out], place_ref[2]))


def _topk(aff, seq, group):
    tokens = aff.shape[0]
    nreq = tokens // seq
    cap = EC_CAPACITY * seq // N_EXPERTS
    tb = min(seq, 512)
    ti = np.arange(tb)
    tri = jnp.asarray((ti[None, :] <= ti[:, None]).astype(np.float32), dtype=BF16)
    eye = jnp.asarray(np.eye(N_EXPERTS, dtype=np.float32), dtype=BF16)
    place = np.zeros((3, N_EXPERTS, 128), np.float32)
    for s in range(3):
        place[s, np.arange(N_EXPERTS), s * N_EXPERTS + np.arange(N_EXPERTS)] = 1.0
    place = jnp.asarray(place, dtype=BF16)
    nblk = seq // tb
    slot, slotrow, gfull, ends = pl.pallas_call(
        functools.partial(_topk_kernel, cap=cap, group=group, seq=seq, tb=tb),
        grid=(nreq // group,),
        in_specs=[pl.BlockSpec((group * seq, N_EXPERTS), lambda b: (b, 0)), _full((tb, tb)),
                  _full((N_EXPERTS, N_EXPERTS)), _full((3, N_EXPERTS, 128))],
        out_specs=[pl.BlockSpec((group * seq, N_EXPERTS), lambda b: (b, 0)),
                   pl.BlockSpec((1, N_EXPERTS, 1, group * seq), lambda b: (b, 0, 0, 0)),
                   pl.BlockSpec((group * seq, 128), lambda b: (b, 0)),
                   pl.BlockSpec((group, nblk, N_EXPERTS), lambda b: (b, 0, 0))],
        out_shape=[jax.ShapeDtypeStruct((tokens, N_EXPERTS), F32),
                   jax.ShapeDtypeStruct((nreq // group, N_EXPERTS, 1, group * seq), F32),
                   jax.ShapeDtypeStruct((tokens, 128), BF16),
                   jax.ShapeDtypeStruct((nreq, nblk, N_EXPERTS), F32)],
        compiler_params=_cp("parallel"),
        name="topk",
    )(aff, tri, eye, place)
    ends = ends.reshape(nreq // group, group * nblk, N_EXPERTS).transpose(0, 2, 1)
    return slot, slotrow, gfull, ends.astype(jnp.int32).reshape(-1), tb


MOE_EB = 8


def _moe_dispatch_kernel(ends_ref, h_ref, slotrow_ref, gfull_ref, xe_ref, gs_ref, *, ct, nch):
    gi = pl.program_id(0)
    eb = pl.program_id(1)
    c = pl.program_id(2)
    mt = 128

    @pl.when(c == 0)
    def _():
        xe_ref[...] = jnp.zeros_like(xe_ref)
        gs_ref[...] = jnp.zeros_like(gs_ref)

    jcol = lax.broadcasted_iota(jnp.int32, (mt, 1), 0)
    starts, his, pieces = [], [], []
    for i in range(MOE_EB):
        base = (gi * N_EXPERTS + eb * MOE_EB + i) * nch
        lo = jnp.where(c == 0, 0, ends_ref[base + jnp.maximum(c - 1, 0)])
        his.append(ends_ref[base + c])
        start = pl.multiple_of(jnp.minimum((lo // 16) * 16, ct - mt), 16)
        starts.append(start)
        pieces.append(_bf(jnp.where(slotrow_ref[0, i] == (jcol + start).astype(F32), 1.0, 0.0)))
    onehot = jnp.concatenate(pieces, axis=0)
    xw = _bf(_dot(onehot, h_ref[...]))
    gw = _bf(_dot(onehot, gfull_ref[...]))
    for i in range(MOE_EB):
        rows = pl.ds(starts[i], mt)
        xe_ref[0, i, rows, :] += xw[i * mt:(i + 1) * mt]
        gs_ref[0, i, rows, :] += gw[i * mt:(i + 1) * mt]
    for i in range(MOE_EB):
        for w in range(1, ct // mt):
            wlo = starts[i] + w * mt

            @pl.when(wlo < his[i])
            def _(i=i, wlo=wlo):
                ws = pl.multiple_of(jnp.minimum(wlo, ct - mt), 16)
                slot = slotrow_ref[0, i]
                hit = (jnp.where(slot == (jcol + ws).astype(F32), 1.0, 0.0)
                       * jnp.where(slot >= wlo.astype(F32), 1.0, 0.0))
                rows = pl.ds(ws, mt)
                xe_ref[0, i, rows, :] += _bf(_dot(_bf(hit), h_ref[...]))
                gs_ref[0, i, rows, :] += _bf(_dot(_bf(hit), gfull_ref[...]))


def _moe_dispatch(ends, h2, slotrow, gfull, lg, ct, kc):
    tokens, d = h2.shape
    ngrp = tokens // lg
    nch = lg // kc
    grid_spec = pltpu.PrefetchScalarGridSpec(
        num_scalar_prefetch=1,
        grid=(ngrp, N_EXPERTS // MOE_EB, nch),
        in_specs=[pl.BlockSpec((kc, d), lambda gi, eb, c, ends: (gi * nch + c, 0)),
                  pl.BlockSpec((1, MOE_EB, 1, kc), lambda gi, eb, c, ends: (gi, eb, 0, c)),
                  pl.BlockSpec((kc, 128), lambda gi, eb, c, ends: (gi * nch + c, 0))],
        out_specs=[pl.BlockSpec((1, MOE_EB, ct, d), lambda gi, eb, c, ends: (gi, eb, 0, 0)),
                   pl.BlockSpec((1, MOE_EB, ct, 128), lambda gi, eb, c, ends: (gi, eb, 0, 0))])
    return pl.pallas_call(
        functools.partial(_moe_dispatch_kernel, ct=ct, nch=nch),
        grid_spec=grid_spec,
        out_shape=[jax.ShapeDtypeStruct((ngrp, N_EXPERTS, ct, d), BF16),
                   jax.ShapeDtypeStruct((ngrp, N_EXPERTS, ct, 128), BF16)],
        compiler_params=_cp("parallel", "parallel", "arbitrary"),
        name="moe_dispatch",
    )(ends, h2, slotrow, gfull)


def _moe_ffn_kernel(xe_ref, gs_ref, mod_ref, wg_ref, wu_ref, wd_ref, ye_ref):
    e = pl.program_id(1)
    lane = lax.broadcasted_iota(jnp.int32, (1, 128), 1)
    pick = (lane == e) | (lane == e + N_EXPERTS) | (lane == e + 2 * N_EXPERTS)
    gate = jnp.sum(jnp.where(pick, gs_ref[0, 0].astype(F32), 0.0), axis=-1, keepdims=True)
    xb = xe_ref[0, 0]
    hg = _dot(xb, wg_ref[0])
    hu = _dot(xb, wu_ref[0])
    he = _bf(hg * _sigmoid(hg) * hu)
    y = _dot(he, wd_ref[0])
    ye_ref[0, 0] = _bf(y * gate * mod_ref[0, 5:6, :])


def _moe_ffn(xe, gs, mod, wg, wu, wd, per_group_mod):
    ngrp, _, ct, d = xe.shape
    f = wg.shape[2]
    mod_map = (lambda gi, e: (gi, 0, 0)) if per_group_mod else (lambda gi, e: (0, 0, 0))
    return pl.pallas_call(
        _moe_ffn_kernel,
        grid=(ngrp, N_EXPERTS),
        in_specs=[pl.BlockSpec((1, 1, ct, d), lambda gi, e: (gi, e, 0, 0)),
                  pl.BlockSpec((1, 1, ct, 128), lambda gi, e: (gi, e, 0, 0)),
                  pl.BlockSpec((1, 6, d), mod_map),
                  pl.BlockSpec((1, d, f), lambda gi, e: (e, 0, 0)),
                  pl.BlockSpec((1, d, f), lambda gi, e: (e, 0, 0)),
                  pl.BlockSpec((1, f, d), lambda gi, e: (e, 0, 0))],
        out_specs=pl.BlockSpec((1, 1, ct, d), lambda gi, e: (gi, e, 0, 0)),
        out_shape=jax.ShapeDtypeStruct((ngrp, N_EXPERTS, ct, d), BF16),
        compiler_params=_cp("parallel", "parallel"),
        name="moe_ffn",
    )(xe, gs, mod, wg, wu, wd)


def _moe_combine_kernel(ends_ref, x1_ref, slot_ref, ye_ref, o_ref, win_scr, *, ct, nch):
    gi = pl.program_id(0)
    j = pl.program_id(1)
    mt = 128
    wide = N_EXPERTS * mt
    shi, slo = _split2(slot_ref[...])
    col_e = lax.broadcasted_iota(jnp.int32, (N_EXPERTS, wide), 1) // mt
    row_e = lax.broadcasted_iota(jnp.int32, (N_EXPERTS, wide), 0)
    expand = _bf(jnp.where(col_e == row_e, 1.0, 0.0))
    sb = _dot(shi, expand) + _dot(slo, expand)
    lane = lax.broadcasted_iota(jnp.int32, (1, mt), 1)
    starts, his, targets = [], [], []
    for e in range(N_EXPERTS):
        base = (gi * N_EXPERTS + e) * nch
        lo = jnp.where(j == 0, 0, ends_ref[base + jnp.maximum(j - 1, 0)])
        his.append(ends_ref[base + j])
        start = pl.multiple_of(jnp.minimum((lo // 16) * 16, ct - mt), 16)
        win_scr[e * mt:(e + 1) * mt, :] = ye_ref[0, e, pl.ds(start, mt), :]
        starts.append(start)
        targets.append((lane + start).astype(F32))
    onehot = _bf(jnp.where(sb == jnp.concatenate(targets, axis=1), 1.0, 0.0))
    o_ref[...] = x1_ref[...] + _dot(onehot, win_scr[...])
    for e in range(N_EXPERTS):
        for w in range(1, ct // mt):
            wlo = starts[e] + w * mt

            @pl.when(wlo < his[e])
            def _(e=e, wlo=wlo):
                ws = pl.multiple_of(jnp.minimum(wlo, ct - mt), 16)
                sbe = sb[:, e * mt:(e + 1) * mt]
                hit = jnp.where(sbe == (lane + ws).astype(F32), 1.0, 0.0) * jnp.where(sbe >= wlo.astype(F32), 1.0, 0.0)
                o_ref[...] += _dot(_bf(hit), ye_ref[0, e, pl.ds(ws, mt), :])


def _moe_combine(ends, x1, slot, ye, lg, ct, kc):
    tokens, d = x1.shape
    ngrp = tokens // lg
    nch = lg // kc
    grid_spec = pltpu.PrefetchScalarGridSpec(
        num_scalar_prefetch=1,
        grid=(ngrp, nch),
        in_specs=[pl.BlockSpec((kc, d), lambda gi, j, ends: (gi * nch + j, 0)),
                  pl.BlockSpec((kc, N_EXPERTS), lambda gi, j, ends: (gi * nch + j, 0)),
                  pl.BlockSpec((1, N_EXPERTS, ct, d), lambda gi, j, ends: (gi, 0, 0, 0),
                               pipeline_mode=pl.Buffered(1))],
        out_specs=pl.BlockSpec((kc, d), lambda gi, j, ends: (gi * nch + j, 0)),
        scratch_shapes=[pltpu.VMEM((N_EXPERTS * 128, d), BF16)])
    return pl.pallas_call(
        functools.partial(_moe_combine_kernel, ct=ct, nch=nch),
        grid_spec=grid_spec,
        out_shape=jax.ShapeDtypeStruct((tokens, d), F32),
        compiler_params=_cp("parallel", "arbitrary"),
        name="moe_combine",
    )(ends, x1, slot, ye)


def _rope_tables(seq):
    t = np.arange(seq)
    n_freq = HEAD_DIM // 4
    inv = ROPE_THETA ** (-np.arange(n_freq, dtype=np.float32) / n_freq)
    ang = np.concatenate([(t // GRID_W).astype(np.float32)[:, None] * inv,
                          (t % GRID_W).astype(np.float32)[:, None] * inv], axis=-1)
    ang = jnp.asarray(ang, dtype=F32)
    cos, sin = jnp.cos(ang), jnp.sin(ang)
    cos_t = jnp.tile(jnp.concatenate([cos, cos], axis=-1), (1, SWA_HEADS))
    sin_t = jnp.tile(jnp.concatenate([-sin, sin], axis=-1), (1, SWA_HEADS))
    return cos_t, sin_t


def _blockdiag2(w):
    z = jnp.zeros_like(w[0])
    return jnp.concatenate([jnp.concatenate([w[0], z], axis=1), jnp.concatenate([z, w[1]], axis=1)], axis=0)


def _layer_params(l, ada_w, ada_b, norm1_g, norm2_g, w_in, na_q_norm, na_k_norm, na_rpb, rw_mu, rw_w0, rw_w2,
                  rw_a0, rw_a2, rw_g2, rw_k_k, rw_k_a, rw_r_k, rw_ln_g, rw_ln_b, swa_q_norm, swa_k_norm,
                  swa_sink, w_out, w_router, w_gate, w_up, w_down):
    wr = w_router[l]
    wr_hi = wr.astype(BF16)
    w_rw = w_in[l][:, 3 * NA_W:3 * NA_W + RW_IN_W]
    return {
        'n1': norm1_g[l][None], 'n2': norm2_g[l][None], 'w_in': w_in[l].astype(BF16),
        'w_rw_lo': (w_rw - w_rw.astype(BF16).astype(F32)).astype(BF16),
        'gains': (jnp.tile(na_q_norm[l], NA_HEADS)[None], jnp.tile(na_k_norm[l], NA_HEADS)[None],
                  jnp.tile(swa_q_norm[l], SWA_HEADS)[None], jnp.tile(swa_k_norm[l], SWA_KV_HEADS)[None]),
        'bias': _na_bias_table(na_rpb[l]),
        'mu': rw_mu[l], 'w0': rw_w0[l].reshape(1, 2 * RW_W), 'w2': _blockdiag2(rw_w2[l]).astype(BF16),
        'a0': rw_a0[l].reshape(1, 2 * RW_W), 'a2': _blockdiag2(rw_a2[l]).astype(BF16),
        'g2': rw_g2[l].astype(BF16), 'k_k': rw_k_k[l][None], 'k_a': rw_k_a[l][None],
        'r_k': rw_r_k[l].reshape(1, RW_W), 'ln_g': rw_ln_g[l][None], 'ln_b': rw_ln_b[l][None],
        'sink': swa_sink[l], 'w_out': w_out[l].astype(BF16),
        'wr_hi': wr_hi, 'wr_lo': (wr - wr_hi.astype(F32)).astype(BF16),
        'wg': w_gate[l].astype(BF16), 'wu': w_up[l].astype(BF16), 'wd': w_down[l].astype(BF16),
        'ones256': _block_ones(RW_W),
    }


def _mix_and_ffn(x, mod, p, oa, oc, urw, s0_bd, scan_consts, seq, per_request_mod, group):
    r, kk, v, ld, kka, kd, g, bonus = _rw_pre(urw, p, seq)
    o2, sfin = _rw_scan(r, kk, v, ld, kka, kd, s0_bd, scan_consts, seq)
    x1, h2, aff = _finish(x, oa, oc, o2, bonus, g, mod, p, seq, per_request_mod)
    slot, slotrow, gfull, ends, kc = _topk(aff, seq, group)
    cap = EC_CAPACITY * seq // N_EXPERTS
    lg, ct = group * seq, group * cap
    xe, gs = _moe_dispatch(ends, h2, slotrow, gfull, lg, ct, kc)
    ye = _moe_ffn(xe, gs, mod, p['wg'], p['wu'], p['wd'], per_request_mod)
    return _moe_combine(ends, x1, slot, ye, lg, ct, kc), sfin


def _context_layer(x, mod, p, ones384, scan_consts, seq):
    qa, ka, va, urw, qc, kc, vc = _proj(x, mod, p['n1'], p['w_in'], p['gains'], ones384, None, seq, False, F32,
                                        w_rw_lo=p['w_rw_lo'])
    oa, oc = _ctx_attn(p['sink'], qa, ka, va, qc, kc, vc, seq)
    nreq = x.shape[0] // seq
    s0 = jnp.zeros((nreq, 2, RW_W, RW_W), F32)
    y, sfin = _mix_and_ffn(x, mod, p, oa, oc, urw, s0, scan_consts, seq, False, 8)
    return y, ka, va, kc, vc, sfin


def _latent_layer(x, mod, p, ones384, scan_consts, rope_tabs, seq, kx_na, vx_na, kx_swa, vx_swa, s0_bd, past):
    qa, ka, va, urw, qc, kc, vc = _proj(x, mod, p['n1'], p['w_in'], p['gains'], ones384, rope_tabs, seq, True,
                                        BF16)
    oa = _na_attn(qa, ka, va, kx_na, vx_na, p['bias'], seq, past)
    oc = _swa_attn(p['sink'], qc, kc, vc, kx_swa, vx_swa, seq, past)
    y, _ = _mix_and_ffn(x, mod, p, oa, oc, urw, s0_bd, scan_consts, seq, True, 1)
    return y


def _heads_first(z, nreq, seq, heads):
    return z.reshape(nreq, seq, heads, HEAD_DIM).transpose(0, 2, 1, 3)


def _tokens_first(z):
    b, h, n, dh = z.shape
    return z.transpose(0, 2, 1, 3).reshape(b * n, h * dh)


def kernel(x_prompt, x_sample, cache_na_k, cache_na_v, cache_swa_k, cache_swa_v, state_rwkv, c, c_ctx, ada_w, ada_b, norm1_g, norm2_g, w_in, na_q_norm, na_k_norm, na_rpb, rw_mu, rw_w0, rw_w2, rw_a0, rw_a2, rw_g2, rw_k_k, rw_k_a, rw_r_k, rw_ln_g, rw_ln_b, swa_q_norm, swa_k_norm, swa_sink, w_out, w_router, w_gate, w_up, w_down):
    nb, seq, d = x_prompt.shape
    db, dseq, _ = x_sample.shape
    depth = ada_w.shape[0]
    past = cache_na_k.shape[3]
    cond = jnp.concatenate([c, c_ctx[None], jnp.zeros((16 - db - 1, d), F32)], axis=0)
    mod_all = _adaln(cond, ada_w, ada_b).reshape(depth, 16, 6, d)
    ones384 = _block_ones(NA_W)
    scan_consts = _rw_masks()
    rope_tabs = _rope_tables(dseq)
    xp = x_prompt.reshape(nb * seq, d)
    xs = x_sample.reshape(db * dseq, d)
    new_ka, new_va, new_kc, new_vc, new_s = [], [], [], [], []
    for l in range(depth):
        p = _layer_params(l, ada_w, ada_b, norm1_g, norm2_g, w_in, na_q_norm, na_k_norm, na_rpb, rw_mu, rw_w0,
                          rw_w2, rw_a0, rw_a2, rw_g2, rw_k_k, rw_k_a, rw_r_k, rw_ln_g, rw_ln_b, swa_q_norm,
                          swa_k_norm, swa_sink, w_out, w_router, w_gate, w_up, w_down)
        mod_ctx = mod_all[l, db:db + 1]
        mod_lat = mod_all[l, 0:db]
        xp, ka, va, kc, vc, sfin = _context_layer(xp, mod_ctx, p, ones384, scan_consts, seq)
        new_ka.append(_heads_first(ka, nb, seq, NA_HEADS))
        new_va.append(_heads_first(va, nb, seq, NA_HEADS))
        new_kc.append(_heads_first(kc, nb, seq, SWA_KV_HEADS))
        new_vc.append(_heads_first(vc, nb, seq, SWA_KV_HEADS))
        new_s.append(_blockdiag_to_state(sfin))
        xs = _latent_layer(xs, mod_lat, p, ones384, scan_consts, rope_tabs, dseq,
                           _tokens_first(cache_na_k[:, l]), _tokens_first(cache_na_v[:, l]),
                           _tokens_first(cache_swa_k[:, l]), _tokens_first(cache_swa_v[:, l]),
                           _state_to_blockdiag(state_rwkv[:, l]), past)
    return (xp.reshape(nb, seq, d), xs.reshape(db, dseq, d), jnp.stack(new_ka, axis=1),
            jnp.stack(new_va, axis=1), jnp.stack(new_kc, axis=1), jnp.stack(new_vc, axis=1),
            jnp.stack(new_s, axis=1))
```

```python
import functools

import numpy as np
import jax
import jax.numpy as jnp
from jax import lax
from jax.experimental import pallas as pl
from jax.experimental.pallas import tpu as pltpu

F32 = jnp.float32
BF16 = jnp.bfloat16

HEAD_DIM = 64
GRID_W = 64
NA_HEADS = 6
NA_KH = 8
NA_KW = 16
RW_HEADS = 4
SWA_HEADS = 6
SWA_KV_HEADS = 2
SWA_WINDOW = 128
N_EXPERTS = 16
EC_CAPACITY = 2
ROPE_THETA = 10000.0
NORM_EPS = 1e-6
GN_EPS = 64e-5
NEG_INF = -1e30
SUB_ROWS = 256
NA_UNROLL = 8
SWA_UNROLL = 2
RW_CHUNK = 64
assert RW_CHUNK == HEAD_DIM
CTX_GROUP = 16
RW_W = RW_HEADS * HEAD_DIM
NA_W = NA_HEADS * HEAD_DIM
SWA_W = SWA_HEADS * HEAD_DIM
SWA_KV_W = SWA_KV_HEADS * HEAD_DIM
RW_IN_W = 1152
VMEM_LIMIT = 56 * 1024 * 1024


def _cp(*sem):
    return pltpu.CompilerParams(dimension_semantics=sem, vmem_limit_bytes=VMEM_LIMIT)


def _bf(x):
    return x.astype(BF16)


def _dot(a, b):
    return jnp.dot(a, b, preferred_element_type=F32)


def _dot_nt(a, b):
    return lax.dot_general(a, b, (((1,), (1,)), ((), ())), preferred_element_type=F32)


def _dot_tn(a, b):
    return lax.dot_general(a, b, (((0,), (0,)), ((), ())), preferred_element_type=F32)


def _split2(x):
    hi = x.astype(BF16)
    lo = (x - hi.astype(F32)).astype(BF16)
    return hi, lo


def _split3(x):
    hi = x.astype(BF16)
    r1 = x - hi.astype(F32)
    mid = r1.astype(BF16)
    lo = (r1 - mid.astype(F32)).astype(BF16)
    return hi, mid, lo


def _dot2(a, b_bf):
    hi, lo = _split2(a)
    return _dot(hi, b_bf) + _dot(lo, b_bf)


def _sigmoid(x):
    return 1.0 / (1.0 + jnp.exp(-x))


def _block_ones(width):
    i = np.arange(width) // HEAD_DIM
    return jnp.asarray((i[:, None] == i[None, :]).astype(np.float32), dtype=BF16)


def _row_tile(seq):
    return 512 if seq % 512 == 0 else 256


def _full(shape):
    return pl.BlockSpec(shape, lambda *_: (0,) * len(shape))


def _adaln_kernel(c_ref, w_ref, b_ref, o_ref):
    c = c_ref[...]
    s = c * _sigmoid(c)
    shi, slo = _split2(s)
    whi, wlo = _split2(w_ref[0])
    o_ref[0] = _dot(shi, whi) + _dot(slo, whi) + _dot(shi, wlo) + b_ref[0]


def _adaln(cond, ada_w, ada_b):
    nl, d, n6 = ada_w.shape
    tn = 1536
    rows = cond.shape[0]
    return pl.pallas_call(
        _adaln_kernel,
        grid=(nl, n6 // tn),
        in_specs=[pl.BlockSpec((rows, d), lambda l, j: (0, 0)),
                  pl.BlockSpec((1, d, tn), lambda l, j: (l, 0, j)),
                  pl.BlockSpec((1, 1, tn), lambda l, j: (l, 0, j))],
        out_specs=pl.BlockSpec((1, rows, tn), lambda l, j: (l, 0, j)),
        out_shape=jax.ShapeDtypeStruct((nl, rows, n6), F32),
        compiler_params=_cp("parallel", "parallel"),
        name="adaln",
    )(cond, ada_w, ada_b.reshape(nl, 1, n6))


def _head_sums(zz, ones128):
    zz = _bf(zz)
    parts = [_dot(zz[:, i:i + 128], ones128) for i in range(0, zz.shape[1], 128)]
    return jnp.concatenate(parts, axis=1) if len(parts) > 1 else parts[0]


def _head_norm(z, gain, ones128):
    ms = _head_sums(z * z, ones128) * (1.0 / HEAD_DIM)
    return z * lax.rsqrt(ms + NORM_EPS) * gain


def _rope(z, cos, sin_signed):
    w = z.shape[1]
    lane = lax.broadcasted_iota(jnp.int32, z.shape, 1)
    first = (lane % HEAD_DIM) < (HEAD_DIM // 2)
    swapped = jnp.where(first, pltpu.roll(z, w - HEAD_DIM // 2, 1), pltpu.roll(z, HEAD_DIM // 2, 1))
    return z * cos + swapped * sin_signed


def _proj_kernel(*refs, rope, split_rw):
    refs = list(refs)
    x_ref, mod_ref, n1_ref, w_ref, gqa_ref, gka_ref, gqc_ref, gkc_ref, ones_ref = refs[:9]
    qa_ref, ka_ref, va_ref, urw_ref, qc_ref, kc_ref, vc_ref = refs[-7:]
    extra = refs[9:-7]
    if rope:
        cos_ref, sin_ref = extra[:2]
    sh1 = mod_ref[0, 0:1, :]
    sc1 = mod_ref[0, 1:2, :]
    o0 = 0
    o1 = NA_W
    o2 = 2 * NA_W
    o3 = 3 * NA_W
    o4 = o3 + RW_IN_W
    o5 = o4 + SWA_W
    o6 = o5 + SWA_KV_W
    ones = ones_ref[0:128, 0:128]
    ones_kv = ones
    subs = [slice(i, i + SUB_ROWS) for i in range(0, x_ref.shape[0], SUB_ROWS)]
    hs, us = [], []
    for sl in subs:
        x = x_ref[sl, :]
        ms = jnp.mean(x * x, axis=-1, keepdims=True)
        hs.append(x * lax.rsqrt(ms + NORM_EPS) * n1_ref[...] * (1.0 + sc1) + sh1)
    h_hi = [_bf(h) for h in hs]
    us = [_dot(hh, w_ref[...]) for hh in h_hi]
    for sl, h, hh, u in zip(subs, hs, h_hi, us):
        qa = _head_norm(u[:, o0:o1], gqa_ref[...], ones)
        ka = _head_norm(u[:, o1:o2], gka_ref[...], ones)
        qc = _head_norm(u[:, o4:o5], gqc_ref[...], ones)
        kc = _head_norm(u[:, o5:o6], gkc_ref[...], ones_kv)
        if rope:
            qc = _rope(qc, cos_ref[sl, :], sin_ref[sl, :])
            kc = _rope(kc, cos_ref[sl, 0:SWA_KV_W], sin_ref[sl, 0:SWA_KV_W])
        qa_ref[sl, :] = qa.astype(qa_ref.dtype)
        ka_ref[sl, :] = ka.astype(ka_ref.dtype)
        va_ref[sl, :] = u[:, o2:o3].astype(va_ref.dtype)
        urw = u[:, o3:o4]
        if split_rw:
            h_lo = _bf(h - hh.astype(F32))
            urw = urw + _dot(h_lo, w_ref[:, o3:o4]) + _dot(hh, extra[-1][...])
        urw_ref[sl, :] = urw
        qc_ref[sl, :] = qc.astype(qc_ref.dtype)
        kc_ref[sl, :] = kc.astype(kc_ref.dtype)
        vc_ref[sl, :] = u[:, o6:].astype(vc_ref.dtype)


def _proj(x, mod, n1, w_in_bf, gains, ones384, rope_tabs, seq, per_request_mod, qkv_dtype, w_rw_lo=None):
    tokens, d = x.shape
    tm = _row_tile(seq)
    tiles_per_req = seq // tm
    in_w = w_in_bf.shape[1]
    rope = rope_tabs is not None
    mod_map = (lambda i: (i // tiles_per_req, 0, 0)) if per_request_mod else (lambda i: (0, 0, 0))
    row = lambda w: pl.BlockSpec((tm, w), lambda i: (i, 0))
    in_specs = [row(d), pl.BlockSpec((1, 6, d), mod_map), _full((1, d)), _full((d, in_w)),
                _full((1, NA_W)), _full((1, NA_W)), _full((1, SWA_W)), _full((1, SWA_KV_W)),
                _full((NA_W, NA_W))]
    args = [x, mod, n1, w_in_bf, *gains, ones384]
    if rope:
        tab = pl.BlockSpec((tm, SWA_W), lambda i: (i % tiles_per_req, 0))
        in_specs += [tab, tab]
        args += list(rope_tabs)
    if w_rw_lo is not None:
        in_specs.append(_full((d, RW_IN_W)))
        args.append(w_rw_lo)
    widths = [NA_W, NA_W, NA_W, RW_IN_W, SWA_W, SWA_KV_W, SWA_KV_W]
    dtypes = [qkv_dtype, qkv_dtype, qkv_dtype, F32, qkv_dtype, qkv_dtype, qkv_dtype]
    return pl.pallas_call(
        functools.partial(_proj_kernel, rope=rope, split_rw=w_rw_lo is not None),
        grid=(tokens // tm,),
        in_specs=in_specs,
        out_specs=[row(w) for w in widths],
        out_shape=[jax.ShapeDtypeStruct((tokens, w), dt) for w, dt in zip(widths, dtypes)],
        compiler_params=_cp("parallel"),
        name="proj",
    )(*args)


def _half_masks(width=2 * HEAD_DIM):
    lane = lax.broadcasted_iota(jnp.int32, (1, width), 1)
    return lane < HEAD_DIM, lane >= HEAD_DIM


def _swap_halves(z):
    return pltpu.roll(z, HEAD_DIM, 1)


def _ctx_attn_kernel(sink_ref, qa_ref, ka_ref, va_ref, qc_ref, kc_ref, vc_ref, oa_ref, oc_ref):
    scale = HEAD_DIM ** -0.5
    m0, m1 = _half_masks()
    masks = (m0, m1)
    for pair in range(NA_HEADS // 2):
        sl = slice(pair * 128, (pair + 1) * 128)
        qp = qa_ref[:, sl].astype(F32) * scale
        kp = _bf(ka_ref[:, sl])
        vp = _bf(va_ref[:, sl])
        outs = []
        for half in range(2):
            qm = _bf(jnp.where(masks[half], qp, 0.0))
            s = _dot_nt(qm, kp)
            m = jnp.max(s, axis=-1, keepdims=True)
            e = jnp.exp(s - m)
            l = jnp.sum(e, axis=-1, keepdims=True)
            outs.append(_dot(_bf(e), vp) / l)
        oa_ref[:, sl] = jnp.where(m0, outs[0], outs[1])
    kc = _bf(kc_ref[...])
    vc = _bf(vc_ref[...])
    group = SWA_HEADS // SWA_KV_HEADS
    for pair in range(SWA_HEADS // 2):
        sl = slice(pair * 128, (pair + 1) * 128)
        qp = qc_ref[:, sl].astype(F32) * scale
        outs = []
        for half in range(2):
            h = 2 * pair + half
            g = h // group
            qh = qp if g == half else _swap_halves(qp)
            qm = _bf(jnp.where(masks[g], qh, 0.0))
            s = _dot_nt(qm, kc)
            sk = sink_ref[h]
            m = jnp.maximum(jnp.max(s, axis=-1, keepdims=True), sk)
            e = jnp.exp(s - m)
            l = jnp.sum(e, axis=-1, keepdims=True) + jnp.exp(sk - m)
            o = _dot(_bf(e), vc) / l
            outs.append(o if g == half else _swap_halves(o))
        oc_ref[:, sl] = jnp.where(m0, outs[0], outs[1])


def _ctx_attn(sink, qa, ka, va, qc, kc, vc, seq):
    tokens = qa.shape[0]
    blk = lambda w: pl.BlockSpec((seq, w), lambda b: (b, 0))
    return pl.pallas_call(
        _ctx_attn_kernel,
        grid=(tokens // seq,),
        in_specs=[pl.BlockSpec(memory_space=pltpu.SMEM), blk(NA_W), blk(NA_W), blk(NA_W), blk(SWA_W),
                  blk(SWA_KV_W), blk(SWA_KV_W)],
        out_specs=[blk(NA_W), blk(SWA_W)],
        out_shape=[jax.ShapeDtypeStruct((tokens, NA_W), F32), jax.ShapeDtypeStruct((tokens, SWA_W), F32)],
        compiler_params=_cp("parallel"),
        name="ctx_attn",
    )(sink, qa, ka, va, qc, kc, vc)


def _na_bias_kernel(rpb_ref, o_ref):
    h = pl.program_id(0)
    nrow = 2 * NA_KH - 1
    ncol = 2 * NA_KW - 1
    width = NA_KH * GRID_W
    shape = (GRID_W, width)
    lane = lax.broadcasted_iota(jnp.int32, shape, 1)
    qc = lax.broadcasted_iota(jnp.int32, shape, 0)
    kc = lane % GRID_W
    c_start = jnp.clip(qc - NA_KW // 2, 0, GRID_W - NA_KW)
    ok = (kc >= c_start) & (kc < c_start + NA_KW)
    d_col = jnp.clip(kc - qc, 1 - NA_KW, NA_KW - 1) + NA_KW - 1
    key_row = lax.broadcasted_iota(jnp.int32, (1, width), 1) // GRID_W

    def case_body(case, carry):
        acc = jnp.zeros(shape, F32)
        for dc in range(ncol):
            val = jnp.zeros((1, width), F32)
            for i in range(NA_KH):
                val = jnp.where(key_row == i, rpb_ref[(h * nrow + case + i) * ncol + dc], val)
            acc = jnp.where(d_col == dc, val, acc)
        o_ref[0, pl.ds(case, 1)] = jnp.where(ok, acc, NEG_INF)[None]
        return carry

    lax.fori_loop(0, NA_KH, case_body, 0)


def _na_bias_table(rpb):
    nh = rpb.shape[0]
    return pl.pallas_call(
        _na_bias_kernel,
        grid=(nh,),
        in_specs=[pl.BlockSpec(memory_space=pltpu.SMEM)],
        out_specs=pl.BlockSpec((1, NA_KH, GRID_W, NA_KH * GRID_W), lambda h: (h // 2, 0, h % 2, 0)),
        out_shape=jax.ShapeDtypeStruct((nh // 2, NA_KH, 2 * GRID_W, NA_KH * GRID_W), F32),
        compiler_params=_cp("parallel"),
        name="na_bias",
    )(rpb.reshape(-1))


def _na_kernel(q_ref, k_ref, v_ref, kx_ref, vx_ref, bias_ref, o_ref, *, rows):
    scale = HEAD_DIM ** -0.5
    m0, m1 = _half_masks()
    kx = _bf(kx_ref[...])
    vx = _bf(vx_ref[...])

    def body(it, carry):
        us = range(NA_UNROLL)
        r = [it * NA_UNROLL + u for u in us]
        rs = [jnp.clip(r[u] - NA_KH // 2, 0, rows - NA_KH) for u in us]
        case = [rs[u] - r[u] + NA_KH - 1 for u in us]
        q0 = [pl.multiple_of(r[u] * GRID_W, GRID_W) for u in us]
        k0 = [pl.multiple_of(rs[u] * GRID_W, GRID_W) for u in us]
        qp = [q_ref[pl.ds(q0[u], GRID_W), :].astype(F32) * scale for u in us]
        kw = [_bf(k_ref[pl.ds(k0[u], NA_KH * GRID_W), :]) for u in us]
        vw = [_bf(v_ref[pl.ds(k0[u], NA_KH * GRID_W), :]) for u in us]
        q2 = [_bf(jnp.concatenate([jnp.where(m0, qp[u], 0.0), jnp.where(m1, qp[u], 0.0)], axis=0)) for u in us]
        s = [jnp.concatenate([_dot_nt(q2[u], kw[u]) + bias_ref[0, pl.ds(case[u], 1)][0], _dot_nt(q2[u], kx)],
                             axis=1) for u in us]
        m = [jnp.max(s[u], axis=-1, keepdims=True) for u in us]
        e = [jnp.exp(s[u] - m[u]) for u in us]
        l = [jnp.sum(e[u], axis=-1, keepdims=True) for u in us]
        o = [_dot(_bf(e[u]), jnp.concatenate([vw[u], vx], axis=0)) / l[u] for u in us]
        for u in us:
            o_ref[pl.ds(q0[u], GRID_W), :] = jnp.where(m0, o[u][0:GRID_W], o[u][GRID_W:])
        return carry

    lax.fori_loop(0, rows // NA_UNROLL, body, 0)


def _na_attn(q, k, v, kx, vx, bias, seq, past):
    tokens = q.shape[0]
    nb = tokens // seq
    rows = seq // GRID_W
    blk = pl.BlockSpec((seq, 128), lambda b, p: (b, p))
    cblk = pl.BlockSpec((past, 128), lambda b, p: (b, p))
    return pl.pallas_call(
        functools.partial(_na_kernel, rows=rows),
        grid=(nb, NA_HEADS // 2),
        in_specs=[blk, blk, blk, cblk, cblk,
                  pl.BlockSpec((1, NA_KH, 2 * GRID_W, NA_KH * GRID_W), lambda b, p: (p, 0, 0, 0))],
        out_specs=blk,
        out_shape=jax.ShapeDtypeStruct((tokens, NA_W), F32),
        compiler_params=_cp("parallel", "parallel"),
        name="na_attn",
    )(q, k, v, kx, vx, bias)


def _swa_kernel(sink_ref, q_ref, k_ref, v_ref, kx_ref, vx_ref, o_ref, *, seq):
    scale = HEAD_DIM ** -0.5
    blk = SWA_WINDOW
    m0, m1 = _half_masks()
    masks = (m0, m1)
    kx = _bf(kx_ref[...])
    vx = _bf(vx_ref[...])
    group = SWA_HEADS // SWA_KV_HEADS

    sk = []
    for g in range(SWA_KV_HEADS):
        sk.append(jnp.concatenate([jnp.full((blk, 1), sink_ref[h], F32) for h in range(g * group, (g + 1) * group)],
                                  axis=0))

    def body(it, carry):
        us = range(SWA_UNROLL)
        cs = [(u, g) for u in us for g in range(SWA_KV_HEADS)]
        nb = [it * SWA_UNROLL + u for u in us]
        ks = [pl.multiple_of(jnp.clip((nb[u] - 1) * blk, 0, seq - 3 * blk), blk) for u in us]
        q0 = [pl.multiple_of(nb[u] * blk, blk) for u in us]
        kw = [_bf(k_ref[pl.ds(ks[u], 3 * blk), :]) for u in us]
        vw = [_bf(v_ref[pl.ds(ks[u], 3 * blk), :]) for u in us]
        ok = []
        for u in us:
            qpos = q0[u] + lax.broadcasted_iota(jnp.int32, (group * blk, 1), 0) % blk
            kpos = ks[u] + lax.broadcasted_iota(jnp.int32, (1, 3 * blk), 1)
            ok.append(jnp.abs(qpos - kpos) <= SWA_WINDOW)
        qg = {}
        for u in us:
            pairs = [q_ref[pl.ds(q0[u], blk), p * 128:(p + 1) * 128].astype(F32) * scale
                     for p in range(SWA_HEADS // 2)]
            for g in range(SWA_KV_HEADS):
                qs = []
                for h in range(g * group, (g + 1) * group):
                    qh = pairs[h // 2] if h % 2 == g else _swap_halves(pairs[h // 2])
                    qs.append(jnp.where(masks[g], qh, 0.0))
                qg[u, g] = _bf(jnp.concatenate(qs, axis=0))
        sw = {c: jnp.where(ok[c[0]], _dot_nt(qg[c], kw[c[0]]), NEG_INF) for c in cs}
        sx = {c: _dot_nt(qg[c], kx) for c in cs}
        m = {c: jnp.maximum(jnp.maximum(jnp.max(sw[c], axis=-1, keepdims=True),
                                        jnp.max(sx[c], axis=-1, keepdims=True)), sk[c[1]]) for c in cs}
        ew = {c: jnp.exp(sw[c] - m[c]) for c in cs}
        ex = {c: jnp.exp(sx[c] - m[c]) for c in cs}
        l = {c: jnp.sum(ew[c], axis=-1, keepdims=True) + jnp.sum(ex[c], axis=-1, keepdims=True)
             + jnp.exp(sk[c[1]] - m[c]) for c in cs}
        o = {c: (_dot(_bf(ew[c]), vw[c[0]]) + _dot(_bf(ex[c]), vx)) / l[c] for c in cs}
        for u in us:
            head_out = []
            for g in range(SWA_KV_HEADS):
                for i in range(group):
                    h = g * group + i
                    oh = o[u, g][i * blk:(i + 1) * blk]
                    head_out.append(oh if h % 2 == g else _swap_halves(oh))
            for p in range(SWA_HEADS // 2):
                o_ref[pl.ds(q0[u], blk), p * 128:(p + 1) * 128] = jnp.where(m0, head_out[2 * p], head_out[2 * p + 1])
        return carry

    lax.fori_loop(0, seq // (blk * SWA_UNROLL), body, 0)


def _swa_attn(sink, q, k, v, kx, vx, seq, past):
    tokens = q.shape[0]
    blk = lambda w: pl.BlockSpec((seq, w), lambda b: (b, 0))
    cblk = pl.BlockSpec((past, SWA_KV_W), lambda b: (b, 0))
    return pl.pallas_call(
        functools.partial(_swa_kernel, seq=seq),
        grid=(tokens // seq,),
        in_specs=[pl.BlockSpec(memory_space=pltpu.SMEM), blk(SWA_W), blk(SWA_KV_W), blk(SWA_KV_W), cblk, cblk],
        out_specs=blk(SWA_W),
        out_shape=jax.ShapeDtypeStruct((tokens, SWA_W), F32),
        compiler_params=_cp("parallel"),
        name="swa_attn",
    )(sink, q, k, v, kx, vx)


def _rw_pre_kernel(u_ref, up_ref, un_ref, mu_ref, w0_ref, w2_ref, a0_ref, a2_ref, g2_ref, kk_ref_, ka_ref_,
                   rk_ref, ones_ref, r_o, kk_o, v_o, ld_o, kka_o, kd_o, g_o, bonus_o, *, tiles_per_req):
    i = pl.program_id(0)
    u = u_ref[...]
    tm = u.shape[0]
    rowi = lax.broadcasted_iota(jnp.int32, (tm, 1), 0)
    first = (i % tiles_per_req) == 0
    last = (i % tiles_per_req) == tiles_per_req - 1
    prev_row = jnp.where(first, 0.0, up_ref[7:8, :])
    next_row = jnp.where(last, 0.0, un_ref[0:1, :])
    prev = jnp.where(rowi == 0, prev_row, pltpu.roll(u, 1, 0))
    nxt = jnp.where(rowi == tm - 1, next_row, pltpu.roll(u, tm - 1, 0))
    us = u + mu_ref[0:1, :] * (prev - u) + mu_ref[1:2, :] * (nxt - u)
    r = us[:, 0:RW_W]
    k = us[:, RW_W:2 * RW_W]
    v = us[:, 2 * RW_W:3 * RW_W]
    wl = us[:, 3 * RW_W:3 * RW_W + 128]
    al = us[:, 3 * RW_W + 128:3 * RW_W + 256]
    gl = us[:, 3 * RW_W + 256:3 * RW_W + 384]
    z = -(w0_ref[...] + _dot(_bf(jnp.tanh(wl)), w2_ref[...]))
    softplus = jnp.maximum(z, 0.0) + jnp.log(1.0 + jnp.exp(-jnp.abs(z)))
    w = -softplus - 0.5
    ld = -jnp.exp(w)
    a = _sigmoid(a0_ref[...] + _dot(_bf(al), a2_ref[...]))
    g = _dot(_bf(_sigmoid(gl)), g2_ref[...])
    ones = ones_ref[...]
    kkr = k * kk_ref_[...]
    kk = kkr * lax.rsqrt(jnp.maximum(_dot2(kkr * kkr, ones), 1e-24))
    k_a = ka_ref_[...]
    kd_f = k * (1.0 + (a[:, 0:RW_W] - 1.0) * k_a)
    kd_b = k * (1.0 + (a[:, RW_W:] - 1.0) * k_a)
    r_o[...] = r
    kk_o[...] = kk
    v_o[...] = v
    ld_o[...] = ld
    kka_o[:, 0:RW_W] = kk * a[:, 0:RW_W]
    kka_o[:, RW_W:] = kk * a[:, RW_W:]
    kd_o[:, 0:RW_W] = kd_f
    kd_o[:, RW_W:] = kd_b
    g_o[...] = g
    bonus_o[...] = _dot2(r * (kd_f + kd_b) * rk_ref[...], ones) * v


def _rw_pre(urw, p, seq):
    tokens = urw.shape[0]
    tm = _row_tile(seq)
    tpr = seq // tm
    nt = tokens // tm
    r8 = tm // 8
    row = lambda w: pl.BlockSpec((tm, w), lambda i: (i, 0))
    in_specs = [row(RW_IN_W),
                pl.BlockSpec((8, RW_IN_W), lambda i: (jnp.maximum(i * r8 - 1, 0), 0)),
                pl.BlockSpec((8, RW_IN_W), lambda i: (jnp.minimum((i + 1) * r8, nt * r8 - 1), 0)),
                _full((2, RW_IN_W)), _full((1, 2 * RW_W)), _full((128, 2 * RW_W)), _full((1, 2 * RW_W)),
                _full((128, 2 * RW_W)), _full((128, RW_W)), _full((1, RW_W)), _full((1, RW_W)),
                _full((1, RW_W)), _full((RW_W, RW_W))]
    widths = [RW_W, RW_W, RW_W, 2 * RW_W, 2 * RW_W, 2 * RW_W, RW_W, RW_W]
    return pl.pallas_call(
        functools.partial(_rw_pre_kernel, tiles_per_req=tpr),
        grid=(nt,),
        in_specs=in_specs,
        out_specs=[row(w) for w in widths],
        out_shape=[jax.ShapeDtypeStruct((tokens, w), F32) for w in widths],
        compiler_params=_cp("parallel"),
        name="rw_pre",
    )(urw, urw, urw, p['mu'], p['w0'], p['w2'], p['a0'], p['a2'], p['g2'], p['k_k'], p['k_a'], p['r_k'],
      p['ones256'])


def _rw_masks():
    t = RW_CHUNK
    n = RW_HEADS * t
    tt = np.arange(t)[:, None]
    ss = (np.arange(n) % t)[None, :]
    before = np.stack([ss < tt, ss > tt])
    diag = (ss == tt)
    strict = before.astype(np.float32)
    incl = (before | diag[None]).astype(np.float32)
    eye = diag.astype(np.float32)
    ti = np.arange(t)
    tri = np.stack([ti[None, :] <= ti[:, None], ti[None, :] >= ti[:, None]]).astype(np.float32)
    hd = np.arange(n) // t
    same = (hd[:, None] == hd[None, :]).astype(np.float32)
    return (jnp.asarray(strict), jnp.asarray(incl), jnp.asarray(tri, dtype=BF16), jnp.asarray(same),
            jnp.asarray(eye))


def _rw_scan_kernel(r_ref, kk_ref, v_ref, ld_ref, kka_ref, kd_ref, s0_ref, strict_ref, incl_ref, tri_ref,
                    same_ref, eye_ref, o_ref, sfin_ref, s_scr, *, nsub):
    d = pl.program_id(1)
    c = pl.program_id(2)
    t = RW_CHUNK
    n = RW_HEADS * t

    @pl.when(c == 0)
    def _():
        s_scr[...] = s0_ref[0, 0]

    strict = strict_ref[0]
    incl = incl_ref[0]
    tri = tri_ref[0]
    eye = eye_ref[...]
    same = same_ref[...]
    same_bf = _bf(same)

    def bd(x):
        return jnp.concatenate([_bf(x)] * RW_HEADS, axis=0) * same_bf

    js = range(nsub)
    rows = [pl.ds(pl.multiple_of((j + d * (nsub - 1 - 2 * j)) * t, t), t) for j in js]
    ld = [ld_ref[rows[j], :] for j in js]
    cum = []
    for j in js:
        lhi, lmid, llo = _split3(ld[j])
        cum.append(_dot(tri, lhi) + _dot(tri, lmid) + _dot(tri, llo))
    cend = [jnp.sum(ld[j], axis=0, keepdims=True) for j in js]
    kka = [kka_ref[rows[j], :] for j in js]
    kd = [kd_ref[rows[j], :] for j in js]
    v = [v_ref[rows[j], :] for j in js]
    at = [-kk_ref[rows[j], :] * jnp.exp(cum[j] - ld[j]) for j in js]
    rt = [r_ref[rows[j], :] * jnp.exp(cum[j]) for j in js]
    e_inv = [jnp.exp(-cum[j]) for j in js]
    aa = [_dot_nt(_bf(jnp.concatenate([at[j], rt[j]], axis=0)),
                  jnp.concatenate([bd(kka[j] * e_inv[j]), bd(kd[j] * e_inv[j])], axis=0)) for j in js]
    a_ab = [aa[j][0:t, 0:n] * strict for j in js]
    x = [eye + a_ab[j] for j in js]
    pw = a_ab
    for _ in range(RW_CHUNK.bit_length() - 3):
        pw = [_dot(_bf(pw[j]), bd(pw[j])) for j in js]
        x = [x[j] + _dot(_bf(x[j]), bd(pw[j])) for j in js]
    xs = [_split2(x[j]) for j in js]
    sa = [_split2(a_ab[j]) for j in js]
    ax = [_dot(sa[j][0], bd(xs[j][0])) + _dot(sa[j][1], bd(xs[j][0])) + _dot(sa[j][0], bd(xs[j][1])) for j in js]
    x = [x[j] + _dot(xs[j][0], bd(eye - x[j] + ax[j])) for j in js]
    v_bd = [bd(v[j]) for j in js]
    wv = [_dot(_bf(aa[j][0:t, n:] * strict), v_bd[j]) for j in js]
    mu = [_dot(_bf(x[j]), jnp.concatenate([bd(at[j]), bd(wv[j])], axis=1)) for j in js]
    m1 = [mu[j][:, 0:n] for j in js]
    u0 = [mu[j][:, n:] for j in js]
    e_end = [jnp.exp(cend[j] - cum[j]) for j in js]
    bend = [_bf(kka[j] * e_end[j]) for j in js]
    g = [_bf(_dot_tn(_bf(m1[j]), bend[j]) * same) for j in js]
    cst = []
    for j in js:
        full = _dot_tn(_bf(jnp.concatenate([u0[j], v[j]], axis=0)),
                       jnp.concatenate([bend[j], _bf(kd[j] * e_end[j])], axis=0)) * same
        cst.append(functools.reduce(jnp.add, [full[h * t:(h + 1) * t] for h in range(RW_HEADS)]))
    qo = [_dot(_bf(aa[j][t:, 0:n] * incl), jnp.concatenate([bd(m1[j]), bd(u0[j])], axis=1)) for j in js]
    q = [_bf(rt[j] + qo[j][:, 0:n]) for j in js]
    o0 = [qo[j][:, n:] + _dot(_bf(aa[j][t:, n:] * incl), v_bd[j]) for j in js]

    s = s_scr[...]
    for j in js:
        o_ref[0, rows[j], :] = _dot_nt(q[j], bd(s)) + o0[j]
        s = s * jnp.exp(cend[j]) + _dot(_bf(s), g[j]) + cst[j]
    s_scr[...] = s

    @pl.when(c == pl.num_programs(2) - 1)
    def _():
        sfin_ref[0, 0] = s


def _rw_scan(r, kk, v, ld, kka, kd, s0_bd, consts, seq):
    tokens = r.shape[0]
    nreq = tokens // seq
    tb = min(seq, 512)
    nblk = seq // tb
    nsub = tb // RW_CHUNK
    n = RW_HEADS * RW_CHUNK
    cc = lambda d, c: c + d * (nblk - 1 - 2 * c)
    shared = pl.BlockSpec((tb, RW_W), lambda b, d, c: (b * nblk + cc(d, c), 0))
    dirw = pl.BlockSpec((tb, RW_W), lambda b, d, c: (b * nblk + cc(d, c), d))
    strict, incl, tri, same, eye = consts
    return pl.pallas_call(
        functools.partial(_rw_scan_kernel, nsub=nsub),
        grid=(nreq, 2, nblk),
        in_specs=[shared, shared, shared, dirw, dirw, dirw,
                  pl.BlockSpec((1, 1, HEAD_DIM, n), lambda b, d, c: (b, d, 0, 0)),
                  pl.BlockSpec((1, RW_CHUNK, n), lambda b, d, c: (d, 0, 0)),
                  pl.BlockSpec((1, RW_CHUNK, n), lambda b, d, c: (d, 0, 0)),
                  pl.BlockSpec((1, RW_CHUNK, RW_CHUNK), lambda b, d, c: (d, 0, 0)),
                  _full((n, n)), _full((RW_CHUNK, n))],
        out_specs=[pl.BlockSpec((1, tb, RW_W), lambda b, d, c: (d, b * nblk + cc(d, c), 0)),
                   pl.BlockSpec((1, 1, HEAD_DIM, n), lambda b, d, c: (b, d, 0, 0))],
        out_shape=[jax.ShapeDtypeStruct((2, tokens, RW_W), F32),
                   jax.ShapeDtypeStruct((nreq, 2, HEAD_DIM, n), F32)],
        scratch_shapes=[pltpu.VMEM((HEAD_DIM, n), F32)],
        compiler_params=_cp("parallel", "parallel", "arbitrary"),
        name="rw_scan",
    )(r, kk, v, ld, kka, kd, s0_bd, strict, incl, tri, same, eye)


def _state_to_lanes(s):
    b = s.shape[0]
    return jnp.transpose(s, (0, 1, 3, 2, 4)).reshape(b, 2, HEAD_DIM, RW_W)


def _lanes_to_state(slc):
    b = slc.shape[0]
    return jnp.transpose(slc.reshape(b, 2, HEAD_DIM, RW_HEADS, HEAD_DIM), (0, 1, 3, 2, 4))


def _finish_kernel(x_ref, oa_ref, oc_ref, o2_ref, bonus_ref, g_ref, mod_ref, wout_ref, lng_ref, lnb_ref,
                   n2_ref, wr_hi_ref, wr_lo_ref, ones_ref, x1_ref, h2_ref, aff_ref):
    ones = ones_ref[0:128, 0:128]
    g1 = mod_ref[0, 2:3, :]
    sh2 = mod_ref[0, 3:4, :]
    sc2 = mod_ref[0, 4:5, :]
    subs = [slice(i, i + SUB_ROWS) for i in range(0, x_ref.shape[0], SUB_ROWS)]
    mixins = []
    for sl in subs:
        y = o2_ref[0, sl, :] + o2_ref[1, sl, :]
        mu = _head_sums(y, ones) * (1.0 / HEAD_DIM)
        yc = y - mu
        var = _head_sums(yc * yc, ones) * (1.0 / HEAD_DIM)
        yn = yc * lax.rsqrt(var + GN_EPS) * lng_ref[...] + lnb_ref[...]
        ob = (yn + bonus_ref[sl, :]) * g_ref[sl, :]
        mixins.append(jnp.concatenate([_bf(oa_ref[sl, :]), _bf(ob), _bf(oc_ref[sl, :])], axis=1))
    mixes = [_dot(mixin, wout_ref[...]) for mixin in mixins]
    for sl, mix in zip(subs, mixes):
        x1 = x_ref[sl, :] + g1 * mix
        ms = jnp.mean(x1 * x1, axis=-1, keepdims=True)
        h2 = x1 * lax.rsqrt(ms + NORM_EPS) * n2_ref[...] * (1.0 + sc2) + sh2
        x1_ref[sl, :] = x1
        h2_ref[sl, :] = _bf(h2)
        hhi, hlo = _split2(h2)
        logits = _dot(hhi, wr_hi_ref[...]) + _dot(hlo, wr_hi_ref[...]) + _dot(hhi, wr_lo_ref[...])
        m = jnp.max(logits, axis=-1, keepdims=True)
        e = jnp.exp(logits - m)
        aff_ref[sl, :] = e / jnp.sum(e, axis=-1, keepdims=True)


def _finish(x, oa, oc, o2, bonus, g, mod, p, seq, per_request_mod):
    tokens, d = x.shape
    tm = _row_tile(seq)
    tpr = seq // tm
    mod_map = (lambda i: (i // tpr, 0, 0)) if per_request_mod else (lambda i: (0, 0, 0))
    row = lambda w: pl.BlockSpec((tm, w), lambda i: (i, 0))
    return pl.pallas_call(
        _finish_kernel,
        grid=(tokens // tm,),
        in_specs=[row(d), row(NA_W), row(SWA_W), pl.BlockSpec((2, tm, RW_W), lambda i: (0, i, 0)), row(RW_W),
                  row(RW_W), pl.BlockSpec((1, 6, d), mod_map), _full((d, d)), _full((1, RW_W)),
                  _full((1, RW_W)), _full((1, d)), _full((d, N_EXPERTS)), _full((d, N_EXPERTS)),
                  _full((RW_W, RW_W))],
        out_specs=[row(d), row(d), row(N_EXPERTS)],
        out_shape=[jax.ShapeDtypeStruct((tokens, d), F32), jax.ShapeDtypeStruct((tokens, d), BF16),
                   jax.ShapeDtypeStruct((tokens, N_EXPERTS), F32)],
        compiler_params=_cp("parallel"),
        name="finish",
    )(x, oa, oc, o2, bonus, g, mod, p['w_out'], p['ln_g'], p['ln_b'], p['n2'], p['wr_hi'], p['wr_lo'],
      p['ones256'])


def _topk_kernel(aff_ref, tri_ref, eye_ref, place_ref, slot_ref, slotrow_ref, gfull_ref, ends_ref, *, cap, group,
                 seq, tb):
    aff = aff_ref[...]
    bits = lax.bitcast_convert_type(aff, jnp.int32)
    capf = jnp.float32(cap)
    eye = eye_ref[...]
    ghi, gmid, glo = _split3(aff)
    aff_t = _dot_nt(eye, ghi) + _dot_nt(eye, gmid) + _dot_nt(eye, glo)
    bits_t = lax.bitcast_convert_type(aff_t, jnp.int32)
    rs = range(group)

    def bis(_, carry):
        los, his = carry
        nlo, nhi = [], []
        for r in rs:
            mid = los[r] + ((his[r] - los[r] + 1) >> 1)
            cnt = jnp.sum(jnp.where(bits_t[:, r * seq:(r + 1) * seq] >= mid, 1.0, 0.0), axis=1, keepdims=True)
            ge = cnt >= capf
            nlo.append(jnp.where(ge, mid, los[r]))
            nhi.append(jnp.where(ge, his[r], mid - 1))
        return tuple(nlo), tuple(nhi)

    lo0 = tuple(jnp.zeros((N_EXPERTS, 1), jnp.int32) for _ in rs)
    hi0 = tuple(jnp.full((N_EXPERTS, 1), 0x7F7FFFFF, jnp.int32) for _ in rs)
    thr_cols, _ = lax.fori_loop(0, 31, bis, (lo0, hi0))
    ri = lax.broadcasted_iota(jnp.int32, (N_EXPERTS, N_EXPERTS), 0)
    ci = lax.broadcasted_iota(jnp.int32, (N_EXPERTS, N_EXPERTS), 1)
    tri = tri_ref[...]
    for r in rs:
        thr = jnp.sum(jnp.where(ri == ci, thr_cols[r], 0), axis=0, keepdims=True)
        rows = slice(r * seq, (r + 1) * seq)
        gt = jnp.where(bits[rows] > thr, 1.0, 0.0)
        eq = jnp.where(bits[rows] == thr, 1.0, 0.0)
        need = capf - jnp.sum(gt, axis=0, keepdims=True)
        offset = float(r * cap)
        carry_g = jnp.zeros((1, N_EXPERTS), F32)
        carry_e = jnp.zeros((1, N_EXPERTS), F32)
        for blk in range(seq // tb):
            sl = slice(blk * tb, (blk + 1) * tb)
            out = slice(r * seq + blk * tb, r * seq + (blk + 1) * tb)
            pg = _dot(tri, _bf(gt[sl])) + carry_g
            pe = _dot(tri, _bf(eq[sl])) + carry_e
            carry_g = pg[tb - 1:tb, :]
            carry_e = pe[tb - 1:tb, :]
            sel = gt[sl] + eq[sl] * jnp.where(pe <= need, 1.0, 0.0)
            slot = jnp.where(sel > 0.5, pg + jnp.minimum(pe, need) - 1.0 + offset, -1.0)
            slot_ref[out, :] = slot
            ends_ref[r, blk:blk + 1, :] = carry_g + jnp.minimum(carry_e, need) + offset
            shi, slo = _split2(slot)
            slotrow_ref[0, :, 0, out] = _dot_nt(eye, shi) + _dot_nt(eye, slo)
            gfull_ref[out, :] = _bf(_dot(ghi[out], place_ref[0]) + _dot(gmid[out], place_ref[1])
                                    + _dot(glo[out], place_ref[2]))


def _topk(aff, seq, group):
    tokens = aff.shape[0]
    nreq = tokens // seq
    cap = EC_CAPACITY * seq // N_EXPERTS
    tb = min(seq, 512)
    ti = np.arange(tb)
    tri = jnp.asarray((ti[None, :] <= ti[:, None]).astype(np.float32), dtype=BF16)
    eye = jnp.asarray(np.eye(N_EXPERTS, dtype=np.float32), dtype=BF16)
    place = np.zeros((3, N_EXPERTS, 128), np.float32)
    for s in range(3):
        place[s, np.arange(N_EXPERTS), s * N_EXPERTS + np.arange(N_EXPERTS)] = 1.0
    place = jnp.asarray(place, dtype=BF16)
    nblk = seq // tb
    slot, slotrow, gfull, ends = pl.pallas_call(
        functools.partial(_topk_kernel, cap=cap, group=group, seq=seq, tb=tb),
        grid=(nreq // group,),
        in_specs=[pl.BlockSpec((group * seq, N_EXPERTS), lambda b: (b, 0)), _full((tb, tb)),
                  _full((N_EXPERTS, N_EXPERTS)), _full((3, N_EXPERTS, 128))],
        out_specs=[pl.BlockSpec((group * seq, N_EXPERTS), lambda b: (b, 0)),
                   pl.BlockSpec((1, N_EXPERTS, 1, group * seq), lambda b: (b, 0, 0, 0)),
                   pl.BlockSpec((group * seq, 128), lambda b: (b, 0)),
                   pl.BlockSpec((group, nblk, N_EXPERTS), lambda b: (b, 0, 0))],
        out_shape=[jax.ShapeDtypeStruct((tokens, N_EXPERTS), F32),
                   jax.ShapeDtypeStruct((nreq // group, N_EXPERTS, 1, group * seq), F32),
                   jax.ShapeDtypeStruct((tokens, 128), BF16),
                   jax.ShapeDtypeStruct((nreq, nblk, N_EXPERTS), F32)],
        compiler_params=_cp("parallel"),
        name="topk",
    )(aff, tri, eye, place)
    ends = ends.reshape(nreq // group, group * nblk, N_EXPERTS).transpose(0, 2, 1)
    return slot, slotrow, gfull, ends.astype(jnp.int32).reshape(-1), tb


MOE_EB = 8


def _moe_dispatch_kernel(ends_ref, h_ref, slotrow_ref, gfull_ref, xe_ref, gs_ref, *, ct, nch):
    gi = pl.program_id(0)
    eb = pl.program_id(1)
    c = pl.program_id(2)
    mt = 128

    @pl.when(c == 0)
    def _():
        xe_ref[...] = jnp.zeros_like(xe_ref)
        gs_ref[...] = jnp.zeros_like(gs_ref)

    jcol = lax.broadcasted_iota(jnp.int32, (mt, 1), 0)
    starts, his, pieces = [], [], []
    for i in range(MOE_EB):
        base = (gi * N_EXPERTS + eb * MOE_EB + i) * nch
        lo = jnp.where(c == 0, 0, ends_ref[base + jnp.maximum(c - 1, 0)])
        his.append(ends_ref[base + c])
        start = pl.multiple_of(jnp.minimum((lo // 16) * 16, ct - mt), 16)
        starts.append(start)
        pieces.append(_bf(jnp.where(slotrow_ref[0, i] == (jcol + start).astype(F32), 1.0, 0.0)))
    onehot = jnp.concatenate(pieces, axis=0)
    xw = _bf(_dot(onehot, h_ref[...]))
    gw = _bf(_dot(onehot, gfull_ref[...]))
    for i in range(MOE_EB):
        rows = pl.ds(starts[i], mt)
        xe_ref[0, i, rows, :] += xw[i * mt:(i + 1) * mt]
        gs_ref[0, i, rows, :] += gw[i * mt:(i + 1) * mt]
    for i in range(MOE_EB):
        for w in range(1, ct // mt):
            wlo = starts[i] + w * mt

            @pl.when(wlo < his[i])
            def _(i=i, wlo=wlo):
                ws = pl.multiple_of(jnp.minimum(wlo, ct - mt), 16)
                slot = slotrow_ref[0, i]
                hit = (jnp.where(slot == (jcol + ws).astype(F32), 1.0, 0.0)
                       * jnp.where(slot >= wlo.astype(F32), 1.0, 0.0))
                rows = pl.ds(ws, mt)
                xe_ref[0, i, rows, :] += _bf(_dot(_bf(hit), h_ref[...]))
                gs_ref[0, i, rows, :] += _bf(_dot(_bf(hit), gfull_ref[...]))


def _moe_dispatch(ends, h2, slotrow, gfull, lg, ct, kc):
    tokens, d = h2.shape
    ngrp = tokens // lg
    nch = lg // kc
    grid_spec = pltpu.PrefetchScalarGridSpec(
        num_scalar_prefetch=1,
        grid=(ngrp, N_EXPERTS // MOE_EB, nch),
        in_specs=[pl.BlockSpec((kc, d), lambda gi, eb, c, ends: (gi * nch + c, 0)),
                  pl.BlockSpec((1, MOE_EB, 1, kc), lambda gi, eb, c, ends: (gi, eb, 0, c)),
                  pl.BlockSpec((kc, 128), lambda gi, eb, c, ends: (gi * nch + c, 0))],
        out_specs=[pl.BlockSpec((1, MOE_EB, ct, d), lambda gi, eb, c, ends: (gi, eb, 0, 0)),
                   pl.BlockSpec((1, MOE_EB, ct, 128), lambda gi, eb, c, ends: (gi, eb, 0, 0))])
    return pl.pallas_call(
        functools.partial(_moe_dispatch_kernel, ct=ct, nch=nch),
        grid_spec=grid_spec,
        out_shape=[jax.ShapeDtypeStruct((ngrp, N_EXPERTS, ct, d), BF16),
                   jax.ShapeDtypeStruct((ngrp, N_EXPERTS, ct, 128), BF16)],
        compiler_params=_cp("parallel", "parallel", "arbitrary"),
        name="moe_dispatch",
    )(ends, h2, slotrow, gfull)


def _moe_ffn_kernel(xe_ref, gs_ref, mod_ref, wg_ref, wu_ref, wd_ref, ye_ref):
    e = pl.program_id(1)
    lane = lax.broadcasted_iota(jnp.int32, (1, 128), 1)
    pick = (lane == e) | (lane == e + N_EXPERTS) | (lane == e + 2 * N_EXPERTS)
    gate = jnp.sum(jnp.where(pick, gs_ref[0, 0].astype(F32), 0.0), axis=-1, keepdims=True)
    xb = xe_ref[0, 0]
    hg = _dot(xb, wg_ref[0])
    hu = _dot(xb, wu_ref[0])
    he = _bf(hg * _sigmoid(hg) * hu)
    y = _dot(he, wd_ref[0])
    ye_ref[0, 0] = _bf(y * gate * mod_ref[0, 5:6, :])


def _moe_ffn(xe, gs, mod, wg, wu, wd, per_group_mod):
    ngrp, _, ct, d = xe.shape
    f = wg.shape[2]
    mod_map = (lambda gi, e: (gi, 0, 0)) if per_group_mod else (lambda gi, e: (0, 0, 0))
    return pl.pallas_call(
        _moe_ffn_kernel,
        grid=(ngrp, N_EXPERTS),
        in_specs=[pl.BlockSpec((1, 1, ct, d), lambda gi, e: (gi, e, 0, 0)),
                  pl.BlockSpec((1, 1, ct, 128), lambda gi, e: (gi, e, 0, 0)),
                  pl.BlockSpec((1, 6, d), mod_map),
                  pl.BlockSpec((1, d, f), lambda gi, e: (e, 0, 0)),
                  pl.BlockSpec((1, d, f), lambda gi, e: (e, 0, 0)),
                  pl.BlockSpec((1, f, d), lambda gi, e: (e, 0, 0))],
        out_specs=pl.BlockSpec((1, 1, ct, d), lambda gi, e: (gi, e, 0, 0)),
        out_shape=jax.ShapeDtypeStruct((ngrp, N_EXPERTS, ct, d), BF16),
        compiler_params=_cp("parallel", "parallel"),
        name="moe_ffn",
    )(xe, gs, mod, wg, wu, wd)


def _moe_combine_kernel(ends_ref, x1_ref, slot_ref, ye_ref, o_ref, win_scr, *, ct, nch):
    gi = pl.program_id(0)
    j = pl.program_id(1)
    mt = 128
    wide = N_EXPERTS * mt
    shi, slo = _split2(slot_ref[...])
    col_e = lax.broadcasted_iota(jnp.int32, (N_EXPERTS, wide), 1) // mt
    row_e = lax.broadcasted_iota(jnp.int32, (N_EXPERTS, wide), 0)
    expand = _bf(jnp.where(col_e == row_e, 1.0, 0.0))
    sb = _dot(shi, expand) + _dot(slo, expand)
    lane = lax.broadcasted_iota(jnp.int32, (1, mt), 1)
    starts, his, targets = [], [], []
    for e in range(N_EXPERTS):
        base = (gi * N_EXPERTS + e) * nch
        lo = jnp.where(j == 0, 0, ends_ref[base + jnp.maximum(j - 1, 0)])
        his.append(ends_ref[base + j])
        start = pl.multiple_of(jnp.minimum((lo // 16) * 16, ct - mt), 16)
        win_scr[e * mt:(e + 1) * mt, :] = ye_ref[0, e, pl.ds(start, mt), :]
        starts.append(start)
        targets.append((lane + start).astype(F32))
    onehot = _bf(jnp.where(sb == jnp.concatenate(targets, axis=1), 1.0, 0.0))
    o_ref[...] = x1_ref[...] + _dot(onehot, win_scr[...])
    for e in range(N_EXPERTS):
        for w in range(1, ct // mt):
            wlo = starts[e] + w * mt

            @pl.when(wlo < his[e])
            def _(e=e, wlo=wlo):
                ws = pl.multiple_of(jnp.minimum(wlo, ct - mt), 16)
                sbe = sb[:, e * mt:(e + 1) * mt]
                hit = jnp.where(sbe == (lane + ws).astype(F32), 1.0, 0.0) * jnp.where(sbe >= wlo.astype(F32), 1.0, 0.0)
                o_ref[...] += _dot(_bf(hit), ye_ref[0, e, pl.ds(ws, mt), :])


def _moe_combine(ends, x1, slot, ye, lg, ct, kc):
    tokens, d = x1.shape
    ngrp = tokens // lg
    nch = lg // kc
    grid_spec = pltpu.PrefetchScalarGridSpec(
        num_scalar_prefetch=1,
        grid=(ngrp, nch),
        in_specs=[pl.BlockSpec((kc, d), lambda gi, j, ends: (gi * nch + j, 0)),
                  pl.BlockSpec((kc, N_EXPERTS), lambda gi, j, ends: (gi * nch + j, 0)),
                  pl.BlockSpec((1, N_EXPERTS, ct, d), lambda gi, j, ends: (gi, 0, 0, 0),
                               pipeline_mode=pl.Buffered(1))],
        out_specs=pl.BlockSpec((kc, d), lambda gi, j, ends: (gi * nch + j, 0)),
        scratch_shapes=[pltpu.VMEM((N_EXPERTS * 128, d), BF16)])
    return pl.pallas_call(
        functools.partial(_moe_combine_kernel, ct=ct, nch=nch),
        grid_spec=grid_spec,
        out_shape=jax.ShapeDtypeStruct((tokens, d), F32),
        compiler_params=_cp("parallel", "arbitrary"),
        name="moe_combine",
    )(ends, x1, slot, ye)


def _rope_tables(seq):
    t = np.arange(seq)
    n_freq = HEAD_DIM // 4
    inv = ROPE_THETA ** (-np.arange(n_freq, dtype=np.float32) / n_freq)
    ang = np.concatenate([(t // GRID_W).astype(np.float32)[:, None] * inv,
                          (t % GRID_W).astype(np.float32)[:, None] * inv], axis=-1)
    ang = jnp.asarray(ang, dtype=F32)
    cos, sin = jnp.cos(ang), jnp.sin(ang)
    cos_t = jnp.tile(jnp.concatenate([cos, cos], axis=-1), (1, SWA_HEADS))
    sin_t = jnp.tile(jnp.concatenate([-sin, sin], axis=-1), (1, SWA_HEADS))
    return cos_t, sin_t


def _blockdiag2(w):
    z = jnp.zeros_like(w[0])
    return jnp.concatenate([jnp.concatenate([w[0], z], axis=1), jnp.concatenate([z, w[1]], axis=1)], axis=0)


def _layer_params(l, ada_w, ada_b, norm1_g, norm2_g, w_in, na_q_norm, na_k_norm, na_rpb, rw_mu, rw_w0, rw_w2,
                  rw_a0, rw_a2, rw_g2, rw_k_k, rw_k_a, rw_r_k, rw_ln_g, rw_ln_b, swa_q_norm, swa_k_norm,
                  swa_sink, w_out, w_router, w_gate, w_up, w_down):
    wr = w_router[l]
    wr_hi = wr.astype(BF16)
    w_rw = w_in[l][:, 3 * NA_W:3 * NA_W + RW_IN_W]
    return {
        'n1': norm1_g[l][None], 'n2': norm2_g[l][None], 'w_in': w_in[l].astype(BF16),
        'w_rw_lo': (w_rw - w_rw.astype(BF16).astype(F32)).astype(BF16),
        'gains': (jnp.tile(na_q_norm[l], NA_HEADS)[None], jnp.tile(na_k_norm[l], NA_HEADS)[None],
                  jnp.tile(swa_q_norm[l], SWA_HEADS)[None], jnp.tile(swa_k_norm[l], SWA_KV_HEADS)[None]),
        'bias': _na_bias_table(na_rpb[l]),
        'mu': rw_mu[l], 'w0': rw_w0[l].reshape(1, 2 * RW_W), 'w2': _blockdiag2(rw_w2[l]).astype(BF16),
        'a0': rw_a0[l].reshape(1, 2 * RW_W), 'a2': _blockdiag2(rw_a2[l]).astype(BF16),
        'g2': rw_g2[l].astype(BF16), 'k_k': rw_k_k[l][None], 'k_a': rw_k_a[l][None],
        'r_k': rw_r_k[l].reshape(1, RW_W), 'ln_g': rw_ln_g[l][None], 'ln_b': rw_ln_b[l][None],
        'sink': swa_sink[l], 'w_out': w_out[l].astype(BF16),
        'wr_hi': wr_hi, 'wr_lo': (wr - wr_hi.astype(F32)).astype(BF16),
        'wg': w_gate[l].astype(BF16), 'wu': w_up[l].astype(BF16), 'wd': w_down[l].astype(BF16),
        'ones256': _block_ones(RW_W),
    }


def _mix_and_ffn(x, mod, p, oa, oc, urw, s0_bd, scan_consts, seq, per_request_mod, group):
    r, kk, v, ld, kka, kd, g, bonus = _rw_pre(urw, p, seq)
    o2, sfin = _rw_scan(r, kk, v, ld, kka, kd, s0_bd, scan_consts, seq)
    x1, h2, aff = _finish(x, oa, oc, o2, bonus, g, mod, p, seq, per_request_mod)
    slot, slotrow, gfull, ends, kc = _topk(aff, seq, group)
    cap = EC_CAPACITY * seq // N_EXPERTS
    lg, ct = group * seq, group * cap
    xe, gs = _moe_dispatch(ends, h2, slotrow, gfull, lg, ct, kc)
    ye = _moe_ffn(xe, gs, mod, p['wg'], p['wu'], p['wd'], per_request_mod)
    return _moe_combine(ends, x1, slot, ye, lg, ct, kc), sfin


def _context_layer(x, mod, p, ones384, scan_consts, seq):
    qa, ka, va, urw, qc, kc, vc = _proj(x, mod, p['n1'], p['w_in'], p['gains'], ones384, None, seq, False, F32,
                                        w_rw_lo=p['w_rw_lo'])
    oa, oc = _ctx_attn(p['sink'], qa, ka, va, qc, kc, vc, seq)
    nreq = x.shape[0] // seq
    s0 = jnp.zeros((nreq, 2, HEAD_DIM, RW_W), F32)
    y, sfin = _mix_and_ffn(x, mod, p, oa, oc, urw, s0, scan_consts, seq, False, CTX_GROUP)
    return y, ka, va, kc, vc, sfin


def _latent_layer(x, mod, p, ones384, scan_consts, rope_tabs, seq, kx_na, vx_na, kx_swa, vx_swa, s0_bd, past):
    qa, ka, va, urw, qc, kc, vc = _proj(x, mod, p['n1'], p['w_in'], p['gains'], ones384, rope_tabs, seq, True,
                                        BF16)
    oa = _na_attn(qa, ka, va, kx_na, vx_na, p['bias'], seq, past)
    oc = _swa_attn(p['sink'], qc, kc, vc, kx_swa, vx_swa, seq, past)
    y, _ = _mix_and_ffn(x, mod, p, oa, oc, urw, s0_bd, scan_consts, seq, True, 1)
    return y


def _heads_first(z, nreq, seq, heads):
    return z.reshape(nreq, seq, heads, HEAD_DIM).transpose(0, 2, 1, 3)


def _tokens_first(z):
    b, h, n, dh = z.shape
    return z.transpose(0, 2, 1, 3).reshape(b * n, h * dh)


def kernel(x_prompt, x_sample, cache_na_k, cache_na_v, cache_swa_k, cache_swa_v, state_rwkv, c, c_ctx, ada_w, ada_b, norm1_g, norm2_g, w_in, na_q_norm, na_k_norm, na_rpb, rw_mu, rw_w0, rw_w2, rw_a0, rw_a2, rw_g2, rw_k_k, rw_k_a, rw_r_k, rw_ln_g, rw_ln_b, swa_q_norm, swa_k_norm, swa_sink, w_out, w_router, w_gate, w_up, w_down):
    nb, seq, d = x_prompt.shape
    db, dseq, _ = x_sample.shape
    depth = ada_w.shape[0]
    past = cache_na_k.shape[3]
    cond = jnp.concatenate([c, c_ctx[None], jnp.zeros((16 - db - 1, d), F32)], axis=0)
    mod_all = _adaln(cond, ada_w, ada_b).reshape(depth, 16, 6, d)
    ones384 = _block_ones(NA_W)
    scan_consts = _rw_masks()
    rope_tabs = _rope_tables(dseq)
    xp = x_prompt.reshape(nb * seq, d)
    xs = x_sample.reshape(db * dseq, d)
    new_ka, new_va, new_kc, new_vc, new_s = [], [], [], [], []
    for l in range(depth):
        p = _layer_params(l, ada_w, ada_b, norm1_g, norm2_g, w_in, na_q_norm, na_k_norm, na_rpb, rw_mu, rw_w0,
                          rw_w2, rw_a0, rw_a2, rw_g2, rw_k_k, rw_k_a, rw_r_k, rw_ln_g, rw_ln_b, swa_q_norm,
                          swa_k_norm, swa_sink, w_out, w_router, w_gate, w_up, w_down)
        mod_ctx = mod_all[l, db:db + 1]
        mod_lat = mod_all[l, 0:db]
        xp, ka, va, kc, vc, sfin = _context_layer(xp, mod_ctx, p, ones384, scan_consts, seq)
        new_ka.append(_heads_first(ka, nb, seq, NA_HEADS))
        new_va.append(_heads_first(va, nb, seq, NA_HEADS))
        new_kc.append(_heads_first(kc, nb, seq, SWA_KV_HEADS))
        new_vc.append(_heads_first(vc, nb, seq, SWA_KV_HEADS))
        new_s.append(_lanes_to_state(sfin))
        xs = _latent_layer(xs, mod_lat, p, ones384, scan_consts, rope_tabs, dseq,
                           _tokens_first(cache_na_k[:, l]), _tokens_first(cache_na_v[:, l]),
                           _tokens_first(cache_swa_k[:, l]), _tokens_first(cache_swa_v[:, l]),
                           _state_to_lanes(state_rwkv[:, l]), past)
    return (xp.reshape(nb, seq, d), xs.reshape(db, dseq, d), jnp.stack(new_ka, axis=1),
            jnp.stack(new_va, axis=1), jnp.stack(new_kc, axis=1), jnp.stack(new_vc, axis=1),
            jnp.stack(new_s, axis=1))
```

```python
import functools

import numpy as np
import jax
import jax.numpy as jnp
from jax import lax
from jax.experimental import pallas as pl
from jax.experimental.pallas import tpu as pltpu

F32 = jnp.float32
BF16 = jnp.bfloat16

HEAD_DIM = 64
GRID_W = 64
NA_HEADS = 6
NA_KH = 8
NA_KW = 16
RW_HEADS = 4
SWA_HEADS = 6
SWA_KV_HEADS = 2
SWA_WINDOW = 128
N_EXPERTS = 16
EC_CAPACITY = 2
ROPE_THETA = 10000.0
NORM_EPS = 1e-6
GN_EPS = 64e-5
NEG_INF = -1e30
SUB_ROWS = 256
NA_UNROLL = 8
SWA_UNROLL = 2
RW_CHUNK = 64
assert RW_CHUNK == HEAD_DIM
CTX_GROUP = 16
RW_W = RW_HEADS * HEAD_DIM
NA_W = NA_HEADS * HEAD_DIM
SWA_W = SWA_HEADS * HEAD_DIM
SWA_KV_W = SWA_KV_HEADS * HEAD_DIM
RW_IN_W = 1152
VMEM_LIMIT = 56 * 1024 * 1024


def _cp(*sem):
    return pltpu.CompilerParams(dimension_semantics=sem, vmem_limit_bytes=VMEM_LIMIT)


def _bf(x):
    return x.astype(BF16)


def _dot(a, b):
    return jnp.dot(a, b, preferred_element_type=F32)


def _dot_nt(a, b):
    return lax.dot_general(a, b, (((1,), (1,)), ((), ())), preferred_element_type=F32)


def _dot_tn(a, b):
    return lax.dot_general(a, b, (((0,), (0,)), ((), ())), preferred_element_type=F32)


def _split2(x):
    hi = x.astype(BF16)
    lo = (x - hi.astype(F32)).astype(BF16)
    return hi, lo


def _split3(x):
    hi = x.astype(BF16)
    r1 = x - hi.astype(F32)
    mid = r1.astype(BF16)
    lo = (r1 - mid.astype(F32)).astype(BF16)
    return hi, mid, lo


def _dot2(a, b_bf):
    hi, lo = _split2(a)
    return _dot(hi, b_bf) + _dot(lo, b_bf)


def _sigmoid(x):
    return 1.0 / (1.0 + jnp.exp(-x))


def _block_ones(width):
    i = np.arange(width) // HEAD_DIM
    return jnp.asarray((i[:, None] == i[None, :]).astype(np.float32), dtype=BF16)


def _row_tile(seq):
    return 512 if seq % 512 == 0 else 256


def _full(shape):
    return pl.BlockSpec(shape, lambda *_: (0,) * len(shape))


def _adaln_kernel(c_ref, w_ref, b_ref, o_ref):
    c = c_ref[...]
    s = c * _sigmoid(c)
    shi, slo = _split2(s)
    whi, wlo = _split2(w_ref[0])
    o_ref[0] = _dot(shi, whi) + _dot(slo, whi) + _dot(shi, wlo) + b_ref[0]


def _adaln(cond, ada_w, ada_b):
    nl, d, n6 = ada_w.shape
    tn = 1536
    rows = cond.shape[0]
    return pl.pallas_call(
        _adaln_kernel,
        grid=(nl, n6 // tn),
        in_specs=[pl.BlockSpec((rows, d), lambda l, j: (0, 0)),
                  pl.BlockSpec((1, d, tn), lambda l, j: (l, 0, j)),
                  pl.BlockSpec((1, 1, tn), lambda l, j: (l, 0, j))],
        out_specs=pl.BlockSpec((1, rows, tn), lambda l, j: (l, 0, j)),
        out_shape=jax.ShapeDtypeStruct((nl, rows, n6), F32),
        compiler_params=_cp("parallel", "parallel"),
        name="adaln",
    )(cond, ada_w, ada_b.reshape(nl, 1, n6))


def _head_sums(zz, ones128):
    zz = _bf(zz)
    parts = [_dot(zz[:, i:i + 128], ones128) for i in range(0, zz.shape[1], 128)]
    return jnp.concatenate(parts, axis=1) if len(parts) > 1 else parts[0]


def _head_norm(z, gain, ones128):
    ms = _head_sums(z * z, ones128) * (1.0 / HEAD_DIM)
    return z * lax.rsqrt(ms + NORM_EPS) * gain


def _rope(z, cos, sin_signed):
    w = z.shape[1]
    lane = lax.broadcasted_iota(jnp.int32, z.shape, 1)
    first = (lane % HEAD_DIM) < (HEAD_DIM // 2)
    swapped = jnp.where(first, pltpu.roll(z, w - HEAD_DIM // 2, 1), pltpu.roll(z, HEAD_DIM // 2, 1))
    return z * cos + swapped * sin_signed


def _proj_kernel(*refs, rope, split_rw):
    refs = list(refs)
    x_ref, mod_ref, n1_ref, w_ref, gqa_ref, gka_ref, gqc_ref, gkc_ref, ones_ref = refs[:9]
    qa_ref, ka_ref, va_ref, urw_ref, qc_ref, kc_ref, vc_ref = refs[-7:]
    extra = refs[9:-7]
    if rope:
        cos_ref, sin_ref = extra[:2]
    sh1 = mod_ref[0, 0:1, :]
    sc1 = mod_ref[0, 1:2, :]
    o0 = 0
    o1 = NA_W
    o2 = 2 * NA_W
    o3 = 3 * NA_W
    o4 = o3 + RW_IN_W
    o5 = o4 + SWA_W
    o6 = o5 + SWA_KV_W
    ones = ones_ref[0:128, 0:128]
    ones_kv = ones
    subs = [slice(i, i + SUB_ROWS) for i in range(0, x_ref.shape[0], SUB_ROWS)]
    hs, us = [], []
    for sl in subs:
        x = x_ref[sl, :]
        ms = jnp.mean(x * x, axis=-1, keepdims=True)
        hs.append(x * lax.rsqrt(ms + NORM_EPS) * n1_ref[...] * (1.0 + sc1) + sh1)
    h_hi = [_bf(h) for h in hs]
    us = [_dot(hh, w_ref[...]) for hh in h_hi]
    for sl, h, hh, u in zip(subs, hs, h_hi, us):
        qa = _head_norm(u[:, o0:o1], gqa_ref[...], ones)
        ka = _head_norm(u[:, o1:o2], gka_ref[...], ones)
        qc = _head_norm(u[:, o4:o5], gqc_ref[...], ones)
        kc = _head_norm(u[:, o5:o6], gkc_ref[...], ones_kv)
        if rope:
            qc = _rope(qc, cos_ref[sl, :], sin_ref[sl, :])
            kc = _rope(kc, cos_ref[sl, 0:SWA_KV_W], sin_ref[sl, 0:SWA_KV_W])
        qa_ref[sl, :] = qa.astype(qa_ref.dtype)
        ka_ref[sl, :] = ka.astype(ka_ref.dtype)
        va_ref[sl, :] = u[:, o2:o3].astype(va_ref.dtype)
        urw = u[:, o3:o4]
        if split_rw:
            h_lo = _bf(h - hh.astype(F32))
            urw = urw + _dot(h_lo, w_ref[:, o3:o4]) + _dot(hh, extra[-1][...])
        urw_ref[sl, :] = urw
        qc_ref[sl, :] = qc.astype(qc_ref.dtype)
        kc_ref[sl, :] = kc.astype(kc_ref.dtype)
        vc_ref[sl, :] = u[:, o6:].astype(vc_ref.dtype)


def _proj(x, mod, n1, w_in_bf, gains, ones384, rope_tabs, seq, per_request_mod, qkv_dtype, w_rw_lo=None):
    tokens, d = x.shape
    tm = _row_tile(seq)
    tiles_per_req = seq // tm
    in_w = w_in_bf.shape[1]
    rope = rope_tabs is not None
    mod_map = (lambda i: (i // tiles_per_req, 0, 0)) if per_request_mod else (lambda i: (0, 0, 0))
    row = lambda w: pl.BlockSpec((tm, w), lambda i: (i, 0))
    in_specs = [row(d), pl.BlockSpec((1, 6, d), mod_map), _full((1, d)), _full((d, in_w)),
                _full((1, NA_W)), _full((1, NA_W)), _full((1, SWA_W)), _full((1, SWA_KV_W)),
                _full((NA_W, NA_W))]
    args = [x, mod, n1, w_in_bf, *gains, ones384]
    if rope:
        tab = pl.BlockSpec((tm, SWA_W), lambda i: (i % tiles_per_req, 0))
        in_specs += [tab, tab]
        args += list(rope_tabs)
    if w_rw_lo is not None:
        in_specs.append(_full((d, RW_IN_W)))
        args.append(w_rw_lo)
    widths = [NA_W, NA_W, NA_W, RW_IN_W, SWA_W, SWA_KV_W, SWA_KV_W]
    dtypes = [qkv_dtype, qkv_dtype, qkv_dtype, F32, qkv_dtype, qkv_dtype, qkv_dtype]
    return pl.pallas_call(
        functools.partial(_proj_kernel, rope=rope, split_rw=w_rw_lo is not None),
        grid=(tokens // tm,),
        in_specs=in_specs,
        out_specs=[row(w) for w in widths],
        out_shape=[jax.ShapeDtypeStruct((tokens, w), dt) for w, dt in zip(widths, dtypes)],
        compiler_params=_cp("parallel"),
        name="proj",
    )(*args)


def _half_masks(width=2 * HEAD_DIM):
    lane = lax.broadcasted_iota(jnp.int32, (1, width), 1)
    return lane < HEAD_DIM, lane >= HEAD_DIM


def _swap_halves(z):
    return pltpu.roll(z, HEAD_DIM, 1)


def _ctx_attn_kernel(sink_ref, qa_ref, ka_ref, va_ref, qc_ref, kc_ref, vc_ref, oa_ref, oc_ref):
    scale = HEAD_DIM ** -0.5
    m0, m1 = _half_masks()
    masks = (m0, m1)
    for pair in range(NA_HEADS // 2):
        sl = slice(pair * 128, (pair + 1) * 128)
        qp = qa_ref[:, sl].astype(F32) * scale
        kp = _bf(ka_ref[:, sl])
        vp = _bf(va_ref[:, sl])
        outs = []
        for half in range(2):
            qm = _bf(jnp.where(masks[half], qp, 0.0))
            s = _dot_nt(qm, kp)
            m = jnp.max(s, axis=-1, keepdims=True)
            e = jnp.exp(s - m)
            l = jnp.sum(e, axis=-1, keepdims=True)
            outs.append(_dot(_bf(e), vp) / l)
        oa_ref[:, sl] = jnp.where(m0, outs[0], outs[1])
    kc = _bf(kc_ref[...])
    vc = _bf(vc_ref[...])
    group = SWA_HEADS // SWA_KV_HEADS
    for pair in range(SWA_HEADS // 2):
        sl = slice(pair * 128, (pair + 1) * 128)
        qp = qc_ref[:, sl].astype(F32) * scale
        outs = []
        for half in range(2):
            h = 2 * pair + half
            g = h // group
            qh = qp if g == half else _swap_halves(qp)
            qm = _bf(jnp.where(masks[g], qh, 0.0))
            s = _dot_nt(qm, kc)
            sk = sink_ref[h]
            m = jnp.maximum(jnp.max(s, axis=-1, keepdims=True), sk)
            e = jnp.exp(s - m)
            l = jnp.sum(e, axis=-1, keepdims=True) + jnp.exp(sk - m)
            o = _dot(_bf(e), vc) / l
            outs.append(o if g == half else _swap_halves(o))
        oc_ref[:, sl] = jnp.where(m0, outs[0], outs[1])


def _ctx_attn(sink, qa, ka, va, qc, kc, vc, seq):
    tokens = qa.shape[0]
    blk = lambda w: pl.BlockSpec((seq, w), lambda b: (b, 0))
    return pl.pallas_call(
        _ctx_attn_kernel,
        grid=(tokens // seq,),
        in_specs=[pl.BlockSpec(memory_space=pltpu.SMEM), blk(NA_W), blk(NA_W), blk(NA_W), blk(SWA_W),
                  blk(SWA_KV_W), blk(SWA_KV_W)],
        out_specs=[blk(NA_W), blk(SWA_W)],
        out_shape=[jax.ShapeDtypeStruct((tokens, NA_W), F32), jax.ShapeDtypeStruct((tokens, SWA_W), F32)],
        compiler_params=_cp("parallel"),
        name="ctx_attn",
    )(sink, qa, ka, va, qc, kc, vc)


def _na_bias_kernel(rpb_ref, o_ref):
    h = pl.program_id(0)
    nrow = 2 * NA_KH - 1
    ncol = 2 * NA_KW - 1
    width = NA_KH * GRID_W
    shape = (GRID_W, width)
    lane = lax.broadcasted_iota(jnp.int32, shape, 1)
    qc = lax.broadcasted_iota(jnp.int32, shape, 0)
    kc = lane % GRID_W
    c_start = jnp.clip(qc - NA_KW // 2, 0, GRID_W - NA_KW)
    ok = (kc >= c_start) & (kc < c_start + NA_KW)
    d_col = jnp.clip(kc - qc, 1 - NA_KW, NA_KW - 1) + NA_KW - 1
    key_row = lax.broadcasted_iota(jnp.int32, (1, width), 1) // GRID_W

    def case_body(case, carry):
        acc = jnp.zeros(shape, F32)
        for dc in range(ncol):
            val = jnp.zeros((1, width), F32)
            for i in range(NA_KH):
                val = jnp.where(key_row == i, rpb_ref[(h * nrow + case + i) * ncol + dc], val)
            acc = jnp.where(d_col == dc, val, acc)
        o_ref[0, pl.ds(case, 1)] = jnp.where(ok, acc, NEG_INF)[None]
        return carry

    lax.fori_loop(0, NA_KH, case_body, 0)


def _na_bias_table(rpb):
    nh = rpb.shape[0]
    return pl.pallas_call(
        _na_bias_kernel,
        grid=(nh,),
        in_specs=[pl.BlockSpec(memory_space=pltpu.SMEM)],
        out_specs=pl.BlockSpec((1, NA_KH, GRID_W, NA_KH * GRID_W), lambda h: (h // 2, 0, h % 2, 0)),
        out_shape=jax.ShapeDtypeStruct((nh // 2, NA_KH, 2 * GRID_W, NA_KH * GRID_W), F32),
        compiler_params=_cp("parallel"),
        name="na_bias",
    )(rpb.reshape(-1))


def _na_kernel(q_ref, k_ref, v_ref, kx_ref, vx_ref, bias_ref, o_ref, *, rows):
    scale = HEAD_DIM ** -0.5
    m0, m1 = _half_masks()
    kx = _bf(kx_ref[...])
    vx = _bf(vx_ref[...])

    def body(it, carry):
        us = range(NA_UNROLL)
        r = [it * NA_UNROLL + u for u in us]
        rs = [jnp.clip(r[u] - NA_KH // 2, 0, rows - NA_KH) for u in us]
        case = [rs[u] - r[u] + NA_KH - 1 for u in us]
        q0 = [pl.multiple_of(r[u] * GRID_W, GRID_W) for u in us]
        k0 = [pl.multiple_of(rs[u] * GRID_W, GRID_W) for u in us]
        qp = [q_ref[pl.ds(q0[u], GRID_W), :].astype(F32) * scale for u in us]
        kw = [_bf(k_ref[pl.ds(k0[u], NA_KH * GRID_W), :]) for u in us]
        vw = [_bf(v_ref[pl.ds(k0[u], NA_KH * GRID_W), :]) for u in us]
        q2 = [_bf(jnp.concatenate([jnp.where(m0, qp[u], 0.0), jnp.where(m1, qp[u], 0.0)], axis=0)) for u in us]
        s = [jnp.concatenate([_dot_nt(q2[u], kw[u]) + bias_ref[0, pl.ds(case[u], 1)][0], _dot_nt(q2[u], kx)],
                             axis=1) for u in us]
        m = [jnp.max(s[u], axis=-1, keepdims=True) for u in us]
        e = [jnp.exp(s[u] - m[u]) for u in us]
        l = [jnp.sum(e[u], axis=-1, keepdims=True) for u in us]
        o = [_dot(_bf(e[u]), jnp.concatenate([vw[u], vx], axis=0)) / l[u] for u in us]
        for u in us:
            o_ref[pl.ds(q0[u], GRID_W), :] = jnp.where(m0, o[u][0:GRID_W], o[u][GRID_W:])
        return carry

    lax.fori_loop(0, rows // NA_UNROLL, body, 0)


def _na_attn(q, k, v, kx, vx, bias, seq, past):
    tokens = q.shape[0]
    nb = tokens // seq
    rows = seq // GRID_W
    blk = pl.BlockSpec((seq, 128), lambda b, p: (b, p))
    cblk = pl.BlockSpec((past, 128), lambda b, p: (b, p))
    return pl.pallas_call(
        functools.partial(_na_kernel, rows=rows),
        grid=(nb, NA_HEADS // 2),
        in_specs=[blk, blk, blk, cblk, cblk,
                  pl.BlockSpec((1, NA_KH, 2 * GRID_W, NA_KH * GRID_W), lambda b, p: (p, 0, 0, 0))],
        out_specs=blk,
        out_shape=jax.ShapeDtypeStruct((tokens, NA_W), F32),
        compiler_params=_cp("parallel", "parallel"),
        name="na_attn",
    )(q, k, v, kx, vx, bias)


def _swa_kernel(sink_ref, q_ref, k_ref, v_ref, kx_ref, vx_ref, o_ref, *, seq):
    scale = HEAD_DIM ** -0.5
    blk = SWA_WINDOW
    m0, m1 = _half_masks()
    masks = (m0, m1)
    kx = _bf(kx_ref[...])
    vx = _bf(vx_ref[...])
    group = SWA_HEADS // SWA_KV_HEADS

    sk = []
    for g in range(SWA_KV_HEADS):
        sk.append(jnp.concatenate([jnp.full((blk, 1), sink_ref[h], F32) for h in range(g * group, (g + 1) * group)],
                                  axis=0))

    def body(it, carry):
        us = range(SWA_UNROLL)
        cs = [(u, g) for u in us for g in range(SWA_KV_HEADS)]
        nb = [it * SWA_UNROLL + u for u in us]
        ks = [pl.multiple_of(jnp.clip((nb[u] - 1) * blk, 0, seq - 3 * blk), blk) for u in us]
        q0 = [pl.multiple_of(nb[u] * blk, blk) for u in us]
        kw = [_bf(k_ref[pl.ds(ks[u], 3 * blk), :]) for u in us]
        vw = [_bf(v_ref[pl.ds(ks[u], 3 * blk), :]) for u in us]
        ok = []
        for u in us:
            qpos = q0[u] + lax.broadcasted_iota(jnp.int32, (group * blk, 1), 0) % blk
            kpos = ks[u] + lax.broadcasted_iota(jnp.int32, (1, 3 * blk), 1)
            ok.append(jnp.abs(qpos - kpos) <= SWA_WINDOW)
        qg = {}
        for u in us:
            pairs = [q_ref[pl.ds(q0[u], blk), p * 128:(p + 1) * 128].astype(F32) * scale
                     for p in range(SWA_HEADS // 2)]
            for g in range(SWA_KV_HEADS):
                qs = []
                for h in range(g * group, (g + 1) * group):
                    qh = pairs[h // 2] if h % 2 == g else _swap_halves(pairs[h // 2])
                    qs.append(jnp.where(masks[g], qh, 0.0))
                qg[u, g] = _bf(jnp.concatenate(qs, axis=0))
        sw = {c: jnp.where(ok[c[0]], _dot_nt(qg[c], kw[c[0]]), NEG_INF) for c in cs}
        sx = {c: _dot_nt(qg[c], kx) for c in cs}
        m = {c: jnp.maximum(jnp.maximum(jnp.max(sw[c], axis=-1, keepdims=True),
                                        jnp.max(sx[c], axis=-1, keepdims=True)), sk[c[1]]) for c in cs}
        ew = {c: jnp.exp(sw[c] - m[c]) for c in cs}
        ex = {c: jnp.exp(sx[c] - m[c]) for c in cs}
        l = {c: jnp.sum(ew[c], axis=-1, keepdims=True) + jnp.sum(ex[c], axis=-1, keepdims=True)
             + jnp.exp(sk[c[1]] - m[c]) for c in cs}
        o = {c: (_dot(_bf(ew[c]), vw[c[0]]) + _dot(_bf(ex[c]), vx)) / l[c] for c in cs}
        for u in us:
            head_out = []
            for g in range(SWA_KV_HEADS):
                for i in range(group):
                    h = g * group + i
                    oh = o[u, g][i * blk:(i + 1) * blk]
                    head_out.append(oh if h % 2 == g else _swap_halves(oh))
            for p in range(SWA_HEADS // 2):
                o_ref[pl.ds(q0[u], blk), p * 128:(p + 1) * 128] = jnp.where(m0, head_out[2 * p], head_out[2 * p + 1])
        return carry

    lax.fori_loop(0, seq // (blk * SWA_UNROLL), body, 0)


def _swa_attn(sink, q, k, v, kx, vx, seq, past):
    tokens = q.shape[0]
    blk = lambda w: pl.BlockSpec((seq, w), lambda b: (b, 0))
    cblk = pl.BlockSpec((past, SWA_KV_W), lambda b: (b, 0))
    return pl.pallas_call(
        functools.partial(_swa_kernel, seq=seq),
        grid=(tokens // seq,),
        in_specs=[pl.BlockSpec(memory_space=pltpu.SMEM), blk(SWA_W), blk(SWA_KV_W), blk(SWA_KV_W), cblk, cblk],
        out_specs=blk(SWA_W),
        out_shape=jax.ShapeDtypeStruct((tokens, SWA_W), F32),
        compiler_params=_cp("parallel"),
        name="swa_attn",
    )(sink, q, k, v, kx, vx)


def _rw_pre_kernel(u_ref, up_ref, un_ref, mu_ref, w0_ref, w2_ref, a0_ref, a2_ref, g2_ref, kk_ref_, ka_ref_,
                   rk_ref, ones_ref, r_o, kk_o, v_o, ld_o, kka_o, kd_o, g_o, bonus_o, *, tiles_per_req):
    i = pl.program_id(0)
    u = u_ref[...]
    tm = u.shape[0]
    rowi = lax.broadcasted_iota(jnp.int32, (tm, 1), 0)
    first = (i % tiles_per_req) == 0
    last = (i % tiles_per_req) == tiles_per_req - 1
    prev_row = jnp.where(first, 0.0, up_ref[7:8, :])
    next_row = jnp.where(last, 0.0, un_ref[0:1, :])
    prev = jnp.where(rowi == 0, prev_row, pltpu.roll(u, 1, 0))
    nxt = jnp.where(rowi == tm - 1, next_row, pltpu.roll(u, tm - 1, 0))
    us = u + mu_ref[0:1, :] * (prev - u) + mu_ref[1:2, :] * (nxt - u)
    r = us[:, 0:RW_W]
    k = us[:, RW_W:2 * RW_W]
    v = us[:, 2 * RW_W:3 * RW_W]
    wl = us[:, 3 * RW_W:3 * RW_W + 128]
    al = us[:, 3 * RW_W + 128:3 * RW_W + 256]
    gl = us[:, 3 * RW_W + 256:3 * RW_W + 384]
    z = -(w0_ref[...] + _dot(_bf(jnp.tanh(wl)), w2_ref[...]))
    softplus = jnp.maximum(z, 0.0) + jnp.log(1.0 + jnp.exp(-jnp.abs(z)))
    w = -softplus - 0.5
    ld = -jnp.exp(w)
    a = _sigmoid(a0_ref[...] + _dot(_bf(al), a2_ref[...]))
    g = _dot(_bf(_sigmoid(gl)), g2_ref[...])
    ones = ones_ref[...]
    kkr = k * kk_ref_[...]
    kk = kkr * lax.rsqrt(jnp.maximum(_dot2(kkr * kkr, ones), 1e-24))
    k_a = ka_ref_[...]
    kd_f = k * (1.0 + (a[:, 0:RW_W] - 1.0) * k_a)
    kd_b = k * (1.0 + (a[:, RW_W:] - 1.0) * k_a)
    r_o[...] = r
    kk_o[...] = kk
    v_o[...] = v
    ld_o[...] = ld
    kka_o[:, 0:RW_W] = kk * a[:, 0:RW_W]
    kka_o[:, RW_W:] = kk * a[:, RW_W:]
    kd_o[:, 0:RW_W] = kd_f
    kd_o[:, RW_W:] = kd_b
    g_o[...] = g
    bonus_o[...] = _dot2(r * (kd_f + kd_b) * rk_ref[...], ones) * v


def _rw_pre(urw, p, seq):
    tokens = urw.shape[0]
    tm = _row_tile(seq)
    tpr = seq // tm
    nt = tokens // tm
    r8 = tm // 8
    row = lambda w: pl.BlockSpec((tm, w), lambda i: (i, 0))
    in_specs = [row(RW_IN_W),
                pl.BlockSpec((8, RW_IN_W), lambda i: (jnp.maximum(i * r8 - 1, 0), 0)),
                pl.BlockSpec((8, RW_IN_W), lambda i: (jnp.minimum((i + 1) * r8, nt * r8 - 1), 0)),
                _full((2, RW_IN_W)), _full((1, 2 * RW_W)), _full((128, 2 * RW_W)), _full((1, 2 * RW_W)),
                _full((128, 2 * RW_W)), _full((128, RW_W)), _full((1, RW_W)), _full((1, RW_W)),
                _full((1, RW_W)), _full((RW_W, RW_W))]
    widths = [RW_W, RW_W, RW_W, 2 * RW_W, 2 * RW_W, 2 * RW_W, RW_W, RW_W]
    return pl.pallas_call(
        functools.partial(_rw_pre_kernel, tiles_per_req=tpr),
        grid=(nt,),
        in_specs=in_specs,
        out_specs=[row(w) for w in widths],
        out_shape=[jax.ShapeDtypeStruct((tokens, w), F32) for w in widths],
        compiler_params=_cp("parallel"),
        name="rw_pre",
    )(urw, urw, urw, p['mu'], p['w0'], p['w2'], p['a0'], p['a2'], p['g2'], p['k_k'], p['k_a'], p['r_k'],
      p['ones256'])


def _rw_masks():
    t = RW_CHUNK
    n = RW_HEADS * t
    tt = np.arange(t)[:, None]
    ss = (np.arange(n) % t)[None, :]
    before = np.stack([ss < tt, ss > tt])
    diag = (ss == tt)
    strict = before.astype(np.float32)
    incl = (before | diag[None]).astype(np.float32)
    eye = diag.astype(np.float32)
    ti = np.arange(t)
    tri = np.stack([ti[None, :] <= ti[:, None], ti[None, :] >= ti[:, None]]).astype(np.float32)
    hd = np.arange(n) // t
    same = (hd[:, None] == hd[None, :]).astype(np.float32)
    return (jnp.asarray(strict), jnp.asarray(incl), jnp.asarray(tri, dtype=BF16), jnp.asarray(same),
            jnp.asarray(eye))


def _rw_scan_kernel(r_ref, kk_ref, v_ref, ld_ref, kka_ref, kd_ref, s0_ref, strict_ref, incl_ref, tri_ref,
                    same_ref, eye_ref, o_ref, sfin_ref, s_scr, *, nsub):
    d = pl.program_id(1)
    c = pl.program_id(2)
    t = RW_CHUNK
    n = RW_HEADS * t

    @pl.when(c == 0)
    def _():
        s_scr[...] = s0_ref[0, 0]

    strict = strict_ref[0]
    incl = incl_ref[0]
    tri = tri_ref[0]
    eye = eye_ref[...]
    same = same_ref[...]
    same_bf = _bf(same)

    def bd(x):
        return jnp.concatenate([_bf(x)] * RW_HEADS, axis=0) * same_bf

    js = range(nsub)
    rows = [pl.ds(pl.multiple_of((j + d * (nsub - 1 - 2 * j)) * t, t), t) for j in js]
    ld = [ld_ref[rows[j], :] for j in js]
    cum = []
    for j in js:
        lhi, lmid, llo = _split3(ld[j])
        cum.append(_dot(tri, lhi) + _dot(tri, lmid) + _dot(tri, llo))
    cend = [jnp.sum(ld[j], axis=0, keepdims=True) for j in js]
    kka = [kka_ref[rows[j], :] for j in js]
    kd = [kd_ref[rows[j], :] for j in js]
    v = [v_ref[rows[j], :] for j in js]
    at = [-kk_ref[rows[j], :] * jnp.exp(cum[j] - ld[j]) for j in js]
    rt = [r_ref[rows[j], :] * jnp.exp(cum[j]) for j in js]
    e_inv = [jnp.exp(-cum[j]) for j in js]
    aa = [_dot_nt(_bf(jnp.concatenate([at[j], rt[j]], axis=0)),
                  jnp.concatenate([bd(kka[j] * e_inv[j]), bd(kd[j] * e_inv[j])], axis=0)) for j in js]
    a_ab = [aa[j][0:t, 0:n] * strict for j in js]
    x = [eye + a_ab[j] for j in js]
    pw = a_ab
    for _ in range(RW_CHUNK.bit_length() - 3):
        pw = [_dot(_bf(pw[j]), bd(pw[j])) for j in js]
        x = [x[j] + _dot(_bf(x[j]), bd(pw[j])) for j in js]
    xs = [_split2(x[j]) for j in js]
    sa = [_split2(a_ab[j]) for j in js]
    ax = [_dot(sa[j][0], bd(xs[j][0])) + _dot(sa[j][1], bd(xs[j][0])) + _dot(sa[j][0], bd(xs[j][1])) for j in js]
    x = [x[j] + _dot(xs[j][0], bd(eye - x[j] + ax[j])) for j in js]
    v_bd = [bd(v[j]) for j in js]
    wv = [_dot(_bf(aa[j][0:t, n:] * strict), v_bd[j]) for j in js]
    mu = [_dot(_bf(x[j]), jnp.concatenate([bd(at[j]), bd(wv[j])], axis=1)) for j in js]
    m1 = [mu[j][:, 0:n] for j in js]
    u0 = [mu[j][:, n:] for j in js]
    e_end = [jnp.exp(cend[j] - cum[j]) for j in js]
    bend = [_bf(kka[j] * e_end[j]) for j in js]
    g = [_bf(_dot_tn(_bf(m1[j]), bend[j]) * same) for j in js]
    cst = []
    for j in js:
        full = _dot_tn(_bf(jnp.concatenate([u0[j], v[j]], axis=0)),
                       jnp.concatenate([bend[j], _bf(kd[j] * e_end[j])], axis=0)) * same
        cst.append(functools.reduce(jnp.add, [full[h * t:(h + 1) * t] for h in range(RW_HEADS)]))
    qo = [_dot(_bf(aa[j][t:, 0:n] * incl), jnp.concatenate([bd(m1[j]), bd(u0[j])], axis=1)) for j in js]
    q = [_bf(rt[j] + qo[j][:, 0:n]) for j in js]
    o0 = [qo[j][:, n:] + _dot(_bf(aa[j][t:, n:] * incl), v_bd[j]) for j in js]

    s = s_scr[...]
    for j in js:
        o_ref[0, rows[j], :] = _dot_nt(q[j], bd(s)) + o0[j]
        s = s * jnp.exp(cend[j]) + _dot(_bf(s), g[j]) + cst[j]
    s_scr[...] = s

    @pl.when(c == pl.num_programs(2) - 1)
    def _():
        sfin_ref[0, 0] = s


def _rw_scan(r, kk, v, ld, kka, kd, s0_bd, consts, seq):
    tokens = r.shape[0]
    nreq = tokens // seq
    tb = min(seq, 512)
    nblk = seq // tb
    nsub = tb // RW_CHUNK
    n = RW_HEADS * RW_CHUNK
    cc = lambda d, c: c + d * (nblk - 1 - 2 * c)
    shared = pl.BlockSpec((tb, RW_W), lambda b, d, c: (b * nblk + cc(d, c), 0))
    dirw = pl.BlockSpec((tb, RW_W), lambda b, d, c: (b * nblk + cc(d, c), d))
    strict, incl, tri, same, eye = consts
    return pl.pallas_call(
        functools.partial(_rw_scan_kernel, nsub=nsub),
        grid=(nreq, 2, nblk),
        in_specs=[shared, shared, shared, dirw, dirw, dirw,
                  pl.BlockSpec((1, 1, HEAD_DIM, n), lambda b, d, c: (b, d, 0, 0)),
                  pl.BlockSpec((1, RW_CHUNK, n), lambda b, d, c: (d, 0, 0)),
                  pl.BlockSpec((1, RW_CHUNK, n), lambda b, d, c: (d, 0, 0)),
                  pl.BlockSpec((1, RW_CHUNK, RW_CHUNK), lambda b, d, c: (d, 0, 0)),
                  _full((n, n)), _full((RW_CHUNK, n))],
        out_specs=[pl.BlockSpec((1, tb, RW_W), lambda b, d, c: (d, b * nblk + cc(d, c), 0)),
                   pl.BlockSpec((1, 1, HEAD_DIM, n), lambda b, d, c: (b, d, 0, 0))],
        out_shape=[jax.ShapeDtypeStruct((2, tokens, RW_W), F32),
                   jax.ShapeDtypeStruct((nreq, 2, HEAD_DIM, n), F32)],
        scratch_shapes=[pltpu.VMEM((HEAD_DIM, n), F32)],
        compiler_params=_cp("parallel", "parallel", "arbitrary"),
        name="rw_scan",
    )(r, kk, v, ld, kka, kd, s0_bd, strict, incl, tri, same, eye)


def _state_to_lanes(s):
    b = s.shape[0]
    return jnp.transpose(s, (0, 1, 3, 2, 4)).reshape(b, 2, HEAD_DIM, RW_W)


def _lanes_to_state(slc):
    b = slc.shape[0]
    return jnp.transpose(slc.reshape(b, 2, HEAD_DIM, RW_HEADS, HEAD_DIM), (0, 1, 3, 2, 4))


def _finish_kernel(x_ref, oa_ref, oc_ref, o2_ref, bonus_ref, g_ref, mod_ref, wout_ref, lng_ref, lnb_ref,
                   n2_ref, wr_hi_ref, wr_lo_ref, ones_ref, x1_ref, h2_ref, aff_ref):
    ones = ones_ref[0:128, 0:128]
    g1 = mod_ref[0, 2:3, :]
    sh2 = mod_ref[0, 3:4, :]
    sc2 = mod_ref[0, 4:5, :]
    subs = [slice(i, i + SUB_ROWS) for i in range(0, x_ref.shape[0], SUB_ROWS)]
    mixins = []
    for sl in subs:
        y = o2_ref[0, sl, :] + o2_ref[1, sl, :]
        mu = _head_sums(y, ones) * (1.0 / HEAD_DIM)
        yc = y - mu
        var = _head_sums(yc * yc, ones) * (1.0 / HEAD_DIM)
        yn = yc * lax.rsqrt(var + GN_EPS) * lng_ref[...] + lnb_ref[...]
        ob = (yn + bonus_ref[sl, :]) * g_ref[sl, :]
        mixins.append(jnp.concatenate([_bf(oa_ref[sl, :]), _bf(ob), _bf(oc_ref[sl, :])], axis=1))
    mixes = [_dot(mixin, wout_ref[...]) for mixin in mixins]
    for sl, mix in zip(subs, mixes):
        x1 = x_ref[sl, :] + g1 * mix
        ms = jnp.mean(x1 * x1, axis=-1, keepdims=True)
        h2 = x1 * lax.rsqrt(ms + NORM_EPS) * n2_ref[...] * (1.0 + sc2) + sh2
        x1_ref[sl, :] = x1
        h2_ref[sl, :] = _bf(h2)
        hhi, hlo = _split2(h2)
        logits = _dot(hhi, wr_hi_ref[...]) + _dot(hlo, wr_hi_ref[...]) + _dot(hhi, wr_lo_ref[...])
        m = jnp.max(logits, axis=-1, keepdims=True)
        e = jnp.exp(logits - m)
        aff_ref[sl, :] = e / jnp.sum(e, axis=-1, keepdims=True)


def _finish(x, oa, oc, o2, bonus, g, mod, p, seq, per_request_mod):
    tokens, d = x.shape
    tm = _row_tile(seq)
    tpr = seq // tm
    mod_map = (lambda i: (i // tpr, 0, 0)) if per_request_mod else (lambda i: (0, 0, 0))
    row = lambda w: pl.BlockSpec((tm, w), lambda i: (i, 0))
    return pl.pallas_call(
        _finish_kernel,
        grid=(tokens // tm,),
        in_specs=[row(d), row(NA_W), row(SWA_W), pl.BlockSpec((2, tm, RW_W), lambda i: (0, i, 0)), row(RW_W),
                  row(RW_W), pl.BlockSpec((1, 6, d), mod_map), _full((d, d)), _full((1, RW_W)),
                  _full((1, RW_W)), _full((1, d)), _full((d, N_EXPERTS)), _full((d, N_EXPERTS)),
                  _full((RW_W, RW_W))],
        out_specs=[row(d), row(d), row(N_EXPERTS)],
        out_shape=[jax.ShapeDtypeStruct((tokens, d), F32), jax.ShapeDtypeStruct((tokens, d), BF16),
                   jax.ShapeDtypeStruct((tokens, N_EXPERTS), F32)],
        compiler_params=_cp("parallel"),
        name="finish",
    )(x, oa, oc, o2, bonus, g, mod, p['w_out'], p['ln_g'], p['ln_b'], p['n2'], p['wr_hi'], p['wr_lo'],
      p['ones256'])


def _topk_kernel(aff_ref, tri_ref, eye_ref, place_ref, slot_ref, slotrow_ref, gfull_ref, ends_ref, *, cap, group,
                 seq, tb):
    aff = aff_ref[...]
    bits = lax.bitcast_convert_type(aff, jnp.int32)
    capf = jnp.float32(cap)
    eye = eye_ref[...]
    ghi, gmid, glo = _split3(aff)
    aff_t = _dot_nt(eye, ghi) + _dot_nt(eye, gmid) + _dot_nt(eye, glo)
    bits_t = lax.bitcast_convert_type(aff_t, jnp.int32)
    rs = range(group)

    def bis(_, carry):
        los, his = carry
        nlo, nhi = [], []
        for r in rs:
            mid = los[r] + ((his[r] - los[r] + 1) >> 1)
            cnt = jnp.sum(jnp.where(bits_t[:, r * seq:(r + 1) * seq] >= mid, 1.0, 0.0), axis=1, keepdims=True)
            ge = cnt >= capf
            nlo.append(jnp.where(ge, mid, los[r]))
            nhi.append(jnp.where(ge, his[r], mid - 1))
        return tuple(nlo), tuple(nhi)

    lo0 = tuple(jnp.zeros((N_EXPERTS, 1), jnp.int32) for _ in rs)
    hi0 = tuple(jnp.full((N_EXPERTS, 1), 0x7F7FFFFF, jnp.int32) for _ in rs)
    thr_cols, _ = lax.fori_loop(0, 31, bis, (lo0, hi0))
    ri = lax.broadcasted_iota(jnp.int32, (N_EXPERTS, N_EXPERTS), 0)
    ci = lax.broadcasted_iota(jnp.int32, (N_EXPERTS, N_EXPERTS), 1)
    tri = tri_ref[...]
    for r in rs:
        thr = jnp.sum(jnp.where(ri == ci, thr_cols[r], 0), axis=0, keepdims=True)
        rows = slice(r * seq, (r + 1) * seq)
        gt = jnp.where(bits[rows] > thr, 1.0, 0.0)
        eq = jnp.where(bits[rows] == thr, 1.0, 0.0)
        need = capf - jnp.sum(gt, axis=0, keepdims=True)
        offset = float(r * cap)
        carry_g = jnp.zeros((1, N_EXPERTS), F32)
        carry_e = jnp.zeros((1, N_EXPERTS), F32)
        for blk in range(seq // tb):
            sl = slice(blk * tb, (blk + 1) * tb)
            out = slice(r * seq + blk * tb, r * seq + (blk + 1) * tb)
            pg = _dot(tri, _bf(gt[sl])) + carry_g
            pe = _dot(tri, _bf(eq[sl])) + carry_e
            carry_g = pg[tb - 1:tb, :]
            carry_e = pe[tb - 1:tb, :]
            sel = gt[sl] + eq[sl] * jnp.where(pe <= need, 1.0, 0.0)
            slot = jnp.where(sel > 0.5, pg + jnp.minimum(pe, need) - 1.0 + offset, -1.0)
            slot_ref[out, :] = slot
            ends_ref[r, blk:blk + 1, :] = carry_g + jnp.minimum(carry_e, need) + offset
            shi, slo = _split2(slot)
            slotrow_ref[0, :, 0, out] = _dot_nt(eye, shi) + _dot_nt(eye, slo)
            gfull_ref[out, :] = _bf(_dot(ghi[out], place_ref[0]) + _dot(gmid[out], place_ref[1])
                                    + _dot(glo[out], place_ref[2]))


def _topk(aff, seq, group):
    tokens = aff.shape[0]
    nreq = tokens // seq
    cap = EC_CAPACITY * seq // N_EXPERTS
    tb = min(seq, 512)
    ti = np.arange(tb)
    tri = jnp.asarray((ti[None, :] <= ti[:, None]).astype(np.float32), dtype=BF16)
    eye = jnp.asarray(np.eye(N_EXPERTS, dtype=np.float32), dtype=BF16)
    place = np.zeros((3, N_EXPERTS, 128), np.float32)
    for s in range(3):
        place[s, np.arange(N_EXPERTS), s * N_EXPERTS + np.arange(N_EXPERTS)] = 1.0
    place = jnp.asarray(place, dtype=BF16)
    nblk = seq // tb
    slot, slotrow, gfull, ends = pl.pallas_call(
        functools.partial(_topk_kernel, cap=cap, group=group, seq=seq, tb=tb),
        grid=(nreq // group,),
        in_specs=[pl.BlockSpec((group * seq, N_EXPERTS), lambda b: (b, 0)), _full((tb, tb)),
                  _full((N_EXPERTS, N_EXPERTS)), _full((3, N_EXPERTS, 128))],
        out_specs=[pl.BlockSpec((group * seq, N_EXPERTS), lambda b: (b, 0)),
                   pl.BlockSpec((1, N_EXPERTS, 1, group * seq), lambda b: (b, 0, 0, 0)),
                   pl.BlockSpec((group * seq, 128), lambda b: (b, 0)),
                   pl.BlockSpec((group, nblk, N_EXPERTS), lambda b: (b, 0, 0))],
        out_shape=[jax.ShapeDtypeStruct((tokens, N_EXPERTS), F32),
                   jax.ShapeDtypeStruct((nreq // group, N_EXPERTS, 1, group * seq), F32),
                   jax.ShapeDtypeStruct((tokens, 128), BF16),
                   jax.ShapeDtypeStruct((nreq, nblk, N_EXPERTS), F32)],
        compiler_params=_cp("parallel"),
        name="topk",
    )(aff, tri, eye, place)
    ends = ends.reshape(nreq // group, group * nblk, N_EXPERTS).transpose(0, 2, 1)
    return slot, slotrow, gfull, ends.astype(jnp.int32).reshape(-1), tb


MOE_EB = 8


def _moe_dispatch_kernel(ends_ref, h_ref, slotrow_ref, gfull_ref, xe_ref, gs_ref, *, ct, nch):
    gi = pl.program_id(0)
    eb = pl.program_id(1)
    c = pl.program_id(2)
    mt = 128

    @pl.when(c == 0)
    def _():
        xe_ref[...] = jnp.zeros_like(xe_ref)
        gs_ref[...] = jnp.zeros_like(gs_ref)

    jcol = lax.broadcasted_iota(jnp.int32, (mt, 1), 0)
    starts, his, pieces = [], [], []
    for i in range(MOE_EB):
        base = (gi * N_EXPERTS + eb * MOE_EB + i) * nch
        lo = jnp.where(c == 0, 0, ends_ref[base + jnp.maximum(c - 1, 0)])
        his.append(ends_ref[base + c])
        start = pl.multiple_of(jnp.minimum((lo // 16) * 16, ct - mt), 16)
        starts.append(start)
        pieces.append(_bf(jnp.where(slotrow_ref[0, i] == (jcol + start).astype(F32), 1.0, 0.0)))
    onehot = jnp.concatenate(pieces, axis=0)
    xw = _bf(_dot(onehot, h_ref[...]))
    gw = _bf(_dot(onehot, gfull_ref[...]))
    for i in range(MOE_EB):
        rows = pl.ds(starts[i], mt)
        xe_ref[0, i, rows, :] += xw[i * mt:(i + 1) * mt]
        gs_ref[0, i, rows, :] += gw[i * mt:(i + 1) * mt]
    for i in range(MOE_EB):
        for w in range(1, ct // mt):
            wlo = starts[i] + w * mt

            @pl.when(wlo < his[i])
            def _(i=i, wlo=wlo):
                ws = pl.multiple_of(jnp.minimum(wlo, ct - mt), 16)
                slot = slotrow_ref[0, i]
                hit = (jnp.where(slot == (jcol + ws).astype(F32), 1.0, 0.0)
                       * jnp.where(slot >= wlo.astype(F32), 1.0, 0.0))
                rows = pl.ds(ws, mt)
                xe_ref[0, i, rows, :] += _bf(_dot(_bf(hit), h_ref[...]))
                gs_ref[0, i, rows, :] += _bf(_dot(_bf(hit), gfull_ref[...]))


def _moe_dispatch(ends, h2, slotrow, gfull, lg, ct, kc):
    tokens, d = h2.shape
    ngrp = tokens // lg
    nch = lg // kc
    grid_spec = pltpu.PrefetchScalarGridSpec(
        num_scalar_prefetch=1,
        grid=(ngrp, N_EXPERTS // MOE_EB, nch),
        in_specs=[pl.BlockSpec((kc, d), lambda gi, eb, c, ends: (gi * nch + c, 0)),
                  pl.BlockSpec((1, MOE_EB, 1, kc), lambda gi, eb, c, ends: (gi, eb, 0, c)),
                  pl.BlockSpec((kc, 128), lambda gi, eb, c, ends: (gi * nch + c, 0))],
        out_specs=[pl.BlockSpec((1, MOE_EB, ct, d), lambda gi, eb, c, ends: (gi, eb, 0, 0)),
                   pl.BlockSpec((1, MOE_EB, ct, 128), lambda gi, eb, c, ends: (gi, eb, 0, 0))])
    return pl.pallas_call(
        functools.partial(_moe_dispatch_kernel, ct=ct, nch=nch),
        grid_spec=grid_spec,
        out_shape=[jax.ShapeDtypeStruct((ngrp, N_EXPERTS, ct, d), BF16),
                   jax.ShapeDtypeStruct((ngrp, N_EXPERTS, ct, 128), BF16)],
        compiler_params=_cp("parallel", "parallel", "arbitrary"),
        name="moe_dispatch",
    )(ends, h2, slotrow, gfull)


def _moe_ffn_kernel(xe_ref, gs_ref, mod_ref, wg_ref, wu_ref, wd_ref, ye_ref):
    e = pl.program_id(1)
    lane = lax.broadcasted_iota(jnp.int32, (1, 128), 1)
    pick = (lane == e) | (lane == e + N_EXPERTS) | (lane == e + 2 * N_EXPERTS)
    gate = jnp.sum(jnp.where(pick, gs_ref[0, 0].astype(F32), 0.0), axis=-1, keepdims=True)
    xb = xe_ref[0, 0]
    hg = _dot(xb, wg_ref[0, 0])
    hu = _dot(xb, wu_ref[0, 0])
    he = _bf(hg * _sigmoid(hg) * hu)
    y = _dot(he, wd_ref[0, 0])
    ye_ref[0, 0] = _bf(y * gate * mod_ref[0, 5:6, :])


def _moe_ffn(xe, gs, mod, wg, wu, wd, layer, per_group_mod):
    ngrp, _, ct, d = xe.shape
    f = wg.shape[3]
    mod_map = (lambda gi, e: (gi, 0, 0)) if per_group_mod else (lambda gi, e: (0, 0, 0))
    return pl.pallas_call(
        _moe_ffn_kernel,
        grid=(ngrp, N_EXPERTS),
        in_specs=[pl.BlockSpec((1, 1, ct, d), lambda gi, e: (gi, e, 0, 0)),
                  pl.BlockSpec((1, 1, ct, 128), lambda gi, e: (gi, e, 0, 0)),
                  pl.BlockSpec((1, 6, d), mod_map),
                  pl.BlockSpec((1, 1, d, f), lambda gi, e: (layer, e, 0, 0)),
                  pl.BlockSpec((1, 1, d, f), lambda gi, e: (layer, e, 0, 0)),
                  pl.BlockSpec((1, 1, f, d), lambda gi, e: (layer, e, 0, 0))],
        out_specs=pl.BlockSpec((1, 1, ct, d), lambda gi, e: (gi, e, 0, 0)),
        out_shape=jax.ShapeDtypeStruct((ngrp, N_EXPERTS, ct, d), BF16),
        compiler_params=_cp("parallel", "parallel"),
        name="moe_ffn",
    )(xe, gs, mod, wg, wu, wd)


def _moe_combine_kernel(ends_ref, x1_ref, slot_ref, ye_ref, o_ref, win_scr, *, ct, nch):
    gi = pl.program_id(0)
    j = pl.program_id(1)
    mt = 128
    wide = N_EXPERTS * mt
    shi, slo = _split2(slot_ref[...])
    col_e = lax.broadcasted_iota(jnp.int32, (N_EXPERTS, wide), 1) // mt
    row_e = lax.broadcasted_iota(jnp.int32, (N_EXPERTS, wide), 0)
    expand = _bf(jnp.where(col_e == row_e, 1.0, 0.0))
    sb = _dot(shi, expand) + _dot(slo, expand)
    lane = lax.broadcasted_iota(jnp.int32, (1, mt), 1)
    starts, his, targets = [], [], []
    for e in range(N_EXPERTS):
        base = (gi * N_EXPERTS + e) * nch
        lo = jnp.where(j == 0, 0, ends_ref[base + jnp.maximum(j - 1, 0)])
        his.append(ends_ref[base + j])
        start = pl.multiple_of(jnp.minimum((lo // 16) * 16, ct - mt), 16)
        win_scr[e * mt:(e + 1) * mt, :] = ye_ref[0, e, pl.ds(start, mt), :]
        starts.append(start)
        targets.append((lane + start).astype(F32))
    onehot = _bf(jnp.where(sb == jnp.concatenate(targets, axis=1), 1.0, 0.0))
    o_ref[...] = x1_ref[...] + _dot(onehot, win_scr[...])
    for e in range(N_EXPERTS):
        for w in range(1, ct // mt):
            wlo = starts[e] + w * mt

            @pl.when(wlo < his[e])
            def _(e=e, wlo=wlo):
                ws = pl.multiple_of(jnp.minimum(wlo, ct - mt), 16)
                sbe = sb[:, e * mt:(e + 1) * mt]
                hit = jnp.where(sbe == (lane + ws).astype(F32), 1.0, 0.0) * jnp.where(sbe >= wlo.astype(F32), 1.0, 0.0)
                o_ref[...] += _dot(_bf(hit), ye_ref[0, e, pl.ds(ws, mt), :])


def _moe_combine(ends, x1, slot, ye, lg, ct, kc):
    tokens, d = x1.shape
    ngrp = tokens // lg
    nch = lg // kc
    grid_spec = pltpu.PrefetchScalarGridSpec(
        num_scalar_prefetch=1,
        grid=(ngrp, nch),
        in_specs=[pl.BlockSpec((kc, d), lambda gi, j, ends: (gi * nch + j, 0)),
                  pl.BlockSpec((kc, N_EXPERTS), lambda gi, j, ends: (gi * nch + j, 0)),
                  pl.BlockSpec((1, N_EXPERTS, ct, d), lambda gi, j, ends: (gi, 0, 0, 0),
                               pipeline_mode=pl.Buffered(1))],
        out_specs=pl.BlockSpec((kc, d), lambda gi, j, ends: (gi * nch + j, 0)),
        scratch_shapes=[pltpu.VMEM((N_EXPERTS * 128, d), BF16)])
    return pl.pallas_call(
        functools.partial(_moe_combine_kernel, ct=ct, nch=nch),
        grid_spec=grid_spec,
        out_shape=jax.ShapeDtypeStruct((tokens, d), F32),
        compiler_params=_cp("parallel", "arbitrary"),
        name="moe_combine",
    )(ends, x1, slot, ye)


def _rope_tables(seq):
    t = np.arange(seq)
    n_freq = HEAD_DIM // 4
    inv = ROPE_THETA ** (-np.arange(n_freq, dtype=np.float32) / n_freq)
    ang = np.concatenate([(t // GRID_W).astype(np.float32)[:, None] * inv,
                          (t % GRID_W).astype(np.float32)[:, None] * inv], axis=-1)
    ang = jnp.asarray(ang, dtype=F32)
    cos, sin = jnp.cos(ang), jnp.sin(ang)
    cos_t = jnp.tile(jnp.concatenate([cos, cos], axis=-1), (1, SWA_HEADS))
    sin_t = jnp.tile(jnp.concatenate([-sin, sin], axis=-1), (1, SWA_HEADS))
    return cos_t, sin_t


def _blockdiag2(w):
    z = jnp.zeros_like(w[0])
    return jnp.concatenate([jnp.concatenate([w[0], z], axis=1), jnp.concatenate([z, w[1]], axis=1)], axis=0)


def _layer_params(l, ada_w, ada_b, norm1_g, norm2_g, w_in, na_q_norm, na_k_norm, na_rpb, rw_mu, rw_w0, rw_w2,
                  rw_a0, rw_a2, rw_g2, rw_k_k, rw_k_a, rw_r_k, rw_ln_g, rw_ln_b, swa_q_norm, swa_k_norm,
                  swa_sink, w_out, w_router, w_gate, w_up, w_down):
    wr = w_router[l]
    wr_hi = wr.astype(BF16)
    w_rw = w_in[l][:, 3 * NA_W:3 * NA_W + RW_IN_W]
    return {
        'n1': norm1_g[l][None], 'n2': norm2_g[l][None], 'w_in': w_in[l].astype(BF16),
        'w_rw_lo': (w_rw - w_rw.astype(BF16).astype(F32)).astype(BF16),
        'gains': (jnp.tile(na_q_norm[l], NA_HEADS)[None], jnp.tile(na_k_norm[l], NA_HEADS)[None],
                  jnp.tile(swa_q_norm[l], SWA_HEADS)[None], jnp.tile(swa_k_norm[l], SWA_KV_HEADS)[None]),
        'bias': _na_bias_table(na_rpb[l]),
        'mu': rw_mu[l], 'w0': rw_w0[l].reshape(1, 2 * RW_W), 'w2': _blockdiag2(rw_w2[l]).astype(BF16),
        'a0': rw_a0[l].reshape(1, 2 * RW_W), 'a2': _blockdiag2(rw_a2[l]).astype(BF16),
        'g2': rw_g2[l].astype(BF16), 'k_k': rw_k_k[l][None], 'k_a': rw_k_a[l][None],
        'r_k': rw_r_k[l].reshape(1, RW_W), 'ln_g': rw_ln_g[l][None], 'ln_b': rw_ln_b[l][None],
        'sink': swa_sink[l], 'w_out': w_out[l].astype(BF16),
        'wr_hi': wr_hi, 'wr_lo': (wr - wr_hi.astype(F32)).astype(BF16),
        'wg': w_gate, 'wu': w_up, 'wd': w_down, 'layer': l,
        'ones256': _block_ones(RW_W),
    }


def _mix_and_ffn(x, mod, p, oa, oc, urw, s0_bd, scan_consts, seq, per_request_mod, group):
    r, kk, v, ld, kka, kd, g, bonus = _rw_pre(urw, p, seq)
    o2, sfin = _rw_scan(r, kk, v, ld, kka, kd, s0_bd, scan_consts, seq)
    x1, h2, aff = _finish(x, oa, oc, o2, bonus, g, mod, p, seq, per_request_mod)
    slot, slotrow, gfull, ends, kc = _topk(aff, seq, group)
    cap = EC_CAPACITY * seq // N_EXPERTS
    lg, ct = group * seq, group * cap
    xe, gs = _moe_dispatch(ends, h2, slotrow, gfull, lg, ct, kc)
    ye = _moe_ffn(xe, gs, mod, p['wg'], p['wu'], p['wd'], p['layer'], per_request_mod)
    return _moe_combine(ends, x1, slot, ye, lg, ct, kc), sfin


def _context_layer(x, mod, p, ones384, scan_consts, seq):
    qa, ka, va, urw, qc, kc, vc = _proj(x, mod, p['n1'], p['w_in'], p['gains'], ones384, None, seq, False, F32,
                                        w_rw_lo=p['w_rw_lo'])
    oa, oc = _ctx_attn(p['sink'], qa, ka, va, qc, kc, vc, seq)
    nreq = x.shape[0] // seq
    s0 = jnp.zeros((nreq, 2, HEAD_DIM, RW_W), F32)
    y, sfin = _mix_and_ffn(x, mod, p, oa, oc, urw, s0, scan_consts, seq, False, CTX_GROUP)
    return y, ka, va, kc, vc, sfin


def _latent_layer(x, mod, p, ones384, scan_consts, rope_tabs, seq, kx_na, vx_na, kx_swa, vx_swa, s0_bd, past):
    qa, ka, va, urw, qc, kc, vc = _proj(x, mod, p['n1'], p['w_in'], p['gains'], ones384, rope_tabs, seq, True,
                                        BF16)
    oa = _na_attn(qa, ka, va, kx_na, vx_na, p['bias'], seq, past)
    oc = _swa_attn(p['sink'], qc, kc, vc, kx_swa, vx_swa, seq, past)
    y, _ = _mix_and_ffn(x, mod, p, oa, oc, urw, s0_bd, scan_consts, seq, True, 1)
    return y


def _heads_first(z, nreq, seq, heads):
    return z.reshape(nreq, seq, heads, HEAD_DIM).transpose(0, 2, 1, 3)


def _tokens_first(z):
    b, h, n, dh = z.shape
    return z.transpose(0, 2, 1, 3).reshape(b * n, h * dh)


def kernel(x_prompt, x_sample, cache_na_k, cache_na_v, cache_swa_k, cache_swa_v, state_rwkv, c, c_ctx, ada_w, ada_b, norm1_g, norm2_g, w_in, na_q_norm, na_k_norm, na_rpb, rw_mu, rw_w0, rw_w2, rw_a0, rw_a2, rw_g2, rw_k_k, rw_k_a, rw_r_k, rw_ln_g, rw_ln_b, swa_q_norm, swa_k_norm, swa_sink, w_out, w_router, w_gate, w_up, w_down):
    nb, seq, d = x_prompt.shape
    db, dseq, _ = x_sample.shape
    depth = ada_w.shape[0]
    past = cache_na_k.shape[3]
    cond = jnp.concatenate([c, c_ctx[None], jnp.zeros((16 - db - 1, d), F32)], axis=0)
    mod_all = _adaln(cond, ada_w, ada_b).reshape(depth, 16, 6, d)
    ones384 = _block_ones(NA_W)
    scan_consts = _rw_masks()
    rope_tabs = _rope_tables(dseq)
    xp = x_prompt.reshape(nb * seq, d)
    xs = x_sample.reshape(db * dseq, d)
    new_ka, new_va, new_kc, new_vc, new_s = [], [], [], [], []
    wg_bf, wu_bf, wd_bf = w_gate.astype(BF16), w_up.astype(BF16), w_down.astype(BF16)
    for l in range(depth):
        p = _layer_params(l, ada_w, ada_b, norm1_g, norm2_g, w_in, na_q_norm, na_k_norm, na_rpb, rw_mu, rw_w0,
                          rw_w2, rw_a0, rw_a2, rw_g2, rw_k_k, rw_k_a, rw_r_k, rw_ln_g, rw_ln_b, swa_q_norm,
                          swa_k_norm, swa_sink, w_out, w_router, wg_bf, wu_bf, wd_bf)
        mod_ctx = mod_all[l, db:db + 1]
        mod_lat = mod_all[l, 0:db]
        xp, ka, va, kc, vc, sfin = _context_layer(xp, mod_ctx, p, ones384, scan_consts, seq)
        new_ka.append(_heads_first(ka, nb, seq, NA_HEADS))
        new_va.append(_heads_first(va, nb, seq, NA_HEADS))
        new_kc.append(_heads_first(kc, nb, seq, SWA_KV_HEADS))
        new_vc.append(_heads_first(vc, nb, seq, SWA_KV_HEADS))
        new_s.append(_lanes_to_state(sfin))
        xs = _latent_layer(xs, mod_lat, p, ones384, scan_consts, rope_tabs, dseq,
                           _tokens_first(cache_na_k[:, l]), _tokens_first(cache_na_v[:, l]),
                           _tokens_first(cache_swa_k[:, l]), _tokens_first(cache_swa_v[:, l]),
                           _state_to_lanes(state_rwkv[:, l]), past)
    return (xp.reshape(nb, seq, d), xs.reshape(db, dseq, d), jnp.stack(new_ka, axis=1),
            jnp.stack(new_va, axis=1), jnp.stack(new_kc, axis=1), jnp.stack(new_vc, axis=1),
            jnp.stack(new_s, axis=1))
```

```python
import functools

import numpy as np
import jax
import jax.numpy as jnp
from jax import lax
from jax.experimental import pallas as pl
from jax.experimental.pallas import tpu as pltpu

F32 = jnp.float32
BF16 = jnp.bfloat16

HEAD_DIM = 64
GRID_W = 64
NA_HEADS = 6
NA_KH = 8
NA_KW = 16
RW_HEADS = 4
SWA_HEADS = 6
SWA_KV_HEADS = 2
SWA_WINDOW = 128
N_EXPERTS = 16
EC_CAPACITY = 2
ROPE_THETA = 10000.0
NORM_EPS = 1e-6
GN_EPS = 64e-5
NEG_INF = -1e30
SUB_ROWS = 256
NA_UNROLL = 8
SWA_UNROLL = 2
RW_CHUNK = 64
assert RW_CHUNK == HEAD_DIM
CTX_GROUP = 16
RW_W = RW_HEADS * HEAD_DIM
NA_W = NA_HEADS * HEAD_DIM
SWA_W = SWA_HEADS * HEAD_DIM
SWA_KV_W = SWA_KV_HEADS * HEAD_DIM
RW_IN_W = 1152
VMEM_LIMIT = 56 * 1024 * 1024


def _cp(*sem):
    return pltpu.CompilerParams(dimension_semantics=sem, vmem_limit_bytes=VMEM_LIMIT)


def _bf(x):
    return x.astype(BF16)


def _dot(a, b):
    return jnp.dot(a, b, preferred_element_type=F32)


def _dot_nt(a, b):
    return lax.dot_general(a, b, (((1,), (1,)), ((), ())), preferred_element_type=F32)


def _dot_tn(a, b):
    return lax.dot_general(a, b, (((0,), (0,)), ((), ())), preferred_element_type=F32)


def _split2(x):
    hi = x.astype(BF16)
    lo = (x - hi.astype(F32)).astype(BF16)
    return hi, lo


def _split3(x):
    hi = x.astype(BF16)
    r1 = x - hi.astype(F32)
    mid = r1.astype(BF16)
    lo = (r1 - mid.astype(F32)).astype(BF16)
    return hi, mid, lo


def _dot2(a, b_bf):
    hi, lo = _split2(a)
    return _dot(hi, b_bf) + _dot(lo, b_bf)


def _sigmoid(x):
    return 1.0 / (1.0 + jnp.exp(-x))


def _block_ones(width):
    i = np.arange(width) // HEAD_DIM
    return jnp.asarray((i[:, None] == i[None, :]).astype(np.float32), dtype=BF16)


def _row_tile(seq):
    return 512 if seq % 512 == 0 else 256


def _full(shape):
    return pl.BlockSpec(shape, lambda *_: (0,) * len(shape))


def _adaln_kernel(c_ref, w_ref, b_ref, o_ref):
    c = c_ref[...]
    s = c * _sigmoid(c)
    shi, slo = _split2(s)
    whi, wlo = _split2(w_ref[0])
    o_ref[0] = _dot(shi, whi) + _dot(slo, whi) + _dot(shi, wlo) + b_ref[0]


def _adaln(cond, ada_w, ada_b):
    nl, d, n6 = ada_w.shape
    tn = 1536
    rows = cond.shape[0]
    return pl.pallas_call(
        _adaln_kernel,
        grid=(nl, n6 // tn),
        in_specs=[pl.BlockSpec((rows, d), lambda l, j: (0, 0)),
                  pl.BlockSpec((1, d, tn), lambda l, j: (l, 0, j)),
                  pl.BlockSpec((1, 1, tn), lambda l, j: (l, 0, j))],
        out_specs=pl.BlockSpec((1, rows, tn), lambda l, j: (l, 0, j)),
        out_shape=jax.ShapeDtypeStruct((nl, rows, n6), F32),
        compiler_params=_cp("parallel", "parallel"),
        name="adaln",
    )(cond, ada_w, ada_b.reshape(nl, 1, n6))


def _head_sums(zz, ones128):
    zz = _bf(zz)
    parts = [_dot(zz[:, i:i + 128], ones128) for i in range(0, zz.shape[1], 128)]
    return jnp.concatenate(parts, axis=1) if len(parts) > 1 else parts[0]


def _head_norm(z, gain, ones128):
    ms = _head_sums(z * z, ones128) * (1.0 / HEAD_DIM)
    return z * lax.rsqrt(ms + NORM_EPS) * gain


def _rope(z, cos, sin_signed):
    w = z.shape[1]
    lane = lax.broadcasted_iota(jnp.int32, z.shape, 1)
    first = (lane % HEAD_DIM) < (HEAD_DIM // 2)
    swapped = jnp.where(first, pltpu.roll(z, w - HEAD_DIM // 2, 1), pltpu.roll(z, HEAD_DIM // 2, 1))
    return z * cos + swapped * sin_signed


def _proj_kernel(*refs, rope, split_rw):
    refs = list(refs)
    x_ref, mod_ref, n1_ref, w_ref, gqa_ref, gka_ref, gqc_ref, gkc_ref, ones_ref = refs[:9]
    qa_ref, ka_ref, va_ref, urw_ref, qc_ref, kc_ref, vc_ref = refs[-7:]
    extra = refs[9:-7]
    if rope:
        cos_ref, sin_ref = extra[:2]
    sh1 = mod_ref[0, 0:1, :]
    sc1 = mod_ref[0, 1:2, :]
    o0 = 0
    o1 = NA_W
    o2 = 2 * NA_W
    o3 = 3 * NA_W
    o4 = o3 + RW_IN_W
    o5 = o4 + SWA_W
    o6 = o5 + SWA_KV_W
    ones = ones_ref[0:128, 0:128]
    ones_kv = ones
    subs = [slice(i, i + SUB_ROWS) for i in range(0, x_ref.shape[0], SUB_ROWS)]
    hs, us = [], []
    for sl in subs:
        x = x_ref[sl, :]
        ms = jnp.mean(x * x, axis=-1, keepdims=True)
        hs.append(x * lax.rsqrt(ms + NORM_EPS) * n1_ref[...] * (1.0 + sc1) + sh1)
    h_hi = [_bf(h) for h in hs]
    us = [_dot(hh, w_ref[...]) for hh in h_hi]
    for sl, h, hh, u in zip(subs, hs, h_hi, us):
        qa = _head_norm(u[:, o0:o1], gqa_ref[...], ones)
        ka = _head_norm(u[:, o1:o2], gka_ref[...], ones)
        qc = _head_norm(u[:, o4:o5], gqc_ref[...], ones)
        kc = _head_norm(u[:, o5:o6], gkc_ref[...], ones_kv)
        if rope:
            qc = _rope(qc, cos_ref[sl, :], sin_ref[sl, :])
            kc = _rope(kc, cos_ref[sl, 0:SWA_KV_W], sin_ref[sl, 0:SWA_KV_W])
        qa_ref[sl, :] = qa.astype(qa_ref.dtype)
        ka_ref[sl, :] = ka.astype(ka_ref.dtype)
        va_ref[sl, :] = u[:, o2:o3].astype(va_ref.dtype)
        urw = u[:, o3:o4]
        if split_rw:
            h_lo = _bf(h - hh.astype(F32))
            urw = urw + _dot(h_lo, w_ref[:, o3:o4]) + _dot(hh, extra[-1][...])
        urw_ref[sl, :] = urw
        qc_ref[sl, :] = qc.astype(qc_ref.dtype)
        kc_ref[sl, :] = kc.astype(kc_ref.dtype)
        vc_ref[sl, :] = u[:, o6:].astype(vc_ref.dtype)


def _proj(x, mod, n1, w_in_bf, gains, ones384, rope_tabs, seq, per_request_mod, qkv_dtype, w_rw_lo=None):
    tokens, d = x.shape
    tm = _row_tile(seq)
    tiles_per_req = seq // tm
    in_w = w_in_bf.shape[1]
    rope = rope_tabs is not None
    mod_map = (lambda i: (i // tiles_per_req, 0, 0)) if per_request_mod else (lambda i: (0, 0, 0))
    row = lambda w: pl.BlockSpec((tm, w), lambda i: (i, 0))
    in_specs = [row(d), pl.BlockSpec((1, 6, d), mod_map), _full((1, d)), _full((d, in_w)),
                _full((1, NA_W)), _full((1, NA_W)), _full((1, SWA_W)), _full((1, SWA_KV_W)),
                _full((NA_W, NA_W))]
    args = [x, mod, n1, w_in_bf, *gains, ones384]
    if rope:
        tab = pl.BlockSpec((tm, SWA_W), lambda i: (i % tiles_per_req, 0))
        in_specs += [tab, tab]
        args += list(rope_tabs)
    if w_rw_lo is not None:
        in_specs.append(_full((d, RW_IN_W)))
        args.append(w_rw_lo)
    widths = [NA_W, NA_W, NA_W, RW_IN_W, SWA_W, SWA_KV_W, SWA_KV_W]
    dtypes = [qkv_dtype, qkv_dtype, qkv_dtype, F32, qkv_dtype, qkv_dtype, qkv_dtype]
    return pl.pallas_call(
        functools.partial(_proj_kernel, rope=rope, split_rw=w_rw_lo is not None),
        grid=(tokens // tm,),
        in_specs=in_specs,
        out_specs=[row(w) for w in widths],
        out_shape=[jax.ShapeDtypeStruct((tokens, w), dt) for w, dt in zip(widths, dtypes)],
        compiler_params=_cp("parallel"),
        name="proj",
    )(*args)


def _half_masks(width=2 * HEAD_DIM):
    lane = lax.broadcasted_iota(jnp.int32, (1, width), 1)
    return lane < HEAD_DIM, lane >= HEAD_DIM


def _swap_halves(z):
    return pltpu.roll(z, HEAD_DIM, 1)


def _ctx_attn_kernel(sink_ref, qa_ref, ka_ref, va_ref, qc_ref, kc_ref, vc_ref, oa_ref, oc_ref):
    scale = HEAD_DIM ** -0.5
    m0, m1 = _half_masks()
    masks = (m0, m1)
    for pair in range(NA_HEADS // 2):
        sl = slice(pair * 128, (pair + 1) * 128)
        qp = qa_ref[:, sl].astype(F32) * scale
        kp = _bf(ka_ref[:, sl])
        vp = _bf(va_ref[:, sl])
        outs = []
        for half in range(2):
            qm = _bf(jnp.where(masks[half], qp, 0.0))
            s = _dot_nt(qm, kp)
            m = jnp.max(s, axis=-1, keepdims=True)
            e = jnp.exp(s - m)
            l = jnp.sum(e, axis=-1, keepdims=True)
            outs.append(_dot(_bf(e), vp) / l)
        oa_ref[:, sl] = jnp.where(m0, outs[0], outs[1])
    kc = _bf(kc_ref[...])
    vc = _bf(vc_ref[...])
    group = SWA_HEADS // SWA_KV_HEADS
    for pair in range(SWA_HEADS // 2):
        sl = slice(pair * 128, (pair + 1) * 128)
        qp = qc_ref[:, sl].astype(F32) * scale
        outs = []
        for half in range(2):
            h = 2 * pair + half
            g = h // group
            qh = qp if g == half else _swap_halves(qp)
            qm = _bf(jnp.where(masks[g], qh, 0.0))
            s = _dot_nt(qm, kc)
            sk = sink_ref[h]
            m = jnp.maximum(jnp.max(s, axis=-1, keepdims=True), sk)
            e = jnp.exp(s - m)
            l = jnp.sum(e, axis=-1, keepdims=True) + jnp.exp(sk - m)
            o = _dot(_bf(e), vc) / l
            outs.append(o if g == half else _swap_halves(o))
        oc_ref[:, sl] = jnp.where(m0, outs[0], outs[1])


def _ctx_attn(sink, qa, ka, va, qc, kc, vc, seq):
    tokens = qa.shape[0]
    blk = lambda w: pl.BlockSpec((seq, w), lambda b: (b, 0))
    return pl.pallas_call(
        _ctx_attn_kernel,
        grid=(tokens // seq,),
        in_specs=[pl.BlockSpec(memory_space=pltpu.SMEM), blk(NA_W), blk(NA_W), blk(NA_W), blk(SWA_W),
                  blk(SWA_KV_W), blk(SWA_KV_W)],
        out_specs=[blk(NA_W), blk(SWA_W)],
        out_shape=[jax.ShapeDtypeStruct((tokens, NA_W), F32), jax.ShapeDtypeStruct((tokens, SWA_W), F32)],
        compiler_params=_cp("parallel"),
        name="ctx_attn",
    )(sink, qa, ka, va, qc, kc, vc)


def _na_bias_kernel(rpb_ref, o_ref):
    h = pl.program_id(0)
    nrow = 2 * NA_KH - 1
    ncol = 2 * NA_KW - 1
    width = NA_KH * GRID_W
    shape = (GRID_W, width)
    lane = lax.broadcasted_iota(jnp.int32, shape, 1)
    qc = lax.broadcasted_iota(jnp.int32, shape, 0)
    kc = lane % GRID_W
    c_start = jnp.clip(qc - NA_KW // 2, 0, GRID_W - NA_KW)
    ok = (kc >= c_start) & (kc < c_start + NA_KW)
    d_col = jnp.clip(kc - qc, 1 - NA_KW, NA_KW - 1) + NA_KW - 1
    key_row = lax.broadcasted_iota(jnp.int32, (1, width), 1) // GRID_W

    def case_body(case, carry):
        acc = jnp.zeros(shape, F32)
        for dc in range(ncol):
            val = jnp.zeros((1, width), F32)
            for i in range(NA_KH):
                val = jnp.where(key_row == i, rpb_ref[(h * nrow + case + i) * ncol + dc], val)
            acc = jnp.where(d_col == dc, val, acc)
        o_ref[0, pl.ds(case, 1)] = jnp.where(ok, acc, NEG_INF)[None]
        return carry

    lax.fori_loop(0, NA_KH, case_body, 0)


def _na_bias_table(rpb):
    nh = rpb.shape[0]
    return pl.pallas_call(
        _na_bias_kernel,
        grid=(nh,),
        in_specs=[pl.BlockSpec(memory_space=pltpu.SMEM)],
        out_specs=pl.BlockSpec((1, NA_KH, GRID_W, NA_KH * GRID_W), lambda h: (h // 2, 0, h % 2, 0)),
        out_shape=jax.ShapeDtypeStruct((nh // 2, NA_KH, 2 * GRID_W, NA_KH * GRID_W), F32),
        compiler_params=_cp("parallel"),
        name="na_bias",
    )(rpb.reshape(-1))


def _na_kernel(q_ref, k_ref, v_ref, kx_ref, vx_ref, bias_ref, o_ref, *, rows):
    scale = HEAD_DIM ** -0.5
    m0, m1 = _half_masks()
    kx = _bf(kx_ref[...])
    vx = _bf(vx_ref[...])

    def body(it, carry):
        us = range(NA_UNROLL)
        r = [it * NA_UNROLL + u for u in us]
        rs = [jnp.clip(r[u] - NA_KH // 2, 0, rows - NA_KH) for u in us]
        case = [rs[u] - r[u] + NA_KH - 1 for u in us]
        q0 = [pl.multiple_of(r[u] * GRID_W, GRID_W) for u in us]
        k0 = [pl.multiple_of(rs[u] * GRID_W, GRID_W) for u in us]
        qp = [q_ref[pl.ds(q0[u], GRID_W), :].astype(F32) * scale for u in us]
        kw = [_bf(k_ref[pl.ds(k0[u], NA_KH * GRID_W), :]) for u in us]
        vw = [_bf(v_ref[pl.ds(k0[u], NA_KH * GRID_W), :]) for u in us]
        q2 = [_bf(jnp.concatenate([jnp.where(m0, qp[u], 0.0), jnp.where(m1, qp[u], 0.0)], axis=0)) for u in us]
        s = [jnp.concatenate([_dot_nt(q2[u], kw[u]) + bias_ref[0, pl.ds(case[u], 1)][0], _dot_nt(q2[u], kx)],
                             axis=1) for u in us]
        m = [jnp.max(s[u], axis=-1, keepdims=True) for u in us]
        e = [jnp.exp(s[u] - m[u]) for u in us]
        l = [jnp.sum(e[u], axis=-1, keepdims=True) for u in us]
        o = [_dot(_bf(e[u]), jnp.concatenate([vw[u], vx], axis=0)) / l[u] for u in us]
        for u in us:
            o_ref[pl.ds(q0[u], GRID_W), :] = jnp.where(m0, o[u][0:GRID_W], o[u][GRID_W:])
        return carry

    lax.fori_loop(0, rows // NA_UNROLL, body, 0)


def _na_attn(q, k, v, kx, vx, bias, seq, past):
    tokens = q.shape[0]
    nb = tokens // seq
    rows = seq // GRID_W
    blk = pl.BlockSpec((seq, 128), lambda b, p: (b, p))
    cblk = pl.BlockSpec((past, 128), lambda b, p: (b, p))
    return pl.pallas_call(
        functools.partial(_na_kernel, rows=rows),
        grid=(nb, NA_HEADS // 2),
        in_specs=[blk, blk, blk, cblk, cblk,
                  pl.BlockSpec((1, NA_KH, 2 * GRID_W, NA_KH * GRID_W), lambda b, p: (p, 0, 0, 0))],
        out_specs=blk,
        out_shape=jax.ShapeDtypeStruct((tokens, NA_W), F32),
        compiler_params=_cp("parallel", "parallel"),
        name="na_attn",
    )(q, k, v, kx, vx, bias)


def _swa_kernel(sink_ref, q_ref, k_ref, v_ref, kx_ref, vx_ref, o_ref, *, seq):
    scale = HEAD_DIM ** -0.5
    blk = SWA_WINDOW
    m0, m1 = _half_masks()
    masks = (m0, m1)
    kx = _bf(kx_ref[...])
    vx = _bf(vx_ref[...])
    group = SWA_HEADS // SWA_KV_HEADS

    sk = []
    for g in range(SWA_KV_HEADS):
        sk.append(jnp.concatenate([jnp.full((blk, 1), sink_ref[h], F32) for h in range(g * group, (g + 1) * group)],
                                  axis=0))

    def body(it, carry):
        us = range(SWA_UNROLL)
        cs = [(u, g) for u in us for g in range(SWA_KV_HEADS)]
        nb = [it * SWA_UNROLL + u for u in us]
        ks = [pl.multiple_of(jnp.clip((nb[u] - 1) * blk, 0, seq - 3 * blk), blk) for u in us]
        q0 = [pl.multiple_of(nb[u] * blk, blk) for u in us]
        kw = [_bf(k_ref[pl.ds(ks[u], 3 * blk), :]) for u in us]
        vw = [_bf(v_ref[pl.ds(ks[u], 3 * blk), :]) for u in us]
        ok = []
        for u in us:
            qpos = q0[u] + lax.broadcasted_iota(jnp.int32, (group * blk, 1), 0) % blk
            kpos = ks[u] + lax.broadcasted_iota(jnp.int32, (1, 3 * blk), 1)
            ok.append(jnp.abs(qpos - kpos) <= SWA_WINDOW)
        qg = {}
        for u in us:
            pairs = [q_ref[pl.ds(q0[u], blk), p * 128:(p + 1) * 128].astype(F32) * scale
                     for p in range(SWA_HEADS // 2)]
            for g in range(SWA_KV_HEADS):
                qs = []
                for h in range(g * group, (g + 1) * group):
                    qh = pairs[h // 2] if h % 2 == g else _swap_halves(pairs[h // 2])
                    qs.append(jnp.where(masks[g], qh, 0.0))
                qg[u, g] = _bf(jnp.concatenate(qs, axis=0))
        sw = {c: jnp.where(ok[c[0]], _dot_nt(qg[c], kw[c[0]]), NEG_INF) for c in cs}
        sx = {c: _dot_nt(qg[c], kx) for c in cs}
        m = {c: jnp.maximum(jnp.maximum(jnp.max(sw[c], axis=-1, keepdims=True),
                                        jnp.max(sx[c], axis=-1, keepdims=True)), sk[c[1]]) for c in cs}
        ew = {c: jnp.exp(sw[c] - m[c]) for c in cs}
        ex = {c: jnp.exp(sx[c] - m[c]) for c in cs}
        l = {c: jnp.sum(ew[c], axis=-1, keepdims=True) + jnp.sum(ex[c], axis=-1, keepdims=True)
             + jnp.exp(sk[c[1]] - m[c]) for c in cs}
        o = {c: (_dot(_bf(ew[c]), vw[c[0]]) + _dot(_bf(ex[c]), vx)) / l[c] for c in cs}
        for u in us:
            head_out = []
            for g in range(SWA_KV_HEADS):
                for i in range(group):
                    h = g * group + i
                    oh = o[u, g][i * blk:(i + 1) * blk]
                    head_out.append(oh if h % 2 == g else _swap_halves(oh))
            for p in range(SWA_HEADS // 2):
                o_ref[pl.ds(q0[u], blk), p * 128:(p + 1) * 128] = jnp.where(m0, head_out[2 * p], head_out[2 * p + 1])
        return carry

    lax.fori_loop(0, seq // (blk * SWA_UNROLL), body, 0)


def _swa_attn(sink, q, k, v, kx, vx, seq, past):
    tokens = q.shape[0]
    blk = lambda w: pl.BlockSpec((seq, w), lambda b: (b, 0))
    cblk = pl.BlockSpec((past, SWA_KV_W), lambda b: (b, 0))
    return pl.pallas_call(
        functools.partial(_swa_kernel, seq=seq),
        grid=(tokens // seq,),
        in_specs=[pl.BlockSpec(memory_space=pltpu.SMEM), blk(SWA_W), blk(SWA_KV_W), blk(SWA_KV_W), cblk, cblk],
        out_specs=blk(SWA_W),
        out_shape=jax.ShapeDtypeStruct((tokens, SWA_W), F32),
        compiler_params=_cp("parallel"),
        name="swa_attn",
    )(sink, q, k, v, kx, vx)


def _rw_pre_kernel(u_ref, up_ref, un_ref, mu_ref, w0_ref, w2_ref, a0_ref, a2_ref, g2_ref, kk_ref_, ka_ref_,
                   rk_ref, ones_ref, r_o, kk_o, v_o, ld_o, kka_o, kd_o, g_o, bonus_o, *, tiles_per_req):
    i = pl.program_id(0)
    u = u_ref[...]
    tm = u.shape[0]
    rowi = lax.broadcasted_iota(jnp.int32, (tm, 1), 0)
    first = (i % tiles_per_req) == 0
    last = (i % tiles_per_req) == tiles_per_req - 1
    prev_row = jnp.where(first, 0.0, up_ref[7:8, :])
    next_row = jnp.where(last, 0.0, un_ref[0:1, :])
    prev = jnp.where(rowi == 0, prev_row, pltpu.roll(u, 1, 0))
    nxt = jnp.where(rowi == tm - 1, next_row, pltpu.roll(u, tm - 1, 0))
    us = u + mu_ref[0:1, :] * (prev - u) + mu_ref[1:2, :] * (nxt - u)
    r = us[:, 0:RW_W]
    k = us[:, RW_W:2 * RW_W]
    v = us[:, 2 * RW_W:3 * RW_W]
    wl = us[:, 3 * RW_W:3 * RW_W + 128]
    al = us[:, 3 * RW_W + 128:3 * RW_W + 256]
    gl = us[:, 3 * RW_W + 256:3 * RW_W + 384]
    z = -(w0_ref[...] + _dot(_bf(jnp.tanh(wl)), w2_ref[...]))
    softplus = jnp.maximum(z, 0.0) + jnp.log(1.0 + jnp.exp(-jnp.abs(z)))
    w = -softplus - 0.5
    ld = -jnp.exp(w)
    a = _sigmoid(a0_ref[...] + _dot(_bf(al), a2_ref[...]))
    g = _dot(_bf(_sigmoid(gl)), g2_ref[...])
    ones = ones_ref[...]
    kkr = k * kk_ref_[...]
    kk = kkr * lax.rsqrt(jnp.maximum(_dot2(kkr * kkr, ones), 1e-24))
    k_a = ka_ref_[...]
    kd_f = k * (1.0 + (a[:, 0:RW_W] - 1.0) * k_a)
    kd_b = k * (1.0 + (a[:, RW_W:] - 1.0) * k_a)
    r_o[...] = r
    kk_o[...] = kk
    v_o[...] = v
    ld_o[...] = ld
    kka_o[:, 0:RW_W] = kk * a[:, 0:RW_W]
    kka_o[:, RW_W:] = kk * a[:, RW_W:]
    kd_o[:, 0:RW_W] = kd_f
    kd_o[:, RW_W:] = kd_b
    g_o[...] = g
    bonus_o[...] = _dot2(r * (kd_f + kd_b) * rk_ref[...], ones) * v


def _rw_pre(urw, p, seq):
    tokens = urw.shape[0]
    tm = _row_tile(seq)
    tpr = seq // tm
    nt = tokens // tm
    r8 = tm // 8
    row = lambda w: pl.BlockSpec((tm, w), lambda i: (i, 0))
    in_specs = [row(RW_IN_W),
                pl.BlockSpec((8, RW_IN_W), lambda i: (jnp.maximum(i * r8 - 1, 0), 0)),
                pl.BlockSpec((8, RW_IN_W), lambda i: (jnp.minimum((i + 1) * r8, nt * r8 - 1), 0)),
                _full((2, RW_IN_W)), _full((1, 2 * RW_W)), _full((128, 2 * RW_W)), _full((1, 2 * RW_W)),
                _full((128, 2 * RW_W)), _full((128, RW_W)), _full((1, RW_W)), _full((1, RW_W)),
                _full((1, RW_W)), _full((RW_W, RW_W))]
    widths = [RW_W, RW_W, RW_W, 2 * RW_W, 2 * RW_W, 2 * RW_W, RW_W, RW_W]
    return pl.pallas_call(
        functools.partial(_rw_pre_kernel, tiles_per_req=tpr),
        grid=(nt,),
        in_specs=in_specs,
        out_specs=[row(w) for w in widths],
        out_shape=[jax.ShapeDtypeStruct((tokens, w), F32) for w in widths],
        compiler_params=_cp("parallel"),
        name="rw_pre",
    )(urw, urw, urw, p['mu'], p['w0'], p['w2'], p['a0'], p['a2'], p['g2'], p['k_k'], p['k_a'], p['r_k'],
      p['ones256'])


def _rw_masks():
    t = RW_CHUNK
    n = RW_HEADS * t
    tt = np.arange(t)[:, None]
    ss = (np.arange(n) % t)[None, :]
    before = np.stack([ss < tt, ss > tt])
    diag = (ss == tt)
    strict = before.astype(np.float32)
    incl = (before | diag[None]).astype(np.float32)
    eye = diag.astype(np.float32)
    ti = np.arange(t)
    tri = np.stack([ti[None, :] <= ti[:, None], ti[None, :] >= ti[:, None]]).astype(np.float32)
    hd = np.arange(n) // t
    same = (hd[:, None] == hd[None, :]).astype(np.float32)
    return (jnp.asarray(strict), jnp.asarray(incl), jnp.asarray(tri, dtype=BF16), jnp.asarray(same),
            jnp.asarray(eye))


def _rw_scan_kernel(r_ref, kk_ref, v_ref, ld_ref, kka_ref, kd_ref, s0_ref, strict_ref, incl_ref, tri_ref,
                    same_ref, eye_ref, o_ref, sfin_ref, s_scr, *, nsub):
    d = pl.program_id(1)
    c = pl.program_id(2)
    t = RW_CHUNK
    n = RW_HEADS * t

    @pl.when(c == 0)
    def _():
        s_scr[...] = s0_ref[0, 0]

    strict = strict_ref[0]
    incl = incl_ref[0]
    tri = tri_ref[0]
    eye = eye_ref[...]
    same = same_ref[...]
    same_bf = _bf(same)

    def bd(x):
        return jnp.concatenate([_bf(x)] * RW_HEADS, axis=0) * same_bf

    js = range(nsub)
    rows = [pl.ds(pl.multiple_of((j + d * (nsub - 1 - 2 * j)) * t, t), t) for j in js]
    ld = [ld_ref[rows[j], :] for j in js]
    cum = []
    for j in js:
        lhi, lmid, llo = _split3(ld[j])
        cum.append(_dot(tri, lhi) + _dot(tri, lmid) + _dot(tri, llo))
    cend = [jnp.sum(ld[j], axis=0, keepdims=True) for j in js]
    kka = [kka_ref[rows[j], :] for j in js]
    kd = [kd_ref[rows[j], :] for j in js]
    v = [v_ref[rows[j], :] for j in js]
    at = [-kk_ref[rows[j], :] * jnp.exp(cum[j] - ld[j]) for j in js]
    rt = [r_ref[rows[j], :] * jnp.exp(cum[j]) for j in js]
    e_inv = [jnp.exp(-cum[j]) for j in js]
    aa = [_dot_nt(_bf(jnp.concatenate([at[j], rt[j]], axis=0)),
                  jnp.concatenate([bd(kka[j] * e_inv[j]), bd(kd[j] * e_inv[j])], axis=0)) for j in js]
    a_ab = [aa[j][0:t, 0:n] * strict for j in js]
    x = [eye + a_ab[j] for j in js]
    pw = a_ab
    for _ in range(RW_CHUNK.bit_length() - 3):
        pw = [_dot(_bf(pw[j]), bd(pw[j])) for j in js]
        x = [x[j] + _dot(_bf(x[j]), bd(pw[j])) for j in js]
    xs = [_split2(x[j]) for j in js]
    sa = [_split2(a_ab[j]) for j in js]
    ax = [_dot(sa[j][0], bd(xs[j][0])) + _dot(sa[j][1], bd(xs[j][0])) + _dot(sa[j][0], bd(xs[j][1])) for j in js]
    x = [x[j] + _dot(xs[j][0], bd(eye - x[j] + ax[j])) for j in js]
    v_bd = [bd(v[j]) for j in js]
    wv = [_dot(_bf(aa[j][0:t, n:] * strict), v_bd[j]) for j in js]
    mu = [_dot(_bf(x[j]), jnp.concatenate([bd(at[j]), bd(wv[j])], axis=1)) for j in js]
    m1 = [mu[j][:, 0:n] for j in js]
    u0 = [mu[j][:, n:] for j in js]
    e_end = [jnp.exp(cend[j] - cum[j]) for j in js]
    bend = [_bf(kka[j] * e_end[j]) for j in js]
    g = [_bf(_dot_tn(_bf(m1[j]), bend[j]) * same) for j in js]
    cst = []
    for j in js:
        full = _dot_tn(_bf(jnp.concatenate([u0[j], v[j]], axis=0)),
                       jnp.concatenate([bend[j], _bf(kd[j] * e_end[j])], axis=0)) * same
        cst.append(functools.reduce(jnp.add, [full[h * t:(h + 1) * t] for h in range(RW_HEADS)]))
    qo = [_dot(_bf(aa[j][t:, 0:n] * incl), jnp.concatenate([bd(m1[j]), bd(u0[j])], axis=1)) for j in js]
    q = [_bf(rt[j] + qo[j][:, 0:n]) for j in js]
    o0 = [qo[j][:, n:] + _dot(_bf(aa[j][t:, n:] * incl), v_bd[j]) for j in js]

    s = s_scr[...]
    for j in js:
        o_ref[0, rows[j], :] = _dot_nt(q[j], bd(s)) + o0[j]
        s = s * jnp.exp(cend[j]) + _dot(_bf(s), g[j]) + cst[j]
    s_scr[...] = s

    @pl.when(c == pl.num_programs(2) - 1)
    def _():
        sfin_ref[0, 0] = s


def _rw_scan(r, kk, v, ld, kka, kd, s0_bd, consts, seq):
    tokens = r.shape[0]
    nreq = tokens // seq
    tb = min(seq, 512)
    nblk = seq // tb
    nsub = tb // RW_CHUNK
    n = RW_HEADS * RW_CHUNK
    cc = lambda d, c: c + d * (nblk - 1 - 2 * c)
    shared = pl.BlockSpec((tb, RW_W), lambda b, d, c: (b * nblk + cc(d, c), 0))
    dirw = pl.BlockSpec((tb, RW_W), lambda b, d, c: (b * nblk + cc(d, c), d))
    strict, incl, tri, same, eye = consts
    return pl.pallas_call(
        functools.partial(_rw_scan_kernel, nsub=nsub),
        grid=(nreq, 2, nblk),
        in_specs=[shared, shared, shared, dirw, dirw, dirw,
                  pl.BlockSpec((1, 1, HEAD_DIM, n), lambda b, d, c: (b, d, 0, 0)),
                  pl.BlockSpec((1, RW_CHUNK, n), lambda b, d, c: (d, 0, 0)),
                  pl.BlockSpec((1, RW_CHUNK, n), lambda b, d, c: (d, 0, 0)),
                  pl.BlockSpec((1, RW_CHUNK, RW_CHUNK), lambda b, d, c: (d, 0, 0)),
                  _full((n, n)), _full((RW_CHUNK, n))],
        out_specs=[pl.BlockSpec((1, tb, RW_W), lambda b, d, c: (d, b * nblk + cc(d, c), 0)),
                   pl.BlockSpec((1, 1, HEAD_DIM, n), lambda b, d, c: (b, d, 0, 0))],
        out_shape=[jax.ShapeDtypeStruct((2, tokens, RW_W), F32),
                   jax.ShapeDtypeStruct((nreq, 2, HEAD_DIM, n), F32)],
        scratch_shapes=[pltpu.VMEM((HEAD_DIM, n), F32)],
        compiler_params=_cp("parallel", "parallel", "arbitrary"),
        name="rw_scan",
    )(r, kk, v, ld, kka, kd, s0_bd, strict, incl, tri, same, eye)


def _finish_kernel(x_ref, oa_ref, oc_ref, o2_ref, bonus_ref, g_ref, mod_ref, wout_ref, lng_ref, lnb_ref,
                   n2_ref, wr_hi_ref, wr_lo_ref, ones_ref, x1_ref, h2_ref, aff_ref):
    ones = ones_ref[0:128, 0:128]
    g1 = mod_ref[0, 2:3, :]
    sh2 = mod_ref[0, 3:4, :]
    sc2 = mod_ref[0, 4:5, :]
    subs = [slice(i, i + SUB_ROWS) for i in range(0, x_ref.shape[0], SUB_ROWS)]
    mixins = []
    for sl in subs:
        y = o2_ref[0, sl, :] + o2_ref[1, sl, :]
        mu = _head_sums(y, ones) * (1.0 / HEAD_DIM)
        yc = y - mu
        var = _head_sums(yc * yc, ones) * (1.0 / HEAD_DIM)
        yn = yc * lax.rsqrt(var + GN_EPS) * lng_ref[...] + lnb_ref[...]
        ob = (yn + bonus_ref[sl, :]) * g_ref[sl, :]
        mixins.append(jnp.concatenate([_bf(oa_ref[sl, :]), _bf(ob), _bf(oc_ref[sl, :])], axis=1))
    mixes = [_dot(mixin, wout_ref[...]) for mixin in mixins]
    for sl, mix in zip(subs, mixes):
        x1 = x_ref[sl, :] + g1 * mix
        ms = jnp.mean(x1 * x1, axis=-1, keepdims=True)
        h2 = x1 * lax.rsqrt(ms + NORM_EPS) * n2_ref[...] * (1.0 + sc2) + sh2
        x1_ref[sl, :] = x1
        h2_ref[sl, :] = _bf(h2)
        hhi, hlo = _split2(h2)
        logits = _dot(hhi, wr_hi_ref[...]) + _dot(hlo, wr_hi_ref[...]) + _dot(hhi, wr_lo_ref[...])
        m = jnp.max(logits, axis=-1, keepdims=True)
        e = jnp.exp(logits - m)
        aff_ref[sl, :] = e / jnp.sum(e, axis=-1, keepdims=True)


def _finish(x, oa, oc, o2, bonus, g, mod, p, seq, per_request_mod):
    tokens, d = x.shape
    tm = _row_tile(seq)
    tpr = seq // tm
    mod_map = (lambda i: (i // tpr, 0, 0)) if per_request_mod else (lambda i: (0, 0, 0))
    row = lambda w: pl.BlockSpec((tm, w), lambda i: (i, 0))
    return pl.pallas_call(
        _finish_kernel,
        grid=(tokens // tm,),
        in_specs=[row(d), row(NA_W), row(SWA_W), pl.BlockSpec((2, tm, RW_W), lambda i: (0, i, 0)), row(RW_W),
                  row(RW_W), pl.BlockSpec((1, 6, d), mod_map), _full((d, d)), _full((1, RW_W)),
                  _full((1, RW_W)), _full((1, d)), _full((d, N_EXPERTS)), _full((d, N_EXPERTS)),
                  _full((RW_W, RW_W))],
        out_specs=[row(d), row(d), row(N_EXPERTS)],
        out_shape=[jax.ShapeDtypeStruct((tokens, d), F32), jax.ShapeDtypeStruct((tokens, d), BF16),
                   jax.ShapeDtypeStruct((tokens, N_EXPERTS), F32)],
        compiler_params=_cp("parallel"),
        name="finish",
    )(x, oa, oc, o2, bonus, g, mod, p['w_out'], p['ln_g'], p['ln_b'], p['n2'], p['wr_hi'], p['wr_lo'],
      p['ones256'])


def _topk_kernel(aff_ref, tri_ref, eye_ref, place_ref, slot_ref, slotrow_ref, gfull_ref, ends_ref, *, cap, group,
                 seq, tb):
    aff = aff_ref[...]
    bits = lax.bitcast_convert_type(aff, jnp.int32)
    capf = jnp.float32(cap)
    eye = eye_ref[...]
    ghi, gmid, glo = _split3(aff)
    aff_t = _dot_nt(eye, ghi) + _dot_nt(eye, gmid) + _dot_nt(eye, glo)
    bits_t = lax.bitcast_convert_type(aff_t, jnp.int32)
    rs = range(group)

    def bis(_, carry):
        los, his = carry
        nlo, nhi = [], []
        for r in rs:
            mid = los[r] + ((his[r] - los[r] + 1) >> 1)
            cnt = jnp.sum(jnp.where(bits_t[:, r * seq:(r + 1) * seq] >= mid, 1.0, 0.0), axis=1, keepdims=True)
            ge = cnt >= capf
            nlo.append(jnp.where(ge, mid, los[r]))
            nhi.append(jnp.where(ge, his[r], mid - 1))
        return tuple(nlo), tuple(nhi)

    lo0 = tuple(jnp.zeros((N_EXPERTS, 1), jnp.int32) for _ in rs)
    hi0 = tuple(jnp.full((N_EXPERTS, 1), 0x7F7FFFFF, jnp.int32) for _ in rs)
    thr_cols, _ = lax.fori_loop(0, 31, bis, (lo0, hi0))
    ri = lax.broadcasted_iota(jnp.int32, (N_EXPERTS, N_EXPERTS), 0)
    ci = lax.broadcasted_iota(jnp.int32, (N_EXPERTS, N_EXPERTS), 1)
    tri = tri_ref[...]
    for r in rs:
        thr = jnp.sum(jnp.where(ri == ci, thr_cols[r], 0), axis=0, keepdims=True)
        rows = slice(r * seq, (r + 1) * seq)
        gt = jnp.where(bits[rows] > thr, 1.0, 0.0)
        eq = jnp.where(bits[rows] == thr, 1.0, 0.0)
        need = capf - jnp.sum(gt, axis=0, keepdims=True)
        offset = float(r * cap)
        carry_g = jnp.zeros((1, N_EXPERTS), F32)
        carry_e = jnp.zeros((1, N_EXPERTS), F32)
        for blk in range(seq // tb):
            sl = slice(blk * tb, (blk + 1) * tb)
            out = slice(r * seq + blk * tb, r * seq + (blk + 1) * tb)
            pg = _dot(tri, _bf(gt[sl])) + carry_g
            pe = _dot(tri, _bf(eq[sl])) + carry_e
            carry_g = pg[tb - 1:tb, :]
            carry_e = pe[tb - 1:tb, :]
            sel = gt[sl] + eq[sl] * jnp.where(pe <= need, 1.0, 0.0)
            slot = jnp.where(sel > 0.5, pg + jnp.minimum(pe, need) - 1.0 + offset, -1.0)
            slot_ref[out, :] = slot
            ends_ref[r, blk:blk + 1, :] = carry_g + jnp.minimum(carry_e, need) + offset
            shi, slo = _split2(slot)
            slotrow_ref[0, :, 0, out] = _dot_nt(eye, shi) + _dot_nt(eye, slo)
            gfull_ref[out, :] = _bf(_dot(ghi[out], place_ref[0]) + _dot(gmid[out], place_ref[1])
                                    + _dot(glo[out], place_ref[2]))


def _topk(aff, seq, group):
    tokens = aff.shape[0]
    nreq = tokens // seq
    cap = EC_CAPACITY * seq // N_EXPERTS
    tb = min(seq, 512)
    ti = np.arange(tb)
    tri = jnp.asarray((ti[None, :] <= ti[:, None]).astype(np.float32), dtype=BF16)
    eye = jnp.asarray(np.eye(N_EXPERTS, dtype=np.float32), dtype=BF16)
    place = np.zeros((3, N_EXPERTS, 128), np.float32)
    for s in range(3):
        place[s, np.arange(N_EXPERTS), s * N_EXPERTS + np.arange(N_EXPERTS)] = 1.0
    place = jnp.asarray(place, dtype=BF16)
    nblk = seq // tb
    slot, slotrow, gfull, ends = pl.pallas_call(
        functools.partial(_topk_kernel, cap=cap, group=group, seq=seq, tb=tb),
        grid=(nreq // group,),
        in_specs=[pl.BlockSpec((group * seq, N_EXPERTS), lambda b: (b, 0)), _full((tb, tb)),
                  _full((N_EXPERTS, N_EXPERTS)), _full((3, N_EXPERTS, 128))],
        out_specs=[pl.BlockSpec((group * seq, N_EXPERTS), lambda b: (b, 0)),
                   pl.BlockSpec((1, N_EXPERTS, 1, group * seq), lambda b: (b, 0, 0, 0)),
                   pl.BlockSpec((group * seq, 128), lambda b: (b, 0)),
                   pl.BlockSpec((group, nblk, N_EXPERTS), lambda b: (b, 0, 0))],
        out_shape=[jax.ShapeDtypeStruct((tokens, N_EXPERTS), F32),
                   jax.ShapeDtypeStruct((nreq // group, N_EXPERTS, 1, group * seq), F32),
                   jax.ShapeDtypeStruct((tokens, 128), BF16),
                   jax.ShapeDtypeStruct((nreq, nblk, N_EXPERTS), F32)],
        compiler_params=_cp("parallel"),
        name="topk",
    )(aff, tri, eye, place)
    ends = ends.reshape(nreq // group, group * nblk, N_EXPERTS).transpose(0, 2, 1)
    return slot, slotrow, gfull, ends.astype(jnp.int32).reshape(-1), tb


MOE_EB = 8


def _moe_dispatch_kernel(ends_ref, h_ref, slotrow_ref, gfull_ref, xe_ref, gs_ref, *, ct, nch):
    gi = pl.program_id(0)
    eb = pl.program_id(1)
    c = pl.program_id(2)
    mt = 128

    @pl.when(c == 0)
    def _():
        xe_ref[...] = jnp.zeros_like(xe_ref)
        gs_ref[...] = jnp.zeros_like(gs_ref)

    jcol = lax.broadcasted_iota(jnp.int32, (mt, 1), 0)
    starts, his, pieces = [], [], []
    for i in range(MOE_EB):
        base = (gi * N_EXPERTS + eb * MOE_EB + i) * nch
        lo = jnp.where(c == 0, 0, ends_ref[base + jnp.maximum(c - 1, 0)])
        his.append(ends_ref[base + c])
        start = pl.multiple_of(jnp.minimum((lo // 16) * 16, ct - mt), 16)
        starts.append(start)
        pieces.append(_bf(jnp.where(slotrow_ref[0, i] == (jcol + start).astype(F32), 1.0, 0.0)))
    onehot = jnp.concatenate(pieces, axis=0)
    xw = _bf(_dot(onehot, h_ref[...]))
    gw = _bf(_dot(onehot, gfull_ref[...]))
    for i in range(MOE_EB):
        rows = pl.ds(starts[i], mt)
        xe_ref[0, i, rows, :] += xw[i * mt:(i + 1) * mt]
        gs_ref[0, i, rows, :] += gw[i * mt:(i + 1) * mt]
    for i in range(MOE_EB):
        for w in range(1, ct // mt):
            wlo = starts[i] + w * mt

            @pl.when(wlo < his[i])
            def _(i=i, wlo=wlo):
                ws = pl.multiple_of(jnp.minimum(wlo, ct - mt), 16)
                slot = slotrow_ref[0, i]
                hit = (jnp.where(slot == (jcol + ws).astype(F32), 1.0, 0.0)
                       * jnp.where(slot >= wlo.astype(F32), 1.0, 0.0))
                rows = pl.ds(ws, mt)
                xe_ref[0, i, rows, :] += _bf(_dot(_bf(hit), h_ref[...]))
                gs_ref[0, i, rows, :] += _bf(_dot(_bf(hit), gfull_ref[...]))


def _moe_dispatch(ends, h2, slotrow, gfull, lg, ct, kc):
    tokens, d = h2.shape
    ngrp = tokens // lg
    nch = lg // kc
    grid_spec = pltpu.PrefetchScalarGridSpec(
        num_scalar_prefetch=1,
        grid=(ngrp, N_EXPERTS // MOE_EB, nch),
        in_specs=[pl.BlockSpec((kc, d), lambda gi, eb, c, ends: (gi * nch + c, 0)),
                  pl.BlockSpec((1, MOE_EB, 1, kc), lambda gi, eb, c, ends: (gi, eb, 0, c)),
                  pl.BlockSpec((kc, 128), lambda gi, eb, c, ends: (gi * nch + c, 0))],
        out_specs=[pl.BlockSpec((1, MOE_EB, ct, d), lambda gi, eb, c, ends: (gi, eb, 0, 0)),
                   pl.BlockSpec((1, MOE_EB, ct, 128), lambda gi, eb, c, ends: (gi, eb, 0, 0))])
    return pl.pallas_call(
        functools.partial(_moe_dispatch_kernel, ct=ct, nch=nch),
        grid_spec=grid_spec,
        out_shape=[jax.ShapeDtypeStruct((ngrp, N_EXPERTS, ct, d), BF16),
                   jax.ShapeDtypeStruct((ngrp, N_EXPERTS, ct, 128), BF16)],
        compiler_params=_cp("parallel", "parallel", "arbitrary"),
        name="moe_dispatch",
    )(ends, h2, slotrow, gfull)


def _moe_ffn_kernel(xe_ref, gs_ref, mod_ref, wg_ref, wu_ref, wd_ref, ye_ref):
    e = pl.program_id(1)
    lane = lax.broadcasted_iota(jnp.int32, (1, 128), 1)
    pick = (lane == e) | (lane == e + N_EXPERTS) | (lane == e + 2 * N_EXPERTS)
    gate = jnp.sum(jnp.where(pick, gs_ref[0, 0].astype(F32), 0.0), axis=-1, keepdims=True)
    xb = xe_ref[0, 0]
    hg = _dot(xb, wg_ref[0, 0])
    hu = _dot(xb, wu_ref[0, 0])
    he = _bf(hg * _sigmoid(hg) * hu)
    y = _dot(he, wd_ref[0, 0])
    ye_ref[0, 0] = _bf(y * gate * mod_ref[0, 5:6, :])


def _moe_ffn(xe, gs, mod, wg, wu, wd, layer, per_group_mod):
    ngrp, _, ct, d = xe.shape
    f = wg.shape[3]
    mod_map = (lambda gi, e: (gi, 0, 0)) if per_group_mod else (lambda gi, e: (0, 0, 0))
    return pl.pallas_call(
        _moe_ffn_kernel,
        grid=(ngrp, N_EXPERTS),
        in_specs=[pl.BlockSpec((1, 1, ct, d), lambda gi, e: (gi, e, 0, 0)),
                  pl.BlockSpec((1, 1, ct, 128), lambda gi, e: (gi, e, 0, 0)),
                  pl.BlockSpec((1, 6, d), mod_map),
                  pl.BlockSpec((1, 1, d, f), lambda gi, e: (layer, e, 0, 0)),
                  pl.BlockSpec((1, 1, d, f), lambda gi, e: (layer, e, 0, 0)),
                  pl.BlockSpec((1, 1, f, d), lambda gi, e: (layer, e, 0, 0))],
        out_specs=pl.BlockSpec((1, 1, ct, d), lambda gi, e: (gi, e, 0, 0)),
        out_shape=jax.ShapeDtypeStruct((ngrp, N_EXPERTS, ct, d), BF16),
        compiler_params=_cp("parallel", "parallel"),
        name="moe_ffn",
    )(xe, gs, mod, wg, wu, wd)


def _moe_combine_kernel(ends_ref, x1_ref, slot_ref, ye_ref, o_ref, win_scr, *, ct, nch):
    gi = pl.program_id(0)
    j = pl.program_id(1)
    mt = 128
    wide = N_EXPERTS * mt
    shi, slo = _split2(slot_ref[...])
    col_e = lax.broadcasted_iota(jnp.int32, (N_EXPERTS, wide), 1) // mt
    row_e = lax.broadcasted_iota(jnp.int32, (N_EXPERTS, wide), 0)
    expand = _bf(jnp.where(col_e == row_e, 1.0, 0.0))
    sb = _dot(shi, expand) + _dot(slo, expand)
    lane = lax.broadcasted_iota(jnp.int32, (1, mt), 1)
    starts, his, targets = [], [], []
    for e in range(N_EXPERTS):
        base = (gi * N_EXPERTS + e) * nch
        lo = jnp.where(j == 0, 0, ends_ref[base + jnp.maximum(j - 1, 0)])
        his.append(ends_ref[base + j])
        start = pl.multiple_of(jnp.minimum((lo // 16) * 16, ct - mt), 16)
        win_scr[e * mt:(e + 1) * mt, :] = ye_ref[0, e, pl.ds(start, mt), :]
        starts.append(start)
        targets.append((lane + start).astype(F32))
    onehot = _bf(jnp.where(sb == jnp.concatenate(targets, axis=1), 1.0, 0.0))
    o_ref[...] = x1_ref[...] + _dot(onehot, win_scr[...])
    for e in range(N_EXPERTS):
        for w in range(1, ct // mt):
            wlo = starts[e] + w * mt

            @pl.when(wlo < his[e])
            def _(e=e, wlo=wlo):
                ws = pl.multiple_of(jnp.minimum(wlo, ct - mt), 16)
                sbe = sb[:, e * mt:(e + 1) * mt]
                hit = jnp.where(sbe == (lane + ws).astype(F32), 1.0, 0.0) * jnp.where(sbe >= wlo.astype(F32), 1.0, 0.0)
                o_ref[...] += _dot(_bf(hit), ye_ref[0, e, pl.ds(ws, mt), :])


def _moe_combine(ends, x1, slot, ye, lg, ct, kc):
    tokens, d = x1.shape
    ngrp = tokens // lg
    nch = lg // kc
    grid_spec = pltpu.PrefetchScalarGridSpec(
        num_scalar_prefetch=1,
        grid=(ngrp, nch),
        in_specs=[pl.BlockSpec((kc, d), lambda gi, j, ends: (gi * nch + j, 0)),
                  pl.BlockSpec((kc, N_EXPERTS), lambda gi, j, ends: (gi * nch + j, 0)),
                  pl.BlockSpec((1, N_EXPERTS, ct, d), lambda gi, j, ends: (gi, 0, 0, 0),
                               pipeline_mode=pl.Buffered(1))],
        out_specs=pl.BlockSpec((kc, d), lambda gi, j, ends: (gi * nch + j, 0)),
        scratch_shapes=[pltpu.VMEM((N_EXPERTS * 128, d), BF16)])
    return pl.pallas_call(
        functools.partial(_moe_combine_kernel, ct=ct, nch=nch),
        grid_spec=grid_spec,
        out_shape=jax.ShapeDtypeStruct((tokens, d), F32),
        compiler_params=_cp("parallel", "arbitrary"),
        name="moe_combine",
    )(ends, x1, slot, ye)


def _rope_tables(seq):
    t = np.arange(seq)
    n_freq = HEAD_DIM // 4
    inv = ROPE_THETA ** (-np.arange(n_freq, dtype=np.float32) / n_freq)
    ang = np.concatenate([(t // GRID_W).astype(np.float32)[:, None] * inv,
                          (t % GRID_W).astype(np.float32)[:, None] * inv], axis=-1)
    ang = jnp.asarray(ang, dtype=F32)
    cos, sin = jnp.cos(ang), jnp.sin(ang)
    cos_t = jnp.tile(jnp.concatenate([cos, cos], axis=-1), (1, SWA_HEADS))
    sin_t = jnp.tile(jnp.concatenate([-sin, sin], axis=-1), (1, SWA_HEADS))
    return cos_t, sin_t


def _blockdiag2(w):
    z = jnp.zeros_like(w[0])
    return jnp.concatenate([jnp.concatenate([w[0], z], axis=1), jnp.concatenate([z, w[1]], axis=1)], axis=0)


def _layer_params(l, ada_w, ada_b, norm1_g, norm2_g, w_in, na_q_norm, na_k_norm, na_rpb, rw_mu, rw_w0, rw_w2,
                  rw_a0, rw_a2, rw_g2, rw_k_k, rw_k_a, rw_r_k, rw_ln_g, rw_ln_b, swa_q_norm, swa_k_norm,
                  swa_sink, w_out, w_router, w_gate, w_up, w_down):
    wr = w_router[l]
    wr_hi = wr.astype(BF16)
    w_rw = w_in[l][:, 3 * NA_W:3 * NA_W + RW_IN_W]
    return {
        'n1': norm1_g[l][None], 'n2': norm2_g[l][None], 'w_in': w_in[l].astype(BF16),
        'w_rw_lo': (w_rw - w_rw.astype(BF16).astype(F32)).astype(BF16),
        'gains': (jnp.tile(na_q_norm[l], NA_HEADS)[None], jnp.tile(na_k_norm[l], NA_HEADS)[None],
                  jnp.tile(swa_q_norm[l], SWA_HEADS)[None], jnp.tile(swa_k_norm[l], SWA_KV_HEADS)[None]),
        'bias': _na_bias_table(na_rpb[l]),
        'mu': rw_mu[l], 'w0': rw_w0[l].reshape(1, 2 * RW_W), 'w2': _blockdiag2(rw_w2[l]).astype(BF16),
        'a0': rw_a0[l].reshape(1, 2 * RW_W), 'a2': _blockdiag2(rw_a2[l]).astype(BF16),
        'g2': rw_g2[l].astype(BF16), 'k_k': rw_k_k[l][None], 'k_a': rw_k_a[l][None],
        'r_k': rw_r_k[l].reshape(1, RW_W), 'ln_g': rw_ln_g[l][None], 'ln_b': rw_ln_b[l][None],
        'sink': swa_sink[l], 'w_out': w_out[l].astype(BF16),
        'wr_hi': wr_hi, 'wr_lo': (wr - wr_hi.astype(F32)).astype(BF16),
        'wg': w_gate, 'wu': w_up, 'wd': w_down, 'layer': l,
        'ones256': _block_ones(RW_W),
    }


def _mix_and_ffn(x, mod, p, oa, oc, urw, s0_bd, scan_consts, seq, per_request_mod, group):
    r, kk, v, ld, kka, kd, g, bonus = _rw_pre(urw, p, seq)
    o2, sfin = _rw_scan(r, kk, v, ld, kka, kd, s0_bd, scan_consts, seq)
    x1, h2, aff = _finish(x, oa, oc, o2, bonus, g, mod, p, seq, per_request_mod)
    slot, slotrow, gfull, ends, kc = _topk(aff, seq, group)
    cap = EC_CAPACITY * seq // N_EXPERTS
    lg, ct = group * seq, group * cap
    xe, gs = _moe_dispatch(ends, h2, slotrow, gfull, lg, ct, kc)
    ye = _moe_ffn(xe, gs, mod, p['wg'], p['wu'], p['wd'], p['layer'], per_request_mod)
    return _moe_combine(ends, x1, slot, ye, lg, ct, kc), sfin


def _context_layer(x, mod, p, ones384, scan_consts, seq):
    qa, ka, va, urw, qc, kc, vc = _proj(x, mod, p['n1'], p['w_in'], p['gains'], ones384, None, seq, False, F32,
                                        w_rw_lo=p['w_rw_lo'])
    oa, oc = _ctx_attn(p['sink'], qa, ka, va, qc, kc, vc, seq)
    nreq = x.shape[0] // seq
    s0 = jnp.zeros((nreq, 2, HEAD_DIM, RW_W), F32)
    y, sfin = _mix_and_ffn(x, mod, p, oa, oc, urw, s0, scan_consts, seq, False, CTX_GROUP)
    return y, ka, va, kc, vc, sfin


def _latent_layer(x, mod, p, ones384, scan_consts, rope_tabs, seq, kx_na, vx_na, kx_swa, vx_swa, s0_bd, past):
    qa, ka, va, urw, qc, kc, vc = _proj(x, mod, p['n1'], p['w_in'], p['gains'], ones384, rope_tabs, seq, True,
                                        BF16)
    oa = _na_attn(qa, ka, va, kx_na, vx_na, p['bias'], seq, past)
    oc = _swa_attn(p['sink'], qc, kc, vc, kx_swa, vx_swa, seq, past)
    y, _ = _mix_and_ffn(x, mod, p, oa, oc, urw, s0_bd, scan_consts, seq, True, 1)
    return y


def _cache_layout(zs, nreq, seq, heads):
    z = jnp.stack(zs, axis=0).reshape(len(zs), nreq, seq, heads, HEAD_DIM)
    return z.transpose(1, 0, 3, 2, 4)


def _tokens_first(z):
    b, nl, h, n, dh = z.shape
    return z.transpose(1, 0, 3, 2, 4).reshape(nl, b * n, h * dh)


def kernel(x_prompt, x_sample, cache_na_k, cache_na_v, cache_swa_k, cache_swa_v, state_rwkv, c, c_ctx, ada_w, ada_b, norm1_g, norm2_g, w_in, na_q_norm, na_k_norm, na_rpb, rw_mu, rw_w0, rw_w2, rw_a0, rw_a2, rw_g2, rw_k_k, rw_k_a, rw_r_k, rw_ln_g, rw_ln_b, swa_q_norm, swa_k_norm, swa_sink, w_out, w_router, w_gate, w_up, w_down):
    nb, seq, d = x_prompt.shape
    db, dseq, _ = x_sample.shape
    depth = ada_w.shape[0]
    past = cache_na_k.shape[3]
    cond = jnp.concatenate([c, c_ctx[None], jnp.zeros((16 - db - 1, d), F32)], axis=0)
    mod_all = _adaln(cond, ada_w, ada_b).reshape(depth, 16, 6, d)
    ones384 = _block_ones(NA_W)
    scan_consts = _rw_masks()
    rope_tabs = _rope_tables(dseq)
    xp = x_prompt.reshape(nb * seq, d)
    xs = x_sample.reshape(db * dseq, d)
    new_ka, new_va, new_kc, new_vc, new_s = [], [], [], [], []
    wg_bf, wu_bf, wd_bf = w_gate.astype(BF16), w_up.astype(BF16), w_down.astype(BF16)
    kx_na, vx_na = _tokens_first(cache_na_k), _tokens_first(cache_na_v)
    kx_swa, vx_swa = _tokens_first(cache_swa_k), _tokens_first(cache_swa_v)
    s0_lat = jnp.transpose(state_rwkv, (1, 0, 2, 4, 3, 5)).reshape(depth, db, 2, HEAD_DIM, RW_W)
    for l in range(depth):
        p = _layer_params(l, ada_w, ada_b, norm1_g, norm2_g, w_in, na_q_norm, na_k_norm, na_rpb, rw_mu, rw_w0,
                          rw_w2, rw_a0, rw_a2, rw_g2, rw_k_k, rw_k_a, rw_r_k, rw_ln_g, rw_ln_b, swa_q_norm,
                          swa_k_norm, swa_sink, w_out, w_router, wg_bf, wu_bf, wd_bf)
        mod_ctx = mod_all[l, db:db + 1]
        mod_lat = mod_all[l, 0:db]
        xp, ka, va, kc, vc, sfin = _context_layer(xp, mod_ctx, p, ones384, scan_consts, seq)
        new_ka.append(ka)
        new_va.append(va)
        new_kc.append(kc)
        new_vc.append(vc)
        new_s.append(sfin)
        xs = _latent_layer(xs, mod_lat, p, ones384, scan_consts, rope_tabs, dseq,
                           kx_na[l], vx_na[l], kx_swa[l], vx_swa[l], s0_lat[l], past)
    states = jnp.stack(new_s, axis=0).reshape(depth, nb, 2, HEAD_DIM, RW_HEADS, HEAD_DIM)
    return (xp.reshape(nb, seq, d), xs.reshape(db, dseq, d),
            _cache_layout(new_ka, nb, seq, NA_HEADS), _cache_layout(new_va, nb, seq, NA_HEADS),
            _cache_layout(new_kc, nb, seq, SWA_KV_HEADS), _cache_layout(new_vc, nb, seq, SWA_KV_HEADS),
            jnp.transpose(states, (1, 0, 2, 4, 3, 5)))
```

```python
import functools

import numpy as np
import jax
import jax.numpy as jnp
from jax import lax
from jax.experimental import pallas as pl
from jax.experimental.pallas import tpu as pltpu

F32 = jnp.float32
BF16 = jnp.bfloat16

HEAD_DIM = 64
GRID_W = 64
NA_HEADS = 6
NA_KH = 8
NA_KW = 16
RW_HEADS = 4
SWA_HEADS = 6
SWA_KV_HEADS = 2
SWA_WINDOW = 128
N_EXPERTS = 16
EC_CAPACITY = 2
ROPE_THETA = 10000.0
NORM_EPS = 1e-6
GN_EPS = 64e-5
NEG_INF = -1e30
SUB_ROWS = 256
NA_UNROLL = 8
SWA_UNROLL = 2
RW_CHUNK = 64
assert RW_CHUNK == HEAD_DIM
CTX_GROUP = 16
RW_W = RW_HEADS * HEAD_DIM
NA_W = NA_HEADS * HEAD_DIM
SWA_W = SWA_HEADS * HEAD_DIM
SWA_KV_W = SWA_KV_HEADS * HEAD_DIM
RW_IN_W = 1152
VMEM_LIMIT = 56 * 1024 * 1024


def _cp(*sem):
    return pltpu.CompilerParams(dimension_semantics=sem, vmem_limit_bytes=VMEM_LIMIT)


def _bf(x):
    return x.astype(BF16)


def _dot(a, b):
    return jnp.dot(a, b, preferred_element_type=F32)


def _dot_nt(a, b):
    return lax.dot_general(a, b, (((1,), (1,)), ((), ())), preferred_element_type=F32)


def _dot_tn(a, b):
    return lax.dot_general(a, b, (((0,), (0,)), ((), ())), preferred_element_type=F32)


def _split2(x):
    hi = x.astype(BF16)
    lo = (x - hi.astype(F32)).astype(BF16)
    return hi, lo


def _split3(x):
    hi = x.astype(BF16)
    r1 = x - hi.astype(F32)
    mid = r1.astype(BF16)
    lo = (r1 - mid.astype(F32)).astype(BF16)
    return hi, mid, lo


def _dot2(a, b_bf):
    hi, lo = _split2(a)
    return _dot(hi, b_bf) + _dot(lo, b_bf)


def _sigmoid(x):
    return 1.0 / (1.0 + jnp.exp(-x))


def _block_ones(width):
    i = np.arange(width) // HEAD_DIM
    return jnp.asarray((i[:, None] == i[None, :]).astype(np.float32), dtype=BF16)


def _row_tile(seq):
    return 512 if seq % 512 == 0 else 256


def _full(shape):
    return pl.BlockSpec(shape, lambda *_: (0,) * len(shape))


def _adaln_kernel(c_ref, w_ref, b_ref, o_ref):
    c = c_ref[...]
    s = c * _sigmoid(c)
    shi, slo = _split2(s)
    whi, wlo = _split2(w_ref[0])
    o_ref[0] = _dot(shi, whi) + _dot(slo, whi) + _dot(shi, wlo) + b_ref[0]


def _adaln(cond, ada_w, ada_b):
    nl, d, n6 = ada_w.shape
    tn = 1536
    rows = cond.shape[0]
    return pl.pallas_call(
        _adaln_kernel,
        grid=(nl, n6 // tn),
        in_specs=[pl.BlockSpec((rows, d), lambda l, j: (0, 0)),
                  pl.BlockSpec((1, d, tn), lambda l, j: (l, 0, j)),
                  pl.BlockSpec((1, 1, tn), lambda l, j: (l, 0, j))],
        out_specs=pl.BlockSpec((1, rows, tn), lambda l, j: (l, 0, j)),
        out_shape=jax.ShapeDtypeStruct((nl, rows, n6), F32),
        compiler_params=_cp("parallel", "parallel"),
        name="adaln",
    )(cond, ada_w, ada_b.reshape(nl, 1, n6))


def _head_sums(zz, ones128):
    zz = _bf(zz)
    parts = [_dot(zz[:, i:i + 128], ones128) for i in range(0, zz.shape[1], 128)]
    return jnp.concatenate(parts, axis=1) if len(parts) > 1 else parts[0]


def _head_norm(z, gain, ones128):
    ms = _head_sums(z * z, ones128) * (1.0 / HEAD_DIM)
    return z * lax.rsqrt(ms + NORM_EPS) * gain


def _rope(z, cos, sin_signed):
    w = z.shape[1]
    lane = lax.broadcasted_iota(jnp.int32, z.shape, 1)
    first = (lane % HEAD_DIM) < (HEAD_DIM // 2)
    swapped = jnp.where(first, pltpu.roll(z, w - HEAD_DIM // 2, 1), pltpu.roll(z, HEAD_DIM // 2, 1))
    return z * cos + swapped * sin_signed


def _proj_kernel(*refs, rope, split_rw):
    refs = list(refs)
    x_ref, mod_ref, n1_ref, w_ref, gqa_ref, gka_ref, gqc_ref, gkc_ref, ones_ref = refs[:9]
    qa_ref, ka_ref, va_ref, urw_ref, qc_ref, kc_ref, vc_ref = refs[-7:]
    extra = refs[9:-7]
    if rope:
        cos_ref, sin_ref = extra[:2]
    sh1 = mod_ref[0, 0:1, :]
    sc1 = mod_ref[0, 1:2, :]
    o0 = 0
    o1 = NA_W
    o2 = 2 * NA_W
    o3 = 3 * NA_W
    o4 = o3 + RW_IN_W
    o5 = o4 + SWA_W
    o6 = o5 + SWA_KV_W
    ones = ones_ref[0:128, 0:128]
    ones_kv = ones
    subs = [slice(i, i + SUB_ROWS) for i in range(0, x_ref.shape[0], SUB_ROWS)]
    hs, us = [], []
    for sl in subs:
        x = x_ref[sl, :]
        ms = jnp.mean(x * x, axis=-1, keepdims=True)
        hs.append(x * lax.rsqrt(ms + NORM_EPS) * n1_ref[...] * (1.0 + sc1) + sh1)
    h_hi = [_bf(h) for h in hs]
    us = [_dot(hh, w_ref[...]) for hh in h_hi]
    for sl, h, hh, u in zip(subs, hs, h_hi, us):
        qa = _head_norm(u[:, o0:o1], gqa_ref[...], ones)
        ka = _head_norm(u[:, o1:o2], gka_ref[...], ones)
        qc = _head_norm(u[:, o4:o5], gqc_ref[...], ones)
        kc = _head_norm(u[:, o5:o6], gkc_ref[...], ones_kv)
        if rope:
            qc = _rope(qc, cos_ref[sl, :], sin_ref[sl, :])
            kc = _rope(kc, cos_ref[sl, 0:SWA_KV_W], sin_ref[sl, 0:SWA_KV_W])
        qa_ref[sl, :] = qa.astype(qa_ref.dtype)
        ka_ref[sl, :] = ka.astype(ka_ref.dtype)
        va_ref[sl, :] = u[:, o2:o3].astype(va_ref.dtype)
        urw = u[:, o3:o4]
        if split_rw:
            h_lo = _bf(h - hh.astype(F32))
            urw = urw + _dot(h_lo, w_ref[:, o3:o4]) + _dot(hh, extra[-1][...])
        urw_ref[sl, :] = urw
        qc_ref[sl, :] = qc.astype(qc_ref.dtype)
        kc_ref[sl, :] = kc.astype(kc_ref.dtype)
        vc_ref[sl, :] = u[:, o6:].astype(vc_ref.dtype)


def _proj(x, mod, n1, w_in_bf, gains, ones384, rope_tabs, seq, per_request_mod, qkv_dtype, w_rw_lo=None):
    tokens, d = x.shape
    tm = _row_tile(seq)
    tiles_per_req = seq // tm
    in_w = w_in_bf.shape[1]
    rope = rope_tabs is not None
    mod_map = (lambda i: (i // tiles_per_req, 0, 0)) if per_request_mod else (lambda i: (0, 0, 0))
    row = lambda w: pl.BlockSpec((tm, w), lambda i: (i, 0))
    in_specs = [row(d), pl.BlockSpec((1, 6, d), mod_map), _full((1, d)), _full((d, in_w)),
                _full((1, NA_W)), _full((1, NA_W)), _full((1, SWA_W)), _full((1, SWA_KV_W)),
                _full((NA_W, NA_W))]
    args = [x, mod, n1, w_in_bf, *gains, ones384]
    if rope:
        tab = pl.BlockSpec((tm, SWA_W), lambda i: (i % tiles_per_req, 0))
        in_specs += [tab, tab]
        args += list(rope_tabs)
    if w_rw_lo is not None:
        in_specs.append(_full((d, RW_IN_W)))
        args.append(w_rw_lo)
    widths = [NA_W, NA_W, NA_W, RW_IN_W, SWA_W, SWA_KV_W, SWA_KV_W]
    dtypes = [qkv_dtype, qkv_dtype, qkv_dtype, F32, qkv_dtype, qkv_dtype, qkv_dtype]
    return pl.pallas_call(
        functools.partial(_proj_kernel, rope=rope, split_rw=w_rw_lo is not None),
        grid=(tokens // tm,),
        in_specs=in_specs,
        out_specs=[row(w) for w in widths],
        out_shape=[jax.ShapeDtypeStruct((tokens, w), dt) for w, dt in zip(widths, dtypes)],
        compiler_params=_cp("parallel"),
        name="proj",
    )(*args)


def _half_masks(width=2 * HEAD_DIM):
    lane = lax.broadcasted_iota(jnp.int32, (1, width), 1)
    return lane < HEAD_DIM, lane >= HEAD_DIM


def _swap_halves(z):
    return pltpu.roll(z, HEAD_DIM, 1)


def _ctx_attn_kernel(sink_ref, qa_ref, ka_ref, va_ref, qc_ref, kc_ref, vc_ref, oa_ref, oc_ref):
    scale = HEAD_DIM ** -0.5
    m0, m1 = _half_masks()
    masks = (m0, m1)
    for pair in range(NA_HEADS // 2):
        sl = slice(pair * 128, (pair + 1) * 128)
        qp = qa_ref[:, sl].astype(F32) * scale
        kp = _bf(ka_ref[:, sl])
        vp = _bf(va_ref[:, sl])
        outs = []
        for half in range(2):
            qm = _bf(jnp.where(masks[half], qp, 0.0))
            s = _dot_nt(qm, kp)
            m = jnp.max(s, axis=-1, keepdims=True)
            e = jnp.exp(s - m)
            l = jnp.sum(e, axis=-1, keepdims=True)
            outs.append(_dot(_bf(e), vp) / l)
        oa_ref[:, sl] = jnp.where(m0, outs[0], outs[1])
    kc = _bf(kc_ref[...])
    vc = _bf(vc_ref[...])
    group = SWA_HEADS // SWA_KV_HEADS
    for pair in range(SWA_HEADS // 2):
        sl = slice(pair * 128, (pair + 1) * 128)
        qp = qc_ref[:, sl].astype(F32) * scale
        outs = []
        for half in range(2):
            h = 2 * pair + half
            g = h // group
            qh = qp if g == half else _swap_halves(qp)
            qm = _bf(jnp.where(masks[g], qh, 0.0))
            s = _dot_nt(qm, kc)
            sk = sink_ref[h]
            m = jnp.maximum(jnp.max(s, axis=-1, keepdims=True), sk)
            e = jnp.exp(s - m)
            l = jnp.sum(e, axis=-1, keepdims=True) + jnp.exp(sk - m)
            o = _dot(_bf(e), vc) / l
            outs.append(o if g == half else _swap_halves(o))
        oc_ref[:, sl] = jnp.where(m0, outs[0], outs[1])


def _ctx_attn(sink, qa, ka, va, qc, kc, vc, seq):
    tokens = qa.shape[0]
    blk = lambda w: pl.BlockSpec((seq, w), lambda b: (b, 0))
    return pl.pallas_call(
        _ctx_attn_kernel,
        grid=(tokens // seq,),
        in_specs=[pl.BlockSpec(memory_space=pltpu.SMEM), blk(NA_W), blk(NA_W), blk(NA_W), blk(SWA_W),
                  blk(SWA_KV_W), blk(SWA_KV_W)],
        out_specs=[blk(NA_W), blk(SWA_W)],
        out_shape=[jax.ShapeDtypeStruct((tokens, NA_W), F32), jax.ShapeDtypeStruct((tokens, SWA_W), F32)],
        compiler_params=_cp("parallel"),
        name="ctx_attn",
    )(sink, qa, ka, va, qc, kc, vc)


def _na_bias_kernel(rpb_ref, o_ref):
    h = pl.program_id(0)
    nrow = 2 * NA_KH - 1
    ncol = 2 * NA_KW - 1
    width = NA_KH * GRID_W
    shape = (GRID_W, width)
    lane = lax.broadcasted_iota(jnp.int32, shape, 1)
    qc = lax.broadcasted_iota(jnp.int32, shape, 0)
    kc = lane % GRID_W
    c_start = jnp.clip(qc - NA_KW // 2, 0, GRID_W - NA_KW)
    ok = (kc >= c_start) & (kc < c_start + NA_KW)
    d_col = jnp.clip(kc - qc, 1 - NA_KW, NA_KW - 1) + NA_KW - 1
    key_row = lax.broadcasted_iota(jnp.int32, (1, width), 1) // GRID_W

    def case_body(case, carry):
        acc = jnp.zeros(shape, F32)
        for dc in range(ncol):
            val = jnp.zeros((1, width), F32)
            for i in range(NA_KH):
                val = jnp.where(key_row == i, rpb_ref[(h * nrow + case + i) * ncol + dc], val)
            acc = jnp.where(d_col == dc, val, acc)
        o_ref[0, pl.ds(case, 1)] = jnp.where(ok, acc, NEG_INF)[None]
        return carry

    lax.fori_loop(0, NA_KH, case_body, 0)


def _na_bias_table(rpb):
    nh = rpb.shape[0]
    return pl.pallas_call(
        _na_bias_kernel,
        grid=(nh,),
        in_specs=[pl.BlockSpec(memory_space=pltpu.SMEM)],
        out_specs=pl.BlockSpec((1, NA_KH, GRID_W, NA_KH * GRID_W), lambda h: (h // 2, 0, h % 2, 0)),
        out_shape=jax.ShapeDtypeStruct((nh // 2, NA_KH, 2 * GRID_W, NA_KH * GRID_W), F32),
        compiler_params=_cp("parallel"),
        name="na_bias",
    )(rpb.reshape(-1))


def _na_kernel(q_ref, k_ref, v_ref, kx_ref, vx_ref, bias_ref, o_ref, *, rows):
    scale = HEAD_DIM ** -0.5
    m0, m1 = _half_masks()
    kx = _bf(kx_ref[...])
    vx = _bf(vx_ref[...])

    def body(it, carry):
        us = range(NA_UNROLL)
        r = [it * NA_UNROLL + u for u in us]
        rs = [jnp.clip(r[u] - NA_KH // 2, 0, rows - NA_KH) for u in us]
        case = [rs[u] - r[u] + NA_KH - 1 for u in us]
        q0 = [pl.multiple_of(r[u] * GRID_W, GRID_W) for u in us]
        k0 = [pl.multiple_of(rs[u] * GRID_W, GRID_W) for u in us]
        qp = [q_ref[pl.ds(q0[u], GRID_W), :].astype(F32) * scale for u in us]
        kw = [_bf(k_ref[pl.ds(k0[u], NA_KH * GRID_W), :]) for u in us]
        vw = [_bf(v_ref[pl.ds(k0[u], NA_KH * GRID_W), :]) for u in us]
        q2 = [_bf(jnp.concatenate([jnp.where(m0, qp[u], 0.0), jnp.where(m1, qp[u], 0.0)], axis=0)) for u in us]
        s = [jnp.concatenate([_dot_nt(q2[u], kw[u]) + bias_ref[0, pl.ds(case[u], 1)][0], _dot_nt(q2[u], kx)],
                             axis=1) for u in us]
        m = [jnp.max(s[u], axis=-1, keepdims=True) for u in us]
        e = [jnp.exp(s[u] - m[u]) for u in us]
        l = [jnp.sum(e[u], axis=-1, keepdims=True) for u in us]
        o = [_dot(_bf(e[u]), jnp.concatenate([vw[u], vx], axis=0)) / l[u] for u in us]
        for u in us:
            o_ref[pl.ds(q0[u], GRID_W), :] = jnp.where(m0, o[u][0:GRID_W], o[u][GRID_W:])
        return carry

    lax.fori_loop(0, rows // NA_UNROLL, body, 0)


def _na_attn(q, k, v, kx, vx, bias, seq, past):
    tokens = q.shape[0]
    nb = tokens // seq
    rows = seq // GRID_W
    blk = pl.BlockSpec((seq, 128), lambda b, p: (b, p))
    cblk = pl.BlockSpec((past, 128), lambda b, p: (b, p))
    return pl.pallas_call(
        functools.partial(_na_kernel, rows=rows),
        grid=(nb, NA_HEADS // 2),
        in_specs=[blk, blk, blk, cblk, cblk,
                  pl.BlockSpec((1, NA_KH, 2 * GRID_W, NA_KH * GRID_W), lambda b, p: (p, 0, 0, 0))],
        out_specs=blk,
        out_shape=jax.ShapeDtypeStruct((tokens, NA_W), F32),
        compiler_params=_cp("parallel", "parallel"),
        name="na_attn",
    )(q, k, v, kx, vx, bias)


def _swa_kernel(sink_ref, q_ref, k_ref, v_ref, kx_ref, vx_ref, o_ref, *, seq):
    scale = HEAD_DIM ** -0.5
    blk = SWA_WINDOW
    m0, m1 = _half_masks()
    masks = (m0, m1)
    kx = _bf(kx_ref[...])
    vx = _bf(vx_ref[...])
    group = SWA_HEADS // SWA_KV_HEADS

    sk = []
    for g in range(SWA_KV_HEADS):
        sk.append(jnp.concatenate([jnp.full((blk, 1), sink_ref[h], F32) for h in range(g * group, (g + 1) * group)],
                                  axis=0))

    def body(it, carry):
        us = range(SWA_UNROLL)
        cs = [(u, g) for u in us for g in range(SWA_KV_HEADS)]
        nb = [it * SWA_UNROLL + u for u in us]
        ks = [pl.multiple_of(jnp.clip((nb[u] - 1) * blk, 0, seq - 3 * blk), blk) for u in us]
        q0 = [pl.multiple_of(nb[u] * blk, blk) for u in us]
        kw = [_bf(k_ref[pl.ds(ks[u], 3 * blk), :]) for u in us]
        vw = [_bf(v_ref[pl.ds(ks[u], 3 * blk), :]) for u in us]
        ok = []
        for u in us:
            qpos = q0[u] + lax.broadcasted_iota(jnp.int32, (group * blk, 1), 0) % blk
            kpos = ks[u] + lax.broadcasted_iota(jnp.int32, (1, 3 * blk), 1)
            ok.append(jnp.abs(qpos - kpos) <= SWA_WINDOW)
        qg = {}
        for u in us:
            pairs = [q_ref[pl.ds(q0[u], blk), p * 128:(p + 1) * 128].astype(F32) * scale
                     for p in range(SWA_HEADS // 2)]
            for g in range(SWA_KV_HEADS):
                qs = []
                for h in range(g * group, (g + 1) * group):
                    qh = pairs[h // 2] if h % 2 == g else _swap_halves(pairs[h // 2])
                    qs.append(jnp.where(masks[g], qh, 0.0))
                qg[u, g] = _bf(jnp.concatenate(qs, axis=0))
        sw = {c: jnp.where(ok[c[0]], _dot_nt(qg[c], kw[c[0]]), NEG_INF) for c in cs}
        sx = {c: _dot_nt(qg[c], kx) for c in cs}
        m = {c: jnp.maximum(jnp.maximum(jnp.max(sw[c], axis=-1, keepdims=True),
                                        jnp.max(sx[c], axis=-1, keepdims=True)), sk[c[1]]) for c in cs}
        ew = {c: jnp.exp(sw[c] - m[c]) for c in cs}
        ex = {c: jnp.exp(sx[c] - m[c]) for c in cs}
        l = {c: jnp.sum(ew[c], axis=-1, keepdims=True) + jnp.sum(ex[c], axis=-1, keepdims=True)
             + jnp.exp(sk[c[1]] - m[c]) for c in cs}
        o = {c: (_dot(_bf(ew[c]), vw[c[0]]) + _dot(_bf(ex[c]), vx)) / l[c] for c in cs}
        for u in us:
            head_out = []
            for g in range(SWA_KV_HEADS):
                for i in range(group):
                    h = g * group + i
                    oh = o[u, g][i * blk:(i + 1) * blk]
                    head_out.append(oh if h % 2 == g else _swap_halves(oh))
            for p in range(SWA_HEADS // 2):
                o_ref[pl.ds(q0[u], blk), p * 128:(p + 1) * 128] = jnp.where(m0, head_out[2 * p], head_out[2 * p + 1])
        return carry

    lax.fori_loop(0, seq // (blk * SWA_UNROLL), body, 0)


def _swa_attn(sink, q, k, v, kx, vx, seq, past):
    tokens = q.shape[0]
    blk = lambda w: pl.BlockSpec((seq, w), lambda b: (b, 0))
    cblk = pl.BlockSpec((past, SWA_KV_W), lambda b: (b, 0))
    return pl.pallas_call(
        functools.partial(_swa_kernel, seq=seq),
        grid=(tokens // seq,),
        in_specs=[pl.BlockSpec(memory_space=pltpu.SMEM), blk(SWA_W), blk(SWA_KV_W), blk(SWA_KV_W), cblk, cblk],
        out_specs=blk(SWA_W),
        out_shape=jax.ShapeDtypeStruct((tokens, SWA_W), F32),
        compiler_params=_cp("parallel"),
        name="swa_attn",
    )(sink, q, k, v, kx, vx)


def _rw_pre_kernel(u_ref, up_ref, un_ref, mu_ref, w0_ref, w2_ref, a0_ref, a2_ref, g2_ref, kk_ref_, ka_ref_,
                   rk_ref, ones_ref, r_o, kk_o, v_o, ld_o, kka_o, kd_o, g_o, bonus_o, *, tiles_per_req):
    i = pl.program_id(0)
    u = u_ref[...]
    tm = u.shape[0]
    rowi = lax.broadcasted_iota(jnp.int32, (tm, 1), 0)
    first = (i % tiles_per_req) == 0
    last = (i % tiles_per_req) == tiles_per_req - 1
    prev_row = jnp.where(first, 0.0, up_ref[7:8, :])
    next_row = jnp.where(last, 0.0, un_ref[0:1, :])
    prev = jnp.where(rowi == 0, prev_row, pltpu.roll(u, 1, 0))
    nxt = jnp.where(rowi == tm - 1, next_row, pltpu.roll(u, tm - 1, 0))
    us = u + mu_ref[0:1, :] * (prev - u) + mu_ref[1:2, :] * (nxt - u)
    r = us[:, 0:RW_W]
    k = us[:, RW_W:2 * RW_W]
    v = us[:, 2 * RW_W:3 * RW_W]
    wl = us[:, 3 * RW_W:3 * RW_W + 128]
    al = us[:, 3 * RW_W + 128:3 * RW_W + 256]
    gl = us[:, 3 * RW_W + 256:3 * RW_W + 384]
    z = -(w0_ref[...] + _dot(_bf(jnp.tanh(wl)), w2_ref[...]))
    softplus = jnp.maximum(z, 0.0) + jnp.log(1.0 + jnp.exp(-jnp.abs(z)))
    w = -softplus - 0.5
    ld = -jnp.exp(w)
    a = _sigmoid(a0_ref[...] + _dot(_bf(al), a2_ref[...]))
    g = _dot(_bf(_sigmoid(gl)), g2_ref[...])
    ones = ones_ref[...]
    kkr = k * kk_ref_[...]
    kk = kkr * lax.rsqrt(jnp.maximum(_dot2(kkr * kkr, ones), 1e-24))
    k_a = ka_ref_[...]
    kd_f = k * (1.0 + (a[:, 0:RW_W] - 1.0) * k_a)
    kd_b = k * (1.0 + (a[:, RW_W:] - 1.0) * k_a)
    r_o[...] = r
    kk_o[...] = kk
    v_o[...] = v
    ld_o[...] = ld
    kka_o[:, 0:RW_W] = kk * a[:, 0:RW_W]
    kka_o[:, RW_W:] = kk * a[:, RW_W:]
    kd_o[:, 0:RW_W] = kd_f
    kd_o[:, RW_W:] = kd_b
    g_o[...] = g
    bonus_o[...] = _dot2(r * (kd_f + kd_b) * rk_ref[...], ones) * v


def _rw_pre(urw, p, seq):
    tokens = urw.shape[0]
    tm = _row_tile(seq)
    tpr = seq // tm
    nt = tokens // tm
    r8 = tm // 8
    row = lambda w: pl.BlockSpec((tm, w), lambda i: (i, 0))
    in_specs = [row(RW_IN_W),
                pl.BlockSpec((8, RW_IN_W), lambda i: (jnp.maximum(i * r8 - 1, 0), 0)),
                pl.BlockSpec((8, RW_IN_W), lambda i: (jnp.minimum((i + 1) * r8, nt * r8 - 1), 0)),
                _full((2, RW_IN_W)), _full((1, 2 * RW_W)), _full((128, 2 * RW_W)), _full((1, 2 * RW_W)),
                _full((128, 2 * RW_W)), _full((128, RW_W)), _full((1, RW_W)), _full((1, RW_W)),
                _full((1, RW_W)), _full((RW_W, RW_W))]
    widths = [RW_W, RW_W, RW_W, 2 * RW_W, 2 * RW_W, 2 * RW_W, RW_W, RW_W]
    return pl.pallas_call(
        functools.partial(_rw_pre_kernel, tiles_per_req=tpr),
        grid=(nt,),
        in_specs=in_specs,
        out_specs=[row(w) for w in widths],
        out_shape=[jax.ShapeDtypeStruct((tokens, w), F32) for w in widths],
        compiler_params=_cp("parallel"),
        name="rw_pre",
    )(urw, urw, urw, p['mu'], p['w0'], p['w2'], p['a0'], p['a2'], p['g2'], p['k_k'], p['k_a'], p['r_k'],
      p['ones256'])


def _rw_masks():
    t = RW_CHUNK
    n = RW_HEADS * t
    tt = np.arange(t)[:, None]
    ss = (np.arange(n) % t)[None, :]
    before = np.stack([ss < tt, ss > tt])
    diag = (ss == tt)
    strict = before.astype(np.float32)
    incl = (before | diag[None]).astype(np.float32)
    eye = diag.astype(np.float32)
    ti = np.arange(t)
    tri = np.stack([ti[None, :] <= ti[:, None], ti[None, :] >= ti[:, None]]).astype(np.float32)
    hd = np.arange(n) // t
    same = (hd[:, None] == hd[None, :]).astype(np.float32)
    return (jnp.asarray(strict), jnp.asarray(incl), jnp.asarray(tri, dtype=BF16), jnp.asarray(same),
            jnp.asarray(eye))


def _rw_scan_kernel(r_ref, kk_ref, v_ref, ld_ref, kka_ref, kd_ref, s0_ref, strict_ref, incl_ref, tri_ref,
                    same_ref, eye_ref, o_ref, sfin_ref, s_scr, *, nsub):
    d = pl.program_id(1)
    c = pl.program_id(2)
    t = RW_CHUNK
    n = RW_HEADS * t

    @pl.when(c == 0)
    def _():
        s_scr[...] = s0_ref[0, 0]

    strict = strict_ref[0]
    incl = incl_ref[0]
    tri = tri_ref[0]
    eye = eye_ref[...]
    same = same_ref[...]
    same_bf = _bf(same)

    def bd(x):
        return jnp.concatenate([_bf(x)] * RW_HEADS, axis=0) * same_bf

    js = range(nsub)
    rows = [pl.ds(pl.multiple_of((j + d * (nsub - 1 - 2 * j)) * t, t), t) for j in js]
    ld = [ld_ref[rows[j], :] for j in js]
    cum = []
    for j in js:
        lhi, lmid, llo = _split3(ld[j])
        cum.append(_dot(tri, lhi) + _dot(tri, lmid) + _dot(tri, llo))
    cend = [jnp.sum(ld[j], axis=0, keepdims=True) for j in js]
    kka = [kka_ref[rows[j], :] for j in js]
    kd = [kd_ref[rows[j], :] for j in js]
    v = [v_ref[rows[j], :] for j in js]
    at = [-kk_ref[rows[j], :] * jnp.exp(cum[j] - ld[j]) for j in js]
    rt = [r_ref[rows[j], :] * jnp.exp(cum[j]) for j in js]
    e_inv = [jnp.exp(-cum[j]) for j in js]
    aa = [_dot_nt(_bf(jnp.concatenate([at[j], rt[j]], axis=0)),
                  jnp.concatenate([bd(kka[j] * e_inv[j]), bd(kd[j] * e_inv[j])], axis=0)) for j in js]
    a_ab = [aa[j][0:t, 0:n] * strict for j in js]
    x = [eye + a_ab[j] for j in js]
    pw = a_ab
    for _ in range(RW_CHUNK.bit_length() - 3):
        pw = [_dot(_bf(pw[j]), bd(pw[j])) for j in js]
        x = [x[j] + _dot(_bf(x[j]), bd(pw[j])) for j in js]
    xs = [_split2(x[j]) for j in js]
    sa = [_split2(a_ab[j]) for j in js]
    ax = [_dot(sa[j][0], bd(xs[j][0])) + _dot(sa[j][1], bd(xs[j][0])) + _dot(sa[j][0], bd(xs[j][1])) for j in js]
    x = [x[j] + _dot(xs[j][0], bd(eye - x[j] + ax[j])) for j in js]
    v_bd = [bd(v[j]) for j in js]
    wv = [_dot(_bf(aa[j][0:t, n:] * strict), v_bd[j]) for j in js]
    mu = [_dot(_bf(x[j]), jnp.concatenate([bd(at[j]), bd(wv[j])], axis=1)) for j in js]
    m1 = [mu[j][:, 0:n] for j in js]
    u0 = [mu[j][:, n:] for j in js]
    e_end = [jnp.exp(cend[j] - cum[j]) for j in js]
    bend = [_bf(kka[j] * e_end[j]) for j in js]
    g = [_bf(_dot_tn(_bf(m1[j]), bend[j]) * same) for j in js]
    cst = []
    for j in js:
        full = _dot_tn(_bf(jnp.concatenate([u0[j], v[j]], axis=0)),
                       jnp.concatenate([bend[j], _bf(kd[j] * e_end[j])], axis=0)) * same
        cst.append(functools.reduce(jnp.add, [full[h * t:(h + 1) * t] for h in range(RW_HEADS)]))
    qo = [_dot(_bf(aa[j][t:, 0:n] * incl), jnp.concatenate([bd(m1[j]), bd(u0[j])], axis=1)) for j in js]
    q = [_bf(rt[j] + qo[j][:, 0:n]) for j in js]
    o0 = [qo[j][:, n:] + _dot(_bf(aa[j][t:, n:] * incl), v_bd[j]) for j in js]

    s = s_scr[...]
    for j in js:
        o_ref[0, rows[j], :] = _dot_nt(q[j], bd(s)) + o0[j]
        s = s * jnp.exp(cend[j]) + _dot(_bf(s), g[j]) + cst[j]
    s_scr[...] = s

    @pl.when(c == pl.num_programs(2) - 1)
    def _():
        sfin_ref[0, 0] = s


def _rw_scan(r, kk, v, ld, kka, kd, s0_bd, consts, seq):
    tokens = r.shape[0]
    nreq = tokens // seq
    tb = min(seq, 1024)
    nblk = seq // tb
    nsub = tb // RW_CHUNK
    n = RW_HEADS * RW_CHUNK
    cc = lambda d, c: c + d * (nblk - 1 - 2 * c)
    shared = pl.BlockSpec((tb, RW_W), lambda b, d, c: (b * nblk + cc(d, c), 0))
    dirw = pl.BlockSpec((tb, RW_W), lambda b, d, c: (b * nblk + cc(d, c), d))
    strict, incl, tri, same, eye = consts
    return pl.pallas_call(
        functools.partial(_rw_scan_kernel, nsub=nsub),
        grid=(nreq, 2, nblk),
        in_specs=[shared, shared, shared, dirw, dirw, dirw,
                  pl.BlockSpec((1, 1, HEAD_DIM, n), lambda b, d, c: (b, d, 0, 0)),
                  pl.BlockSpec((1, RW_CHUNK, n), lambda b, d, c: (d, 0, 0)),
                  pl.BlockSpec((1, RW_CHUNK, n), lambda b, d, c: (d, 0, 0)),
                  pl.BlockSpec((1, RW_CHUNK, RW_CHUNK), lambda b, d, c: (d, 0, 0)),
                  _full((n, n)), _full((RW_CHUNK, n))],
        out_specs=[pl.BlockSpec((1, tb, RW_W), lambda b, d, c: (d, b * nblk + cc(d, c), 0)),
                   pl.BlockSpec((1, 1, HEAD_DIM, n), lambda b, d, c: (b, d, 0, 0))],
        out_shape=[jax.ShapeDtypeStruct((2, tokens, RW_W), F32),
                   jax.ShapeDtypeStruct((nreq, 2, HEAD_DIM, n), F32)],
        scratch_shapes=[pltpu.VMEM((HEAD_DIM, n), F32)],
        compiler_params=_cp("parallel", "parallel", "arbitrary"),
        name="rw_scan",
    )(r, kk, v, ld, kka, kd, s0_bd, strict, incl, tri, same, eye)


def _finish_kernel(x_ref, oa_ref, oc_ref, o2_ref, bonus_ref, g_ref, mod_ref, wout_ref, lng_ref, lnb_ref,
                   n2_ref, wr_hi_ref, wr_lo_ref, ones_ref, x1_ref, h2_ref, aff_ref):
    ones = ones_ref[0:128, 0:128]
    g1 = mod_ref[0, 2:3, :]
    sh2 = mod_ref[0, 3:4, :]
    sc2 = mod_ref[0, 4:5, :]
    subs = [slice(i, i + SUB_ROWS) for i in range(0, x_ref.shape[0], SUB_ROWS)]
    mixins = []
    for sl in subs:
        y = o2_ref[0, sl, :] + o2_ref[1, sl, :]
        mu = _head_sums(y, ones) * (1.0 / HEAD_DIM)
        yc = y - mu
        var = _head_sums(yc * yc, ones) * (1.0 / HEAD_DIM)
        yn = yc * lax.rsqrt(var + GN_EPS) * lng_ref[...] + lnb_ref[...]
        ob = (yn + bonus_ref[sl, :]) * g_ref[sl, :]
        mixins.append(jnp.concatenate([_bf(oa_ref[sl, :]), _bf(ob), _bf(oc_ref[sl, :])], axis=1))
    mixes = [_dot(mixin, wout_ref[...]) for mixin in mixins]
    for sl, mix in zip(subs, mixes):
        x1 = x_ref[sl, :] + g1 * mix
        ms = jnp.mean(x1 * x1, axis=-1, keepdims=True)
        h2 = x1 * lax.rsqrt(ms + NORM_EPS) * n2_ref[...] * (1.0 + sc2) + sh2
        x1_ref[sl, :] = x1
        h2_ref[sl, :] = _bf(h2)
        hhi, hlo = _split2(h2)
        logits = _dot(hhi, wr_hi_ref[...]) + _dot(hlo, wr_hi_ref[...]) + _dot(hhi, wr_lo_ref[...])
        m = jnp.max(logits, axis=-1, keepdims=True)
        e = jnp.exp(logits - m)
        aff_ref[sl, :] = e / jnp.sum(e, axis=-1, keepdims=True)


def _finish(x, oa, oc, o2, bonus, g, mod, p, seq, per_request_mod):
    tokens, d = x.shape
    tm = _row_tile(seq)
    tpr = seq // tm
    mod_map = (lambda i: (i // tpr, 0, 0)) if per_request_mod else (lambda i: (0, 0, 0))
    row = lambda w: pl.BlockSpec((tm, w), lambda i: (i, 0))
    return pl.pallas_call(
        _finish_kernel,
        grid=(tokens // tm,),
        in_specs=[row(d), row(NA_W), row(SWA_W), pl.BlockSpec((2, tm, RW_W), lambda i: (0, i, 0)), row(RW_W),
                  row(RW_W), pl.BlockSpec((1, 6, d), mod_map), _full((d, d)), _full((1, RW_W)),
                  _full((1, RW_W)), _full((1, d)), _full((d, N_EXPERTS)), _full((d, N_EXPERTS)),
                  _full((RW_W, RW_W))],
        out_specs=[row(d), row(d), row(N_EXPERTS)],
        out_shape=[jax.ShapeDtypeStruct((tokens, d), F32), jax.ShapeDtypeStruct((tokens, d), BF16),
                   jax.ShapeDtypeStruct((tokens, N_EXPERTS), F32)],
        compiler_params=_cp("parallel"),
        name="finish",
    )(x, oa, oc, o2, bonus, g, mod, p['w_out'], p['ln_g'], p['ln_b'], p['n2'], p['wr_hi'], p['wr_lo'],
      p['ones256'])


def _topk_kernel(aff_ref, tri_ref, eye_ref, place_ref, slot_ref, slotrow_ref, gfull_ref, ends_ref, *, cap, group,
                 seq, tb):
    aff = aff_ref[...]
    bits = lax.bitcast_convert_type(aff, jnp.int32)
    capf = jnp.float32(cap)
    eye = eye_ref[...]
    ghi, gmid, glo = _split3(aff)
    aff_t = _dot_nt(eye, ghi) + _dot_nt(eye, gmid) + _dot_nt(eye, glo)
    bits_t = lax.bitcast_convert_type(aff_t, jnp.int32)
    rs = range(group)

    def bis(_, carry):
        los, his = carry
        nlo, nhi = [], []
        for r in rs:
            mid = los[r] + ((his[r] - los[r] + 1) >> 1)
            cnt = jnp.sum(jnp.where(bits_t[:, r * seq:(r + 1) * seq] >= mid, 1.0, 0.0), axis=1, keepdims=True)
            ge = cnt >= capf
            nlo.append(jnp.where(ge, mid, los[r]))
            nhi.append(jnp.where(ge, his[r], mid - 1))
        return tuple(nlo), tuple(nhi)

    lo0 = tuple(jnp.zeros((N_EXPERTS, 1), jnp.int32) for _ in rs)
    hi0 = tuple(jnp.full((N_EXPERTS, 1), 0x7F7FFFFF, jnp.int32) for _ in rs)
    thr_cols, _ = lax.fori_loop(0, 31, bis, (lo0, hi0))
    ri = lax.broadcasted_iota(jnp.int32, (N_EXPERTS, N_EXPERTS), 0)
    ci = lax.broadcasted_iota(jnp.int32, (N_EXPERTS, N_EXPERTS), 1)
    tri = tri_ref[...]
    for r in rs:
        thr = jnp.sum(jnp.where(ri == ci, thr_cols[r], 0), axis=0, keepdims=True)
        rows = slice(r * seq, (r + 1) * seq)
        gt = jnp.where(bits[rows] > thr, 1.0, 0.0)
        eq = jnp.where(bits[rows] == thr, 1.0, 0.0)
        need = capf - jnp.sum(gt, axis=0, keepdims=True)
        offset = float(r * cap)
        carry_g = jnp.zeros((1, N_EXPERTS), F32)
        carry_e = jnp.zeros((1, N_EXPERTS), F32)
        for blk in range(seq // tb):
            sl = slice(blk * tb, (blk + 1) * tb)
            out = slice(r * seq + blk * tb, r * seq + (blk + 1) * tb)
            pg = _dot(tri, _bf(gt[sl])) + carry_g
            pe = _dot(tri, _bf(eq[sl])) + carry_e
            carry_g = pg[tb - 1:tb, :]
            carry_e = pe[tb - 1:tb, :]
            sel = gt[sl] + eq[sl] * jnp.where(pe <= need, 1.0, 0.0)
            slot = jnp.where(sel > 0.5, pg + jnp.minimum(pe, need) - 1.0 + offset, -1.0)
            slot_ref[out, :] = slot
            ends_ref[r, blk:blk + 1, :] = carry_g + jnp.minimum(carry_e, need) + offset
            shi, slo = _split2(slot)
            slotrow_ref[0, :, 0, out] = _dot_nt(eye, shi) + _dot_nt(eye, slo)
            gfull_ref[out, :] = _bf(_dot(ghi[out], place_ref[0]) + _dot(gmid[out], place_ref[1])
                                    + _dot(glo[out], place_ref[2]))


def _topk(aff, seq, group):
    tokens = aff.shape[0]
    nreq = tokens // seq
    cap = EC_CAPACITY * seq // N_EXPERTS
    tb = min(seq, 512)
    ti = np.arange(tb)
    tri = jnp.asarray((ti[None, :] <= ti[:, None]).astype(np.float32), dtype=BF16)
    eye = jnp.asarray(np.eye(N_EXPERTS, dtype=np.float32), dtype=BF16)
    place = np.zeros((3, N_EXPERTS, 128), np.float32)
    for s in range(3):
        place[s, np.arange(N_EXPERTS), s * N_EXPERTS + np.arange(N_EXPERTS)] = 1.0
    place = jnp.asarray(place, dtype=BF16)
    nblk = seq // tb
    slot, slotrow, gfull, ends = pl.pallas_call(
        functools.partial(_topk_kernel, cap=cap, group=group, seq=seq, tb=tb),
        grid=(nreq // group,),
        in_specs=[pl.BlockSpec((group * seq, N_EXPERTS), lambda b: (b, 0)), _full((tb, tb)),
                  _full((N_EXPERTS, N_EXPERTS)), _full((3, N_EXPERTS, 128))],
        out_specs=[pl.BlockSpec((group * seq, N_EXPERTS), lambda b: (b, 0)),
                   pl.BlockSpec((1, N_EXPERTS, 1, group * seq), lambda b: (b, 0, 0, 0)),
                   pl.BlockSpec((group * seq, 128), lambda b: (b, 0)),
                   pl.BlockSpec((group, nblk, N_EXPERTS), lambda b: (b, 0, 0))],
        out_shape=[jax.ShapeDtypeStruct((tokens, N_EXPERTS), F32),
                   jax.ShapeDtypeStruct((nreq // group, N_EXPERTS, 1, group * seq), F32),
                   jax.ShapeDtypeStruct((tokens, 128), BF16),
                   jax.ShapeDtypeStruct((nreq, nblk, N_EXPERTS), F32)],
        compiler_params=_cp("parallel"),
        name="topk",
    )(aff, tri, eye, place)
    ends = ends.reshape(nreq // group, group * nblk, N_EXPERTS).transpose(0, 2, 1)
    return slot, slotrow, gfull, ends.astype(jnp.int32).reshape(-1), tb


MOE_EB = 8


def _moe_dispatch_kernel(ends_ref, h_ref, slotrow_ref, gfull_ref, xe_ref, gs_ref, *, ct, nch):
    gi = pl.program_id(0)
    eb = pl.program_id(1)
    c = pl.program_id(2)
    mt = 128

    @pl.when(c == 0)
    def _():
        xe_ref[...] = jnp.zeros_like(xe_ref)
        gs_ref[...] = jnp.zeros_like(gs_ref)

    jcol = lax.broadcasted_iota(jnp.int32, (mt, 1), 0)
    starts, his, pieces = [], [], []
    for i in range(MOE_EB):
        base = (gi * N_EXPERTS + eb * MOE_EB + i) * nch
        lo = jnp.where(c == 0, 0, ends_ref[base + jnp.maximum(c - 1, 0)])
        his.append(ends_ref[base + c])
        start = pl.multiple_of(jnp.minimum((lo // 16) * 16, ct - mt), 16)
        starts.append(start)
        pieces.append(_bf(jnp.where(slotrow_ref[0, i] == (jcol + start).astype(F32), 1.0, 0.0)))
    onehot = jnp.concatenate(pieces, axis=0)
    xw = _bf(_dot(onehot, h_ref[...]))
    gw = _bf(_dot(onehot, gfull_ref[...]))
    for i in range(MOE_EB):
        rows = pl.ds(starts[i], mt)
        xe_ref[0, i, rows, :] += xw[i * mt:(i + 1) * mt]
        gs_ref[0, i, rows, :] += gw[i * mt:(i + 1) * mt]
    for i in range(MOE_EB):
        for w in range(1, ct // mt):
            wlo = starts[i] + w * mt

            @pl.when(wlo < his[i])
            def _(i=i, wlo=wlo):
                ws = pl.multiple_of(jnp.minimum(wlo, ct - mt), 16)
                slot = slotrow_ref[0, i]
                hit = (jnp.where(slot == (jcol + ws).astype(F32), 1.0, 0.0)
                       * jnp.where(slot >= wlo.astype(F32), 1.0, 0.0))
                rows = pl.ds(ws, mt)
                xe_ref[0, i, rows, :] += _bf(_dot(_bf(hit), h_ref[...]))
                gs_ref[0, i, rows, :] += _bf(_dot(_bf(hit), gfull_ref[...]))


def _moe_dispatch(ends, h2, slotrow, gfull, lg, ct, kc):
    tokens, d = h2.shape
    ngrp = tokens // lg
    nch = lg // kc
    grid_spec = pltpu.PrefetchScalarGridSpec(
        num_scalar_prefetch=1,
        grid=(ngrp, N_EXPERTS // MOE_EB, nch),
        in_specs=[pl.BlockSpec((kc, d), lambda gi, eb, c, ends: (gi * nch + c, 0)),
                  pl.BlockSpec((1, MOE_EB, 1, kc), lambda gi, eb, c, ends: (gi, eb, 0, c)),
                  pl.BlockSpec((kc, 128), lambda gi, eb, c, ends: (gi * nch + c, 0))],
        out_specs=[pl.BlockSpec((1, MOE_EB, ct, d), lambda gi, eb, c, ends: (gi, eb, 0, 0)),
                   pl.BlockSpec((1, MOE_EB, ct, 128), lambda gi, eb, c, ends: (gi, eb, 0, 0))])
    return pl.pallas_call(
        functools.partial(_moe_dispatch_kernel, ct=ct, nch=nch),
        grid_spec=grid_spec,
        out_shape=[jax.ShapeDtypeStruct((ngrp, N_EXPERTS, ct, d), BF16),
                   jax.ShapeDtypeStruct((ngrp, N_EXPERTS, ct, 128), BF16)],
        compiler_params=_cp("parallel", "parallel", "arbitrary"),
        name="moe_dispatch",
    )(ends, h2, slotrow, gfull)


def _moe_ffn_kernel(xe_ref, gs_ref, mod_ref, wg_ref, wu_ref, wd_ref, ye_ref):
    e = pl.program_id(1)
    lane = lax.broadcasted_iota(jnp.int32, (1, 128), 1)
    pick = (lane == e) | (lane == e + N_EXPERTS) | (lane == e + 2 * N_EXPERTS)
    gate = jnp.sum(jnp.where(pick, gs_ref[0, 0].astype(F32), 0.0), axis=-1, keepdims=True)
    xb = xe_ref[0, 0]
    hg = _dot(xb, wg_ref[0, 0])
    hu = _dot(xb, wu_ref[0, 0])
    he = _bf(hg * _sigmoid(hg) * hu)
    y = _dot(he, wd_ref[0, 0])
    ye_ref[0, 0] = _bf(y * gate * mod_ref[0, 5:6, :])


def _moe_ffn(xe, gs, mod, wg, wu, wd, layer, per_group_mod):
    ngrp, _, ct, d = xe.shape
    f = wg.shape[3]
    mod_map = (lambda gi, e: (gi, 0, 0)) if per_group_mod else (lambda gi, e: (0, 0, 0))
    return pl.pallas_call(
        _moe_ffn_kernel,
        grid=(ngrp, N_EXPERTS),
        in_specs=[pl.BlockSpec((1, 1, ct, d), lambda gi, e: (gi, e, 0, 0)),
                  pl.BlockSpec((1, 1, ct, 128), lambda gi, e: (gi, e, 0, 0)),
                  pl.BlockSpec((1, 6, d), mod_map),
                  pl.BlockSpec((1, 1, d, f), lambda gi, e: (layer, e, 0, 0)),
                  pl.BlockSpec((1, 1, d, f), lambda gi, e: (layer, e, 0, 0)),
                  pl.BlockSpec((1, 1, f, d), lambda gi, e: (layer, e, 0, 0))],
        out_specs=pl.BlockSpec((1, 1, ct, d), lambda gi, e: (gi, e, 0, 0)),
        out_shape=jax.ShapeDtypeStruct((ngrp, N_EXPERTS, ct, d), BF16),
        compiler_params=_cp("parallel", "parallel"),
        name="moe_ffn",
    )(xe, gs, mod, wg, wu, wd)


def _moe_combine_kernel(ends_ref, x1_ref, slot_ref, ye_ref, o_ref, win_scr, *, ct, nch):
    gi = pl.program_id(0)
    j = pl.program_id(1)
    mt = 128
    wide = N_EXPERTS * mt
    shi, slo = _split2(slot_ref[...])
    col_e = lax.broadcasted_iota(jnp.int32, (N_EXPERTS, wide), 1) // mt
    row_e = lax.broadcasted_iota(jnp.int32, (N_EXPERTS, wide), 0)
    expand = _bf(jnp.where(col_e == row_e, 1.0, 0.0))
    sb = _dot(shi, expand) + _dot(slo, expand)
    lane = lax.broadcasted_iota(jnp.int32, (1, mt), 1)
    starts, his, targets = [], [], []
    for e in range(N_EXPERTS):
        base = (gi * N_EXPERTS + e) * nch
        lo = jnp.where(j == 0, 0, ends_ref[base + jnp.maximum(j - 1, 0)])
        his.append(ends_ref[base + j])
        start = pl.multiple_of(jnp.minimum((lo // 16) * 16, ct - mt), 16)
        win_scr[e * mt:(e + 1) * mt, :] = ye_ref[0, e, pl.ds(start, mt), :]
        starts.append(start)
        targets.append((lane + start).astype(F32))
    onehot = _bf(jnp.where(sb == jnp.concatenate(targets, axis=1), 1.0, 0.0))
    o_ref[...] = x1_ref[...] + _dot(onehot, win_scr[...])
    for e in range(N_EXPERTS):
        for w in range(1, ct // mt):
            wlo = starts[e] + w * mt

            @pl.when(wlo < his[e])
            def _(e=e, wlo=wlo):
                ws = pl.multiple_of(jnp.minimum(wlo, ct - mt), 16)
                sbe = sb[:, e * mt:(e + 1) * mt]
                hit = jnp.where(sbe == (lane + ws).astype(F32), 1.0, 0.0) * jnp.where(sbe >= wlo.astype(F32), 1.0, 0.0)
                o_ref[...] += _dot(_bf(hit), ye_ref[0, e, pl.ds(ws, mt), :])


def _moe_combine(ends, x1, slot, ye, lg, ct, kc):
    tokens, d = x1.shape
    ngrp = tokens // lg
    nch = lg // kc
    grid_spec = pltpu.PrefetchScalarGridSpec(
        num_scalar_prefetch=1,
        grid=(ngrp, nch),
        in_specs=[pl.BlockSpec((kc, d), lambda gi, j, ends: (gi * nch + j, 0)),
                  pl.BlockSpec((kc, N_EXPERTS), lambda gi, j, ends: (gi * nch + j, 0)),
                  pl.BlockSpec((1, N_EXPERTS, ct, d), lambda gi, j, ends: (gi, 0, 0, 0),
                               pipeline_mode=pl.Buffered(1))],
        out_specs=pl.BlockSpec((kc, d), lambda gi, j, ends: (gi * nch + j, 0)),
        scratch_shapes=[pltpu.VMEM((N_EXPERTS * 128, d), BF16)])
    return pl.pallas_call(
        functools.partial(_moe_combine_kernel, ct=ct, nch=nch),
        grid_spec=grid_spec,
        out_shape=jax.ShapeDtypeStruct((tokens, d), F32),
        compiler_params=_cp("parallel", "arbitrary"),
        name="moe_combine",
    )(ends, x1, slot, ye)


def _rope_tables(seq):
    t = np.arange(seq)
    n_freq = HEAD_DIM // 4
    inv = ROPE_THETA ** (-np.arange(n_freq, dtype=np.float32) / n_freq)
    ang = np.concatenate([(t // GRID_W).astype(np.float32)[:, None] * inv,
                          (t % GRID_W).astype(np.float32)[:, None] * inv], axis=-1)
    ang = jnp.asarray(ang, dtype=F32)
    cos, sin = jnp.cos(ang), jnp.sin(ang)
    cos_t = jnp.tile(jnp.concatenate([cos, cos], axis=-1), (1, SWA_HEADS))
    sin_t = jnp.tile(jnp.concatenate([-sin, sin], axis=-1), (1, SWA_HEADS))
    return cos_t, sin_t


def _blockdiag2(w):
    z = jnp.zeros_like(w[0])
    return jnp.concatenate([jnp.concatenate([w[0], z], axis=1), jnp.concatenate([z, w[1]], axis=1)], axis=0)


def _layer_params(l, ada_w, ada_b, norm1_g, norm2_g, w_in, na_q_norm, na_k_norm, na_rpb, rw_mu, rw_w0, rw_w2,
                  rw_a0, rw_a2, rw_g2, rw_k_k, rw_k_a, rw_r_k, rw_ln_g, rw_ln_b, swa_q_norm, swa_k_norm,
                  swa_sink, w_out, w_router, w_gate, w_up, w_down):
    wr = w_router[l]
    wr_hi = wr.astype(BF16)
    w_rw = w_in[l][:, 3 * NA_W:3 * NA_W + RW_IN_W]
    return {
        'n1': norm1_g[l][None], 'n2': norm2_g[l][None], 'w_in': w_in[l].astype(BF16),
        'w_rw_lo': (w_rw - w_rw.astype(BF16).astype(F32)).astype(BF16),
        'gains': (jnp.tile(na_q_norm[l], NA_HEADS)[None], jnp.tile(na_k_norm[l], NA_HEADS)[None],
                  jnp.tile(swa_q_norm[l], SWA_HEADS)[None], jnp.tile(swa_k_norm[l], SWA_KV_HEADS)[None]),
        'bias': _na_bias_table(na_rpb[l]),
        'mu': rw_mu[l], 'w0': rw_w0[l].reshape(1, 2 * RW_W), 'w2': _blockdiag2(rw_w2[l]).astype(BF16),
        'a0': rw_a0[l].reshape(1, 2 * RW_W), 'a2': _blockdiag2(rw_a2[l]).astype(BF16),
        'g2': rw_g2[l].astype(BF16), 'k_k': rw_k_k[l][None], 'k_a': rw_k_a[l][None],
        'r_k': rw_r_k[l].reshape(1, RW_W), 'ln_g': rw_ln_g[l][None], 'ln_b': rw_ln_b[l][None],
        'sink': swa_sink[l], 'w_out': w_out[l].astype(BF16),
        'wr_hi': wr_hi, 'wr_lo': (wr - wr_hi.astype(F32)).astype(BF16),
        'wg': w_gate, 'wu': w_up, 'wd': w_down, 'layer': l,
        'ones256': _block_ones(RW_W),
    }


def _mix_and_ffn(x, mod, p, oa, oc, urw, s0_bd, scan_consts, seq, per_request_mod, group):
    r, kk, v, ld, kka, kd, g, bonus = _rw_pre(urw, p, seq)
    o2, sfin = _rw_scan(r, kk, v, ld, kka, kd, s0_bd, scan_consts, seq)
    x1, h2, aff = _finish(x, oa, oc, o2, bonus, g, mod, p, seq, per_request_mod)
    slot, slotrow, gfull, ends, kc = _topk(aff, seq, group)
    cap = EC_CAPACITY * seq // N_EXPERTS
    lg, ct = group * seq, group * cap
    xe, gs = _moe_dispatch(ends, h2, slotrow, gfull, lg, ct, kc)
    ye = _moe_ffn(xe, gs, mod, p['wg'], p['wu'], p['wd'], p['layer'], per_request_mod)
    return _moe_combine(ends, x1, slot, ye, lg, ct, kc), sfin


def _context_layer(x, mod, p, ones384, scan_consts, seq):
    qa, ka, va, urw, qc, kc, vc = _proj(x, mod, p['n1'], p['w_in'], p['gains'], ones384, None, seq, False, F32,
                                        w_rw_lo=p['w_rw_lo'])
    oa, oc = _ctx_attn(p['sink'], qa, ka, va, qc, kc, vc, seq)
    nreq = x.shape[0] // seq
    s0 = jnp.zeros((nreq, 2, HEAD_DIM, RW_W), F32)
    y, sfin = _mix_and_ffn(x, mod, p, oa, oc, urw, s0, scan_consts, seq, False, CTX_GROUP)
    return y, ka, va, kc, vc, sfin


def _latent_layer(x, mod, p, ones384, scan_consts, rope_tabs, seq, kx_na, vx_na, kx_swa, vx_swa, s0_bd, past):
    qa, ka, va, urw, qc, kc, vc = _proj(x, mod, p['n1'], p['w_in'], p['gains'], ones384, rope_tabs, seq, True,
                                        BF16)
    oa = _na_attn(qa, ka, va, kx_na, vx_na, p['bias'], seq, past)
    oc = _swa_attn(p['sink'], qc, kc, vc, kx_swa, vx_swa, seq, past)
    y, _ = _mix_and_ffn(x, mod, p, oa, oc, urw, s0_bd, scan_consts, seq, True, 1)
    return y


def _cache_layout(zs, nreq, seq, heads):
    z = jnp.stack(zs, axis=0).reshape(len(zs), nreq, seq, heads, HEAD_DIM)
    return z.transpose(1, 0, 3, 2, 4)


def _tokens_first(z):
    b, nl, h, n, dh = z.shape
    return z.transpose(1, 0, 3, 2, 4).reshape(nl, b * n, h * dh)


def kernel(x_prompt, x_sample, cache_na_k, cache_na_v, cache_swa_k, cache_swa_v, state_rwkv, c, c_ctx, ada_w, ada_b, norm1_g, norm2_g, w_in, na_q_norm, na_k_norm, na_rpb, rw_mu, rw_w0, rw_w2, rw_a0, rw_a2, rw_g2, rw_k_k, rw_k_a, rw_r_k, rw_ln_g, rw_ln_b, swa_q_norm, swa_k_norm, swa_sink, w_out, w_router, w_gate, w_up, w_down):
    nb, seq, d = x_prompt.shape
    db, dseq, _ = x_sample.shape
    depth = ada_w.shape[0]
    past = cache_na_k.shape[3]
    cond = jnp.concatenate([c, c_ctx[None], jnp.zeros((16 - db - 1, d), F32)], axis=0)
    mod_all = _adaln(cond, ada_w, ada_b).reshape(depth, 16, 6, d)
    ones384 = _block_ones(NA_W)
    scan_consts = _rw_masks()
    rope_tabs = _rope_tables(dseq)
    xp = x_prompt.reshape(nb * seq, d)
    xs = x_sample.reshape(db * dseq, d)
    new_ka, new_va, new_kc, new_vc, new_s = [], [], [], [], []
    wg_bf, wu_bf, wd_bf = w_gate.astype(BF16), w_up.astype(BF16), w_down.astype(BF16)
    kx_na, vx_na = _tokens_first(cache_na_k), _tokens_first(cache_na_v)
    kx_swa, vx_swa = _tokens_first(cache_swa_k), _tokens_first(cache_swa_v)
    s0_lat = jnp.transpose(state_rwkv, (1, 0, 2, 4, 3, 5)).reshape(depth, db, 2, HEAD_DIM, RW_W)
    for l in range(depth):
        p = _layer_params(l, ada_w, ada_b, norm1_g, norm2_g, w_in, na_q_norm, na_k_norm, na_rpb, rw_mu, rw_w0,
                          rw_w2, rw_a0, rw_a2, rw_g2, rw_k_k, rw_k_a, rw_r_k, rw_ln_g, rw_ln_b, swa_q_norm,
                          swa_k_norm, swa_sink, w_out, w_router, wg_bf, wu_bf, wd_bf)
        mod_ctx = mod_all[l, db:db + 1]
        mod_lat = mod_all[l, 0:db]
        xp, ka, va, kc, vc, sfin = _context_layer(xp, mod_ctx, p, ones384, scan_consts, seq)
        new_ka.append(ka)
        new_va.append(va)
        new_kc.append(kc)
        new_vc.append(vc)
        new_s.append(sfin)
        xs = _latent_layer(xs, mod_lat, p, ones384, scan_consts, rope_tabs, dseq,
                           kx_na[l], vx_na[l], kx_swa[l], vx_swa[l], s0_lat[l], past)
    states = jnp.stack(new_s, axis=0).reshape(depth, nb, 2, HEAD_DIM, RW_HEADS, HEAD_DIM)
    return (xp.reshape(nb, seq, d), xs.reshape(db, dseq, d),
            _cache_layout(new_ka, nb, seq, NA_HEADS), _cache_layout(new_va, nb, seq, NA_HEADS),
            _cache_layout(new_kc, nb, seq, SWA_KV_HEADS), _cache_layout(new_vc, nb, seq, SWA_KV_HEADS),
            jnp.transpose(states, (1, 0, 2, 4, 3, 5)))
```

```python
import functools

import numpy as np
import jax
import jax.numpy as jnp
from jax import lax
from jax.experimental import pallas as pl
from jax.experimental.pallas import tpu as pltpu

F32 = jnp.float32
BF16 = jnp.bfloat16

HEAD_DIM = 64
GRID_W = 64
NA_HEADS = 6
NA_KH = 8
NA_KW = 16
RW_HEADS = 4
SWA_HEADS = 6
SWA_KV_HEADS = 2
SWA_WINDOW = 128
N_EXPERTS = 16
EC_CAPACITY = 2
ROPE_THETA = 10000.0
NORM_EPS = 1e-6
GN_EPS = 64e-5
NEG_INF = -1e30
SUB_ROWS = 256
NA_UNROLL = 8
SWA_UNROLL = 2
RW_CHUNK = 64
assert RW_CHUNK == HEAD_DIM
CTX_GROUP = 16
RW_W = RW_HEADS * HEAD_DIM
NA_W = NA_HEADS * HEAD_DIM
SWA_W = SWA_HEADS * HEAD_DIM
SWA_KV_W = SWA_KV_HEADS * HEAD_DIM
RW_IN_W = 1152
VMEM_LIMIT = 56 * 1024 * 1024


def _cp(*sem):
    return pltpu.CompilerParams(dimension_semantics=sem, vmem_limit_bytes=VMEM_LIMIT)


def _bf(x):
    return x.astype(BF16)


def _dot(a, b):
    return jnp.dot(a, b, preferred_element_type=F32)


def _dot_nt(a, b):
    return lax.dot_general(a, b, (((1,), (1,)), ((), ())), preferred_element_type=F32)


def _dot_tn(a, b):
    return lax.dot_general(a, b, (((0,), (0,)), ((), ())), preferred_element_type=F32)


def _split2(x):
    hi = x.astype(BF16)
    lo = (x - hi.astype(F32)).astype(BF16)
    return hi, lo


def _split3(x):
    hi = x.astype(BF16)
    r1 = x - hi.astype(F32)
    mid = r1.astype(BF16)
    lo = (r1 - mid.astype(F32)).astype(BF16)
    return hi, mid, lo


def _dot2(a, b_bf):
    hi, lo = _split2(a)
    return _dot(hi, b_bf) + _dot(lo, b_bf)


def _sigmoid(x):
    return 1.0 / (1.0 + jnp.exp(-x))


def _block_ones(width):
    i = np.arange(width) // HEAD_DIM
    return jnp.asarray((i[:, None] == i[None, :]).astype(np.float32), dtype=BF16)


def _row_tile(seq):
    return 512 if seq % 512 == 0 else 256


def _full(shape):
    return pl.BlockSpec(shape, lambda *_: (0,) * len(shape))


def _adaln_kernel(c_ref, w_ref, b_ref, o_ref):
    c = c_ref[...]
    s = c * _sigmoid(c)
    shi, slo = _split2(s)
    whi, wlo = _split2(w_ref[0])
    o_ref[0] = _dot(shi, whi) + _dot(slo, whi) + _dot(shi, wlo) + b_ref[0]


def _adaln(cond, ada_w, ada_b):
    nl, d, n6 = ada_w.shape
    tn = 1536
    rows = cond.shape[0]
    return pl.pallas_call(
        _adaln_kernel,
        grid=(nl, n6 // tn),
        in_specs=[pl.BlockSpec((rows, d), lambda l, j: (0, 0)),
                  pl.BlockSpec((1, d, tn), lambda l, j: (l, 0, j)),
                  pl.BlockSpec((1, 1, tn), lambda l, j: (l, 0, j))],
        out_specs=pl.BlockSpec((1, rows, tn), lambda l, j: (l, 0, j)),
        out_shape=jax.ShapeDtypeStruct((nl, rows, n6), F32),
        compiler_params=_cp("parallel", "parallel"),
        name="adaln",
    )(cond, ada_w, ada_b.reshape(nl, 1, n6))


def _head_sums(zz, ones128):
    zz = _bf(zz)
    parts = [_dot(zz[:, i:i + 128], ones128) for i in range(0, zz.shape[1], 128)]
    return jnp.concatenate(parts, axis=1) if len(parts) > 1 else parts[0]


def _head_norm(z, gain, ones128):
    ms = _head_sums(z * z, ones128) * (1.0 / HEAD_DIM)
    return z * lax.rsqrt(ms + NORM_EPS) * gain


def _rope(z, cos, sin_signed):
    w = z.shape[1]
    lane = lax.broadcasted_iota(jnp.int32, z.shape, 1)
    first = (lane % HEAD_DIM) < (HEAD_DIM // 2)
    swapped = jnp.where(first, pltpu.roll(z, w - HEAD_DIM // 2, 1), pltpu.roll(z, HEAD_DIM // 2, 1))
    return z * cos + swapped * sin_signed


def _proj_kernel(*refs, rope, split_rw):
    refs = list(refs)
    x_ref, mod_ref, n1_ref, w_ref, gqa_ref, gka_ref, gqc_ref, gkc_ref, ones_ref = refs[:9]
    qa_ref, ka_ref, va_ref, urw_ref, qc_ref, kc_ref, vc_ref = refs[-7:]
    extra = refs[9:-7]
    if rope:
        cos_ref, sin_ref = extra[:2]
    sh1 = mod_ref[0, 0:1, :]
    sc1 = mod_ref[0, 1:2, :]
    o0 = 0
    o1 = NA_W
    o2 = 2 * NA_W
    o3 = 3 * NA_W
    o4 = o3 + RW_IN_W
    o5 = o4 + SWA_W
    o6 = o5 + SWA_KV_W
    ones = ones_ref[0:128, 0:128]
    ones_kv = ones
    subs = [slice(i, i + SUB_ROWS) for i in range(0, x_ref.shape[0], SUB_ROWS)]
    hs, us = [], []
    for sl in subs:
        x = x_ref[sl, :]
        ms = jnp.mean(x * x, axis=-1, keepdims=True)
        hs.append(x * lax.rsqrt(ms + NORM_EPS) * n1_ref[...] * (1.0 + sc1) + sh1)
    h_hi = [_bf(h) for h in hs]
    us = [_dot(hh, w_ref[...]) for hh in h_hi]
    for sl, h, hh, u in zip(subs, hs, h_hi, us):
        qa = _head_norm(u[:, o0:o1], gqa_ref[...], ones)
        ka = _head_norm(u[:, o1:o2], gka_ref[...], ones)
        qc = _head_norm(u[:, o4:o5], gqc_ref[...], ones)
        kc = _head_norm(u[:, o5:o6], gkc_ref[...], ones_kv)
        if rope:
            qc = _rope(qc, cos_ref[sl, :], sin_ref[sl, :])
            kc = _rope(kc, cos_ref[sl, 0:SWA_KV_W], sin_ref[sl, 0:SWA_KV_W])
        qa_ref[sl, :] = qa.astype(qa_ref.dtype)
        ka_ref[sl, :] = ka.astype(ka_ref.dtype)
        va_ref[sl, :] = u[:, o2:o3].astype(va_ref.dtype)
        urw = u[:, o3:o4]
        if split_rw:
            h_lo = _bf(h - hh.astype(F32))
            urw = urw + _dot(h_lo, w_ref[:, o3:o4]) + _dot(hh, extra[-1][...])
        urw_ref[sl, :] = urw
        qc_ref[sl, :] = qc.astype(qc_ref.dtype)
        kc_ref[sl, :] = kc.astype(kc_ref.dtype)
        vc_ref[sl, :] = u[:, o6:].astype(vc_ref.dtype)


def _proj(x, mod, n1, w_in_bf, gains, ones384, rope_tabs, seq, per_request_mod, qkv_dtype, w_rw_lo=None):
    tokens, d = x.shape
    tm = _row_tile(seq)
    tiles_per_req = seq // tm
    in_w = w_in_bf.shape[1]
    rope = rope_tabs is not None
    mod_map = (lambda i: (i // tiles_per_req, 0, 0)) if per_request_mod else (lambda i: (0, 0, 0))
    row = lambda w: pl.BlockSpec((tm, w), lambda i: (i, 0))
    in_specs = [row(d), pl.BlockSpec((1, 6, d), mod_map), _full((1, d)), _full((d, in_w)),
                _full((1, NA_W)), _full((1, NA_W)), _full((1, SWA_W)), _full((1, SWA_KV_W)),
                _full((NA_W, NA_W))]
    args = [x, mod, n1, w_in_bf, *gains, ones384]
    if rope:
        tab = pl.BlockSpec((tm, SWA_W), lambda i: (i % tiles_per_req, 0))
        in_specs += [tab, tab]
        args += list(rope_tabs)
    if w_rw_lo is not None:
        in_specs.append(_full((d, RW_IN_W)))
        args.append(w_rw_lo)
    widths = [NA_W, NA_W, NA_W, RW_IN_W, SWA_W, SWA_KV_W, SWA_KV_W]
    dtypes = [qkv_dtype, qkv_dtype, qkv_dtype, F32, qkv_dtype, qkv_dtype, qkv_dtype]
    return pl.pallas_call(
        functools.partial(_proj_kernel, rope=rope, split_rw=w_rw_lo is not None),
        grid=(tokens // tm,),
        in_specs=in_specs,
        out_specs=[row(w) for w in widths],
        out_shape=[jax.ShapeDtypeStruct((tokens, w), dt) for w, dt in zip(widths, dtypes)],
        compiler_params=_cp("parallel"),
        name="proj",
    )(*args)


def _half_masks(width=2 * HEAD_DIM):
    lane = lax.broadcasted_iota(jnp.int32, (1, width), 1)
    return lane < HEAD_DIM, lane >= HEAD_DIM


def _swap_halves(z):
    return pltpu.roll(z, HEAD_DIM, 1)


def _ctx_attn_kernel(sink_ref, qa_ref, ka_ref, va_ref, qc_ref, kc_ref, vc_ref, oa_ref, oc_ref):
    scale = HEAD_DIM ** -0.5
    m0, m1 = _half_masks()
    masks = (m0, m1)
    for pair in range(NA_HEADS // 2):
        sl = slice(pair * 128, (pair + 1) * 128)
        qp = qa_ref[:, sl].astype(F32) * scale
        kp = _bf(ka_ref[:, sl])
        vp = _bf(va_ref[:, sl])
        outs = []
        for half in range(2):
            qm = _bf(jnp.where(masks[half], qp, 0.0))
            s = _dot_nt(qm, kp)
            m = jnp.max(s, axis=-1, keepdims=True)
            e = jnp.exp(s - m)
            l = jnp.sum(e, axis=-1, keepdims=True)
            outs.append(_dot(_bf(e), vp) / l)
        oa_ref[:, sl] = jnp.where(m0, outs[0], outs[1])
    kc = _bf(kc_ref[...])
    vc = _bf(vc_ref[...])
    group = SWA_HEADS // SWA_KV_HEADS
    for pair in range(SWA_HEADS // 2):
        sl = slice(pair * 128, (pair + 1) * 128)
        qp = qc_ref[:, sl].astype(F32) * scale
        outs = []
        for half in range(2):
            h = 2 * pair + half
            g = h // group
            qh = qp if g == half else _swap_halves(qp)
            qm = _bf(jnp.where(masks[g], qh, 0.0))
            s = _dot_nt(qm, kc)
            sk = sink_ref[h]
            m = jnp.maximum(jnp.max(s, axis=-1, keepdims=True), sk)
            e = jnp.exp(s - m)
            l = jnp.sum(e, axis=-1, keepdims=True) + jnp.exp(sk - m)
            o = _dot(_bf(e), vc) / l
            outs.append(o if g == half else _swap_halves(o))
        oc_ref[:, sl] = jnp.where(m0, outs[0], outs[1])


def _ctx_attn(sink, qa, ka, va, qc, kc, vc, seq):
    tokens = qa.shape[0]
    blk = lambda w: pl.BlockSpec((seq, w), lambda b: (b, 0))
    return pl.pallas_call(
        _ctx_attn_kernel,
        grid=(tokens // seq,),
        in_specs=[pl.BlockSpec(memory_space=pltpu.SMEM), blk(NA_W), blk(NA_W), blk(NA_W), blk(SWA_W),
                  blk(SWA_KV_W), blk(SWA_KV_W)],
        out_specs=[blk(NA_W), blk(SWA_W)],
        out_shape=[jax.ShapeDtypeStruct((tokens, NA_W), F32), jax.ShapeDtypeStruct((tokens, SWA_W), F32)],
        compiler_params=_cp("parallel"),
        name="ctx_attn",
    )(sink, qa, ka, va, qc, kc, vc)


def _na_bias_kernel(rpb_ref, o_ref):
    h = pl.program_id(0)
    nrow = 2 * NA_KH - 1
    ncol = 2 * NA_KW - 1
    width = NA_KH * GRID_W
    shape = (GRID_W, width)
    lane = lax.broadcasted_iota(jnp.int32, shape, 1)
    qc = lax.broadcasted_iota(jnp.int32, shape, 0)
    kc = lane % GRID_W
    c_start = jnp.clip(qc - NA_KW // 2, 0, GRID_W - NA_KW)
    ok = (kc >= c_start) & (kc < c_start + NA_KW)
    d_col = jnp.clip(kc - qc, 1 - NA_KW, NA_KW - 1) + NA_KW - 1
    key_row = lax.broadcasted_iota(jnp.int32, (1, width), 1) // GRID_W

    def case_body(case, carry):
        acc = jnp.zeros(shape, F32)
        for dc in range(ncol):
            val = jnp.zeros((1, width), F32)
            for i in range(NA_KH):
                val = jnp.where(key_row == i, rpb_ref[(h * nrow + case + i) * ncol + dc], val)
            acc = jnp.where(d_col == dc, val, acc)
        o_ref[0, pl.ds(case, 1)] = jnp.where(ok, acc, NEG_INF)[None]
        return carry

    lax.fori_loop(0, NA_KH, case_body, 0)


def _na_bias_table(rpb):
    nh = rpb.shape[0]
    return pl.pallas_call(
        _na_bias_kernel,
        grid=(nh,),
        in_specs=[pl.BlockSpec(memory_space=pltpu.SMEM)],
        out_specs=pl.BlockSpec((1, NA_KH, GRID_W, NA_KH * GRID_W), lambda h: (h // 2, 0, h % 2, 0)),
        out_shape=jax.ShapeDtypeStruct((nh // 2, NA_KH, 2 * GRID_W, NA_KH * GRID_W), F32),
        compiler_params=_cp("parallel"),
        name="na_bias",
    )(rpb.reshape(-1))


def _na_kernel(q_ref, k_ref, v_ref, kx_ref, vx_ref, bias_ref, o_ref, *, rows):
    scale = HEAD_DIM ** -0.5
    m0, m1 = _half_masks()
    kx = _bf(kx_ref[...])
    vx = _bf(vx_ref[...])

    def body(it, carry):
        us = range(NA_UNROLL)
        r = [it * NA_UNROLL + u for u in us]
        rs = [jnp.clip(r[u] - NA_KH // 2, 0, rows - NA_KH) for u in us]
        case = [rs[u] - r[u] + NA_KH - 1 for u in us]
        q0 = [pl.multiple_of(r[u] * GRID_W, GRID_W) for u in us]
        k0 = [pl.multiple_of(rs[u] * GRID_W, GRID_W) for u in us]
        qp = [q_ref[pl.ds(q0[u], GRID_W), :].astype(F32) * scale for u in us]
        kw = [_bf(k_ref[pl.ds(k0[u], NA_KH * GRID_W), :]) for u in us]
        vw = [_bf(v_ref[pl.ds(k0[u], NA_KH * GRID_W), :]) for u in us]
        q2 = [_bf(jnp.concatenate([jnp.where(m0, qp[u], 0.0), jnp.where(m1, qp[u], 0.0)], axis=0)) for u in us]
        s = [jnp.concatenate([_dot_nt(q2[u], kw[u]) + bias_ref[0, pl.ds(case[u], 1)][0], _dot_nt(q2[u], kx)],
                             axis=1) for u in us]
        m = [jnp.max(s[u], axis=-1, keepdims=True) for u in us]
        e = [jnp.exp(s[u] - m[u]) for u in us]
        l = [jnp.sum(e[u], axis=-1, keepdims=True) for u in us]
        o = [_dot(_bf(e[u]), jnp.concatenate([vw[u], vx], axis=0)) / l[u] for u in us]
        for u in us:
            o_ref[pl.ds(q0[u], GRID_W), :] = jnp.where(m0, o[u][0:GRID_W], o[u][GRID_W:])
        return carry

    lax.fori_loop(0, rows // NA_UNROLL, body, 0)


def _na_attn(q, k, v, kx, vx, bias, seq, past):
    tokens = q.shape[0]
    nb = tokens // seq
    rows = seq // GRID_W
    blk = pl.BlockSpec((seq, 128), lambda b, p: (b, p))
    cblk = pl.BlockSpec((past, 128), lambda b, p: (b, p))
    return pl.pallas_call(
        functools.partial(_na_kernel, rows=rows),
        grid=(nb, NA_HEADS // 2),
        in_specs=[blk, blk, blk, cblk, cblk,
                  pl.BlockSpec((1, NA_KH, 2 * GRID_W, NA_KH * GRID_W), lambda b, p: (p, 0, 0, 0))],
        out_specs=blk,
        out_shape=jax.ShapeDtypeStruct((tokens, NA_W), F32),
        compiler_params=_cp("parallel", "parallel"),
        name="na_attn",
    )(q, k, v, kx, vx, bias)


def _swa_kernel(sink_ref, q_ref, k_ref, v_ref, kx_ref, vx_ref, o_ref, *, seq):
    scale = HEAD_DIM ** -0.5
    blk = SWA_WINDOW
    m0, m1 = _half_masks()
    masks = (m0, m1)
    kx = _bf(kx_ref[...])
    vx = _bf(vx_ref[...])
    group = SWA_HEADS // SWA_KV_HEADS

    sk = []
    for g in range(SWA_KV_HEADS):
        sk.append(jnp.concatenate([jnp.full((blk, 1), sink_ref[h], F32) for h in range(g * group, (g + 1) * group)],
                                  axis=0))

    def body(it, carry):
        us = range(SWA_UNROLL)
        cs = [(u, g) for u in us for g in range(SWA_KV_HEADS)]
        nb = [it * SWA_UNROLL + u for u in us]
        ks = [pl.multiple_of(jnp.clip((nb[u] - 1) * blk, 0, seq - 3 * blk), blk) for u in us]
        q0 = [pl.multiple_of(nb[u] * blk, blk) for u in us]
        kw = [_bf(k_ref[pl.ds(ks[u], 3 * blk), :]) for u in us]
        vw = [_bf(v_ref[pl.ds(ks[u], 3 * blk), :]) for u in us]
        ok = []
        for u in us:
            qpos = q0[u] + lax.broadcasted_iota(jnp.int32, (group * blk, 1), 0) % blk
            kpos = ks[u] + lax.broadcasted_iota(jnp.int32, (1, 3 * blk), 1)
            ok.append(jnp.abs(qpos - kpos) <= SWA_WINDOW)
        qg = {}
        for u in us:
            pairs = [q_ref[pl.ds(q0[u], blk), p * 128:(p + 1) * 128].astype(F32) * scale
                     for p in range(SWA_HEADS // 2)]
            for g in range(SWA_KV_HEADS):
                qs = []
                for h in range(g * group, (g + 1) * group):
                    qh = pairs[h // 2] if h % 2 == g else _swap_halves(pairs[h // 2])
                    qs.append(jnp.where(masks[g], qh, 0.0))
                qg[u, g] = _bf(jnp.concatenate(qs, axis=0))
        sw = {c: jnp.where(ok[c[0]], _dot_nt(qg[c], kw[c[0]]), NEG_INF) for c in cs}
        sx = {c: _dot_nt(qg[c], kx) for c in cs}
        m = {c: jnp.maximum(jnp.maximum(jnp.max(sw[c], axis=-1, keepdims=True),
                                        jnp.max(sx[c], axis=-1, keepdims=True)), sk[c[1]]) for c in cs}
        ew = {c: jnp.exp(sw[c] - m[c]) for c in cs}
        ex = {c: jnp.exp(sx[c] - m[c]) for c in cs}
        l = {c: jnp.sum(ew[c], axis=-1, keepdims=True) + jnp.sum(ex[c], axis=-1, keepdims=True)
             + jnp.exp(sk[c[1]] - m[c]) for c in cs}
        o = {c: (_dot(_bf(ew[c]), vw[c[0]]) + _dot(_bf(ex[c]), vx)) / l[c] for c in cs}
        for u in us:
            head_out = []
            for g in range(SWA_KV_HEADS):
                for i in range(group):
                    h = g * group + i
                    oh = o[u, g][i * blk:(i + 1) * blk]
                    head_out.append(oh if h % 2 == g else _swap_halves(oh))
            for p in range(SWA_HEADS // 2):
                o_ref[pl.ds(q0[u], blk), p * 128:(p + 1) * 128] = jnp.where(m0, head_out[2 * p], head_out[2 * p + 1])
        return carry

    lax.fori_loop(0, seq // (blk * SWA_UNROLL), body, 0)


def _swa_attn(sink, q, k, v, kx, vx, seq, past):
    tokens = q.shape[0]
    blk = lambda w: pl.BlockSpec((seq, w), lambda b: (b, 0))
    cblk = pl.BlockSpec((past, SWA_KV_W), lambda b: (b, 0))
    return pl.pallas_call(
        functools.partial(_swa_kernel, seq=seq),
        grid=(tokens // seq,),
        in_specs=[pl.BlockSpec(memory_space=pltpu.SMEM), blk(SWA_W), blk(SWA_KV_W), blk(SWA_KV_W), cblk, cblk],
        out_specs=blk(SWA_W),
        out_shape=jax.ShapeDtypeStruct((tokens, SWA_W), F32),
        compiler_params=_cp("parallel"),
        name="swa_attn",
    )(sink, q, k, v, kx, vx)


def _rw_pre_kernel(u_ref, up_ref, un_ref, mu_ref, w0_ref, w2_ref, a0_ref, a2_ref, g2_ref, kk_ref_, ka_ref_,
                   rk_ref, ones_ref, r_o, kk_o, v_o, ld_o, kka_o, kd_o, g_o, bonus_o, *, tiles_per_req):
    i = pl.program_id(0)
    u = u_ref[...]
    tm = u.shape[0]
    rowi = lax.broadcasted_iota(jnp.int32, (tm, 1), 0)
    first = (i % tiles_per_req) == 0
    last = (i % tiles_per_req) == tiles_per_req - 1
    prev_row = jnp.where(first, 0.0, up_ref[7:8, :])
    next_row = jnp.where(last, 0.0, un_ref[0:1, :])
    prev = jnp.where(rowi == 0, prev_row, pltpu.roll(u, 1, 0))
    nxt = jnp.where(rowi == tm - 1, next_row, pltpu.roll(u, tm - 1, 0))
    us = u + mu_ref[0:1, :] * (prev - u) + mu_ref[1:2, :] * (nxt - u)
    r = us[:, 0:RW_W]
    k = us[:, RW_W:2 * RW_W]
    v = us[:, 2 * RW_W:3 * RW_W]
    wl = us[:, 3 * RW_W:3 * RW_W + 128]
    al = us[:, 3 * RW_W + 128:3 * RW_W + 256]
    gl = us[:, 3 * RW_W + 256:3 * RW_W + 384]
    z = -(w0_ref[...] + _dot(_bf(jnp.tanh(wl)), w2_ref[...]))
    softplus = jnp.maximum(z, 0.0) + jnp.log(1.0 + jnp.exp(-jnp.abs(z)))
    w = -softplus - 0.5
    ld = -jnp.exp(w)
    a = _sigmoid(a0_ref[...] + _dot(_bf(al), a2_ref[...]))
    g = _dot(_bf(_sigmoid(gl)), g2_ref[...])
    ones = ones_ref[...]
    kkr = k * kk_ref_[...]
    kk = kkr * lax.rsqrt(jnp.maximum(_dot2(kkr * kkr, ones), 1e-24))
    k_a = ka_ref_[...]
    kd_f = k * (1.0 + (a[:, 0:RW_W] - 1.0) * k_a)
    kd_b = k * (1.0 + (a[:, RW_W:] - 1.0) * k_a)
    r_o[...] = r
    kk_o[...] = kk
    v_o[...] = v
    ld_o[...] = ld
    kka_o[:, 0:RW_W] = kk * a[:, 0:RW_W]
    kka_o[:, RW_W:] = kk * a[:, RW_W:]
    kd_o[:, 0:RW_W] = kd_f
    kd_o[:, RW_W:] = kd_b
    g_o[...] = g
    bonus_o[...] = _dot2(r * (kd_f + kd_b) * rk_ref[...], ones) * v


def _rw_pre(urw, p, seq):
    tokens = urw.shape[0]
    tm = _row_tile(seq)
    tpr = seq // tm
    nt = tokens // tm
    r8 = tm // 8
    row = lambda w: pl.BlockSpec((tm, w), lambda i: (i, 0))
    in_specs = [row(RW_IN_W),
                pl.BlockSpec((8, RW_IN_W), lambda i: (jnp.maximum(i * r8 - 1, 0), 0)),
                pl.BlockSpec((8, RW_IN_W), lambda i: (jnp.minimum((i + 1) * r8, nt * r8 - 1), 0)),
                _full((2, RW_IN_W)), _full((1, 2 * RW_W)), _full((128, 2 * RW_W)), _full((1, 2 * RW_W)),
                _full((128, 2 * RW_W)), _full((128, RW_W)), _full((1, RW_W)), _full((1, RW_W)),
                _full((1, RW_W)), _full((RW_W, RW_W))]
    widths = [RW_W, RW_W, RW_W, 2 * RW_W, 2 * RW_W, 2 * RW_W, RW_W, RW_W]
    return pl.pallas_call(
        functools.partial(_rw_pre_kernel, tiles_per_req=tpr),
        grid=(nt,),
        in_specs=in_specs,
        out_specs=[row(w) for w in widths],
        out_shape=[jax.ShapeDtypeStruct((tokens, w), F32) for w in widths],
        compiler_params=_cp("parallel"),
        name="rw_pre",
    )(urw, urw, urw, p['mu'], p['w0'], p['w2'], p['a0'], p['a2'], p['g2'], p['k_k'], p['k_a'], p['r_k'],
      p['ones256'])


def _rw_masks():
    t = RW_CHUNK
    n = RW_HEADS * t
    tt = np.arange(t)[:, None]
    ss = (np.arange(n) % t)[None, :]
    before = np.stack([ss < tt, ss > tt])
    diag = (ss == tt)
    strict = before.astype(np.float32)
    incl = (before | diag[None]).astype(np.float32)
    eye = diag.astype(np.float32)
    ti = np.arange(t)
    tri = np.stack([ti[None, :] <= ti[:, None], ti[None, :] >= ti[:, None]]).astype(np.float32)
    hd = np.arange(n) // t
    same = (hd[:, None] == hd[None, :]).astype(np.float32)
    return (jnp.asarray(strict), jnp.asarray(incl), jnp.asarray(tri, dtype=BF16), jnp.asarray(same),
            jnp.asarray(eye))


def _rw_scan_kernel(r_ref, kk_ref, v_ref, ld_ref, kka_ref, kd_ref, s0_ref, strict_ref, incl_ref, tri_ref,
                    same_ref, eye_ref, o_ref, sfin_ref, s_scr, *, nsub):
    d = pl.program_id(1)
    c = pl.program_id(2)
    t = RW_CHUNK
    n = RW_HEADS * t

    @pl.when(c == 0)
    def _():
        s_scr[...] = s0_ref[0, 0]

    strict = strict_ref[0]
    incl = incl_ref[0]
    tri = tri_ref[0]
    eye = eye_ref[...]
    same = same_ref[...]
    same_bf = _bf(same)

    def bd(x):
        return jnp.concatenate([_bf(x)] * RW_HEADS, axis=0) * same_bf

    js = range(nsub)
    rows = [pl.ds(pl.multiple_of((j + d * (nsub - 1 - 2 * j)) * t, t), t) for j in js]
    ld = [ld_ref[rows[j], :] for j in js]
    cum = []
    for j in js:
        lhi, lmid, llo = _split3(ld[j])
        cum.append(_dot(tri, lhi) + _dot(tri, lmid) + _dot(tri, llo))
    cend = [jnp.sum(ld[j], axis=0, keepdims=True) for j in js]
    kka = [kka_ref[rows[j], :] for j in js]
    kd = [kd_ref[rows[j], :] for j in js]
    v = [v_ref[rows[j], :] for j in js]
    at = [-kk_ref[rows[j], :] * jnp.exp(cum[j] - ld[j]) for j in js]
    rt = [r_ref[rows[j], :] * jnp.exp(cum[j]) for j in js]
    e_inv = [jnp.exp(-cum[j]) for j in js]
    aa = [_dot_nt(_bf(jnp.concatenate([at[j], rt[j]], axis=0)),
                  jnp.concatenate([bd(kka[j] * e_inv[j]), bd(kd[j] * e_inv[j])], axis=0)) for j in js]
    a_ab = [aa[j][0:t, 0:n] * strict for j in js]
    x = [eye + a_ab[j] for j in js]
    pw = a_ab
    for _ in range(RW_CHUNK.bit_length() - 3):
        pw = [_dot(_bf(pw[j]), bd(pw[j])) for j in js]
        x = [x[j] + _dot(_bf(x[j]), bd(pw[j])) for j in js]
    xs = [_split2(x[j]) for j in js]
    sa = [_split2(a_ab[j]) for j in js]
    ax = [_dot(sa[j][0], bd(xs[j][0])) + _dot(sa[j][1], bd(xs[j][0])) + _dot(sa[j][0], bd(xs[j][1])) for j in js]
    x = [x[j] + _dot(xs[j][0], bd(eye - x[j] + ax[j])) for j in js]
    v_bd = [bd(v[j]) for j in js]
    wv = [_dot(_bf(aa[j][0:t, n:] * strict), v_bd[j]) for j in js]
    mu = [_dot(_bf(x[j]), jnp.concatenate([bd(at[j]), bd(wv[j])], axis=1)) for j in js]
    m1 = [mu[j][:, 0:n] for j in js]
    u0 = [mu[j][:, n:] for j in js]
    e_end = [jnp.exp(cend[j] - cum[j]) for j in js]
    bend = [_bf(kka[j] * e_end[j]) for j in js]
    g = [_bf(_dot_tn(_bf(m1[j]), bend[j]) * same) for j in js]
    cst = []
    for j in js:
        full = _dot_tn(_bf(jnp.concatenate([u0[j], v[j]], axis=0)),
                       jnp.concatenate([bend[j], _bf(kd[j] * e_end[j])], axis=0)) * same
        cst.append(functools.reduce(jnp.add, [full[h * t:(h + 1) * t] for h in range(RW_HEADS)]))
    qo = [_dot(_bf(aa[j][t:, 0:n] * incl), jnp.concatenate([bd(m1[j]), bd(u0[j])], axis=1)) for j in js]
    q = [_bf(rt[j] + qo[j][:, 0:n]) for j in js]
    o0 = [qo[j][:, n:] + _dot(_bf(aa[j][t:, n:] * incl), v_bd[j]) for j in js]

    s = s_scr[...]
    for j in js:
        o_ref[0, rows[j], :] = _dot_nt(q[j], bd(s)) + o0[j]
        s = s * jnp.exp(cend[j]) + _dot(_bf(s), g[j]) + cst[j]
    s_scr[...] = s

    @pl.when(c == pl.num_programs(2) - 1)
    def _():
        sfin_ref[0, 0] = s


def _rw_scan(r, kk, v, ld, kka, kd, s0_bd, consts, seq):
    tokens = r.shape[0]
    nreq = tokens // seq
    tb = min(seq, 1024)
    nblk = seq // tb
    nsub = tb // RW_CHUNK
    n = RW_HEADS * RW_CHUNK
    cc = lambda d, c: c + d * (nblk - 1 - 2 * c)
    shared = pl.BlockSpec((tb, RW_W), lambda b, d, c: (b * nblk + cc(d, c), 0))
    dirw = pl.BlockSpec((tb, RW_W), lambda b, d, c: (b * nblk + cc(d, c), d))
    strict, incl, tri, same, eye = consts
    return pl.pallas_call(
        functools.partial(_rw_scan_kernel, nsub=nsub),
        grid=(nreq, 2, nblk),
        in_specs=[shared, shared, shared, dirw, dirw, dirw,
                  pl.BlockSpec((1, 1, HEAD_DIM, n), lambda b, d, c: (b, d, 0, 0)),
                  pl.BlockSpec((1, RW_CHUNK, n), lambda b, d, c: (d, 0, 0)),
                  pl.BlockSpec((1, RW_CHUNK, n), lambda b, d, c: (d, 0, 0)),
                  pl.BlockSpec((1, RW_CHUNK, RW_CHUNK), lambda b, d, c: (d, 0, 0)),
                  _full((n, n)), _full((RW_CHUNK, n))],
        out_specs=[pl.BlockSpec((1, tb, RW_W), lambda b, d, c: (d, b * nblk + cc(d, c), 0)),
                   pl.BlockSpec((1, 1, HEAD_DIM, n), lambda b, d, c: (b, d, 0, 0))],
        out_shape=[jax.ShapeDtypeStruct((2, tokens, RW_W), F32),
                   jax.ShapeDtypeStruct((nreq, 2, HEAD_DIM, n), F32)],
        scratch_shapes=[pltpu.VMEM((HEAD_DIM, n), F32)],
        compiler_params=_cp("parallel", "parallel", "arbitrary"),
        name="rw_scan",
    )(r, kk, v, ld, kka, kd, s0_bd, strict, incl, tri, same, eye)


def _finish_kernel(x_ref, oa_ref, oc_ref, o2_ref, bonus_ref, g_ref, mod_ref, wout_ref, lng_ref, lnb_ref,
                   n2_ref, wr_hi_ref, wr_lo_ref, ones_ref, x1_ref, h2_ref, aff_ref):
    ones = ones_ref[0:128, 0:128]
    g1 = mod_ref[0, 2:3, :]
    sh2 = mod_ref[0, 3:4, :]
    sc2 = mod_ref[0, 4:5, :]
    subs = [slice(i, i + SUB_ROWS) for i in range(0, x_ref.shape[0], SUB_ROWS)]
    mixins = []
    for sl in subs:
        y = o2_ref[0, sl, :] + o2_ref[1, sl, :]
        mu = _head_sums(y, ones) * (1.0 / HEAD_DIM)
        yc = y - mu
        var = _head_sums(yc * yc, ones) * (1.0 / HEAD_DIM)
        yn = yc * lax.rsqrt(var + GN_EPS) * lng_ref[...] + lnb_ref[...]
        ob = (yn + bonus_ref[sl, :]) * g_ref[sl, :]
        mixins.append(jnp.concatenate([_bf(oa_ref[sl, :]), _bf(ob), _bf(oc_ref[sl, :])], axis=1))
    mixes = [_dot(mixin, wout_ref[...]) for mixin in mixins]
    for sl, mix in zip(subs, mixes):
        x1 = x_ref[sl, :] + g1 * mix
        ms = jnp.mean(x1 * x1, axis=-1, keepdims=True)
        h2 = x1 * lax.rsqrt(ms + NORM_EPS) * n2_ref[...] * (1.0 + sc2) + sh2
        x1_ref[sl, :] = x1
        h2_ref[sl, :] = _bf(h2)
        hhi, hlo = _split2(h2)
        logits = _dot(hhi, wr_hi_ref[...]) + _dot(hlo, wr_hi_ref[...]) + _dot(hhi, wr_lo_ref[...])
        m = jnp.max(logits, axis=-1, keepdims=True)
        e = jnp.exp(logits - m)
        aff_ref[sl, :] = e / jnp.sum(e, axis=-1, keepdims=True)


def _finish(x, oa, oc, o2, bonus, g, mod, p, seq, per_request_mod):
    tokens, d = x.shape
    tm = _row_tile(seq)
    tpr = seq // tm
    mod_map = (lambda i: (i // tpr, 0, 0)) if per_request_mod else (lambda i: (0, 0, 0))
    row = lambda w: pl.BlockSpec((tm, w), lambda i: (i, 0))
    return pl.pallas_call(
        _finish_kernel,
        grid=(tokens // tm,),
        in_specs=[row(d), row(NA_W), row(SWA_W), pl.BlockSpec((2, tm, RW_W), lambda i: (0, i, 0)), row(RW_W),
                  row(RW_W), pl.BlockSpec((1, 6, d), mod_map), _full((d, d)), _full((1, RW_W)),
                  _full((1, RW_W)), _full((1, d)), _full((d, N_EXPERTS)), _full((d, N_EXPERTS)),
                  _full((RW_W, RW_W))],
        out_specs=[row(d), row(d), row(N_EXPERTS)],
        out_shape=[jax.ShapeDtypeStruct((tokens, d), F32), jax.ShapeDtypeStruct((tokens, d), BF16),
                   jax.ShapeDtypeStruct((tokens, N_EXPERTS), F32)],
        compiler_params=_cp("parallel"),
        name="finish",
    )(x, oa, oc, o2, bonus, g, mod, p['w_out'], p['ln_g'], p['ln_b'], p['n2'], p['wr_hi'], p['wr_lo'],
      p['ones256'])


def _topk_kernel(aff_ref, tri_ref, eye_ref, place_ref, slot_ref, slotrow_ref, gfull_ref, ends_ref, *, cap, group,
                 seq, tb):
    aff = aff_ref[...]
    bits = lax.bitcast_convert_type(aff, jnp.int32)
    capf = jnp.float32(cap)
    eye = eye_ref[...]
    ghi, gmid, glo = _split3(aff)
    aff_t = _dot_nt(eye, ghi) + _dot_nt(eye, gmid) + _dot_nt(eye, glo)
    bits_t = lax.bitcast_convert_type(aff_t, jnp.int32)
    rs = range(group)

    def bis(_, carry):
        los, his = carry
        nlo, nhi = [], []
        for r in rs:
            mid = los[r] + ((his[r] - los[r] + 1) >> 1)
            cnt = jnp.sum(jnp.where(bits_t[:, r * seq:(r + 1) * seq] >= mid, 1.0, 0.0), axis=1, keepdims=True)
            ge = cnt >= capf
            nlo.append(jnp.where(ge, mid, los[r]))
            nhi.append(jnp.where(ge, his[r], mid - 1))
        return tuple(nlo), tuple(nhi)

    lo0 = tuple(jnp.zeros((N_EXPERTS, 1), jnp.int32) for _ in rs)
    hi0 = tuple(jnp.full((N_EXPERTS, 1), 0x7F7FFFFF, jnp.int32) for _ in rs)
    thr_cols, _ = lax.fori_loop(0, 31, bis, (lo0, hi0))
    ri = lax.broadcasted_iota(jnp.int32, (N_EXPERTS, N_EXPERTS), 0)
    ci = lax.broadcasted_iota(jnp.int32, (N_EXPERTS, N_EXPERTS), 1)
    tri = tri_ref[...]
    for r in rs:
        thr = jnp.sum(jnp.where(ri == ci, thr_cols[r], 0), axis=0, keepdims=True)
        rows = slice(r * seq, (r + 1) * seq)
        gt = jnp.where(bits[rows] > thr, 1.0, 0.0)
        eq = jnp.where(bits[rows] == thr, 1.0, 0.0)
        need = capf - jnp.sum(gt, axis=0, keepdims=True)
        offset = float(r * cap)
        carry_g = jnp.zeros((1, N_EXPERTS), F32)
        carry_e = jnp.zeros((1, N_EXPERTS), F32)
        for blk in range(seq // tb):
            sl = slice(blk * tb, (blk + 1) * tb)
            out = slice(r * seq + blk * tb, r * seq + (blk + 1) * tb)
            pg = _dot(tri, _bf(gt[sl])) + carry_g
            pe = _dot(tri, _bf(eq[sl])) + carry_e
            carry_g = pg[tb - 1:tb, :]
            carry_e = pe[tb - 1:tb, :]
            sel = gt[sl] + eq[sl] * jnp.where(pe <= need, 1.0, 0.0)
            slot = jnp.where(sel > 0.5, pg + jnp.minimum(pe, need) - 1.0 + offset, -1.0)
            slot_ref[out, :] = slot
            ends_ref[r, blk:blk + 1, :] = carry_g + jnp.minimum(carry_e, need) + offset
            shi, slo = _split2(slot)
            slotrow_ref[0, :, 0, out] = _dot_nt(eye, shi) + _dot_nt(eye, slo)
            gfull_ref[out, :] = _bf(_dot(ghi[out], place_ref[0]) + _dot(gmid[out], place_ref[1])
                                    + _dot(glo[out], place_ref[2]))


def _topk(aff, seq, group):
    tokens = aff.shape[0]
    nreq = tokens // seq
    cap = EC_CAPACITY * seq // N_EXPERTS
    tb = min(seq, 512)
    ti = np.arange(tb)
    tri = jnp.asarray((ti[None, :] <= ti[:, None]).astype(np.float32), dtype=BF16)
    eye = jnp.asarray(np.eye(N_EXPERTS, dtype=np.float32), dtype=BF16)
    place = np.zeros((3, N_EXPERTS, 128), np.float32)
    for s in range(3):
        place[s, np.arange(N_EXPERTS), s * N_EXPERTS + np.arange(N_EXPERTS)] = 1.0
    place = jnp.asarray(place, dtype=BF16)
    nblk = seq // tb
    slot, slotrow, gfull, ends = pl.pallas_call(
        functools.partial(_topk_kernel, cap=cap, group=group, seq=seq, tb=tb),
        grid=(nreq // group,),
        in_specs=[pl.BlockSpec((group * seq, N_EXPERTS), lambda b: (b, 0)), _full((tb, tb)),
                  _full((N_EXPERTS, N_EXPERTS)), _full((3, N_EXPERTS, 128))],
        out_specs=[pl.BlockSpec((group * seq, N_EXPERTS), lambda b: (b, 0)),
                   pl.BlockSpec((1, N_EXPERTS, 1, group * seq), lambda b: (b, 0, 0, 0)),
                   pl.BlockSpec((group * seq, 128), lambda b: (b, 0)),
                   pl.BlockSpec((group, nblk, N_EXPERTS), lambda b: (b, 0, 0))],
        out_shape=[jax.ShapeDtypeStruct((tokens, N_EXPERTS), F32),
                   jax.ShapeDtypeStruct((nreq // group, N_EXPERTS, 1, group * seq), F32),
                   jax.ShapeDtypeStruct((tokens, 128), BF16),
                   jax.ShapeDtypeStruct((nreq, nblk, N_EXPERTS), F32)],
        compiler_params=_cp("parallel"),
        name="topk",
    )(aff, tri, eye, place)
    ends = ends.reshape(nreq // group, group * nblk, N_EXPERTS).transpose(0, 2, 1)
    return slot, slotrow, gfull, ends.astype(jnp.int32).reshape(-1), tb


MOE_EB = 16


def _moe_dispatch_kernel(ends_ref, h_ref, slotrow_ref, gfull_ref, xe_ref, gs_ref, *, ct, nch):
    gi = pl.program_id(0)
    eb = pl.program_id(1)
    c = pl.program_id(2)
    mt = 128

    @pl.when(c == 0)
    def _():
        xe_ref[...] = jnp.zeros_like(xe_ref)
        gs_ref[...] = jnp.zeros_like(gs_ref)

    jcol = lax.broadcasted_iota(jnp.int32, (mt, 1), 0)
    starts, his, pieces = [], [], []
    for i in range(MOE_EB):
        base = (gi * N_EXPERTS + eb * MOE_EB + i) * nch
        lo = jnp.where(c == 0, 0, ends_ref[base + jnp.maximum(c - 1, 0)])
        his.append(ends_ref[base + c])
        start = pl.multiple_of(jnp.minimum((lo // 16) * 16, ct - mt), 16)
        starts.append(start)
        pieces.append(_bf(jnp.where(slotrow_ref[0, i] == (jcol + start).astype(F32), 1.0, 0.0)))
    onehot = jnp.concatenate(pieces, axis=0)
    xw = _bf(_dot(onehot, h_ref[...]))
    gw = _bf(_dot(onehot, gfull_ref[...]))
    for i in range(MOE_EB):
        rows = pl.ds(starts[i], mt)
        xe_ref[0, i, rows, :] += xw[i * mt:(i + 1) * mt]
        gs_ref[0, i, rows, :] += gw[i * mt:(i + 1) * mt]
    for i in range(MOE_EB):
        for w in range(1, ct // mt):
            wlo = starts[i] + w * mt

            @pl.when(wlo < his[i])
            def _(i=i, wlo=wlo):
                ws = pl.multiple_of(jnp.minimum(wlo, ct - mt), 16)
                slot = slotrow_ref[0, i]
                hit = (jnp.where(slot == (jcol + ws).astype(F32), 1.0, 0.0)
                       * jnp.where(slot >= wlo.astype(F32), 1.0, 0.0))
                rows = pl.ds(ws, mt)
                xe_ref[0, i, rows, :] += _bf(_dot(_bf(hit), h_ref[...]))
                gs_ref[0, i, rows, :] += _bf(_dot(_bf(hit), gfull_ref[...]))


def _moe_dispatch(ends, h2, slotrow, gfull, lg, ct, kc):
    tokens, d = h2.shape
    ngrp = tokens // lg
    nch = lg // kc
    grid_spec = pltpu.PrefetchScalarGridSpec(
        num_scalar_prefetch=1,
        grid=(ngrp, N_EXPERTS // MOE_EB, nch),
        in_specs=[pl.BlockSpec((kc, d), lambda gi, eb, c, ends: (gi * nch + c, 0)),
                  pl.BlockSpec((1, MOE_EB, 1, kc), lambda gi, eb, c, ends: (gi, eb, 0, c)),
                  pl.BlockSpec((kc, 128), lambda gi, eb, c, ends: (gi * nch + c, 0))],
        out_specs=[pl.BlockSpec((1, MOE_EB, ct, d), lambda gi, eb, c, ends: (gi, eb, 0, 0)),
                   pl.BlockSpec((1, MOE_EB, ct, 128), lambda gi, eb, c, ends: (gi, eb, 0, 0))])
    return pl.pallas_call(
        functools.partial(_moe_dispatch_kernel, ct=ct, nch=nch),
        grid_spec=grid_spec,
        out_shape=[jax.ShapeDtypeStruct((ngrp, N_EXPERTS, ct, d), BF16),
                   jax.ShapeDtypeStruct((ngrp, N_EXPERTS, ct, 128), BF16)],
        compiler_params=_cp("parallel", "parallel", "arbitrary"),
        name="moe_dispatch",
    )(ends, h2, slotrow, gfull)


def _moe_ffn_kernel(xe_ref, gs_ref, mod_ref, wg_ref, wu_ref, wd_ref, ye_ref):
    e = pl.program_id(1)
    lane = lax.broadcasted_iota(jnp.int32, (1, 128), 1)
    pick = (lane == e) | (lane == e + N_EXPERTS) | (lane == e + 2 * N_EXPERTS)
    gate = jnp.sum(jnp.where(pick, gs_ref[0, 0].astype(F32), 0.0), axis=-1, keepdims=True)
    xb = xe_ref[0, 0]
    hg = _dot(xb, wg_ref[0, 0])
    hu = _dot(xb, wu_ref[0, 0])
    he = _bf(hg * _sigmoid(hg) * hu)
    y = _dot(he, wd_ref[0, 0])
    ye_ref[0, 0] = _bf(y * gate * mod_ref[0, 5:6, :])


def _moe_ffn(xe, gs, mod, wg, wu, wd, layer, per_group_mod):
    ngrp, _, ct, d = xe.shape
    f = wg.shape[3]
    mod_map = (lambda gi, e: (gi, 0, 0)) if per_group_mod else (lambda gi, e: (0, 0, 0))
    return pl.pallas_call(
        _moe_ffn_kernel,
        grid=(ngrp, N_EXPERTS),
        in_specs=[pl.BlockSpec((1, 1, ct, d), lambda gi, e: (gi, e, 0, 0)),
                  pl.BlockSpec((1, 1, ct, 128), lambda gi, e: (gi, e, 0, 0)),
                  pl.BlockSpec((1, 6, d), mod_map),
                  pl.BlockSpec((1, 1, d, f), lambda gi, e: (layer, e, 0, 0)),
                  pl.BlockSpec((1, 1, d, f), lambda gi, e: (layer, e, 0, 0)),
                  pl.BlockSpec((1, 1, f, d), lambda gi, e: (layer, e, 0, 0))],
        out_specs=pl.BlockSpec((1, 1, ct, d), lambda gi, e: (gi, e, 0, 0)),
        out_shape=jax.ShapeDtypeStruct((ngrp, N_EXPERTS, ct, d), BF16),
        compiler_params=_cp("parallel", "parallel"),
        name="moe_ffn",
    )(xe, gs, mod, wg, wu, wd)


def _moe_combine_kernel(ends_ref, x1_ref, slot_ref, ye_ref, o_ref, win_scr, *, ct, nch):
    gi = pl.program_id(0)
    j = pl.program_id(1)
    mt = 128
    wide = N_EXPERTS * mt
    shi, slo = _split2(slot_ref[...])
    col_e = lax.broadcasted_iota(jnp.int32, (N_EXPERTS, wide), 1) // mt
    row_e = lax.broadcasted_iota(jnp.int32, (N_EXPERTS, wide), 0)
    expand = _bf(jnp.where(col_e == row_e, 1.0, 0.0))
    sb = _dot(shi, expand) + _dot(slo, expand)
    lane = lax.broadcasted_iota(jnp.int32, (1, mt), 1)
    starts, his, targets = [], [], []
    for e in range(N_EXPERTS):
        base = (gi * N_EXPERTS + e) * nch
        lo = jnp.where(j == 0, 0, ends_ref[base + jnp.maximum(j - 1, 0)])
        his.append(ends_ref[base + j])
        start = pl.multiple_of(jnp.minimum((lo // 16) * 16, ct - mt), 16)
        win_scr[e * mt:(e + 1) * mt, :] = ye_ref[0, e, pl.ds(start, mt), :]
        starts.append(start)
        targets.append((lane + start).astype(F32))
    onehot = _bf(jnp.where(sb == jnp.concatenate(targets, axis=1), 1.0, 0.0))
    o_ref[...] = x1_ref[...] + _dot(onehot, win_scr[...])
    for e in range(N_EXPERTS):
        for w in range(1, ct // mt):
            wlo = starts[e] + w * mt

            @pl.when(wlo < his[e])
            def _(e=e, wlo=wlo):
                ws = pl.multiple_of(jnp.minimum(wlo, ct - mt), 16)
                sbe = sb[:, e * mt:(e + 1) * mt]
                hit = jnp.where(sbe == (lane + ws).astype(F32), 1.0, 0.0) * jnp.where(sbe >= wlo.astype(F32), 1.0, 0.0)
                o_ref[...] += _dot(_bf(hit), ye_ref[0, e, pl.ds(ws, mt), :])


def _moe_combine(ends, x1, slot, ye, lg, ct, kc):
    tokens, d = x1.shape
    ngrp = tokens // lg
    nch = lg // kc
    grid_spec = pltpu.PrefetchScalarGridSpec(
        num_scalar_prefetch=1,
        grid=(ngrp, nch),
        in_specs=[pl.BlockSpec((kc, d), lambda gi, j, ends: (gi * nch + j, 0)),
                  pl.BlockSpec((kc, N_EXPERTS), lambda gi, j, ends: (gi * nch + j, 0)),
                  pl.BlockSpec((1, N_EXPERTS, ct, d), lambda gi, j, ends: (gi, 0, 0, 0),
                               pipeline_mode=pl.Buffered(1))],
        out_specs=pl.BlockSpec((kc, d), lambda gi, j, ends: (gi * nch + j, 0)),
        scratch_shapes=[pltpu.VMEM((N_EXPERTS * 128, d), BF16)])
    return pl.pallas_call(
        functools.partial(_moe_combine_kernel, ct=ct, nch=nch),
        grid_spec=grid_spec,
        out_shape=jax.ShapeDtypeStruct((tokens, d), F32),
        compiler_params=_cp("parallel", "arbitrary"),
        name="moe_combine",
    )(ends, x1, slot, ye)


def _rope_tables(seq):
    t = np.arange(seq)
    n_freq = HEAD_DIM // 4
    inv = ROPE_THETA ** (-np.arange(n_freq, dtype=np.float32) / n_freq)
    ang = np.concatenate([(t // GRID_W).astype(np.float32)[:, None] * inv,
                          (t % GRID_W).astype(np.float32)[:, None] * inv], axis=-1)
    ang = jnp.asarray(ang, dtype=F32)
    cos, sin = jnp.cos(ang), jnp.sin(ang)
    cos_t = jnp.tile(jnp.concatenate([cos, cos], axis=-1), (1, SWA_HEADS))
    sin_t = jnp.tile(jnp.concatenate([-sin, sin], axis=-1), (1, SWA_HEADS))
    return cos_t, sin_t


def _blockdiag2(w):
    z = jnp.zeros_like(w[0])
    return jnp.concatenate([jnp.concatenate([w[0], z], axis=1), jnp.concatenate([z, w[1]], axis=1)], axis=0)


def _layer_params(l, ada_w, ada_b, norm1_g, norm2_g, w_in, na_q_norm, na_k_norm, na_rpb, rw_mu, rw_w0, rw_w2,
                  rw_a0, rw_a2, rw_g2, rw_k_k, rw_k_a, rw_r_k, rw_ln_g, rw_ln_b, swa_q_norm, swa_k_norm,
                  swa_sink, w_out, w_router, w_gate, w_up, w_down):
    wr = w_router[l]
    wr_hi = wr.astype(BF16)
    w_rw = w_in[l][:, 3 * NA_W:3 * NA_W + RW_IN_W]
    return {
        'n1': norm1_g[l][None], 'n2': norm2_g[l][None], 'w_in': w_in[l].astype(BF16),
        'w_rw_lo': (w_rw - w_rw.astype(BF16).astype(F32)).astype(BF16),
        'gains': (jnp.tile(na_q_norm[l], NA_HEADS)[None], jnp.tile(na_k_norm[l], NA_HEADS)[None],
                  jnp.tile(swa_q_norm[l], SWA_HEADS)[None], jnp.tile(swa_k_norm[l], SWA_KV_HEADS)[None]),
        'bias': _na_bias_table(na_rpb[l]),
        'mu': rw_mu[l], 'w0': rw_w0[l].reshape(1, 2 * RW_W), 'w2': _blockdiag2(rw_w2[l]).astype(BF16),
        'a0': rw_a0[l].reshape(1, 2 * RW_W), 'a2': _blockdiag2(rw_a2[l]).astype(BF16),
        'g2': rw_g2[l].astype(BF16), 'k_k': rw_k_k[l][None], 'k_a': rw_k_a[l][None],
        'r_k': rw_r_k[l].reshape(1, RW_W), 'ln_g': rw_ln_g[l][None], 'ln_b': rw_ln_b[l][None],
        'sink': swa_sink[l], 'w_out': w_out[l].astype(BF16),
        'wr_hi': wr_hi, 'wr_lo': (wr - wr_hi.astype(F32)).astype(BF16),
        'wg': w_gate, 'wu': w_up, 'wd': w_down, 'layer': l,
        'ones256': _block_ones(RW_W),
    }


def _mix_and_ffn(x, mod, p, oa, oc, urw, s0_bd, scan_consts, seq, per_request_mod, group):
    r, kk, v, ld, kka, kd, g, bonus = _rw_pre(urw, p, seq)
    o2, sfin = _rw_scan(r, kk, v, ld, kka, kd, s0_bd, scan_consts, seq)
    x1, h2, aff = _finish(x, oa, oc, o2, bonus, g, mod, p, seq, per_request_mod)
    slot, slotrow, gfull, ends, kc = _topk(aff, seq, group)
    cap = EC_CAPACITY * seq // N_EXPERTS
    lg, ct = group * seq, group * cap
    xe, gs = _moe_dispatch(ends, h2, slotrow, gfull, lg, ct, kc)
    ye = _moe_ffn(xe, gs, mod, p['wg'], p['wu'], p['wd'], p['layer'], per_request_mod)
    return _moe_combine(ends, x1, slot, ye, lg, ct, kc), sfin


def _context_layer(x, mod, p, ones384, scan_consts, seq):
    qa, ka, va, urw, qc, kc, vc = _proj(x, mod, p['n1'], p['w_in'], p['gains'], ones384, None, seq, False, F32,
                                        w_rw_lo=p['w_rw_lo'])
    oa, oc = _ctx_attn(p['sink'], qa, ka, va, qc, kc, vc, seq)
    nreq = x.shape[0] // seq
    s0 = jnp.zeros((nreq, 2, HEAD_DIM, RW_W), F32)
    y, sfin = _mix_and_ffn(x, mod, p, oa, oc, urw, s0, scan_consts, seq, False, CTX_GROUP)
    return y, ka, va, kc, vc, sfin


def _latent_layer(x, mod, p, ones384, scan_consts, rope_tabs, seq, kx_na, vx_na, kx_swa, vx_swa, s0_bd, past):
    qa, ka, va, urw, qc, kc, vc = _proj(x, mod, p['n1'], p['w_in'], p['gains'], ones384, rope_tabs, seq, True,
                                        BF16)
    oa = _na_attn(qa, ka, va, kx_na, vx_na, p['bias'], seq, past)
    oc = _swa_attn(p['sink'], qc, kc, vc, kx_swa, vx_swa, seq, past)
    y, _ = _mix_and_ffn(x, mod, p, oa, oc, urw, s0_bd, scan_consts, seq, True, 1)
    return y


def _cache_layout(zs, nreq, seq, heads):
    z = jnp.stack(zs, axis=0).reshape(len(zs), nreq, seq, heads, HEAD_DIM)
    return z.transpose(1, 0, 3, 2, 4)


def _tokens_first(z):
    b, nl, h, n, dh = z.shape
    return z.transpose(1, 0, 3, 2, 4).reshape(nl, b * n, h * dh)


def kernel(x_prompt, x_sample, cache_na_k, cache_na_v, cache_swa_k, cache_swa_v, state_rwkv, c, c_ctx, ada_w, ada_b, norm1_g, norm2_g, w_in, na_q_norm, na_k_norm, na_rpb, rw_mu, rw_w0, rw_w2, rw_a0, rw_a2, rw_g2, rw_k_k, rw_k_a, rw_r_k, rw_ln_g, rw_ln_b, swa_q_norm, swa_k_norm, swa_sink, w_out, w_router, w_gate, w_up, w_down):
    nb, seq, d = x_prompt.shape
    db, dseq, _ = x_sample.shape
    depth = ada_w.shape[0]
    past = cache_na_k.shape[3]
    cond = jnp.concatenate([c, c_ctx[None], jnp.zeros((16 - db - 1, d), F32)], axis=0)
    mod_all = _adaln(cond, ada_w, ada_b).reshape(depth, 16, 6, d)
    ones384 = _block_ones(NA_W)
    scan_consts = _rw_masks()
    rope_tabs = _rope_tables(dseq)
    xp = x_prompt.reshape(nb * seq, d)
    xs = x_sample.reshape(db * dseq, d)
    new_ka, new_va, new_kc, new_vc, new_s = [], [], [], [], []
    wg_bf, wu_bf, wd_bf = w_gate.astype(BF16), w_up.astype(BF16), w_down.astype(BF16)
    kx_na, vx_na = _tokens_first(cache_na_k), _tokens_first(cache_na_v)
    kx_swa, vx_swa = _tokens_first(cache_swa_k), _tokens_first(cache_swa_v)
    s0_lat = jnp.transpose(state_rwkv, (1, 0, 2, 4, 3, 5)).reshape(depth, db, 2, HEAD_DIM, RW_W)
    for l in range(depth):
        p = _layer_params(l, ada_w, ada_b, norm1_g, norm2_g, w_in, na_q_norm, na_k_norm, na_rpb, rw_mu, rw_w0,
                          rw_w2, rw_a0, rw_a2, rw_g2, rw_k_k, rw_k_a, rw_r_k, rw_ln_g, rw_ln_b, swa_q_norm,
                          swa_k_norm, swa_sink, w_out, w_router, wg_bf, wu_bf, wd_bf)
        mod_ctx = mod_all[l, db:db + 1]
        mod_lat = mod_all[l, 0:db]
        xp, ka, va, kc, vc, sfin = _context_layer(xp, mod_ctx, p, ones384, scan_consts, seq)
        new_ka.append(ka)
        new_va.append(va)
        new_kc.append(kc)
        new_vc.append(vc)
        new_s.append(sfin)
        xs = _latent_layer(xs, mod_lat, p, ones384, scan_consts, rope_tabs, dseq,
                           kx_na[l], vx_na[l], kx_swa[l], vx_swa[l], s0_lat[l], past)
    states = jnp.stack(new_s, axis=0).reshape(depth, nb, 2, HEAD_DIM, RW_HEADS, HEAD_DIM)
    return (xp.reshape(nb, seq, d), xs.reshape(db, dseq, d),
            _cache_layout(new_ka, nb, seq, NA_HEADS), _cache_layout(new_va, nb, seq, NA_HEADS),
            _cache_layout(new_kc, nb, seq, SWA_KV_HEADS), _cache_layout(new_vc, nb, seq, SWA_KV_HEADS),
            jnp.transpose(states, (1, 0, 2, 4, 3, 5)))
```

```python
import functools

import numpy as np
import jax
import jax.numpy as jnp
from jax import lax
from jax.experimental import pallas as pl
from jax.experimental.pallas import tpu as pltpu

F32 = jnp.float32
BF16 = jnp.bfloat16

HEAD_DIM = 64
GRID_W = 64
NA_HEADS = 6
NA_KH = 8
NA_KW = 16
RW_HEADS = 4
SWA_HEADS = 6
SWA_KV_HEADS = 2
SWA_WINDOW = 128
N_EXPERTS = 16
EC_CAPACITY = 2
ROPE_THETA = 10000.0
NORM_EPS = 1e-6
GN_EPS = 64e-5
NEG_INF = -1e30
SUB_ROWS = 256
NA_UNROLL = 8
SWA_UNROLL = 2
RW_CHUNK = 64
assert RW_CHUNK == HEAD_DIM
CTX_GROUP = 16
RW_W = RW_HEADS * HEAD_DIM
NA_W = NA_HEADS * HEAD_DIM
SWA_W = SWA_HEADS * HEAD_DIM
SWA_KV_W = SWA_KV_HEADS * HEAD_DIM
RW_IN_W = 1152
VMEM_LIMIT = 56 * 1024 * 1024


def _cp(*sem):
    return pltpu.CompilerParams(dimension_semantics=sem, vmem_limit_bytes=VMEM_LIMIT)


def _bf(x):
    return x.astype(BF16)


def _dot(a, b):
    return jnp.dot(a, b, preferred_element_type=F32)


def _dot_nt(a, b):
    return lax.dot_general(a, b, (((1,), (1,)), ((), ())), preferred_element_type=F32)


def _dot_tn(a, b):
    return lax.dot_general(a, b, (((0,), (0,)), ((), ())), preferred_element_type=F32)


def _split2(x):
    hi = x.astype(BF16)
    lo = (x - hi.astype(F32)).astype(BF16)
    return hi, lo


def _split3(x):
    hi = x.astype(BF16)
    r1 = x - hi.astype(F32)
    mid = r1.astype(BF16)
    lo = (r1 - mid.astype(F32)).astype(BF16)
    return hi, mid, lo


def _dot2(a, b_bf):
    hi, lo = _split2(a)
    return _dot(hi, b_bf) + _dot(lo, b_bf)


def _sigmoid(x):
    return 1.0 / (1.0 + jnp.exp(-x))


def _block_ones(width):
    i = np.arange(width) // HEAD_DIM
    return jnp.asarray((i[:, None] == i[None, :]).astype(np.float32), dtype=BF16)


def _row_tile(seq):
    return 512 if seq % 512 == 0 else 256


def _full(shape):
    return pl.BlockSpec(shape, lambda *_: (0,) * len(shape))


def _adaln_kernel(c_ref, w_ref, b_ref, o_ref):
    c = c_ref[...]
    s = c * _sigmoid(c)
    shi, slo = _split2(s)
    whi, wlo = _split2(w_ref[0])
    o_ref[0] = _dot(shi, whi) + _dot(slo, whi) + _dot(shi, wlo) + b_ref[0]


def _adaln(cond, ada_w, ada_b):
    nl, d, n6 = ada_w.shape
    tn = 1536
    rows = cond.shape[0]
    return pl.pallas_call(
        _adaln_kernel,
        grid=(nl, n6 // tn),
        in_specs=[pl.BlockSpec((rows, d), lambda l, j: (0, 0)),
                  pl.BlockSpec((1, d, tn), lambda l, j: (l, 0, j)),
                  pl.BlockSpec((1, 1, tn), lambda l, j: (l, 0, j))],
        out_specs=pl.BlockSpec((1, rows, tn), lambda l, j: (l, 0, j)),
        out_shape=jax.ShapeDtypeStruct((nl, rows, n6), F32),
        compiler_params=_cp("parallel", "parallel"),
        name="adaln",
    )(cond, ada_w, ada_b.reshape(nl, 1, n6))


def _head_sums(zz, ones128):
    zz = _bf(zz)
    parts = [_dot(zz[:, i:i + 128], ones128) for i in range(0, zz.shape[1], 128)]
    return jnp.concatenate(parts, axis=1) if len(parts) > 1 else parts[0]


def _head_norm(z, gain, ones128):
    ms = _head_sums(z * z, ones128) * (1.0 / HEAD_DIM)
    return z * lax.rsqrt(ms + NORM_EPS) * gain


def _rope(z, cos, sin_signed):
    w = z.shape[1]
    lane = lax.broadcasted_iota(jnp.int32, z.shape, 1)
    first = (lane % HEAD_DIM) < (HEAD_DIM // 2)
    swapped = jnp.where(first, pltpu.roll(z, w - HEAD_DIM // 2, 1), pltpu.roll(z, HEAD_DIM // 2, 1))
    return z * cos + swapped * sin_signed


def _proj_kernel(*refs, rope, split_rw):
    refs = list(refs)
    x_ref, mod_ref, n1_ref, w_ref, gqa_ref, gka_ref, gqc_ref, gkc_ref, ones_ref = refs[:9]
    qa_ref, ka_ref, va_ref, urw_ref, qc_ref, kc_ref, vc_ref = refs[-7:]
    extra = refs[9:-7]
    if rope:
        cos_ref, sin_ref = extra[:2]
    sh1 = mod_ref[0, 0:1, :]
    sc1 = mod_ref[0, 1:2, :]
    o0 = 0
    o1 = NA_W
    o2 = 2 * NA_W
    o3 = 3 * NA_W
    o4 = o3 + RW_IN_W
    o5 = o4 + SWA_W
    o6 = o5 + SWA_KV_W
    ones = ones_ref[0:128, 0:128]
    ones_kv = ones
    subs = [slice(i, i + SUB_ROWS) for i in range(0, x_ref.shape[0], SUB_ROWS)]
    hs, us = [], []
    for sl in subs:
        x = x_ref[sl, :]
        ms = jnp.mean(x * x, axis=-1, keepdims=True)
        hs.append(x * lax.rsqrt(ms + NORM_EPS) * n1_ref[...] * (1.0 + sc1) + sh1)
    h_hi = [_bf(h) for h in hs]
    us = [_dot(hh, w_ref[0]) for hh in h_hi]
    for sl, h, hh, u in zip(subs, hs, h_hi, us):
        qa = _head_norm(u[:, o0:o1], gqa_ref[...], ones)
        ka = _head_norm(u[:, o1:o2], gka_ref[...], ones)
        qc = _head_norm(u[:, o4:o5], gqc_ref[...], ones)
        kc = _head_norm(u[:, o5:o6], gkc_ref[...], ones_kv)
        if rope:
            qc = _rope(qc, cos_ref[sl, :], sin_ref[sl, :])
            kc = _rope(kc, cos_ref[sl, 0:SWA_KV_W], sin_ref[sl, 0:SWA_KV_W])
        qa_ref[sl, :] = qa.astype(qa_ref.dtype)
        ka_ref[sl, :] = ka.astype(ka_ref.dtype)
        va_ref[sl, :] = u[:, o2:o3].astype(va_ref.dtype)
        urw = u[:, o3:o4]
        if split_rw:
            h_lo = _bf(h - hh.astype(F32))
            urw = urw + _dot(h_lo, w_ref[0, :, o3:o4]) + _dot(hh, extra[-1][...])
        urw_ref[sl, :] = urw
        qc_ref[sl, :] = qc.astype(qc_ref.dtype)
        kc_ref[sl, :] = kc.astype(kc_ref.dtype)
        vc_ref[sl, :] = u[:, o6:].astype(vc_ref.dtype)


def _proj(x, mod, n1, w_in_bf, layer, gains, ones384, rope_tabs, seq, per_request_mod, qkv_dtype, w_rw_lo=None):
    tokens, d = x.shape
    tm = _row_tile(seq)
    tiles_per_req = seq // tm
    in_w = w_in_bf.shape[2]
    rope = rope_tabs is not None
    mod_map = (lambda i: (i // tiles_per_req, 0, 0)) if per_request_mod else (lambda i: (0, 0, 0))
    row = lambda w: pl.BlockSpec((tm, w), lambda i: (i, 0))
    in_specs = [row(d), pl.BlockSpec((1, 6, d), mod_map), _full((1, d)),
                pl.BlockSpec((1, d, in_w), lambda i: (layer, 0, 0)),
                _full((1, NA_W)), _full((1, NA_W)), _full((1, SWA_W)), _full((1, SWA_KV_W)),
                _full((NA_W, NA_W))]
    args = [x, mod, n1, w_in_bf, *gains, ones384]
    if rope:
        tab = pl.BlockSpec((tm, SWA_W), lambda i: (i % tiles_per_req, 0))
        in_specs += [tab, tab]
        args += list(rope_tabs)
    if w_rw_lo is not None:
        in_specs.append(_full((d, RW_IN_W)))
        args.append(w_rw_lo)
    widths = [NA_W, NA_W, NA_W, RW_IN_W, SWA_W, SWA_KV_W, SWA_KV_W]
    dtypes = [qkv_dtype, qkv_dtype, qkv_dtype, F32, qkv_dtype, qkv_dtype, qkv_dtype]
    return pl.pallas_call(
        functools.partial(_proj_kernel, rope=rope, split_rw=w_rw_lo is not None),
        grid=(tokens // tm,),
        in_specs=in_specs,
        out_specs=[row(w) for w in widths],
        out_shape=[jax.ShapeDtypeStruct((tokens, w), dt) for w, dt in zip(widths, dtypes)],
        compiler_params=_cp("parallel"),
        name="proj",
    )(*args)


def _half_masks(width=2 * HEAD_DIM):
    lane = lax.broadcasted_iota(jnp.int32, (1, width), 1)
    return lane < HEAD_DIM, lane >= HEAD_DIM


def _swap_halves(z):
    return pltpu.roll(z, HEAD_DIM, 1)


def _ctx_attn_kernel(sink_ref, qa_ref, ka_ref, va_ref, qc_ref, kc_ref, vc_ref, oa_ref, oc_ref):
    scale = HEAD_DIM ** -0.5
    m0, m1 = _half_masks()
    masks = (m0, m1)
    for pair in range(NA_HEADS // 2):
        sl = slice(pair * 128, (pair + 1) * 128)
        qp = qa_ref[:, sl].astype(F32) * scale
        kp = _bf(ka_ref[:, sl])
        vp = _bf(va_ref[:, sl])
        outs = []
        for half in range(2):
            qm = _bf(jnp.where(masks[half], qp, 0.0))
            s = _dot_nt(qm, kp)
            m = jnp.max(s, axis=-1, keepdims=True)
            e = jnp.exp(s - m)
            l = jnp.sum(e, axis=-1, keepdims=True)
            outs.append(_dot(_bf(e), vp) / l)
        oa_ref[:, sl] = jnp.where(m0, outs[0], outs[1])
    kc = _bf(kc_ref[...])
    vc = _bf(vc_ref[...])
    group = SWA_HEADS // SWA_KV_HEADS
    for pair in range(SWA_HEADS // 2):
        sl = slice(pair * 128, (pair + 1) * 128)
        qp = qc_ref[:, sl].astype(F32) * scale
        outs = []
        for half in range(2):
            h = 2 * pair + half
            g = h // group
            qh = qp if g == half else _swap_halves(qp)
            qm = _bf(jnp.where(masks[g], qh, 0.0))
            s = _dot_nt(qm, kc)
            sk = sink_ref[h]
            m = jnp.maximum(jnp.max(s, axis=-1, keepdims=True), sk)
            e = jnp.exp(s - m)
            l = jnp.sum(e, axis=-1, keepdims=True) + jnp.exp(sk - m)
            o = _dot(_bf(e), vc) / l
            outs.append(o if g == half else _swap_halves(o))
        oc_ref[:, sl] = jnp.where(m0, outs[0], outs[1])


def _ctx_attn(sink, qa, ka, va, qc, kc, vc, seq):
    tokens = qa.shape[0]
    blk = lambda w: pl.BlockSpec((seq, w), lambda b: (b, 0))
    return pl.pallas_call(
        _ctx_attn_kernel,
        grid=(tokens // seq,),
        in_specs=[pl.BlockSpec(memory_space=pltpu.SMEM), blk(NA_W), blk(NA_W), blk(NA_W), blk(SWA_W),
                  blk(SWA_KV_W), blk(SWA_KV_W)],
        out_specs=[blk(NA_W), blk(SWA_W)],
        out_shape=[jax.ShapeDtypeStruct((tokens, NA_W), F32), jax.ShapeDtypeStruct((tokens, SWA_W), F32)],
        compiler_params=_cp("parallel"),
        name="ctx_attn",
    )(sink, qa, ka, va, qc, kc, vc)


def _na_bias_kernel(rpb_ref, o_ref):
    h = pl.program_id(0)
    nrow = 2 * NA_KH - 1
    ncol = 2 * NA_KW - 1
    width = NA_KH * GRID_W
    shape = (GRID_W, width)
    lane = lax.broadcasted_iota(jnp.int32, shape, 1)
    qc = lax.broadcasted_iota(jnp.int32, shape, 0)
    kc = lane % GRID_W
    c_start = jnp.clip(qc - NA_KW // 2, 0, GRID_W - NA_KW)
    ok = (kc >= c_start) & (kc < c_start + NA_KW)
    d_col = jnp.clip(kc - qc, 1 - NA_KW, NA_KW - 1) + NA_KW - 1
    key_row = lax.broadcasted_iota(jnp.int32, (1, width), 1) // GRID_W

    def case_body(case, carry):
        acc = jnp.zeros(shape, F32)
        for dc in range(ncol):
            val = jnp.zeros((1, width), F32)
            for i in range(NA_KH):
                val = jnp.where(key_row == i, rpb_ref[(h * nrow + case + i) * ncol + dc], val)
            acc = jnp.where(d_col == dc, val, acc)
        o_ref[0, pl.ds(case, 1)] = jnp.where(ok, acc, NEG_INF)[None]
        return carry

    lax.fori_loop(0, NA_KH, case_body, 0)


def _na_bias_table(rpb):
    nh = rpb.shape[0]
    return pl.pallas_call(
        _na_bias_kernel,
        grid=(nh,),
        in_specs=[pl.BlockSpec(memory_space=pltpu.SMEM)],
        out_specs=pl.BlockSpec((1, NA_KH, GRID_W, NA_KH * GRID_W), lambda h: (h // 2, 0, h % 2, 0)),
        out_shape=jax.ShapeDtypeStruct((nh // 2, NA_KH, 2 * GRID_W, NA_KH * GRID_W), F32),
        compiler_params=_cp("parallel"),
        name="na_bias",
    )(rpb.reshape(-1))


def _na_kernel(q_ref, k_ref, v_ref, kx_ref, vx_ref, bias_ref, o_ref, *, rows):
    scale = HEAD_DIM ** -0.5
    m0, m1 = _half_masks()
    kx = _bf(kx_ref[...])
    vx = _bf(vx_ref[...])

    def body(it, carry):
        us = range(NA_UNROLL)
        r = [it * NA_UNROLL + u for u in us]
        rs = [jnp.clip(r[u] - NA_KH // 2, 0, rows - NA_KH) for u in us]
        case = [rs[u] - r[u] + NA_KH - 1 for u in us]
        q0 = [pl.multiple_of(r[u] * GRID_W, GRID_W) for u in us]
        k0 = [pl.multiple_of(rs[u] * GRID_W, GRID_W) for u in us]
        qp = [q_ref[pl.ds(q0[u], GRID_W), :].astype(F32) * scale for u in us]
        kw = [_bf(k_ref[pl.ds(k0[u], NA_KH * GRID_W), :]) for u in us]
        vw = [_bf(v_ref[pl.ds(k0[u], NA_KH * GRID_W), :]) for u in us]
        q2 = [_bf(jnp.concatenate([jnp.where(m0, qp[u], 0.0), jnp.where(m1, qp[u], 0.0)], axis=0)) for u in us]
        s = [jnp.concatenate([_dot_nt(q2[u], kw[u]) + bias_ref[0, pl.ds(case[u], 1)][0], _dot_nt(q2[u], kx)],
                             axis=1) for u in us]
        m = [jnp.max(s[u], axis=-1, keepdims=True) for u in us]
        e = [jnp.exp(s[u] - m[u]) for u in us]
        l = [jnp.sum(e[u], axis=-1, keepdims=True) for u in us]
        o = [_dot(_bf(e[u]), jnp.concatenate([vw[u], vx], axis=0)) / l[u] for u in us]
        for u in us:
            o_ref[pl.ds(q0[u], GRID_W), :] = jnp.where(m0, o[u][0:GRID_W], o[u][GRID_W:])
        return carry

    lax.fori_loop(0, rows // NA_UNROLL, body, 0)


def _na_attn(q, k, v, kx, vx, bias, seq, past):
    tokens = q.shape[0]
    nb = tokens // seq
    rows = seq // GRID_W
    blk = pl.BlockSpec((seq, 128), lambda b, p: (b, p))
    cblk = pl.BlockSpec((past, 128), lambda b, p: (b, p))
    return pl.pallas_call(
        functools.partial(_na_kernel, rows=rows),
        grid=(nb, NA_HEADS // 2),
        in_specs=[blk, blk, blk, cblk, cblk,
                  pl.BlockSpec((1, NA_KH, 2 * GRID_W, NA_KH * GRID_W), lambda b, p: (p, 0, 0, 0))],
        out_specs=blk,
        out_shape=jax.ShapeDtypeStruct((tokens, NA_W), F32),
        compiler_params=_cp("parallel", "parallel"),
        name="na_attn",
    )(q, k, v, kx, vx, bias)


def _swa_kernel(sink_ref, q_ref, k_ref, v_ref, kx_ref, vx_ref, o_ref, *, seq):
    scale = HEAD_DIM ** -0.5
    blk = SWA_WINDOW
    m0, m1 = _half_masks()
    masks = (m0, m1)
    kx = _bf(kx_ref[...])
    vx = _bf(vx_ref[...])
    group = SWA_HEADS // SWA_KV_HEADS

    sk = []
    for g in range(SWA_KV_HEADS):
        sk.append(jnp.concatenate([jnp.full((blk, 1), sink_ref[h], F32) for h in range(g * group, (g + 1) * group)],
                                  axis=0))

    def body(it, carry):
        us = range(SWA_UNROLL)
        cs = [(u, g) for u in us for g in range(SWA_KV_HEADS)]
        nb = [it * SWA_UNROLL + u for u in us]
        ks = [pl.multiple_of(jnp.clip((nb[u] - 1) * blk, 0, seq - 3 * blk), blk) for u in us]
        q0 = [pl.multiple_of(nb[u] * blk, blk) for u in us]
        kw = [_bf(k_ref[pl.ds(ks[u], 3 * blk), :]) for u in us]
        vw = [_bf(v_ref[pl.ds(ks[u], 3 * blk), :]) for u in us]
        ok = []
        for u in us:
            qpos = q0[u] + lax.broadcasted_iota(jnp.int32, (group * blk, 1), 0) % blk
            kpos = ks[u] + lax.broadcasted_iota(jnp.int32, (1, 3 * blk), 1)
            ok.append(jnp.abs(qpos - kpos) <= SWA_WINDOW)
        qg = {}
        for u in us:
            pairs = [q_ref[pl.ds(q0[u], blk), p * 128:(p + 1) * 128].astype(F32) * scale
                     for p in range(SWA_HEADS // 2)]
            for g in range(SWA_KV_HEADS):
                qs = []
                for h in range(g * group, (g + 1) * group):
                    qh = pairs[h // 2] if h % 2 == g else _swap_halves(pairs[h // 2])
                    qs.append(jnp.where(masks[g], qh, 0.0))
                qg[u, g] = _bf(jnp.concatenate(qs, axis=0))
        sw = {c: jnp.where(ok[c[0]], _dot_nt(qg[c], kw[c[0]]), NEG_INF) for c in cs}
        sx = {c: _dot_nt(qg[c], kx) for c in cs}
        m = {c: jnp.maximum(jnp.maximum(jnp.max(sw[c], axis=-1, keepdims=True),
                                        jnp.max(sx[c], axis=-1, keepdims=True)), sk[c[1]]) for c in cs}
        ew = {c: jnp.exp(sw[c] - m[c]) for c in cs}
        ex = {c: jnp.exp(sx[c] - m[c]) for c in cs}
        l = {c: jnp.sum(ew[c], axis=-1, keepdims=True) + jnp.sum(ex[c], axis=-1, keepdims=True)
             + jnp.exp(sk[c[1]] - m[c]) for c in cs}
        o = {c: (_dot(_bf(ew[c]), vw[c[0]]) + _dot(_bf(ex[c]), vx)) / l[c] for c in cs}
        for u in us:
            head_out = []
            for g in range(SWA_KV_HEADS):
                for i in range(group):
                    h = g * group + i
                    oh = o[u, g][i * blk:(i + 1) * blk]
                    head_out.append(oh if h % 2 == g else _swap_halves(oh))
            for p in range(SWA_HEADS // 2):
                o_ref[pl.ds(q0[u], blk), p * 128:(p + 1) * 128] = jnp.where(m0, head_out[2 * p], head_out[2 * p + 1])
        return carry

    lax.fori_loop(0, seq // (blk * SWA_UNROLL), body, 0)


def _swa_attn(sink, q, k, v, kx, vx, seq, past):
    tokens = q.shape[0]
    blk = lambda w: pl.BlockSpec((seq, w), lambda b: (b, 0))
    cblk = pl.BlockSpec((past, SWA_KV_W), lambda b: (b, 0))
    return pl.pallas_call(
        functools.partial(_swa_kernel, seq=seq),
        grid=(tokens // seq,),
        in_specs=[pl.BlockSpec(memory_space=pltpu.SMEM), blk(SWA_W), blk(SWA_KV_W), blk(SWA_KV_W), cblk, cblk],
        out_specs=blk(SWA_W),
        out_shape=jax.ShapeDtypeStruct((tokens, SWA_W), F32),
        compiler_params=_cp("parallel"),
        name="swa_attn",
    )(sink, q, k, v, kx, vx)


def _rw_pre_kernel(u_ref, up_ref, un_ref, mu_ref, w0_ref, w2_ref, a0_ref, a2_ref, g2_ref, kk_ref_, ka_ref_,
                   rk_ref, ones_ref, r_o, kk_o, v_o, ld_o, kka_o, kd_o, g_o, bonus_o, *, tiles_per_req):
    i = pl.program_id(0)
    u = u_ref[...]
    tm = u.shape[0]
    rowi = lax.broadcasted_iota(jnp.int32, (tm, 1), 0)
    first = (i % tiles_per_req) == 0
    last = (i % tiles_per_req) == tiles_per_req - 1
    prev_row = jnp.where(first, 0.0, up_ref[7:8, :])
    next_row = jnp.where(last, 0.0, un_ref[0:1, :])
    prev = jnp.where(rowi == 0, prev_row, pltpu.roll(u, 1, 0))
    nxt = jnp.where(rowi == tm - 1, next_row, pltpu.roll(u, tm - 1, 0))
    us = u + mu_ref[0:1, :] * (prev - u) + mu_ref[1:2, :] * (nxt - u)
    r = us[:, 0:RW_W]
    k = us[:, RW_W:2 * RW_W]
    v = us[:, 2 * RW_W:3 * RW_W]
    wl = us[:, 3 * RW_W:3 * RW_W + 128]
    al = us[:, 3 * RW_W + 128:3 * RW_W + 256]
    gl = us[:, 3 * RW_W + 256:3 * RW_W + 384]
    z = -(w0_ref[...] + _dot(_bf(jnp.tanh(wl)), w2_ref[...]))
    softplus = jnp.maximum(z, 0.0) + jnp.log(1.0 + jnp.exp(-jnp.abs(z)))
    w = -softplus - 0.5
    ld = -jnp.exp(w)
    a = _sigmoid(a0_ref[...] + _dot(_bf(al), a2_ref[...]))
    g = _dot(_bf(_sigmoid(gl)), g2_ref[...])
    ones = ones_ref[...]
    kkr = k * kk_ref_[...]
    kk = kkr * lax.rsqrt(jnp.maximum(_dot2(kkr * kkr, ones), 1e-24))
    k_a = ka_ref_[...]
    kd_f = k * (1.0 + (a[:, 0:RW_W] - 1.0) * k_a)
    kd_b = k * (1.0 + (a[:, RW_W:] - 1.0) * k_a)
    r_o[...] = r
    kk_o[...] = kk
    v_o[...] = v
    ld_o[...] = ld
    kka_o[:, 0:RW_W] = kk * a[:, 0:RW_W]
    kka_o[:, RW_W:] = kk * a[:, RW_W:]
    kd_o[:, 0:RW_W] = kd_f
    kd_o[:, RW_W:] = kd_b
    g_o[...] = g
    bonus_o[...] = _dot2(r * (kd_f + kd_b) * rk_ref[...], ones) * v


def _rw_pre(urw, p, seq):
    tokens = urw.shape[0]
    tm = _row_tile(seq)
    tpr = seq // tm
    nt = tokens // tm
    r8 = tm // 8
    row = lambda w: pl.BlockSpec((tm, w), lambda i: (i, 0))
    in_specs = [row(RW_IN_W),
                pl.BlockSpec((8, RW_IN_W), lambda i: (jnp.maximum(i * r8 - 1, 0), 0)),
                pl.BlockSpec((8, RW_IN_W), lambda i: (jnp.minimum((i + 1) * r8, nt * r8 - 1), 0)),
                _full((2, RW_IN_W)), _full((1, 2 * RW_W)), _full((128, 2 * RW_W)), _full((1, 2 * RW_W)),
                _full((128, 2 * RW_W)), _full((128, RW_W)), _full((1, RW_W)), _full((1, RW_W)),
                _full((1, RW_W)), _full((RW_W, RW_W))]
    widths = [RW_W, RW_W, RW_W, 2 * RW_W, 2 * RW_W, 2 * RW_W, RW_W, RW_W]
    return pl.pallas_call(
        functools.partial(_rw_pre_kernel, tiles_per_req=tpr),
        grid=(nt,),
        in_specs=in_specs,
        out_specs=[row(w) for w in widths],
        out_shape=[jax.ShapeDtypeStruct((tokens, w), F32) for w in widths],
        compiler_params=_cp("parallel"),
        name="rw_pre",
    )(urw, urw, urw, p['mu'], p['w0'], p['w2'], p['a0'], p['a2'], p['g2'], p['k_k'], p['k_a'], p['r_k'],
      p['ones256'])


def _rw_masks():
    t = RW_CHUNK
    n = RW_HEADS * t
    tt = np.arange(t)[:, None]
    ss = (np.arange(n) % t)[None, :]
    before = np.stack([ss < tt, ss > tt])
    diag = (ss == tt)
    strict = before.astype(np.float32)
    incl = (before | diag[None]).astype(np.float32)
    eye = diag.astype(np.float32)
    ti = np.arange(t)
    tri = np.stack([ti[None, :] <= ti[:, None], ti[None, :] >= ti[:, None]]).astype(np.float32)
    hd = np.arange(n) // t
    same = (hd[:, None] == hd[None, :]).astype(np.float32)
    return (jnp.asarray(strict), jnp.asarray(incl), jnp.asarray(tri, dtype=BF16), jnp.asarray(same),
            jnp.asarray(eye))


def _rw_scan_kernel(r_ref, kk_ref, v_ref, ld_ref, kka_ref, kd_ref, s0_ref, strict_ref, incl_ref, tri_ref,
                    same_ref, eye_ref, o_ref, sfin_ref, s_scr, *, nsub):
    d = pl.program_id(1)
    c = pl.program_id(2)
    t = RW_CHUNK
    n = RW_HEADS * t

    @pl.when(c == 0)
    def _():
        s_scr[...] = s0_ref[0, 0]

    strict = strict_ref[0]
    incl = incl_ref[0]
    tri = tri_ref[0]
    eye = eye_ref[...]
    same = same_ref[...]
    same_bf = _bf(same)

    def bd(x):
        return jnp.concatenate([_bf(x)] * RW_HEADS, axis=0) * same_bf

    js = range(nsub)
    rows = [pl.ds(pl.multiple_of((j + d * (nsub - 1 - 2 * j)) * t, t), t) for j in js]
    ld = [ld_ref[rows[j], :] for j in js]
    cum = []
    for j in js:
        lhi, lmid, llo = _split3(ld[j])
        cum.append(_dot(tri, lhi) + _dot(tri, lmid) + _dot(tri, llo))
    cend = [jnp.sum(ld[j], axis=0, keepdims=True) for j in js]
    kka = [kka_ref[rows[j], :] for j in js]
    kd = [kd_ref[rows[j], :] for j in js]
    v = [v_ref[rows[j], :] for j in js]
    at = [-kk_ref[rows[j], :] * jnp.exp(cum[j] - ld[j]) for j in js]
    rt = [r_ref[rows[j], :] * jnp.exp(cum[j]) for j in js]
    e_inv = [jnp.exp(-cum[j]) for j in js]
    aa = [_dot_nt(_bf(jnp.concatenate([at[j], rt[j]], axis=0)),
                  jnp.concatenate([bd(kka[j] * e_inv[j]), bd(kd[j] * e_inv[j])], axis=0)) for j in js]
    a_ab = [aa[j][0:t, 0:n] * strict for j in js]
    x = [eye + a_ab[j] for j in js]
    pw = a_ab
    for _ in range(RW_CHUNK.bit_length() - 3):
        pw = [_dot(_bf(pw[j]), bd(pw[j])) for j in js]
        x = [x[j] + _dot(_bf(x[j]), bd(pw[j])) for j in js]
    xs = [_split2(x[j]) for j in js]
    sa = [_split2(a_ab[j]) for j in js]
    ax = [_dot(sa[j][0], bd(xs[j][0])) + _dot(sa[j][1], bd(xs[j][0])) + _dot(sa[j][0], bd(xs[j][1])) for j in js]
    x = [x[j] + _dot(xs[j][0], bd(eye - x[j] + ax[j])) for j in js]
    v_bd = [bd(v[j]) for j in js]
    wv = [_dot(_bf(aa[j][0:t, n:] * strict), v_bd[j]) for j in js]
    mu = [_dot(_bf(x[j]), jnp.concatenate([bd(at[j]), bd(wv[j])], axis=1)) for j in js]
    m1 = [mu[j][:, 0:n] for j in js]
    u0 = [mu[j][:, n:] for j in js]
    e_end = [jnp.exp(cend[j] - cum[j]) for j in js]
    bend = [_bf(kka[j] * e_end[j]) for j in js]
    g = [_bf(_dot_tn(_bf(m1[j]), bend[j]) * same) for j in js]
    cst = []
    for j in js:
        full = _dot_tn(_bf(jnp.concatenate([u0[j], v[j]], axis=0)),
                       jnp.concatenate([bend[j], _bf(kd[j] * e_end[j])], axis=0)) * same
        cst.append(functools.reduce(jnp.add, [full[h * t:(h + 1) * t] for h in range(RW_HEADS)]))
    qo = [_dot(_bf(aa[j][t:, 0:n] * incl), jnp.concatenate([bd(m1[j]), bd(u0[j])], axis=1)) for j in js]
    q = [_bf(rt[j] + qo[j][:, 0:n]) for j in js]
    o0 = [qo[j][:, n:] + _dot(_bf(aa[j][t:, n:] * incl), v_bd[j]) for j in js]

    s = s_scr[...]
    for j in js:
        o_ref[0, rows[j], :] = _dot_nt(q[j], bd(s)) + o0[j]
        s = s * jnp.exp(cend[j]) + _dot(_bf(s), g[j]) + cst[j]
    s_scr[...] = s

    @pl.when(c == pl.num_programs(2) - 1)
    def _():
        sfin_ref[0, 0] = s


def _rw_scan(r, kk, v, ld, kka, kd, s0_bd, consts, seq):
    tokens = r.shape[0]
    nreq = tokens // seq
    tb = min(seq, 1024)
    nblk = seq // tb
    nsub = tb // RW_CHUNK
    n = RW_HEADS * RW_CHUNK
    cc = lambda d, c: c + d * (nblk - 1 - 2 * c)
    shared = pl.BlockSpec((tb, RW_W), lambda b, d, c: (b * nblk + cc(d, c), 0))
    dirw = pl.BlockSpec((tb, RW_W), lambda b, d, c: (b * nblk + cc(d, c), d))
    strict, incl, tri, same, eye = consts
    return pl.pallas_call(
        functools.partial(_rw_scan_kernel, nsub=nsub),
        grid=(nreq, 2, nblk),
        in_specs=[shared, shared, shared, dirw, dirw, dirw,
                  pl.BlockSpec((1, 1, HEAD_DIM, n), lambda b, d, c: (b, d, 0, 0)),
                  pl.BlockSpec((1, RW_CHUNK, n), lambda b, d, c: (d, 0, 0)),
                  pl.BlockSpec((1, RW_CHUNK, n), lambda b, d, c: (d, 0, 0)),
                  pl.BlockSpec((1, RW_CHUNK, RW_CHUNK), lambda b, d, c: (d, 0, 0)),
                  _full((n, n)), _full((RW_CHUNK, n))],
        out_specs=[pl.BlockSpec((1, tb, RW_W), lambda b, d, c: (d, b * nblk + cc(d, c), 0)),
                   pl.BlockSpec((1, 1, HEAD_DIM, n), lambda b, d, c: (b, d, 0, 0))],
        out_shape=[jax.ShapeDtypeStruct((2, tokens, RW_W), F32),
                   jax.ShapeDtypeStruct((nreq, 2, HEAD_DIM, n), F32)],
        scratch_shapes=[pltpu.VMEM((HEAD_DIM, n), F32)],
        compiler_params=_cp("parallel", "parallel", "arbitrary"),
        name="rw_scan",
    )(r, kk, v, ld, kka, kd, s0_bd, strict, incl, tri, same, eye)


def _finish_kernel(x_ref, oa_ref, oc_ref, o2_ref, bonus_ref, g_ref, mod_ref, wout_ref, lng_ref, lnb_ref,
                   n2_ref, wr_hi_ref, wr_lo_ref, ones_ref, x1_ref, h2_ref, aff_ref):
    ones = ones_ref[0:128, 0:128]
    g1 = mod_ref[0, 2:3, :]
    sh2 = mod_ref[0, 3:4, :]
    sc2 = mod_ref[0, 4:5, :]
    subs = [slice(i, i + SUB_ROWS) for i in range(0, x_ref.shape[0], SUB_ROWS)]
    mixins = []
    for sl in subs:
        y = o2_ref[0, sl, :] + o2_ref[1, sl, :]
        mu = _head_sums(y, ones) * (1.0 / HEAD_DIM)
        yc = y - mu
        var = _head_sums(yc * yc, ones) * (1.0 / HEAD_DIM)
        yn = yc * lax.rsqrt(var + GN_EPS) * lng_ref[...] + lnb_ref[...]
        ob = (yn + bonus_ref[sl, :]) * g_ref[sl, :]
        mixins.append(jnp.concatenate([_bf(oa_ref[sl, :]), _bf(ob), _bf(oc_ref[sl, :])], axis=1))
    mixes = [_dot(mixin, wout_ref[0]) for mixin in mixins]
    for sl, mix in zip(subs, mixes):
        x1 = x_ref[sl, :] + g1 * mix
        ms = jnp.mean(x1 * x1, axis=-1, keepdims=True)
        h2 = x1 * lax.rsqrt(ms + NORM_EPS) * n2_ref[...] * (1.0 + sc2) + sh2
        x1_ref[sl, :] = x1
        h2_ref[sl, :] = _bf(h2)
        hhi, hlo = _split2(h2)
        logits = _dot(hhi, wr_hi_ref[...]) + _dot(hlo, wr_hi_ref[...]) + _dot(hhi, wr_lo_ref[...])
        m = jnp.max(logits, axis=-1, keepdims=True)
        e = jnp.exp(logits - m)
        aff_ref[sl, :] = e / jnp.sum(e, axis=-1, keepdims=True)


def _finish(x, oa, oc, o2, bonus, g, mod, p, seq, per_request_mod):
    tokens, d = x.shape
    tm = _row_tile(seq)
    tpr = seq // tm
    mod_map = (lambda i: (i // tpr, 0, 0)) if per_request_mod else (lambda i: (0, 0, 0))
    row = lambda w: pl.BlockSpec((tm, w), lambda i: (i, 0))
    return pl.pallas_call(
        _finish_kernel,
        grid=(tokens // tm,),
        in_specs=[row(d), row(NA_W), row(SWA_W), pl.BlockSpec((2, tm, RW_W), lambda i: (0, i, 0)), row(RW_W),
                  row(RW_W), pl.BlockSpec((1, 6, d), mod_map),
                  pl.BlockSpec((1, d, d), lambda i: (p['layer'], 0, 0)), _full((1, RW_W)),
                  _full((1, RW_W)), _full((1, d)), _full((d, N_EXPERTS)), _full((d, N_EXPERTS)),
                  _full((RW_W, RW_W))],
        out_specs=[row(d), row(d), row(N_EXPERTS)],
        out_shape=[jax.ShapeDtypeStruct((tokens, d), F32), jax.ShapeDtypeStruct((tokens, d), BF16),
                   jax.ShapeDtypeStruct((tokens, N_EXPERTS), F32)],
        compiler_params=_cp("parallel"),
        name="finish",
    )(x, oa, oc, o2, bonus, g, mod, p['w_out'], p['ln_g'], p['ln_b'], p['n2'], p['wr_hi'], p['wr_lo'],
      p['ones256'])


def _topk_kernel(aff_ref, tri_ref, eye_ref, place_ref, slot_ref, slotrow_ref, gfull_ref, ends_ref, *, cap, group,
                 seq, tb):
    aff = aff_ref[...]
    bits = lax.bitcast_convert_type(aff, jnp.int32)
    capf = jnp.float32(cap)
    eye = eye_ref[...]
    ghi, gmid, glo = _split3(aff)
    aff_t = _dot_nt(eye, ghi) + _dot_nt(eye, gmid) + _dot_nt(eye, glo)
    bits_t = lax.bitcast_convert_type(aff_t, jnp.int32)
    rs = range(group)

    def bis(_, carry):
        los, his = carry
        nlo, nhi = [], []
        for r in rs:
            mid = los[r] + ((his[r] - los[r] + 1) >> 1)
            cnt = jnp.sum(jnp.where(bits_t[:, r * seq:(r + 1) * seq] >= mid, 1.0, 0.0), axis=1, keepdims=True)
            ge = cnt >= capf
            nlo.append(jnp.where(ge, mid, los[r]))
            nhi.append(jnp.where(ge, his[r], mid - 1))
        return tuple(nlo), tuple(nhi)

    lo0 = tuple(jnp.zeros((N_EXPERTS, 1), jnp.int32) for _ in rs)
    hi0 = tuple(jnp.full((N_EXPERTS, 1), 0x7F7FFFFF, jnp.int32) for _ in rs)
    thr_cols, _ = lax.fori_loop(0, 31, bis, (lo0, hi0))
    ri = lax.broadcasted_iota(jnp.int32, (N_EXPERTS, N_EXPERTS), 0)
    ci = lax.broadcasted_iota(jnp.int32, (N_EXPERTS, N_EXPERTS), 1)
    tri = tri_ref[...]
    for r in rs:
        thr = jnp.sum(jnp.where(ri == ci, thr_cols[r], 0), axis=0, keepdims=True)
        rows = slice(r * seq, (r + 1) * seq)
        gt = jnp.where(bits[rows] > thr, 1.0, 0.0)
        eq = jnp.where(bits[rows] == thr, 1.0, 0.0)
        need = capf - jnp.sum(gt, axis=0, keepdims=True)
        offset = float(r * cap)
        carry_g = jnp.zeros((1, N_EXPERTS), F32)
        carry_e = jnp.zeros((1, N_EXPERTS), F32)
        for blk in range(seq // tb):
            sl = slice(blk * tb, (blk + 1) * tb)
            out = slice(r * seq + blk * tb, r * seq + (blk + 1) * tb)
            pg = _dot(tri, _bf(gt[sl])) + carry_g
            pe = _dot(tri, _bf(eq[sl])) + carry_e
            carry_g = pg[tb - 1:tb, :]
            carry_e = pe[tb - 1:tb, :]
            sel = gt[sl] + eq[sl] * jnp.where(pe <= need, 1.0, 0.0)
            slot = jnp.where(sel > 0.5, pg + jnp.minimum(pe, need) - 1.0 + offset, -1.0)
            slot_ref[out, :] = slot
            ends_ref[r, blk:blk + 1, :] = carry_g + jnp.minimum(carry_e, need) + offset
            shi, slo = _split2(slot)
            slotrow_ref[0, :, 0, out] = _dot_nt(eye, shi) + _dot_nt(eye, slo)
            gfull_ref[out, :] = _bf(_dot(ghi[out], place_ref[0]) + _dot(gmid[out], place_ref[1])
                                    + _dot(glo[out], place_ref[2]))


def _topk(aff, seq, group):
    tokens = aff.shape[0]
    nreq = tokens // seq
    cap = EC_CAPACITY * seq // N_EXPERTS
    tb = min(seq, 512)
    ti = np.arange(tb)
    tri = jnp.asarray((ti[None, :] <= ti[:, None]).astype(np.float32), dtype=BF16)
    eye = jnp.asarray(np.eye(N_EXPERTS, dtype=np.float32), dtype=BF16)
    place = np.zeros((3, N_EXPERTS, 128), np.float32)
    for s in range(3):
        place[s, np.arange(N_EXPERTS), s * N_EXPERTS + np.arange(N_EXPERTS)] = 1.0
    place = jnp.asarray(place, dtype=BF16)
    nblk = seq // tb
    slot, slotrow, gfull, ends = pl.pallas_call(
        functools.partial(_topk_kernel, cap=cap, group=group, seq=seq, tb=tb),
        grid=(nreq // group,),
        in_specs=[pl.BlockSpec((group * seq, N_EXPERTS), lambda b: (b, 0)), _full((tb, tb)),
                  _full((N_EXPERTS, N_EXPERTS)), _full((3, N_EXPERTS, 128))],
        out_specs=[pl.BlockSpec((group * seq, N_EXPERTS), lambda b: (b, 0)),
                   pl.BlockSpec((1, N_EXPERTS, 1, group * seq), lambda b: (b, 0, 0, 0)),
                   pl.BlockSpec((group * seq, 128), lambda b: (b, 0)),
                   pl.BlockSpec((group, nblk, N_EXPERTS), lambda b: (b, 0, 0))],
        out_shape=[jax.ShapeDtypeStruct((tokens, N_EXPERTS), F32),
                   jax.ShapeDtypeStruct((nreq // group, N_EXPERTS, 1, group * seq), F32),
                   jax.ShapeDtypeStruct((tokens, 128), BF16),
                   jax.ShapeDtypeStruct((nreq, nblk, N_EXPERTS), F32)],
        compiler_params=_cp("parallel"),
        name="topk",
    )(aff, tri, eye, place)
    ends = ends.reshape(nreq // group, group * nblk, N_EXPERTS).transpose(0, 2, 1)
    return slot, slotrow, gfull, ends.astype(jnp.int32).reshape(-1), tb


MOE_EB = 16


def _moe_dispatch_kernel(ends_ref, h_ref, slotrow_ref, gfull_ref, xe_ref, gs_ref, *, ct, nch):
    gi = pl.program_id(0)
    eb = pl.program_id(1)
    c = pl.program_id(2)
    mt = 128

    @pl.when(c == 0)
    def _():
        xe_ref[...] = jnp.zeros_like(xe_ref)
        gs_ref[...] = jnp.zeros_like(gs_ref)

    jcol = lax.broadcasted_iota(jnp.int32, (mt, 1), 0)
    starts, his, pieces = [], [], []
    for i in range(MOE_EB):
        base = (gi * N_EXPERTS + eb * MOE_EB + i) * nch
        lo = jnp.where(c == 0, 0, ends_ref[base + jnp.maximum(c - 1, 0)])
        his.append(ends_ref[base + c])
        start = pl.multiple_of(jnp.minimum((lo // 16) * 16, ct - mt), 16)
        starts.append(start)
        pieces.append(_bf(jnp.where(slotrow_ref[0, i] == (jcol + start).astype(F32), 1.0, 0.0)))
    onehot = jnp.concatenate(pieces, axis=0)
    xw = _bf(_dot(onehot, h_ref[...]))
    gw = _bf(_dot(onehot, gfull_ref[...]))
    for i in range(MOE_EB):
        rows = pl.ds(starts[i], mt)
        xe_ref[0, i, rows, :] += xw[i * mt:(i + 1) * mt]
        gs_ref[0, i, rows, :] += gw[i * mt:(i + 1) * mt]
    for i in range(MOE_EB):
        for w in range(1, ct // mt):
            wlo = starts[i] + w * mt

            @pl.when(wlo < his[i])
            def _(i=i, wlo=wlo):
                ws = pl.multiple_of(jnp.minimum(wlo, ct - mt), 16)
                slot = slotrow_ref[0, i]
                hit = (jnp.where(slot == (jcol + ws).astype(F32), 1.0, 0.0)
                       * jnp.where(slot >= wlo.astype(F32), 1.0, 0.0))
                rows = pl.ds(ws, mt)
                xe_ref[0, i, rows, :] += _bf(_dot(_bf(hit), h_ref[...]))
                gs_ref[0, i, rows, :] += _bf(_dot(_bf(hit), gfull_ref[...]))


def _moe_dispatch(ends, h2, slotrow, gfull, lg, ct, kc):
    tokens, d = h2.shape
    ngrp = tokens // lg
    nch = lg // kc
    grid_spec = pltpu.PrefetchScalarGridSpec(
        num_scalar_prefetch=1,
        grid=(ngrp, N_EXPERTS // MOE_EB, nch),
        in_specs=[pl.BlockSpec((kc, d), lambda gi, eb, c, ends: (gi * nch + c, 0)),
                  pl.BlockSpec((1, MOE_EB, 1, kc), lambda gi, eb, c, ends: (gi, eb, 0, c)),
                  pl.BlockSpec((kc, 128), lambda gi, eb, c, ends: (gi * nch + c, 0))],
        out_specs=[pl.BlockSpec((1, MOE_EB, ct, d), lambda gi, eb, c, ends: (gi, eb, 0, 0)),
                   pl.BlockSpec((1, MOE_EB, ct, 128), lambda gi, eb, c, ends: (gi, eb, 0, 0))])
    return pl.pallas_call(
        functools.partial(_moe_dispatch_kernel, ct=ct, nch=nch),
        grid_spec=grid_spec,
        out_shape=[jax.ShapeDtypeStruct((ngrp, N_EXPERTS, ct, d), BF16),
                   jax.ShapeDtypeStruct((ngrp, N_EXPERTS, ct, 128), BF16)],
        compiler_params=_cp("parallel", "parallel", "arbitrary"),
        name="moe_dispatch",
    )(ends, h2, slotrow, gfull)


def _moe_ffn_kernel(xe_ref, gs_ref, mod_ref, wg_ref, wu_ref, wd_ref, ye_ref):
    e = pl.program_id(1)
    lane = lax.broadcasted_iota(jnp.int32, (1, 128), 1)
    pick = (lane == e) | (lane == e + N_EXPERTS) | (lane == e + 2 * N_EXPERTS)
    gate = jnp.sum(jnp.where(pick, gs_ref[0, 0].astype(F32), 0.0), axis=-1, keepdims=True)
    xb = xe_ref[0, 0]
    hg = _dot(xb, wg_ref[0, 0])
    hu = _dot(xb, wu_ref[0, 0])
    he = _bf(hg * _sigmoid(hg) * hu)
    y = _dot(he, wd_ref[0, 0])
    ye_ref[0, 0] = _bf(y * gate * mod_ref[0, 5:6, :])


def _moe_ffn(xe, gs, mod, wg, wu, wd, layer, per_group_mod):
    ngrp, _, ct, d = xe.shape
    f = wg.shape[3]
    mod_map = (lambda gi, e: (gi, 0, 0)) if per_group_mod else (lambda gi, e: (0, 0, 0))
    return pl.pallas_call(
        _moe_ffn_kernel,
        grid=(ngrp, N_EXPERTS),
        in_specs=[pl.BlockSpec((1, 1, ct, d), lambda gi, e: (gi, e, 0, 0)),
                  pl.BlockSpec((1, 1, ct, 128), lambda gi, e: (gi, e, 0, 0)),
                  pl.BlockSpec((1, 6, d), mod_map),
                  pl.BlockSpec((1, 1, d, f), lambda gi, e: (layer, e, 0, 0)),
                  pl.BlockSpec((1, 1, d, f), lambda gi, e: (layer, e, 0, 0)),
                  pl.BlockSpec((1, 1, f, d), lambda gi, e: (layer, e, 0, 0))],
        out_specs=pl.BlockSpec((1, 1, ct, d), lambda gi, e: (gi, e, 0, 0)),
        out_shape=jax.ShapeDtypeStruct((ngrp, N_EXPERTS, ct, d), BF16),
        compiler_params=_cp("parallel", "parallel"),
        name="moe_ffn",
    )(xe, gs, mod, wg, wu, wd)


def _moe_combine_kernel(ends_ref, x1_ref, slot_ref, ye_ref, o_ref, win_scr, *, ct, nch):
    gi = pl.program_id(0)
    j = pl.program_id(1)
    mt = 128
    wide = N_EXPERTS * mt
    shi, slo = _split2(slot_ref[...])
    col_e = lax.broadcasted_iota(jnp.int32, (N_EXPERTS, wide), 1) // mt
    row_e = lax.broadcasted_iota(jnp.int32, (N_EXPERTS, wide), 0)
    expand = _bf(jnp.where(col_e == row_e, 1.0, 0.0))
    sb = _dot(shi, expand) + _dot(slo, expand)
    lane = lax.broadcasted_iota(jnp.int32, (1, mt), 1)
    starts, his, targets = [], [], []
    for e in range(N_EXPERTS):
        base = (gi * N_EXPERTS + e) * nch
        lo = jnp.where(j == 0, 0, ends_ref[base + jnp.maximum(j - 1, 0)])
        his.append(ends_ref[base + j])
        start = pl.multiple_of(jnp.minimum((lo // 16) * 16, ct - mt), 16)
        win_scr[e * mt:(e + 1) * mt, :] = ye_ref[0, e, pl.ds(start, mt), :]
        starts.append(start)
        targets.append((lane + start).astype(F32))
    onehot = _bf(jnp.where(sb == jnp.concatenate(targets, axis=1), 1.0, 0.0))
    o_ref[...] = x1_ref[...] + _dot(onehot, win_scr[...])
    for e in range(N_EXPERTS):
        for w in range(1, ct // mt):
            wlo = starts[e] + w * mt

            @pl.when(wlo < his[e])
            def _(e=e, wlo=wlo):
                ws = pl.multiple_of(jnp.minimum(wlo, ct - mt), 16)
                sbe = sb[:, e * mt:(e + 1) * mt]
                hit = jnp.where(sbe == (lane + ws).astype(F32), 1.0, 0.0) * jnp.where(sbe >= wlo.astype(F32), 1.0, 0.0)
                o_ref[...] += _dot(_bf(hit), ye_ref[0, e, pl.ds(ws, mt), :])


def _moe_combine(ends, x1, slot, ye, lg, ct, kc):
    tokens, d = x1.shape
    ngrp = tokens // lg
    nch = lg // kc
    grid_spec = pltpu.PrefetchScalarGridSpec(
        num_scalar_prefetch=1,
        grid=(ngrp, nch),
        in_specs=[pl.BlockSpec((kc, d), lambda gi, j, ends: (gi * nch + j, 0)),
                  pl.BlockSpec((kc, N_EXPERTS), lambda gi, j, ends: (gi * nch + j, 0)),
                  pl.BlockSpec((1, N_EXPERTS, ct, d), lambda gi, j, ends: (gi, 0, 0, 0),
                               pipeline_mode=pl.Buffered(1))],
        out_specs=pl.BlockSpec((kc, d), lambda gi, j, ends: (gi * nch + j, 0)),
        scratch_shapes=[pltpu.VMEM((N_EXPERTS * 128, d), BF16)])
    return pl.pallas_call(
        functools.partial(_moe_combine_kernel, ct=ct, nch=nch),
        grid_spec=grid_spec,
        out_shape=jax.ShapeDtypeStruct((tokens, d), F32),
        compiler_params=_cp("parallel", "arbitrary"),
        name="moe_combine",
    )(ends, x1, slot, ye)


def _rope_tables(seq):
    t = np.arange(seq)
    n_freq = HEAD_DIM // 4
    inv = ROPE_THETA ** (-np.arange(n_freq, dtype=np.float32) / n_freq)
    ang = np.concatenate([(t // GRID_W).astype(np.float32)[:, None] * inv,
                          (t % GRID_W).astype(np.float32)[:, None] * inv], axis=-1)
    ang = jnp.asarray(ang, dtype=F32)
    cos, sin = jnp.cos(ang), jnp.sin(ang)
    cos_t = jnp.tile(jnp.concatenate([cos, cos], axis=-1), (1, SWA_HEADS))
    sin_t = jnp.tile(jnp.concatenate([-sin, sin], axis=-1), (1, SWA_HEADS))
    return cos_t, sin_t


def _blockdiag2(w):
    z = jnp.zeros_like(w[0])
    return jnp.concatenate([jnp.concatenate([w[0], z], axis=1), jnp.concatenate([z, w[1]], axis=1)], axis=0)


def _layer_params(l, ada_w, ada_b, norm1_g, norm2_g, w_in, na_q_norm, na_k_norm, na_rpb, rw_mu, rw_w0, rw_w2,
                  rw_a0, rw_a2, rw_g2, rw_k_k, rw_k_a, rw_r_k, rw_ln_g, rw_ln_b, swa_q_norm, swa_k_norm,
                  swa_sink, w_out_bf, w_router, w_gate, w_up, w_down, w_in_bf):
    wr = w_router[l]
    wr_hi = wr.astype(BF16)
    w_rw = w_in[l][:, 3 * NA_W:3 * NA_W + RW_IN_W]
    return {
        'n1': norm1_g[l][None], 'n2': norm2_g[l][None], 'w_in': w_in_bf,
        'w_rw_lo': (w_rw - w_rw.astype(BF16).astype(F32)).astype(BF16),
        'gains': (jnp.tile(na_q_norm[l], NA_HEADS)[None], jnp.tile(na_k_norm[l], NA_HEADS)[None],
                  jnp.tile(swa_q_norm[l], SWA_HEADS)[None], jnp.tile(swa_k_norm[l], SWA_KV_HEADS)[None]),
        'bias': _na_bias_table(na_rpb[l]),
        'mu': rw_mu[l], 'w0': rw_w0[l].reshape(1, 2 * RW_W), 'w2': _blockdiag2(rw_w2[l]).astype(BF16),
        'a0': rw_a0[l].reshape(1, 2 * RW_W), 'a2': _blockdiag2(rw_a2[l]).astype(BF16),
        'g2': rw_g2[l].astype(BF16), 'k_k': rw_k_k[l][None], 'k_a': rw_k_a[l][None],
        'r_k': rw_r_k[l].reshape(1, RW_W), 'ln_g': rw_ln_g[l][None], 'ln_b': rw_ln_b[l][None],
        'sink': swa_sink[l], 'w_out': w_out_bf,
        'wr_hi': wr_hi, 'wr_lo': (wr - wr_hi.astype(F32)).astype(BF16),
        'wg': w_gate, 'wu': w_up, 'wd': w_down, 'layer': l,
        'ones256': _block_ones(RW_W),
    }


def _mix_and_ffn(x, mod, p, oa, oc, urw, s0_bd, scan_consts, seq, per_request_mod, group):
    r, kk, v, ld, kka, kd, g, bonus = _rw_pre(urw, p, seq)
    o2, sfin = _rw_scan(r, kk, v, ld, kka, kd, s0_bd, scan_consts, seq)
    x1, h2, aff = _finish(x, oa, oc, o2, bonus, g, mod, p, seq, per_request_mod)
    slot, slotrow, gfull, ends, kc = _topk(aff, seq, group)
    cap = EC_CAPACITY * seq // N_EXPERTS
    lg, ct = group * seq, group * cap
    xe, gs = _moe_dispatch(ends, h2, slotrow, gfull, lg, ct, kc)
    ye = _moe_ffn(xe, gs, mod, p['wg'], p['wu'], p['wd'], p['layer'], per_request_mod)
    return _moe_combine(ends, x1, slot, ye, lg, ct, kc), sfin


def _context_layer(x, mod, p, ones384, scan_consts, seq):
    qa, ka, va, urw, qc, kc, vc = _proj(x, mod, p['n1'], p['w_in'], p['layer'], p['gains'], ones384, None, seq,
                                        False, F32, w_rw_lo=p['w_rw_lo'])
    oa, oc = _ctx_attn(p['sink'], qa, ka, va, qc, kc, vc, seq)
    nreq = x.shape[0] // seq
    s0 = jnp.zeros((nreq, 2, HEAD_DIM, RW_W), F32)
    y, sfin = _mix_and_ffn(x, mod, p, oa, oc, urw, s0, scan_consts, seq, False, CTX_GROUP)
    return y, ka, va, kc, vc, sfin


def _latent_layer(x, mod, p, ones384, scan_consts, rope_tabs, seq, kx_na, vx_na, kx_swa, vx_swa, s0_bd, past):
    qa, ka, va, urw, qc, kc, vc = _proj(x, mod, p['n1'], p['w_in'], p['layer'], p['gains'], ones384, rope_tabs,
                                        seq, True, BF16)
    oa = _na_attn(qa, ka, va, kx_na, vx_na, p['bias'], seq, past)
    oc = _swa_attn(p['sink'], qc, kc, vc, kx_swa, vx_swa, seq, past)
    y, _ = _mix_and_ffn(x, mod, p, oa, oc, urw, s0_bd, scan_consts, seq, True, 1)
    return y


def _cache_layout(zs, nreq, seq, heads):
    z = jnp.stack(zs, axis=0).reshape(len(zs), nreq, seq, heads, HEAD_DIM)
    return z.transpose(1, 0, 3, 2, 4)


def _tokens_first(z):
    b, nl, h, n, dh = z.shape
    return z.transpose(1, 0, 3, 2, 4).reshape(nl, b * n, h * dh)


def kernel(x_prompt, x_sample, cache_na_k, cache_na_v, cache_swa_k, cache_swa_v, state_rwkv, c, c_ctx, ada_w, ada_b, norm1_g, norm2_g, w_in, na_q_norm, na_k_norm, na_rpb, rw_mu, rw_w0, rw_w2, rw_a0, rw_a2, rw_g2, rw_k_k, rw_k_a, rw_r_k, rw_ln_g, rw_ln_b, swa_q_norm, swa_k_norm, swa_sink, w_out, w_router, w_gate, w_up, w_down):
    nb, seq, d = x_prompt.shape
    db, dseq, _ = x_sample.shape
    depth = ada_w.shape[0]
    past = cache_na_k.shape[3]
    cond = jnp.concatenate([c, c_ctx[None], jnp.zeros((16 - db - 1, d), F32)], axis=0)
    mod_all = _adaln(cond, ada_w, ada_b).reshape(depth, 16, 6, d)
    ones384 = _block_ones(NA_W)
    scan_consts = _rw_masks()
    rope_tabs = _rope_tables(dseq)
    xp = x_prompt.reshape(nb * seq, d)
    xs = x_sample.reshape(db * dseq, d)
    new_ka, new_va, new_kc, new_vc, new_s = [], [], [], [], []
    wg_bf, wu_bf, wd_bf = w_gate.astype(BF16), w_up.astype(BF16), w_down.astype(BF16)
    w_in_bf, w_out_bf = w_in.astype(BF16), w_out.astype(BF16)
    kx_na, vx_na = _tokens_first(cache_na_k), _tokens_first(cache_na_v)
    kx_swa, vx_swa = _tokens_first(cache_swa_k), _tokens_first(cache_swa_v)
    s0_lat = jnp.transpose(state_rwkv, (1, 0, 2, 4, 3, 5)).reshape(depth, db, 2, HEAD_DIM, RW_W)
    for l in range(depth):
        p = _layer_params(l, ada_w, ada_b, norm1_g, norm2_g, w_in, na_q_norm, na_k_norm, na_rpb, rw_mu, rw_w0,
                          rw_w2, rw_a0, rw_a2, rw_g2, rw_k_k, rw_k_a, rw_r_k, rw_ln_g, rw_ln_b, swa_q_norm,
                          swa_k_norm, swa_sink, w_out_bf, w_router, wg_bf, wu_bf, wd_bf, w_in_bf)
        mod_ctx = mod_all[l, db:db + 1]
        mod_lat = mod_all[l, 0:db]
        xp, ka, va, kc, vc, sfin = _context_layer(xp, mod_ctx, p, ones384, scan_consts, seq)
        new_ka.append(ka)
        new_va.append(va)
        new_kc.append(kc)
        new_vc.append(vc)
        new_s.append(sfin)
        xs = _latent_layer(xs, mod_lat, p, ones384, scan_consts, rope_tabs, dseq,
                           kx_na[l], vx_na[l], kx_swa[l], vx_swa[l], s0_lat[l], past)
    states = jnp.stack(new_s, axis=0).reshape(depth, nb, 2, HEAD_DIM, RW_HEADS, HEAD_DIM)
    return (xp.reshape(nb, seq, d), xs.reshape(db, dseq, d),
            _cache_layout(new_ka, nb, seq, NA_HEADS), _cache_layout(new_va, nb, seq, NA_HEADS),
            _cache_layout(new_kc, nb, seq, SWA_KV_HEADS), _cache_layout(new_vc, nb, seq, SWA_KV_HEADS),
            jnp.transpose(states, (1, 0, 2, 4, 3, 5)))
```

```python
import functools

import numpy as np
import jax
import jax.numpy as jnp
from jax import lax
from jax.experimental import pallas as pl
from jax.experimental.pallas import tpu as pltpu

F32 = jnp.float32
BF16 = jnp.bfloat16

HEAD_DIM = 64
GRID_W = 64
NA_HEADS = 6
NA_KH = 8
NA_KW = 16
RW_HEADS = 4
SWA_HEADS = 6
SWA_KV_HEADS = 2
SWA_WINDOW = 128
N_EXPERTS = 16
EC_CAPACITY = 2
ROPE_THETA = 10000.0
NORM_EPS = 1e-6
GN_EPS = 64e-5
NEG_INF = -1e30
SUB_ROWS = 256
NA_UNROLL = 8
SWA_UNROLL = 2
RW_CHUNK = 64
assert RW_CHUNK == HEAD_DIM
CTX_GROUP = 16
RW_W = RW_HEADS * HEAD_DIM
NA_W = NA_HEADS * HEAD_DIM
SWA_W = SWA_HEADS * HEAD_DIM
SWA_KV_W = SWA_KV_HEADS * HEAD_DIM
RW_IN_W = 1152
VMEM_LIMIT = 56 * 1024 * 1024


def _cp(*sem):
    return pltpu.CompilerParams(dimension_semantics=sem, vmem_limit_bytes=VMEM_LIMIT)


def _bf(x):
    return x.astype(BF16)


def _dot(a, b):
    return jnp.dot(a, b, preferred_element_type=F32)


def _dot_nt(a, b):
    return lax.dot_general(a, b, (((1,), (1,)), ((), ())), preferred_element_type=F32)


def _dot_tn(a, b):
    return lax.dot_general(a, b, (((0,), (0,)), ((), ())), preferred_element_type=F32)


def _split2(x):
    hi = x.astype(BF16)
    lo = (x - hi.astype(F32)).astype(BF16)
    return hi, lo


def _split3(x):
    hi = x.astype(BF16)
    r1 = x - hi.astype(F32)
    mid = r1.astype(BF16)
    lo = (r1 - mid.astype(F32)).astype(BF16)
    return hi, mid, lo


def _dot2(a, b_bf):
    hi, lo = _split2(a)
    return _dot(hi, b_bf) + _dot(lo, b_bf)


def _sigmoid(x):
    return 1.0 / (1.0 + jnp.exp(-x))


def _block_ones(width):
    i = np.arange(width) // HEAD_DIM
    return jnp.asarray((i[:, None] == i[None, :]).astype(np.float32), dtype=BF16)


def _row_tile(seq):
    return 512 if seq % 512 == 0 else 256


def _full(shape):
    return pl.BlockSpec(shape, lambda *_: (0,) * len(shape))


def _adaln_kernel(c_ref, w_ref, b_ref, o_ref):
    c = c_ref[...]
    s = c * _sigmoid(c)
    shi, slo = _split2(s)
    whi, wlo = _split2(w_ref[0])
    o_ref[0] = _dot(shi, whi) + _dot(slo, whi) + _dot(shi, wlo) + b_ref[0]


def _adaln(cond, ada_w, ada_b):
    nl, d, n6 = ada_w.shape
    tn = 1536
    rows = cond.shape[0]
    return pl.pallas_call(
        _adaln_kernel,
        grid=(nl, n6 // tn),
        in_specs=[pl.BlockSpec((rows, d), lambda l, j: (0, 0)),
                  pl.BlockSpec((1, d, tn), lambda l, j: (l, 0, j)),
                  pl.BlockSpec((1, 1, tn), lambda l, j: (l, 0, j))],
        out_specs=pl.BlockSpec((1, rows, tn), lambda l, j: (l, 0, j)),
        out_shape=jax.ShapeDtypeStruct((nl, rows, n6), F32),
        compiler_params=_cp("parallel", "parallel"),
        name="adaln",
    )(cond, ada_w, ada_b.reshape(nl, 1, n6))


def _head_sums(zz, ones128):
    zz = _bf(zz)
    parts = [_dot(zz[:, i:i + 128], ones128) for i in range(0, zz.shape[1], 128)]
    return jnp.concatenate(parts, axis=1) if len(parts) > 1 else parts[0]


def _head_norm(z, gain, ones128):
    ms = _head_sums(z * z, ones128) * (1.0 / HEAD_DIM)
    return z * lax.rsqrt(ms + NORM_EPS) * gain


def _rope(z, cos, sin_signed):
    w = z.shape[1]
    lane = lax.broadcasted_iota(jnp.int32, z.shape, 1)
    first = (lane % HEAD_DIM) < (HEAD_DIM // 2)
    swapped = jnp.where(first, pltpu.roll(z, w - HEAD_DIM // 2, 1), pltpu.roll(z, HEAD_DIM // 2, 1))
    return z * cos + swapped * sin_signed


def _proj_kernel(*refs, rope, split_rw):
    refs = list(refs)
    x_ref, mod_ref, n1_ref, w_ref, gqa_ref, gka_ref, gqc_ref, gkc_ref, ones_ref = refs[:9]
    qa_ref, ka_ref, va_ref, urw_ref, qc_ref, kc_ref, vc_ref = refs[-7:]
    extra = refs[9:-7]
    if rope:
        cos_ref, sin_ref = extra[:2]
    sh1 = mod_ref[0, 0:1, :]
    sc1 = mod_ref[0, 1:2, :]
    o0 = 0
    o1 = NA_W
    o2 = 2 * NA_W
    o3 = 3 * NA_W
    o4 = o3 + RW_IN_W
    o5 = o4 + SWA_W
    o6 = o5 + SWA_KV_W
    ones = ones_ref[0:128, 0:128]
    ones_kv = ones
    subs = [slice(i, i + SUB_ROWS) for i in range(0, x_ref.shape[0], SUB_ROWS)]
    hs, us = [], []
    for sl in subs:
        x = x_ref[sl, :]
        ms = jnp.mean(x * x, axis=-1, keepdims=True)
        hs.append(x * lax.rsqrt(ms + NORM_EPS) * n1_ref[...] * (1.0 + sc1) + sh1)
    h_hi = [_bf(h) for h in hs]
    us = [_dot(hh, w_ref[...]) for hh in h_hi]
    for sl, h, hh, u in zip(subs, hs, h_hi, us):
        qa = _head_norm(u[:, o0:o1], gqa_ref[...], ones)
        ka = _head_norm(u[:, o1:o2], gka_ref[...], ones)
        qc = _head_norm(u[:, o4:o5], gqc_ref[...], ones)
        kc = _head_norm(u[:, o5:o6], gkc_ref[...], ones_kv)
        if rope:
            qc = _rope(qc, cos_ref[sl, :], sin_ref[sl, :])
            kc = _rope(kc, cos_ref[sl, 0:SWA_KV_W], sin_ref[sl, 0:SWA_KV_W])
        qa_ref[sl, :] = qa.astype(qa_ref.dtype)
        ka_ref[sl, :] = ka.astype(ka_ref.dtype)
        va_ref[sl, :] = u[:, o2:o3].astype(va_ref.dtype)
        urw = u[:, o3:o4]
        if split_rw:
            h_lo = _bf(h - hh.astype(F32))
            urw = urw + _dot(h_lo, w_ref[:, o3:o4]) + _dot(hh, extra[-1][...])
        urw_ref[sl, :] = urw
        qc_ref[sl, :] = qc.astype(qc_ref.dtype)
        kc_ref[sl, :] = kc.astype(kc_ref.dtype)
        vc_ref[sl, :] = u[:, o6:].astype(vc_ref.dtype)


def _proj(x, mod, n1, w_in_bf, gains, ones384, rope_tabs, seq, per_request_mod, qkv_dtype, w_rw_lo=None):
    tokens, d = x.shape
    tm = _row_tile(seq)
    tiles_per_req = seq // tm
    in_w = w_in_bf.shape[1]
    rope = rope_tabs is not None
    mod_map = (lambda i: (i // tiles_per_req, 0, 0)) if per_request_mod else (lambda i: (0, 0, 0))
    row = lambda w: pl.BlockSpec((tm, w), lambda i: (i, 0))
    in_specs = [row(d), pl.BlockSpec((1, 6, d), mod_map), _full((1, d)), _full((d, in_w)),
                _full((1, NA_W)), _full((1, NA_W)), _full((1, SWA_W)), _full((1, SWA_KV_W)),
                _full((NA_W, NA_W))]
    args = [x, mod, n1, w_in_bf, *gains, ones384]
    if rope:
        tab = pl.BlockSpec((tm, SWA_W), lambda i: (i % tiles_per_req, 0))
        in_specs += [tab, tab]
        args += list(rope_tabs)
    if w_rw_lo is not None:
        in_specs.append(_full((d, RW_IN_W)))
        args.append(w_rw_lo)
    widths = [NA_W, NA_W, NA_W, RW_IN_W, SWA_W, SWA_KV_W, SWA_KV_W]
    dtypes = [qkv_dtype, qkv_dtype, qkv_dtype, F32, qkv_dtype, qkv_dtype, qkv_dtype]
    return pl.pallas_call(
        functools.partial(_proj_kernel, rope=rope, split_rw=w_rw_lo is not None),
        grid=(tokens // tm,),
        in_specs=in_specs,
        out_specs=[row(w) for w in widths],
        out_shape=[jax.ShapeDtypeStruct((tokens, w), dt) for w, dt in zip(widths, dtypes)],
        compiler_params=_cp("parallel"),
        name="proj",
    )(*args)


def _half_masks(width=2 * HEAD_DIM):
    lane = lax.broadcasted_iota(jnp.int32, (1, width), 1)
    return lane < HEAD_DIM, lane >= HEAD_DIM


def _swap_halves(z):
    return pltpu.roll(z, HEAD_DIM, 1)


def _ctx_attn_kernel(sink_ref, qa_ref, ka_ref, va_ref, qc_ref, kc_ref, vc_ref, oa_ref, oc_ref):
    scale = HEAD_DIM ** -0.5
    m0, m1 = _half_masks()
    masks = (m0, m1)
    for pair in range(NA_HEADS // 2):
        sl = slice(pair * 128, (pair + 1) * 128)
        qp = qa_ref[:, sl].astype(F32) * scale
        kp = _bf(ka_ref[:, sl])
        vp = _bf(va_ref[:, sl])
        outs = []
        for half in range(2):
            qm = _bf(jnp.where(masks[half], qp, 0.0))
            s = _dot_nt(qm, kp)
            m = jnp.max(s, axis=-1, keepdims=True)
            e = jnp.exp(s - m)
            l = jnp.sum(e, axis=-1, keepdims=True)
            outs.append(_dot(_bf(e), vp) / l)
        oa_ref[:, sl] = jnp.where(m0, outs[0], outs[1])
    kc = _bf(kc_ref[...])
    vc = _bf(vc_ref[...])
    group = SWA_HEADS // SWA_KV_HEADS
    for pair in range(SWA_HEADS // 2):
        sl = slice(pair * 128, (pair + 1) * 128)
        qp = qc_ref[:, sl].astype(F32) * scale
        outs = []
        for half in range(2):
            h = 2 * pair + half
            g = h // group
            qh = qp if g == half else _swap_halves(qp)
            qm = _bf(jnp.where(masks[g], qh, 0.0))
            s = _dot_nt(qm, kc)
            sk = sink_ref[h]
            m = jnp.maximum(jnp.max(s, axis=-1, keepdims=True), sk)
            e = jnp.exp(s - m)
            l = jnp.sum(e, axis=-1, keepdims=True) + jnp.exp(sk - m)
            o = _dot(_bf(e), vc) / l
            outs.append(o if g == half else _swap_halves(o))
        oc_ref[:, sl] = jnp.where(m0, outs[0], outs[1])


def _ctx_attn(sink, qa, ka, va, qc, kc, vc, seq):
    tokens = qa.shape[0]
    blk = lambda w: pl.BlockSpec((seq, w), lambda b: (b, 0))
    return pl.pallas_call(
        _ctx_attn_kernel,
        grid=(tokens // seq,),
        in_specs=[pl.BlockSpec(memory_space=pltpu.SMEM), blk(NA_W), blk(NA_W), blk(NA_W), blk(SWA_W),
                  blk(SWA_KV_W), blk(SWA_KV_W)],
        out_specs=[blk(NA_W), blk(SWA_W)],
        out_shape=[jax.ShapeDtypeStruct((tokens, NA_W), F32), jax.ShapeDtypeStruct((tokens, SWA_W), F32)],
        compiler_params=_cp("parallel"),
        name="ctx_attn",
    )(sink, qa, ka, va, qc, kc, vc)


def _na_bias_kernel(rpb_ref, o_ref):
    h = pl.program_id(0)
    nrow = 2 * NA_KH - 1
    ncol = 2 * NA_KW - 1
    width = NA_KH * GRID_W
    shape = (GRID_W, width)
    lane = lax.broadcasted_iota(jnp.int32, shape, 1)
    qc = lax.broadcasted_iota(jnp.int32, shape, 0)
    kc = lane % GRID_W
    c_start = jnp.clip(qc - NA_KW // 2, 0, GRID_W - NA_KW)
    ok = (kc >= c_start) & (kc < c_start + NA_KW)
    d_col = jnp.clip(kc - qc, 1 - NA_KW, NA_KW - 1) + NA_KW - 1
    key_row = lax.broadcasted_iota(jnp.int32, (1, width), 1) // GRID_W

    def case_body(case, carry):
        acc = jnp.zeros(shape, F32)
        for dc in range(ncol):
            val = jnp.zeros((1, width), F32)
            for i in range(NA_KH):
                val = jnp.where(key_row == i, rpb_ref[(h * nrow + case + i) * ncol + dc], val)
            acc = jnp.where(d_col == dc, val, acc)
        o_ref[0, pl.ds(case, 1)] = jnp.where(ok, acc, NEG_INF)[None]
        return carry

    lax.fori_loop(0, NA_KH, case_body, 0)


def _na_bias_table(rpb):
    nh = rpb.shape[0]
    return pl.pallas_call(
        _na_bias_kernel,
        grid=(nh,),
        in_specs=[pl.BlockSpec(memory_space=pltpu.SMEM)],
        out_specs=pl.BlockSpec((1, NA_KH, GRID_W, NA_KH * GRID_W), lambda h: (h // 2, 0, h % 2, 0)),
        out_shape=jax.ShapeDtypeStruct((nh // 2, NA_KH, 2 * GRID_W, NA_KH * GRID_W), F32),
        compiler_params=_cp("parallel"),
        name="na_bias",
    )(rpb.reshape(-1))


def _na_kernel(q_ref, k_ref, v_ref, kx_ref, vx_ref, bias_ref, o_ref, *, rows):
    scale = HEAD_DIM ** -0.5
    m0, m1 = _half_masks()
    kx = _bf(kx_ref[...])
    vx = _bf(vx_ref[...])

    def body(it, carry):
        us = range(NA_UNROLL)
        r = [it * NA_UNROLL + u for u in us]
        rs = [jnp.clip(r[u] - NA_KH // 2, 0, rows - NA_KH) for u in us]
        case = [rs[u] - r[u] + NA_KH - 1 for u in us]
        q0 = [pl.multiple_of(r[u] * GRID_W, GRID_W) for u in us]
        k0 = [pl.multiple_of(rs[u] * GRID_W, GRID_W) for u in us]
        qp = [q_ref[pl.ds(q0[u], GRID_W), :].astype(F32) * scale for u in us]
        kw = [_bf(k_ref[pl.ds(k0[u], NA_KH * GRID_W), :]) for u in us]
        vw = [_bf(v_ref[pl.ds(k0[u], NA_KH * GRID_W), :]) for u in us]
        q2 = [_bf(jnp.concatenate([jnp.where(m0, qp[u], 0.0), jnp.where(m1, qp[u], 0.0)], axis=0)) for u in us]
        s = [jnp.concatenate([_dot_nt(q2[u], kw[u]) + bias_ref[0, pl.ds(case[u], 1)][0], _dot_nt(q2[u], kx)],
                             axis=1) for u in us]
        m = [jnp.max(s[u], axis=-1, keepdims=True) for u in us]
        e = [jnp.exp(s[u] - m[u]) for u in us]
        l = [jnp.sum(e[u], axis=-1, keepdims=True) for u in us]
        o = [_dot(_bf(e[u]), jnp.concatenate([vw[u], vx], axis=0)) / l[u] for u in us]
        for u in us:
            o_ref[pl.ds(q0[u], GRID_W), :] = jnp.where(m0, o[u][0:GRID_W], o[u][GRID_W:])
        return carry

    lax.fori_loop(0, rows // NA_UNROLL, body, 0)


def _na_attn(q, k, v, kx, vx, bias, seq, past):
    tokens = q.shape[0]
    nb = tokens // seq
    rows = seq // GRID_W
    blk = pl.BlockSpec((seq, 128), lambda b, p: (b, p))
    cblk = pl.BlockSpec((past, 128), lambda b, p: (b, p))
    return pl.pallas_call(
        functools.partial(_na_kernel, rows=rows),
        grid=(nb, NA_HEADS // 2),
        in_specs=[blk, blk, blk, cblk, cblk,
                  pl.BlockSpec((1, NA_KH, 2 * GRID_W, NA_KH * GRID_W), lambda b, p: (p, 0, 0, 0))],
        out_specs=blk,
        out_shape=jax.ShapeDtypeStruct((tokens, NA_W), F32),
        compiler_params=_cp("parallel", "parallel"),
        name="na_attn",
    )(q, k, v, kx, vx, bias)


def _swa_kernel(sink_ref, q_ref, k_ref, v_ref, kx_ref, vx_ref, o_ref, *, seq):
    scale = HEAD_DIM ** -0.5
    blk = SWA_WINDOW
    m0, m1 = _half_masks()
    masks = (m0, m1)
    kx = _bf(kx_ref[...])
    vx = _bf(vx_ref[...])
    group = SWA_HEADS // SWA_KV_HEADS

    sk = []
    for g in range(SWA_KV_HEADS):
        sk.append(jnp.concatenate([jnp.full((blk, 1), sink_ref[h], F32) for h in range(g * group, (g + 1) * group)],
                                  axis=0))

    def body(it, carry):
        us = range(SWA_UNROLL)
        cs = [(u, g) for u in us for g in range(SWA_KV_HEADS)]
        nb = [it * SWA_UNROLL + u for u in us]
        ks = [pl.multiple_of(jnp.clip((nb[u] - 1) * blk, 0, seq - 3 * blk), blk) for u in us]
        q0 = [pl.multiple_of(nb[u] * blk, blk) for u in us]
        kw = [_bf(k_ref[pl.ds(ks[u], 3 * blk), :]) for u in us]
        vw = [_bf(v_ref[pl.ds(ks[u], 3 * blk), :]) for u in us]
        ok = []
        for u in us:
            qpos = q0[u] + lax.broadcasted_iota(jnp.int32, (group * blk, 1), 0) % blk
            kpos = ks[u] + lax.broadcasted_iota(jnp.int32, (1, 3 * blk), 1)
            ok.append(jnp.abs(qpos - kpos) <= SWA_WINDOW)
        qg = {}
        for u in us:
            pairs = [q_ref[pl.ds(q0[u], blk), p * 128:(p + 1) * 128].astype(F32) * scale
                     for p in range(SWA_HEADS // 2)]
            for g in range(SWA_KV_HEADS):
                qs = []
                for h in range(g * group, (g + 1) * group):
                    qh = pairs[h // 2] if h % 2 == g else _swap_halves(pairs[h // 2])
                    qs.append(jnp.where(masks[g], qh, 0.0))
                qg[u, g] = _bf(jnp.concatenate(qs, axis=0))
        sw = {c: jnp.where(ok[c[0]], _dot_nt(qg[c], kw[c[0]]), NEG_INF) for c in cs}
        sx = {c: _dot_nt(qg[c], kx) for c in cs}
        hs = [c + (i,) for c in cs for i in range(group)]
        rs = {k: slice(k[2] * blk, (k[2] + 1) * blk) for k in hs}
        swi = {k: sw[k[:2]][rs[k]] for k in hs}
        sxi = {k: sx[k[:2]][rs[k]] for k in hs}
        ski = {k: sk[k[1]][rs[k]] for k in hs}
        mi = {k: jnp.maximum(jnp.maximum(jnp.max(swi[k], axis=-1, keepdims=True),
                                         jnp.max(sxi[k], axis=-1, keepdims=True)), ski[k]) for k in hs}
        ewi = {k: jnp.exp(swi[k] - mi[k]) for k in hs}
        exi = {k: jnp.exp(sxi[k] - mi[k]) for k in hs}
        li = {k: jnp.sum(ewi[k], axis=-1, keepdims=True) + jnp.sum(exi[k], axis=-1, keepdims=True)
              + jnp.exp(ski[k] - mi[k]) for k in hs}
        o = {k: (_dot(_bf(ewi[k]), vw[k[0]]) + _dot(_bf(exi[k]), vx)) / li[k] for k in hs}
        for u in us:
            head_out = []
            for g in range(SWA_KV_HEADS):
                for i in range(group):
                    h = g * group + i
                    oh = o[u, g, i]
                    head_out.append(oh if h % 2 == g else _swap_halves(oh))
            for p in range(SWA_HEADS // 2):
                o_ref[pl.ds(q0[u], blk), p * 128:(p + 1) * 128] = jnp.where(m0, head_out[2 * p], head_out[2 * p + 1])
        return carry

    lax.fori_loop(0, seq // (blk * SWA_UNROLL), body, 0)


def _swa_attn(sink, q, k, v, kx, vx, seq, past):
    tokens = q.shape[0]
    blk = lambda w: pl.BlockSpec((seq, w), lambda b: (b, 0))
    cblk = pl.BlockSpec((past, SWA_KV_W), lambda b: (b, 0))
    return pl.pallas_call(
        functools.partial(_swa_kernel, seq=seq),
        grid=(tokens // seq,),
        in_specs=[pl.BlockSpec(memory_space=pltpu.SMEM), blk(SWA_W), blk(SWA_KV_W), blk(SWA_KV_W), cblk, cblk],
        out_specs=blk(SWA_W),
        out_shape=jax.ShapeDtypeStruct((tokens, SWA_W), F32),
        compiler_params=_cp("parallel"),
        name="swa_attn",
    )(sink, q, k, v, kx, vx)


def _rw_pre_kernel(u_ref, up_ref, un_ref, mu_ref, w0_ref, w2_ref, a0_ref, a2_ref, g2_ref, kk_ref_, ka_ref_,
                   rk_ref, ones_ref, r_o, kk_o, v_o, ld_o, kka_o, kd_o, g_o, bonus_o, *, tiles_per_req):
    i = pl.program_id(0)
    u = u_ref[...]
    tm = u.shape[0]
    rowi = lax.broadcasted_iota(jnp.int32, (tm, 1), 0)
    first = (i % tiles_per_req) == 0
    last = (i % tiles_per_req) == tiles_per_req - 1
    prev_row = jnp.where(first, 0.0, up_ref[7:8, :])
    next_row = jnp.where(last, 0.0, un_ref[0:1, :])
    prev = jnp.where(rowi == 0, prev_row, pltpu.roll(u, 1, 0))
    nxt = jnp.where(rowi == tm - 1, next_row, pltpu.roll(u, tm - 1, 0))
    us = u + mu_ref[0:1, :] * (prev - u) + mu_ref[1:2, :] * (nxt - u)
    r = us[:, 0:RW_W]
    k = us[:, RW_W:2 * RW_W]
    v = us[:, 2 * RW_W:3 * RW_W]
    wl = us[:, 3 * RW_W:3 * RW_W + 128]
    al = us[:, 3 * RW_W + 128:3 * RW_W + 256]
    gl = us[:, 3 * RW_W + 256:3 * RW_W + 384]
    z = -(w0_ref[...] + _dot(_bf(jnp.tanh(wl)), w2_ref[...]))
    softplus = jnp.maximum(z, 0.0) + jnp.log(1.0 + jnp.exp(-jnp.abs(z)))
    w = -softplus - 0.5
    ld = -jnp.exp(w)
    a = _sigmoid(a0_ref[...] + _dot(_bf(al), a2_ref[...]))
    g = _dot(_bf(_sigmoid(gl)), g2_ref[...])
    ones = ones_ref[...]
    kkr = k * kk_ref_[...]
    kk = kkr * lax.rsqrt(jnp.maximum(_dot2(kkr * kkr, ones), 1e-24))
    k_a = ka_ref_[...]
    kd_f = k * (1.0 + (a[:, 0:RW_W] - 1.0) * k_a)
    kd_b = k * (1.0 + (a[:, RW_W:] - 1.0) * k_a)
    r_o[...] = r
    kk_o[...] = kk
    v_o[...] = v
    ld_o[...] = ld
    kka_o[:, 0:RW_W] = kk * a[:, 0:RW_W]
    kka_o[:, RW_W:] = kk * a[:, RW_W:]
    kd_o[:, 0:RW_W] = kd_f
    kd_o[:, RW_W:] = kd_b
    g_o[...] = g
    bonus_o[...] = _dot2(r * (kd_f + kd_b) * rk_ref[...], ones) * v


def _rw_pre(urw, p, seq):
    tokens = urw.shape[0]
    tm = _row_tile(seq)
    tpr = seq // tm
    nt = tokens // tm
    r8 = tm // 8
    row = lambda w: pl.BlockSpec((tm, w), lambda i: (i, 0))
    in_specs = [row(RW_IN_W),
                pl.BlockSpec((8, RW_IN_W), lambda i: (jnp.maximum(i * r8 - 1, 0), 0)),
                pl.BlockSpec((8, RW_IN_W), lambda i: (jnp.minimum((i + 1) * r8, nt * r8 - 1), 0)),
                _full((2, RW_IN_W)), _full((1, 2 * RW_W)), _full((128, 2 * RW_W)), _full((1, 2 * RW_W)),
                _full((128, 2 * RW_W)), _full((128, RW_W)), _full((1, RW_W)), _full((1, RW_W)),
                _full((1, RW_W)), _full((RW_W, RW_W))]
    widths = [RW_W, RW_W, RW_W, 2 * RW_W, 2 * RW_W, 2 * RW_W, RW_W, RW_W]
    return pl.pallas_call(
        functools.partial(_rw_pre_kernel, tiles_per_req=tpr),
        grid=(nt,),
        in_specs=in_specs,
        out_specs=[row(w) for w in widths],
        out_shape=[jax.ShapeDtypeStruct((tokens, w), F32) for w in widths],
        compiler_params=_cp("parallel"),
        name="rw_pre",
    )(urw, urw, urw, p['mu'], p['w0'], p['w2'], p['a0'], p['a2'], p['g2'], p['k_k'], p['k_a'], p['r_k'],
      p['ones256'])


def _rw_masks():
    t = RW_CHUNK
    n = RW_HEADS * t
    tt = np.arange(t)[:, None]
    ss = (np.arange(n) % t)[None, :]
    before = np.stack([ss < tt, ss > tt])
    diag = (ss == tt)
    strict = before.astype(np.float32)
    incl = (before | diag[None]).astype(np.float32)
    eye = diag.astype(np.float32)
    ti = np.arange(t)
    tri = np.stack([ti[None, :] <= ti[:, None], ti[None, :] >= ti[:, None]]).astype(np.float32)
    hd = np.arange(n) // t
    same = (hd[:, None] == hd[None, :]).astype(np.float32)
    return (jnp.asarray(strict), jnp.asarray(incl), jnp.asarray(tri, dtype=BF16), jnp.asarray(same),
            jnp.asarray(eye))


def _rw_scan_kernel(r_ref, kk_ref, v_ref, ld_ref, kka_ref, kd_ref, s0_ref, strict_ref, incl_ref, tri_ref,
                    same_ref, eye_ref, o_ref, sfin_ref, s_scr, *, nsub):
    d = pl.program_id(1)
    c = pl.program_id(2)
    t = RW_CHUNK
    n = RW_HEADS * t

    @pl.when(c == 0)
    def _():
        s_scr[...] = s0_ref[0, 0]

    strict = strict_ref[0]
    incl = incl_ref[0]
    tri = tri_ref[0]
    eye = eye_ref[...]
    same = same_ref[...]
    same_bf = _bf(same)

    def bd(x):
        return jnp.concatenate([_bf(x)] * RW_HEADS, axis=0) * same_bf

    js = range(nsub)
    rows = [pl.ds(pl.multiple_of((j + d * (nsub - 1 - 2 * j)) * t, t), t) for j in js]
    ld = [ld_ref[rows[j], :] for j in js]
    cum = []
    for j in js:
        lhi, lmid, llo = _split3(ld[j])
        cum.append(_dot(tri, lhi) + _dot(tri, lmid) + _dot(tri, llo))
    cend = [jnp.sum(ld[j], axis=0, keepdims=True) for j in js]
    kka = [kka_ref[rows[j], :] for j in js]
    kd = [kd_ref[rows[j], :] for j in js]
    v = [v_ref[rows[j], :] for j in js]
    at = [-kk_ref[rows[j], :] * jnp.exp(cum[j] - ld[j]) for j in js]
    rt = [r_ref[rows[j], :] * jnp.exp(cum[j]) for j in js]
    e_inv = [jnp.exp(-cum[j]) for j in js]
    aa = [_dot_nt(_bf(jnp.concatenate([at[j], rt[j]], axis=0)),
                  jnp.concatenate([bd(kka[j] * e_inv[j]), bd(kd[j] * e_inv[j])], axis=0)) for j in js]
    a_ab = [aa[j][0:t, 0:n] * strict for j in js]
    x = [eye + a_ab[j] for j in js]
    pw = a_ab
    for _ in range(RW_CHUNK.bit_length() - 3):
        pw = [_dot(_bf(pw[j]), bd(pw[j])) for j in js]
        x = [x[j] + _dot(_bf(x[j]), bd(pw[j])) for j in js]
    xs = [_split2(x[j]) for j in js]
    sa = [_split2(a_ab[j]) for j in js]
    ax = [_dot(sa[j][0], bd(xs[j][0])) + _dot(sa[j][1], bd(xs[j][0])) + _dot(sa[j][0], bd(xs[j][1])) for j in js]
    x = [x[j] + _dot(xs[j][0], bd(eye - x[j] + ax[j])) for j in js]
    v_bd = [bd(v[j]) for j in js]
    wv = [_dot(_bf(aa[j][0:t, n:] * strict), v_bd[j]) for j in js]
    mu = [_dot(_bf(x[j]), jnp.concatenate([bd(at[j]), bd(wv[j])], axis=1)) for j in js]
    m1 = [mu[j][:, 0:n] for j in js]
    u0 = [mu[j][:, n:] for j in js]
    e_end = [jnp.exp(cend[j] - cum[j]) for j in js]
    bend = [_bf(kka[j] * e_end[j]) for j in js]
    g = [_bf(_dot_tn(_bf(m1[j]), bend[j]) * same) for j in js]
    cst = []
    for j in js:
        full = _dot_tn(_bf(jnp.concatenate([u0[j], v[j]], axis=0)),
                       jnp.concatenate([bend[j], _bf(kd[j] * e_end[j])], axis=0)) * same
        cst.append(functools.reduce(jnp.add, [full[h * t:(h + 1) * t] for h in range(RW_HEADS)]))
    qo = [_dot(_bf(aa[j][t:, 0:n] * incl), jnp.concatenate([bd(m1[j]), bd(u0[j])], axis=1)) for j in js]
    q = [_bf(rt[j] + qo[j][:, 0:n]) for j in js]
    o0 = [qo[j][:, n:] + _dot(_bf(aa[j][t:, n:] * incl), v_bd[j]) for j in js]

    s = s_scr[...]
    for j in js:
        o_ref[0, rows[j], :] = _dot_nt(q[j], bd(s)) + o0[j]
        s = s * jnp.exp(cend[j]) + _dot(_bf(s), g[j]) + cst[j]
    s_scr[...] = s

    @pl.when(c == pl.num_programs(2) - 1)
    def _():
        sfin_ref[0, 0] = s


def _rw_scan(r, kk, v, ld, kka, kd, s0_bd, consts, seq):
    tokens = r.shape[0]
    nreq = tokens // seq
    tb = min(seq, 1024)
    nblk = seq // tb
    nsub = tb // RW_CHUNK
    n = RW_HEADS * RW_CHUNK
    cc = lambda d, c: c + d * (nblk - 1 - 2 * c)
    shared = pl.BlockSpec((tb, RW_W), lambda b, d, c: (b * nblk + cc(d, c), 0))
    dirw = pl.BlockSpec((tb, RW_W), lambda b, d, c: (b * nblk + cc(d, c), d))
    strict, incl, tri, same, eye = consts
    return pl.pallas_call(
        functools.partial(_rw_scan_kernel, nsub=nsub),
        grid=(nreq, 2, nblk),
        in_specs=[shared, shared, shared, dirw, dirw, dirw,
                  pl.BlockSpec((1, 1, HEAD_DIM, n), lambda b, d, c: (b, d, 0, 0)),
                  pl.BlockSpec((1, RW_CHUNK, n), lambda b, d, c: (d, 0, 0)),
                  pl.BlockSpec((1, RW_CHUNK, n), lambda b, d, c: (d, 0, 0)),
                  pl.BlockSpec((1, RW_CHUNK, RW_CHUNK), lambda b, d, c: (d, 0, 0)),
                  _full((n, n)), _full((RW_CHUNK, n))],
        out_specs=[pl.BlockSpec((1, tb, RW_W), lambda b, d, c: (d, b * nblk + cc(d, c), 0)),
                   pl.BlockSpec((1, 1, HEAD_DIM, n), lambda b, d, c: (b, d, 0, 0))],
        out_shape=[jax.ShapeDtypeStruct((2, tokens, RW_W), F32),
                   jax.ShapeDtypeStruct((nreq, 2, HEAD_DIM, n), F32)],
        scratch_shapes=[pltpu.VMEM((HEAD_DIM, n), F32)],
        compiler_params=_cp("parallel", "parallel", "arbitrary"),
        name="rw_scan",
    )(r, kk, v, ld, kka, kd, s0_bd, strict, incl, tri, same, eye)


def _finish_kernel(x_ref, oa_ref, oc_ref, o2_ref, bonus_ref, g_ref, mod_ref, wout_ref, lng_ref, lnb_ref,
                   n2_ref, wr_hi_ref, wr_lo_ref, ones_ref, x1_ref, h2_ref, aff_ref):
    ones = ones_ref[0:128, 0:128]
    g1 = mod_ref[0, 2:3, :]
    sh2 = mod_ref[0, 3:4, :]
    sc2 = mod_ref[0, 4:5, :]
    subs = [slice(i, i + SUB_ROWS) for i in range(0, x_ref.shape[0], SUB_ROWS)]
    mixins = []
    for sl in subs:
        y = o2_ref[0, sl, :] + o2_ref[1, sl, :]
        mu = _head_sums(y, ones) * (1.0 / HEAD_DIM)
        yc = y - mu
        var = _head_sums(yc * yc, ones) * (1.0 / HEAD_DIM)
        yn = yc * lax.rsqrt(var + GN_EPS) * lng_ref[...] + lnb_ref[...]
        ob = (yn + bonus_ref[sl, :]) * g_ref[sl, :]
        mixins.append(jnp.concatenate([_bf(oa_ref[sl, :]), _bf(ob), _bf(oc_ref[sl, :])], axis=1))
    mixes = [_dot(mixin, wout_ref[...]) for mixin in mixins]
    for sl, mix in zip(subs, mixes):
        x1 = x_ref[sl, :] + g1 * mix
        ms = jnp.mean(x1 * x1, axis=-1, keepdims=True)
        h2 = x1 * lax.rsqrt(ms + NORM_EPS) * n2_ref[...] * (1.0 + sc2) + sh2
        x1_ref[sl, :] = x1
        h2_ref[sl, :] = _bf(h2)
        hhi, hlo = _split2(h2)
        logits = _dot(hhi, wr_hi_ref[...]) + _dot(hlo, wr_hi_ref[...]) + _dot(hhi, wr_lo_ref[...])
        m = jnp.max(logits, axis=-1, keepdims=True)
        e = jnp.exp(logits - m)
        aff_ref[sl, :] = e / jnp.sum(e, axis=-1, keepdims=True)


def _finish(x, oa, oc, o2, bonus, g, mod, p, seq, per_request_mod):
    tokens, d = x.shape
    tm = _row_tile(seq)
    tpr = seq // tm
    mod_map = (lambda i: (i // tpr, 0, 0)) if per_request_mod else (lambda i: (0, 0, 0))
    row = lambda w: pl.BlockSpec((tm, w), lambda i: (i, 0))
    return pl.pallas_call(
        _finish_kernel,
        grid=(tokens // tm,),
        in_specs=[row(d), row(NA_W), row(SWA_W), pl.BlockSpec((2, tm, RW_W), lambda i: (0, i, 0)), row(RW_W),
                  row(RW_W), pl.BlockSpec((1, 6, d), mod_map), _full((d, d)), _full((1, RW_W)),
                  _full((1, RW_W)), _full((1, d)), _full((d, N_EXPERTS)), _full((d, N_EXPERTS)),
                  _full((RW_W, RW_W))],
        out_specs=[row(d), row(d), row(N_EXPERTS)],
        out_shape=[jax.ShapeDtypeStruct((tokens, d), F32), jax.ShapeDtypeStruct((tokens, d), BF16),
                   jax.ShapeDtypeStruct((tokens, N_EXPERTS), F32)],
        compiler_params=_cp("parallel"),
        name="finish",
    )(x, oa, oc, o2, bonus, g, mod, p['w_out'], p['ln_g'], p['ln_b'], p['n2'], p['wr_hi'], p['wr_lo'],
      p['ones256'])


def _topk_kernel(aff_ref, tri_ref, eye_ref, place_ref, slot_ref, slotrow_ref, gfull_ref, ends_ref, *, cap, group,
                 seq, tb):
    aff = aff_ref[...]
    bits = lax.bitcast_convert_type(aff, jnp.int32)
    capf = jnp.float32(cap)
    eye = eye_ref[...]
    ghi, gmid, glo = _split3(aff)
    aff_t = _dot_nt(eye, ghi) + _dot_nt(eye, gmid) + _dot_nt(eye, glo)
    bits_t = lax.bitcast_convert_type(aff_t, jnp.int32)
    rs = range(group)

    def bis(_, carry):
        los, his = carry
        nlo, nhi = [], []
        for r in rs:
            mid = los[r] + ((his[r] - los[r] + 1) >> 1)
            cnt = jnp.sum(jnp.where(bits_t[:, r * seq:(r + 1) * seq] >= mid, 1.0, 0.0), axis=1, keepdims=True)
            ge = cnt >= capf
            nlo.append(jnp.where(ge, mid, los[r]))
            nhi.append(jnp.where(ge, his[r], mid - 1))
        return tuple(nlo), tuple(nhi)

    lo0 = tuple(jnp.zeros((N_EXPERTS, 1), jnp.int32) for _ in rs)
    hi0 = tuple(jnp.full((N_EXPERTS, 1), 0x7F7FFFFF, jnp.int32) for _ in rs)
    thr_cols, _ = lax.fori_loop(0, 31, bis, (lo0, hi0))
    ri = lax.broadcasted_iota(jnp.int32, (N_EXPERTS, N_EXPERTS), 0)
    ci = lax.broadcasted_iota(jnp.int32, (N_EXPERTS, N_EXPERTS), 1)
    tri = tri_ref[...]
    for r in rs:
        thr = jnp.sum(jnp.where(ri == ci, thr_cols[r], 0), axis=0, keepdims=True)
        rows = slice(r * seq, (r + 1) * seq)
        gt = jnp.where(bits[rows] > thr, 1.0, 0.0)
        eq = jnp.where(bits[rows] == thr, 1.0, 0.0)
        need = capf - jnp.sum(gt, axis=0, keepdims=True)
        offset = float(r * cap)
        carry_g = jnp.zeros((1, N_EXPERTS), F32)
        carry_e = jnp.zeros((1, N_EXPERTS), F32)
        for blk in range(seq // tb):
            sl = slice(blk * tb, (blk + 1) * tb)
            out = slice(r * seq + blk * tb, r * seq + (blk + 1) * tb)
            pg = _dot(tri, _bf(gt[sl])) + carry_g
            pe = _dot(tri, _bf(eq[sl])) + carry_e
            carry_g = pg[tb - 1:tb, :]
            carry_e = pe[tb - 1:tb, :]
            sel = gt[sl] + eq[sl] * jnp.where(pe <= need, 1.0, 0.0)
            slot = jnp.where(sel > 0.5, pg + jnp.minimum(pe, need) - 1.0 + offset, -1.0)
            slot_ref[out, :] = slot
            ends_ref[r, blk:blk + 1, :] = carry_g + jnp.minimum(carry_e, need) + offset
            shi, slo = _split2(slot)
            slotrow_ref[0, :, 0, out] = _dot_nt(eye, shi) + _dot_nt(eye, slo)
            gfull_ref[out, :] = _bf(_dot(ghi[out], place_ref[0]) + _dot(gmid[out], place_ref[1])
                                    + _dot(glo[out], place_ref[2]))


def _topk(aff, seq, group):
    tokens = aff.shape[0]
    nreq = tokens // seq
    cap = EC_CAPACITY * seq // N_EXPERTS
    tb = min(seq, 512)
    ti = np.arange(tb)
    tri = jnp.asarray((ti[None, :] <= ti[:, None]).astype(np.float32), dtype=BF16)
    eye = jnp.asarray(np.eye(N_EXPERTS, dtype=np.float32), dtype=BF16)
    place = np.zeros((3, N_EXPERTS, 128), np.float32)
    for s in range(3):
        place[s, np.arange(N_EXPERTS), s * N_EXPERTS + np.arange(N_EXPERTS)] = 1.0
    place = jnp.asarray(place, dtype=BF16)
    nblk = seq // tb
    slot, slotrow, gfull, ends = pl.pallas_call(
        functools.partial(_topk_kernel, cap=cap, group=group, seq=seq, tb=tb),
        grid=(nreq // group,),
        in_specs=[pl.BlockSpec((group * seq, N_EXPERTS), lambda b: (b, 0)), _full((tb, tb)),
                  _full((N_EXPERTS, N_EXPERTS)), _full((3, N_EXPERTS, 128))],
        out_specs=[pl.BlockSpec((group * seq, N_EXPERTS), lambda b: (b, 0)),
                   pl.BlockSpec((1, N_EXPERTS, 1, group * seq), lambda b: (b, 0, 0, 0)),
                   pl.BlockSpec((group * seq, 128), lambda b: (b, 0)),
                   pl.BlockSpec((group, nblk, N_EXPERTS), lambda b: (b, 0, 0))],
        out_shape=[jax.ShapeDtypeStruct((tokens, N_EXPERTS), F32),
                   jax.ShapeDtypeStruct((nreq // group, N_EXPERTS, 1, group * seq), F32),
                   jax.ShapeDtypeStruct((tokens, 128), BF16),
                   jax.ShapeDtypeStruct((nreq, nblk, N_EXPERTS), F32)],
        compiler_params=_cp("parallel"),
        name="topk",
    )(aff, tri, eye, place)
    ends = ends.reshape(nreq // group, group * nblk, N_EXPERTS).transpose(0, 2, 1)
    return slot, slotrow, gfull, ends.astype(jnp.int32).reshape(-1), tb


MOE_EB = 16


def _moe_dispatch_kernel(ends_ref, h_ref, slotrow_ref, gfull_ref, xe_ref, gs_ref, *, ct, nch):
    gi = pl.program_id(0)
    eb = pl.program_id(1)
    c = pl.program_id(2)
    mt = 128

    @pl.when(c == 0)
    def _():
        xe_ref[...] = jnp.zeros_like(xe_ref)
        gs_ref[...] = jnp.zeros_like(gs_ref)

    jcol = lax.broadcasted_iota(jnp.int32, (mt, 1), 0)
    starts, his, pieces = [], [], []
    for i in range(MOE_EB):
        base = (gi * N_EXPERTS + eb * MOE_EB + i) * nch
        lo = jnp.where(c == 0, 0, ends_ref[base + jnp.maximum(c - 1, 0)])
        his.append(ends_ref[base + c])
        start = pl.multiple_of(jnp.minimum((lo // 16) * 16, ct - mt), 16)
        starts.append(start)
        pieces.append(_bf(jnp.where(slotrow_ref[0, i] == (jcol + start).astype(F32), 1.0, 0.0)))
    onehot = jnp.concatenate(pieces, axis=0)
    xw = _bf(_dot(onehot, h_ref[...]))
    gw = _bf(_dot(onehot, gfull_ref[...]))
    for i in range(MOE_EB):
        rows = pl.ds(starts[i], mt)
        xe_ref[0, i, rows, :] += xw[i * mt:(i + 1) * mt]
        gs_ref[0, i, rows, :] += gw[i * mt:(i + 1) * mt]
    for i in range(MOE_EB):
        for w in range(1, ct // mt):
            wlo = starts[i] + w * mt

            @pl.when(wlo < his[i])
            def _(i=i, wlo=wlo):
                ws = pl.multiple_of(jnp.minimum(wlo, ct - mt), 16)
                slot = slotrow_ref[0, i]
                hit = (jnp.where(slot == (jcol + ws).astype(F32), 1.0, 0.0)
                       * jnp.where(slot >= wlo.astype(F32), 1.0, 0.0))
                rows = pl.ds(ws, mt)
                xe_ref[0, i, rows, :] += _bf(_dot(_bf(hit), h_ref[...]))
                gs_ref[0, i, rows, :] += _bf(_dot(_bf(hit), gfull_ref[...]))


def _moe_dispatch(ends, h2, slotrow, gfull, lg, ct, kc):
    tokens, d = h2.shape
    ngrp = tokens // lg
    nch = lg // kc
    grid_spec = pltpu.PrefetchScalarGridSpec(
        num_scalar_prefetch=1,
        grid=(ngrp, N_EXPERTS // MOE_EB, nch),
        in_specs=[pl.BlockSpec((kc, d), lambda gi, eb, c, ends: (gi * nch + c, 0)),
                  pl.BlockSpec((1, MOE_EB, 1, kc), lambda gi, eb, c, ends: (gi, eb, 0, c)),
                  pl.BlockSpec((kc, 128), lambda gi, eb, c, ends: (gi * nch + c, 0))],
        out_specs=[pl.BlockSpec((1, MOE_EB, ct, d), lambda gi, eb, c, ends: (gi, eb, 0, 0)),
                   pl.BlockSpec((1, MOE_EB, ct, 128), lambda gi, eb, c, ends: (gi, eb, 0, 0))])
    return pl.pallas_call(
        functools.partial(_moe_dispatch_kernel, ct=ct, nch=nch),
        grid_spec=grid_spec,
        out_shape=[jax.ShapeDtypeStruct((ngrp, N_EXPERTS, ct, d), BF16),
                   jax.ShapeDtypeStruct((ngrp, N_EXPERTS, ct, 128), BF16)],
        compiler_params=_cp("parallel", "parallel", "arbitrary"),
        name="moe_dispatch",
    )(ends, h2, slotrow, gfull)


def _moe_ffn_kernel(xe_ref, gs_ref, mod_ref, wg_ref, wu_ref, wd_ref, ye_ref):
    e = pl.program_id(1)
    lane = lax.broadcasted_iota(jnp.int32, (1, 128), 1)
    pick = (lane == e) | (lane == e + N_EXPERTS) | (lane == e + 2 * N_EXPERTS)
    gate = jnp.sum(jnp.where(pick, gs_ref[0, 0].astype(F32), 0.0), axis=-1, keepdims=True)
    xb = xe_ref[0, 0]
    hg = _dot(xb, wg_ref[0, 0])
    hu = _dot(xb, wu_ref[0, 0])
    he = _bf(hg * _sigmoid(hg) * hu)
    y = _dot(he, wd_ref[0, 0])
    ye_ref[0, 0] = _bf(y * gate * mod_ref[0, 5:6, :])


def _moe_ffn(xe, gs, mod, wg, wu, wd, layer, per_group_mod):
    ngrp, _, ct, d = xe.shape
    f = wg.shape[3]
    mod_map = (lambda gi, e: (gi, 0, 0)) if per_group_mod else (lambda gi, e: (0, 0, 0))
    return pl.pallas_call(
        _moe_ffn_kernel,
        grid=(ngrp, N_EXPERTS),
        in_specs=[pl.BlockSpec((1, 1, ct, d), lambda gi, e: (gi, e, 0, 0)),
                  pl.BlockSpec((1, 1, ct, 128), lambda gi, e: (gi, e, 0, 0)),
                  pl.BlockSpec((1, 6, d), mod_map),
                  pl.BlockSpec((1, 1, d, f), lambda gi, e: (layer, e, 0, 0)),
                  pl.BlockSpec((1, 1, d, f), lambda gi, e: (layer, e, 0, 0)),
                  pl.BlockSpec((1, 1, f, d), lambda gi, e: (layer, e, 0, 0))],
        out_specs=pl.BlockSpec((1, 1, ct, d), lambda gi, e: (gi, e, 0, 0)),
        out_shape=jax.ShapeDtypeStruct((ngrp, N_EXPERTS, ct, d), BF16),
        compiler_params=_cp("parallel", "parallel"),
        name="moe_ffn",
    )(xe, gs, mod, wg, wu, wd)


def _moe_combine_kernel(ends_ref, x1_ref, slot_ref, ye_ref, o_ref, win_scr, *, ct, nch):
    gi = pl.program_id(0)
    j = pl.program_id(1)
    mt = 128
    wide = N_EXPERTS * mt
    shi, slo = _split2(slot_ref[...])
    col_e = lax.broadcasted_iota(jnp.int32, (N_EXPERTS, wide), 1) // mt
    row_e = lax.broadcasted_iota(jnp.int32, (N_EXPERTS, wide), 0)
    expand = _bf(jnp.where(col_e == row_e, 1.0, 0.0))
    sb = _dot(shi, expand) + _dot(slo, expand)
    lane = lax.broadcasted_iota(jnp.int32, (1, mt), 1)
    starts, his, targets = [], [], []
    for e in range(N_EXPERTS):
        base = (gi * N_EXPERTS + e) * nch
        lo = jnp.where(j == 0, 0, ends_ref[base + jnp.maximum(j - 1, 0)])
        his.append(ends_ref[base + j])
        start = pl.multiple_of(jnp.minimum((lo // 16) * 16, ct - mt), 16)
        win_scr[e * mt:(e + 1) * mt, :] = ye_ref[0, e, pl.ds(start, mt), :]
        starts.append(start)
        targets.append((lane + start).astype(F32))
    onehot = _bf(jnp.where(sb == jnp.concatenate(targets, axis=1), 1.0, 0.0))
    o_ref[...] = x1_ref[...] + _dot(onehot, win_scr[...])
    for e in range(N_EXPERTS):
        for w in range(1, ct // mt):
            wlo = starts[e] + w * mt

            @pl.when(wlo < his[e])
            def _(e=e, wlo=wlo):
                ws = pl.multiple_of(jnp.minimum(wlo, ct - mt), 16)
                sbe = sb[:, e * mt:(e + 1) * mt]
                hit = jnp.where(sbe == (lane + ws).astype(F32), 1.0, 0.0) * jnp.where(sbe >= wlo.astype(F32), 1.0, 0.0)
                o_ref[...] += _dot(_bf(hit), ye_ref[0, e, pl.ds(ws, mt), :])


def _moe_combine(ends, x1, slot, ye, lg, ct, kc):
    tokens, d = x1.shape
    ngrp = tokens // lg
    nch = lg // kc
    grid_spec = pltpu.PrefetchScalarGridSpec(
        num_scalar_prefetch=1,
        grid=(ngrp, nch),
        in_specs=[pl.BlockSpec((kc, d), lambda gi, j, ends: (gi * nch + j, 0)),
                  pl.BlockSpec((kc, N_EXPERTS), lambda gi, j, ends: (gi * nch + j, 0)),
                  pl.BlockSpec((1, N_EXPERTS, ct, d), lambda gi, j, ends: (gi, 0, 0, 0),
                               pipeline_mode=pl.Buffered(1))],
        out_specs=pl.BlockSpec((kc, d), lambda gi, j, ends: (gi * nch + j, 0)),
        scratch_shapes=[pltpu.VMEM((N_EXPERTS * 128, d), BF16)])
    return pl.pallas_call(
        functools.partial(_moe_combine_kernel, ct=ct, nch=nch),
        grid_spec=grid_spec,
        out_shape=jax.ShapeDtypeStruct((tokens, d), F32),
        compiler_params=_cp("parallel", "arbitrary"),
        name="moe_combine",
    )(ends, x1, slot, ye)


def _rope_tables(seq):
    t = np.arange(seq)
    n_freq = HEAD_DIM // 4
    inv = ROPE_THETA ** (-np.arange(n_freq, dtype=np.float32) / n_freq)
    ang = np.concatenate([(t // GRID_W).astype(np.float32)[:, None] * inv,
                          (t % GRID_W).astype(np.float32)[:, None] * inv], axis=-1)
    ang = jnp.asarray(ang, dtype=F32)
    cos, sin = jnp.cos(ang), jnp.sin(ang)
    cos_t = jnp.tile(jnp.concatenate([cos, cos], axis=-1), (1, SWA_HEADS))
    sin_t = jnp.tile(jnp.concatenate([-sin, sin], axis=-1), (1, SWA_HEADS))
    return cos_t, sin_t


def _blockdiag2(w):
    z = jnp.zeros_like(w[0])
    return jnp.concatenate([jnp.concatenate([w[0], z], axis=1), jnp.concatenate([z, w[1]], axis=1)], axis=0)


def _layer_params(l, ada_w, ada_b, norm1_g, norm2_g, w_in, na_q_norm, na_k_norm, na_rpb, rw_mu, rw_w0, rw_w2,
                  rw_a0, rw_a2, rw_g2, rw_k_k, rw_k_a, rw_r_k, rw_ln_g, rw_ln_b, swa_q_norm, swa_k_norm,
                  swa_sink, w_out, w_router, w_gate, w_up, w_down):
    wr = w_router[l]
    wr_hi = wr.astype(BF16)
    w_rw = w_in[l][:, 3 * NA_W:3 * NA_W + RW_IN_W]
    return {
        'n1': norm1_g[l][None], 'n2': norm2_g[l][None], 'w_in': w_in[l].astype(BF16),
        'w_rw_lo': (w_rw - w_rw.astype(BF16).astype(F32)).astype(BF16),
        'gains': (jnp.tile(na_q_norm[l], NA_HEADS)[None], jnp.tile(na_k_norm[l], NA_HEADS)[None],
                  jnp.tile(swa_q_norm[l], SWA_HEADS)[None], jnp.tile(swa_k_norm[l], SWA_KV_HEADS)[None]),
        'bias': _na_bias_table(na_rpb[l]),
        'mu': rw_mu[l], 'w0': rw_w0[l].reshape(1, 2 * RW_W), 'w2': _blockdiag2(rw_w2[l]).astype(BF16),
        'a0': rw_a0[l].reshape(1, 2 * RW_W), 'a2': _blockdiag2(rw_a2[l]).astype(BF16),
        'g2': rw_g2[l].astype(BF16), 'k_k': rw_k_k[l][None], 'k_a': rw_k_a[l][None],
        'r_k': rw_r_k[l].reshape(1, RW_W), 'ln_g': rw_ln_g[l][None], 'ln_b': rw_ln_b[l][None],
        'sink': swa_sink[l], 'w_out': w_out[l].astype(BF16),
        'wr_hi': wr_hi, 'wr_lo': (wr - wr_hi.astype(F32)).astype(BF16),
        'wg': w_gate, 'wu': w_up, 'wd': w_down, 'layer': l,
        'ones256': _block_ones(RW_W),
    }


def _mix_and_ffn(x, mod, p, oa, oc, urw, s0_bd, scan_consts, seq, per_request_mod, group):
    r, kk, v, ld, kka, kd, g, bonus = _rw_pre(urw, p, seq)
    o2, sfin = _rw_scan(r, kk, v, ld, kka, kd, s0_bd, scan_consts, seq)
    x1, h2, aff = _finish(x, oa, oc, o2, bonus, g, mod, p, seq, per_request_mod)
    slot, slotrow, gfull, ends, kc = _topk(aff, seq, group)
    cap = EC_CAPACITY * seq // N_EXPERTS
    lg, ct = group * seq, group * cap
    xe, gs = _moe_dispatch(ends, h2, slotrow, gfull, lg, ct, kc)
    ye = _moe_ffn(xe, gs, mod, p['wg'], p['wu'], p['wd'], p['layer'], per_request_mod)
    return _moe_combine(ends, x1, slot, ye, lg, ct, kc), sfin


def _context_layer(x, mod, p, ones384, scan_consts, seq):
    qa, ka, va, urw, qc, kc, vc = _proj(x, mod, p['n1'], p['w_in'], p['gains'], ones384, None, seq, False, F32,
                                        w_rw_lo=p['w_rw_lo'])
    oa, oc = _ctx_attn(p['sink'], qa, ka, va, qc, kc, vc, seq)
    nreq = x.shape[0] // seq
    s0 = jnp.zeros((nreq, 2, HEAD_DIM, RW_W), F32)
    y, sfin = _mix_and_ffn(x, mod, p, oa, oc, urw, s0, scan_consts, seq, False, CTX_GROUP)
    return y, ka, va, kc, vc, sfin


def _latent_layer(x, mod, p, ones384, scan_consts, rope_tabs, seq, kx_na, vx_na, kx_swa, vx_swa, s0_bd, past):
    qa, ka, va, urw, qc, kc, vc = _proj(x, mod, p['n1'], p['w_in'], p['gains'], ones384, rope_tabs, seq, True,
                                        BF16)
    oa = _na_attn(qa, ka, va, kx_na, vx_na, p['bias'], seq, past)
    oc = _swa_attn(p['sink'], qc, kc, vc, kx_swa, vx_swa, seq, past)
    y, _ = _mix_and_ffn(x, mod, p, oa, oc, urw, s0_bd, scan_consts, seq, True, 1)
    return y


def _cache_layout(zs, nreq, seq, heads):
    z = jnp.stack(zs, axis=0).reshape(len(zs), nreq, seq, heads, HEAD_DIM)
    return z.transpose(1, 0, 3, 2, 4)


def _tokens_first(z):
    b, nl, h, n, dh = z.shape
    return z.transpose(1, 0, 3, 2, 4).reshape(nl, b * n, h * dh)


def kernel(x_prompt, x_sample, cache_na_k, cache_na_v, cache_swa_k, cache_swa_v, state_rwkv, c, c_ctx, ada_w, ada_b, norm1_g, norm2_g, w_in, na_q_norm, na_k_norm, na_rpb, rw_mu, rw_w0, rw_w2, rw_a0, rw_a2, rw_g2, rw_k_k, rw_k_a, rw_r_k, rw_ln_g, rw_ln_b, swa_q_norm, swa_k_norm, swa_sink, w_out, w_router, w_gate, w_up, w_down):
    nb, seq, d = x_prompt.shape
    db, dseq, _ = x_sample.shape
    depth = ada_w.shape[0]
    past = cache_na_k.shape[3]
    cond = jnp.concatenate([c, c_ctx[None], jnp.zeros((16 - db - 1, d), F32)], axis=0)
    mod_all = _adaln(cond, ada_w, ada_b).reshape(depth, 16, 6, d)
    ones384 = _block_ones(NA_W)
    scan_consts = _rw_masks()
    rope_tabs = _rope_tables(dseq)
    xp = x_prompt.reshape(nb * seq, d)
    xs = x_sample.reshape(db * dseq, d)
    new_ka, new_va, new_kc, new_vc, new_s = [], [], [], [], []
    wg_bf, wu_bf, wd_bf = w_gate.astype(BF16), w_up.astype(BF16), w_down.astype(BF16)
    kx_na, vx_na = _tokens_first(cache_na_k), _tokens_first(cache_na_v)
    kx_swa, vx_swa = _tokens_first(cache_swa_k), _tokens_first(cache_swa_v)
    s0_lat = jnp.transpose(state_rwkv, (1, 0, 2, 4, 3, 5)).reshape(depth, db, 2, HEAD_DIM, RW_W)
    for l in range(depth):
        p = _layer_params(l, ada_w, ada_b, norm1_g, norm2_g, w_in, na_q_norm, na_k_norm, na_rpb, rw_mu, rw_w0,
                          rw_w2, rw_a0, rw_a2, rw_g2, rw_k_k, rw_k_a, rw_r_k, rw_ln_g, rw_ln_b, swa_q_norm,
                          swa_k_norm, swa_sink, w_out, w_router, wg_bf, wu_bf, wd_bf)
        mod_ctx = mod_all[l, db:db + 1]
        mod_lat = mod_all[l, 0:db]
        xp, ka, va, kc, vc, sfin = _context_layer(xp, mod_ctx, p, ones384, scan_consts, seq)
        new_ka.append(ka)
        new_va.append(va)
        new_kc.append(kc)
        new_vc.append(vc)
        new_s.append(sfin)
        xs = _latent_layer(xs, mod_lat, p, ones384, scan_consts, rope_tabs, dseq,
                           kx_na[l], vx_na[l], kx_swa[l], vx_swa[l], s0_lat[l], past)
    states = jnp.stack(new_s, axis=0).reshape(depth, nb, 2, HEAD_DIM, RW_HEADS, HEAD_DIM)
    return (xp.reshape(nb, seq, d), xs.reshape(db, dseq, d),
            _cache_layout(new_ka, nb, seq, NA_HEADS), _cache_layout(new_va, nb, seq, NA_HEADS),
            _cache_layout(new_kc, nb, seq, SWA_KV_HEADS), _cache_layout(new_vc, nb, seq, SWA_KV_HEADS),
            jnp.transpose(states, (1, 0, 2, 4, 3, 5)))
```
